```python
import jax
import jax.numpy as jnp
from jax import lax
import numpy as np

D_MODEL = 1024
BATCH = 1
SEQ = 16384
DEPTH = 1

GRID_W = 64
CTX_LEN = 256
NORM_EPS = 1e-6

A_HEAD_DIM = 64
A_WIDTH = D_MODEL // 2
A_HEADS = A_WIDTH // A_HEAD_DIM
W_LORA = 64
ICLR_LORA = 64
G_LORA = 128
RWKV_GN_EPS = 64e-5

B_HEADS = 4
B_QK_WIDTH = D_MODEL // 2
B_V_WIDTH = D_MODEL
B_QK_DIM = B_QK_WIDTH // B_HEADS
B_V_DIM = B_V_WIDTH // B_HEADS
RET_CHUNK = 128
RET_GN_EPS = 1e-5
ROPE_BASE = 10000.0

A_SIZES = (A_WIDTH, A_WIDTH, A_WIDTH, W_LORA, ICLR_LORA, G_LORA)
B_SIZES = (B_QK_WIDTH, B_QK_WIDTH, B_V_WIDTH, B_V_WIDTH, D_MODEL, D_MODEL)
SHIFT_COLS = 3 * A_WIDTH + W_LORA + ICLR_LORA + G_LORA
IN_COLS = SHIFT_COLS + 2 * B_QK_WIDTH + 2 * B_V_WIDTH + 2 * D_MODEL

N_EXPERTS = 256
TOP_K = 8
EXPERT_FF = D_MODEL // 4
SHARED_FF = D_MODEL // 4
ROUTED_SCALE = 2.5
MOE_BLOCK = 128

kernel_name = 'hybrid_rwkv7_retention_moe_dit_block'


def _offsets(sizes):
    return np.cumsum(sizes)[:-1].tolist()


def rmsnorm(x, g):
    xf = x.astype(jnp.float32)
    y = xf * lax.rsqrt(jnp.mean(xf * xf, axis=-1, keepdims=True) + NORM_EPS)
    return (y * g.astype(jnp.float32)).astype(x.dtype)


def head_layernorm(y, w, b, eps):
    mu = jnp.mean(y, axis=-1, keepdims=True)
    var = jnp.mean(jnp.square(y - mu), axis=-1, keepdims=True)
    yn = ((y - mu) * lax.rsqrt(var + eps)).reshape(y.shape[:2] + (-1,))
    return yn * w.astype(jnp.float32) + b.astype(jnp.float32)


def bidirectional_token_shift(p, mu):
    zero = jnp.zeros_like(p[:, :1])
    prev = jnp.concatenate([zero, p[:, :-1]], axis=1)
    nxt = jnp.concatenate([p[:, 1:], zero], axis=1)
    return p + mu[0] * (prev - p) + mu[1] * (nxt - p)


def flip_t(t):
    return jnp.flip(t, axis=1)


def axial_rope(x, rope):
    cos_r, sin_r, cos_c, sin_c = rope
    half = x.shape[-1] // 2

    def rot(xh, cos, sin):
        x1, x2 = jnp.split(xh, 2, axis=-1)
        cos = cos[None, :, None, :]
        sin = sin[None, :, None, :]
        return jnp.concatenate([x1 * cos - x2 * sin, x1 * sin + x2 * cos], axis=-1)

    return jnp.concatenate([rot(x[..., :half], cos_r, sin_r), rot(x[..., half:], cos_c, sin_c)], axis=-1)


def rwkv7_scan(r, w, k, v, a, b, s0):
    def step(s, inp):
        r_t, w_t, k_t, v_t, a_t, b_t = inp
        sa = jnp.einsum('bhvk,bhk->bhv', s, a_t)
        s = s * w_t[:, :, None, :] + sa[..., None] * b_t[:, :, None, :] + v_t[..., None] * k_t[:, :, None, :]
        return s, jnp.einsum('bhvk,bhk->bhv', s, r_t)

    xs = tuple(jnp.moveaxis(t, 1, 0) for t in (r, w, k, v, a, b))
    s_fin, ys = lax.scan(step, s0, xs)
    return jnp.moveaxis(ys, 0, 1), s_fin


def retention_chunkwise(q, k, v, log_g, r0):
    bsz, t_len, n_h, dk = q.shape
    dv = v.shape[-1]
    n_c = t_len // RET_CHUNK
    qc = q.reshape(bsz, n_c, RET_CHUNK, n_h, dk)
    kc = k.reshape(bsz, n_c, RET_CHUNK, n_h, dk)
    vc = v.reshape(bsz, n_c, RET_CHUNK, n_h, dv)
    idx = jnp.arange(RET_CHUNK, dtype=jnp.float32)
    diff = idx[:, None] - idx[None, :]
    dmask = jnp.where(diff >= 0, jnp.exp(log_g[:, None, None] * jnp.maximum(diff, 0.0)), 0.0)
    scores = jnp.einsum('bcnhd,bcmhd->bchnm', qc, kc) * dmask[None, None]
    inner = jnp.einsum('bchnm,bcmhe->bcnhe', scores, vc)
    zeta = jnp.exp(log_g[:, None] * (RET_CHUNK - 1 - idx))
    kv = jnp.einsum('bcmhd,bcmhe,hm->cbhde', kc, vc, zeta)
    g_chunk = jnp.exp(log_g * RET_CHUNK)[None, :, None, None]

    def step(r_state, kv_c):
        return g_chunk * r_state + kv_c, r_state

    r_fin, r_prev = lax.scan(step, r0, kv)
    xi = jnp.exp(log_g[:, None] * (idx + 1.0))
    cross = jnp.einsum('bcnhd,cbhde,hn->bcnhe', qc, r_prev, xi)
    return (inner + cross).reshape(bsz, t_len, n_h, dv), r_fin


def token_mixer(h, lp, rope, init, need_out):
    bsz, t_len, _ = h.shape
    f32 = jnp.float32
    proj = (h @ lp['w_in']).astype(f32)
    rw = bidirectional_token_shift(proj[..., :SHIFT_COLS], lp['shift_mu'].astype(f32))
    r, k, v, xw, xa, xg = jnp.split(rw, _offsets(A_SIZES), axis=-1)
    q_b, k_b, v_b, g_b, gate_a, gate_b = jnp.split(proj[..., SHIFT_COLS:], _offsets(B_SIZES), axis=-1)

    def hd(t, n):
        return t.reshape(bsz, t_len, n, -1)

    kk = hd(k * lp['rwkv_k_k'], A_HEADS)
    kk = kk / jnp.maximum(jnp.linalg.norm(kk, axis=-1, keepdims=True), 1e-12)
    tw = jnp.tanh(xw)
    decay, iclr, kdir = [], [], []
    for d in range(2):
        w_log = -jax.nn.softplus(-(lp['rwkv_w0'][d] + tw @ lp['rwkv_w_up'][d])) - 0.5
        a_d = jax.nn.sigmoid(lp['rwkv_a0'][d] + xa @ lp['rwkv_a_up'][d])
        decay.append(hd(jnp.exp(-jnp.exp(w_log)), A_HEADS))
        iclr.append(hd(a_d, A_HEADS))
        kdir.append(hd(k * (1.0 + (a_d - 1.0) * lp['rwkv_k_a']), A_HEADS))
    r_h = hd(r, A_HEADS)
    v_h = hd(v, A_HEADS)
    if init is None:
        zs = jnp.zeros((bsz, A_HEADS, A_HEAD_DIM, A_HEAD_DIM), f32)
        zr = jnp.zeros((bsz, B_HEADS, B_QK_DIM, B_V_DIM), f32)
        init = (zs, zs, zr, zr)
    s0_f, s0_b, r0_f, r0_b = init
    o_f, s_f = rwkv7_scan(r_h, decay[0], kdir[0], v_h, -kk, kk * iclr[0], s0_f)
    o_b, s_b = rwkv7_scan(*[flip_t(t) for t in (r_h, decay[1], kdir[1], v_h, -kk, kk * iclr[1])], s0_b)
    o_b = flip_t(o_b)

    q = hd(q_b, B_HEADS)
    kr = hd(k_b, B_HEADS) * (B_QK_DIM ** -0.5)
    vr = hd(v_b, B_HEADS)
    if rope is not None:
        q = axial_rope(q, rope)
        kr = axial_rope(kr, rope)
    log_g = jax.nn.log_sigmoid(lp['ret_decay_logit'].astype(f32))
    y_f, r_f = retention_chunkwise(q, kr, vr, log_g[0], r0_f)
    y_b, r_b = retention_chunkwise(flip_t(q), flip_t(kr), flip_t(vr), log_g[1], r0_b)
    y_b = flip_t(y_b)
    states = (s_f, s_b, r_f, r_b)
    if not need_out:
        return None, states

    bonus = jnp.sum(r_h * (kdir[0] + kdir[1]) * lp['rwkv_r_k'], axis=-1, keepdims=True) * v_h
    ya = head_layernorm(o_f + o_b, lp['rwkv_ln_w'], lp['rwkv_ln_b'], RWKV_GN_EPS) + bonus.reshape(bsz, t_len, A_WIDTH)
    ya = (ya * (jax.nn.sigmoid(xg) @ lp['rwkv_g_up'])).astype(h.dtype) @ lp['w_branch_a']
    yb = head_layernorm(y_f + y_b, lp['ret_ln_w'], lp['ret_ln_b'], RET_GN_EPS) * jax.nn.silu(g_b)
    yb = yb.astype(h.dtype) @ lp['w_branch_b']
    merged = jax.nn.sigmoid(gate_a).astype(h.dtype) * ya + jax.nn.sigmoid(gate_b).astype(h.dtype) * yb
    return merged @ lp['w_out'], states


def routed_moe(h, router_w, router_bias, w_gate, w_up, w_down, s_gate, s_up, s_down):
    shape = h.shape
    h = h.reshape(-1, shape[-1])
    n_tok = h.shape[0]
    scores = jax.nn.sigmoid((h @ router_w).astype(jnp.float32))
    _, sel = lax.top_k(scores + router_bias.astype(jnp.float32), TOP_K)
    s_sel = jnp.take_along_axis(scores, sel, axis=1)
    wts = s_sel / jnp.sum(s_sel, axis=1, keepdims=True) * ROUTED_SCALE
    n_assign = n_tok * TOP_K
    flat_e = sel.reshape(n_assign)
    order = jnp.argsort(flat_e)
    se = flat_e[order]
    tok_sorted = (order // TOP_K).astype(jnp.int32)
    w_sorted = wts.reshape(n_assign)[order]
    counts = jnp.zeros((N_EXPERTS,), jnp.int32).at[flat_e].add(1)
    padded = (counts + MOE_BLOCK - 1) // MOE_BLOCK * MOE_BLOCK
    pad_end = jnp.cumsum(padded)
    starts = jnp.cumsum(counts) - counts
    dest = (pad_end - padded)[se] + jnp.arange(n_assign, dtype=jnp.int32) - starts[se]
    n_blocks = (n_assign + N_EXPERTS * (MOE_BLOCK - 1) + MOE_BLOCK - 1) // MOE_BLOCK
    n_slots = n_blocks * MOE_BLOCK
    slot_tok = jnp.zeros((n_slots,), jnp.int32).at[dest].set(tok_sorted)
    slot_w = jnp.zeros((n_slots,), jnp.float32).at[dest].set(w_sorted)
    block_start = jnp.arange(n_blocks, dtype=jnp.int32) * MOE_BLOCK
    block_e = jnp.minimum(jnp.searchsorted(pad_end, block_start, side='right'), N_EXPERTS - 1)

    def expert_block(args):
        tok, wgt, e = args
        xb = h[tok]
        y = (jax.nn.silu(xb @ w_gate[e]) * (xb @ w_up[e])) @ w_down[e]
        return y * wgt[:, None].astype(y.dtype)

    ys = lax.map(expert_block, (slot_tok.reshape(n_blocks, MOE_BLOCK), slot_w.reshape(n_blocks, MOE_BLOCK), block_e))
    routed = jax.ops.segment_sum(ys.reshape(n_slots, -1), slot_tok, num_segments=n_tok)
    shared = (jax.nn.silu(h @ s_gate) * (h @ s_up)) @ s_down
    return (routed.astype(h.dtype) + shared).reshape(shape)


def setup_inputs(seed: int = 0) -> dict:
    key = jax.random.key(seed)
    keys = jax.random.split(key, 64)
    count = [0]

    def nk():
        kk = keys[count[0]]
        count[0] += 1
        return kk

    def nrm(shape, scale):
        return jax.random.normal(nk(), shape, jnp.float32) * scale

    def gain(shape):
        return 1.0 + nrm(shape, 0.05)

    L, D = DEPTH, D_MODEL
    s = 5.0 + jnp.arange(B_HEADS, dtype=jnp.float32)
    ret_base = jnp.log(2.0 ** s - 1.0)
    return {
        'x': nrm((BATCH, SEQ, D), 1.0),
        'c': nrm((BATCH, D), 1.0),
        'ctx': nrm((BATCH, CTX_LEN, D), 1.0),
        'c_ctx': nrm((D,), 1.0),
        'w_mod': nrm((L, D, 6 * D), 0.5 * D ** -0.5),
        'b_mod': nrm((L, 6 * D), 0.02),
        'norm_pre_mix': gain((L, D)),
        'norm_post_mix': gain((L, D)),
        'norm_pre_ffn': gain((L, D)),
        'norm_post_ffn': gain((L, D)),
        'w_in': nrm((L, D, IN_COLS), D ** -0.5),
        'shift_mu': jax.random.uniform(nk(), (L, 2, SHIFT_COLS), jnp.float32, 0.0, 0.5),
        'rwkv_w0': jax.random.uniform(nk(), (L, 2, A_WIDTH), jnp.float32, -6.0, 1.0),
        'rwkv_w_up': nrm((L, 2, W_LORA, A_WIDTH), 0.5 * W_LORA ** -0.5),
        'rwkv_a0': nrm((L, 2, A_WIDTH), 0.5),
        'rwkv_a_up': nrm((L, 2, ICLR_LORA, A_WIDTH), 0.5 * ICLR_LORA ** -0.5),
        'rwkv_g_up': nrm((L, G_LORA, A_WIDTH), G_LORA ** -0.5),
        'rwkv_k_k': 0.85 + nrm((L, A_WIDTH), 0.05),
        'rwkv_k_a': gain((L, A_WIDTH)),
        'rwkv_r_k': nrm((L, A_HEADS, A_HEAD_DIM), 0.1),
        'rwkv_ln_w': gain((L, A_WIDTH)),
        'rwkv_ln_b': nrm((L, A_WIDTH), 0.02),
        'w_branch_a': nrm((L, A_WIDTH, D), A_WIDTH ** -0.5),
        'ret_decay_logit': ret_base + nrm((L, 2, B_HEADS), 0.1),
        'ret_ln_w': gain((L, B_V_WIDTH)),
        'ret_ln_b': nrm((L, B_V_WIDTH), 0.02),
        'w_branch_b': nrm((L, B_V_WIDTH, D), B_V_WIDTH ** -0.5),
        'w_out': nrm((L, D, D), D ** -0.5),
        'router_w': nrm((L, D, N_EXPERTS), D ** -0.5),
        'router_bias': nrm((L, N_EXPERTS), 0.01),
        'exp_w_gate': nrm((L, N_EXPERTS, D, EXPERT_FF), D ** -0.5),
        'exp_w_up': nrm((L, N_EXPERTS, D, EXPERT_FF), D ** -0.5),
        'exp_w_down': nrm((L, N_EXPERTS, EXPERT_FF, D), EXPERT_FF ** -0.5),
        'sh_w_gate': nrm((L, D, SHARED_FF), D ** -0.5),
        'sh_w_up': nrm((L, D, SHARED_FF), D ** -0.5),
        'sh_w_down': nrm((L, SHARED_FF, D), SHARED_FF ** -0.5),
    }


def reference(x, c, ctx, c_ctx, w_mod, b_mod, norm_pre_mix, norm_post_mix, norm_pre_ffn, norm_post_ffn,
              w_in, shift_mu, rwkv_w0, rwkv_w_up, rwkv_a0, rwkv_a_up, rwkv_g_up, rwkv_k_k, rwkv_k_a,
              rwkv_r_k, rwkv_ln_w, rwkv_ln_b, w_branch_a, ret_decay_logit, ret_ln_w, ret_ln_b, w_branch_b,
              w_out, router_w, router_bias, exp_w_gate, exp_w_up, exp_w_down, sh_w_gate, sh_w_up, sh_w_down):
    t_len = x.shape[1]
    ROWS = t_len // GRID_W
    rows = jnp.repeat(jnp.arange(ROWS), GRID_W).astype(jnp.float32)
    cols = jnp.tile(jnp.arange(GRID_W), ROWS).astype(jnp.float32)
    quarter = B_QK_DIM // 4
    inv_freq = ROPE_BASE ** (-jnp.arange(quarter, dtype=jnp.float32) / quarter)
    ang_r = rows[:, None] * inv_freq
    ang_c = cols[:, None] * inv_freq
    rope = (jnp.cos(ang_r), jnp.sin(ang_r), jnp.cos(ang_c), jnp.sin(ang_c))

    for i in range(DEPTH):
        last = i == DEPTH - 1
        lp = {
            'w_in': w_in[i], 'shift_mu': shift_mu[i],
            'rwkv_w0': rwkv_w0[i], 'rwkv_w_up': rwkv_w_up[i], 'rwkv_a0': rwkv_a0[i], 'rwkv_a_up': rwkv_a_up[i],
            'rwkv_g_up': rwkv_g_up[i], 'rwkv_k_k': rwkv_k_k[i], 'rwkv_k_a': rwkv_k_a[i], 'rwkv_r_k': rwkv_r_k[i],
            'rwkv_ln_w': rwkv_ln_w[i], 'rwkv_ln_b': rwkv_ln_b[i], 'w_branch_a': w_branch_a[i],
            'ret_decay_logit': ret_decay_logit[i], 'ret_ln_w': ret_ln_w[i], 'ret_ln_b': ret_ln_b[i],
            'w_branch_b': w_branch_b[i], 'w_out': w_out[i],
        }
        moe_p = (router_w[i], router_bias[i], exp_w_gate[i], exp_w_up[i], exp_w_down[i],
                 sh_w_gate[i], sh_w_up[i], sh_w_down[i])
        mod = jax.nn.silu(c) @ w_mod[i] + b_mod[i]
        mod_ctx = jax.nn.silu(c_ctx) @ w_mod[i] + b_mod[i]
        sh1, sc1, g1, sh2, sc2, g2 = [m[:, None, :] for m in jnp.split(mod, 6, axis=-1)]
        csh1, csc1, cg1, csh2, csc2, cg2 = jnp.split(mod_ctx, 6, axis=-1)

        hc = rmsnorm(ctx, norm_pre_mix[i]) * (1.0 + csc1) + csh1
        mix_c, ctx_states = token_mixer(hc, lp, None, None, not last)

        h = rmsnorm(x, norm_pre_mix[i]) * (1.0 + sc1) + sh1
        mix, _ = token_mixer(h, lp, rope, ctx_states, True)
        x = x + g1 * rmsnorm(mix, norm_post_mix[i])
        h = rmsnorm(x, norm_pre_ffn[i]) * (1.0 + sc2) + sh2
        x = x + g2 * rmsnorm(routed_moe(h, *moe_p), norm_post_ffn[i])

        if not last:
            ctx = ctx + cg1 * rmsnorm(mix_c, norm_post_mix[i])
            hc = rmsnorm(ctx, norm_pre_ffn[i]) * (1.0 + csc2) + csh2
            ctx = ctx + cg2 * rmsnorm(routed_moe(hc, *moe_p), norm_post_ffn[i])
    return x
```

```python
import functools

import jax
import jax.numpy as jnp
import numpy as np
from jax import lax
from jax.experimental import pallas as pl
from jax.experimental.pallas import tpu as pltpu

F32 = jnp.float32
BF16 = jnp.bfloat16

D_MODEL = 1024
GRID_W = 64
NORM_EPS = 1e-6

A_HEAD_DIM = 64
A_WIDTH = D_MODEL // 2
A_HEADS = A_WIDTH // A_HEAD_DIM
W_LORA = 64
ICLR_LORA = 64
G_LORA = 128
RWKV_GN_EPS = 64e-5

B_HEADS = 4
B_QK_WIDTH = D_MODEL // 2
B_V_WIDTH = D_MODEL
B_QK_DIM = B_QK_WIDTH // B_HEADS
B_V_DIM = B_V_WIDTH // B_HEADS
RET_CHUNK = 128
RET_GN_EPS = 1e-5
ROPE_BASE = 10000.0

SHIFT_COLS = 3 * A_WIDTH + W_LORA + ICLR_LORA + G_LORA
LORA_COLS = W_LORA + ICLR_LORA + G_LORA
SHIFT_PAD = 2048
PROJ_COLS = SHIFT_PAD + 2 * B_QK_WIDTH + 2 * B_V_WIDTH + 2 * D_MODEL

N_EXPERTS = 256
TOP_K = 8
EXPERT_FF = D_MODEL // 4
SHARED_FF = D_MODEL // 4
ROUTED_SCALE = 2.5
MOE_BLOCK = 128

RWKV_CHUNK = 64
PROJ_TN = 256
VMEM_LIMIT = 48 * 1024 * 1024


def _params(sem):
    return pltpu.CompilerParams(dimension_semantics=sem, vmem_limit_bytes=VMEM_LIMIT)


def _bf(a):
    return a.astype(BF16)


def _bdot(a, b):
    return jnp.dot(a, b, preferred_element_type=F32)


def _dot_nt(a, b):
    return lax.dot_general(a, b, (((1,), (1,)), ((), ())), preferred_element_type=F32)


def _dot_tn(a, b):
    return lax.dot_general(a, b, (((0,), (0,)), ((), ())), preferred_element_type=F32)


def _split(a, n):
    out = []
    rem = a
    for _ in range(n):
        p = _bf(rem)
        out.append(p)
        rem = rem - p.astype(F32)
    return out


def _dot_split_lhs(a, b_bf, n=3):
    acc = None
    for p in _split(a, n):
        t = _bdot(p, b_bf)
        acc = t if acc is None else acc + t
    return acc


def _dot_split_rhs(a_bf, b, n=3):
    acc = None
    for p in _split(b, n):
        t = _bdot(a_bf, p)
        acc = t if acc is None else acc + t
    return acc


def _dot3(a, b):
    ah, al = _split(a, 2)
    bh, bl = _split(b, 2)
    return _bdot(ah, bh) + (_bdot(ah, bl) + _bdot(al, bh))


def _sigmoid(x):
    return 1.0 / (1.0 + jnp.exp(-x))


def _softplus(x):
    return jnp.maximum(x, 0.0) + jnp.log1p(jnp.exp(-jnp.abs(x)))


def _rms(x):
    return x * lax.rsqrt(jnp.mean(x * x, axis=-1, keepdims=True) + NORM_EPS)


def _mod_kernel(cs_ref, w_ref, b_ref, o_ref):
    cs = cs_ref[...]
    s = cs * _sigmoid(cs)
    o_ref[...] = _dot3(s, w_ref[...]) + b_ref[...]


def _modulation(cs, w_mod, b_mod):
    n_out = w_mod.shape[1]
    tn = 1536
    return pl.pallas_call(
        _mod_kernel,
        out_shape=jax.ShapeDtypeStruct((8, n_out), F32),
        grid=(n_out // tn,),
        in_specs=[
            pl.BlockSpec((8, D_MODEL), lambda j: (0, 0)),
            pl.BlockSpec((D_MODEL, tn), lambda j: (0, j)),
            pl.BlockSpec((1, tn), lambda j: (0, j)),
        ],
        out_specs=pl.BlockSpec((8, tn), lambda j: (0, j)),
        compiler_params=_params(("arbitrary",)),
        name="modulation",
    )(cs, w_mod, b_mod.reshape(1, n_out))


def _in_proj_kernel(x_ref, xp_ref, xn_ref, g_ref, sc_ref, sh_ref, w_ref, mu_ref, o_ref, h_s, hh_s, *, tm, n_shift):
    i = pl.program_id(0)
    j = pl.program_id(1)
    n_i = pl.num_programs(0)

    def norm_mod(xv):
        return (_rms(xv) * g_ref[...]) * (1.0 + sc_ref[...]) + sh_ref[...]

    @pl.when(j == 0)
    def _():
        h_s[...] = _bf(norm_mod(x_ref[...]))
        hp = jnp.where(i > 0, norm_mod(xp_ref[...]), 0.0)
        hn = jnp.where(i < n_i - 1, norm_mod(xn_ref[...]), 0.0)
        hh_s[0:8, :] = hp
        hh_s[8:16, :] = hn

    w = w_ref[...]
    p = _bdot(h_s[...], w)

    @pl.when(j < n_shift)
    def _():
        ph = _bdot(_bf(hh_s[...]), w)
        row = lax.broadcasted_iota(jnp.int32, p.shape, 0)
        prev = jnp.where(row == 0, ph[7:8, :], pltpu.roll(p, 1, 0))
        nxt = jnp.where(row == tm - 1, ph[8:9, :], pltpu.roll(p, tm - 1, 0))
        mu = mu_ref[...]
        o_ref[...] = p + mu[0:1, :] * (prev - p) + mu[1:2, :] * (nxt - p)

    @pl.when(j >= n_shift)
    def _():
        o_ref[...] = p


def _in_proj(x2d, g, sc, sh, w_bf, mu_pad):
    t_len = x2d.shape[0]
    tm = min(t_len, 1024)
    tn = PROJ_TN
    n_shift = SHIFT_PAD // tn
    tb8 = tm // 8
    nb8 = t_len // 8
    kern = functools.partial(_in_proj_kernel, tm=tm, n_shift=n_shift)
    vec = lambda: pl.BlockSpec((1, D_MODEL), lambda i, j: (0, 0))
    return pl.pallas_call(
        kern,
        out_shape=jax.ShapeDtypeStruct((t_len, PROJ_COLS), F32),
        grid=(t_len // tm, PROJ_COLS // tn),
        in_specs=[
            pl.BlockSpec((tm, D_MODEL), lambda i, j: (i, 0)),
            pl.BlockSpec((8, D_MODEL), lambda i, j: (jnp.maximum(i * tb8 - 1, 0), 0)),
            pl.BlockSpec((8, D_MODEL), lambda i, j: (jnp.minimum((i + 1) * tb8, nb8 - 1), 0)),
            vec(), vec(), vec(),
            pl.BlockSpec((D_MODEL, tn), lambda i, j: (0, j)),
            pl.BlockSpec((2, tn), lambda i, j: (0, jnp.minimum(j, n_shift - 1))),
        ],
        out_specs=pl.BlockSpec((tm, tn), lambda i, j: (i, j)),
        scratch_shapes=[pltpu.VMEM((tm, D_MODEL), BF16), pltpu.VMEM((16, D_MODEL), F32)],
        compiler_params=_params(("arbitrary", "arbitrary")),
        name="in_proj",
    )(x2d, x2d, x2d, g, sc, sh, w_bf, mu_pad)


def _rwkv_prep_kernel(r_ref, k_ref, v_ref, lora_ref, kk_w, ka_w, rk_w, w0_ref, wup_ref, a0_ref, aup_ref, gup_ref,
                      bd_ref, kk_o, lw0_o, lw1_o, kd0_o, kd1_o, b0_o, b1_o, bonus_o, g_o):
    r = r_ref[...]
    k = k_ref[...]
    v = v_ref[...]
    lora = lora_ref[...]
    xw = lora[:, 0:W_LORA]
    xa = lora[:, W_LORA:W_LORA + ICLR_LORA]
    xg = lora[:, W_LORA + ICLR_LORA:]
    bd = bd_ref[...]

    kk = k * kk_w[...]
    nrm = jnp.sqrt(_dot_split_lhs(kk * kk, bd))
    kk = kk / jnp.maximum(nrm, 1e-12)
    kk_o[...] = kk

    tw = jnp.tanh(xw)
    lw_outs = (lw0_o, lw1_o)
    kd_outs = (kd0_o, kd1_o)
    b_outs = (b0_o, b1_o)
    kd_sum = None
    for d in range(2):
        z = w0_ref[d:d + 1, :] + _dot3(tw, wup_ref[d])
        w_log = -_softplus(-z) - 0.5
        lw_outs[d][...] = -jnp.exp(w_log)
        a_d = _sigmoid(a0_ref[d:d + 1, :] + _dot3(xa, aup_ref[d]))
        kd = k * (1.0 + (a_d - 1.0) * ka_w[...])
        kd_outs[d][...] = kd
        b_outs[d][...] = kk * a_d
        kd_sum = kd if kd_sum is None else kd_sum + kd
    g_o[...] = _dot3(_sigmoid(xg), gup_ref[...])
    bonus_o[...] = _dot_split_lhs(r * kd_sum * rk_w[...], bd) * v


def _rwkv_prep(proj, lp, bd):
    t_len = proj.shape[0]
    tm = min(t_len, 512)
    aw = A_WIDTH
    col = lambda c: pl.BlockSpec((tm, aw), lambda i: (i, c))
    vec = lambda: pl.BlockSpec((1, aw), lambda i: (0, 0))
    full = lambda shp: pl.BlockSpec(shp, lambda i: (0,) * len(shp))
    outs = [jax.ShapeDtypeStruct((t_len, aw), F32)] * 9
    return pl.pallas_call(
        _rwkv_prep_kernel,
        out_shape=outs,
        grid=(t_len // tm,),
        in_specs=[
            col(0), col(1), col(2),
            pl.BlockSpec((tm, LORA_COLS), lambda i: (i, 3 * aw // LORA_COLS)),
            vec(), vec(), vec(),
            full((2, aw)), full((2, W_LORA, aw)), full((2, aw)), full((2, ICLR_LORA, aw)), full((G_LORA, aw)),
            full((aw, aw)),
        ],
        out_specs=[pl.BlockSpec((tm, aw), lambda i: (i, 0))] * 9,
        compiler_params=_params(("arbitrary",)),
        name="rwkv_prep",
    )(proj, proj, proj, proj, lp["k_k"], lp["k_a"], lp["r_k"], lp["w0"], lp["w_up"], lp["a0"], lp["a_up"],
      lp["g_up"], bd)


def _rwkv_scan_kernel(rf, vf, kkf, lwf, kdf, bf_, rb, vb, kkb, lwb, kdb, bb, s0_ref, of_ref, ob_ref, sfin_ref, s_s,
                      *, tb):
    step = pl.program_id(0)

    @pl.when(step == 0)
    def _():
        s_s[...] = s0_ref[...]

    c = RWKV_CHUNK
    n = A_HEAD_DIM
    row = lax.broadcasted_iota(jnp.int32, (c, c), 0)
    col = lax.broadcasted_iota(jnp.int32, (c, c), 1)
    eye = (row == col).astype(F32)
    n_ch = tb // c
    dirs = ((rf, vf, kkf, lwf, kdf, bf_, of_ref), (rb, vb, kkb, lwb, kdb, bb, ob_ref))
    for d, (r_ref, v_ref, kk_ref, lw_ref, kd_ref, b_ref, o_ref) in enumerate(dirs):
        if d == 0:
            incl, strict = row >= col, row > col
        else:
            incl, strict = row <= col, row < col
        tri = incl.astype(BF16)
        for cc in range(n_ch):
            ci = cc if d == 0 else n_ch - 1 - cc
            rows = slice(ci * c, (ci + 1) * c)
            lw = lw_ref[rows, :]
            r = r_ref[rows, :]
            v = v_ref[rows, :]
            kk = kk_ref[rows, :]
            kd = kd_ref[rows, :]
            b = b_ref[rows, :]
            cl = _dot_split_rhs(tri, lw)
            cl_tot = cl[c - 1:c, :] if d == 0 else cl[0:1, :]
            e_neg = jnp.exp(-cl)
            e_end = jnp.exp(cl_tot - cl)
            g_tot = jnp.exp(cl_tot)
            rt = r * jnp.exp(cl)
            x_all = _bf(jnp.concatenate([-(kk * jnp.exp(cl - lw)), rt], axis=0))
            z_all = _bf(jnp.concatenate([b * e_neg, kd * e_neg], axis=0))
            bh_all = _bf(b * e_end)
            kh_all = _bf(kd * e_end)
            v_all = _bf(v)
            for h in range(A_HEADS):
                hs = slice(h * n, (h + 1) * n)
                g = _dot_nt(x_all[:, hs], z_all[:, hs])
                a_ab = jnp.where(strict, g[:c, :c], 0.0)
                a_ak = jnp.where(strict, g[:c, c:], 0.0)
                a_rb = jnp.where(incl, g[c:, :c], 0.0)
                a_rk = jnp.where(incl, g[c:, c:], 0.0)
                p = eye + a_ab
                ap = a_ab
                for _ in range(5):
                    apb = _bf(ap)
                    ap = _bdot(apb, apb)
                    p = p + _bdot(_bf(p), _bf(ap))
                vh = v_all[:, hs]
                av = _bdot(_bf(a_ak), vh)
                tw = _bf(_bdot(_bf(p), jnp.concatenate([x_all[:c, hs], _bf(av)], axis=1)))
                mn = _dot_tn(tw, bh_all[:, hs])
                nt = mn[n:, :] + _dot_tn(vh, kh_all[:, hs])
                mt = mn[:n, :]
                rw = _bdot(_bf(a_rb), tw)
                ry = rt[:, hs] + rw[:, :n]
                y0 = rw[:, n:] + _bdot(_bf(a_rk), vh)
                s0 = s_s[d, h]
                o_ref[rows, hs] = _dot_nt(_bf(ry), _bf(s0)) + y0
                s_s[d, h] = s0 * g_tot[:, hs] + _dot_split_lhs(s0, _bf(mt), 2) + nt

    @pl.when(step == pl.num_programs(0) - 1)
    def _():
        sfin_ref[...] = s_s[...]


def _rwkv_scan(proj, prep, s0):
    kk, lw0, lw1, kd0, kd1, b0, b1 = prep[:7]
    t_len = proj.shape[0]
    tb = 2 * RWKV_CHUNK
    nb = t_len // tb
    aw = A_WIDTH
    fwd = lambda c: pl.BlockSpec((tb, aw), lambda i: (i, c))
    bwd = lambda c: pl.BlockSpec((tb, aw), lambda i: (nb - 1 - i, c))
    st = pl.BlockSpec((2, A_HEADS, A_HEAD_DIM, A_HEAD_DIM), lambda i: (0, 0, 0, 0))
    kern = functools.partial(_rwkv_scan_kernel, tb=tb)
    return pl.pallas_call(
        kern,
        out_shape=[jax.ShapeDtypeStruct((t_len, aw), F32), jax.ShapeDtypeStruct((t_len, aw), F32),
                   jax.ShapeDtypeStruct((2, A_HEADS, A_HEAD_DIM, A_HEAD_DIM), F32)],
        grid=(nb,),
        in_specs=[fwd(0), fwd(2), fwd(0), fwd(0), fwd(0), fwd(0),
                  bwd(0), bwd(2), bwd(0), bwd(0), bwd(0), bwd(0), st],
        out_specs=[fwd(0), bwd(0), st],
        scratch_shapes=[pltpu.VMEM((2, A_HEADS, A_HEAD_DIM, A_HEAD_DIM), F32)],
        compiler_params=_params(("arbitrary",)),
        name="rwkv_scan",
    )(proj, proj, kk, lw0, kd0, b0, proj, proj, kk, lw1, kd1, b1, s0)


def _retention_kernel(qf, kf, vf, cosf, sinf, qb, kb, vb, cosb, sinb, lgt_ref, r0_ref, yf_ref, yb_ref, rfin_ref, r_s):
    step = pl.program_id(0)

    @pl.when(step == 0)
    def _():
        r_s[...] = r0_ref[...]

    c = RET_CHUNK
    dk = B_QK_DIM
    dv = B_V_DIM
    lg_all = -_softplus(-lgt_ref[...])
    rowf = lax.broadcasted_iota(jnp.int32, (c, c), 0).astype(F32)
    colf = lax.broadcasted_iota(jnp.int32, (c, c), 1).astype(F32)
    lane = lax.broadcasted_iota(jnp.int32, (c, dk), 1)
    first_half = (lane % 64) < 32
    dirs = ((qf, kf, vf, cosf, sinf, yf_ref), (qb, kb, vb, cosb, sinb, yb_ref))
    for d, (q_ref, k_ref, v_ref, cos_ref, sin_ref, y_ref) in enumerate(dirs):
        cos = cos_ref[...]
        sin = sin_ref[...]
        diff = (rowf - colf) if d == 0 else (colf - rowf)
        pos = rowf if d == 0 else (c - 1.0) - rowf
        for h in range(B_HEADS):
            lg = lg_all[d * B_HEADS + h:d * B_HEADS + h + 1, :]
            qh = q_ref[:, h * dk:(h + 1) * dk]
            kh = k_ref[:, h * dk:(h + 1) * dk] * (dk ** -0.5)

            def rope(xv):
                swapped = jnp.where(first_half, pltpu.roll(xv, dk - 32, 1), pltpu.roll(xv, 32, 1))
                return xv * cos + swapped * sin

            qh = rope(qh)
            kh = rope(kh)
            vh = _bf(v_ref[:, h * dv:(h + 1) * dv])
            dmask = jnp.where(diff >= 0.0, jnp.exp(lg * jnp.maximum(diff, 0.0)), 0.0)
            scores = _dot_nt(_bf(qh), _bf(kh)) * dmask
            inner = _bdot(_bf(scores), vh)
            xi = jnp.exp(lg * (pos + 1.0))
            zeta = jnp.exp(lg * ((c - 1.0) - pos))
            r_prev = r_s[d, h]
            cross = _bdot(_bf(qh * xi), _bf(r_prev))
            y_ref[:, h * dv:(h + 1) * dv] = inner + cross
            kv = _dot_tn(_bf(kh * zeta), vh)
            g_chunk = jnp.exp(lg * float(c))
            r_s[d, h] = jnp.concatenate([g_chunk, g_chunk], axis=1) * r_prev + kv

    @pl.when(step == pl.num_programs(0) - 1)
    def _():
        rfin_ref[...] = r_s[...]


def _retention(proj, cos, sin, lgt, r0):
    t_len = proj.shape[0]
    c = RET_CHUNK
    nc = t_len // c
    qw = B_QK_WIDTH
    vw = B_V_WIDTH
    fq = lambda col: pl.BlockSpec((c, qw), lambda i: (i, col))
    bq = lambda col: pl.BlockSpec((c, qw), lambda i: (nc - 1 - i, col))
    st = pl.BlockSpec((2, B_HEADS, B_QK_DIM, B_V_DIM), lambda i: (0, 0, 0, 0))
    q_col = SHIFT_PAD // qw
    v_col = (SHIFT_PAD + 2 * qw) // vw
    return pl.pallas_call(
        _retention_kernel,
        out_shape=[jax.ShapeDtypeStruct((t_len, vw), F32), jax.ShapeDtypeStruct((t_len, vw), F32),
                   jax.ShapeDtypeStruct((2, B_HEADS, B_QK_DIM, B_V_DIM), F32)],
        grid=(nc,),
        in_specs=[
            fq(q_col), fq(q_col + 1), pl.BlockSpec((c, vw), lambda i: (i, v_col)),
            pl.BlockSpec((c, B_QK_DIM), lambda i: (i, 0)), pl.BlockSpec((c, B_QK_DIM), lambda i: (i, 0)),
            bq(q_col), bq(q_col + 1), pl.BlockSpec((c, vw), lambda i: (nc - 1 - i, v_col)),
            pl.BlockSpec((c, B_QK_DIM), lambda i: (nc - 1 - i, 0)),
            pl.BlockSpec((c, B_QK_DIM), lambda i: (nc - 1 - i, 0)),
            pl.BlockSpec((2 * B_HEADS, 128), lambda i: (0, 0)), st,
        ],
        out_specs=[pl.BlockSpec((c, vw), lambda i: (i, 0)), pl.BlockSpec((c, vw), lambda i: (nc - 1 - i, 0)), st],
        scratch_shapes=[pltpu.VMEM((2, B_HEADS, B_QK_DIM, B_V_DIM), F32)],
        compiler_params=_params(("arbitrary",)),
        name="retention",
    )(proj, proj, proj, cos, sin, proj, proj, proj, cos, sin, lgt, r0)


def _mix_out_kernel(of_ref, ob_ref, bonus_ref, g_ref, yf_ref, yb_ref, gb_ref, ga_ref, gbb_ref, x_ref,
                    alnw, alnb, rlnw, rlnb, npm, npf, g1_ref, sc2_ref, sh2_ref, wa_ref, wb_ref, wo_ref, bd_ref,
                    x1_ref, h2_ref):
    bd = bd_ref[...]
    o = of_ref[...] + ob_ref[...]
    mu = _dot_split_lhs(o, bd) * (1.0 / A_HEAD_DIM)
    oc = o - mu
    var = _dot_split_lhs(oc * oc, bd) * (1.0 / A_HEAD_DIM)
    ya = oc * lax.rsqrt(var + RWKV_GN_EPS) * alnw[...] + alnb[...] + bonus_ref[...]
    ya = _bdot(_bf(ya * g_ref[...]), wa_ref[...])

    y = yf_ref[...] + yb_ref[...]
    parts = []
    for h in range(B_HEADS):
        seg = y[:, h * B_V_DIM:(h + 1) * B_V_DIM]
        m = jnp.mean(seg, axis=-1, keepdims=True)
        sc = seg - m
        vr = jnp.mean(sc * sc, axis=-1, keepdims=True)
        parts.append(sc * lax.rsqrt(vr + RET_GN_EPS))
    yn = jnp.concatenate(parts, axis=1) * rlnw[...] + rlnb[...]
    gb = gb_ref[...]
    yb = _bdot(_bf(yn * (gb * _sigmoid(gb))), wb_ref[...])

    merged = _sigmoid(ga_ref[...]) * ya + _sigmoid(gbb_ref[...]) * yb
    mix = _bdot(_bf(merged), wo_ref[...])
    x1 = x_ref[...] + g1_ref[...] * (_rms(mix) * npm[...])
    x1_ref[...] = x1
    h2_ref[...] = (_rms(x1) * npf[...]) * (1.0 + sc2_ref[...]) + sh2_ref[...]


def _mix_out(x2d, proj, o_f, o_b, bonus, g, y_f, y_b, lp, vecs, bd):
    t_len = x2d.shape[0]
    tm = min(t_len, 256)
    aw = A_WIDTH
    d = D_MODEL
    ta = lambda: pl.BlockSpec((tm, aw), lambda i: (i, 0))
    td = lambda: pl.BlockSpec((tm, d), lambda i: (i, 0))
    pc = lambda c: pl.BlockSpec((tm, d), lambda i: (i, c))
    va = lambda: pl.BlockSpec((1, aw), lambda i: (0, 0))
    vd = lambda: pl.BlockSpec((1, d), lambda i: (0, 0))
    full = lambda shp: pl.BlockSpec(shp, lambda i: (0, 0))
    gcol = (SHIFT_PAD + 2 * B_QK_WIDTH + B_V_WIDTH) // d
    return pl.pallas_call(
        _mix_out_kernel,
        out_shape=[jax.ShapeDtypeStruct((t_len, d), F32), jax.ShapeDtypeStruct((t_len, d), F32)],
        grid=(t_len // tm,),
        in_specs=[ta(), ta(), ta(), ta(), td(), td(), pc(gcol), pc(gcol + 1), pc(gcol + 2), td(),
                  va(), va(), vd(), vd(), vd(), vd(), vd(), vd(), vd(),
                  full((aw, d)), full((d, d)), full((d, d)), full((aw, aw))],
        out_specs=[td(), td()],
        compiler_params=_params(("arbitrary",)),
        name="mix_out",
    )(o_f, o_b, bonus, g, y_f, y_b, proj, proj, proj, x2d,
      lp["ln_w"], lp["ln_b"], lp["ret_ln_w"], lp["ret_ln_b"], vecs["npm"], vecs["npf"], vecs["g1"], vecs["sc2"],
      vecs["sh2"], lp["w_a"], lp["w_b"], lp["w_o"], bd)


def _router_kernel(h_ref, rw_ref, bias_ref, sel_ref, wts_ref, rank_ref, cnt_ref, cnt_s, *, tm):
    step = pl.program_id(0)

    @pl.when(step == 0)
    def _():
        cnt_s[...] = jnp.zeros_like(cnt_s)

    ne = N_EXPERTS
    scores = _sigmoid(_dot3(h_ref[...], rw_ref[...]))
    work = scores + bias_ref[...]
    lane = lax.broadcasted_iota(jnp.int32, (tm, ne), 1).astype(F32)
    idxs = []
    vals = []
    for _ in range(TOP_K):
        m = jnp.max(work, axis=-1, keepdims=True)
        idx = jnp.min(jnp.where(work == m, lane, float(ne)), axis=-1, keepdims=True)
        oh = lane == idx
        vals.append(jnp.sum(jnp.where(oh, scores, 0.0), axis=-1, keepdims=True))
        idxs.append(idx)
        work = jnp.where(oh, -jnp.inf, work)
    sel_f = jnp.concatenate(idxs, axis=1)
    s_sel = jnp.concatenate(vals, axis=1)
    wts_ref[...] = s_sel / jnp.sum(s_sel, axis=1, keepdims=True) * ROUTED_SCALE
    sel_ref[...] = sel_f.astype(jnp.int32)

    hit = work == -jnp.inf
    r_i = lax.broadcasted_iota(jnp.int32, (tm, tm), 0)
    c_i = lax.broadcasted_iota(jnp.int32, (tm, tm), 1)
    before = _bdot((r_i > c_i).astype(BF16), hit.astype(BF16)) + cnt_s[...]
    ranks = [jnp.sum(jnp.where(lane == idxs[k], before, 0.0), axis=-1, keepdims=True) for k in range(TOP_K)]
    rank_ref[...] = jnp.concatenate(ranks, axis=1).astype(jnp.int32)
    cnt = cnt_s[...] + jnp.sum(hit.astype(F32), axis=0, keepdims=True)
    cnt_s[...] = cnt
    cnt_ref[...] = cnt.astype(jnp.int32)


def _router(h2, router_w, router_bias):
    t_len = h2.shape[0]
    tm = min(t_len, 256)
    kern = functools.partial(_router_kernel, tm=tm)
    tk = lambda: pl.BlockSpec((tm, TOP_K), lambda i: (i, 0))
    return pl.pallas_call(
        kern,
        out_shape=[jax.ShapeDtypeStruct((t_len, TOP_K), jnp.int32), jax.ShapeDtypeStruct((t_len, TOP_K), F32),
                   jax.ShapeDtypeStruct((t_len, TOP_K), jnp.int32), jax.ShapeDtypeStruct((1, N_EXPERTS), jnp.int32)],
        grid=(t_len // tm,),
        in_specs=[pl.BlockSpec((tm, D_MODEL), lambda i: (i, 0)),
                  pl.BlockSpec((D_MODEL, N_EXPERTS), lambda i: (0, 0)),
                  pl.BlockSpec((1, N_EXPERTS), lambda i: (0, 0))],
        out_specs=[tk(), tk(), tk(), pl.BlockSpec((1, N_EXPERTS), lambda i: (0, 0))],
        scratch_shapes=[pltpu.VMEM((1, N_EXPERTS), F32)],
        compiler_params=_params(("arbitrary",)),
        name="router",
    )(h2, router_w, router_bias.reshape(1, N_EXPERTS))


def _dispatch_kernel(sel_ref, rank_ref, start_ref, h_ref, xs_in, xs_out, sem, *, tm):
    del xs_in

    def row_copy(t, k):
        slot = start_ref[sel_ref[t * TOP_K + k]] + rank_ref[t * TOP_K + k]
        return pltpu.make_async_copy(h_ref.at[pl.ds(t, 1)], xs_out.at[pl.ds(slot, 1)], sem)

    def issue(t, carry):
        for k in range(TOP_K):
            row_copy(t, k).start()
        return carry

    lax.fori_loop(0, tm, issue, 0)

    def drain(t, carry):
        for k in range(TOP_K):
            row_copy(t, k).wait()
        return carry

    lax.fori_loop(0, tm, drain, 0)


def _dispatch(h2, sel_flat, rank_flat, pad_start, n_slots):
    t_len = h2.shape[0]
    tm = min(t_len, 256)
    kern = functools.partial(_dispatch_kernel, tm=tm)
    smem_blk = lambda: pl.BlockSpec((tm * TOP_K,), lambda i: (i,), memory_space=pltpu.SMEM)
    xs0 = jnp.zeros((n_slots, D_MODEL), F32)
    return pl.pallas_call(
        kern,
        out_shape=jax.ShapeDtypeStruct((n_slots, D_MODEL), F32),
        grid=(t_len // tm,),
        in_specs=[smem_blk(), smem_blk(),
                  pl.BlockSpec((N_EXPERTS,), lambda i: (0,), memory_space=pltpu.SMEM),
                  pl.BlockSpec((tm, D_MODEL), lambda i: (i, 0)),
                  pl.BlockSpec(memory_space=pl.ANY)],
        out_specs=pl.BlockSpec(memory_space=pl.ANY),
        scratch_shapes=[pltpu.SemaphoreType.DMA(())],
        input_output_aliases={4: 0},
        compiler_params=_params(("arbitrary",)),
        name="dispatch",
    )(sel_flat, rank_flat, pad_start, h2, xs0)


def _expert_kernel(be_ref, nu_ref, xs_ref, wg_ref, wu_ref, wd_ref, ys_ref):
    b = pl.program_id(0)

    @pl.when(b < nu_ref[0])
    def _():
        xb = _bf(xs_ref[...])
        gate = _bdot(xb, _bf(wg_ref[0]))
        up = _bdot(xb, _bf(wu_ref[0]))
        act = gate * _sigmoid(gate) * up
        ys_ref[...] = _bdot(_bf(act), _bf(wd_ref[0]))

    @pl.when(b >= nu_ref[0])
    def _():
        ys_ref[...] = jnp.zeros_like(ys_ref)


def _experts(xs, block_e, n_used, w_gate, w_up, w_down):
    n_slots = xs.shape[0]
    n_blocks = n_slots // MOE_BLOCK
    grid_spec = pltpu.PrefetchScalarGridSpec(
        num_scalar_prefetch=2,
        grid=(n_blocks,),
        in_specs=[
            pl.BlockSpec((MOE_BLOCK, D_MODEL), lambda b, be, nu: (b, 0)),
            pl.BlockSpec((1, D_MODEL, EXPERT_FF), lambda b, be, nu: (be[b], 0, 0)),
            pl.BlockSpec((1, D_MODEL, EXPERT_FF), lambda b, be, nu: (be[b], 0, 0)),
            pl.BlockSpec((1, EXPERT_FF, D_MODEL), lambda b, be, nu: (be[b], 0, 0)),
        ],
        out_specs=pl.BlockSpec((MOE_BLOCK, D_MODEL), lambda b, be, nu: (b, 0)),
    )
    return pl.pallas_call(
        _expert_kernel,
        out_shape=jax.ShapeDtypeStruct((n_slots, D_MODEL), F32),
        grid_spec=grid_spec,
        compiler_params=_params(("arbitrary",)),
        name="experts",
    )(block_e, n_used, xs, w_gate, w_up, w_down)


def _combine_kernel(sel_ref, rank_ref, start_ref, ys_ref, wts_ref, h_ref, x1_ref, sg_ref, su_ref, sd_ref, npo, g2_ref,
                    o_ref, buf, sem, *, tm):
    def row_copy(t, k):
        slot = start_ref[sel_ref[t * TOP_K + k]] + rank_ref[t * TOP_K + k]
        return pltpu.make_async_copy(ys_ref.at[pl.ds(slot, 1)], buf.at[k, pl.ds(t, 1)], sem)

    def issue(t, carry):
        for k in range(TOP_K):
            row_copy(t, k).start()
        return carry

    lax.fori_loop(0, tm, issue, 0)

    hb = _bf(h_ref[...])
    gate = _bdot(hb, sg_ref[...])
    up = _bdot(hb, su_ref[...])
    shared = _bdot(_bf(gate * _sigmoid(gate) * up), sd_ref[...])

    def drain(t, carry):
        for k in range(TOP_K):
            row_copy(t, k).wait()
        return carry

    lax.fori_loop(0, tm, drain, 0)

    wts = wts_ref[...]
    routed = buf[0] * wts[:, 0:1]
    for k in range(1, TOP_K):
        routed = routed + buf[k] * wts[:, k:k + 1]
    o_ref[...] = x1_ref[...] + g2_ref[...] * (_rms(routed + shared) * npo[...])


def _combine(ys, sel_flat, rank_flat, pad_start, wts, h2, x1, sg, su, sd, npo, g2):
    t_len = h2.shape[0]
    tm = min(t_len, 256)
    d = D_MODEL
    kern = functools.partial(_combine_kernel, tm=tm)
    smem_blk = lambda: pl.BlockSpec((tm * TOP_K,), lambda i: (i,), memory_space=pltpu.SMEM)
    td = lambda: pl.BlockSpec((tm, d), lambda i: (i, 0))
    vd = lambda: pl.BlockSpec((1, d), lambda i: (0, 0))
    return pl.pallas_call(
        kern,
        out_shape=jax.ShapeDtypeStruct((t_len, d), F32),
        grid=(t_len // tm,),
        in_specs=[smem_blk(), smem_blk(),
                  pl.BlockSpec((N_EXPERTS,), lambda i: (0,), memory_space=pltpu.SMEM),
                  pl.BlockSpec(memory_space=pl.ANY),
                  pl.BlockSpec((tm, TOP_K), lambda i: (i, 0)),
                  td(), td(),
                  pl.BlockSpec((d, SHARED_FF), lambda i: (0, 0)), pl.BlockSpec((d, SHARED_FF), lambda i: (0, 0)),
                  pl.BlockSpec((SHARED_FF, d), lambda i: (0, 0)), vd(), vd()],
        out_specs=td(),
        scratch_shapes=[pltpu.VMEM((TOP_K, tm, d), F32), pltpu.SemaphoreType.DMA(())],
        compiler_params=_params(("arbitrary",)),
        name="combine",
    )(sel_flat, rank_flat, pad_start, ys, wts, h2, x1, sg, su, sd, npo, g2)


def _block_diag_ones(width, group):
    idx = np.arange(width) // group
    return jnp.asarray(idx[:, None] == idx[None, :], dtype=BF16)


def _rope_tables(t_len):
    pos = np.arange(t_len)
    rows = (pos // GRID_W).astype(np.float32)
    cols = (pos % GRID_W).astype(np.float32)
    quarter = B_QK_DIM // 4
    inv_freq = jnp.asarray(ROPE_BASE, F32) ** (-jnp.arange(quarter, dtype=F32) / quarter)
    ang_r = jnp.asarray(rows)[:, None] * inv_freq
    ang_c = jnp.asarray(cols)[:, None] * inv_freq
    cr, sr, cc, sc = jnp.cos(ang_r), jnp.sin(ang_r), jnp.cos(ang_c), jnp.sin(ang_c)
    return jnp.concatenate([cr, cr, cc, cc], axis=1), jnp.concatenate([-sr, sr, -sc, sc], axis=1)


def _token_mixer(x2d, g_pre, sc, sh, lp, cos, sin, states, bd):
    proj = _in_proj(x2d, g_pre, sc, sh, lp["w_in"], lp["mu"])
    prep = _rwkv_prep(proj, lp, bd)
    o_f, o_b, s_fin = _rwkv_scan(proj, prep, states[0])
    y_f, y_b, r_fin = _retention(proj, cos, sin, lp["lgt"], states[1])
    return proj, prep, (o_f, o_b), (y_f, y_b), (s_fin, r_fin)


def kernel(x, c, ctx, c_ctx, w_mod, b_mod, norm_pre_mix, norm_post_mix, norm_pre_ffn, norm_post_ffn, w_in, shift_mu,
           rwkv_w0, rwkv_w_up, rwkv_a0, rwkv_a_up, rwkv_g_up, rwkv_k_k, rwkv_k_a, rwkv_r_k, rwkv_ln_w, rwkv_ln_b,
           w_branch_a, ret_decay_logit, ret_ln_w, ret_ln_b, w_branch_b, w_out, router_w, router_bias, exp_w_gate,
           exp_w_up, exp_w_down, sh_w_gate, sh_w_up, sh_w_down):
    d = D_MODEL
    assert x.shape[0] == 1 and w_in.shape[0] == 1, "single batch element, single layer"
    t_len = x.shape[1]
    x2d = x.reshape(t_len, d)
    ctx2d = ctx.reshape(ctx.shape[1], d)
    row = lambda a: a.reshape(1, -1)

    cs = jnp.zeros((8, d), F32).at[0].set(c[0]).at[1].set(c_ctx)
    mod = _modulation(cs, w_mod[0], b_mod[0])
    sh1, sc1, g1, sh2, sc2, g2 = [mod[0:1, i * d:(i + 1) * d] for i in range(6)]
    csh1, csc1 = mod[1:2, 0:d], mod[1:2, d:2 * d]

    w_in_p = jnp.concatenate(
        [w_in[0][:, :SHIFT_COLS], jnp.zeros((d, SHIFT_PAD - SHIFT_COLS), F32), w_in[0][:, SHIFT_COLS:]], axis=1)
    lp = {
        "w_in": w_in_p.astype(BF16),
        "mu": jnp.pad(shift_mu[0], ((0, 0), (0, SHIFT_PAD - SHIFT_COLS))),
        "k_k": row(rwkv_k_k[0]), "k_a": row(rwkv_k_a[0]), "r_k": row(rwkv_r_k[0]),
        "w0": rwkv_w0[0], "w_up": rwkv_w_up[0], "a0": rwkv_a0[0], "a_up": rwkv_a_up[0], "g_up": rwkv_g_up[0],
        "ln_w": row(rwkv_ln_w[0]), "ln_b": row(rwkv_ln_b[0]),
        "ret_ln_w": row(ret_ln_w[0]), "ret_ln_b": row(ret_ln_b[0]),
        "lgt": jnp.broadcast_to(ret_decay_logit[0].reshape(2 * B_HEADS, 1), (2 * B_HEADS, 128)),
        "w_a": w_branch_a[0].astype(BF16), "w_b": w_branch_b[0].astype(BF16), "w_o": w_out[0].astype(BF16),
    }
    bd = _block_diag_ones(A_WIDTH, A_HEAD_DIM)
    g_pre = row(norm_pre_mix[0])

    t_ctx = ctx2d.shape[0]
    zero_states = (jnp.zeros((2, A_HEADS, A_HEAD_DIM, A_HEAD_DIM), F32),
                   jnp.zeros((2, B_HEADS, B_QK_DIM, B_V_DIM), F32))
    ones = jnp.ones((t_ctx, B_QK_DIM), F32)
    *_, ctx_states = _token_mixer(ctx2d, g_pre, csc1, csh1, lp, ones, jnp.zeros_like(ones), zero_states, bd)

    cos, sin = _rope_tables(t_len)
    proj, prep, (o_f, o_b), (y_f, y_b), _ = _token_mixer(x2d, g_pre, sc1, sh1, lp, cos, sin, ctx_states, bd)
    vecs = {"npm": row(norm_post_mix[0]), "npf": row(norm_pre_ffn[0]), "g1": g1, "sc2": sc2, "sh2": sh2}
    x1, h2 = _mix_out(x2d, proj, o_f, o_b, prep[7], prep[8], y_f, y_b, lp, vecs, bd)

    sel, wts, rank, counts = _router(h2, router_w[0], router_bias[0])
    counts = counts.reshape(N_EXPERTS)
    padded = (counts + MOE_BLOCK - 1) // MOE_BLOCK * MOE_BLOCK
    pad_end = jnp.cumsum(padded)
    pad_start = (pad_end - padded).astype(jnp.int32)
    n_assign = t_len * TOP_K
    n_blocks = (n_assign + N_EXPERTS * (MOE_BLOCK - 1) + MOE_BLOCK - 1) // MOE_BLOCK
    block_start = jnp.arange(n_blocks, dtype=jnp.int32) * MOE_BLOCK
    block_e = jnp.minimum(jnp.searchsorted(pad_end, block_start, side="right"), N_EXPERTS - 1).astype(jnp.int32)
    n_used = (pad_end[-1:] // MOE_BLOCK).astype(jnp.int32)
    sel_flat = sel.reshape(n_assign)
    rank_flat = rank.reshape(n_assign)

    xs = _dispatch(h2, sel_flat, rank_flat, pad_start, n_blocks * MOE_BLOCK)
    ys = _experts(xs, block_e, n_used, exp_w_gate[0], exp_w_up[0], exp_w_down[0])
    out = _combine(ys, sel_flat, rank_flat, pad_start, wts, h2, x1, sh_w_gate[0].astype(BF16),
                   sh_w_up[0].astype(BF16), sh_w_down[0].astype(BF16), row(norm_post_ffn[0]), g2)
    return out.reshape(x.shape)
```

```python
import functools

import jax
import jax.numpy as jnp
import numpy as np
from jax import lax
from jax.experimental import pallas as pl
from jax.experimental.pallas import tpu as pltpu

F32 = jnp.float32
BF16 = jnp.bfloat16

D_MODEL = 1024
GRID_W = 64
NORM_EPS = 1e-6

A_HEAD_DIM = 64
A_WIDTH = D_MODEL // 2
A_HEADS = A_WIDTH // A_HEAD_DIM
W_LORA = 64
ICLR_LORA = 64
G_LORA = 128
RWKV_GN_EPS = 64e-5

B_HEADS = 4
B_QK_WIDTH = D_MODEL // 2
B_V_WIDTH = D_MODEL
B_QK_DIM = B_QK_WIDTH // B_HEADS
B_V_DIM = B_V_WIDTH // B_HEADS
RET_CHUNK = 128
RET_GN_EPS = 1e-5
ROPE_BASE = 10000.0

SHIFT_COLS = 3 * A_WIDTH + W_LORA + ICLR_LORA + G_LORA
LORA_COLS = W_LORA + ICLR_LORA + G_LORA
SHIFT_PAD = 2048
PROJ_COLS = SHIFT_PAD + 2 * B_QK_WIDTH + 2 * B_V_WIDTH + 2 * D_MODEL

N_EXPERTS = 256
TOP_K = 8
EXPERT_FF = D_MODEL // 4
SHARED_FF = D_MODEL // 4
ROUTED_SCALE = 2.5
MOE_BLOCK = 128

RWKV_CHUNK = 64
PROJ_TN = 1024
VMEM_LIMIT = 48 * 1024 * 1024


def _params(sem):
    return pltpu.CompilerParams(dimension_semantics=sem, vmem_limit_bytes=VMEM_LIMIT)


def _bf(a):
    return a.astype(BF16)


def _bdot(a, b):
    return jnp.dot(a, b, preferred_element_type=F32)


def _dot_nt(a, b):
    return lax.dot_general(a, b, (((1,), (1,)), ((), ())), preferred_element_type=F32)


def _dot_tn(a, b):
    return lax.dot_general(a, b, (((0,), (0,)), ((), ())), preferred_element_type=F32)


def _split(a, n):
    out = []
    rem = a
    for _ in range(n):
        p = _bf(rem)
        out.append(p)
        rem = rem - p.astype(F32)
    return out


def _dot_split_lhs(a, b_bf, n=3):
    acc = None
    for p in _split(a, n):
        t = _bdot(p, b_bf)
        acc = t if acc is None else acc + t
    return acc


def _dot_split_rhs(a_bf, b, n=3):
    acc = None
    for p in _split(b, n):
        t = _bdot(a_bf, p)
        acc = t if acc is None else acc + t
    return acc


def _dot3(a, b):
    ah, al = _split(a, 2)
    bh, bl = _split(b, 2)
    return _bdot(ah, bh) + (_bdot(ah, bl) + _bdot(al, bh))


def _sigmoid(x):
    return 1.0 / (1.0 + jnp.exp(-x))


def _softplus(x):
    return jnp.maximum(x, 0.0) + jnp.log1p(jnp.exp(-jnp.abs(x)))


def _rms(x):
    return x * lax.rsqrt(jnp.mean(x * x, axis=-1, keepdims=True) + NORM_EPS)


def _mod_kernel(cs_ref, w_ref, b_ref, o_ref):
    cs = cs_ref[...]
    s = cs * _sigmoid(cs)
    o_ref[...] = _dot3(s, w_ref[...]) + b_ref[...]


def _modulation(cs, w_mod, b_mod):
    n_out = w_mod.shape[1]
    tn = 1536
    return pl.pallas_call(
        _mod_kernel,
        out_shape=jax.ShapeDtypeStruct((8, n_out), F32),
        grid=(n_out // tn,),
        in_specs=[
            pl.BlockSpec((8, D_MODEL), lambda j: (0, 0)),
            pl.BlockSpec((D_MODEL, tn), lambda j: (0, j)),
            pl.BlockSpec((1, tn), lambda j: (0, j)),
        ],
        out_specs=pl.BlockSpec((8, tn), lambda j: (0, j)),
        compiler_params=_params(("arbitrary",)),
        name="modulation",
    )(cs, w_mod, b_mod.reshape(1, n_out))


def _in_proj_kernel(x_ref, xp_ref, xn_ref, g_ref, sc_ref, sh_ref, w_ref, mu_ref, o_ref, h_s, hh_s, *, tm, n_shift):
    i = pl.program_id(0)
    j = pl.program_id(1)
    n_i = pl.num_programs(0)

    def norm_mod(xv):
        return (_rms(xv) * g_ref[...]) * (1.0 + sc_ref[...]) + sh_ref[...]

    @pl.when(j == 0)
    def _():
        h_s[...] = _bf(norm_mod(x_ref[...]))
        hp = jnp.where(i > 0, norm_mod(xp_ref[...]), 0.0)
        hn = jnp.where(i < n_i - 1, norm_mod(xn_ref[...]), 0.0)
        hh_s[0:8, :] = hp
        hh_s[8:16, :] = hn

    w = w_ref[...]
    p = _bdot(h_s[...], w)

    @pl.when(j < n_shift)
    def _():
        ph = _bdot(_bf(hh_s[...]), w)
        row = lax.broadcasted_iota(jnp.int32, p.shape, 0)
        prev = jnp.where(row == 0, ph[7:8, :], pltpu.roll(p, 1, 0))
        nxt = jnp.where(row == tm - 1, ph[8:9, :], pltpu.roll(p, tm - 1, 0))
        mu = mu_ref[...]
        o_ref[...] = p + mu[0:1, :] * (prev - p) + mu[1:2, :] * (nxt - p)

    @pl.when(j >= n_shift)
    def _():
        o_ref[...] = p


def _in_proj(x2d, g, sc, sh, w_bf, mu_pad):
    t_len = x2d.shape[0]
    tm = min(t_len, 1024)
    tn = PROJ_TN
    n_shift = SHIFT_PAD // tn
    tb8 = tm // 8
    nb8 = t_len // 8
    kern = functools.partial(_in_proj_kernel, tm=tm, n_shift=n_shift)
    vec = lambda: pl.BlockSpec((1, D_MODEL), lambda i, j: (0, 0))
    return pl.pallas_call(
        kern,
        out_shape=jax.ShapeDtypeStruct((t_len, PROJ_COLS), F32),
        grid=(t_len // tm, PROJ_COLS // tn),
        in_specs=[
            pl.BlockSpec((tm, D_MODEL), lambda i, j: (i, 0)),
            pl.BlockSpec((8, D_MODEL), lambda i, j: (jnp.maximum(i * tb8 - 1, 0), 0)),
            pl.BlockSpec((8, D_MODEL), lambda i, j: (jnp.minimum((i + 1) * tb8, nb8 - 1), 0)),
            vec(), vec(), vec(),
            pl.BlockSpec((D_MODEL, tn), lambda i, j: (0, j)),
            pl.BlockSpec((2, tn), lambda i, j: (0, jnp.minimum(j, n_shift - 1))),
        ],
        out_specs=pl.BlockSpec((tm, tn), lambda i, j: (i, j)),
        scratch_shapes=[pltpu.VMEM((tm, D_MODEL), BF16), pltpu.VMEM((16, D_MODEL), F32)],
        compiler_params=_params(("arbitrary", "arbitrary")),
        name="in_proj",
    )(x2d, x2d, x2d, g, sc, sh, w_bf, mu_pad)


def _rwkv_prep_kernel(r_ref, k_ref, v_ref, lora_ref, kk_w, ka_w, rk_w, w0_ref, wup_ref, a0_ref, aup_ref, gup_ref,
                      bd_ref, kk_o, lw0_o, lw1_o, kd0_o, kd1_o, b0_o, b1_o, bonus_o, g_o):
    r = r_ref[...]
    k = k_ref[...]
    v = v_ref[...]
    lora = lora_ref[...]
    xw = lora[:, 0:W_LORA]
    xa = lora[:, W_LORA:W_LORA + ICLR_LORA]
    xg = lora[:, W_LORA + ICLR_LORA:]
    bd = bd_ref[...]

    kk = k * kk_w[...]
    nrm = jnp.sqrt(_dot_split_lhs(kk * kk, bd))
    kk = kk / jnp.maximum(nrm, 1e-12)
    kk_o[...] = kk

    tw = jnp.tanh(xw)
    lw_outs = (lw0_o, lw1_o)
    kd_outs = (kd0_o, kd1_o)
    b_outs = (b0_o, b1_o)
    kd_sum = None
    for d in range(2):
        z = w0_ref[d:d + 1, :] + _dot3(tw, wup_ref[d])
        w_log = -_softplus(-z) - 0.5
        lw_outs[d][...] = -jnp.exp(w_log)
        a_d = _sigmoid(a0_ref[d:d + 1, :] + _dot3(xa, aup_ref[d]))
        kd = k * (1.0 + (a_d - 1.0) * ka_w[...])
        kd_outs[d][...] = kd
        b_outs[d][...] = kk * a_d
        kd_sum = kd if kd_sum is None else kd_sum + kd
    g_o[...] = _dot3(_sigmoid(xg), gup_ref[...])
    bonus_o[...] = _dot_split_lhs(r * kd_sum * rk_w[...], bd) * v


def _rwkv_prep(proj, lp, bd):
    t_len = proj.shape[0]
    tm = min(t_len, 512)
    aw = A_WIDTH
    col = lambda c: pl.BlockSpec((tm, aw), lambda i: (i, c))
    vec = lambda: pl.BlockSpec((1, aw), lambda i: (0, 0))
    full = lambda shp: pl.BlockSpec(shp, lambda i: (0,) * len(shp))
    outs = [jax.ShapeDtypeStruct((t_len, aw), F32)] * 9
    return pl.pallas_call(
        _rwkv_prep_kernel,
        out_shape=outs,
        grid=(t_len // tm,),
        in_specs=[
            col(0), col(1), col(2),
            pl.BlockSpec((tm, LORA_COLS), lambda i: (i, 3 * aw // LORA_COLS)),
            vec(), vec(), vec(),
            full((2, aw)), full((2, W_LORA, aw)), full((2, aw)), full((2, ICLR_LORA, aw)), full((G_LORA, aw)),
            full((aw, aw)),
        ],
        out_specs=[pl.BlockSpec((tm, aw), lambda i: (i, 0))] * 9,
        compiler_params=_params(("arbitrary",)),
        name="rwkv_prep",
    )(proj, proj, proj, proj, lp["k_k"], lp["k_a"], lp["r_k"], lp["w0"], lp["w_up"], lp["a0"], lp["a_up"],
      lp["g_up"], bd)


def _rwkv_scan_kernel(rf, vf, kkf, lwf, kdf, bf_, rb, vb, kkb, lwb, kdb, bb, s0_ref, of_ref, ob_ref, sfin_ref, s_s,
                      *, tb):
    step = pl.program_id(0)

    @pl.when(step == 0)
    def _():
        s_s[...] = s0_ref[...]

    c = RWKV_CHUNK
    n = A_HEAD_DIM
    row = lax.broadcasted_iota(jnp.int32, (c, c), 0)
    col = lax.broadcasted_iota(jnp.int32, (c, c), 1)
    eye = (row == col).astype(F32)
    n_ch = tb // c
    dirs = ((rf, vf, kkf, lwf, kdf, bf_, of_ref), (rb, vb, kkb, lwb, kdb, bb, ob_ref))
    heads = range(A_HEADS)
    hsl = [slice(h * n, (h + 1) * n) for h in heads]

    def chunk_terms(d, refs, ci):
        r_ref, v_ref, kk_ref, lw_ref, kd_ref, b_ref, _ = refs
        if d == 0:
            incl, strict = row >= col, row > col
        else:
            incl, strict = row <= col, row < col
        rows = slice(ci * c, (ci + 1) * c)
        lw = lw_ref[rows, :]
        r = r_ref[rows, :]
        kk = kk_ref[rows, :]
        kd = kd_ref[rows, :]
        b = b_ref[rows, :]
        cl = _dot_split_rhs(incl.astype(BF16), lw)
        cl_tot = cl[c - 1:c, :] if d == 0 else cl[0:1, :]
        e_neg = jnp.exp(-cl)
        e_end = jnp.exp(cl_tot - cl)
        g_tot = jnp.exp(cl_tot)
        rt = r * jnp.exp(cl)
        x_all = _bf(jnp.concatenate([-(kk * jnp.exp(cl - lw)), rt], axis=0))
        z_all = _bf(jnp.concatenate([b * e_neg, kd * e_neg], axis=0))
        bh_all = _bf(b * e_end)
        kh_all = _bf(kd * e_end)
        v_all = _bf(v_ref[rows, :])
        g = [_dot_nt(x_all[:, hs], z_all[:, hs]) for hs in hsl]
        a_ab = [jnp.where(strict, gi[:c, :c], 0.0) for gi in g]
        a_rb = [_bf(jnp.where(incl, gi[c:, :c], 0.0)) for gi in g]
        a_k = [_bf(jnp.concatenate([jnp.where(strict, gi[:c, c:], 0.0), jnp.where(incl, gi[c:, c:], 0.0)], axis=0))
               for gi in g]
        a2 = [_bdot(_bf(a), _bf(a)) for a in a_ab]
        pa = [jnp.concatenate([eye + a, sq], axis=0) for a, sq in zip(a_ab, a2)]
        for _ in range(4):
            nxt = [_bdot(_bf(x), _bf(x[c:])) for x in pa]
            pa = [jnp.concatenate([x[:c] + y[:c], y[c:]], axis=0) for x, y in zip(pa, nxt)]
        p = [_bf(x[:c] + _bdot(_bf(x[:c]), _bf(x[c:]))) for x in pa]
        vk = [_bdot(a, v_all[:, hs]) for a, hs in zip(a_k, hsl)]
        tw = [_bf(_bdot(pi, jnp.concatenate([x_all[:c, hs], _bf(vki[:c])], axis=1)))
              for pi, vki, hs in zip(p, vk, hsl)]
        mn = [_dot_tn(twi, bh_all[:, hs]) for twi, hs in zip(tw, hsl)]
        kv = [_dot_tn(v_all[:, hs], kh_all[:, hs]) for hs in hsl]
        rw = [_bdot(a, twi) for a, twi in zip(a_rb, tw)]
        mt = [_bf(m[:n, :]) for m in mn]
        nt = [m[n:, :] + k for m, k in zip(mn, kv)]
        ry = [_bf(rt[:, hs] + w[:, :n]) for w, hs in zip(rw, hsl)]
        y0 = [w[:, n:] + vki[c:] for w, vki in zip(rw, vk)]
        return rows, g_tot, mt, nt, ry, y0

    terms = [[chunk_terms(d, dirs[d], cc if d == 0 else n_ch - 1 - cc) for cc in range(n_ch)] for d in range(2)]
    state = [[s_s[d, h] for h in heads] for d in range(2)]
    for cc in range(n_ch):
        for d in range(2):
            rows, g_tot, mt, nt, ry, y0 = terms[d][cc]
            s0 = state[d]
            ys = [_dot_nt(ry[h], _bf(s0[h])) + y0[h] for h in heads]
            upd = [_bdot(jnp.concatenate(_split(s0[h], 2), axis=0), mt[h]) for h in heads]
            state[d] = [s0[h] * g_tot[:, hsl[h]] + (upd[h][:n] + upd[h][n:]) + nt[h] for h in heads]
            dirs[d][6][rows, :] = jnp.concatenate(ys, axis=1)
    for d in range(2):
        for h in heads:
            s_s[d, h] = state[d][h]


    @pl.when(step == pl.num_programs(0) - 1)
    def _():
        sfin_ref[...] = s_s[...]


def _rwkv_scan(proj, prep, s0):
    kk, lw0, lw1, kd0, kd1, b0, b1 = prep[:7]
    t_len = proj.shape[0]
    tb = 2 * RWKV_CHUNK
    nb = t_len // tb
    aw = A_WIDTH
    fwd = lambda c: pl.BlockSpec((tb, aw), lambda i: (i, c))
    bwd = lambda c: pl.BlockSpec((tb, aw), lambda i: (nb - 1 - i, c))
    st = pl.BlockSpec((2, A_HEADS, A_HEAD_DIM, A_HEAD_DIM), lambda i: (0, 0, 0, 0))
    kern = functools.partial(_rwkv_scan_kernel, tb=tb)
    return pl.pallas_call(
        kern,
        out_shape=[jax.ShapeDtypeStruct((t_len, aw), F32), jax.ShapeDtypeStruct((t_len, aw), F32),
                   jax.ShapeDtypeStruct((2, A_HEADS, A_HEAD_DIM, A_HEAD_DIM), F32)],
        grid=(nb,),
        in_specs=[fwd(0), fwd(2), fwd(0), fwd(0), fwd(0), fwd(0),
                  bwd(0), bwd(2), bwd(0), bwd(0), bwd(0), bwd(0), st],
        out_specs=[fwd(0), bwd(0), st],
        scratch_shapes=[pltpu.VMEM((2, A_HEADS, A_HEAD_DIM, A_HEAD_DIM), F32)],
        compiler_params=_params(("arbitrary",)),
        name="rwkv_scan",
    )(proj, proj, kk, lw0, kd0, b0, proj, proj, kk, lw1, kd1, b1, s0)


def _retention_kernel(qf, kf, vf, cosf, sinf, qb, kb, vb, cosb, sinb, lgt_ref, r0_ref, yf_ref, yb_ref, rfin_ref, r_s):
    step = pl.program_id(0)

    @pl.when(step == 0)
    def _():
        r_s[...] = r0_ref[...]

    c = RET_CHUNK
    dk = B_QK_DIM
    dv = B_V_DIM
    lg_all = -_softplus(-lgt_ref[...])
    rowf = lax.broadcasted_iota(jnp.int32, (c, c), 0).astype(F32)
    colf = lax.broadcasted_iota(jnp.int32, (c, c), 1).astype(F32)
    lane = lax.broadcasted_iota(jnp.int32, (c, dk), 1)
    first_half = (lane % 64) < 32
    dirs = ((qf, kf, vf, cosf, sinf, yf_ref), (qb, kb, vb, cosb, sinb, yb_ref))
    for d, (q_ref, k_ref, v_ref, cos_ref, sin_ref, y_ref) in enumerate(dirs):
        cos = cos_ref[...]
        sin = sin_ref[...]
        diff = (rowf - colf) if d == 0 else (colf - rowf)
        pos = rowf if d == 0 else (c - 1.0) - rowf
        for h in range(B_HEADS):
            lg = lg_all[d * B_HEADS + h:d * B_HEADS + h + 1, :]
            qh = q_ref[:, h * dk:(h + 1) * dk]
            kh = k_ref[:, h * dk:(h + 1) * dk] * (dk ** -0.5)

            def rope(xv):
                swapped = jnp.where(first_half, pltpu.roll(xv, dk - 32, 1), pltpu.roll(xv, 32, 1))
                return xv * cos + swapped * sin

            qh = rope(qh)
            kh = rope(kh)
            vh = _bf(v_ref[:, h * dv:(h + 1) * dv])
            dmask = jnp.where(diff >= 0.0, jnp.exp(lg * jnp.maximum(diff, 0.0)), 0.0)
            scores = _dot_nt(_bf(qh), _bf(kh)) * dmask
            inner = _bdot(_bf(scores), vh)
            xi = jnp.exp(lg * (pos + 1.0))
            zeta = jnp.exp(lg * ((c - 1.0) - pos))
            r_prev = r_s[d, h]
            cross = _bdot(_bf(qh * xi), _bf(r_prev))
            y_ref[:, h * dv:(h + 1) * dv] = inner + cross
            kv = _dot_tn(_bf(kh * zeta), vh)
            g_chunk = jnp.exp(lg * float(c))
            r_s[d, h] = jnp.concatenate([g_chunk, g_chunk], axis=1) * r_prev + kv

    @pl.when(step == pl.num_programs(0) - 1)
    def _():
        rfin_ref[...] = r_s[...]


def _retention(proj, cos, sin, lgt, r0):
    t_len = proj.shape[0]
    c = RET_CHUNK
    nc = t_len // c
    qw = B_QK_WIDTH
    vw = B_V_WIDTH
    fq = lambda col: pl.BlockSpec((c, qw), lambda i: (i, col))
    bq = lambda col: pl.BlockSpec((c, qw), lambda i: (nc - 1 - i, col))
    st = pl.BlockSpec((2, B_HEADS, B_QK_DIM, B_V_DIM), lambda i: (0, 0, 0, 0))
    q_col = SHIFT_PAD // qw
    v_col = (SHIFT_PAD + 2 * qw) // vw
    return pl.pallas_call(
        _retention_kernel,
        out_shape=[jax.ShapeDtypeStruct((t_len, vw), F32), jax.ShapeDtypeStruct((t_len, vw), F32),
                   jax.ShapeDtypeStruct((2, B_HEADS, B_QK_DIM, B_V_DIM), F32)],
        grid=(nc,),
        in_specs=[
            fq(q_col), fq(q_col + 1), pl.BlockSpec((c, vw), lambda i: (i, v_col)),
            pl.BlockSpec((c, B_QK_DIM), lambda i: (i, 0)), pl.BlockSpec((c, B_QK_DIM), lambda i: (i, 0)),
            bq(q_col), bq(q_col + 1), pl.BlockSpec((c, vw), lambda i: (nc - 1 - i, v_col)),
            pl.BlockSpec((c, B_QK_DIM), lambda i: (nc - 1 - i, 0)),
            pl.BlockSpec((c, B_QK_DIM), lambda i: (nc - 1 - i, 0)),
            pl.BlockSpec((2 * B_HEADS, 128), lambda i: (0, 0)), st,
        ],
        out_specs=[pl.BlockSpec((c, vw), lambda i: (i, 0)), pl.BlockSpec((c, vw), lambda i: (nc - 1 - i, 0)), st],
        scratch_shapes=[pltpu.VMEM((2, B_HEADS, B_QK_DIM, B_V_DIM), F32)],
        compiler_params=_params(("arbitrary",)),
        name="retention",
    )(proj, proj, proj, cos, sin, proj, proj, proj, cos, sin, lgt, r0)


def _mix_out_kernel(of_ref, ob_ref, bonus_ref, g_ref, yf_ref, yb_ref, gb_ref, ga_ref, gbb_ref, x_ref,
                    alnw, alnb, rlnw, rlnb, npm, npf, g1_ref, sc2_ref, sh2_ref, wa_ref, wb_ref, wo_ref, bd_ref,
                    x1_ref, h2_ref):
    bd = bd_ref[...]
    o = of_ref[...] + ob_ref[...]
    mu = _dot_split_lhs(o, bd) * (1.0 / A_HEAD_DIM)
    oc = o - mu
    var = _dot_split_lhs(oc * oc, bd) * (1.0 / A_HEAD_DIM)
    ya = oc * lax.rsqrt(var + RWKV_GN_EPS) * alnw[...] + alnb[...] + bonus_ref[...]
    ya = _bdot(_bf(ya * g_ref[...]), wa_ref[...])

    y = yf_ref[...] + yb_ref[...]
    parts = []
    for h in range(B_HEADS):
        seg = y[:, h * B_V_DIM:(h + 1) * B_V_DIM]
        m = jnp.mean(seg, axis=-1, keepdims=True)
        sc = seg - m
        vr = jnp.mean(sc * sc, axis=-1, keepdims=True)
        parts.append(sc * lax.rsqrt(vr + RET_GN_EPS))
    yn = jnp.concatenate(parts, axis=1) * rlnw[...] + rlnb[...]
    gb = gb_ref[...]
    yb = _bdot(_bf(yn * (gb * _sigmoid(gb))), wb_ref[...])

    merged = _sigmoid(ga_ref[...]) * ya + _sigmoid(gbb_ref[...]) * yb
    mix = _bdot(_bf(merged), wo_ref[...])
    x1 = x_ref[...] + g1_ref[...] * (_rms(mix) * npm[...])
    x1_ref[...] = x1
    h2_ref[...] = (_rms(x1) * npf[...]) * (1.0 + sc2_ref[...]) + sh2_ref[...]


def _mix_out(x2d, proj, o_f, o_b, bonus, g, y_f, y_b, lp, vecs, bd):
    t_len = x2d.shape[0]
    tm = min(t_len, 256)
    aw = A_WIDTH
    d = D_MODEL
    ta = lambda: pl.BlockSpec((tm, aw), lambda i: (i, 0))
    td = lambda: pl.BlockSpec((tm, d), lambda i: (i, 0))
    pc = lambda c: pl.BlockSpec((tm, d), lambda i: (i, c))
    va = lambda: pl.BlockSpec((1, aw), lambda i: (0, 0))
    vd = lambda: pl.BlockSpec((1, d), lambda i: (0, 0))
    full = lambda shp: pl.BlockSpec(shp, lambda i: (0, 0))
    gcol = (SHIFT_PAD + 2 * B_QK_WIDTH + B_V_WIDTH) // d
    return pl.pallas_call(
        _mix_out_kernel,
        out_shape=[jax.ShapeDtypeStruct((t_len, d), F32), jax.ShapeDtypeStruct((t_len, d), F32)],
        grid=(t_len // tm,),
        in_specs=[ta(), ta(), ta(), ta(), td(), td(), pc(gcol), pc(gcol + 1), pc(gcol + 2), td(),
                  va(), va(), vd(), vd(), vd(), vd(), vd(), vd(), vd(),
                  full((aw, d)), full((d, d)), full((d, d)), full((aw, aw))],
        out_specs=[td(), td()],
        compiler_params=_params(("arbitrary",)),
        name="mix_out",
    )(o_f, o_b, bonus, g, y_f, y_b, proj, proj, proj, x2d,
      lp["ln_w"], lp["ln_b"], lp["ret_ln_w"], lp["ret_ln_b"], vecs["npm"], vecs["npf"], vecs["g1"], vecs["sc2"],
      vecs["sh2"], lp["w_a"], lp["w_b"], lp["w_o"], bd)


def _router_kernel(h_ref, rw_ref, bias_ref, sel_ref, wts_ref, rank_ref, cnt_ref, cnt_s, *, tm):
    step = pl.program_id(0)

    @pl.when(step == 0)
    def _():
        cnt_s[...] = jnp.zeros_like(cnt_s)

    ne = N_EXPERTS
    scores = _sigmoid(_dot3(h_ref[...], rw_ref[...]))
    work = scores + bias_ref[...]
    lane = lax.broadcasted_iota(jnp.int32, (tm, ne), 1).astype(F32)
    idxs = []
    vals = []
    for _ in range(TOP_K):
        m = jnp.max(work, axis=-1, keepdims=True)
        idx = jnp.min(jnp.where(work == m, lane, float(ne)), axis=-1, keepdims=True)
        oh = lane == idx
        vals.append(jnp.sum(jnp.where(oh, scores, 0.0), axis=-1, keepdims=True))
        idxs.append(idx)
        work = jnp.where(oh, -jnp.inf, work)
    sel_f = jnp.concatenate(idxs, axis=1)
    s_sel = jnp.concatenate(vals, axis=1)
    wts_ref[...] = s_sel / jnp.sum(s_sel, axis=1, keepdims=True) * ROUTED_SCALE
    sel_ref[...] = sel_f.astype(jnp.int32)

    hit = work == -jnp.inf
    r_i = lax.broadcasted_iota(jnp.int32, (tm, tm), 0)
    c_i = lax.broadcasted_iota(jnp.int32, (tm, tm), 1)
    before = _bdot((r_i > c_i).astype(BF16), hit.astype(BF16)) + cnt_s[...]
    ranks = [jnp.sum(jnp.where(lane == idxs[k], before, 0.0), axis=-1, keepdims=True) for k in range(TOP_K)]
    rank_ref[...] = jnp.concatenate(ranks, axis=1).astype(jnp.int32)
    cnt = cnt_s[...] + jnp.sum(hit.astype(F32), axis=0, keepdims=True)
    cnt_s[...] = cnt
    cnt_ref[...] = cnt.astype(jnp.int32)


def _router(h2, router_w, router_bias):
    t_len = h2.shape[0]
    tm = min(t_len, 256)
    kern = functools.partial(_router_kernel, tm=tm)
    tk = lambda: pl.BlockSpec((tm, TOP_K), lambda i: (i, 0))
    return pl.pallas_call(
        kern,
        out_shape=[jax.ShapeDtypeStruct((t_len, TOP_K), jnp.int32), jax.ShapeDtypeStruct((t_len, TOP_K), F32),
                   jax.ShapeDtypeStruct((t_len, TOP_K), jnp.int32), jax.ShapeDtypeStruct((1, N_EXPERTS), jnp.int32)],
        grid=(t_len // tm,),
        in_specs=[pl.BlockSpec((tm, D_MODEL), lambda i: (i, 0)),
                  pl.BlockSpec((D_MODEL, N_EXPERTS), lambda i: (0, 0)),
                  pl.BlockSpec((1, N_EXPERTS), lambda i: (0, 0))],
        out_specs=[tk(), tk(), tk(), pl.BlockSpec((1, N_EXPERTS), lambda i: (0, 0))],
        scratch_shapes=[pltpu.VMEM((1, N_EXPERTS), F32)],
        compiler_params=_params(("arbitrary",)),
        name="router",
    )(h2, router_w, router_bias.reshape(1, N_EXPERTS))


def _dispatch_kernel(sel_ref, rank_ref, start_ref, h_ref, xs_in, xs_out, sem, *, tm):
    del xs_in

    def row_copy(t, k):
        slot = start_ref[sel_ref[t * TOP_K + k]] + rank_ref[t * TOP_K + k]
        return pltpu.make_async_copy(h_ref.at[pl.ds(t, 1)], xs_out.at[pl.ds(slot, 1)], sem)

    def issue(t, carry):
        for k in range(TOP_K):
            row_copy(t, k).start()
        return carry

    lax.fori_loop(0, tm, issue, 0)

    def drain(t, carry):
        for k in range(TOP_K):
            row_copy(t, k).wait()
        return carry

    lax.fori_loop(0, tm, drain, 0)


def _dispatch(h2, sel_flat, rank_flat, pad_start, n_slots):
    t_len = h2.shape[0]
    tm = min(t_len, 256)
    kern = functools.partial(_dispatch_kernel, tm=tm)
    smem_blk = lambda: pl.BlockSpec((tm * TOP_K,), lambda i: (i,), memory_space=pltpu.SMEM)
    xs0 = jnp.zeros((n_slots, D_MODEL), F32)
    return pl.pallas_call(
        kern,
        out_shape=jax.ShapeDtypeStruct((n_slots, D_MODEL), F32),
        grid=(t_len // tm,),
        in_specs=[smem_blk(), smem_blk(),
                  pl.BlockSpec((N_EXPERTS,), lambda i: (0,), memory_space=pltpu.SMEM),
                  pl.BlockSpec((tm, D_MODEL), lambda i: (i, 0)),
                  pl.BlockSpec(memory_space=pl.ANY)],
        out_specs=pl.BlockSpec(memory_space=pl.ANY),
        scratch_shapes=[pltpu.SemaphoreType.DMA(())],
        input_output_aliases={4: 0},
        compiler_params=_params(("arbitrary",)),
        name="dispatch",
    )(sel_flat, rank_flat, pad_start, h2, xs0)


def _expert_kernel(be_ref, nu_ref, xs_ref, wg_ref, wu_ref, wd_ref, ys_ref):
    b = pl.program_id(0)

    @pl.when(b < nu_ref[0])
    def _():
        xb = _bf(xs_ref[...])
        gate = _bdot(xb, _bf(wg_ref[0]))
        up = _bdot(xb, _bf(wu_ref[0]))
        act = gate * _sigmoid(gate) * up
        ys_ref[...] = _bdot(_bf(act), _bf(wd_ref[0]))

    @pl.when(b >= nu_ref[0])
    def _():
        ys_ref[...] = jnp.zeros_like(ys_ref)


def _experts(xs, block_e, n_used, w_gate, w_up, w_down):
    n_slots = xs.shape[0]
    n_blocks = n_slots // MOE_BLOCK
    grid_spec = pltpu.PrefetchScalarGridSpec(
        num_scalar_prefetch=2,
        grid=(n_blocks,),
        in_specs=[
            pl.BlockSpec((MOE_BLOCK, D_MODEL), lambda b, be, nu: (b, 0)),
            pl.BlockSpec((1, D_MODEL, EXPERT_FF), lambda b, be, nu: (be[b], 0, 0)),
            pl.BlockSpec((1, D_MODEL, EXPERT_FF), lambda b, be, nu: (be[b], 0, 0)),
            pl.BlockSpec((1, EXPERT_FF, D_MODEL), lambda b, be, nu: (be[b], 0, 0)),
        ],
        out_specs=pl.BlockSpec((MOE_BLOCK, D_MODEL), lambda b, be, nu: (b, 0)),
    )
    return pl.pallas_call(
        _expert_kernel,
        out_shape=jax.ShapeDtypeStruct((n_slots, D_MODEL), F32),
        grid_spec=grid_spec,
        compiler_params=_params(("arbitrary",)),
        name="experts",
    )(block_e, n_used, xs, w_gate, w_up, w_down)


def _combine_kernel(sel_ref, rank_ref, start_ref, ys_ref, wts_ref, h_ref, x1_ref, sg_ref, su_ref, sd_ref, npo, g2_ref,
                    o_ref, buf, sem, *, tm):
    def row_copy(t, k):
        slot = start_ref[sel_ref[t * TOP_K + k]] + rank_ref[t * TOP_K + k]
        return pltpu.make_async_copy(ys_ref.at[pl.ds(slot, 1)], buf.at[k, pl.ds(t, 1)], sem)

    def issue(t, carry):
        for k in range(TOP_K):
            row_copy(t, k).start()
        return carry

    lax.fori_loop(0, tm, issue, 0)

    hb = _bf(h_ref[...])
    gate = _bdot(hb, sg_ref[...])
    up = _bdot(hb, su_ref[...])
    shared = _bdot(_bf(gate * _sigmoid(gate) * up), sd_ref[...])

    def drain(t, carry):
        for k in range(TOP_K):
            row_copy(t, k).wait()
        return carry

    lax.fori_loop(0, tm, drain, 0)

    wts = wts_ref[...]
    routed = buf[0] * wts[:, 0:1]
    for k in range(1, TOP_K):
        routed = routed + buf[k] * wts[:, k:k + 1]
    o_ref[...] = x1_ref[...] + g2_ref[...] * (_rms(routed + shared) * npo[...])


def _combine(ys, sel_flat, rank_flat, pad_start, wts, h2, x1, sg, su, sd, npo, g2):
    t_len = h2.shape[0]
    tm = min(t_len, 256)
    d = D_MODEL
    kern = functools.partial(_combine_kernel, tm=tm)
    smem_blk = lambda: pl.BlockSpec((tm * TOP_K,), lambda i: (i,), memory_space=pltpu.SMEM)
    td = lambda: pl.BlockSpec((tm, d), lambda i: (i, 0))
    vd = lambda: pl.BlockSpec((1, d), lambda i: (0, 0))
    return pl.pallas_call(
        kern,
        out_shape=jax.ShapeDtypeStruct((t_len, d), F32),
        grid=(t_len // tm,),
        in_specs=[smem_blk(), smem_blk(),
                  pl.BlockSpec((N_EXPERTS,), lambda i: (0,), memory_space=pltpu.SMEM),
                  pl.BlockSpec(memory_space=pl.ANY),
                  pl.BlockSpec((tm, TOP_K), lambda i: (i, 0)),
                  td(), td(),
                  pl.BlockSpec((d, SHARED_FF), lambda i: (0, 0)), pl.BlockSpec((d, SHARED_FF), lambda i: (0, 0)),
                  pl.BlockSpec((SHARED_FF, d), lambda i: (0, 0)), vd(), vd()],
        out_specs=td(),
        scratch_shapes=[pltpu.VMEM((TOP_K, tm, d), F32), pltpu.SemaphoreType.DMA(())],
        compiler_params=_params(("arbitrary",)),
        name="combine",
    )(sel_flat, rank_flat, pad_start, ys, wts, h2, x1, sg, su, sd, npo, g2)


def _block_diag_ones(width, group):
    idx = np.arange(width) // group
    return jnp.asarray(idx[:, None] == idx[None, :], dtype=BF16)


def _rope_tables(t_len):
    pos = np.arange(t_len)
    rows = (pos // GRID_W).astype(np.float32)
    cols = (pos % GRID_W).astype(np.float32)
    quarter = B_QK_DIM // 4
    inv_freq = jnp.asarray(ROPE_BASE, F32) ** (-jnp.arange(quarter, dtype=F32) / quarter)
    ang_r = jnp.asarray(rows)[:, None] * inv_freq
    ang_c = jnp.asarray(cols)[:, None] * inv_freq
    cr, sr, cc, sc = jnp.cos(ang_r), jnp.sin(ang_r), jnp.cos(ang_c), jnp.sin(ang_c)
    return jnp.concatenate([cr, cr, cc, cc], axis=1), jnp.concatenate([-sr, sr, -sc, sc], axis=1)


def _token_mixer(x2d, g_pre, sc, sh, lp, cos, sin, states, bd):
    proj = _in_proj(x2d, g_pre, sc, sh, lp["w_in"], lp["mu"])
    prep = _rwkv_prep(proj, lp, bd)
    o_f, o_b, s_fin = _rwkv_scan(proj, prep, states[0])
    y_f, y_b, r_fin = _retention(proj, cos, sin, lp["lgt"], states[1])
    return proj, prep, (o_f, o_b), (y_f, y_b), (s_fin, r_fin)


def kernel(x, c, ctx, c_ctx, w_mod, b_mod, norm_pre_mix, norm_post_mix, norm_pre_ffn, norm_post_ffn, w_in, shift_mu,
           rwkv_w0, rwkv_w_up, rwkv_a0, rwkv_a_up, rwkv_g_up, rwkv_k_k, rwkv_k_a, rwkv_r_k, rwkv_ln_w, rwkv_ln_b,
           w_branch_a, ret_decay_logit, ret_ln_w, ret_ln_b, w_branch_b, w_out, router_w, router_bias, exp_w_gate,
           exp_w_up, exp_w_down, sh_w_gate, sh_w_up, sh_w_down):
    d = D_MODEL
    assert x.shape[0] == 1 and w_in.shape[0] == 1, "single batch element, single layer"
    t_len = x.shape[1]
    x2d = x.reshape(t_len, d)
    ctx2d = ctx.reshape(ctx.shape[1], d)
    row = lambda a: a.reshape(1, -1)

    cs = jnp.zeros((8, d), F32).at[0].set(c[0]).at[1].set(c_ctx)
    mod = _modulation(cs, w_mod[0], b_mod[0])
    sh1, sc1, g1, sh2, sc2, g2 = [mod[0:1, i * d:(i + 1) * d] for i in range(6)]
    csh1, csc1 = mod[1:2, 0:d], mod[1:2, d:2 * d]

    w_in_p = jnp.concatenate(
        [w_in[0][:, :SHIFT_COLS], jnp.zeros((d, SHIFT_PAD - SHIFT_COLS), F32), w_in[0][:, SHIFT_COLS:]], axis=1)
    lp = {
        "w_in": w_in_p.astype(BF16),
        "mu": jnp.pad(shift_mu[0], ((0, 0), (0, SHIFT_PAD - SHIFT_COLS))),
        "k_k": row(rwkv_k_k[0]), "k_a": row(rwkv_k_a[0]), "r_k": row(rwkv_r_k[0]),
        "w0": rwkv_w0[0], "w_up": rwkv_w_up[0], "a0": rwkv_a0[0], "a_up": rwkv_a_up[0], "g_up": rwkv_g_up[0],
        "ln_w": row(rwkv_ln_w[0]), "ln_b": row(rwkv_ln_b[0]),
        "ret_ln_w": row(ret_ln_w[0]), "ret_ln_b": row(ret_ln_b[0]),
        "lgt": jnp.broadcast_to(ret_decay_logit[0].reshape(2 * B_HEADS, 1), (2 * B_HEADS, 128)),
        "w_a": w_branch_a[0].astype(BF16), "w_b": w_branch_b[0].astype(BF16), "w_o": w_out[0].astype(BF16),
    }
    bd = _block_diag_ones(A_WIDTH, A_HEAD_DIM)
    g_pre = row(norm_pre_mix[0])

    t_ctx = ctx2d.shape[0]
    zero_states = (jnp.zeros((2, A_HEADS, A_HEAD_DIM, A_HEAD_DIM), F32),
                   jnp.zeros((2, B_HEADS, B_QK_DIM, B_V_DIM), F32))
    ones = jnp.ones((t_ctx, B_QK_DIM), F32)
    *_, ctx_states = _token_mixer(ctx2d, g_pre, csc1, csh1, lp, ones, jnp.zeros_like(ones), zero_states, bd)

    cos, sin = _rope_tables(t_len)
    proj, prep, (o_f, o_b), (y_f, y_b), _ = _token_mixer(x2d, g_pre, sc1, sh1, lp, cos, sin, ctx_states, bd)
    vecs = {"npm": row(norm_post_mix[0]), "npf": row(norm_pre_ffn[0]), "g1": g1, "sc2": sc2, "sh2": sh2}
    x1, h2 = _mix_out(x2d, proj, o_f, o_b, prep[7], prep[8], y_f, y_b, lp, vecs, bd)

    sel, wts, rank, counts = _router(h2, router_w[0], router_bias[0])
    counts = counts.reshape(N_EXPERTS)
    padded = (counts + MOE_BLOCK - 1) // MOE_BLOCK * MOE_BLOCK
    pad_end = jnp.cumsum(padded)
    pad_start = (pad_end - padded).astype(jnp.int32)
    n_assign = t_len * TOP_K
    n_blocks = (n_assign + N_EXPERTS * (MOE_BLOCK - 1) + MOE_BLOCK - 1) // MOE_BLOCK
    block_start = jnp.arange(n_blocks, dtype=jnp.int32) * MOE_BLOCK
    block_e = jnp.minimum(jnp.searchsorted(pad_end, block_start, side="right"), N_EXPERTS - 1).astype(jnp.int32)
    n_used = (pad_end[-1:] // MOE_BLOCK).astype(jnp.int32)
    sel_flat = sel.reshape(n_assign)
    rank_flat = rank.reshape(n_assign)

    xs = _dispatch(h2, sel_flat, rank_flat, pad_start, n_blocks * MOE_BLOCK)
    ys = _experts(xs, block_e, n_used, exp_w_gate[0], exp_w_up[0], exp_w_down[0])
    out = _combine(ys, sel_flat, rank_flat, pad_start, wts, h2, x1, sh_w_gate[0].astype(BF16),
                   sh_w_up[0].astype(BF16), sh_w_down[0].astype(BF16), row(norm_post_ffn[0]), g2)
    return out.reshape(x.shape)
```

```python
import functools

import jax
import jax.numpy as jnp
import numpy as np
from jax import lax
from jax.experimental import pallas as pl
from jax.experimental.pallas import tpu as pltpu

F32 = jnp.float32
BF16 = jnp.bfloat16

D_MODEL = 1024
GRID_W = 64
NORM_EPS = 1e-6

A_HEAD_DIM = 64
A_WIDTH = D_MODEL // 2
A_HEADS = A_WIDTH // A_HEAD_DIM
W_LORA = 64
ICLR_LORA = 64
G_LORA = 128
RWKV_GN_EPS = 64e-5

B_HEADS = 4
B_QK_WIDTH = D_MODEL // 2
B_V_WIDTH = D_MODEL
B_QK_DIM = B_QK_WIDTH // B_HEADS
B_V_DIM = B_V_WIDTH // B_HEADS
RET_CHUNK = 128
RET_GN_EPS = 1e-5
ROPE_BASE = 10000.0

SHIFT_COLS = 3 * A_WIDTH + W_LORA + ICLR_LORA + G_LORA
LORA_COLS = W_LORA + ICLR_LORA + G_LORA
SHIFT_PAD = 2048
PROJ_COLS = SHIFT_PAD + 2 * B_QK_WIDTH + 2 * B_V_WIDTH + 2 * D_MODEL

N_EXPERTS = 256
TOP_K = 8
EXPERT_FF = D_MODEL // 4
SHARED_FF = D_MODEL // 4
ROUTED_SCALE = 2.5
MOE_BLOCK = 128

RWKV_CHUNK = 64
PROJ_TN = 1024
VMEM_LIMIT = 48 * 1024 * 1024


def _params(sem):
    return pltpu.CompilerParams(dimension_semantics=sem, vmem_limit_bytes=VMEM_LIMIT)


def _bf(a):
    return a.astype(BF16)


def _bdot(a, b):
    return jnp.dot(a, b, preferred_element_type=F32)


def _dot_nt(a, b):
    return lax.dot_general(a, b, (((1,), (1,)), ((), ())), preferred_element_type=F32)


def _dot_tn(a, b):
    return lax.dot_general(a, b, (((0,), (0,)), ((), ())), preferred_element_type=F32)


def _split(a, n):
    out = []
    rem = a
    for _ in range(n):
        p = _bf(rem)
        out.append(p)
        rem = rem - p.astype(F32)
    return out


def _dot_split_lhs(a, b_bf, n=3):
    acc = None
    for p in _split(a, n):
        t = _bdot(p, b_bf)
        acc = t if acc is None else acc + t
    return acc


def _dot_split_rhs(a_bf, b, n=3):
    acc = None
    for p in _split(b, n):
        t = _bdot(a_bf, p)
        acc = t if acc is None else acc + t
    return acc


def _dot3(a, b):
    ah, al = _split(a, 2)
    bh, bl = _split(b, 2)
    return _bdot(ah, bh) + (_bdot(ah, bl) + _bdot(al, bh))


def _sigmoid(x):
    return 1.0 / (1.0 + jnp.exp(-x))


def _softplus(x):
    return jnp.maximum(x, 0.0) + jnp.log1p(jnp.exp(-jnp.abs(x)))


def _rms(x):
    return x * lax.rsqrt(jnp.mean(x * x, axis=-1, keepdims=True) + NORM_EPS)


LANES = 128
ROW_TILE = D_MODEL // LANES


def _rows_to_tiles(x2d, ref, lead=()):
    n = x2d.shape[0]
    for j in range(ROW_TILE):
        ref[lead + (pl.ds(j, n, stride=ROW_TILE), slice(None))] = x2d[:, j * LANES:(j + 1) * LANES]


def _tiles_to_rows(ref, n, lead=()):
    return jnp.concatenate(
        [ref[lead + (pl.ds(j, n, stride=ROW_TILE), slice(None))] for j in range(ROW_TILE)], axis=1)


def _row_tile(ref, r, lead=()):
    return ref.at[lead + (pl.ds(pl.multiple_of(r * ROW_TILE, ROW_TILE), ROW_TILE),)]


def _mod_kernel(cs_ref, w_ref, b_ref, o_ref):
    cs = cs_ref[...]
    s = cs * _sigmoid(cs)
    o_ref[...] = _dot3(s, w_ref[...]) + b_ref[...]


def _modulation(cs, w_mod, b_mod):
    n_out = w_mod.shape[1]
    tn = 1536
    return pl.pallas_call(
        _mod_kernel,
        out_shape=jax.ShapeDtypeStruct((8, n_out), F32),
        grid=(n_out // tn,),
        in_specs=[
            pl.BlockSpec((8, D_MODEL), lambda j: (0, 0)),
            pl.BlockSpec((D_MODEL, tn), lambda j: (0, j)),
            pl.BlockSpec((1, tn), lambda j: (0, j)),
        ],
        out_specs=pl.BlockSpec((8, tn), lambda j: (0, j)),
        compiler_params=_params(("arbitrary",)),
        name="modulation",
    )(cs, w_mod, b_mod.reshape(1, n_out))


def _in_proj_kernel(x_ref, xp_ref, xn_ref, g_ref, sc_ref, sh_ref, w_ref, mu_ref, o_ref, h_s, hh_s, *, tm, n_shift):
    i = pl.program_id(0)
    j = pl.program_id(1)
    n_i = pl.num_programs(0)

    def norm_mod(xv):
        return (_rms(xv) * g_ref[...]) * (1.0 + sc_ref[...]) + sh_ref[...]

    @pl.when(j == 0)
    def _():
        h_s[...] = _bf(norm_mod(x_ref[...]))
        hp = jnp.where(i > 0, norm_mod(xp_ref[...]), 0.0)
        hn = jnp.where(i < n_i - 1, norm_mod(xn_ref[...]), 0.0)
        hh_s[0:8, :] = hp
        hh_s[8:16, :] = hn

    w = w_ref[...]
    p = _bdot(h_s[...], w)

    @pl.when(j < n_shift)
    def _():
        ph = _bdot(_bf(hh_s[...]), w)
        row = lax.broadcasted_iota(jnp.int32, p.shape, 0)
        prev = jnp.where(row == 0, ph[7:8, :], pltpu.roll(p, 1, 0))
        nxt = jnp.where(row == tm - 1, ph[8:9, :], pltpu.roll(p, tm - 1, 0))
        mu = mu_ref[...]
        o_ref[...] = p + mu[0:1, :] * (prev - p) + mu[1:2, :] * (nxt - p)

    @pl.when(j >= n_shift)
    def _():
        o_ref[...] = p


def _in_proj(x2d, g, sc, sh, w_bf, mu_pad):
    t_len = x2d.shape[0]
    tm = min(t_len, 1024)
    tn = PROJ_TN
    n_shift = SHIFT_PAD // tn
    tb8 = tm // 8
    nb8 = t_len // 8
    kern = functools.partial(_in_proj_kernel, tm=tm, n_shift=n_shift)
    vec = lambda: pl.BlockSpec((1, D_MODEL), lambda i, j: (0, 0))
    return pl.pallas_call(
        kern,
        out_shape=jax.ShapeDtypeStruct((t_len, PROJ_COLS), F32),
        grid=(t_len // tm, PROJ_COLS // tn),
        in_specs=[
            pl.BlockSpec((tm, D_MODEL), lambda i, j: (i, 0)),
            pl.BlockSpec((8, D_MODEL), lambda i, j: (jnp.maximum(i * tb8 - 1, 0), 0)),
            pl.BlockSpec((8, D_MODEL), lambda i, j: (jnp.minimum((i + 1) * tb8, nb8 - 1), 0)),
            vec(), vec(), vec(),
            pl.BlockSpec((D_MODEL, tn), lambda i, j: (0, j)),
            pl.BlockSpec((2, tn), lambda i, j: (0, jnp.minimum(j, n_shift - 1))),
        ],
        out_specs=pl.BlockSpec((tm, tn), lambda i, j: (i, j)),
        scratch_shapes=[pltpu.VMEM((tm, D_MODEL), BF16), pltpu.VMEM((16, D_MODEL), F32)],
        compiler_params=_params(("arbitrary", "arbitrary")),
        name="in_proj",
    )(x2d, x2d, x2d, g, sc, sh, w_bf, mu_pad)


def _rwkv_prep_kernel(r_ref, k_ref, v_ref, lora_ref, kk_w, ka_w, rk_w, w0_ref, wup_ref, a0_ref, aup_ref, gup_ref,
                      bd_ref, kk_o, lw0_o, lw1_o, kd0_o, kd1_o, b0_o, b1_o, bonus_o, g_o):
    r = r_ref[...]
    k = k_ref[...]
    v = v_ref[...]
    lora = lora_ref[...]
    xw = lora[:, 0:W_LORA]
    xa = lora[:, W_LORA:W_LORA + ICLR_LORA]
    xg = lora[:, W_LORA + ICLR_LORA:]
    bd = bd_ref[...]

    kk = k * kk_w[...]
    nrm = jnp.sqrt(_dot_split_lhs(kk * kk, bd))
    kk = kk / jnp.maximum(nrm, 1e-12)
    kk_o[...] = kk

    tw = jnp.tanh(xw)
    lw_outs = (lw0_o, lw1_o)
    kd_outs = (kd0_o, kd1_o)
    b_outs = (b0_o, b1_o)
    kd_sum = None
    for d in range(2):
        z = w0_ref[d:d + 1, :] + _dot3(tw, wup_ref[d])
        w_log = -_softplus(-z) - 0.5
        lw_outs[d][...] = -jnp.exp(w_log)
        a_d = _sigmoid(a0_ref[d:d + 1, :] + _dot3(xa, aup_ref[d]))
        kd = k * (1.0 + (a_d - 1.0) * ka_w[...])
        kd_outs[d][...] = kd
        b_outs[d][...] = kk * a_d
        kd_sum = kd if kd_sum is None else kd_sum + kd
    g_o[...] = _dot3(_sigmoid(xg), gup_ref[...])
    bonus_o[...] = _dot_split_lhs(r * kd_sum * rk_w[...], bd) * v


def _rwkv_prep(proj, lp, bd):
    t_len = proj.shape[0]
    tm = min(t_len, 512)
    aw = A_WIDTH
    col = lambda c: pl.BlockSpec((tm, aw), lambda i: (i, c))
    vec = lambda: pl.BlockSpec((1, aw), lambda i: (0, 0))
    full = lambda shp: pl.BlockSpec(shp, lambda i: (0,) * len(shp))
    outs = [jax.ShapeDtypeStruct((t_len, aw), F32)] * 9
    return pl.pallas_call(
        _rwkv_prep_kernel,
        out_shape=outs,
        grid=(t_len // tm,),
        in_specs=[
            col(0), col(1), col(2),
            pl.BlockSpec((tm, LORA_COLS), lambda i: (i, 3 * aw // LORA_COLS)),
            vec(), vec(), vec(),
            full((2, aw)), full((2, W_LORA, aw)), full((2, aw)), full((2, ICLR_LORA, aw)), full((G_LORA, aw)),
            full((aw, aw)),
        ],
        out_specs=[pl.BlockSpec((tm, aw), lambda i: (i, 0))] * 9,
        compiler_params=_params(("arbitrary",)),
        name="rwkv_prep",
    )(proj, proj, proj, proj, lp["k_k"], lp["k_a"], lp["r_k"], lp["w0"], lp["w_up"], lp["a0"], lp["a_up"],
      lp["g_up"], bd)


def _rwkv_scan_kernel(rf, vf, kkf, lwf, kdf, bf_, rb, vb, kkb, lwb, kdb, bb, s0_ref, of_ref, ob_ref, sfin_ref, s_s,
                      *, tb):
    step = pl.program_id(0)

    @pl.when(step == 0)
    def _():
        s_s[...] = s0_ref[...]

    c = RWKV_CHUNK
    n = A_HEAD_DIM
    row = lax.broadcasted_iota(jnp.int32, (c, c), 0)
    col = lax.broadcasted_iota(jnp.int32, (c, c), 1)
    eye = (row == col).astype(F32)
    n_ch = tb // c
    dirs = ((rf, vf, kkf, lwf, kdf, bf_, of_ref), (rb, vb, kkb, lwb, kdb, bb, ob_ref))
    heads = range(A_HEADS)
    hsl = [slice(h * n, (h + 1) * n) for h in heads]

    def chunk_terms(d, refs, ci):
        r_ref, v_ref, kk_ref, lw_ref, kd_ref, b_ref, _ = refs
        if d == 0:
            incl, strict = row >= col, row > col
        else:
            incl, strict = row <= col, row < col
        rows = slice(ci * c, (ci + 1) * c)
        lw = lw_ref[rows, :]
        r = r_ref[rows, :]
        kk = kk_ref[rows, :]
        kd = kd_ref[rows, :]
        b = b_ref[rows, :]
        cl = _dot_split_rhs(incl.astype(BF16), lw)
        cl_tot = cl[c - 1:c, :] if d == 0 else cl[0:1, :]
        e_neg = jnp.exp(-cl)
        e_end = jnp.exp(cl_tot - cl)
        g_tot = jnp.exp(cl_tot)
        rt = r * jnp.exp(cl)
        x_all = _bf(jnp.concatenate([-(kk * jnp.exp(cl - lw)), rt], axis=0))
        z_all = _bf(jnp.concatenate([b * e_neg, kd * e_neg], axis=0))
        bh_all = _bf(b * e_end)
        kh_all = _bf(kd * e_end)
        v_all = _bf(v_ref[rows, :])
        g = [_dot_nt(x_all[:, hs], z_all[:, hs]) for hs in hsl]
        a_ab = [jnp.where(strict, gi[:c, :c], 0.0) for gi in g]
        a_rb = [_bf(jnp.where(incl, gi[c:, :c], 0.0)) for gi in g]
        a_k = [_bf(jnp.concatenate([jnp.where(strict, gi[:c, c:], 0.0), jnp.where(incl, gi[c:, c:], 0.0)], axis=0))
               for gi in g]
        a2 = [_bdot(_bf(a), _bf(a)) for a in a_ab]
        pa = [jnp.concatenate([eye + a, sq], axis=0) for a, sq in zip(a_ab, a2)]
        for _ in range(4):
            nxt = [_bdot(_bf(x), _bf(x[c:])) for x in pa]
            pa = [jnp.concatenate([x[:c] + y[:c], y[c:]], axis=0) for x, y in zip(pa, nxt)]
        p = [_bf(x[:c] + _bdot(_bf(x[:c]), _bf(x[c:]))) for x in pa]
        vk = [_bdot(a, v_all[:, hs]) for a, hs in zip(a_k, hsl)]
        tw = [_bf(_bdot(pi, jnp.concatenate([x_all[:c, hs], _bf(vki[:c])], axis=1)))
              for pi, vki, hs in zip(p, vk, hsl)]
        mn = [_dot_tn(twi, bh_all[:, hs]) for twi, hs in zip(tw, hsl)]
        kv = [_dot_tn(v_all[:, hs], kh_all[:, hs]) for hs in hsl]
        rw = [_bdot(a, twi) for a, twi in zip(a_rb, tw)]
        mt = [_bf(m[:n, :]) for m in mn]
        nt = [m[n:, :] + k for m, k in zip(mn, kv)]
        ry = [_bf(rt[:, hs] + w[:, :n]) for w, hs in zip(rw, hsl)]
        y0 = [w[:, n:] + vki[c:] for w, vki in zip(rw, vk)]
        return rows, g_tot, mt, nt, ry, y0

    terms = [[chunk_terms(d, dirs[d], cc if d == 0 else n_ch - 1 - cc) for cc in range(n_ch)] for d in range(2)]
    state = [[s_s[d, h] for h in heads] for d in range(2)]
    for cc in range(n_ch):
        for d in range(2):
            rows, g_tot, mt, nt, ry, y0 = terms[d][cc]
            s0 = state[d]
            ys = [_dot_nt(ry[h], _bf(s0[h])) + y0[h] for h in heads]
            upd = [_bdot(jnp.concatenate(_split(s0[h], 2), axis=0), mt[h]) for h in heads]
            state[d] = [s0[h] * g_tot[:, hsl[h]] + (upd[h][:n] + upd[h][n:]) + nt[h] for h in heads]
            dirs[d][6][rows, :] = jnp.concatenate(ys, axis=1)
    for d in range(2):
        for h in heads:
            s_s[d, h] = state[d][h]


    @pl.when(step == pl.num_programs(0) - 1)
    def _():
        sfin_ref[...] = s_s[...]


def _rwkv_scan(proj, prep, s0):
    kk, lw0, lw1, kd0, kd1, b0, b1 = prep[:7]
    t_len = proj.shape[0]
    tb = 2 * RWKV_CHUNK
    nb = t_len // tb
    aw = A_WIDTH
    fwd = lambda c: pl.BlockSpec((tb, aw), lambda i: (i, c))
    bwd = lambda c: pl.BlockSpec((tb, aw), lambda i: (nb - 1 - i, c))
    st = pl.BlockSpec((2, A_HEADS, A_HEAD_DIM, A_HEAD_DIM), lambda i: (0, 0, 0, 0))
    kern = functools.partial(_rwkv_scan_kernel, tb=tb)
    return pl.pallas_call(
        kern,
        out_shape=[jax.ShapeDtypeStruct((t_len, aw), F32), jax.ShapeDtypeStruct((t_len, aw), F32),
                   jax.ShapeDtypeStruct((2, A_HEADS, A_HEAD_DIM, A_HEAD_DIM), F32)],
        grid=(nb,),
        in_specs=[fwd(0), fwd(2), fwd(0), fwd(0), fwd(0), fwd(0),
                  bwd(0), bwd(2), bwd(0), bwd(0), bwd(0), bwd(0), st],
        out_specs=[fwd(0), bwd(0), st],
        scratch_shapes=[pltpu.VMEM((2, A_HEADS, A_HEAD_DIM, A_HEAD_DIM), F32)],
        compiler_params=_params(("arbitrary",)),
        name="rwkv_scan",
    )(proj, proj, kk, lw0, kd0, b0, proj, proj, kk, lw1, kd1, b1, s0)


def _retention_kernel(qf, kf, vf, cosf, sinf, qb, kb, vb, cosb, sinb, lgt_ref, r0_ref, yf_ref, yb_ref, rfin_ref, r_s):
    step = pl.program_id(0)

    @pl.when(step == 0)
    def _():
        r_s[...] = r0_ref[...]

    c = RET_CHUNK
    dk = B_QK_DIM
    dv = B_V_DIM
    lg_all = -_softplus(-lgt_ref[...])
    rowf = lax.broadcasted_iota(jnp.int32, (c, c), 0).astype(F32)
    colf = lax.broadcasted_iota(jnp.int32, (c, c), 1).astype(F32)
    lane = lax.broadcasted_iota(jnp.int32, (c, dk), 1)
    first_half = (lane % 64) < 32
    dirs = ((qf, kf, vf, cosf, sinf, yf_ref), (qb, kb, vb, cosb, sinb, yb_ref))
    for d, (q_ref, k_ref, v_ref, cos_ref, sin_ref, y_ref) in enumerate(dirs):
        cos = cos_ref[...]
        sin = sin_ref[...]
        diff = (rowf - colf) if d == 0 else (colf - rowf)
        pos = rowf if d == 0 else (c - 1.0) - rowf
        for h in range(B_HEADS):
            lg = lg_all[d * B_HEADS + h:d * B_HEADS + h + 1, :]
            qh = q_ref[:, h * dk:(h + 1) * dk]
            kh = k_ref[:, h * dk:(h + 1) * dk] * (dk ** -0.5)

            def rope(xv):
                swapped = jnp.where(first_half, pltpu.roll(xv, dk - 32, 1), pltpu.roll(xv, 32, 1))
                return xv * cos + swapped * sin

            qh = rope(qh)
            kh = rope(kh)
            vh = _bf(v_ref[:, h * dv:(h + 1) * dv])
            dmask = jnp.where(diff >= 0.0, jnp.exp(lg * jnp.maximum(diff, 0.0)), 0.0)
            scores = _dot_nt(_bf(qh), _bf(kh)) * dmask
            inner = _bdot(_bf(scores), vh)
            xi = jnp.exp(lg * (pos + 1.0))
            zeta = jnp.exp(lg * ((c - 1.0) - pos))
            r_prev = r_s[d, h]
            cross = _bdot(_bf(qh * xi), _bf(r_prev))
            y_ref[:, h * dv:(h + 1) * dv] = inner + cross
            kv = _dot_tn(_bf(kh * zeta), vh)
            g_chunk = jnp.exp(lg * float(c))
            r_s[d, h] = jnp.concatenate([g_chunk, g_chunk], axis=1) * r_prev + kv

    @pl.when(step == pl.num_programs(0) - 1)
    def _():
        rfin_ref[...] = r_s[...]


def _retention(proj, cos, sin, lgt, r0):
    t_len = proj.shape[0]
    c = RET_CHUNK
    nc = t_len // c
    qw = B_QK_WIDTH
    vw = B_V_WIDTH
    fq = lambda col: pl.BlockSpec((c, qw), lambda i: (i, col))
    bq = lambda col: pl.BlockSpec((c, qw), lambda i: (nc - 1 - i, col))
    st = pl.BlockSpec((2, B_HEADS, B_QK_DIM, B_V_DIM), lambda i: (0, 0, 0, 0))
    q_col = SHIFT_PAD // qw
    v_col = (SHIFT_PAD + 2 * qw) // vw
    return pl.pallas_call(
        _retention_kernel,
        out_shape=[jax.ShapeDtypeStruct((t_len, vw), F32), jax.ShapeDtypeStruct((t_len, vw), F32),
                   jax.ShapeDtypeStruct((2, B_HEADS, B_QK_DIM, B_V_DIM), F32)],
        grid=(nc,),
        in_specs=[
            fq(q_col), fq(q_col + 1), pl.BlockSpec((c, vw), lambda i: (i, v_col)),
            pl.BlockSpec((c, B_QK_DIM), lambda i: (i, 0)), pl.BlockSpec((c, B_QK_DIM), lambda i: (i, 0)),
            bq(q_col), bq(q_col + 1), pl.BlockSpec((c, vw), lambda i: (nc - 1 - i, v_col)),
            pl.BlockSpec((c, B_QK_DIM), lambda i: (nc - 1 - i, 0)),
            pl.BlockSpec((c, B_QK_DIM), lambda i: (nc - 1 - i, 0)),
            pl.BlockSpec((2 * B_HEADS, 128), lambda i: (0, 0)), st,
        ],
        out_specs=[pl.BlockSpec((c, vw), lambda i: (i, 0)), pl.BlockSpec((c, vw), lambda i: (nc - 1 - i, 0)), st],
        scratch_shapes=[pltpu.VMEM((2, B_HEADS, B_QK_DIM, B_V_DIM), F32)],
        compiler_params=_params(("arbitrary",)),
        name="retention",
    )(proj, proj, proj, cos, sin, proj, proj, proj, cos, sin, lgt, r0)


def _mix_out_kernel(of_ref, ob_ref, bonus_ref, g_ref, yf_ref, yb_ref, gb_ref, ga_ref, gbb_ref, x_ref,
                    alnw, alnb, rlnw, rlnb, npm, npf, g1_ref, sc2_ref, sh2_ref, wa_ref, wb_ref, wo_ref, bd_ref,
                    x1_ref, h2_ref, h3_ref):
    bd = bd_ref[...]
    o = of_ref[...] + ob_ref[...]
    mu = _dot_split_lhs(o, bd) * (1.0 / A_HEAD_DIM)
    oc = o - mu
    var = _dot_split_lhs(oc * oc, bd) * (1.0 / A_HEAD_DIM)
    ya = oc * lax.rsqrt(var + RWKV_GN_EPS) * alnw[...] + alnb[...] + bonus_ref[...]
    ya = _bdot(_bf(ya * g_ref[...]), wa_ref[...])

    y = yf_ref[...] + yb_ref[...]
    parts = []
    for h in range(B_HEADS):
        seg = y[:, h * B_V_DIM:(h + 1) * B_V_DIM]
        m = jnp.mean(seg, axis=-1, keepdims=True)
        sc = seg - m
        vr = jnp.mean(sc * sc, axis=-1, keepdims=True)
        parts.append(sc * lax.rsqrt(vr + RET_GN_EPS))
    yn = jnp.concatenate(parts, axis=1) * rlnw[...] + rlnb[...]
    gb = gb_ref[...]
    yb = _bdot(_bf(yn * (gb * _sigmoid(gb))), wb_ref[...])

    merged = _sigmoid(ga_ref[...]) * ya + _sigmoid(gbb_ref[...]) * yb
    mix = _bdot(_bf(merged), wo_ref[...])
    x1 = x_ref[...] + g1_ref[...] * (_rms(mix) * npm[...])
    x1_ref[...] = x1
    h2 = (_rms(x1) * npf[...]) * (1.0 + sc2_ref[...]) + sh2_ref[...]
    h2_ref[...] = h2
    _rows_to_tiles(h2, h3_ref)


def _mix_out(x2d, proj, o_f, o_b, bonus, g, y_f, y_b, lp, vecs, bd):
    t_len = x2d.shape[0]
    tm = min(t_len, 256)
    aw = A_WIDTH
    d = D_MODEL
    ta = lambda: pl.BlockSpec((tm, aw), lambda i: (i, 0))
    td = lambda: pl.BlockSpec((tm, d), lambda i: (i, 0))
    pc = lambda c: pl.BlockSpec((tm, d), lambda i: (i, c))
    va = lambda: pl.BlockSpec((1, aw), lambda i: (0, 0))
    vd = lambda: pl.BlockSpec((1, d), lambda i: (0, 0))
    full = lambda shp: pl.BlockSpec(shp, lambda i: (0, 0))
    gcol = (SHIFT_PAD + 2 * B_QK_WIDTH + B_V_WIDTH) // d
    return pl.pallas_call(
        _mix_out_kernel,
        out_shape=[jax.ShapeDtypeStruct((t_len, d), F32), jax.ShapeDtypeStruct((t_len, d), F32),
                   jax.ShapeDtypeStruct((t_len * ROW_TILE, LANES), F32)],
        grid=(t_len // tm,),
        in_specs=[ta(), ta(), ta(), ta(), td(), td(), pc(gcol), pc(gcol + 1), pc(gcol + 2), td(),
                  va(), va(), vd(), vd(), vd(), vd(), vd(), vd(), vd(),
                  full((aw, d)), full((d, d)), full((d, d)), full((aw, aw))],
        out_specs=[td(), td(), pl.BlockSpec((tm * ROW_TILE, LANES), lambda i: (i, 0))],
        compiler_params=_params(("arbitrary",)),
        name="mix_out",
    )(o_f, o_b, bonus, g, y_f, y_b, proj, proj, proj, x2d,
      lp["ln_w"], lp["ln_b"], lp["ret_ln_w"], lp["ret_ln_b"], vecs["npm"], vecs["npf"], vecs["g1"], vecs["sc2"],
      vecs["sh2"], lp["w_a"], lp["w_b"], lp["w_o"], bd)


def _router_kernel(h_ref, rw_ref, bias_ref, sel_ref, wts_ref, rank_ref, cnt_ref, cnt_s, *, tm):
    step = pl.program_id(0)

    @pl.when(step == 0)
    def _():
        cnt_s[...] = jnp.zeros_like(cnt_s)

    ne = N_EXPERTS
    scores = _sigmoid(_dot3(h_ref[...], rw_ref[...]))
    work = scores + bias_ref[...]
    lane = lax.broadcasted_iota(jnp.int32, (tm, ne), 1).astype(F32)
    idxs = []
    vals = []
    for _ in range(TOP_K):
        m = jnp.max(work, axis=-1, keepdims=True)
        idx = jnp.min(jnp.where(work == m, lane, float(ne)), axis=-1, keepdims=True)
        oh = lane == idx
        vals.append(jnp.sum(jnp.where(oh, scores, 0.0), axis=-1, keepdims=True))
        idxs.append(idx)
        work = jnp.where(oh, -jnp.inf, work)
    sel_f = jnp.concatenate(idxs, axis=1)
    s_sel = jnp.concatenate(vals, axis=1)
    wts_ref[...] = s_sel / jnp.sum(s_sel, axis=1, keepdims=True) * ROUTED_SCALE
    sel_ref[...] = sel_f.astype(jnp.int32)

    hit = work == -jnp.inf
    r_i = lax.broadcasted_iota(jnp.int32, (tm, tm), 0)
    c_i = lax.broadcasted_iota(jnp.int32, (tm, tm), 1)
    before = _bdot((r_i > c_i).astype(BF16), hit.astype(BF16)) + cnt_s[...]
    ranks = [jnp.sum(jnp.where(lane == idxs[k], before, 0.0), axis=-1, keepdims=True) for k in range(TOP_K)]
    rank_ref[...] = jnp.concatenate(ranks, axis=1).astype(jnp.int32)
    cnt = cnt_s[...] + jnp.sum(hit.astype(F32), axis=0, keepdims=True)
    cnt_s[...] = cnt
    cnt_ref[...] = cnt.astype(jnp.int32)


def _router(h2, router_w, router_bias):
    t_len = h2.shape[0]
    tm = min(t_len, 256)
    kern = functools.partial(_router_kernel, tm=tm)
    tk = lambda: pl.BlockSpec((tm, TOP_K), lambda i: (i, 0))
    return pl.pallas_call(
        kern,
        out_shape=[jax.ShapeDtypeStruct((t_len, TOP_K), jnp.int32), jax.ShapeDtypeStruct((t_len, TOP_K), F32),
                   jax.ShapeDtypeStruct((t_len, TOP_K), jnp.int32), jax.ShapeDtypeStruct((1, N_EXPERTS), jnp.int32)],
        grid=(t_len // tm,),
        in_specs=[pl.BlockSpec((tm, D_MODEL), lambda i: (i, 0)),
                  pl.BlockSpec((D_MODEL, N_EXPERTS), lambda i: (0, 0)),
                  pl.BlockSpec((1, N_EXPERTS), lambda i: (0, 0))],
        out_specs=[tk(), tk(), tk(), pl.BlockSpec((1, N_EXPERTS), lambda i: (0, 0))],
        scratch_shapes=[pltpu.VMEM((1, N_EXPERTS), F32)],
        compiler_params=_params(("arbitrary",)),
        name="router",
    )(h2, router_w, router_bias.reshape(1, N_EXPERTS))


def _dispatch_kernel(slot_ref, start_ref, cnt_ref, h_ref, xs_out, zrow, sem, zsem, *, tm):
    def row_copy(t, k):
        return pltpu.make_async_copy(_row_tile(h_ref, t), _row_tile(xs_out, slot_ref[t * TOP_K + k]), sem)

    def issue(t, carry):
        for k in range(TOP_K):
            row_copy(t, k).start()
        return carry

    lax.fori_loop(0, tm, issue, 0)

    def drain(t, carry):
        for k in range(TOP_K):
            row_copy(t, k).wait()
        return carry

    lax.fori_loop(0, tm, drain, 0)

    @pl.when(pl.program_id(0) == pl.num_programs(0) - 1)
    def _():
        zrow[...] = jnp.zeros_like(zrow)

        def per_expert(e, carry):
            n = cnt_ref[e]
            base = start_ref[e]
            n_pad = (n + MOE_BLOCK - 1) // MOE_BLOCK * MOE_BLOCK

            def zero_copy(j):
                return pltpu.make_async_copy(zrow, _row_tile(xs_out, base + j), zsem)

            def z_issue(j, c):
                zero_copy(j).start()
                return c

            def z_drain(j, c):
                zero_copy(j).wait()
                return c

            lax.fori_loop(n, n_pad, z_issue, 0)
            lax.fori_loop(n, n_pad, z_drain, 0)
            return carry

        lax.fori_loop(0, N_EXPERTS, per_expert, 0)


def _dispatch(h3, slots, pad_start, counts, n_slots):
    t_len = h3.shape[0] // ROW_TILE
    tm = min(t_len, 256)
    kern = functools.partial(_dispatch_kernel, tm=tm)
    smem_blk = lambda: pl.BlockSpec((tm * TOP_K,), lambda i: (i,), memory_space=pltpu.SMEM)
    smem_all = lambda: pl.BlockSpec((N_EXPERTS,), lambda i: (0,), memory_space=pltpu.SMEM)
    return pl.pallas_call(
        kern,
        out_shape=jax.ShapeDtypeStruct((n_slots * ROW_TILE, LANES), F32),
        grid=(t_len // tm,),
        in_specs=[smem_blk(), smem_all(), smem_all(),
                  pl.BlockSpec((tm * ROW_TILE, LANES), lambda i: (i, 0))],
        out_specs=pl.BlockSpec(memory_space=pl.ANY),
        scratch_shapes=[pltpu.VMEM((ROW_TILE, LANES), F32), pltpu.SemaphoreType.DMA(()),
                        pltpu.SemaphoreType.DMA(())],
        compiler_params=_params(("arbitrary",)),
        name="dispatch",
    )(slots, pad_start, counts, h3)


def _expert_kernel(be_ref, nu_ref, xs_ref, wg_ref, wu_ref, wd_ref, ys_ref, wg_s, wu_s, wd_s):
    b = pl.program_id(0)
    used = b < nu_ref[0]
    new_expert = (b == 0) | (be_ref[b] != be_ref[jnp.maximum(b - 1, 0)])

    @pl.when(used & new_expert)
    def _():
        wg_s[...] = _bf(wg_ref[0])
        wu_s[...] = _bf(wu_ref[0])
        wd_s[...] = _bf(wd_ref[0])

    @pl.when(used)
    def _():
        xb = _bf(_tiles_to_rows(xs_ref, MOE_BLOCK))
        gate = _bdot(xb, wg_s[...])
        up = _bdot(xb, wu_s[...])
        act = gate * _sigmoid(gate) * up
        _rows_to_tiles(_bdot(_bf(act), wd_s[...]), ys_ref)

    @pl.when(jnp.logical_not(used))
    def _():
        ys_ref[...] = jnp.zeros_like(ys_ref)


def _experts(xs, block_e, n_used, w_gate, w_up, w_down):
    n_slots = xs.shape[0] // ROW_TILE
    n_blocks = n_slots // MOE_BLOCK
    rows = (MOE_BLOCK * ROW_TILE, LANES)
    grid_spec = pltpu.PrefetchScalarGridSpec(
        num_scalar_prefetch=2,
        grid=(n_blocks,),
        in_specs=[
            pl.BlockSpec(rows, lambda b, be, nu: (jnp.where(b < nu[0], b, 0), 0)),
            pl.BlockSpec((1, D_MODEL, EXPERT_FF), lambda b, be, nu: (be[b], 0, 0)),
            pl.BlockSpec((1, D_MODEL, EXPERT_FF), lambda b, be, nu: (be[b], 0, 0)),
            pl.BlockSpec((1, EXPERT_FF, D_MODEL), lambda b, be, nu: (be[b], 0, 0)),
        ],
        out_specs=pl.BlockSpec(rows, lambda b, be, nu: (b, 0)),
        scratch_shapes=[pltpu.VMEM((D_MODEL, EXPERT_FF), BF16), pltpu.VMEM((D_MODEL, EXPERT_FF), BF16),
                        pltpu.VMEM((EXPERT_FF, D_MODEL), BF16)],
    )
    return pl.pallas_call(
        _expert_kernel,
        out_shape=jax.ShapeDtypeStruct((n_slots * ROW_TILE, LANES), F32),
        grid_spec=grid_spec,
        compiler_params=_params(("arbitrary",)),
        name="experts",
    )(block_e, n_used, xs, w_gate, w_up, w_down)


def _combine_kernel(slot_ref, ys_ref, wts_ref, h_ref, x1_ref, sg_ref, su_ref, sd_ref, npo, g2_ref,
                    o_ref, buf, sem, *, tm):
    def row_copy(t, k):
        return pltpu.make_async_copy(_row_tile(ys_ref, slot_ref[t * TOP_K + k]), _row_tile(buf, t, (k,)), sem)

    def issue(t, carry):
        for k in range(TOP_K):
            row_copy(t, k).start()
        return carry

    lax.fori_loop(0, tm, issue, 0)

    hb = _bf(h_ref[...])
    gate = _bdot(hb, sg_ref[...])
    up = _bdot(hb, su_ref[...])
    shared = _bdot(_bf(gate * _sigmoid(gate) * up), sd_ref[...])

    def drain(t, carry):
        for k in range(TOP_K):
            row_copy(t, k).wait()
        return carry

    lax.fori_loop(0, tm, drain, 0)

    wts = wts_ref[...]
    routed = _tiles_to_rows(buf, tm, (0,)) * wts[:, 0:1]
    for k in range(1, TOP_K):
        routed = routed + _tiles_to_rows(buf, tm, (k,)) * wts[:, k:k + 1]
    o_ref[...] = x1_ref[...] + g2_ref[...] * (_rms(routed + shared) * npo[...])


def _combine(ys, slots, wts, h2, x1, sg, su, sd, npo, g2):
    t_len = h2.shape[0]
    tm = min(t_len, 256)
    d = D_MODEL
    kern = functools.partial(_combine_kernel, tm=tm)
    smem_blk = lambda: pl.BlockSpec((tm * TOP_K,), lambda i: (i,), memory_space=pltpu.SMEM)
    td = lambda: pl.BlockSpec((tm, d), lambda i: (i, 0))
    vd = lambda: pl.BlockSpec((1, d), lambda i: (0, 0))
    return pl.pallas_call(
        kern,
        out_shape=jax.ShapeDtypeStruct((t_len, d), F32),
        grid=(t_len // tm,),
        in_specs=[smem_blk(),
                  pl.BlockSpec(memory_space=pl.ANY),
                  pl.BlockSpec((tm, TOP_K), lambda i: (i, 0)),
                  td(), td(),
                  pl.BlockSpec((d, SHARED_FF), lambda i: (0, 0)), pl.BlockSpec((d, SHARED_FF), lambda i: (0, 0)),
                  pl.BlockSpec((SHARED_FF, d), lambda i: (0, 0)), vd(), vd()],
        out_specs=td(),
        scratch_shapes=[pltpu.VMEM((TOP_K, tm * ROW_TILE, LANES), F32), pltpu.SemaphoreType.DMA(())],
        compiler_params=_params(("arbitrary",)),
        name="combine",
    )(slots, ys, wts, h2, x1, sg, su, sd, npo, g2)


def _block_diag_ones(width, group):
    idx = np.arange(width) // group
    return jnp.asarray(idx[:, None] == idx[None, :], dtype=BF16)


def _rope_tables(t_len):
    pos = np.arange(t_len)
    rows = (pos // GRID_W).astype(np.float32)
    cols = (pos % GRID_W).astype(np.float32)
    quarter = B_QK_DIM // 4
    inv_freq = jnp.asarray(ROPE_BASE, F32) ** (-jnp.arange(quarter, dtype=F32) / quarter)
    ang_r = jnp.asarray(rows)[:, None] * inv_freq
    ang_c = jnp.asarray(cols)[:, None] * inv_freq
    cr, sr, cc, sc = jnp.cos(ang_r), jnp.sin(ang_r), jnp.cos(ang_c), jnp.sin(ang_c)
    return jnp.concatenate([cr, cr, cc, cc], axis=1), jnp.concatenate([-sr, sr, -sc, sc], axis=1)


def _token_mixer(x2d, g_pre, sc, sh, lp, cos, sin, states, bd):
    proj = _in_proj(x2d, g_pre, sc, sh, lp["w_in"], lp["mu"])
    prep = _rwkv_prep(proj, lp, bd)
    o_f, o_b, s_fin = _rwkv_scan(proj, prep, states[0])
    y_f, y_b, r_fin = _retention(proj, cos, sin, lp["lgt"], states[1])
    return proj, prep, (o_f, o_b), (y_f, y_b), (s_fin, r_fin)


def kernel(x, c, ctx, c_ctx, w_mod, b_mod, norm_pre_mix, norm_post_mix, norm_pre_ffn, norm_post_ffn, w_in, shift_mu,
           rwkv_w0, rwkv_w_up, rwkv_a0, rwkv_a_up, rwkv_g_up, rwkv_k_k, rwkv_k_a, rwkv_r_k, rwkv_ln_w, rwkv_ln_b,
           w_branch_a, ret_decay_logit, ret_ln_w, ret_ln_b, w_branch_b, w_out, router_w, router_bias, exp_w_gate,
           exp_w_up, exp_w_down, sh_w_gate, sh_w_up, sh_w_down):
    d = D_MODEL
    assert x.shape[0] == 1 and w_in.shape[0] == 1, "single batch element, single layer"
    t_len = x.shape[1]
    x2d = x.reshape(t_len, d)
    ctx2d = ctx.reshape(ctx.shape[1], d)
    row = lambda a: a.reshape(1, -1)

    cs = jnp.zeros((8, d), F32).at[0].set(c[0]).at[1].set(c_ctx)
    mod = _modulation(cs, w_mod[0], b_mod[0])
    sh1, sc1, g1, sh2, sc2, g2 = [mod[0:1, i * d:(i + 1) * d] for i in range(6)]
    csh1, csc1 = mod[1:2, 0:d], mod[1:2, d:2 * d]

    w_in_p = jnp.concatenate(
        [w_in[0][:, :SHIFT_COLS], jnp.zeros((d, SHIFT_PAD - SHIFT_COLS), F32), w_in[0][:, SHIFT_COLS:]], axis=1)
    lp = {
        "w_in": w_in_p.astype(BF16),
        "mu": jnp.pad(shift_mu[0], ((0, 0), (0, SHIFT_PAD - SHIFT_COLS))),
        "k_k": row(rwkv_k_k[0]), "k_a": row(rwkv_k_a[0]), "r_k": row(rwkv_r_k[0]),
        "w0": rwkv_w0[0], "w_up": rwkv_w_up[0], "a0": rwkv_a0[0], "a_up": rwkv_a_up[0], "g_up": rwkv_g_up[0],
        "ln_w": row(rwkv_ln_w[0]), "ln_b": row(rwkv_ln_b[0]),
        "ret_ln_w": row(ret_ln_w[0]), "ret_ln_b": row(ret_ln_b[0]),
        "lgt": jnp.broadcast_to(ret_decay_logit[0].reshape(2 * B_HEADS, 1), (2 * B_HEADS, 128)),
        "w_a": w_branch_a[0].astype(BF16), "w_b": w_branch_b[0].astype(BF16), "w_o": w_out[0].astype(BF16),
    }
    bd = _block_diag_ones(A_WIDTH, A_HEAD_DIM)
    g_pre = row(norm_pre_mix[0])

    t_ctx = ctx2d.shape[0]
    zero_states = (jnp.zeros((2, A_HEADS, A_HEAD_DIM, A_HEAD_DIM), F32),
                   jnp.zeros((2, B_HEADS, B_QK_DIM, B_V_DIM), F32))
    ones = jnp.ones((t_ctx, B_QK_DIM), F32)
    *_, ctx_states = _token_mixer(ctx2d, g_pre, csc1, csh1, lp, ones, jnp.zeros_like(ones), zero_states, bd)

    cos, sin = _rope_tables(t_len)
    proj, prep, (o_f, o_b), (y_f, y_b), _ = _token_mixer(x2d, g_pre, sc1, sh1, lp, cos, sin, ctx_states, bd)
    vecs = {"npm": row(norm_post_mix[0]), "npf": row(norm_pre_ffn[0]), "g1": g1, "sc2": sc2, "sh2": sh2}
    x1, h2, h3 = _mix_out(x2d, proj, o_f, o_b, prep[7], prep[8], y_f, y_b, lp, vecs, bd)

    sel, wts, rank, counts = _router(h2, router_w[0], router_bias[0])
    counts = counts.reshape(N_EXPERTS)
    padded = (counts + MOE_BLOCK - 1) // MOE_BLOCK * MOE_BLOCK
    pad_end = jnp.cumsum(padded)
    pad_start = (pad_end - padded).astype(jnp.int32)
    n_assign = t_len * TOP_K
    n_blocks = (n_assign + N_EXPERTS * (MOE_BLOCK - 1) + MOE_BLOCK - 1) // MOE_BLOCK
    block_start = jnp.arange(n_blocks, dtype=jnp.int32) * MOE_BLOCK
    block_e = jnp.minimum(jnp.sum(pad_end[None, :] <= block_start[:, None], axis=1), N_EXPERTS - 1).astype(jnp.int32)
    n_used = (pad_end[-1:] // MOE_BLOCK).astype(jnp.int32)
    slots = (jnp.take(pad_start, sel) + rank).reshape(n_assign)

    xs = _dispatch(h3, slots, pad_start, counts, n_blocks * MOE_BLOCK)
    ys = _experts(xs, block_e, n_used, exp_w_gate[0], exp_w_up[0], exp_w_down[0])
    out = _combine(ys, slots, wts, h2, x1, sh_w_gate[0].astype(BF16),
                   sh_w_up[0].astype(BF16), sh_w_down[0].astype(BF16), row(norm_post_ffn[0]), g2)
    return out.reshape(x.shape)
```

```python
import functools

import jax
import jax.numpy as jnp
import numpy as np
from jax import lax
from jax.experimental import pallas as pl
from jax.experimental.pallas import tpu as pltpu

F32 = jnp.float32
BF16 = jnp.bfloat16

D_MODEL = 1024
GRID_W = 64
NORM_EPS = 1e-6

A_HEAD_DIM = 64
A_WIDTH = D_MODEL // 2
A_HEADS = A_WIDTH // A_HEAD_DIM
W_LORA = 64
ICLR_LORA = 64
G_LORA = 128
RWKV_GN_EPS = 64e-5

B_HEADS = 4
B_QK_WIDTH = D_MODEL // 2
B_V_WIDTH = D_MODEL
B_QK_DIM = B_QK_WIDTH // B_HEADS
B_V_DIM = B_V_WIDTH // B_HEADS
RET_CHUNK = 128
RET_GN_EPS = 1e-5
ROPE_BASE = 10000.0

SHIFT_COLS = 3 * A_WIDTH + W_LORA + ICLR_LORA + G_LORA
LORA_COLS = W_LORA + ICLR_LORA + G_LORA
SHIFT_PAD = 2048
PROJ_COLS = SHIFT_PAD + 2 * B_QK_WIDTH + 2 * B_V_WIDTH + 2 * D_MODEL

N_EXPERTS = 256
TOP_K = 8
EXPERT_FF = D_MODEL // 4
SHARED_FF = D_MODEL // 4
ROUTED_SCALE = 2.5
MOE_BLOCK = 128
ZERO_PAD_GROUP = 16

RWKV_CHUNK = 64
RWKV_STATE_SHAPE = (2, A_HEADS // 2, A_HEAD_DIM, 2 * A_HEAD_DIM)
PROJ_TN = 1024
VMEM_LIMIT = 48 * 1024 * 1024


def _params(sem):
    return pltpu.CompilerParams(dimension_semantics=sem, vmem_limit_bytes=VMEM_LIMIT)


def _bf(a):
    return a.astype(BF16)


def _bdot(a, b):
    return jnp.dot(a, b, preferred_element_type=F32)


def _dot_nt(a, b):
    return lax.dot_general(a, b, (((1,), (1,)), ((), ())), preferred_element_type=F32)


def _dot_tn(a, b):
    return lax.dot_general(a, b, (((0,), (0,)), ((), ())), preferred_element_type=F32)


def _split(a, n):
    out = []
    rem = a
    for _ in range(n):
        p = _bf(rem)
        out.append(p)
        rem = rem - p.astype(F32)
    return out


def _dot_split_lhs(a, b_bf, n=3):
    acc = None
    for p in _split(a, n):
        t = _bdot(p, b_bf)
        acc = t if acc is None else acc + t
    return acc


def _dot_split_rhs(a_bf, b, n=3):
    acc = None
    for p in _split(b, n):
        t = _bdot(a_bf, p)
        acc = t if acc is None else acc + t
    return acc


def _dot3(a, b):
    ah, al = _split(a, 2)
    bh, bl = _split(b, 2)
    return _bdot(ah, bh) + (_bdot(ah, bl) + _bdot(al, bh))


def _sigmoid(x):
    return 1.0 / (1.0 + jnp.exp(-x))


def _softplus(x):
    return jnp.maximum(x, 0.0) + jnp.log1p(jnp.exp(-jnp.abs(x)))


def _rms(x):
    return x * lax.rsqrt(jnp.mean(x * x, axis=-1, keepdims=True) + NORM_EPS)


LANES = 128
ROW_TILE = D_MODEL // LANES


def _rows_to_tiles(x2d, ref, lead=()):
    n = x2d.shape[0]
    for j in range(ROW_TILE):
        ref[lead + (pl.ds(j, n, stride=ROW_TILE), slice(None))] = x2d[:, j * LANES:(j + 1) * LANES]


def _tiles_to_rows(ref, n, lead=()):
    return jnp.concatenate(
        [ref[lead + (pl.ds(j, n, stride=ROW_TILE), slice(None))] for j in range(ROW_TILE)], axis=1)


def _row_tile(ref, r, lead=()):
    return ref.at[lead + (pl.ds(pl.multiple_of(r * ROW_TILE, ROW_TILE), ROW_TILE),)]


def _mod_kernel(cs_ref, w_ref, b_ref, o_ref):
    cs = cs_ref[...]
    s = cs * _sigmoid(cs)
    o_ref[...] = _dot3(s, w_ref[...]) + b_ref[...]


def _modulation(cs, w_mod, b_mod):
    n_out = w_mod.shape[1]
    tn = 1536
    return pl.pallas_call(
        _mod_kernel,
        out_shape=jax.ShapeDtypeStruct((8, n_out), F32),
        grid=(n_out // tn,),
        in_specs=[
            pl.BlockSpec((8, D_MODEL), lambda j: (0, 0)),
            pl.BlockSpec((D_MODEL, tn), lambda j: (0, j)),
            pl.BlockSpec((1, tn), lambda j: (0, j)),
        ],
        out_specs=pl.BlockSpec((8, tn), lambda j: (0, j)),
        compiler_params=_params(("arbitrary",)),
        name="modulation",
    )(cs, w_mod, b_mod.reshape(1, n_out))


def _in_proj_kernel(x_ref, xp_ref, xn_ref, g_ref, sc_ref, sh_ref, w_ref, mu_ref, o_ref, h_s, hh_s, *, tm, n_shift):
    i = pl.program_id(0)
    j = pl.program_id(1)
    n_i = pl.num_programs(0)

    def norm_mod(xv):
        return (_rms(xv) * g_ref[...]) * (1.0 + sc_ref[...]) + sh_ref[...]

    @pl.when(j == 0)
    def _():
        h_s[...] = _bf(norm_mod(x_ref[...]))
        hp = jnp.where(i > 0, norm_mod(xp_ref[...]), 0.0)
        hn = jnp.where(i < n_i - 1, norm_mod(xn_ref[...]), 0.0)
        hh_s[0:8, :] = hp
        hh_s[8:16, :] = hn

    w = w_ref[...]
    p = _bdot(h_s[...], w)

    @pl.when(j < n_shift)
    def _():
        ph = _bdot(_bf(hh_s[...]), w)
        row = lax.broadcasted_iota(jnp.int32, p.shape, 0)
        prev = jnp.where(row == 0, ph[7:8, :], pltpu.roll(p, 1, 0))
        nxt = jnp.where(row == tm - 1, ph[8:9, :], pltpu.roll(p, tm - 1, 0))
        mu = mu_ref[...]
        o_ref[...] = p + mu[0:1, :] * (prev - p) + mu[1:2, :] * (nxt - p)

    @pl.when(j >= n_shift)
    def _():
        o_ref[...] = p


def _in_proj(x2d, g, sc, sh, w_bf, mu_pad):
    t_len = x2d.shape[0]
    tm = min(t_len, 1024)
    tn = PROJ_TN
    n_shift = SHIFT_PAD // tn
    tb8 = tm // 8
    nb8 = t_len // 8
    kern = functools.partial(_in_proj_kernel, tm=tm, n_shift=n_shift)
    vec = lambda: pl.BlockSpec((1, D_MODEL), lambda i, j: (0, 0))
    return pl.pallas_call(
        kern,
        out_shape=jax.ShapeDtypeStruct((t_len, PROJ_COLS), F32),
        grid=(t_len // tm, PROJ_COLS // tn),
        in_specs=[
            pl.BlockSpec((tm, D_MODEL), lambda i, j: (i, 0)),
            pl.BlockSpec((8, D_MODEL), lambda i, j: (jnp.maximum(i * tb8 - 1, 0), 0)),
            pl.BlockSpec((8, D_MODEL), lambda i, j: (jnp.minimum((i + 1) * tb8, nb8 - 1), 0)),
            vec(), vec(), vec(),
            pl.BlockSpec((D_MODEL, tn), lambda i, j: (0, j)),
            pl.BlockSpec((2, tn), lambda i, j: (0, jnp.minimum(j, n_shift - 1))),
        ],
        out_specs=pl.BlockSpec((tm, tn), lambda i, j: (i, j)),
        scratch_shapes=[pltpu.VMEM((tm, D_MODEL), BF16), pltpu.VMEM((16, D_MODEL), F32)],
        compiler_params=_params(("arbitrary", "arbitrary")),
        name="in_proj",
    )(x2d, x2d, x2d, g, sc, sh, w_bf, mu_pad)


def _rwkv_prep_kernel(r_ref, k_ref, v_ref, lora_ref, kk_w, ka_w, rk_w, w0_ref, wup_ref, a0_ref, aup_ref, gup_ref,
                      bd_ref, kk_o, lw0_o, lw1_o, kd0_o, kd1_o, b0_o, b1_o, bonus_o, g_o):
    r = r_ref[...]
    k = k_ref[...]
    v = v_ref[...]
    lora = lora_ref[...]
    xw = lora[:, 0:W_LORA]
    xa = lora[:, W_LORA:W_LORA + ICLR_LORA]
    xg = lora[:, W_LORA + ICLR_LORA:]
    bd = bd_ref[...]

    kk = k * kk_w[...]
    nrm = jnp.sqrt(_dot_split_lhs(kk * kk, bd))
    kk = kk / jnp.maximum(nrm, 1e-12)
    kk_o[...] = kk

    tw = jnp.tanh(xw)
    lw_outs = (lw0_o, lw1_o)
    kd_outs = (kd0_o, kd1_o)
    b_outs = (b0_o, b1_o)
    kd_sum = None
    for d in range(2):
        z = w0_ref[d:d + 1, :] + _dot3(tw, wup_ref[d])
        w_log = -_softplus(-z) - 0.5
        lw_outs[d][...] = -jnp.exp(w_log)
        a_d = _sigmoid(a0_ref[d:d + 1, :] + _dot3(xa, aup_ref[d]))
        kd = k * (1.0 + (a_d - 1.0) * ka_w[...])
        kd_outs[d][...] = kd
        b_outs[d][...] = kk * a_d
        kd_sum = kd if kd_sum is None else kd_sum + kd
    g_o[...] = _dot3(_sigmoid(xg), gup_ref[...])
    bonus_o[...] = _dot_split_lhs(r * kd_sum * rk_w[...], bd) * v


def _rwkv_prep(proj, lp, bd):
    t_len = proj.shape[0]
    tm = min(t_len, 512)
    aw = A_WIDTH
    col = lambda c: pl.BlockSpec((tm, aw), lambda i: (i, c))
    vec = lambda: pl.BlockSpec((1, aw), lambda i: (0, 0))
    full = lambda shp: pl.BlockSpec(shp, lambda i: (0,) * len(shp))
    outs = [jax.ShapeDtypeStruct((t_len, aw), F32)] * 9
    return pl.pallas_call(
        _rwkv_prep_kernel,
        out_shape=outs,
        grid=(t_len // tm,),
        in_specs=[
            col(0), col(1), col(2),
            pl.BlockSpec((tm, LORA_COLS), lambda i: (i, 3 * aw // LORA_COLS)),
            vec(), vec(), vec(),
            full((2, aw)), full((2, W_LORA, aw)), full((2, aw)), full((2, ICLR_LORA, aw)), full((G_LORA, aw)),
            full((aw, aw)),
        ],
        out_specs=[pl.BlockSpec((tm, aw), lambda i: (i, 0))] * 9,
        compiler_params=_params(("arbitrary",)),
        name="rwkv_prep",
    )(proj, proj, proj, proj, lp["k_k"], lp["k_a"], lp["r_k"], lp["w0"], lp["w_up"], lp["a0"], lp["a_up"],
      lp["g_up"], bd)


def _rwkv_scan_kernel(rf, vf, kkf, lwf, kdf, bf_, rb, vb, kkb, lwb, kdb, bb, s0_ref, of_ref, ob_ref, sfin_ref, s_s,
                      *, tb):
    step = pl.program_id(0)

    @pl.when(step == 0)
    def _():
        s_s[...] = s0_ref[...]

    c = RWKV_CHUNK
    n = A_HEAD_DIM
    assert c == n
    pw = 2 * n
    row = lax.broadcasted_iota(jnp.int32, (c, pw), 0)
    lane = lax.broadcasted_iota(jnp.int32, (c, pw), 1)
    col = lane % c
    first = lane < n
    eye = (row == col).astype(F32)
    r_c = lax.broadcasted_iota(jnp.int32, (c, c), 0)
    c_c = lax.broadcasted_iota(jnp.int32, (c, c), 1)
    n_ch = tb // c
    dirs = ((rf, vf, kkf, lwf, kdf, bf_, of_ref), (rb, vb, kkb, lwb, kdb, bb, ob_ref))
    pairs = range(A_HEADS // 2)
    psl = [slice(q * pw, (q + 1) * pw) for q in pairs]

    def bd(x):
        z = jnp.zeros_like(x)
        return jnp.concatenate([jnp.where(first, x, z), jnp.where(first, z, x)], axis=0)

    def chunk_terms(d, refs, ci):
        r_ref, v_ref, kk_ref, lw_ref, kd_ref, b_ref, _ = refs
        if d == 0:
            incl, strict, tri = row >= col, row > col, r_c >= c_c
        else:
            incl, strict, tri = row <= col, row < col, r_c <= c_c
        rows = slice(ci * c, (ci + 1) * c)
        lw = lw_ref[rows, :]
        r = r_ref[rows, :]
        kk = kk_ref[rows, :]
        kd = kd_ref[rows, :]
        b = b_ref[rows, :]
        cl = _dot_split_rhs(tri.astype(BF16), lw)
        cl_tot = cl[c - 1:c, :] if d == 0 else cl[0:1, :]
        e_neg = jnp.exp(-cl)
        e_end = jnp.exp(cl_tot - cl)
        g_tot = jnp.exp(cl_tot)
        rt = r * jnp.exp(cl)
        x_all = _bf(jnp.concatenate([-(kk * jnp.exp(cl - lw)), rt], axis=0))
        bt_all = _bf(b * e_neg)
        kt_all = _bf(kd * e_neg)
        return {
            "rows": rows, "strict": strict, "incl": incl,
            "x": [x_all[:, ps] for ps in psl],
            "z": [jnp.concatenate([bd(bt_all[:, ps]), bd(kt_all[:, ps])], axis=0) for ps in psl],
            "bd_bh": [bd(_bf(b * e_end)[:, ps]) for ps in psl],
            "bd_kh": [bd(_bf(kd * e_end)[:, ps]) for ps in psl],
            "bd_v": [bd(_bf(v_ref[rows, :])[:, ps]) for ps in psl],
            "rt": [rt[:, ps] for ps in psl],
            "g_tot": [g_tot[:, ps] for ps in psl],
        }

    chunks = [(d, cc) for cc in range(n_ch) for d in range(2)]
    prep = {(d, cc): chunk_terms(d, dirs[d], cc if d == 0 else n_ch - 1 - cc) for d, cc in chunks}
    units = [(k, q) for k in chunks for q in pairs]

    def per_unit(name, indexed=True):
        return [prep[k][name][q] if indexed else prep[k][name] for k, q in units]

    strict, incl = per_unit("strict", False), per_unit("incl", False)
    x_u, bd_bh, bd_kh, bd_v, rt_u = (per_unit(s) for s in ("x", "bd_bh", "bd_kh", "bd_v", "rt"))
    g = [_dot_nt(x, z) for x, z in zip(x_u, per_unit("z"))]
    a_ab = [jnp.where(s, gi[:c, :pw], 0.0) for s, gi in zip(strict, g)]
    a_rb = [_bf(jnp.where(i, gi[c:, :pw], 0.0)) for i, gi in zip(incl, g)]
    a_k = [_bf(jnp.concatenate([jnp.where(s, gi[:c, pw:], 0.0), jnp.where(i, gi[c:, pw:], 0.0)], axis=0))
           for s, i, gi in zip(strict, incl, g)]
    a2 = [_bdot(_bf(a), bd(_bf(a))) for a in a_ab]
    pa = [jnp.concatenate([eye + a, sq], axis=0) for a, sq in zip(a_ab, a2)]
    for _ in range(4):
        nxt = [_bdot(_bf(x), bd(_bf(x[c:]))) for x in pa]
        pa = [jnp.concatenate([x[:c] + y[:c], y[c:]], axis=0) for x, y in zip(pa, nxt)]
    p = [_bf(x[:c] + _bdot(_bf(x[:c]), bd(_bf(x[c:])))) for x in pa]
    vk = [_bdot(a, v) for a, v in zip(a_k, bd_v)]
    tw = [_bf(_bdot(pi, jnp.concatenate([bd(x[:c]), bd(_bf(vki[:c]))], axis=1)))
          for pi, vki, x in zip(p, vk, x_u)]
    bd_at = [bd(t[:, :pw]) for t in tw]
    bd_w = [bd(t[:, pw:]) for t in tw]
    mt = [_bf(_dot_tn(a, bh)) for a, bh in zip(bd_at, bd_bh)]
    ntf = [_dot_tn(jnp.concatenate([w, v], axis=0), jnp.concatenate([bh, kh], axis=0))
           for w, v, bh, kh in zip(bd_w, bd_v, bd_bh, bd_kh)]
    nt = [m[:n] + m[n:] for m in ntf]
    rw = [_bdot(a, jnp.concatenate([at, w], axis=1)) for a, at, w in zip(a_rb, bd_at, bd_w)]
    ry = [_bf(r + w[:, :pw]) for w, r in zip(rw, rt_u)]
    y0 = [w[:, pw:] + vki[c:] for w, vki in zip(rw, vk)]
    term = {u: vals for u, vals in zip(units, zip(mt, nt, ry, y0, per_unit("g_tot")))}

    state = [[s_s[d, q] for q in pairs] for d in range(2)]
    dq = [(d, q) for d in range(2) for q in pairs]
    for cc in range(n_ch):
        mt_c, nt_c, ry_c, y0_c, gt_c = zip(*[term[((d, cc), q)] for d, q in dq])
        s0 = [state[d][q] for d, q in dq]
        ys = [_dot_nt(ry_c[i], bd(_bf(s0[i]))) + y0_c[i] for i in range(len(dq))]
        upd = [_bdot(jnp.concatenate(_split(s0[i], 2), axis=0), mt_c[i]) for i in range(len(dq))]
        for i, (d, q) in enumerate(dq):
            state[d][q] = s0[i] * gt_c[i] + (upd[i][:n] + upd[i][n:]) + nt_c[i]
        for d in range(2):
            y_d = [ys[i] for i, (dd, _) in enumerate(dq) if dd == d]
            dirs[d][6][prep[(d, cc)]["rows"], :] = jnp.concatenate(y_d, axis=1)
    for d in range(2):
        for q in pairs:
            s_s[d, q] = state[d][q]

    @pl.when(step == pl.num_programs(0) - 1)
    def _():
        sfin_ref[...] = s_s[...]


def _rwkv_scan(proj, prep, s0):
    kk, lw0, lw1, kd0, kd1, b0, b1 = prep[:7]
    t_len = proj.shape[0]
    tb = 2 * RWKV_CHUNK
    nb = t_len // tb
    aw = A_WIDTH
    fwd = lambda c: pl.BlockSpec((tb, aw), lambda i: (i, c))
    bwd = lambda c: pl.BlockSpec((tb, aw), lambda i: (nb - 1 - i, c))
    st = pl.BlockSpec(RWKV_STATE_SHAPE, lambda i: (0, 0, 0, 0))
    kern = functools.partial(_rwkv_scan_kernel, tb=tb)
    return pl.pallas_call(
        kern,
        out_shape=[jax.ShapeDtypeStruct((t_len, aw), F32), jax.ShapeDtypeStruct((t_len, aw), F32),
                   jax.ShapeDtypeStruct(RWKV_STATE_SHAPE, F32)],
        grid=(nb,),
        in_specs=[fwd(0), fwd(2), fwd(0), fwd(0), fwd(0), fwd(0),
                  bwd(0), bwd(2), bwd(0), bwd(0), bwd(0), bwd(0), st],
        out_specs=[fwd(0), bwd(0), st],
        scratch_shapes=[pltpu.VMEM(RWKV_STATE_SHAPE, F32)],
        compiler_params=_params(("arbitrary",)),
        name="rwkv_scan",
    )(proj, proj, kk, lw0, kd0, b0, proj, proj, kk, lw1, kd1, b1, s0)


def _retention_kernel(qf, kf, vf, cosf, sinf, qb, kb, vb, cosb, sinb, lgt_ref, r0_ref, yf_ref, yb_ref, rfin_ref, r_s):
    step = pl.program_id(0)

    @pl.when(step == 0)
    def _():
        r_s[...] = r0_ref[...]

    c = RET_CHUNK
    dk = B_QK_DIM
    dv = B_V_DIM
    lg_all = -_softplus(-lgt_ref[...])
    rowf = lax.broadcasted_iota(jnp.int32, (c, c), 0).astype(F32)
    colf = lax.broadcasted_iota(jnp.int32, (c, c), 1).astype(F32)
    lane = lax.broadcasted_iota(jnp.int32, (c, dk), 1)
    first_half = (lane % 64) < 32
    dirs = ((qf, kf, vf, cosf, sinf, yf_ref), (qb, kb, vb, cosb, sinb, yb_ref))
    for d, (q_ref, k_ref, v_ref, cos_ref, sin_ref, y_ref) in enumerate(dirs):
        cos = cos_ref[...]
        sin = sin_ref[...]
        diff = (rowf - colf) if d == 0 else (colf - rowf)
        pos = rowf if d == 0 else (c - 1.0) - rowf
        for h in range(B_HEADS):
            lg = lg_all[d * B_HEADS + h:d * B_HEADS + h + 1, :]
            qh = q_ref[:, h * dk:(h + 1) * dk]
            kh = k_ref[:, h * dk:(h + 1) * dk] * (dk ** -0.5)

            def rope(xv):
                swapped = jnp.where(first_half, pltpu.roll(xv, dk - 32, 1), pltpu.roll(xv, 32, 1))
                return xv * cos + swapped * sin

            qh = rope(qh)
            kh = rope(kh)
            vh = _bf(v_ref[:, h * dv:(h + 1) * dv])
            dmask = jnp.where(diff >= 0.0, jnp.exp(lg * jnp.maximum(diff, 0.0)), 0.0)
            scores = _dot_nt(_bf(qh), _bf(kh)) * dmask
            inner = _bdot(_bf(scores), vh)
            xi = jnp.exp(lg * (pos + 1.0))
            zeta = jnp.exp(lg * ((c - 1.0) - pos))
            r_prev = r_s[d, h]
            cross = _bdot(_bf(qh * xi), _bf(r_prev))
            y_ref[:, h * dv:(h + 1) * dv] = inner + cross
            kv = _dot_tn(_bf(kh * zeta), vh)
            g_chunk = jnp.exp(lg * float(c))
            r_s[d, h] = jnp.concatenate([g_chunk, g_chunk], axis=1) * r_prev + kv

    @pl.when(step == pl.num_programs(0) - 1)
    def _():
        rfin_ref[...] = r_s[...]


def _retention(proj, cos, sin, lgt, r0):
    t_len = proj.shape[0]
    c = RET_CHUNK
    nc = t_len // c
    qw = B_QK_WIDTH
    vw = B_V_WIDTH
    fq = lambda col: pl.BlockSpec((c, qw), lambda i: (i, col))
    bq = lambda col: pl.BlockSpec((c, qw), lambda i: (nc - 1 - i, col))
    st = pl.BlockSpec((2, B_HEADS, B_QK_DIM, B_V_DIM), lambda i: (0, 0, 0, 0))
    q_col = SHIFT_PAD // qw
    v_col = (SHIFT_PAD + 2 * qw) // vw
    return pl.pallas_call(
        _retention_kernel,
        out_shape=[jax.ShapeDtypeStruct((t_len, vw), F32), jax.ShapeDtypeStruct((t_len, vw), F32),
                   jax.ShapeDtypeStruct((2, B_HEADS, B_QK_DIM, B_V_DIM), F32)],
        grid=(nc,),
        in_specs=[
            fq(q_col), fq(q_col + 1), pl.BlockSpec((c, vw), lambda i: (i, v_col)),
            pl.BlockSpec((c, B_QK_DIM), lambda i: (i, 0)), pl.BlockSpec((c, B_QK_DIM), lambda i: (i, 0)),
            bq(q_col), bq(q_col + 1), pl.BlockSpec((c, vw), lambda i: (nc - 1 - i, v_col)),
            pl.BlockSpec((c, B_QK_DIM), lambda i: (nc - 1 - i, 0)),
            pl.BlockSpec((c, B_QK_DIM), lambda i: (nc - 1 - i, 0)),
            pl.BlockSpec((2 * B_HEADS, 128), lambda i: (0, 0)), st,
        ],
        out_specs=[pl.BlockSpec((c, vw), lambda i: (i, 0)), pl.BlockSpec((c, vw), lambda i: (nc - 1 - i, 0)), st],
        scratch_shapes=[pltpu.VMEM((2, B_HEADS, B_QK_DIM, B_V_DIM), F32)],
        compiler_params=_params(("arbitrary",)),
        name="retention",
    )(proj, proj, proj, cos, sin, proj, proj, proj, cos, sin, lgt, r0)


def _mix_out_kernel(of_ref, ob_ref, bonus_ref, g_ref, yf_ref, yb_ref, gb_ref, ga_ref, gbb_ref, x_ref,
                    alnw, alnb, rlnw, rlnb, npm, npf, g1_ref, sc2_ref, sh2_ref, wa_ref, wb_ref, wo_ref, bd_ref,
                    x1_ref, h2_ref, h3_ref):
    bd = bd_ref[...]
    o = of_ref[...] + ob_ref[...]
    mu = _dot_split_lhs(o, bd) * (1.0 / A_HEAD_DIM)
    oc = o - mu
    var = _dot_split_lhs(oc * oc, bd) * (1.0 / A_HEAD_DIM)
    ya = oc * lax.rsqrt(var + RWKV_GN_EPS) * alnw[...] + alnb[...] + bonus_ref[...]
    ya = _bdot(_bf(ya * g_ref[...]), wa_ref[...])

    y = yf_ref[...] + yb_ref[...]
    parts = []
    for h in range(B_HEADS):
        seg = y[:, h * B_V_DIM:(h + 1) * B_V_DIM]
        m = jnp.mean(seg, axis=-1, keepdims=True)
        sc = seg - m
        vr = jnp.mean(sc * sc, axis=-1, keepdims=True)
        parts.append(sc * lax.rsqrt(vr + RET_GN_EPS))
    yn = jnp.concatenate(parts, axis=1) * rlnw[...] + rlnb[...]
    gb = gb_ref[...]
    yb = _bdot(_bf(yn * (gb * _sigmoid(gb))), wb_ref[...])

    merged = _sigmoid(ga_ref[...]) * ya + _sigmoid(gbb_ref[...]) * yb
    mix = _bdot(_bf(merged), wo_ref[...])
    x1 = x_ref[...] + g1_ref[...] * (_rms(mix) * npm[...])
    x1_ref[...] = x1
    h2 = (_rms(x1) * npf[...]) * (1.0 + sc2_ref[...]) + sh2_ref[...]
    h2_ref[...] = h2
    _rows_to_tiles(h2, h3_ref)


def _mix_out(x2d, proj, o_f, o_b, bonus, g, y_f, y_b, lp, vecs, bd):
    t_len = x2d.shape[0]
    tm = min(t_len, 256)
    aw = A_WIDTH
    d = D_MODEL
    ta = lambda: pl.BlockSpec((tm, aw), lambda i: (i, 0))
    td = lambda: pl.BlockSpec((tm, d), lambda i: (i, 0))
    pc = lambda c: pl.BlockSpec((tm, d), lambda i: (i, c))
    va = lambda: pl.BlockSpec((1, aw), lambda i: (0, 0))
    vd = lambda: pl.BlockSpec((1, d), lambda i: (0, 0))
    full = lambda shp: pl.BlockSpec(shp, lambda i: (0, 0))
    gcol = (SHIFT_PAD + 2 * B_QK_WIDTH + B_V_WIDTH) // d
    return pl.pallas_call(
        _mix_out_kernel,
        out_shape=[jax.ShapeDtypeStruct((t_len, d), F32), jax.ShapeDtypeStruct((t_len, d), F32),
                   jax.ShapeDtypeStruct((t_len * ROW_TILE, LANES), F32)],
        grid=(t_len // tm,),
        in_specs=[ta(), ta(), ta(), ta(), td(), td(), pc(gcol), pc(gcol + 1), pc(gcol + 2), td(),
                  va(), va(), vd(), vd(), vd(), vd(), vd(), vd(), vd(),
                  full((aw, d)), full((d, d)), full((d, d)), full((aw, aw))],
        out_specs=[td(), td(), pl.BlockSpec((tm * ROW_TILE, LANES), lambda i: (i, 0))],
        compiler_params=_params(("arbitrary",)),
        name="mix_out",
    )(o_f, o_b, bonus, g, y_f, y_b, proj, proj, proj, x2d,
      lp["ln_w"], lp["ln_b"], lp["ret_ln_w"], lp["ret_ln_b"], vecs["npm"], vecs["npf"], vecs["g1"], vecs["sc2"],
      vecs["sh2"], lp["w_a"], lp["w_b"], lp["w_o"], bd)


def _router_kernel(h_ref, rw_ref, bias_ref, sel_ref, wts_ref, rank_ref, cnt_ref, cnt_s, *, tm):
    step = pl.program_id(0)

    @pl.when(step == 0)
    def _():
        cnt_s[...] = jnp.zeros_like(cnt_s)

    ne = N_EXPERTS
    scores = _sigmoid(_dot3(h_ref[...], rw_ref[...]))
    work = scores + bias_ref[...]
    lane = lax.broadcasted_iota(jnp.int32, (tm, ne), 1).astype(F32)
    idxs = []
    vals = []
    for _ in range(TOP_K):
        m = jnp.max(work, axis=-1, keepdims=True)
        idx = jnp.min(jnp.where(work == m, lane, float(ne)), axis=-1, keepdims=True)
        oh = lane == idx
        vals.append(jnp.sum(jnp.where(oh, scores, 0.0), axis=-1, keepdims=True))
        idxs.append(idx)
        work = jnp.where(oh, -jnp.inf, work)
    sel_f = jnp.concatenate(idxs, axis=1)
    s_sel = jnp.concatenate(vals, axis=1)
    wts_ref[...] = s_sel / jnp.sum(s_sel, axis=1, keepdims=True) * ROUTED_SCALE
    sel_ref[...] = sel_f.astype(jnp.int32)

    hit = work == -jnp.inf
    r_i = lax.broadcasted_iota(jnp.int32, (tm, tm), 0)
    c_i = lax.broadcasted_iota(jnp.int32, (tm, tm), 1)
    before = _bdot((r_i > c_i).astype(BF16), hit.astype(BF16)) + cnt_s[...]
    ranks = [jnp.sum(jnp.where(lane == idxs[k], before, 0.0), axis=-1, keepdims=True) for k in range(TOP_K)]
    rank_ref[...] = jnp.concatenate(ranks, axis=1).astype(jnp.int32)
    cnt = cnt_s[...] + jnp.sum(hit.astype(F32), axis=0, keepdims=True)
    cnt_s[...] = cnt
    cnt_ref[...] = cnt.astype(jnp.int32)


def _router(h2, router_w, router_bias):
    t_len = h2.shape[0]
    tm = min(t_len, 256)
    kern = functools.partial(_router_kernel, tm=tm)
    tk = lambda: pl.BlockSpec((tm, TOP_K), lambda i: (i, 0))
    return pl.pallas_call(
        kern,
        out_shape=[jax.ShapeDtypeStruct((t_len, TOP_K), jnp.int32), jax.ShapeDtypeStruct((t_len, TOP_K), F32),
                   jax.ShapeDtypeStruct((t_len, TOP_K), jnp.int32), jax.ShapeDtypeStruct((1, N_EXPERTS), jnp.int32)],
        grid=(t_len // tm,),
        in_specs=[pl.BlockSpec((tm, D_MODEL), lambda i: (i, 0)),
                  pl.BlockSpec((D_MODEL, N_EXPERTS), lambda i: (0, 0)),
                  pl.BlockSpec((1, N_EXPERTS), lambda i: (0, 0))],
        out_specs=[tk(), tk(), tk(), pl.BlockSpec((1, N_EXPERTS), lambda i: (0, 0))],
        scratch_shapes=[pltpu.VMEM((1, N_EXPERTS), F32)],
        compiler_params=_params(("arbitrary",)),
        name="router",
    )(h2, router_w, router_bias.reshape(1, N_EXPERTS))


def _dispatch_kernel(slot_ref, start_ref, cnt_ref, h_ref, xs_out, zrow, sem, zsem, *, tm):
    def row_copy(t, k):
        return pltpu.make_async_copy(_row_tile(h_ref, t), _row_tile(xs_out, slot_ref[t * TOP_K + k]), sem)

    def issue(t, carry):
        for k in range(TOP_K):
            row_copy(t, k).start()
        return carry

    lax.fori_loop(0, tm, issue, 0)

    def drain(t, carry):
        for k in range(TOP_K):
            row_copy(t, k).wait()
        return carry

    lax.fori_loop(0, tm, drain, 0)

    @pl.when(pl.program_id(0) == pl.num_programs(0) - 1)
    def _():
        zrow[...] = jnp.zeros_like(zrow)

        def zero_copy(slot):
            return pltpu.make_async_copy(zrow, _row_tile(xs_out, slot), zsem)

        def for_pad_slots(e0, fn):
            def per_expert(e, carry):
                n = cnt_ref[e]
                base = start_ref[e]
                n_pad = (n + MOE_BLOCK - 1) // MOE_BLOCK * MOE_BLOCK

                def body(j, c):
                    fn(zero_copy(base + j))
                    return c

                lax.fori_loop(n, n_pad, body, 0)
                return carry

            lax.fori_loop(e0, e0 + ZERO_PAD_GROUP, per_expert, 0)

        def per_group(gi, carry):
            e0 = gi * ZERO_PAD_GROUP
            for_pad_slots(e0, lambda cp: cp.start())
            for_pad_slots(e0, lambda cp: cp.wait())
            return carry

        lax.fori_loop(0, N_EXPERTS // ZERO_PAD_GROUP, per_group, 0)


def _dispatch(h3, slots, pad_start, counts, n_slots):
    t_len = h3.shape[0] // ROW_TILE
    tm = min(t_len, 256)
    kern = functools.partial(_dispatch_kernel, tm=tm)
    smem_blk = lambda: pl.BlockSpec((tm * TOP_K,), lambda i: (i,), memory_space=pltpu.SMEM)
    smem_all = lambda: pl.BlockSpec((N_EXPERTS,), lambda i: (0,), memory_space=pltpu.SMEM)
    return pl.pallas_call(
        kern,
        out_shape=jax.ShapeDtypeStruct((n_slots * ROW_TILE, LANES), F32),
        grid=(t_len // tm,),
        in_specs=[smem_blk(), smem_all(), smem_all(),
                  pl.BlockSpec((tm * ROW_TILE, LANES), lambda i: (i, 0))],
        out_specs=pl.BlockSpec(memory_space=pl.ANY),
        scratch_shapes=[pltpu.VMEM((ROW_TILE, LANES), F32), pltpu.SemaphoreType.DMA(()),
                        pltpu.SemaphoreType.DMA(())],
        compiler_params=_params(("arbitrary",)),
        name="dispatch",
    )(slots, pad_start, counts, h3)


def _expert_kernel(be_ref, nu_ref, xs_ref, wg_ref, wu_ref, wd_ref, ys_ref, wg_s, wu_s, wd_s):
    b = pl.program_id(0)
    used = b < nu_ref[0]
    new_expert = (b == 0) | (be_ref[b] != be_ref[jnp.maximum(b - 1, 0)])

    @pl.when(used & new_expert)
    def _():
        wg_s[...] = _bf(wg_ref[0])
        wu_s[...] = _bf(wu_ref[0])
        wd_s[...] = _bf(wd_ref[0])

    @pl.when(used)
    def _():
        xb = _bf(_tiles_to_rows(xs_ref, MOE_BLOCK))
        gate = _bdot(xb, wg_s[...])
        up = _bdot(xb, wu_s[...])
        act = gate * _sigmoid(gate) * up
        _rows_to_tiles(_bdot(_bf(act), wd_s[...]), ys_ref)

    @pl.when(jnp.logical_not(used))
    def _():
        ys_ref[...] = jnp.zeros_like(ys_ref)


def _experts(xs, block_e, n_used, w_gate, w_up, w_down):
    n_slots = xs.shape[0] // ROW_TILE
    n_blocks = n_slots // MOE_BLOCK
    rows = (MOE_BLOCK * ROW_TILE, LANES)
    grid_spec = pltpu.PrefetchScalarGridSpec(
        num_scalar_prefetch=2,
        grid=(n_blocks,),
        in_specs=[
            pl.BlockSpec(rows, lambda b, be, nu: (jnp.where(b < nu[0], b, 0), 0)),
            pl.BlockSpec((1, D_MODEL, EXPERT_FF), lambda b, be, nu: (be[b], 0, 0)),
            pl.BlockSpec((1, D_MODEL, EXPERT_FF), lambda b, be, nu: (be[b], 0, 0)),
            pl.BlockSpec((1, EXPERT_FF, D_MODEL), lambda b, be, nu: (be[b], 0, 0)),
        ],
        out_specs=pl.BlockSpec(rows, lambda b, be, nu: (b, 0)),
        scratch_shapes=[pltpu.VMEM((D_MODEL, EXPERT_FF), BF16), pltpu.VMEM((D_MODEL, EXPERT_FF), BF16),
                        pltpu.VMEM((EXPERT_FF, D_MODEL), BF16)],
    )
    return pl.pallas_call(
        _expert_kernel,
        out_shape=jax.ShapeDtypeStruct((n_slots * ROW_TILE, LANES), F32),
        grid_spec=grid_spec,
        compiler_params=_params(("arbitrary",)),
        name="experts",
    )(block_e, n_used, xs, w_gate, w_up, w_down)


def _combine_kernel(slot_ref, ys_ref, wts_ref, h_ref, x1_ref, sg_ref, su_ref, sd_ref, npo, g2_ref,
                    o_ref, buf, sem, *, tm):
    def row_copy(t, k):
        return pltpu.make_async_copy(_row_tile(ys_ref, slot_ref[t * TOP_K + k]), _row_tile(buf, t, (k,)), sem)

    def issue(t, carry):
        for k in range(TOP_K):
            row_copy(t, k).start()
        return carry

    lax.fori_loop(0, tm, issue, 0)

    hb = _bf(h_ref[...])
    gate = _bdot(hb, sg_ref[...])
    up = _bdot(hb, su_ref[...])
    shared = _bdot(_bf(gate * _sigmoid(gate) * up), sd_ref[...])

    def drain(t, carry):
        for k in range(TOP_K):
            row_copy(t, k).wait()
        return carry

    lax.fori_loop(0, tm, drain, 0)

    wts = wts_ref[...]
    routed = _tiles_to_rows(buf, tm, (0,)) * wts[:, 0:1]
    for k in range(1, TOP_K):
        routed = routed + _tiles_to_rows(buf, tm, (k,)) * wts[:, k:k + 1]
    o_ref[...] = x1_ref[...] + g2_ref[...] * (_rms(routed + shared) * npo[...])


def _combine(ys, slots, wts, h2, x1, sg, su, sd, npo, g2):
    t_len = h2.shape[0]
    tm = min(t_len, 256)
    d = D_MODEL
    kern = functools.partial(_combine_kernel, tm=tm)
    smem_blk = lambda: pl.BlockSpec((tm * TOP_K,), lambda i: (i,), memory_space=pltpu.SMEM)
    td = lambda: pl.BlockSpec((tm, d), lambda i: (i, 0))
    vd = lambda: pl.BlockSpec((1, d), lambda i: (0, 0))
    return pl.pallas_call(
        kern,
        out_shape=jax.ShapeDtypeStruct((t_len, d), F32),
        grid=(t_len // tm,),
        in_specs=[smem_blk(),
                  pl.BlockSpec(memory_space=pl.ANY),
                  pl.BlockSpec((tm, TOP_K), lambda i: (i, 0)),
                  td(), td(),
                  pl.BlockSpec((d, SHARED_FF), lambda i: (0, 0)), pl.BlockSpec((d, SHARED_FF), lambda i: (0, 0)),
                  pl.BlockSpec((SHARED_FF, d), lambda i: (0, 0)), vd(), vd()],
        out_specs=td(),
        scratch_shapes=[pltpu.VMEM((TOP_K, tm * ROW_TILE, LANES), F32), pltpu.SemaphoreType.DMA(())],
        compiler_params=_params(("arbitrary",)),
        name="combine",
    )(slots, ys, wts, h2, x1, sg, su, sd, npo, g2)


def _block_diag_ones(width, group):
    idx = np.arange(width) // group
    return jnp.asarray(idx[:, None] == idx[None, :], dtype=BF16)


def _rope_tables(t_len):
    pos = np.arange(t_len)
    rows = (pos // GRID_W).astype(np.float32)
    cols = (pos % GRID_W).astype(np.float32)
    quarter = B_QK_DIM // 4
    inv_freq = jnp.asarray(ROPE_BASE, F32) ** (-jnp.arange(quarter, dtype=F32) / quarter)
    ang_r = jnp.asarray(rows)[:, None] * inv_freq
    ang_c = jnp.asarray(cols)[:, None] * inv_freq
    cr, sr, cc, sc = jnp.cos(ang_r), jnp.sin(ang_r), jnp.cos(ang_c), jnp.sin(ang_c)
    return jnp.concatenate([cr, cr, cc, cc], axis=1), jnp.concatenate([-sr, sr, -sc, sc], axis=1)


def _token_mixer(x2d, g_pre, sc, sh, lp, cos, sin, states, bd):
    proj = _in_proj(x2d, g_pre, sc, sh, lp["w_in"], lp["mu"])
    prep = _rwkv_prep(proj, lp, bd)
    o_f, o_b, s_fin = _rwkv_scan(proj, prep, states[0])
    y_f, y_b, r_fin = _retention(proj, cos, sin, lp["lgt"], states[1])
    return proj, prep, (o_f, o_b), (y_f, y_b), (s_fin, r_fin)


def kernel(x, c, ctx, c_ctx, w_mod, b_mod, norm_pre_mix, norm_post_mix, norm_pre_ffn, norm_post_ffn, w_in, shift_mu,
           rwkv_w0, rwkv_w_up, rwkv_a0, rwkv_a_up, rwkv_g_up, rwkv_k_k, rwkv_k_a, rwkv_r_k, rwkv_ln_w, rwkv_ln_b,
           w_branch_a, ret_decay_logit, ret_ln_w, ret_ln_b, w_branch_b, w_out, router_w, router_bias, exp_w_gate,
           exp_w_up, exp_w_down, sh_w_gate, sh_w_up, sh_w_down):
    d = D_MODEL
    assert x.shape[0] == 1 and w_in.shape[0] == 1, "single batch element, single layer"
    t_len = x.shape[1]
    x2d = x.reshape(t_len, d)
    ctx2d = ctx.reshape(ctx.shape[1], d)
    row = lambda a: a.reshape(1, -1)

    cs = jnp.zeros((8, d), F32).at[0].set(c[0]).at[1].set(c_ctx)
    mod = _modulation(cs, w_mod[0], b_mod[0])
    sh1, sc1, g1, sh2, sc2, g2 = [mod[0:1, i * d:(i + 1) * d] for i in range(6)]
    csh1, csc1 = mod[1:2, 0:d], mod[1:2, d:2 * d]

    w_in_p = jnp.concatenate(
        [w_in[0][:, :SHIFT_COLS], jnp.zeros((d, SHIFT_PAD - SHIFT_COLS), F32), w_in[0][:, SHIFT_COLS:]], axis=1)
    lp = {
        "w_in": w_in_p.astype(BF16),
        "mu": jnp.pad(shift_mu[0], ((0, 0), (0, SHIFT_PAD - SHIFT_COLS))),
        "k_k": row(rwkv_k_k[0]), "k_a": row(rwkv_k_a[0]), "r_k": row(rwkv_r_k[0]),
        "w0": rwkv_w0[0], "w_up": rwkv_w_up[0], "a0": rwkv_a0[0], "a_up": rwkv_a_up[0], "g_up": rwkv_g_up[0],
        "ln_w": row(rwkv_ln_w[0]), "ln_b": row(rwkv_ln_b[0]),
        "ret_ln_w": row(ret_ln_w[0]), "ret_ln_b": row(ret_ln_b[0]),
        "lgt": jnp.broadcast_to(ret_decay_logit[0].reshape(2 * B_HEADS, 1), (2 * B_HEADS, 128)),
        "w_a": w_branch_a[0].astype(BF16), "w_b": w_branch_b[0].astype(BF16), "w_o": w_out[0].astype(BF16),
    }
    bd = _block_diag_ones(A_WIDTH, A_HEAD_DIM)
    g_pre = row(norm_pre_mix[0])

    t_ctx = ctx2d.shape[0]
    zero_states = (jnp.zeros(RWKV_STATE_SHAPE, F32),
                   jnp.zeros((2, B_HEADS, B_QK_DIM, B_V_DIM), F32))
    ones = jnp.ones((t_ctx, B_QK_DIM), F32)
    *_, ctx_states = _token_mixer(ctx2d, g_pre, csc1, csh1, lp, ones, jnp.zeros_like(ones), zero_states, bd)

    cos, sin = _rope_tables(t_len)
    proj, prep, (o_f, o_b), (y_f, y_b), _ = _token_mixer(x2d, g_pre, sc1, sh1, lp, cos, sin, ctx_states, bd)
    vecs = {"npm": row(norm_post_mix[0]), "npf": row(norm_pre_ffn[0]), "g1": g1, "sc2": sc2, "sh2": sh2}
    x1, h2, h3 = _mix_out(x2d, proj, o_f, o_b, prep[7], prep[8], y_f, y_b, lp, vecs, bd)

    sel, wts, rank, counts = _router(h2, router_w[0], router_bias[0])
    counts = counts.reshape(N_EXPERTS)
    padded = (counts + MOE_BLOCK - 1) // MOE_BLOCK * MOE_BLOCK
    pad_end = jnp.cumsum(padded)
    pad_start = (pad_end - padded).astype(jnp.int32)
    n_assign = t_len * TOP_K
    n_blocks = (n_assign + N_EXPERTS * (MOE_BLOCK - 1) + MOE_BLOCK - 1) // MOE_BLOCK
    block_start = jnp.arange(n_blocks, dtype=jnp.int32) * MOE_BLOCK
    block_e = jnp.minimum(jnp.sum(pad_end[None, :] <= block_start[:, None], axis=1), N_EXPERTS - 1).astype(jnp.int32)
    n_used = (pad_end[-1:] // MOE_BLOCK).astype(jnp.int32)
    is_e = sel[..., None] == jnp.arange(N_EXPERTS, dtype=jnp.int32)
    slots = (jnp.sum(jnp.where(is_e, pad_start, 0), axis=-1) + rank).reshape(n_assign)

    xs = _dispatch(h3, slots, pad_start, counts, n_blocks * MOE_BLOCK)
    ys = _experts(xs, block_e, n_used, exp_w_gate[0], exp_w_up[0], exp_w_down[0])
    out = _combine(ys, slots, wts, h2, x1, sh_w_gate[0].astype(BF16),
                   sh_w_up[0].astype(BF16), sh_w_down[0].astype(BF16), row(norm_post_ffn[0]), g2)
    return out.reshape(x.shape)
```

```python
import functools

import jax
import jax.numpy as jnp
import numpy as np
from jax import lax
from jax.experimental import pallas as pl
from jax.experimental.pallas import tpu as pltpu

F32 = jnp.float32
BF16 = jnp.bfloat16

D_MODEL = 1024
GRID_W = 64
NORM_EPS = 1e-6

A_HEAD_DIM = 64
A_WIDTH = D_MODEL // 2
A_HEADS = A_WIDTH // A_HEAD_DIM
W_LORA = 64
ICLR_LORA = 64
G_LORA = 128
RWKV_GN_EPS = 64e-5

B_HEADS = 4
B_QK_WIDTH = D_MODEL // 2
B_V_WIDTH = D_MODEL
B_QK_DIM = B_QK_WIDTH // B_HEADS
B_V_DIM = B_V_WIDTH // B_HEADS
RET_CHUNK = 128
RET_GN_EPS = 1e-5
ROPE_BASE = 10000.0

SHIFT_COLS = 3 * A_WIDTH + W_LORA + ICLR_LORA + G_LORA
LORA_COLS = W_LORA + ICLR_LORA + G_LORA
SHIFT_PAD = 2048
PROJ_COLS = SHIFT_PAD + 2 * B_QK_WIDTH + 2 * B_V_WIDTH + 2 * D_MODEL

N_EXPERTS = 256
TOP_K = 8
EXPERT_FF = D_MODEL // 4
SHARED_FF = D_MODEL // 4
ROUTED_SCALE = 2.5
MOE_BLOCK = 128
ZERO_PAD_GROUP = 16

RWKV_CHUNK = 64
RWKV_STATE_SHAPE = (2, A_HEADS // 2, A_HEAD_DIM, 2 * A_HEAD_DIM)
PROJ_TN = 1024
VMEM_LIMIT = 48 * 1024 * 1024


def _params(sem):
    return pltpu.CompilerParams(dimension_semantics=sem, vmem_limit_bytes=VMEM_LIMIT)


def _bf(a):
    return a.astype(BF16)


def _bdot(a, b):
    return jnp.dot(a, b, preferred_element_type=F32)


def _dot_nt(a, b):
    return lax.dot_general(a, b, (((1,), (1,)), ((), ())), preferred_element_type=F32)


def _dot_tn(a, b):
    return lax.dot_general(a, b, (((0,), (0,)), ((), ())), preferred_element_type=F32)


def _split(a, n):
    out = []
    rem = a
    for _ in range(n):
        p = _bf(rem)
        out.append(p)
        rem = rem - p.astype(F32)
    return out


def _dot_split_lhs(a, b_bf, n=3):
    acc = None
    for p in _split(a, n):
        t = _bdot(p, b_bf)
        acc = t if acc is None else acc + t
    return acc


def _dot_split_rhs(a_bf, b, n=3):
    acc = None
    for p in _split(b, n):
        t = _bdot(a_bf, p)
        acc = t if acc is None else acc + t
    return acc


def _dot3(a, b):
    ah, al = _split(a, 2)
    bh, bl = _split(b, 2)
    return _bdot(ah, bh) + (_bdot(ah, bl) + _bdot(al, bh))


def _sigmoid(x):
    return 1.0 / (1.0 + jnp.exp(-x))


def _softplus(x):
    return jnp.maximum(x, 0.0) + jnp.log1p(jnp.exp(-jnp.abs(x)))


def _rms(x):
    return x * lax.rsqrt(jnp.mean(x * x, axis=-1, keepdims=True) + NORM_EPS)


LANES = 128
ROW_TILE = D_MODEL // LANES


def _rows_to_tiles(x2d, ref, lead=()):
    n = x2d.shape[0]
    for j in range(ROW_TILE):
        ref[lead + (pl.ds(j, n, stride=ROW_TILE), slice(None))] = x2d[:, j * LANES:(j + 1) * LANES]


def _tiles_to_rows(ref, n, lead=()):
    return jnp.concatenate(
        [ref[lead + (pl.ds(j, n, stride=ROW_TILE), slice(None))] for j in range(ROW_TILE)], axis=1)


def _row_tile(ref, r, lead=()):
    return ref.at[lead + (pl.ds(pl.multiple_of(r * ROW_TILE, ROW_TILE), ROW_TILE),)]


def _mod_kernel(cs_ref, w_ref, b_ref, o_ref):
    cs = cs_ref[...]
    s = cs * _sigmoid(cs)
    o_ref[...] = _dot3(s, w_ref[...]) + b_ref[...]


def _modulation(cs, w_mod, b_mod):
    n_out = w_mod.shape[1]
    tn = 1536
    return pl.pallas_call(
        _mod_kernel,
        out_shape=jax.ShapeDtypeStruct((8, n_out), F32),
        grid=(n_out // tn,),
        in_specs=[
            pl.BlockSpec((8, D_MODEL), lambda j: (0, 0)),
            pl.BlockSpec((D_MODEL, tn), lambda j: (0, j)),
            pl.BlockSpec((1, tn), lambda j: (0, j)),
        ],
        out_specs=pl.BlockSpec((8, tn), lambda j: (0, j)),
        compiler_params=_params(("arbitrary",)),
        name="modulation",
    )(cs, w_mod, b_mod.reshape(1, n_out))


def _in_proj_kernel(x_ref, xp_ref, xn_ref, g_ref, sc_ref, sh_ref, w_ref, mu_ref, o_ref, h_s, hh_s, *, tm, n_shift):
    i = pl.program_id(0)
    j = pl.program_id(1)
    n_i = pl.num_programs(0)

    def norm_mod(xv):
        return (_rms(xv) * g_ref[...]) * (1.0 + sc_ref[...]) + sh_ref[...]

    @pl.when(j == 0)
    def _():
        h_s[...] = _bf(norm_mod(x_ref[...]))
        hp = jnp.where(i > 0, norm_mod(xp_ref[...]), 0.0)
        hn = jnp.where(i < n_i - 1, norm_mod(xn_ref[...]), 0.0)
        hh_s[0:8, :] = hp
        hh_s[8:16, :] = hn

    w = w_ref[...]
    p = _bdot(h_s[...], w)

    @pl.when(j < n_shift)
    def _():
        ph = _bdot(_bf(hh_s[...]), w)
        row = lax.broadcasted_iota(jnp.int32, p.shape, 0)
        prev = jnp.where(row == 0, ph[7:8, :], pltpu.roll(p, 1, 0))
        nxt = jnp.where(row == tm - 1, ph[8:9, :], pltpu.roll(p, tm - 1, 0))
        mu = mu_ref[...]
        o_ref[...] = p + mu[0:1, :] * (prev - p) + mu[1:2, :] * (nxt - p)

    @pl.when(j >= n_shift)
    def _():
        o_ref[...] = p


def _in_proj(x2d, g, sc, sh, w_bf, mu_pad):
    t_len = x2d.shape[0]
    tm = min(t_len, 1024)
    tn = PROJ_TN
    n_shift = SHIFT_PAD // tn
    tb8 = tm // 8
    nb8 = t_len // 8
    kern = functools.partial(_in_proj_kernel, tm=tm, n_shift=n_shift)
    vec = lambda: pl.BlockSpec((1, D_MODEL), lambda i, j: (0, 0))
    return pl.pallas_call(
        kern,
        out_shape=jax.ShapeDtypeStruct((t_len, PROJ_COLS), F32),
        grid=(t_len // tm, PROJ_COLS // tn),
        in_specs=[
            pl.BlockSpec((tm, D_MODEL), lambda i, j: (i, 0)),
            pl.BlockSpec((8, D_MODEL), lambda i, j: (jnp.maximum(i * tb8 - 1, 0), 0)),
            pl.BlockSpec((8, D_MODEL), lambda i, j: (jnp.minimum((i + 1) * tb8, nb8 - 1), 0)),
            vec(), vec(), vec(),
            pl.BlockSpec((D_MODEL, tn), lambda i, j: (0, j)),
            pl.BlockSpec((2, tn), lambda i, j: (0, jnp.minimum(j, n_shift - 1))),
        ],
        out_specs=pl.BlockSpec((tm, tn), lambda i, j: (i, j)),
        scratch_shapes=[pltpu.VMEM((tm, D_MODEL), BF16), pltpu.VMEM((16, D_MODEL), F32)],
        compiler_params=_params(("arbitrary", "arbitrary")),
        name="in_proj",
    )(x2d, x2d, x2d, g, sc, sh, w_bf, mu_pad)


def _rwkv_prep_kernel(r_ref, k_ref, v_ref, lora_ref, kk_w, ka_w, rk_w, w0_ref, wup_ref, a0_ref, aup_ref, gup_ref,
                      bd_ref, kk_o, lw0_o, lw1_o, kd0_o, kd1_o, b0_o, b1_o, bonus_o, g_o):
    r = r_ref[...]
    k = k_ref[...]
    v = v_ref[...]
    lora = lora_ref[...]
    xw = lora[:, 0:W_LORA]
    xa = lora[:, W_LORA:W_LORA + ICLR_LORA]
    xg = lora[:, W_LORA + ICLR_LORA:]
    bd = bd_ref[...]

    kk = k * kk_w[...]
    nrm = jnp.sqrt(_dot_split_lhs(kk * kk, bd))
    kk = kk / jnp.maximum(nrm, 1e-12)
    kk_o[...] = kk

    tw = jnp.tanh(xw)
    lw_outs = (lw0_o, lw1_o)
    kd_outs = (kd0_o, kd1_o)
    b_outs = (b0_o, b1_o)
    kd_sum = None
    for d in range(2):
        z = w0_ref[d:d + 1, :] + _dot3(tw, wup_ref[d])
        w_log = -_softplus(-z) - 0.5
        lw_outs[d][...] = -jnp.exp(w_log)
        a_d = _sigmoid(a0_ref[d:d + 1, :] + _dot3(xa, aup_ref[d]))
        kd = k * (1.0 + (a_d - 1.0) * ka_w[...])
        kd_outs[d][...] = kd
        b_outs[d][...] = kk * a_d
        kd_sum = kd if kd_sum is None else kd_sum + kd
    g_o[...] = _dot3(_sigmoid(xg), gup_ref[...])
    bonus_o[...] = _dot_split_lhs(r * kd_sum * rk_w[...], bd) * v


def _rwkv_prep(proj, lp, bd):
    t_len = proj.shape[0]
    tm = min(t_len, 512)
    aw = A_WIDTH
    col = lambda c: pl.BlockSpec((tm, aw), lambda i: (i, c))
    vec = lambda: pl.BlockSpec((1, aw), lambda i: (0, 0))
    full = lambda shp: pl.BlockSpec(shp, lambda i: (0,) * len(shp))
    outs = [jax.ShapeDtypeStruct((t_len, aw), F32)] * 9
    return pl.pallas_call(
        _rwkv_prep_kernel,
        out_shape=outs,
        grid=(t_len // tm,),
        in_specs=[
            col(0), col(1), col(2),
            pl.BlockSpec((tm, LORA_COLS), lambda i: (i, 3 * aw // LORA_COLS)),
            vec(), vec(), vec(),
            full((2, aw)), full((2, W_LORA, aw)), full((2, aw)), full((2, ICLR_LORA, aw)), full((G_LORA, aw)),
            full((aw, aw)),
        ],
        out_specs=[pl.BlockSpec((tm, aw), lambda i: (i, 0))] * 9,
        compiler_params=_params(("arbitrary",)),
        name="rwkv_prep",
    )(proj, proj, proj, proj, lp["k_k"], lp["k_a"], lp["r_k"], lp["w0"], lp["w_up"], lp["a0"], lp["a_up"],
      lp["g_up"], bd)


def _rwkv_scan_kernel(rf, vf, kkf, lwf, kdf, bf_, rb, vb, kkb, lwb, kdb, bb, s0_ref, of_ref, ob_ref, sfin_ref, s_s,
                      *, tb):
    step = pl.program_id(0)

    @pl.when(step == 0)
    def _():
        s_s[...] = s0_ref[...]

    c = RWKV_CHUNK
    n = A_HEAD_DIM
    assert c == n
    pw = 2 * n
    row = lax.broadcasted_iota(jnp.int32, (c, pw), 0)
    lane = lax.broadcasted_iota(jnp.int32, (c, pw), 1)
    col = lane % c
    first = lane < n
    eye = (row == col).astype(F32)
    r_c = lax.broadcasted_iota(jnp.int32, (c, c), 0)
    c_c = lax.broadcasted_iota(jnp.int32, (c, c), 1)
    n_ch = tb // c
    dirs = ((rf, vf, kkf, lwf, kdf, bf_, of_ref), (rb, vb, kkb, lwb, kdb, bb, ob_ref))
    pairs = range(A_HEADS // 2)
    psl = [slice(q * pw, (q + 1) * pw) for q in pairs]

    def bd(x):
        z = jnp.zeros_like(x)
        return jnp.concatenate([jnp.where(first, x, z), jnp.where(first, z, x)], axis=0)

    def chunk_terms(d, refs, ci):
        r_ref, v_ref, kk_ref, lw_ref, kd_ref, b_ref, _ = refs
        if d == 0:
            incl, strict, tri = row >= col, row > col, r_c >= c_c
        else:
            incl, strict, tri = row <= col, row < col, r_c <= c_c
        rows = slice(ci * c, (ci + 1) * c)
        lw = lw_ref[rows, :]
        r = r_ref[rows, :]
        kk = kk_ref[rows, :]
        kd = kd_ref[rows, :]
        b = b_ref[rows, :]
        cl = _dot_split_rhs(tri.astype(BF16), lw)
        cl_tot = cl[c - 1:c, :] if d == 0 else cl[0:1, :]
        e_neg = jnp.exp(-cl)
        e_end = jnp.exp(cl_tot - cl)
        g_tot = jnp.exp(cl_tot)
        rt = r * jnp.exp(cl)
        x_all = _bf(jnp.concatenate([-(kk * jnp.exp(cl - lw)), rt], axis=0))
        bt_all = _bf(b * e_neg)
        kt_all = _bf(kd * e_neg)
        return {
            "rows": rows, "strict": strict, "incl": incl,
            "x": [x_all[:, ps] for ps in psl],
            "z": [jnp.concatenate([bd(bt_all[:, ps]), bd(kt_all[:, ps])], axis=0) for ps in psl],
            "bd_bh": [bd(_bf(b * e_end)[:, ps]) for ps in psl],
            "bd_kh": [bd(_bf(kd * e_end)[:, ps]) for ps in psl],
            "bd_v": [bd(_bf(v_ref[rows, :])[:, ps]) for ps in psl],
            "rt": [rt[:, ps] for ps in psl],
            "g_tot": [g_tot[:, ps] for ps in psl],
        }

    chunks = [(d, cc) for cc in range(n_ch) for d in range(2)]
    prep = {(d, cc): chunk_terms(d, dirs[d], cc if d == 0 else n_ch - 1 - cc) for d, cc in chunks}
    units = [(k, q) for k in chunks for q in pairs]

    def per_unit(name, indexed=True):
        return [prep[k][name][q] if indexed else prep[k][name] for k, q in units]

    strict, incl = per_unit("strict", False), per_unit("incl", False)
    x_u, bd_bh, bd_kh, bd_v, rt_u = (per_unit(s) for s in ("x", "bd_bh", "bd_kh", "bd_v", "rt"))
    g = [_dot_nt(x, z) for x, z in zip(x_u, per_unit("z"))]
    a_ab = [jnp.where(s, gi[:c, :pw], 0.0) for s, gi in zip(strict, g)]
    a_rb = [_bf(jnp.where(i, gi[c:, :pw], 0.0)) for i, gi in zip(incl, g)]
    a_k = [_bf(jnp.concatenate([jnp.where(s, gi[:c, pw:], 0.0), jnp.where(i, gi[c:, pw:], 0.0)], axis=0))
           for s, i, gi in zip(strict, incl, g)]
    a2 = [_bdot(_bf(a), bd(_bf(a))) for a in a_ab]
    pa = [jnp.concatenate([eye + a, sq], axis=0) for a, sq in zip(a_ab, a2)]
    for _ in range(4):
        nxt = [_bdot(_bf(x), bd(_bf(x[c:]))) for x in pa]
        pa = [jnp.concatenate([x[:c] + y[:c], y[c:]], axis=0) for x, y in zip(pa, nxt)]
    p = [_bf(x[:c] + _bdot(_bf(x[:c]), bd(_bf(x[c:])))) for x in pa]
    vk = [_bdot(a, v) for a, v in zip(a_k, bd_v)]
    tw = [_bf(_bdot(pi, jnp.concatenate([bd(x[:c]), bd(_bf(vki[:c]))], axis=1)))
          for pi, vki, x in zip(p, vk, x_u)]
    bd_at = [bd(t[:, :pw]) for t in tw]
    bd_w = [bd(t[:, pw:]) for t in tw]
    mt = [_bf(_dot_tn(a, bh)) for a, bh in zip(bd_at, bd_bh)]
    ntf = [_dot_tn(jnp.concatenate([w, v], axis=0), jnp.concatenate([bh, kh], axis=0))
           for w, v, bh, kh in zip(bd_w, bd_v, bd_bh, bd_kh)]
    nt = [m[:n] + m[n:] for m in ntf]
    rw = [_bdot(a, jnp.concatenate([at, w], axis=1)) for a, at, w in zip(a_rb, bd_at, bd_w)]
    ry = [_bf(r + w[:, :pw]) for w, r in zip(rw, rt_u)]
    y0 = [w[:, pw:] + vki[c:] for w, vki in zip(rw, vk)]
    term = {u: vals for u, vals in zip(units, zip(mt, nt, ry, y0, per_unit("g_tot")))}

    state = [[s_s[d, q] for q in pairs] for d in range(2)]
    dq = [(d, q) for d in range(2) for q in pairs]
    for cc in range(n_ch):
        mt_c, nt_c, ry_c, y0_c, gt_c = zip(*[term[((d, cc), q)] for d, q in dq])
        s0 = [state[d][q] for d, q in dq]
        ys = [_dot_nt(ry_c[i], bd(_bf(s0[i]))) + y0_c[i] for i in range(len(dq))]
        upd = [_bdot(jnp.concatenate(_split(s0[i], 2), axis=0), mt_c[i]) for i in range(len(dq))]
        for i, (d, q) in enumerate(dq):
            state[d][q] = s0[i] * gt_c[i] + (upd[i][:n] + upd[i][n:]) + nt_c[i]
        for d in range(2):
            y_d = [ys[i] for i, (dd, _) in enumerate(dq) if dd == d]
            dirs[d][6][prep[(d, cc)]["rows"], :] = jnp.concatenate(y_d, axis=1)
    for d in range(2):
        for q in pairs:
            s_s[d, q] = state[d][q]

    @pl.when(step == pl.num_programs(0) - 1)
    def _():
        sfin_ref[...] = s_s[...]


def _rwkv_scan(proj, prep, s0):
    kk, lw0, lw1, kd0, kd1, b0, b1 = prep[:7]
    t_len = proj.shape[0]
    tb = 2 * RWKV_CHUNK
    nb = t_len // tb
    aw = A_WIDTH
    fwd = lambda c: pl.BlockSpec((tb, aw), lambda i: (i, c))
    bwd = lambda c: pl.BlockSpec((tb, aw), lambda i: (nb - 1 - i, c))
    st = pl.BlockSpec(RWKV_STATE_SHAPE, lambda i: (0, 0, 0, 0))
    kern = functools.partial(_rwkv_scan_kernel, tb=tb)
    return pl.pallas_call(
        kern,
        out_shape=[jax.ShapeDtypeStruct((t_len, aw), F32), jax.ShapeDtypeStruct((t_len, aw), F32),
                   jax.ShapeDtypeStruct(RWKV_STATE_SHAPE, F32)],
        grid=(nb,),
        in_specs=[fwd(0), fwd(2), fwd(0), fwd(0), fwd(0), fwd(0),
                  bwd(0), bwd(2), bwd(0), bwd(0), bwd(0), bwd(0), st],
        out_specs=[fwd(0), bwd(0), st],
        scratch_shapes=[pltpu.VMEM(RWKV_STATE_SHAPE, F32)],
        compiler_params=_params(("arbitrary",)),
        name="rwkv_scan",
    )(proj, proj, kk, lw0, kd0, b0, proj, proj, kk, lw1, kd1, b1, s0)


def _retention_kernel(qf, kf, vf, cosf, sinf, qb, kb, vb, cosb, sinb, lgt_ref, r0_ref, yf_ref, yb_ref, rfin_ref, r_s):
    step = pl.program_id(0)

    @pl.when(step == 0)
    def _():
        r_s[...] = r0_ref[...]

    c = RET_CHUNK
    dk = B_QK_DIM
    dv = B_V_DIM
    lg_all = -_softplus(-lgt_ref[...])
    rowf = lax.broadcasted_iota(jnp.int32, (c, c), 0).astype(F32)
    colf = lax.broadcasted_iota(jnp.int32, (c, c), 1).astype(F32)
    lane = lax.broadcasted_iota(jnp.int32, (c, dk), 1)
    first_half = (lane % 64) < 32
    dirs = ((qf, kf, vf, cosf, sinf, yf_ref), (qb, kb, vb, cosb, sinb, yb_ref))
    for d, (q_ref, k_ref, v_ref, cos_ref, sin_ref, y_ref) in enumerate(dirs):
        cos = cos_ref[...]
        sin = sin_ref[...]
        diff = (rowf - colf) if d == 0 else (colf - rowf)
        pos = rowf if d == 0 else (c - 1.0) - rowf
        for h in range(B_HEADS):
            lg = lg_all[d * B_HEADS + h:d * B_HEADS + h + 1, :]
            qh = q_ref[:, h * dk:(h + 1) * dk]
            kh = k_ref[:, h * dk:(h + 1) * dk] * (dk ** -0.5)

            def rope(xv):
                swapped = jnp.where(first_half, pltpu.roll(xv, dk - 32, 1), pltpu.roll(xv, 32, 1))
                return xv * cos + swapped * sin

            qh = rope(qh)
            kh = rope(kh)
            vh = _bf(v_ref[:, h * dv:(h + 1) * dv])
            dmask = jnp.where(diff >= 0.0, jnp.exp(lg * jnp.maximum(diff, 0.0)), 0.0)
            scores = _dot_nt(_bf(qh), _bf(kh)) * dmask
            inner = _bdot(_bf(scores), vh)
            xi = jnp.exp(lg * (pos + 1.0))
            zeta = jnp.exp(lg * ((c - 1.0) - pos))
            r_prev = r_s[d, h]
            cross = _bdot(_bf(qh * xi), _bf(r_prev))
            y_ref[:, h * dv:(h + 1) * dv] = inner + cross
            kv = _dot_tn(_bf(kh * zeta), vh)
            g_chunk = jnp.exp(lg * float(c))
            r_s[d, h] = jnp.concatenate([g_chunk, g_chunk], axis=1) * r_prev + kv

    @pl.when(step == pl.num_programs(0) - 1)
    def _():
        rfin_ref[...] = r_s[...]


def _retention(proj, cos, sin, lgt, r0):
    t_len = proj.shape[0]
    c = RET_CHUNK
    nc = t_len // c
    qw = B_QK_WIDTH
    vw = B_V_WIDTH
    fq = lambda col: pl.BlockSpec((c, qw), lambda i: (i, col))
    bq = lambda col: pl.BlockSpec((c, qw), lambda i: (nc - 1 - i, col))
    st = pl.BlockSpec((2, B_HEADS, B_QK_DIM, B_V_DIM), lambda i: (0, 0, 0, 0))
    q_col = SHIFT_PAD // qw
    v_col = (SHIFT_PAD + 2 * qw) // vw
    return pl.pallas_call(
        _retention_kernel,
        out_shape=[jax.ShapeDtypeStruct((t_len, vw), F32), jax.ShapeDtypeStruct((t_len, vw), F32),
                   jax.ShapeDtypeStruct((2, B_HEADS, B_QK_DIM, B_V_DIM), F32)],
        grid=(nc,),
        in_specs=[
            fq(q_col), fq(q_col + 1), pl.BlockSpec((c, vw), lambda i: (i, v_col)),
            pl.BlockSpec((c, B_QK_DIM), lambda i: (i, 0)), pl.BlockSpec((c, B_QK_DIM), lambda i: (i, 0)),
            bq(q_col), bq(q_col + 1), pl.BlockSpec((c, vw), lambda i: (nc - 1 - i, v_col)),
            pl.BlockSpec((c, B_QK_DIM), lambda i: (nc - 1 - i, 0)),
            pl.BlockSpec((c, B_QK_DIM), lambda i: (nc - 1 - i, 0)),
            pl.BlockSpec((2 * B_HEADS, 128), lambda i: (0, 0)), st,
        ],
        out_specs=[pl.BlockSpec((c, vw), lambda i: (i, 0)), pl.BlockSpec((c, vw), lambda i: (nc - 1 - i, 0)), st],
        scratch_shapes=[pltpu.VMEM((2, B_HEADS, B_QK_DIM, B_V_DIM), F32)],
        compiler_params=_params(("arbitrary",)),
        name="retention",
    )(proj, proj, proj, cos, sin, proj, proj, proj, cos, sin, lgt, r0)


def _mix_out_kernel(of_ref, ob_ref, bonus_ref, g_ref, yf_ref, yb_ref, gb_ref, ga_ref, gbb_ref, x_ref,
                    alnw, alnb, rlnw, rlnb, npm, npf, g1_ref, sc2_ref, sh2_ref, wa_ref, wb_ref, wo_ref, bd_ref,
                    x1_ref, h2_ref, h3_ref):
    bd = bd_ref[...]
    o = of_ref[...] + ob_ref[...]
    mu = _dot_split_lhs(o, bd) * (1.0 / A_HEAD_DIM)
    oc = o - mu
    var = _dot_split_lhs(oc * oc, bd) * (1.0 / A_HEAD_DIM)
    ya = oc * lax.rsqrt(var + RWKV_GN_EPS) * alnw[...] + alnb[...] + bonus_ref[...]
    ya = _bdot(_bf(ya * g_ref[...]), wa_ref[...])

    y = yf_ref[...] + yb_ref[...]
    parts = []
    for h in range(B_HEADS):
        seg = y[:, h * B_V_DIM:(h + 1) * B_V_DIM]
        m = jnp.mean(seg, axis=-1, keepdims=True)
        sc = seg - m
        vr = jnp.mean(sc * sc, axis=-1, keepdims=True)
        parts.append(sc * lax.rsqrt(vr + RET_GN_EPS))
    yn = jnp.concatenate(parts, axis=1) * rlnw[...] + rlnb[...]
    gb = gb_ref[...]
    yb = _bdot(_bf(yn * (gb * _sigmoid(gb))), wb_ref[...])

    merged = _sigmoid(ga_ref[...]) * ya + _sigmoid(gbb_ref[...]) * yb
    mix = _bdot(_bf(merged), wo_ref[...])
    x1 = x_ref[...] + g1_ref[...] * (_rms(mix) * npm[...])
    x1_ref[...] = x1
    h2 = (_rms(x1) * npf[...]) * (1.0 + sc2_ref[...]) + sh2_ref[...]
    h2_ref[...] = h2
    _rows_to_tiles(h2, h3_ref)


def _mix_out(x2d, proj, o_f, o_b, bonus, g, y_f, y_b, lp, vecs, bd):
    t_len = x2d.shape[0]
    tm = min(t_len, 256)
    aw = A_WIDTH
    d = D_MODEL
    ta = lambda: pl.BlockSpec((tm, aw), lambda i: (i, 0))
    td = lambda: pl.BlockSpec((tm, d), lambda i: (i, 0))
    pc = lambda c: pl.BlockSpec((tm, d), lambda i: (i, c))
    va = lambda: pl.BlockSpec((1, aw), lambda i: (0, 0))
    vd = lambda: pl.BlockSpec((1, d), lambda i: (0, 0))
    full = lambda shp: pl.BlockSpec(shp, lambda i: (0, 0))
    gcol = (SHIFT_PAD + 2 * B_QK_WIDTH + B_V_WIDTH) // d
    return pl.pallas_call(
        _mix_out_kernel,
        out_shape=[jax.ShapeDtypeStruct((t_len, d), F32), jax.ShapeDtypeStruct((t_len, d), F32),
                   jax.ShapeDtypeStruct((t_len * ROW_TILE, LANES), F32)],
        grid=(t_len // tm,),
        in_specs=[ta(), ta(), ta(), ta(), td(), td(), pc(gcol), pc(gcol + 1), pc(gcol + 2), td(),
                  va(), va(), vd(), vd(), vd(), vd(), vd(), vd(), vd(),
                  full((aw, d)), full((d, d)), full((d, d)), full((aw, aw))],
        out_specs=[td(), td(), pl.BlockSpec((tm * ROW_TILE, LANES), lambda i: (i, 0))],
        compiler_params=_params(("arbitrary",)),
        name="mix_out",
    )(o_f, o_b, bonus, g, y_f, y_b, proj, proj, proj, x2d,
      lp["ln_w"], lp["ln_b"], lp["ret_ln_w"], lp["ret_ln_b"], vecs["npm"], vecs["npf"], vecs["g1"], vecs["sc2"],
      vecs["sh2"], lp["w_a"], lp["w_b"], lp["w_o"], bd)


def _router_kernel(h_ref, rw_ref, bias_ref, sel_ref, wts_ref, rank_ref, cnt_ref, cnt_s, *, tm):
    step = pl.program_id(0)

    @pl.when(step == 0)
    def _():
        cnt_s[...] = jnp.zeros_like(cnt_s)

    ne = N_EXPERTS
    scores = _sigmoid(_dot3(h_ref[...], rw_ref[...]))
    work = scores + bias_ref[...]
    lane = lax.broadcasted_iota(jnp.int32, (tm, ne), 1).astype(F32)
    idxs = []
    vals = []
    for _ in range(TOP_K):
        m = jnp.max(work, axis=-1, keepdims=True)
        idx = jnp.min(jnp.where(work == m, lane, float(ne)), axis=-1, keepdims=True)
        oh = lane == idx
        vals.append(jnp.sum(jnp.where(oh, scores, 0.0), axis=-1, keepdims=True))
        idxs.append(idx)
        work = jnp.where(oh, -jnp.inf, work)
    sel_f = jnp.concatenate(idxs, axis=1)
    s_sel = jnp.concatenate(vals, axis=1)
    wts_ref[...] = s_sel / jnp.sum(s_sel, axis=1, keepdims=True) * ROUTED_SCALE
    sel_ref[...] = sel_f.astype(jnp.int32)

    hit = work == -jnp.inf
    r_i = lax.broadcasted_iota(jnp.int32, (tm, tm), 0)
    c_i = lax.broadcasted_iota(jnp.int32, (tm, tm), 1)
    before = _bdot((r_i > c_i).astype(BF16), hit.astype(BF16)) + cnt_s[...]
    ranks = [jnp.sum(jnp.where(lane == idxs[k], before, 0.0), axis=-1, keepdims=True) for k in range(TOP_K)]
    rank_ref[...] = jnp.concatenate(ranks, axis=1).astype(jnp.int32)
    cnt = cnt_s[...] + jnp.sum(hit.astype(F32), axis=0, keepdims=True)
    cnt_s[...] = cnt
    cnt_ref[...] = cnt.astype(jnp.int32)


def _router(h2, router_w, router_bias):
    t_len = h2.shape[0]
    tm = min(t_len, 256)
    kern = functools.partial(_router_kernel, tm=tm)
    tk = lambda: pl.BlockSpec((tm, TOP_K), lambda i: (i, 0))
    return pl.pallas_call(
        kern,
        out_shape=[jax.ShapeDtypeStruct((t_len, TOP_K), jnp.int32), jax.ShapeDtypeStruct((t_len, TOP_K), F32),
                   jax.ShapeDtypeStruct((t_len, TOP_K), jnp.int32), jax.ShapeDtypeStruct((1, N_EXPERTS), jnp.int32)],
        grid=(t_len // tm,),
        in_specs=[pl.BlockSpec((tm, D_MODEL), lambda i: (i, 0)),
                  pl.BlockSpec((D_MODEL, N_EXPERTS), lambda i: (0, 0)),
                  pl.BlockSpec((1, N_EXPERTS), lambda i: (0, 0))],
        out_specs=[tk(), tk(), tk(), pl.BlockSpec((1, N_EXPERTS), lambda i: (0, 0))],
        scratch_shapes=[pltpu.VMEM((1, N_EXPERTS), F32)],
        compiler_params=_params(("arbitrary",)),
        name="router",
    )(h2, router_w, router_bias.reshape(1, N_EXPERTS))


def _slot(sel_ref, rank_ref, start_ref, a):
    return start_ref[sel_ref[a]] + rank_ref[a]


def _dispatch_kernel(sel_ref, rank_ref, start_ref, cnt_ref, h_ref, xs_out, zrow, sem, zsem, *, tm):
    def row_copy(t, k):
        slot = _slot(sel_ref, rank_ref, start_ref, t * TOP_K + k)
        return pltpu.make_async_copy(_row_tile(h_ref, t), _row_tile(xs_out, slot), sem)

    def issue(t, carry):
        for k in range(TOP_K):
            row_copy(t, k).start(priority=k % 2)
        return carry

    lax.fori_loop(0, tm, issue, 0)

    def drain(t, carry):
        for k in range(TOP_K):
            row_copy(t, k).wait()
        return carry

    lax.fori_loop(0, tm, drain, 0)

    @pl.when(pl.program_id(0) == pl.num_programs(0) - 1)
    def _():
        zrow[...] = jnp.zeros_like(zrow)

        def zero_copy(slot):
            return pltpu.make_async_copy(zrow, _row_tile(xs_out, slot), zsem)

        def for_pad_slots(e0, fn):
            def per_expert(e, carry):
                n = cnt_ref[e]
                base = start_ref[e]
                n_pad = (n + MOE_BLOCK - 1) // MOE_BLOCK * MOE_BLOCK

                def body(j, c):
                    fn(zero_copy(base + j))
                    return c

                lax.fori_loop(n, n_pad, body, 0)
                return carry

            lax.fori_loop(e0, e0 + ZERO_PAD_GROUP, per_expert, 0)

        def per_group(gi, carry):
            e0 = gi * ZERO_PAD_GROUP
            for_pad_slots(e0, lambda cp: cp.start())
            for_pad_slots(e0, lambda cp: cp.wait())
            return carry

        lax.fori_loop(0, N_EXPERTS // ZERO_PAD_GROUP, per_group, 0)


def _dispatch(h3, sel_flat, rank_flat, pad_start, counts, n_slots):
    t_len = h3.shape[0] // ROW_TILE
    tm = min(t_len, 256)
    kern = functools.partial(_dispatch_kernel, tm=tm)
    smem_blk = lambda: pl.BlockSpec((tm * TOP_K,), lambda i: (i,), memory_space=pltpu.SMEM)
    smem_all = lambda: pl.BlockSpec((N_EXPERTS,), lambda i: (0,), memory_space=pltpu.SMEM)
    return pl.pallas_call(
        kern,
        out_shape=jax.ShapeDtypeStruct((n_slots * ROW_TILE, LANES), F32),
        grid=(t_len // tm,),
        in_specs=[smem_blk(), smem_blk(), smem_all(), smem_all(),
                  pl.BlockSpec((tm * ROW_TILE, LANES), lambda i: (i, 0))],
        out_specs=pl.BlockSpec(memory_space=pl.ANY),
        scratch_shapes=[pltpu.VMEM((ROW_TILE, LANES), F32), pltpu.SemaphoreType.DMA(()),
                        pltpu.SemaphoreType.DMA(())],
        compiler_params=_params(("arbitrary",)),
        name="dispatch",
    )(sel_flat, rank_flat, pad_start, counts, h3)


def _expert_kernel(be_ref, nu_ref, xs_ref, wg_ref, wu_ref, wd_ref, ys_ref, wg_s, wu_s, wd_s):
    b = pl.program_id(0)
    used = b < nu_ref[0]
    new_expert = (b == 0) | (be_ref[b] != be_ref[jnp.maximum(b - 1, 0)])

    @pl.when(used & new_expert)
    def _():
        wg_s[...] = _bf(wg_ref[0])
        wu_s[...] = _bf(wu_ref[0])
        wd_s[...] = _bf(wd_ref[0])

    @pl.when(used)
    def _():
        xb = _bf(_tiles_to_rows(xs_ref, MOE_BLOCK))
        gate = _bdot(xb, wg_s[...])
        up = _bdot(xb, wu_s[...])
        act = gate * _sigmoid(gate) * up
        _rows_to_tiles(_bdot(_bf(act), wd_s[...]), ys_ref)

    @pl.when(jnp.logical_not(used))
    def _():
        ys_ref[...] = jnp.zeros_like(ys_ref)


def _experts(xs, block_e, n_used, w_gate, w_up, w_down):
    n_slots = xs.shape[0] // ROW_TILE
    n_blocks = n_slots // MOE_BLOCK
    rows = (MOE_BLOCK * ROW_TILE, LANES)
    grid_spec = pltpu.PrefetchScalarGridSpec(
        num_scalar_prefetch=2,
        grid=(n_blocks,),
        in_specs=[
            pl.BlockSpec(rows, lambda b, be, nu: (jnp.where(b < nu[0], b, 0), 0)),
            pl.BlockSpec((1, D_MODEL, EXPERT_FF), lambda b, be, nu: (be[b], 0, 0)),
            pl.BlockSpec((1, D_MODEL, EXPERT_FF), lambda b, be, nu: (be[b], 0, 0)),
            pl.BlockSpec((1, EXPERT_FF, D_MODEL), lambda b, be, nu: (be[b], 0, 0)),
        ],
        out_specs=pl.BlockSpec(rows, lambda b, be, nu: (b, 0)),
        scratch_shapes=[pltpu.VMEM((D_MODEL, EXPERT_FF), BF16), pltpu.VMEM((D_MODEL, EXPERT_FF), BF16),
                        pltpu.VMEM((EXPERT_FF, D_MODEL), BF16)],
    )
    return pl.pallas_call(
        _expert_kernel,
        out_shape=jax.ShapeDtypeStruct((n_slots * ROW_TILE, LANES), F32),
        grid_spec=grid_spec,
        compiler_params=_params(("arbitrary",)),
        name="experts",
    )(block_e, n_used, xs, w_gate, w_up, w_down)


def _combine_kernel(sel_ref, rank_ref, start_ref, ys_ref, wts_ref, h_ref, x1_ref, sg_ref, su_ref, sd_ref, npo, g2_ref,
                    o_ref, buf, sem, *, tm):
    def row_copy(t, k):
        slot = _slot(sel_ref, rank_ref, start_ref, t * TOP_K + k)
        return pltpu.make_async_copy(_row_tile(ys_ref, slot), _row_tile(buf, t, (k,)), sem)

    def issue(t, carry):
        for k in range(TOP_K):
            row_copy(t, k).start(priority=k % 2)
        return carry

    lax.fori_loop(0, tm, issue, 0)

    hb = _bf(h_ref[...])
    gate = _bdot(hb, sg_ref[...])
    up = _bdot(hb, su_ref[...])
    shared = _bdot(_bf(gate * _sigmoid(gate) * up), sd_ref[...])

    def drain(t, carry):
        for k in range(TOP_K):
            row_copy(t, k).wait()
        return carry

    lax.fori_loop(0, tm, drain, 0)

    wts = wts_ref[...]
    routed = _tiles_to_rows(buf, tm, (0,)) * wts[:, 0:1]
    for k in range(1, TOP_K):
        routed = routed + _tiles_to_rows(buf, tm, (k,)) * wts[:, k:k + 1]
    o_ref[...] = x1_ref[...] + g2_ref[...] * (_rms(routed + shared) * npo[...])


def _combine(ys, sel_flat, rank_flat, pad_start, wts, h2, x1, sg, su, sd, npo, g2):
    t_len = h2.shape[0]
    tm = min(t_len, 256)
    d = D_MODEL
    kern = functools.partial(_combine_kernel, tm=tm)
    smem_blk = lambda: pl.BlockSpec((tm * TOP_K,), lambda i: (i,), memory_space=pltpu.SMEM)
    td = lambda: pl.BlockSpec((tm, d), lambda i: (i, 0))
    vd = lambda: pl.BlockSpec((1, d), lambda i: (0, 0))
    return pl.pallas_call(
        kern,
        out_shape=jax.ShapeDtypeStruct((t_len, d), F32),
        grid=(t_len // tm,),
        in_specs=[smem_blk(), smem_blk(),
                  pl.BlockSpec((N_EXPERTS,), lambda i: (0,), memory_space=pltpu.SMEM),
                  pl.BlockSpec(memory_space=pl.ANY),
                  pl.BlockSpec((tm, TOP_K), lambda i: (i, 0)),
                  td(), td(),
                  pl.BlockSpec((d, SHARED_FF), lambda i: (0, 0)), pl.BlockSpec((d, SHARED_FF), lambda i: (0, 0)),
                  pl.BlockSpec((SHARED_FF, d), lambda i: (0, 0)), vd(), vd()],
        out_specs=td(),
        scratch_shapes=[pltpu.VMEM((TOP_K, tm * ROW_TILE, LANES), F32), pltpu.SemaphoreType.DMA(())],
        compiler_params=_params(("arbitrary",)),
        name="combine",
    )(sel_flat, rank_flat, pad_start, ys, wts, h2, x1, sg, su, sd, npo, g2)


def _block_diag_ones(width, group):
    idx = np.arange(width) // group
    return jnp.asarray(idx[:, None] == idx[None, :], dtype=BF16)


def _rope_tables(t_len):
    pos = np.arange(t_len)
    rows = (pos // GRID_W).astype(np.float32)
    cols = (pos % GRID_W).astype(np.float32)
    quarter = B_QK_DIM // 4
    inv_freq = jnp.asarray(ROPE_BASE, F32) ** (-jnp.arange(quarter, dtype=F32) / quarter)
    ang_r = jnp.asarray(rows)[:, None] * inv_freq
    ang_c = jnp.asarray(cols)[:, None] * inv_freq
    cr, sr, cc, sc = jnp.cos(ang_r), jnp.sin(ang_r), jnp.cos(ang_c), jnp.sin(ang_c)
    return jnp.concatenate([cr, cr, cc, cc], axis=1), jnp.concatenate([-sr, sr, -sc, sc], axis=1)


def _token_mixer(x2d, g_pre, sc, sh, lp, cos, sin, states, bd):
    proj = _in_proj(x2d, g_pre, sc, sh, lp["w_in"], lp["mu"])
    prep = _rwkv_prep(proj, lp, bd)
    o_f, o_b, s_fin = _rwkv_scan(proj, prep, states[0])
    y_f, y_b, r_fin = _retention(proj, cos, sin, lp["lgt"], states[1])
    return proj, prep, (o_f, o_b), (y_f, y_b), (s_fin, r_fin)


def kernel(x, c, ctx, c_ctx, w_mod, b_mod, norm_pre_mix, norm_post_mix, norm_pre_ffn, norm_post_ffn, w_in, shift_mu,
           rwkv_w0, rwkv_w_up, rwkv_a0, rwkv_a_up, rwkv_g_up, rwkv_k_k, rwkv_k_a, rwkv_r_k, rwkv_ln_w, rwkv_ln_b,
           w_branch_a, ret_decay_logit, ret_ln_w, ret_ln_b, w_branch_b, w_out, router_w, router_bias, exp_w_gate,
           exp_w_up, exp_w_down, sh_w_gate, sh_w_up, sh_w_down):
    d = D_MODEL
    assert x.shape[0] == 1 and w_in.shape[0] == 1, "single batch element, single layer"
    t_len = x.shape[1]
    x2d = x.reshape(t_len, d)
    ctx2d = ctx.reshape(ctx.shape[1], d)
    row = lambda a: a.reshape(1, -1)

    cs = jnp.zeros((8, d), F32).at[0].set(c[0]).at[1].set(c_ctx)
    mod = _modulation(cs, w_mod[0], b_mod[0])
    sh1, sc1, g1, sh2, sc2, g2 = [mod[0:1, i * d:(i + 1) * d] for i in range(6)]
    csh1, csc1 = mod[1:2, 0:d], mod[1:2, d:2 * d]

    w_in_p = jnp.concatenate(
        [w_in[0][:, :SHIFT_COLS], jnp.zeros((d, SHIFT_PAD - SHIFT_COLS), F32), w_in[0][:, SHIFT_COLS:]], axis=1)
    lp = {
        "w_in": w_in_p.astype(BF16),
        "mu": jnp.pad(shift_mu[0], ((0, 0), (0, SHIFT_PAD - SHIFT_COLS))),
        "k_k": row(rwkv_k_k[0]), "k_a": row(rwkv_k_a[0]), "r_k": row(rwkv_r_k[0]),
        "w0": rwkv_w0[0], "w_up": rwkv_w_up[0], "a0": rwkv_a0[0], "a_up": rwkv_a_up[0], "g_up": rwkv_g_up[0],
        "ln_w": row(rwkv_ln_w[0]), "ln_b": row(rwkv_ln_b[0]),
        "ret_ln_w": row(ret_ln_w[0]), "ret_ln_b": row(ret_ln_b[0]),
        "lgt": jnp.broadcast_to(ret_decay_logit[0].reshape(2 * B_HEADS, 1), (2 * B_HEADS, 128)),
        "w_a": w_branch_a[0].astype(BF16), "w_b": w_branch_b[0].astype(BF16), "w_o": w_out[0].astype(BF16),
    }
    bd = _block_diag_ones(A_WIDTH, A_HEAD_DIM)
    g_pre = row(norm_pre_mix[0])

    t_ctx = ctx2d.shape[0]
    zero_states = (jnp.zeros(RWKV_STATE_SHAPE, F32),
                   jnp.zeros((2, B_HEADS, B_QK_DIM, B_V_DIM), F32))
    ones = jnp.ones((t_ctx, B_QK_DIM), F32)
    *_, ctx_states = _token_mixer(ctx2d, g_pre, csc1, csh1, lp, ones, jnp.zeros_like(ones), zero_states, bd)

    cos, sin = _rope_tables(t_len)
    proj, prep, (o_f, o_b), (y_f, y_b), _ = _token_mixer(x2d, g_pre, sc1, sh1, lp, cos, sin, ctx_states, bd)
    vecs = {"npm": row(norm_post_mix[0]), "npf": row(norm_pre_ffn[0]), "g1": g1, "sc2": sc2, "sh2": sh2}
    x1, h2, h3 = _mix_out(x2d, proj, o_f, o_b, prep[7], prep[8], y_f, y_b, lp, vecs, bd)

    sel, wts, rank, counts = _router(h2, router_w[0], router_bias[0])
    counts = counts.reshape(N_EXPERTS)
    padded = (counts + MOE_BLOCK - 1) // MOE_BLOCK * MOE_BLOCK
    pad_end = jnp.cumsum(padded)
    pad_start = (pad_end - padded).astype(jnp.int32)
    n_assign = t_len * TOP_K
    n_blocks = (n_assign + N_EXPERTS * (MOE_BLOCK - 1) + MOE_BLOCK - 1) // MOE_BLOCK
    block_start = jnp.arange(n_blocks, dtype=jnp.int32) * MOE_BLOCK
    block_e = jnp.minimum(jnp.sum(pad_end[None, :] <= block_start[:, None], axis=1), N_EXPERTS - 1).astype(jnp.int32)
    n_used = (pad_end[-1:] // MOE_BLOCK).astype(jnp.int32)
    sel_flat = sel.reshape(n_assign)
    rank_flat = rank.reshape(n_assign)

    xs = _dispatch(h3, sel_flat, rank_flat, pad_start, counts, n_blocks * MOE_BLOCK)
    ys = _experts(xs, block_e, n_used, exp_w_gate[0], exp_w_up[0], exp_w_down[0])
    out = _combine(ys, sel_flat, rank_flat, pad_start, wts, h2, x1, sh_w_gate[0].astype(BF16),
                   sh_w_up[0].astype(BF16), sh_w_down[0].astype(BF16), row(norm_post_ffn[0]), g2)
    return out.reshape(x.shape)
```

```python
import functools

import jax
import jax.numpy as jnp
import numpy as np
from jax import lax
from jax.experimental import pallas as pl
from jax.experimental.pallas import tpu as pltpu

F32 = jnp.float32
BF16 = jnp.bfloat16

D_MODEL = 1024
GRID_W = 64
NORM_EPS = 1e-6

A_HEAD_DIM = 64
A_WIDTH = D_MODEL // 2
A_HEADS = A_WIDTH // A_HEAD_DIM
W_LORA = 64
ICLR_LORA = 64
G_LORA = 128
RWKV_GN_EPS = 64e-5

B_HEADS = 4
B_QK_WIDTH = D_MODEL // 2
B_V_WIDTH = D_MODEL
B_QK_DIM = B_QK_WIDTH // B_HEADS
B_V_DIM = B_V_WIDTH // B_HEADS
RET_CHUNK = 128
RET_GN_EPS = 1e-5
ROPE_BASE = 10000.0

SHIFT_COLS = 3 * A_WIDTH + W_LORA + ICLR_LORA + G_LORA
LORA_COLS = W_LORA + ICLR_LORA + G_LORA
SHIFT_PAD = 2048
PROJ_COLS = SHIFT_PAD + 2 * B_QK_WIDTH + 2 * B_V_WIDTH + 2 * D_MODEL

N_EXPERTS = 256
TOP_K = 8
EXPERT_FF = D_MODEL // 4
SHARED_FF = D_MODEL // 4
ROUTED_SCALE = 2.5
MOE_BLOCK = 256
ZERO_PAD_GROUP = 16

RWKV_CHUNK = 64
RWKV_STATE_SHAPE = (2, A_HEADS // 2, A_HEAD_DIM, 2 * A_HEAD_DIM)
PROJ_TN = 1024
VMEM_LIMIT = 48 * 1024 * 1024


def _params(sem):
    return pltpu.CompilerParams(dimension_semantics=sem, vmem_limit_bytes=VMEM_LIMIT)


def _bf(a):
    return a.astype(BF16)


def _bdot(a, b):
    return jnp.dot(a, b, preferred_element_type=F32)


def _dot_nt(a, b):
    return lax.dot_general(a, b, (((1,), (1,)), ((), ())), preferred_element_type=F32)


def _dot_tn(a, b):
    return lax.dot_general(a, b, (((0,), (0,)), ((), ())), preferred_element_type=F32)


def _split(a, n):
    out = []
    rem = a
    for _ in range(n):
        p = _bf(rem)
        out.append(p)
        rem = rem - p.astype(F32)
    return out


def _dot_split_lhs(a, b_bf, n=3):
    acc = None
    for p in _split(a, n):
        t = _bdot(p, b_bf)
        acc = t if acc is None else acc + t
    return acc


def _dot_split_rhs(a_bf, b, n=3):
    acc = None
    for p in _split(b, n):
        t = _bdot(a_bf, p)
        acc = t if acc is None else acc + t
    return acc


def _dot3(a, b):
    ah, al = _split(a, 2)
    bh, bl = _split(b, 2)
    return _bdot(ah, bh) + (_bdot(ah, bl) + _bdot(al, bh))


def _sigmoid(x):
    return 1.0 / (1.0 + jnp.exp(-x))


def _softplus(x):
    return jnp.maximum(x, 0.0) + jnp.log1p(jnp.exp(-jnp.abs(x)))


def _rms(x):
    return x * lax.rsqrt(jnp.mean(x * x, axis=-1, keepdims=True) + NORM_EPS)


LANES = 128
ROW_TILE = D_MODEL // LANES


def _rows_to_tiles(x2d, ref, lead=()):
    n = x2d.shape[0]
    for j in range(ROW_TILE):
        ref[lead + (pl.ds(j, n, stride=ROW_TILE), slice(None))] = x2d[:, j * LANES:(j + 1) * LANES]


def _tiles_to_rows(ref, n, lead=()):
    return jnp.concatenate(
        [ref[lead + (pl.ds(j, n, stride=ROW_TILE), slice(None))] for j in range(ROW_TILE)], axis=1)


def _row_tile(ref, r, lead=()):
    return ref.at[lead + (pl.ds(pl.multiple_of(r * ROW_TILE, ROW_TILE), ROW_TILE),)]


def _mod_kernel(cs_ref, w_ref, b_ref, o_ref):
    cs = cs_ref[...]
    s = cs * _sigmoid(cs)
    o_ref[...] = _dot3(s, w_ref[...]) + b_ref[...]


def _modulation(cs, w_mod, b_mod):
    n_out = w_mod.shape[1]
    tn = 1536
    return pl.pallas_call(
        _mod_kernel,
        out_shape=jax.ShapeDtypeStruct((8, n_out), F32),
        grid=(n_out // tn,),
        in_specs=[
            pl.BlockSpec((8, D_MODEL), lambda j: (0, 0)),
            pl.BlockSpec((D_MODEL, tn), lambda j: (0, j)),
            pl.BlockSpec((1, tn), lambda j: (0, j)),
        ],
        out_specs=pl.BlockSpec((8, tn), lambda j: (0, j)),
        compiler_params=_params(("arbitrary",)),
        name="modulation",
    )(cs, w_mod, b_mod.reshape(1, n_out))


def _in_proj_kernel(x_ref, xp_ref, xn_ref, g_ref, sc_ref, sh_ref, w_ref, mu_ref, o_ref, h_s, hh_s, *, tm, n_shift):
    i = pl.program_id(0)
    j = pl.program_id(1)
    n_i = pl.num_programs(0)

    def norm_mod(xv):
        return (_rms(xv) * g_ref[...]) * (1.0 + sc_ref[...]) + sh_ref[...]

    @pl.when(j == 0)
    def _():
        h_s[...] = _bf(norm_mod(x_ref[...]))
        hp = jnp.where(i > 0, norm_mod(xp_ref[...]), 0.0)
        hn = jnp.where(i < n_i - 1, norm_mod(xn_ref[...]), 0.0)
        hh_s[0:8, :] = hp
        hh_s[8:16, :] = hn

    w = w_ref[...]
    p = _bdot(h_s[...], w)

    @pl.when(j < n_shift)
    def _():
        ph = _bdot(_bf(hh_s[...]), w)
        row = lax.broadcasted_iota(jnp.int32, p.shape, 0)
        prev = jnp.where(row == 0, ph[7:8, :], pltpu.roll(p, 1, 0))
        nxt = jnp.where(row == tm - 1, ph[8:9, :], pltpu.roll(p, tm - 1, 0))
        mu = mu_ref[...]
        o_ref[...] = p + mu[0:1, :] * (prev - p) + mu[1:2, :] * (nxt - p)

    @pl.when(j >= n_shift)
    def _():
        o_ref[...] = p


def _in_proj(x2d, g, sc, sh, w_bf, mu_pad):
    t_len = x2d.shape[0]
    tm = min(t_len, 1024)
    tn = PROJ_TN
    n_shift = SHIFT_PAD // tn
    tb8 = tm // 8
    nb8 = t_len // 8
    kern = functools.partial(_in_proj_kernel, tm=tm, n_shift=n_shift)
    vec = lambda: pl.BlockSpec((1, D_MODEL), lambda i, j: (0, 0))
    return pl.pallas_call(
        kern,
        out_shape=jax.ShapeDtypeStruct((t_len, PROJ_COLS), F32),
        grid=(t_len // tm, PROJ_COLS // tn),
        in_specs=[
            pl.BlockSpec((tm, D_MODEL), lambda i, j: (i, 0)),
            pl.BlockSpec((8, D_MODEL), lambda i, j: (jnp.maximum(i * tb8 - 1, 0), 0)),
            pl.BlockSpec((8, D_MODEL), lambda i, j: (jnp.minimum((i + 1) * tb8, nb8 - 1), 0)),
            vec(), vec(), vec(),
            pl.BlockSpec((D_MODEL, tn), lambda i, j: (0, j)),
            pl.BlockSpec((2, tn), lambda i, j: (0, jnp.minimum(j, n_shift - 1))),
        ],
        out_specs=pl.BlockSpec((tm, tn), lambda i, j: (i, j)),
        scratch_shapes=[pltpu.VMEM((tm, D_MODEL), BF16), pltpu.VMEM((16, D_MODEL), F32)],
        compiler_params=_params(("arbitrary", "arbitrary")),
        name="in_proj",
    )(x2d, x2d, x2d, g, sc, sh, w_bf, mu_pad)


def _rwkv_prep_kernel(r_ref, k_ref, v_ref, lora_ref, kk_w, ka_w, rk_w, w0_ref, wup_ref, a0_ref, aup_ref, gup_ref,
                      bd_ref, kk_o, lw0_o, lw1_o, kd0_o, kd1_o, b0_o, b1_o, bonus_o, g_o):
    r = r_ref[...]
    k = k_ref[...]
    v = v_ref[...]
    lora = lora_ref[...]
    xw = lora[:, 0:W_LORA]
    xa = lora[:, W_LORA:W_LORA + ICLR_LORA]
    xg = lora[:, W_LORA + ICLR_LORA:]
    bd = bd_ref[...]

    kk = k * kk_w[...]
    nrm = jnp.sqrt(_dot_split_lhs(kk * kk, bd))
    kk = kk / jnp.maximum(nrm, 1e-12)
    kk_o[...] = kk

    tw = jnp.tanh(xw)
    lw_outs = (lw0_o, lw1_o)
    kd_outs = (kd0_o, kd1_o)
    b_outs = (b0_o, b1_o)
    kd_sum = None
    for d in range(2):
        z = w0_ref[d:d + 1, :] + _dot3(tw, wup_ref[d])
        w_log = -_softplus(-z) - 0.5
        lw_outs[d][...] = -jnp.exp(w_log)
        a_d = _sigmoid(a0_ref[d:d + 1, :] + _dot3(xa, aup_ref[d]))
        kd = k * (1.0 + (a_d - 1.0) * ka_w[...])
        kd_outs[d][...] = kd
        b_outs[d][...] = kk * a_d
        kd_sum = kd if kd_sum is None else kd_sum + kd
    g_o[...] = _dot3(_sigmoid(xg), gup_ref[...])
    bonus_o[...] = _dot_split_lhs(r * kd_sum * rk_w[...], bd) * v


def _rwkv_prep(proj, lp, bd):
    t_len = proj.shape[0]
    tm = min(t_len, 512)
    aw = A_WIDTH
    col = lambda c: pl.BlockSpec((tm, aw), lambda i: (i, c))
    vec = lambda: pl.BlockSpec((1, aw), lambda i: (0, 0))
    full = lambda shp: pl.BlockSpec(shp, lambda i: (0,) * len(shp))
    outs = [jax.ShapeDtypeStruct((t_len, aw), F32)] * 9
    return pl.pallas_call(
        _rwkv_prep_kernel,
        out_shape=outs,
        grid=(t_len // tm,),
        in_specs=[
            col(0), col(1), col(2),
            pl.BlockSpec((tm, LORA_COLS), lambda i: (i, 3 * aw // LORA_COLS)),
            vec(), vec(), vec(),
            full((2, aw)), full((2, W_LORA, aw)), full((2, aw)), full((2, ICLR_LORA, aw)), full((G_LORA, aw)),
            full((aw, aw)),
        ],
        out_specs=[pl.BlockSpec((tm, aw), lambda i: (i, 0))] * 9,
        compiler_params=_params(("arbitrary",)),
        name="rwkv_prep",
    )(proj, proj, proj, proj, lp["k_k"], lp["k_a"], lp["r_k"], lp["w0"], lp["w_up"], lp["a0"], lp["a_up"],
      lp["g_up"], bd)


def _rwkv_scan_kernel(rf, vf, kkf, lwf, kdf, bf_, rb, vb, kkb, lwb, kdb, bb, s0_ref, of_ref, ob_ref, sfin_ref, s_s,
                      *, tb):
    step = pl.program_id(0)

    @pl.when(step == 0)
    def _():
        s_s[...] = s0_ref[...]

    c = RWKV_CHUNK
    n = A_HEAD_DIM
    assert c == n
    pw = 2 * n
    row = lax.broadcasted_iota(jnp.int32, (c, pw), 0)
    lane = lax.broadcasted_iota(jnp.int32, (c, pw), 1)
    col = lane % c
    first = lane < n
    eye = (row == col).astype(F32)
    r_c = lax.broadcasted_iota(jnp.int32, (c, c), 0)
    c_c = lax.broadcasted_iota(jnp.int32, (c, c), 1)
    n_ch = tb // c
    dirs = ((rf, vf, kkf, lwf, kdf, bf_, of_ref), (rb, vb, kkb, lwb, kdb, bb, ob_ref))
    pairs = range(A_HEADS // 2)
    psl = [slice(q * pw, (q + 1) * pw) for q in pairs]

    def bd(x):
        z = jnp.zeros_like(x)
        return jnp.concatenate([jnp.where(first, x, z), jnp.where(first, z, x)], axis=0)

    def chunk_terms(d, refs, ci):
        r_ref, v_ref, kk_ref, lw_ref, kd_ref, b_ref, _ = refs
        if d == 0:
            incl, strict, tri = row >= col, row > col, r_c >= c_c
        else:
            incl, strict, tri = row <= col, row < col, r_c <= c_c
        rows = slice(ci * c, (ci + 1) * c)
        lw = lw_ref[rows, :]
        r = r_ref[rows, :]
        kk = kk_ref[rows, :]
        kd = kd_ref[rows, :]
        b = b_ref[rows, :]
        cl = _dot_split_rhs(tri.astype(BF16), lw)
        cl_tot = cl[c - 1:c, :] if d == 0 else cl[0:1, :]
        e_neg = jnp.exp(-cl)
        e_end = jnp.exp(cl_tot - cl)
        g_tot = jnp.exp(cl_tot)
        rt = r * jnp.exp(cl)
        x_all = _bf(jnp.concatenate([-(kk * jnp.exp(cl - lw)), rt], axis=0))
        bt_all = _bf(b * e_neg)
        kt_all = _bf(kd * e_neg)
        return {
            "rows": rows, "strict": strict, "incl": incl,
            "x": [x_all[:, ps] for ps in psl],
            "z": [jnp.concatenate([bd(bt_all[:, ps]), bd(kt_all[:, ps])], axis=0) for ps in psl],
            "bd_bh": [bd(_bf(b * e_end)[:, ps]) for ps in psl],
            "bd_kh": [bd(_bf(kd * e_end)[:, ps]) for ps in psl],
            "bd_v": [bd(_bf(v_ref[rows, :])[:, ps]) for ps in psl],
            "rt": [rt[:, ps] for ps in psl],
            "g_tot": [g_tot[:, ps] for ps in psl],
        }

    chunks = [(d, cc) for cc in range(n_ch) for d in range(2)]
    prep = {(d, cc): chunk_terms(d, dirs[d], cc if d == 0 else n_ch - 1 - cc) for d, cc in chunks}
    units = [(k, q) for k in chunks for q in pairs]

    def per_unit(name, indexed=True):
        return [prep[k][name][q] if indexed else prep[k][name] for k, q in units]

    strict, incl = per_unit("strict", False), per_unit("incl", False)
    x_u, bd_bh, bd_kh, bd_v, rt_u = (per_unit(s) for s in ("x", "bd_bh", "bd_kh", "bd_v", "rt"))
    g = [_dot_nt(x, z) for x, z in zip(x_u, per_unit("z"))]
    a_ab = [jnp.where(s, gi[:c, :pw], 0.0) for s, gi in zip(strict, g)]
    a_rb = [_bf(jnp.where(i, gi[c:, :pw], 0.0)) for i, gi in zip(incl, g)]
    a_k = [_bf(jnp.concatenate([jnp.where(s, gi[:c, pw:], 0.0), jnp.where(i, gi[c:, pw:], 0.0)], axis=0))
           for s, i, gi in zip(strict, incl, g)]
    a2 = [_bdot(_bf(a), bd(_bf(a))) for a in a_ab]
    pa = [jnp.concatenate([eye + a, sq], axis=0) for a, sq in zip(a_ab, a2)]
    for _ in range(4):
        nxt = [_bdot(_bf(x), bd(_bf(x[c:]))) for x in pa]
        pa = [jnp.concatenate([x[:c] + y[:c], y[c:]], axis=0) for x, y in zip(pa, nxt)]
    p = [_bf(x[:c] + _bdot(_bf(x[:c]), bd(_bf(x[c:])))) for x in pa]
    vk = [_bdot(a, v) for a, v in zip(a_k, bd_v)]
    tw = [_bf(_bdot(pi, jnp.concatenate([bd(x[:c]), bd(_bf(vki[:c]))], axis=1)))
          for pi, vki, x in zip(p, vk, x_u)]
    bd_at = [bd(t[:, :pw]) for t in tw]
    bd_w = [bd(t[:, pw:]) for t in tw]
    mt = [_bf(_dot_tn(a, bh)) for a, bh in zip(bd_at, bd_bh)]
    ntf = [_dot_tn(jnp.concatenate([w, v], axis=0), jnp.concatenate([bh, kh], axis=0))
           for w, v, bh, kh in zip(bd_w, bd_v, bd_bh, bd_kh)]
    nt = [m[:n] + m[n:] for m in ntf]
    rw = [_bdot(a, jnp.concatenate([at, w], axis=1)) for a, at, w in zip(a_rb, bd_at, bd_w)]
    ry = [_bf(r + w[:, :pw]) for w, r in zip(rw, rt_u)]
    y0 = [w[:, pw:] + vki[c:] for w, vki in zip(rw, vk)]
    term = {u: vals for u, vals in zip(units, zip(mt, nt, ry, y0, per_unit("g_tot")))}

    state = [[s_s[d, q] for q in pairs] for d in range(2)]
    dq = [(d, q) for d in range(2) for q in pairs]
    for cc in range(n_ch):
        mt_c, nt_c, ry_c, y0_c, gt_c = zip(*[term[((d, cc), q)] for d, q in dq])
        s0 = [state[d][q] for d, q in dq]
        ys = [_dot_nt(ry_c[i], bd(_bf(s0[i]))) + y0_c[i] for i in range(len(dq))]
        upd = [_bdot(jnp.concatenate(_split(s0[i], 2), axis=0), mt_c[i]) for i in range(len(dq))]
        for i, (d, q) in enumerate(dq):
            state[d][q] = s0[i] * gt_c[i] + (upd[i][:n] + upd[i][n:]) + nt_c[i]
        for d in range(2):
            y_d = [ys[i] for i, (dd, _) in enumerate(dq) if dd == d]
            dirs[d][6][prep[(d, cc)]["rows"], :] = jnp.concatenate(y_d, axis=1)
    for d in range(2):
        for q in pairs:
            s_s[d, q] = state[d][q]

    @pl.when(step == pl.num_programs(0) - 1)
    def _():
        sfin_ref[...] = s_s[...]


def _rwkv_scan(proj, prep, s0):
    kk, lw0, lw1, kd0, kd1, b0, b1 = prep[:7]
    t_len = proj.shape[0]
    tb = 2 * RWKV_CHUNK
    nb = t_len // tb
    aw = A_WIDTH
    fwd = lambda c: pl.BlockSpec((tb, aw), lambda i: (i, c))
    bwd = lambda c: pl.BlockSpec((tb, aw), lambda i: (nb - 1 - i, c))
    st = pl.BlockSpec(RWKV_STATE_SHAPE, lambda i: (0, 0, 0, 0))
    kern = functools.partial(_rwkv_scan_kernel, tb=tb)
    return pl.pallas_call(
        kern,
        out_shape=[jax.ShapeDtypeStruct((t_len, aw), F32), jax.ShapeDtypeStruct((t_len, aw), F32),
                   jax.ShapeDtypeStruct(RWKV_STATE_SHAPE, F32)],
        grid=(nb,),
        in_specs=[fwd(0), fwd(2), fwd(0), fwd(0), fwd(0), fwd(0),
                  bwd(0), bwd(2), bwd(0), bwd(0), bwd(0), bwd(0), st],
        out_specs=[fwd(0), bwd(0), st],
        scratch_shapes=[pltpu.VMEM(RWKV_STATE_SHAPE, F32)],
        compiler_params=_params(("arbitrary",)),
        name="rwkv_scan",
    )(proj, proj, kk, lw0, kd0, b0, proj, proj, kk, lw1, kd1, b1, s0)


def _retention_kernel(qf, kf, vf, cosf, sinf, qb, kb, vb, cosb, sinb, lgt_ref, r0_ref, yf_ref, yb_ref, rfin_ref, r_s):
    step = pl.program_id(0)

    @pl.when(step == 0)
    def _():
        r_s[...] = r0_ref[...]

    c = RET_CHUNK
    dk = B_QK_DIM
    dv = B_V_DIM
    lg_all = -_softplus(-lgt_ref[...])
    rowf = lax.broadcasted_iota(jnp.int32, (c, c), 0).astype(F32)
    colf = lax.broadcasted_iota(jnp.int32, (c, c), 1).astype(F32)
    lane = lax.broadcasted_iota(jnp.int32, (c, dk), 1)
    first_half = (lane % 64) < 32
    dirs = ((qf, kf, vf, cosf, sinf, yf_ref), (qb, kb, vb, cosb, sinb, yb_ref))
    for d, (q_ref, k_ref, v_ref, cos_ref, sin_ref, y_ref) in enumerate(dirs):
        cos = cos_ref[...]
        sin = sin_ref[...]
        diff = (rowf - colf) if d == 0 else (colf - rowf)
        pos = rowf if d == 0 else (c - 1.0) - rowf
        for h in range(B_HEADS):
            lg = lg_all[d * B_HEADS + h:d * B_HEADS + h + 1, :]
            qh = q_ref[:, h * dk:(h + 1) * dk]
            kh = k_ref[:, h * dk:(h + 1) * dk] * (dk ** -0.5)

            def rope(xv):
                swapped = jnp.where(first_half, pltpu.roll(xv, dk - 32, 1), pltpu.roll(xv, 32, 1))
                return xv * cos + swapped * sin

            qh = rope(qh)
            kh = rope(kh)
            vh = _bf(v_ref[:, h * dv:(h + 1) * dv])
            dmask = jnp.where(diff >= 0.0, jnp.exp(lg * jnp.maximum(diff, 0.0)), 0.0)
            scores = _dot_nt(_bf(qh), _bf(kh)) * dmask
            inner = _bdot(_bf(scores), vh)
            xi = jnp.exp(lg * (pos + 1.0))
            zeta = jnp.exp(lg * ((c - 1.0) - pos))
            r_prev = r_s[d, h]
            cross = _bdot(_bf(qh * xi), _bf(r_prev))
            y_ref[:, h * dv:(h + 1) * dv] = inner + cross
            kv = _dot_tn(_bf(kh * zeta), vh)
            g_chunk = jnp.exp(lg * float(c))
            r_s[d, h] = jnp.concatenate([g_chunk, g_chunk], axis=1) * r_prev + kv

    @pl.when(step == pl.num_programs(0) - 1)
    def _():
        rfin_ref[...] = r_s[...]


def _retention(proj, cos, sin, lgt, r0):
    t_len = proj.shape[0]
    c = RET_CHUNK
    nc = t_len // c
    qw = B_QK_WIDTH
    vw = B_V_WIDTH
    fq = lambda col: pl.BlockSpec((c, qw), lambda i: (i, col))
    bq = lambda col: pl.BlockSpec((c, qw), lambda i: (nc - 1 - i, col))
    st = pl.BlockSpec((2, B_HEADS, B_QK_DIM, B_V_DIM), lambda i: (0, 0, 0, 0))
    q_col = SHIFT_PAD // qw
    v_col = (SHIFT_PAD + 2 * qw) // vw
    return pl.pallas_call(
        _retention_kernel,
        out_shape=[jax.ShapeDtypeStruct((t_len, vw), F32), jax.ShapeDtypeStruct((t_len, vw), F32),
                   jax.ShapeDtypeStruct((2, B_HEADS, B_QK_DIM, B_V_DIM), F32)],
        grid=(nc,),
        in_specs=[
            fq(q_col), fq(q_col + 1), pl.BlockSpec((c, vw), lambda i: (i, v_col)),
            pl.BlockSpec((c, B_QK_DIM), lambda i: (i, 0)), pl.BlockSpec((c, B_QK_DIM), lambda i: (i, 0)),
            bq(q_col), bq(q_col + 1), pl.BlockSpec((c, vw), lambda i: (nc - 1 - i, v_col)),
            pl.BlockSpec((c, B_QK_DIM), lambda i: (nc - 1 - i, 0)),
            pl.BlockSpec((c, B_QK_DIM), lambda i: (nc - 1 - i, 0)),
            pl.BlockSpec((2 * B_HEADS, 128), lambda i: (0, 0)), st,
        ],
        out_specs=[pl.BlockSpec((c, vw), lambda i: (i, 0)), pl.BlockSpec((c, vw), lambda i: (nc - 1 - i, 0)), st],
        scratch_shapes=[pltpu.VMEM((2, B_HEADS, B_QK_DIM, B_V_DIM), F32)],
        compiler_params=_params(("arbitrary",)),
        name="retention",
    )(proj, proj, proj, cos, sin, proj, proj, proj, cos, sin, lgt, r0)


def _mix_out_kernel(of_ref, ob_ref, bonus_ref, g_ref, yf_ref, yb_ref, gb_ref, ga_ref, gbb_ref, x_ref,
                    alnw, alnb, rlnw, rlnb, npm, npf, g1_ref, sc2_ref, sh2_ref, wa_ref, wb_ref, wo_ref, bd_ref,
                    x1_ref, h2_ref, h3_ref):
    bd = bd_ref[...]
    o = of_ref[...] + ob_ref[...]
    mu = _dot_split_lhs(o, bd) * (1.0 / A_HEAD_DIM)
    oc = o - mu
    var = _dot_split_lhs(oc * oc, bd) * (1.0 / A_HEAD_DIM)
    ya = oc * lax.rsqrt(var + RWKV_GN_EPS) * alnw[...] + alnb[...] + bonus_ref[...]
    ya = _bdot(_bf(ya * g_ref[...]), wa_ref[...])

    y = yf_ref[...] + yb_ref[...]
    parts = []
    for h in range(B_HEADS):
        seg = y[:, h * B_V_DIM:(h + 1) * B_V_DIM]
        m = jnp.mean(seg, axis=-1, keepdims=True)
        sc = seg - m
        vr = jnp.mean(sc * sc, axis=-1, keepdims=True)
        parts.append(sc * lax.rsqrt(vr + RET_GN_EPS))
    yn = jnp.concatenate(parts, axis=1) * rlnw[...] + rlnb[...]
    gb = gb_ref[...]
    yb = _bdot(_bf(yn * (gb * _sigmoid(gb))), wb_ref[...])

    merged = _sigmoid(ga_ref[...]) * ya + _sigmoid(gbb_ref[...]) * yb
    mix = _bdot(_bf(merged), wo_ref[...])
    x1 = x_ref[...] + g1_ref[...] * (_rms(mix) * npm[...])
    x1_ref[...] = x1
    h2 = (_rms(x1) * npf[...]) * (1.0 + sc2_ref[...]) + sh2_ref[...]
    h2_ref[...] = h2
    _rows_to_tiles(h2, h3_ref)


def _mix_out(x2d, proj, o_f, o_b, bonus, g, y_f, y_b, lp, vecs, bd):
    t_len = x2d.shape[0]
    tm = min(t_len, 256)
    aw = A_WIDTH
    d = D_MODEL
    ta = lambda: pl.BlockSpec((tm, aw), lambda i: (i, 0))
    td = lambda: pl.BlockSpec((tm, d), lambda i: (i, 0))
    pc = lambda c: pl.BlockSpec((tm, d), lambda i: (i, c))
    va = lambda: pl.BlockSpec((1, aw), lambda i: (0, 0))
    vd = lambda: pl.BlockSpec((1, d), lambda i: (0, 0))
    full = lambda shp: pl.BlockSpec(shp, lambda i: (0, 0))
    gcol = (SHIFT_PAD + 2 * B_QK_WIDTH + B_V_WIDTH) // d
    return pl.pallas_call(
        _mix_out_kernel,
        out_shape=[jax.ShapeDtypeStruct((t_len, d), F32), jax.ShapeDtypeStruct((t_len, d), F32),
                   jax.ShapeDtypeStruct((t_len * ROW_TILE, LANES), F32)],
        grid=(t_len // tm,),
        in_specs=[ta(), ta(), ta(), ta(), td(), td(), pc(gcol), pc(gcol + 1), pc(gcol + 2), td(),
                  va(), va(), vd(), vd(), vd(), vd(), vd(), vd(), vd(),
                  full((aw, d)), full((d, d)), full((d, d)), full((aw, aw))],
        out_specs=[td(), td(), pl.BlockSpec((tm * ROW_TILE, LANES), lambda i: (i, 0))],
        compiler_params=_params(("arbitrary",)),
        name="mix_out",
    )(o_f, o_b, bonus, g, y_f, y_b, proj, proj, proj, x2d,
      lp["ln_w"], lp["ln_b"], lp["ret_ln_w"], lp["ret_ln_b"], vecs["npm"], vecs["npf"], vecs["g1"], vecs["sc2"],
      vecs["sh2"], lp["w_a"], lp["w_b"], lp["w_o"], bd)


def _router_kernel(h_ref, rw_ref, bias_ref, sel_ref, wts_ref, rank_ref, cnt_ref, cnt_s, *, tm):
    step = pl.program_id(0)

    @pl.when(step == 0)
    def _():
        cnt_s[...] = jnp.zeros_like(cnt_s)

    ne = N_EXPERTS
    scores = _sigmoid(_dot3(h_ref[...], rw_ref[...]))
    work = scores + bias_ref[...]
    lane = lax.broadcasted_iota(jnp.int32, (tm, ne), 1).astype(F32)
    idxs = []
    vals = []
    for _ in range(TOP_K):
        m = jnp.max(work, axis=-1, keepdims=True)
        idx = jnp.min(jnp.where(work == m, lane, float(ne)), axis=-1, keepdims=True)
        oh = lane == idx
        vals.append(jnp.sum(jnp.where(oh, scores, 0.0), axis=-1, keepdims=True))
        idxs.append(idx)
        work = jnp.where(oh, -jnp.inf, work)
    sel_f = jnp.concatenate(idxs, axis=1)
    s_sel = jnp.concatenate(vals, axis=1)
    wts_ref[...] = s_sel / jnp.sum(s_sel, axis=1, keepdims=True) * ROUTED_SCALE
    sel_ref[...] = sel_f.astype(jnp.int32)

    hit = work == -jnp.inf
    r_i = lax.broadcasted_iota(jnp.int32, (tm, tm), 0)
    c_i = lax.broadcasted_iota(jnp.int32, (tm, tm), 1)
    before = _bdot((r_i > c_i).astype(BF16), hit.astype(BF16)) + cnt_s[...]
    ranks = [jnp.sum(jnp.where(lane == idxs[k], before, 0.0), axis=-1, keepdims=True) for k in range(TOP_K)]
    rank_ref[...] = jnp.concatenate(ranks, axis=1).astype(jnp.int32)
    cnt = cnt_s[...] + jnp.sum(hit.astype(F32), axis=0, keepdims=True)
    cnt_s[...] = cnt
    cnt_ref[...] = cnt.astype(jnp.int32)


def _router(h2, router_w, router_bias):
    t_len = h2.shape[0]
    tm = min(t_len, 256)
    kern = functools.partial(_router_kernel, tm=tm)
    tk = lambda: pl.BlockSpec((tm, TOP_K), lambda i: (i, 0))
    return pl.pallas_call(
        kern,
        out_shape=[jax.ShapeDtypeStruct((t_len, TOP_K), jnp.int32), jax.ShapeDtypeStruct((t_len, TOP_K), F32),
                   jax.ShapeDtypeStruct((t_len, TOP_K), jnp.int32), jax.ShapeDtypeStruct((1, N_EXPERTS), jnp.int32)],
        grid=(t_len // tm,),
        in_specs=[pl.BlockSpec((tm, D_MODEL), lambda i: (i, 0)),
                  pl.BlockSpec((D_MODEL, N_EXPERTS), lambda i: (0, 0)),
                  pl.BlockSpec((1, N_EXPERTS), lambda i: (0, 0))],
        out_specs=[tk(), tk(), tk(), pl.BlockSpec((1, N_EXPERTS), lambda i: (0, 0))],
        scratch_shapes=[pltpu.VMEM((1, N_EXPERTS), F32)],
        compiler_params=_params(("arbitrary",)),
        name="router",
    )(h2, router_w, router_bias.reshape(1, N_EXPERTS))


def _slot(sel_ref, rank_ref, start_ref, a):
    return start_ref[sel_ref[a]] + rank_ref[a]


def _dispatch_kernel(sel_ref, rank_ref, start_ref, cnt_ref, h_ref, xs_out, zrow, sem, zsem, *, tm):
    def row_copy(t, k):
        slot = _slot(sel_ref, rank_ref, start_ref, t * TOP_K + k)
        return pltpu.make_async_copy(_row_tile(h_ref, t), _row_tile(xs_out, slot), sem)

    def issue(t, carry):
        for k in range(TOP_K):
            row_copy(t, k).start(priority=k % 2)
        return carry

    lax.fori_loop(0, tm, issue, 0)

    def drain(t, carry):
        for k in range(TOP_K):
            row_copy(t, k).wait()
        return carry

    lax.fori_loop(0, tm, drain, 0)

    @pl.when(pl.program_id(0) == pl.num_programs(0) - 1)
    def _():
        zrow[...] = jnp.zeros_like(zrow)

        def zero_copy(slot):
            return pltpu.make_async_copy(zrow, _row_tile(xs_out, slot), zsem)

        def for_pad_slots(e0, fn):
            def per_expert(e, carry):
                n = cnt_ref[e]
                base = start_ref[e]
                n_pad = (n + MOE_BLOCK - 1) // MOE_BLOCK * MOE_BLOCK

                def body(j, c):
                    fn(zero_copy(base + j))
                    return c

                lax.fori_loop(n, n_pad, body, 0)
                return carry

            lax.fori_loop(e0, e0 + ZERO_PAD_GROUP, per_expert, 0)

        def per_group(gi, carry):
            e0 = gi * ZERO_PAD_GROUP
            for_pad_slots(e0, lambda cp: cp.start())
            for_pad_slots(e0, lambda cp: cp.wait())
            return carry

        lax.fori_loop(0, N_EXPERTS // ZERO_PAD_GROUP, per_group, 0)


def _dispatch(h3, sel_flat, rank_flat, pad_start, counts, n_slots):
    t_len = h3.shape[0] // ROW_TILE
    tm = min(t_len, 256)
    kern = functools.partial(_dispatch_kernel, tm=tm)
    smem_blk = lambda: pl.BlockSpec((tm * TOP_K,), lambda i: (i,), memory_space=pltpu.SMEM)
    smem_all = lambda: pl.BlockSpec((N_EXPERTS,), lambda i: (0,), memory_space=pltpu.SMEM)
    return pl.pallas_call(
        kern,
        out_shape=jax.ShapeDtypeStruct((n_slots * ROW_TILE, LANES), F32),
        grid=(t_len // tm,),
        in_specs=[smem_blk(), smem_blk(), smem_all(), smem_all(),
                  pl.BlockSpec((tm * ROW_TILE, LANES), lambda i: (i, 0))],
        out_specs=pl.BlockSpec(memory_space=pl.ANY),
        scratch_shapes=[pltpu.VMEM((ROW_TILE, LANES), F32), pltpu.SemaphoreType.DMA(()),
                        pltpu.SemaphoreType.DMA(())],
        compiler_params=_params(("arbitrary",)),
        name="dispatch",
    )(sel_flat, rank_flat, pad_start, counts, h3)


def _expert_kernel(be_ref, nu_ref, xs_ref, wg_ref, wu_ref, wd_ref, ys_ref, wg_s, wu_s, wd_s):
    b = pl.program_id(0)
    used = b < nu_ref[0]
    new_expert = (b == 0) | (be_ref[b] != be_ref[jnp.maximum(b - 1, 0)])

    @pl.when(used & new_expert)
    def _():
        wg_s[...] = _bf(wg_ref[0])
        wu_s[...] = _bf(wu_ref[0])
        wd_s[...] = _bf(wd_ref[0])

    @pl.when(used)
    def _():
        xb = _bf(_tiles_to_rows(xs_ref, MOE_BLOCK))
        gate = _bdot(xb, wg_s[...])
        up = _bdot(xb, wu_s[...])
        act = gate * _sigmoid(gate) * up
        _rows_to_tiles(_bdot(_bf(act), wd_s[...]), ys_ref)

    @pl.when(jnp.logical_not(used))
    def _():
        ys_ref[...] = jnp.zeros_like(ys_ref)


def _experts(xs, block_e, n_used, w_gate, w_up, w_down):
    n_slots = xs.shape[0] // ROW_TILE
    n_blocks = n_slots // MOE_BLOCK
    rows = (MOE_BLOCK * ROW_TILE, LANES)
    grid_spec = pltpu.PrefetchScalarGridSpec(
        num_scalar_prefetch=2,
        grid=(n_blocks,),
        in_specs=[
            pl.BlockSpec(rows, lambda b, be, nu: (jnp.where(b < nu[0], b, 0), 0)),
            pl.BlockSpec((1, D_MODEL, EXPERT_FF), lambda b, be, nu: (be[b], 0, 0)),
            pl.BlockSpec((1, D_MODEL, EXPERT_FF), lambda b, be, nu: (be[b], 0, 0)),
            pl.BlockSpec((1, EXPERT_FF, D_MODEL), lambda b, be, nu: (be[b], 0, 0)),
        ],
        out_specs=pl.BlockSpec(rows, lambda b, be, nu: (b, 0)),
        scratch_shapes=[pltpu.VMEM((D_MODEL, EXPERT_FF), BF16), pltpu.VMEM((D_MODEL, EXPERT_FF), BF16),
                        pltpu.VMEM((EXPERT_FF, D_MODEL), BF16)],
    )
    return pl.pallas_call(
        _expert_kernel,
        out_shape=jax.ShapeDtypeStruct((n_slots * ROW_TILE, LANES), F32),
        grid_spec=grid_spec,
        compiler_params=_params(("arbitrary",)),
        name="experts",
    )(block_e, n_used, xs, w_gate, w_up, w_down)


def _combine_kernel(sel_ref, rank_ref, start_ref, ys_ref, wts_ref, h_ref, x1_ref, sg_ref, su_ref, sd_ref, npo, g2_ref,
                    o_ref, buf, sem, *, tm):
    def row_copy(t, k):
        slot = _slot(sel_ref, rank_ref, start_ref, t * TOP_K + k)
        return pltpu.make_async_copy(_row_tile(ys_ref, slot), _row_tile(buf, t, (k,)), sem)

    def issue(t, carry):
        for k in range(TOP_K):
            row_copy(t, k).start(priority=k % 2)
        return carry

    lax.fori_loop(0, tm, issue, 0)

    hb = _bf(h_ref[...])
    gate = _bdot(hb, sg_ref[...])
    up = _bdot(hb, su_ref[...])
    shared = _bdot(_bf(gate * _sigmoid(gate) * up), sd_ref[...])

    def drain(t, carry):
        for k in range(TOP_K):
            row_copy(t, k).wait()
        return carry

    lax.fori_loop(0, tm, drain, 0)

    wts = wts_ref[...]
    routed = _tiles_to_rows(buf, tm, (0,)) * wts[:, 0:1]
    for k in range(1, TOP_K):
        routed = routed + _tiles_to_rows(buf, tm, (k,)) * wts[:, k:k + 1]
    o_ref[...] = x1_ref[...] + g2_ref[...] * (_rms(routed + shared) * npo[...])


def _combine(ys, sel_flat, rank_flat, pad_start, wts, h2, x1, sg, su, sd, npo, g2):
    t_len = h2.shape[0]
    tm = min(t_len, 256)
    d = D_MODEL
    kern = functools.partial(_combine_kernel, tm=tm)
    smem_blk = lambda: pl.BlockSpec((tm * TOP_K,), lambda i: (i,), memory_space=pltpu.SMEM)
    td = lambda: pl.BlockSpec((tm, d), lambda i: (i, 0))
    vd = lambda: pl.BlockSpec((1, d), lambda i: (0, 0))
    return pl.pallas_call(
        kern,
        out_shape=jax.ShapeDtypeStruct((t_len, d), F32),
        grid=(t_len // tm,),
        in_specs=[smem_blk(), smem_blk(),
                  pl.BlockSpec((N_EXPERTS,), lambda i: (0,), memory_space=pltpu.SMEM),
                  pl.BlockSpec(memory_space=pl.ANY),
                  pl.BlockSpec((tm, TOP_K), lambda i: (i, 0)),
                  td(), td(),
                  pl.BlockSpec((d, SHARED_FF), lambda i: (0, 0)), pl.BlockSpec((d, SHARED_FF), lambda i: (0, 0)),
                  pl.BlockSpec((SHARED_FF, d), lambda i: (0, 0)), vd(), vd()],
        out_specs=td(),
        scratch_shapes=[pltpu.VMEM((TOP_K, tm * ROW_TILE, LANES), F32), pltpu.SemaphoreType.DMA(())],
        compiler_params=_params(("arbitrary",)),
        name="combine",
    )(sel_flat, rank_flat, pad_start, ys, wts, h2, x1, sg, su, sd, npo, g2)


def _block_diag_ones(width, group):
    idx = np.arange(width) // group
    return jnp.asarray(idx[:, None] == idx[None, :], dtype=BF16)


def _rope_tables(t_len):
    pos = np.arange(t_len)
    rows = (pos // GRID_W).astype(np.float32)
    cols = (pos % GRID_W).astype(np.float32)
    quarter = B_QK_DIM // 4
    inv_freq = jnp.asarray(ROPE_BASE, F32) ** (-jnp.arange(quarter, dtype=F32) / quarter)
    ang_r = jnp.asarray(rows)[:, None] * inv_freq
    ang_c = jnp.asarray(cols)[:, None] * inv_freq
    cr, sr, cc, sc = jnp.cos(ang_r), jnp.sin(ang_r), jnp.cos(ang_c), jnp.sin(ang_c)
    return jnp.concatenate([cr, cr, cc, cc], axis=1), jnp.concatenate([-sr, sr, -sc, sc], axis=1)


def _token_mixer(x2d, g_pre, sc, sh, lp, cos, sin, states, bd):
    proj = _in_proj(x2d, g_pre, sc, sh, lp["w_in"], lp["mu"])
    prep = _rwkv_prep(proj, lp, bd)
    o_f, o_b, s_fin = _rwkv_scan(proj, prep, states[0])
    y_f, y_b, r_fin = _retention(proj, cos, sin, lp["lgt"], states[1])
    return proj, prep, (o_f, o_b), (y_f, y_b), (s_fin, r_fin)


def kernel(x, c, ctx, c_ctx, w_mod, b_mod, norm_pre_mix, norm_post_mix, norm_pre_ffn, norm_post_ffn, w_in, shift_mu,
           rwkv_w0, rwkv_w_up, rwkv_a0, rwkv_a_up, rwkv_g_up, rwkv_k_k, rwkv_k_a, rwkv_r_k, rwkv_ln_w, rwkv_ln_b,
           w_branch_a, ret_decay_logit, ret_ln_w, ret_ln_b, w_branch_b, w_out, router_w, router_bias, exp_w_gate,
           exp_w_up, exp_w_down, sh_w_gate, sh_w_up, sh_w_down):
    d = D_MODEL
    assert x.shape[0] == 1 and w_in.shape[0] == 1, "single batch element, single layer"
    t_len = x.shape[1]
    x2d = x.reshape(t_len, d)
    ctx2d = ctx.reshape(ctx.shape[1], d)
    row = lambda a: a.reshape(1, -1)

    cs = jnp.zeros((8, d), F32).at[0].set(c[0]).at[1].set(c_ctx)
    mod = _modulation(cs, w_mod[0], b_mod[0])
    sh1, sc1, g1, sh2, sc2, g2 = [mod[0:1, i * d:(i + 1) * d] for i in range(6)]
    csh1, csc1 = mod[1:2, 0:d], mod[1:2, d:2 * d]

    w_in_p = jnp.concatenate(
        [w_in[0][:, :SHIFT_COLS], jnp.zeros((d, SHIFT_PAD - SHIFT_COLS), F32), w_in[0][:, SHIFT_COLS:]], axis=1)
    lp = {
        "w_in": w_in_p.astype(BF16),
        "mu": jnp.pad(shift_mu[0], ((0, 0), (0, SHIFT_PAD - SHIFT_COLS))),
        "k_k": row(rwkv_k_k[0]), "k_a": row(rwkv_k_a[0]), "r_k": row(rwkv_r_k[0]),
        "w0": rwkv_w0[0], "w_up": rwkv_w_up[0], "a0": rwkv_a0[0], "a_up": rwkv_a_up[0], "g_up": rwkv_g_up[0],
        "ln_w": row(rwkv_ln_w[0]), "ln_b": row(rwkv_ln_b[0]),
        "ret_ln_w": row(ret_ln_w[0]), "ret_ln_b": row(ret_ln_b[0]),
        "lgt": jnp.broadcast_to(ret_decay_logit[0].reshape(2 * B_HEADS, 1), (2 * B_HEADS, 128)),
        "w_a": w_branch_a[0].astype(BF16), "w_b": w_branch_b[0].astype(BF16), "w_o": w_out[0].astype(BF16),
    }
    bd = _block_diag_ones(A_WIDTH, A_HEAD_DIM)
    g_pre = row(norm_pre_mix[0])

    t_ctx = ctx2d.shape[0]
    zero_states = (jnp.zeros(RWKV_STATE_SHAPE, F32),
                   jnp.zeros((2, B_HEADS, B_QK_DIM, B_V_DIM), F32))
    ones = jnp.ones((t_ctx, B_QK_DIM), F32)
    *_, ctx_states = _token_mixer(ctx2d, g_pre, csc1, csh1, lp, ones, jnp.zeros_like(ones), zero_states, bd)

    cos, sin = _rope_tables(t_len)
    proj, prep, (o_f, o_b), (y_f, y_b), _ = _token_mixer(x2d, g_pre, sc1, sh1, lp, cos, sin, ctx_states, bd)
    vecs = {"npm": row(norm_post_mix[0]), "npf": row(norm_pre_ffn[0]), "g1": g1, "sc2": sc2, "sh2": sh2}
    x1, h2, h3 = _mix_out(x2d, proj, o_f, o_b, prep[7], prep[8], y_f, y_b, lp, vecs, bd)

    sel, wts, rank, counts = _router(h2, router_w[0], router_bias[0])
    counts = counts.reshape(N_EXPERTS)
    padded = (counts + MOE_BLOCK - 1) // MOE_BLOCK * MOE_BLOCK
    pad_end = jnp.cumsum(padded)
    pad_start = (pad_end - padded).astype(jnp.int32)
    n_assign = t_len * TOP_K
    n_blocks = (n_assign + N_EXPERTS * (MOE_BLOCK - 1) + MOE_BLOCK - 1) // MOE_BLOCK
    block_start = jnp.arange(n_blocks, dtype=jnp.int32) * MOE_BLOCK
    block_e = jnp.minimum(jnp.sum(pad_end[None, :] <= block_start[:, None], axis=1), N_EXPERTS - 1).astype(jnp.int32)
    n_used = (pad_end[-1:] // MOE_BLOCK).astype(jnp.int32)
    sel_flat = sel.reshape(n_assign)
    rank_flat = rank.reshape(n_assign)

    xs = _dispatch(h3, sel_flat, rank_flat, pad_start, counts, n_blocks * MOE_BLOCK)
    ys = _experts(xs, block_e, n_used, exp_w_gate[0], exp_w_up[0], exp_w_down[0])
    out = _combine(ys, sel_flat, rank_flat, pad_start, wts, h2, x1, sh_w_gate[0].astype(BF16),
                   sh_w_up[0].astype(BF16), sh_w_down[0].astype(BF16), row(norm_post_ffn[0]), g2)
    return out.reshape(x.shape)
```

```python
import functools

import jax
import jax.numpy as jnp
import numpy as np
from jax import lax
from jax.experimental import pallas as pl
from jax.experimental.pallas import tpu as pltpu

F32 = jnp.float32
BF16 = jnp.bfloat16

D_MODEL = 1024
GRID_W = 64
NORM_EPS = 1e-6

A_HEAD_DIM = 64
A_WIDTH = D_MODEL // 2
A_HEADS = A_WIDTH // A_HEAD_DIM
W_LORA = 64
ICLR_LORA = 64
G_LORA = 128
RWKV_GN_EPS = 64e-5

B_HEADS = 4
B_QK_WIDTH = D_MODEL // 2
B_V_WIDTH = D_MODEL
B_QK_DIM = B_QK_WIDTH // B_HEADS
B_V_DIM = B_V_WIDTH // B_HEADS
RET_CHUNK = 128
RET_GN_EPS = 1e-5
ROPE_BASE = 10000.0

SHIFT_COLS = 3 * A_WIDTH + W_LORA + ICLR_LORA + G_LORA
LORA_COLS = W_LORA + ICLR_LORA + G_LORA
SHIFT_PAD = 2048
PROJ_COLS = SHIFT_PAD + 2 * B_QK_WIDTH + 2 * B_V_WIDTH + 2 * D_MODEL

N_EXPERTS = 256
TOP_K = 8
EXPERT_FF = D_MODEL // 4
SHARED_FF = D_MODEL // 4
ROUTED_SCALE = 2.5
MOE_BLOCK = 256
ZERO_PAD_GROUP = 16

RWKV_CHUNK = 64
RWKV_STATE_SHAPE = (2, A_HEADS // 2, A_HEAD_DIM, 2 * A_HEAD_DIM)
PROJ_TN = 1024
VMEM_LIMIT = 48 * 1024 * 1024


def _params(sem):
    return pltpu.CompilerParams(dimension_semantics=sem, vmem_limit_bytes=VMEM_LIMIT)


def _bf(a):
    return a.astype(BF16)


def _bdot(a, b):
    return jnp.dot(a, b, preferred_element_type=F32)


def _dot_nt(a, b):
    return lax.dot_general(a, b, (((1,), (1,)), ((), ())), preferred_element_type=F32)


def _dot_tn(a, b):
    return lax.dot_general(a, b, (((0,), (0,)), ((), ())), preferred_element_type=F32)


def _split(a, n):
    out = []
    rem = a
    for _ in range(n):
        p = _bf(rem)
        out.append(p)
        rem = rem - p.astype(F32)
    return out


def _dot_split_lhs(a, b_bf, n=3):
    acc = None
    for p in _split(a, n):
        t = _bdot(p, b_bf)
        acc = t if acc is None else acc + t
    return acc


def _dot_split_rhs(a_bf, b, n=3):
    acc = None
    for p in _split(b, n):
        t = _bdot(a_bf, p)
        acc = t if acc is None else acc + t
    return acc


def _dot3(a, b):
    ah, al = _split(a, 2)
    bh, bl = _split(b, 2)
    return _bdot(ah, bh) + (_bdot(ah, bl) + _bdot(al, bh))


def _sigmoid(x):
    return 1.0 / (1.0 + jnp.exp(-x))


def _softplus(x):
    return jnp.maximum(x, 0.0) + jnp.log1p(jnp.exp(-jnp.abs(x)))


def _rms(x):
    return x * lax.rsqrt(jnp.mean(x * x, axis=-1, keepdims=True) + NORM_EPS)


LANES = 128
ROW_TILE = D_MODEL // LANES


def _rows_to_tiles(x2d, ref, lead=()):
    n = x2d.shape[0]
    for j in range(ROW_TILE):
        ref[lead + (pl.ds(j, n, stride=ROW_TILE), slice(None))] = x2d[:, j * LANES:(j + 1) * LANES]


def _tiles_to_rows(ref, n, lead=()):
    return jnp.concatenate(
        [ref[lead + (pl.ds(j, n, stride=ROW_TILE), slice(None))] for j in range(ROW_TILE)], axis=1)


def _row_tile(ref, r, lead=()):
    return ref.at[lead + (pl.ds(pl.multiple_of(r * ROW_TILE, ROW_TILE), ROW_TILE),)]


def _mod_kernel(cs_ref, w_ref, b_ref, o_ref):
    cs = cs_ref[...]
    s = cs * _sigmoid(cs)
    o_ref[...] = _dot3(s, w_ref[...]) + b_ref[...]


def _modulation(cs, w_mod, b_mod):
    n_out = w_mod.shape[1]
    tn = 1536
    return pl.pallas_call(
        _mod_kernel,
        out_shape=jax.ShapeDtypeStruct((8, n_out), F32),
        grid=(n_out // tn,),
        in_specs=[
            pl.BlockSpec((8, D_MODEL), lambda j: (0, 0)),
            pl.BlockSpec((D_MODEL, tn), lambda j: (0, j)),
            pl.BlockSpec((1, tn), lambda j: (0, j)),
        ],
        out_specs=pl.BlockSpec((8, tn), lambda j: (0, j)),
        compiler_params=_params(("arbitrary",)),
        name="modulation",
    )(cs, w_mod, b_mod.reshape(1, n_out))


def _in_proj_kernel(x_ref, xp_ref, xn_ref, g_ref, sc_ref, sh_ref, w_ref, mu_ref, o_ref, h_s, hh_s, *, tm, n_shift):
    i = pl.program_id(0)
    j = pl.program_id(1)
    n_i = pl.num_programs(0)

    def norm_mod(xv):
        return (_rms(xv) * g_ref[...]) * (1.0 + sc_ref[...]) + sh_ref[...]

    @pl.when(j == 0)
    def _():
        h_s[...] = _bf(norm_mod(x_ref[...]))
        hp = jnp.where(i > 0, norm_mod(xp_ref[...]), 0.0)
        hn = jnp.where(i < n_i - 1, norm_mod(xn_ref[...]), 0.0)
        hh_s[0:8, :] = hp
        hh_s[8:16, :] = hn

    w = w_ref[...]
    p = _bdot(h_s[...], w)

    @pl.when(j < n_shift)
    def _():
        ph = _bdot(_bf(hh_s[...]), w)
        row = lax.broadcasted_iota(jnp.int32, p.shape, 0)
        prev = jnp.where(row == 0, ph[7:8, :], pltpu.roll(p, 1, 0))
        nxt = jnp.where(row == tm - 1, ph[8:9, :], pltpu.roll(p, tm - 1, 0))
        mu = mu_ref[...]
        o_ref[...] = p + mu[0:1, :] * (prev - p) + mu[1:2, :] * (nxt - p)

    @pl.when(j >= n_shift)
    def _():
        o_ref[...] = p


def _in_proj(x2d, g, sc, sh, w_bf, mu_pad):
    t_len = x2d.shape[0]
    tm = min(t_len, 1024)
    tn = PROJ_TN
    n_shift = SHIFT_PAD // tn
    tb8 = tm // 8
    nb8 = t_len // 8
    kern = functools.partial(_in_proj_kernel, tm=tm, n_shift=n_shift)
    vec = lambda: pl.BlockSpec((1, D_MODEL), lambda i, j: (0, 0))
    return pl.pallas_call(
        kern,
        out_shape=jax.ShapeDtypeStruct((t_len, PROJ_COLS), F32),
        grid=(t_len // tm, PROJ_COLS // tn),
        in_specs=[
            pl.BlockSpec((tm, D_MODEL), lambda i, j: (i, 0)),
            pl.BlockSpec((8, D_MODEL), lambda i, j: (jnp.maximum(i * tb8 - 1, 0), 0)),
            pl.BlockSpec((8, D_MODEL), lambda i, j: (jnp.minimum((i + 1) * tb8, nb8 - 1), 0)),
            vec(), vec(), vec(),
            pl.BlockSpec((D_MODEL, tn), lambda i, j: (0, j)),
            pl.BlockSpec((2, tn), lambda i, j: (0, jnp.minimum(j, n_shift - 1))),
        ],
        out_specs=pl.BlockSpec((tm, tn), lambda i, j: (i, j)),
        scratch_shapes=[pltpu.VMEM((tm, D_MODEL), BF16), pltpu.VMEM((16, D_MODEL), F32)],
        compiler_params=_params(("arbitrary", "arbitrary")),
        name="in_proj",
    )(x2d, x2d, x2d, g, sc, sh, w_bf, mu_pad)


def _rwkv_prep_kernel(r_ref, k_ref, v_ref, lora_ref, kk_w, ka_w, rk_w, w0_ref, wup_ref, a0_ref, aup_ref, gup_ref,
                      bd_ref, kk_o, lw0_o, lw1_o, kd0_o, kd1_o, b0_o, b1_o, bonus_o, g_o):
    r = r_ref[...]
    k = k_ref[...]
    v = v_ref[...]
    lora = lora_ref[...]
    xw = lora[:, 0:W_LORA]
    xa = lora[:, W_LORA:W_LORA + ICLR_LORA]
    xg = lora[:, W_LORA + ICLR_LORA:]
    bd = bd_ref[...]

    kk = k * kk_w[...]
    nrm = jnp.sqrt(_dot_split_lhs(kk * kk, bd))
    kk = kk / jnp.maximum(nrm, 1e-12)
    kk_o[...] = kk

    tw = jnp.tanh(xw)
    lw_outs = (lw0_o, lw1_o)
    kd_outs = (kd0_o, kd1_o)
    b_outs = (b0_o, b1_o)
    kd_sum = None
    for d in range(2):
        z = w0_ref[d:d + 1, :] + _dot3(tw, wup_ref[d])
        w_log = -_softplus(-z) - 0.5
        lw_outs[d][...] = -jnp.exp(w_log)
        a_d = _sigmoid(a0_ref[d:d + 1, :] + _dot3(xa, aup_ref[d]))
        kd = k * (1.0 + (a_d - 1.0) * ka_w[...])
        kd_outs[d][...] = kd
        b_outs[d][...] = kk * a_d
        kd_sum = kd if kd_sum is None else kd_sum + kd
    g_o[...] = _dot3(_sigmoid(xg), gup_ref[...])
    bonus_o[...] = _dot_split_lhs(r * kd_sum * rk_w[...], bd) * v


def _rwkv_prep(proj, lp, bd):
    t_len = proj.shape[0]
    tm = min(t_len, 512)
    aw = A_WIDTH
    col = lambda c: pl.BlockSpec((tm, aw), lambda i: (i, c))
    vec = lambda: pl.BlockSpec((1, aw), lambda i: (0, 0))
    full = lambda shp: pl.BlockSpec(shp, lambda i: (0,) * len(shp))
    outs = [jax.ShapeDtypeStruct((t_len, aw), F32)] * 9
    return pl.pallas_call(
        _rwkv_prep_kernel,
        out_shape=outs,
        grid=(t_len // tm,),
        in_specs=[
            col(0), col(1), col(2),
            pl.BlockSpec((tm, LORA_COLS), lambda i: (i, 3 * aw // LORA_COLS)),
            vec(), vec(), vec(),
            full((2, aw)), full((2, W_LORA, aw)), full((2, aw)), full((2, ICLR_LORA, aw)), full((G_LORA, aw)),
            full((aw, aw)),
        ],
        out_specs=[pl.BlockSpec((tm, aw), lambda i: (i, 0))] * 9,
        compiler_params=_params(("arbitrary",)),
        name="rwkv_prep",
    )(proj, proj, proj, proj, lp["k_k"], lp["k_a"], lp["r_k"], lp["w0"], lp["w_up"], lp["a0"], lp["a_up"],
      lp["g_up"], bd)


def _rwkv_scan_kernel(rf, vf, kkf, lwf, kdf, bf_, rb, vb, kkb, lwb, kdb, bb, s0_ref, of_ref, ob_ref, sfin_ref, s_s,
                      *, tb):
    step = pl.program_id(0)

    @pl.when(step == 0)
    def _():
        s_s[...] = s0_ref[...]

    c = RWKV_CHUNK
    n = A_HEAD_DIM
    assert c == n
    pw = 2 * n
    row = lax.broadcasted_iota(jnp.int32, (c, pw), 0)
    lane = lax.broadcasted_iota(jnp.int32, (c, pw), 1)
    col = lane % c
    first = lane < n
    eye = (row == col).astype(F32)
    r_c = lax.broadcasted_iota(jnp.int32, (c, c), 0)
    c_c = lax.broadcasted_iota(jnp.int32, (c, c), 1)
    n_ch = tb // c
    dirs = ((rf, vf, kkf, lwf, kdf, bf_, of_ref), (rb, vb, kkb, lwb, kdb, bb, ob_ref))
    pairs = range(A_HEADS // 2)
    psl = [slice(q * pw, (q + 1) * pw) for q in pairs]

    def bd(x):
        z = jnp.zeros_like(x)
        return jnp.concatenate([jnp.where(first, x, z), jnp.where(first, z, x)], axis=0)

    def chunk_terms(d, refs, ci):
        r_ref, v_ref, kk_ref, lw_ref, kd_ref, b_ref, _ = refs
        if d == 0:
            incl, strict, tri = row >= col, row > col, r_c >= c_c
        else:
            incl, strict, tri = row <= col, row < col, r_c <= c_c
        rows = slice(ci * c, (ci + 1) * c)
        lw = lw_ref[rows, :]
        r = r_ref[rows, :]
        kk = kk_ref[rows, :]
        kd = kd_ref[rows, :]
        b = b_ref[rows, :]
        cl = _dot_split_rhs(tri.astype(BF16), lw)
        cl_tot = cl[c - 1:c, :] if d == 0 else cl[0:1, :]
        e_neg = jnp.exp(-cl)
        e_end = jnp.exp(cl_tot - cl)
        g_tot = jnp.exp(cl_tot)
        rt = r * jnp.exp(cl)
        x_all = _bf(jnp.concatenate([-(kk * jnp.exp(cl - lw)), rt], axis=0))
        bt_all = _bf(b * e_neg)
        kt_all = _bf(kd * e_neg)
        return {
            "rows": rows, "strict": strict, "incl": incl,
            "x": [x_all[:, ps] for ps in psl],
            "z": [jnp.concatenate([bd(bt_all[:, ps]), bd(kt_all[:, ps])], axis=0) for ps in psl],
            "bd_bh": [bd(_bf(b * e_end)[:, ps]) for ps in psl],
            "bd_kh": [bd(_bf(kd * e_end)[:, ps]) for ps in psl],
            "bd_v": [bd(_bf(v_ref[rows, :])[:, ps]) for ps in psl],
            "rt": [rt[:, ps] for ps in psl],
            "g_tot": [g_tot[:, ps] for ps in psl],
        }

    chunks = [(d, cc) for cc in range(n_ch) for d in range(2)]
    prep = {(d, cc): chunk_terms(d, dirs[d], cc if d == 0 else n_ch - 1 - cc) for d, cc in chunks}
    units = [(k, q) for k in chunks for q in pairs]

    def per_unit(name, indexed=True):
        return [prep[k][name][q] if indexed else prep[k][name] for k, q in units]

    strict, incl = per_unit("strict", False), per_unit("incl", False)
    x_u, bd_bh, bd_kh, bd_v, rt_u = (per_unit(s) for s in ("x", "bd_bh", "bd_kh", "bd_v", "rt"))
    g = [_dot_nt(x, z) for x, z in zip(x_u, per_unit("z"))]
    a_ab = [jnp.where(s, gi[:c, :pw], 0.0) for s, gi in zip(strict, g)]
    a_rb = [_bf(jnp.where(i, gi[c:, :pw], 0.0)) for i, gi in zip(incl, g)]
    a_k = [_bf(jnp.concatenate([jnp.where(s, gi[:c, pw:], 0.0), jnp.where(i, gi[c:, pw:], 0.0)], axis=0))
           for s, i, gi in zip(strict, incl, g)]
    a2 = [_bdot(_bf(a), bd(_bf(a))) for a in a_ab]
    pa = [jnp.concatenate([eye + a, sq], axis=0) for a, sq in zip(a_ab, a2)]
    for _ in range(4):
        nxt = [_bdot(_bf(x), bd(_bf(x[c:]))) for x in pa]
        pa = [jnp.concatenate([x[:c] + y[:c], y[c:]], axis=0) for x, y in zip(pa, nxt)]
    p = [_bf(x[:c] + _bdot(_bf(x[:c]), bd(_bf(x[c:])))) for x in pa]
    vk = [_bdot(a, v) for a, v in zip(a_k, bd_v)]
    tw = [_bf(_bdot(pi, jnp.concatenate([bd(x[:c]), bd(_bf(vki[:c]))], axis=1)))
          for pi, vki, x in zip(p, vk, x_u)]
    bd_at = [bd(t[:, :pw]) for t in tw]
    bd_w = [bd(t[:, pw:]) for t in tw]
    mt = [_bf(_dot_tn(a, bh)) for a, bh in zip(bd_at, bd_bh)]
    ntf = [_dot_tn(jnp.concatenate([w, v], axis=0), jnp.concatenate([bh, kh], axis=0))
           for w, v, bh, kh in zip(bd_w, bd_v, bd_bh, bd_kh)]
    nt = [m[:n] + m[n:] for m in ntf]
    rw = [_bdot(a, jnp.concatenate([at, w], axis=1)) for a, at, w in zip(a_rb, bd_at, bd_w)]
    ry = [_bf(r + w[:, :pw]) for w, r in zip(rw, rt_u)]
    y0 = [w[:, pw:] + vki[c:] for w, vki in zip(rw, vk)]
    term = {u: vals for u, vals in zip(units, zip(mt, nt, ry, y0, per_unit("g_tot")))}

    state = [[s_s[d, q] for q in pairs] for d in range(2)]
    dq = [(d, q) for d in range(2) for q in pairs]
    for cc in range(n_ch):
        mt_c, nt_c, ry_c, y0_c, gt_c = zip(*[term[((d, cc), q)] for d, q in dq])
        s0 = [state[d][q] for d, q in dq]
        ys = [_dot_nt(ry_c[i], bd(_bf(s0[i]))) + y0_c[i] for i in range(len(dq))]
        upd = [_bdot(jnp.concatenate(_split(s0[i], 2), axis=0), mt_c[i]) for i in range(len(dq))]
        for i, (d, q) in enumerate(dq):
            state[d][q] = s0[i] * gt_c[i] + (upd[i][:n] + upd[i][n:]) + nt_c[i]
        for d in range(2):
            y_d = [ys[i] for i, (dd, _) in enumerate(dq) if dd == d]
            dirs[d][6][prep[(d, cc)]["rows"], :] = jnp.concatenate(y_d, axis=1)
    for d in range(2):
        for q in pairs:
            s_s[d, q] = state[d][q]

    @pl.when(step == pl.num_programs(0) - 1)
    def _():
        sfin_ref[...] = s_s[...]


def _rwkv_scan(proj, prep, s0):
    kk, lw0, lw1, kd0, kd1, b0, b1 = prep[:7]
    t_len = proj.shape[0]
    tb = 2 * RWKV_CHUNK
    nb = t_len // tb
    aw = A_WIDTH
    fwd = lambda c: pl.BlockSpec((tb, aw), lambda i: (i, c))
    bwd = lambda c: pl.BlockSpec((tb, aw), lambda i: (nb - 1 - i, c))
    st = pl.BlockSpec(RWKV_STATE_SHAPE, lambda i: (0, 0, 0, 0))
    kern = functools.partial(_rwkv_scan_kernel, tb=tb)
    return pl.pallas_call(
        kern,
        out_shape=[jax.ShapeDtypeStruct((t_len, aw), F32), jax.ShapeDtypeStruct((t_len, aw), F32),
                   jax.ShapeDtypeStruct(RWKV_STATE_SHAPE, F32)],
        grid=(nb,),
        in_specs=[fwd(0), fwd(2), fwd(0), fwd(0), fwd(0), fwd(0),
                  bwd(0), bwd(2), bwd(0), bwd(0), bwd(0), bwd(0), st],
        out_specs=[fwd(0), bwd(0), st],
        scratch_shapes=[pltpu.VMEM(RWKV_STATE_SHAPE, F32)],
        compiler_params=_params(("arbitrary",)),
        name="rwkv_scan",
    )(proj, proj, kk, lw0, kd0, b0, proj, proj, kk, lw1, kd1, b1, s0)


def _retention_kernel(qf, kf, vf, cosf, sinf, qb, kb, vb, cosb, sinb, lgt_ref, r0_ref, yf_ref, yb_ref, rfin_ref, r_s):
    step = pl.program_id(0)

    @pl.when(step == 0)
    def _():
        r_s[...] = r0_ref[...]

    c = RET_CHUNK
    dk = B_QK_DIM
    dv = B_V_DIM
    lg_all = -_softplus(-lgt_ref[...])
    rowf = lax.broadcasted_iota(jnp.int32, (c, c), 0).astype(F32)
    colf = lax.broadcasted_iota(jnp.int32, (c, c), 1).astype(F32)
    lane = lax.broadcasted_iota(jnp.int32, (c, dk), 1)
    first_half = (lane % 64) < 32
    dirs = ((qf, kf, vf, cosf, sinf, yf_ref), (qb, kb, vb, cosb, sinb, yb_ref))
    for d, (q_ref, k_ref, v_ref, cos_ref, sin_ref, y_ref) in enumerate(dirs):
        cos = cos_ref[...]
        sin = sin_ref[...]
        diff = (rowf - colf) if d == 0 else (colf - rowf)
        pos = rowf if d == 0 else (c - 1.0) - rowf
        for h in range(B_HEADS):
            lg = lg_all[d * B_HEADS + h:d * B_HEADS + h + 1, :]
            qh = q_ref[:, h * dk:(h + 1) * dk]
            kh = k_ref[:, h * dk:(h + 1) * dk] * (dk ** -0.5)

            def rope(xv):
                swapped = jnp.where(first_half, pltpu.roll(xv, dk - 32, 1), pltpu.roll(xv, 32, 1))
                return xv * cos + swapped * sin

            qh = rope(qh)
            kh = rope(kh)
            vh = _bf(v_ref[:, h * dv:(h + 1) * dv])
            dmask = jnp.where(diff >= 0.0, jnp.exp(lg * jnp.maximum(diff, 0.0)), 0.0)
            scores = _dot_nt(_bf(qh), _bf(kh)) * dmask
            inner = _bdot(_bf(scores), vh)
            xi = jnp.exp(lg * (pos + 1.0))
            zeta = jnp.exp(lg * ((c - 1.0) - pos))
            r_prev = r_s[d, h]
            cross = _bdot(_bf(qh * xi), _bf(r_prev))
            y_ref[:, h * dv:(h + 1) * dv] = inner + cross
            kv = _dot_tn(_bf(kh * zeta), vh)
            g_chunk = jnp.exp(lg * float(c))
            r_s[d, h] = jnp.concatenate([g_chunk, g_chunk], axis=1) * r_prev + kv

    @pl.when(step == pl.num_programs(0) - 1)
    def _():
        rfin_ref[...] = r_s[...]


def _retention(proj, cos, sin, lgt, r0):
    t_len = proj.shape[0]
    c = RET_CHUNK
    nc = t_len // c
    qw = B_QK_WIDTH
    vw = B_V_WIDTH
    fq = lambda col: pl.BlockSpec((c, qw), lambda i: (i, col))
    bq = lambda col: pl.BlockSpec((c, qw), lambda i: (nc - 1 - i, col))
    st = pl.BlockSpec((2, B_HEADS, B_QK_DIM, B_V_DIM), lambda i: (0, 0, 0, 0))
    q_col = SHIFT_PAD // qw
    v_col = (SHIFT_PAD + 2 * qw) // vw
    return pl.pallas_call(
        _retention_kernel,
        out_shape=[jax.ShapeDtypeStruct((t_len, vw), F32), jax.ShapeDtypeStruct((t_len, vw), F32),
                   jax.ShapeDtypeStruct((2, B_HEADS, B_QK_DIM, B_V_DIM), F32)],
        grid=(nc,),
        in_specs=[
            fq(q_col), fq(q_col + 1), pl.BlockSpec((c, vw), lambda i: (i, v_col)),
            pl.BlockSpec((c, B_QK_DIM), lambda i: (i, 0)), pl.BlockSpec((c, B_QK_DIM), lambda i: (i, 0)),
            bq(q_col), bq(q_col + 1), pl.BlockSpec((c, vw), lambda i: (nc - 1 - i, v_col)),
            pl.BlockSpec((c, B_QK_DIM), lambda i: (nc - 1 - i, 0)),
            pl.BlockSpec((c, B_QK_DIM), lambda i: (nc - 1 - i, 0)),
            pl.BlockSpec((2 * B_HEADS, 128), lambda i: (0, 0)), st,
        ],
        out_specs=[pl.BlockSpec((c, vw), lambda i: (i, 0)), pl.BlockSpec((c, vw), lambda i: (nc - 1 - i, 0)), st],
        scratch_shapes=[pltpu.VMEM((2, B_HEADS, B_QK_DIM, B_V_DIM), F32)],
        compiler_params=_params(("arbitrary",)),
        name="retention",
    )(proj, proj, proj, cos, sin, proj, proj, proj, cos, sin, lgt, r0)


def _mix_out_kernel(of_ref, ob_ref, bonus_ref, g_ref, yf_ref, yb_ref, gb_ref, ga_ref, gbb_ref, x_ref,
                    alnw, alnb, rlnw, rlnb, npm, npf, g1_ref, sc2_ref, sh2_ref, wa_ref, wb_ref, wo_ref, bd_ref,
                    x1_ref, h2_ref, h3_ref):
    bd = bd_ref[...]
    o = of_ref[...] + ob_ref[...]
    mu = _dot_split_lhs(o, bd) * (1.0 / A_HEAD_DIM)
    oc = o - mu
    var = _dot_split_lhs(oc * oc, bd) * (1.0 / A_HEAD_DIM)
    ya = oc * lax.rsqrt(var + RWKV_GN_EPS) * alnw[...] + alnb[...] + bonus_ref[...]
    ya = _bdot(_bf(ya * g_ref[...]), wa_ref[...])

    y = yf_ref[...] + yb_ref[...]
    parts = []
    for h in range(B_HEADS):
        seg = y[:, h * B_V_DIM:(h + 1) * B_V_DIM]
        m = jnp.mean(seg, axis=-1, keepdims=True)
        sc = seg - m
        vr = jnp.mean(sc * sc, axis=-1, keepdims=True)
        parts.append(sc * lax.rsqrt(vr + RET_GN_EPS))
    yn = jnp.concatenate(parts, axis=1) * rlnw[...] + rlnb[...]
    gb = gb_ref[...]
    yb = _bdot(_bf(yn * (gb * _sigmoid(gb))), wb_ref[...])

    merged = _sigmoid(ga_ref[...]) * ya + _sigmoid(gbb_ref[...]) * yb
    mix = _bdot(_bf(merged), wo_ref[...])
    x1 = x_ref[...] + g1_ref[...] * (_rms(mix) * npm[...])
    x1_ref[...] = x1
    h2 = (_rms(x1) * npf[...]) * (1.0 + sc2_ref[...]) + sh2_ref[...]
    h2_ref[...] = h2
    _rows_to_tiles(h2, h3_ref)


def _mix_out(x2d, proj, o_f, o_b, bonus, g, y_f, y_b, lp, vecs, bd):
    t_len = x2d.shape[0]
    tm = min(t_len, 256)
    aw = A_WIDTH
    d = D_MODEL
    ta = lambda: pl.BlockSpec((tm, aw), lambda i: (i, 0))
    td = lambda: pl.BlockSpec((tm, d), lambda i: (i, 0))
    pc = lambda c: pl.BlockSpec((tm, d), lambda i: (i, c))
    va = lambda: pl.BlockSpec((1, aw), lambda i: (0, 0))
    vd = lambda: pl.BlockSpec((1, d), lambda i: (0, 0))
    full = lambda shp: pl.BlockSpec(shp, lambda i: (0, 0))
    gcol = (SHIFT_PAD + 2 * B_QK_WIDTH + B_V_WIDTH) // d
    return pl.pallas_call(
        _mix_out_kernel,
        out_shape=[jax.ShapeDtypeStruct((t_len, d), F32), jax.ShapeDtypeStruct((t_len, d), F32),
                   jax.ShapeDtypeStruct((t_len * ROW_TILE, LANES), F32)],
        grid=(t_len // tm,),
        in_specs=[ta(), ta(), ta(), ta(), td(), td(), pc(gcol), pc(gcol + 1), pc(gcol + 2), td(),
                  va(), va(), vd(), vd(), vd(), vd(), vd(), vd(), vd(),
                  full((aw, d)), full((d, d)), full((d, d)), full((aw, aw))],
        out_specs=[td(), td(), pl.BlockSpec((tm * ROW_TILE, LANES), lambda i: (i, 0))],
        compiler_params=_params(("arbitrary",)),
        name="mix_out",
    )(o_f, o_b, bonus, g, y_f, y_b, proj, proj, proj, x2d,
      lp["ln_w"], lp["ln_b"], lp["ret_ln_w"], lp["ret_ln_b"], vecs["npm"], vecs["npf"], vecs["g1"], vecs["sc2"],
      vecs["sh2"], lp["w_a"], lp["w_b"], lp["w_o"], bd)


def _router_kernel(h_ref, rw_ref, bias_ref, sel_ref, wts_ref, rank_ref, cnt_ref, cnt_s, *, tm):
    step = pl.program_id(0)

    @pl.when(step == 0)
    def _():
        cnt_s[...] = jnp.zeros_like(cnt_s)

    ne = N_EXPERTS
    scores = _sigmoid(_dot3(h_ref[...], rw_ref[...]))
    work = scores + bias_ref[...]
    lane = lax.broadcasted_iota(jnp.int32, (tm, ne), 1).astype(F32)
    idxs = []
    vals = []
    for _ in range(TOP_K):
        m = jnp.max(work, axis=-1, keepdims=True)
        idx = jnp.min(jnp.where(work == m, lane, float(ne)), axis=-1, keepdims=True)
        oh = lane == idx
        vals.append(jnp.sum(jnp.where(oh, scores, 0.0), axis=-1, keepdims=True))
        idxs.append(idx)
        work = jnp.where(oh, -jnp.inf, work)
    sel_f = jnp.concatenate(idxs, axis=1)
    s_sel = jnp.concatenate(vals, axis=1)
    wts_ref[...] = s_sel / jnp.sum(s_sel, axis=1, keepdims=True) * ROUTED_SCALE
    sel_ref[...] = sel_f.astype(jnp.int32)

    hit = work == -jnp.inf
    r_i = lax.broadcasted_iota(jnp.int32, (tm, tm), 0)
    c_i = lax.broadcasted_iota(jnp.int32, (tm, tm), 1)
    before = _bdot((r_i > c_i).astype(BF16), hit.astype(BF16)) + cnt_s[...]
    ranks = [jnp.sum(jnp.where(lane == idxs[k], before, 0.0), axis=-1, keepdims=True) for k in range(TOP_K)]
    rank_ref[...] = jnp.concatenate(ranks, axis=1).astype(jnp.int32)
    cnt = cnt_s[...] + jnp.sum(hit.astype(F32), axis=0, keepdims=True)
    cnt_s[...] = cnt
    cnt_ref[...] = cnt.astype(jnp.int32)


def _router(h2, router_w, router_bias):
    t_len = h2.shape[0]
    tm = min(t_len, 256)
    kern = functools.partial(_router_kernel, tm=tm)
    tk = lambda: pl.BlockSpec((tm, TOP_K), lambda i: (i, 0))
    return pl.pallas_call(
        kern,
        out_shape=[jax.ShapeDtypeStruct((t_len, TOP_K), jnp.int32), jax.ShapeDtypeStruct((t_len, TOP_K), F32),
                   jax.ShapeDtypeStruct((t_len, TOP_K), jnp.int32), jax.ShapeDtypeStruct((1, N_EXPERTS), jnp.int32)],
        grid=(t_len // tm,),
        in_specs=[pl.BlockSpec((tm, D_MODEL), lambda i: (i, 0)),
                  pl.BlockSpec((D_MODEL, N_EXPERTS), lambda i: (0, 0)),
                  pl.BlockSpec((1, N_EXPERTS), lambda i: (0, 0))],
        out_specs=[tk(), tk(), tk(), pl.BlockSpec((1, N_EXPERTS), lambda i: (0, 0))],
        scratch_shapes=[pltpu.VMEM((1, N_EXPERTS), F32)],
        compiler_params=_params(("arbitrary",)),
        name="router",
    )(h2, router_w, router_bias.reshape(1, N_EXPERTS))


def _slot(sel_ref, rank_ref, start_ref, a):
    return start_ref[sel_ref[a]] + rank_ref[a]


def _dispatch_kernel(sel_ref, rank_ref, start_ref, cnt_ref, h_ref, xs_out, zrow, sem, zsem, *, tm):
    def row_copy(t, k):
        slot = _slot(sel_ref, rank_ref, start_ref, t * TOP_K + k)
        return pltpu.make_async_copy(_row_tile(h_ref, t), _row_tile(xs_out, slot), sem)

    def issue(t, carry):
        for k in range(TOP_K):
            row_copy(t, k).start(priority=k % 2)
        return carry

    lax.fori_loop(0, tm, issue, 0)

    def drain(t, carry):
        for k in range(TOP_K):
            row_copy(t, k).wait()
        return carry

    lax.fori_loop(0, tm, drain, 0)

    @pl.when(pl.program_id(0) == pl.num_programs(0) - 1)
    def _():
        zrow[...] = jnp.zeros_like(zrow)

        def for_pad_runs(e0, fn):
            def per_expert(e, carry):
                n = cnt_ref[e]
                n_pad = (n + MOE_BLOCK - 1) // MOE_BLOCK * MOE_BLOCK
                pad = n_pad - n
                off = start_ref[e] + n
                for bit in reversed(range(MOE_BLOCK.bit_length() - 1)):
                    size = 1 << bit
                    is_set = ((pad >> bit) & 1) == 1

                    @pl.when(is_set)
                    def _(off=off, size=size):
                        dst = xs_out.at[pl.ds(pl.multiple_of(off * ROW_TILE, ROW_TILE), size * ROW_TILE)]
                        fn(pltpu.make_async_copy(zrow.at[pl.ds(0, size * ROW_TILE)], dst, zsem))

                    off = off + jnp.where(is_set, size, 0)
                return carry

            lax.fori_loop(e0, e0 + ZERO_PAD_GROUP, per_expert, 0)

        def per_group(gi, carry):
            e0 = gi * ZERO_PAD_GROUP
            for_pad_runs(e0, lambda cp: cp.start())
            for_pad_runs(e0, lambda cp: cp.wait())
            return carry

        lax.fori_loop(0, N_EXPERTS // ZERO_PAD_GROUP, per_group, 0)


def _dispatch(h3, sel_flat, rank_flat, pad_start, counts, n_slots):
    t_len = h3.shape[0] // ROW_TILE
    tm = min(t_len, 256)
    kern = functools.partial(_dispatch_kernel, tm=tm)
    smem_blk = lambda: pl.BlockSpec((tm * TOP_K,), lambda i: (i,), memory_space=pltpu.SMEM)
    smem_all = lambda: pl.BlockSpec((N_EXPERTS,), lambda i: (0,), memory_space=pltpu.SMEM)
    return pl.pallas_call(
        kern,
        out_shape=jax.ShapeDtypeStruct((n_slots * ROW_TILE, LANES), F32),
        grid=(t_len // tm,),
        in_specs=[smem_blk(), smem_blk(), smem_all(), smem_all(),
                  pl.BlockSpec((tm * ROW_TILE, LANES), lambda i: (i, 0))],
        out_specs=pl.BlockSpec(memory_space=pl.ANY),
        scratch_shapes=[pltpu.VMEM((MOE_BLOCK // 2 * ROW_TILE, LANES), F32), pltpu.SemaphoreType.DMA(()),
                        pltpu.SemaphoreType.DMA(())],
        compiler_params=_params(("arbitrary",)),
        name="dispatch",
    )(sel_flat, rank_flat, pad_start, counts, h3)


def _expert_kernel(be_ref, nu_ref, xs_ref, wg_ref, wu_ref, wd_ref, ys_ref, wg_s, wu_s, wd_s):
    b = pl.program_id(0)
    used = b < nu_ref[0]
    new_expert = (b == 0) | (be_ref[b] != be_ref[jnp.maximum(b - 1, 0)])

    @pl.when(used & new_expert)
    def _():
        wg_s[...] = _bf(wg_ref[0])
        wu_s[...] = _bf(wu_ref[0])
        wd_s[...] = _bf(wd_ref[0])

    @pl.when(used)
    def _():
        xb = _bf(_tiles_to_rows(xs_ref, MOE_BLOCK))
        gate = _bdot(xb, wg_s[...])
        up = _bdot(xb, wu_s[...])
        act = gate * _sigmoid(gate) * up
        _rows_to_tiles(_bdot(_bf(act), wd_s[...]), ys_ref)

    @pl.when(jnp.logical_not(used))
    def _():
        ys_ref[...] = jnp.zeros_like(ys_ref)


def _experts(xs, block_e, n_used, w_gate, w_up, w_down):
    n_slots = xs.shape[0] // ROW_TILE
    n_blocks = n_slots // MOE_BLOCK
    rows = (MOE_BLOCK * ROW_TILE, LANES)
    grid_spec = pltpu.PrefetchScalarGridSpec(
        num_scalar_prefetch=2,
        grid=(n_blocks,),
        in_specs=[
            pl.BlockSpec(rows, lambda b, be, nu: (jnp.where(b < nu[0], b, 0), 0)),
            pl.BlockSpec((1, D_MODEL, EXPERT_FF), lambda b, be, nu: (be[b], 0, 0)),
            pl.BlockSpec((1, D_MODEL, EXPERT_FF), lambda b, be, nu: (be[b], 0, 0)),
            pl.BlockSpec((1, EXPERT_FF, D_MODEL), lambda b, be, nu: (be[b], 0, 0)),
        ],
        out_specs=pl.BlockSpec(rows, lambda b, be, nu: (b, 0)),
        scratch_shapes=[pltpu.VMEM((D_MODEL, EXPERT_FF), BF16), pltpu.VMEM((D_MODEL, EXPERT_FF), BF16),
                        pltpu.VMEM((EXPERT_FF, D_MODEL), BF16)],
    )
    return pl.pallas_call(
        _expert_kernel,
        out_shape=jax.ShapeDtypeStruct((n_slots * ROW_TILE, LANES), F32),
        grid_spec=grid_spec,
        compiler_params=_params(("arbitrary",)),
        name="experts",
    )(block_e, n_used, xs, w_gate, w_up, w_down)


def _combine_kernel(sel_ref, rank_ref, start_ref, ys_ref, wts_ref, h_ref, x1_ref, sg_ref, su_ref, sd_ref, npo, g2_ref,
                    o_ref, buf, sem, *, tm):
    def row_copy(t, k):
        slot = _slot(sel_ref, rank_ref, start_ref, t * TOP_K + k)
        return pltpu.make_async_copy(_row_tile(ys_ref, slot), _row_tile(buf, t, (k,)), sem)

    def issue(t, carry):
        for k in range(TOP_K):
            row_copy(t, k).start(priority=k % 2)
        return carry

    lax.fori_loop(0, tm, issue, 0)

    hb = _bf(h_ref[...])
    gate = _bdot(hb, sg_ref[...])
    up = _bdot(hb, su_ref[...])
    shared = _bdot(_bf(gate * _sigmoid(gate) * up), sd_ref[...])

    def drain(t, carry):
        for k in range(TOP_K):
            row_copy(t, k).wait()
        return carry

    lax.fori_loop(0, tm, drain, 0)

    wts = wts_ref[...]
    routed = _tiles_to_rows(buf, tm, (0,)) * wts[:, 0:1]
    for k in range(1, TOP_K):
        routed = routed + _tiles_to_rows(buf, tm, (k,)) * wts[:, k:k + 1]
    o_ref[...] = x1_ref[...] + g2_ref[...] * (_rms(routed + shared) * npo[...])


def _combine(ys, sel_flat, rank_flat, pad_start, wts, h2, x1, sg, su, sd, npo, g2):
    t_len = h2.shape[0]
    tm = min(t_len, 256)
    d = D_MODEL
    kern = functools.partial(_combine_kernel, tm=tm)
    smem_blk = lambda: pl.BlockSpec((tm * TOP_K,), lambda i: (i,), memory_space=pltpu.SMEM)
    td = lambda: pl.BlockSpec((tm, d), lambda i: (i, 0))
    vd = lambda: pl.BlockSpec((1, d), lambda i: (0, 0))
    return pl.pallas_call(
        kern,
        out_shape=jax.ShapeDtypeStruct((t_len, d), F32),
        grid=(t_len // tm,),
        in_specs=[smem_blk(), smem_blk(),
                  pl.BlockSpec((N_EXPERTS,), lambda i: (0,), memory_space=pltpu.SMEM),
                  pl.BlockSpec(memory_space=pl.ANY),
                  pl.BlockSpec((tm, TOP_K), lambda i: (i, 0)),
                  td(), td(),
                  pl.BlockSpec((d, SHARED_FF), lambda i: (0, 0)), pl.BlockSpec((d, SHARED_FF), lambda i: (0, 0)),
                  pl.BlockSpec((SHARED_FF, d), lambda i: (0, 0)), vd(), vd()],
        out_specs=td(),
        scratch_shapes=[pltpu.VMEM((TOP_K, tm * ROW_TILE, LANES), F32), pltpu.SemaphoreType.DMA(())],
        compiler_params=_params(("arbitrary",)),
        name="combine",
    )(sel_flat, rank_flat, pad_start, ys, wts, h2, x1, sg, su, sd, npo, g2)


def _block_diag_ones(width, group):
    idx = np.arange(width) // group
    return jnp.asarray(idx[:, None] == idx[None, :], dtype=BF16)


def _rope_tables(t_len):
    pos = np.arange(t_len)
    rows = (pos // GRID_W).astype(np.float32)
    cols = (pos % GRID_W).astype(np.float32)
    quarter = B_QK_DIM // 4
    inv_freq = jnp.asarray(ROPE_BASE, F32) ** (-jnp.arange(quarter, dtype=F32) / quarter)
    ang_r = jnp.asarray(rows)[:, None] * inv_freq
    ang_c = jnp.asarray(cols)[:, None] * inv_freq
    cr, sr, cc, sc = jnp.cos(ang_r), jnp.sin(ang_r), jnp.cos(ang_c), jnp.sin(ang_c)
    return jnp.concatenate([cr, cr, cc, cc], axis=1), jnp.concatenate([-sr, sr, -sc, sc], axis=1)


def _token_mixer(x2d, g_pre, sc, sh, lp, cos, sin, states, bd):
    proj = _in_proj(x2d, g_pre, sc, sh, lp["w_in"], lp["mu"])
    prep = _rwkv_prep(proj, lp, bd)
    o_f, o_b, s_fin = _rwkv_scan(proj, prep, states[0])
    y_f, y_b, r_fin = _retention(proj, cos, sin, lp["lgt"], states[1])
    return proj, prep, (o_f, o_b), (y_f, y_b), (s_fin, r_fin)


def kernel(x, c, ctx, c_ctx, w_mod, b_mod, norm_pre_mix, norm_post_mix, norm_pre_ffn, norm_post_ffn, w_in, shift_mu,
           rwkv_w0, rwkv_w_up, rwkv_a0, rwkv_a_up, rwkv_g_up, rwkv_k_k, rwkv_k_a, rwkv_r_k, rwkv_ln_w, rwkv_ln_b,
           w_branch_a, ret_decay_logit, ret_ln_w, ret_ln_b, w_branch_b, w_out, router_w, router_bias, exp_w_gate,
           exp_w_up, exp_w_down, sh_w_gate, sh_w_up, sh_w_down):
    d = D_MODEL
    assert x.shape[0] == 1 and w_in.shape[0] == 1, "single batch element, single layer"
    t_len = x.shape[1]
    x2d = x.reshape(t_len, d)
    ctx2d = ctx.reshape(ctx.shape[1], d)
    row = lambda a: a.reshape(1, -1)

    cs = jnp.zeros((8, d), F32).at[0].set(c[0]).at[1].set(c_ctx)
    mod = _modulation(cs, w_mod[0], b_mod[0])
    sh1, sc1, g1, sh2, sc2, g2 = [mod[0:1, i * d:(i + 1) * d] for i in range(6)]
    csh1, csc1 = mod[1:2, 0:d], mod[1:2, d:2 * d]

    w_in_p = jnp.concatenate(
        [w_in[0][:, :SHIFT_COLS], jnp.zeros((d, SHIFT_PAD - SHIFT_COLS), F32), w_in[0][:, SHIFT_COLS:]], axis=1)
    lp = {
        "w_in": w_in_p.astype(BF16),
        "mu": jnp.pad(shift_mu[0], ((0, 0), (0, SHIFT_PAD - SHIFT_COLS))),
        "k_k": row(rwkv_k_k[0]), "k_a": row(rwkv_k_a[0]), "r_k": row(rwkv_r_k[0]),
        "w0": rwkv_w0[0], "w_up": rwkv_w_up[0], "a0": rwkv_a0[0], "a_up": rwkv_a_up[0], "g_up": rwkv_g_up[0],
        "ln_w": row(rwkv_ln_w[0]), "ln_b": row(rwkv_ln_b[0]),
        "ret_ln_w": row(ret_ln_w[0]), "ret_ln_b": row(ret_ln_b[0]),
        "lgt": jnp.broadcast_to(ret_decay_logit[0].reshape(2 * B_HEADS, 1), (2 * B_HEADS, 128)),
        "w_a": w_branch_a[0].astype(BF16), "w_b": w_branch_b[0].astype(BF16), "w_o": w_out[0].astype(BF16),
    }
    bd = _block_diag_ones(A_WIDTH, A_HEAD_DIM)
    g_pre = row(norm_pre_mix[0])

    t_ctx = ctx2d.shape[0]
    zero_states = (jnp.zeros(RWKV_STATE_SHAPE, F32),
                   jnp.zeros((2, B_HEADS, B_QK_DIM, B_V_DIM), F32))
    ones = jnp.ones((t_ctx, B_QK_DIM), F32)
    *_, ctx_states = _token_mixer(ctx2d, g_pre, csc1, csh1, lp, ones, jnp.zeros_like(ones), zero_states, bd)

    cos, sin = _rope_tables(t_len)
    proj, prep, (o_f, o_b), (y_f, y_b), _ = _token_mixer(x2d, g_pre, sc1, sh1, lp, cos, sin, ctx_states, bd)
    vecs = {"npm": row(norm_post_mix[0]), "npf": row(norm_pre_ffn[0]), "g1": g1, "sc2": sc2, "sh2": sh2}
    x1, h2, h3 = _mix_out(x2d, proj, o_f, o_b, prep[7], prep[8], y_f, y_b, lp, vecs, bd)

    sel, wts, rank, counts = _router(h2, router_w[0], router_bias[0])
    counts = counts.reshape(N_EXPERTS)
    padded = (counts + MOE_BLOCK - 1) // MOE_BLOCK * MOE_BLOCK
    pad_end = jnp.cumsum(padded)
    pad_start = (pad_end - padded).astype(jnp.int32)
    n_assign = t_len * TOP_K
    n_blocks = (n_assign + N_EXPERTS * (MOE_BLOCK - 1) + MOE_BLOCK - 1) // MOE_BLOCK
    block_start = jnp.arange(n_blocks, dtype=jnp.int32) * MOE_BLOCK
    block_e = jnp.minimum(jnp.sum(pad_end[None, :] <= block_start[:, None], axis=1), N_EXPERTS - 1).astype(jnp.int32)
    n_used = (pad_end[-1:] // MOE_BLOCK).astype(jnp.int32)
    sel_flat = sel.reshape(n_assign)
    rank_flat = rank.reshape(n_assign)

    xs = _dispatch(h3, sel_flat, rank_flat, pad_start, counts, n_blocks * MOE_BLOCK)
    ys = _experts(xs, block_e, n_used, exp_w_gate[0], exp_w_up[0], exp_w_down[0])
    out = _combine(ys, sel_flat, rank_flat, pad_start, wts, h2, x1, sh_w_gate[0].astype(BF16),
                   sh_w_up[0].astype(BF16), sh_w_down[0].astype(BF16), row(norm_post_ffn[0]), g2)
    return out.reshape(x.shape)
```

```python
import functools

import jax
import jax.numpy as jnp
import numpy as np
from jax import lax
from jax.experimental import pallas as pl
from jax.experimental.pallas import tpu as pltpu

F32 = jnp.float32
BF16 = jnp.bfloat16

D_MODEL = 1024
GRID_W = 64
NORM_EPS = 1e-6

A_HEAD_DIM = 64
A_WIDTH = D_MODEL // 2
A_HEADS = A_WIDTH // A_HEAD_DIM
W_LORA = 64
ICLR_LORA = 64
G_LORA = 128
RWKV_GN_EPS = 64e-5

B_HEADS = 4
B_QK_WIDTH = D_MODEL // 2
B_V_WIDTH = D_MODEL
B_QK_DIM = B_QK_WIDTH // B_HEADS
B_V_DIM = B_V_WIDTH // B_HEADS
RET_CHUNK = 128
RET_GN_EPS = 1e-5
ROPE_BASE = 10000.0

SHIFT_COLS = 3 * A_WIDTH + W_LORA + ICLR_LORA + G_LORA
LORA_COLS = W_LORA + ICLR_LORA + G_LORA
SHIFT_PAD = 2048
PROJ_COLS = SHIFT_PAD + 2 * B_QK_WIDTH + 2 * B_V_WIDTH + 2 * D_MODEL

N_EXPERTS = 256
TOP_K = 8
EXPERT_FF = D_MODEL // 4
SHARED_FF = D_MODEL // 4
ROUTED_SCALE = 2.5
MOE_BLOCK = 512
ZERO_PAD_GROUP = 16

RWKV_CHUNK = 64
RWKV_STATE_SHAPE = (2, A_HEADS // 2, A_HEAD_DIM, 2 * A_HEAD_DIM)
PROJ_TN = 1024
VMEM_LIMIT = 48 * 1024 * 1024


def _params(sem):
    return pltpu.CompilerParams(dimension_semantics=sem, vmem_limit_bytes=VMEM_LIMIT)


def _bf(a):
    return a.astype(BF16)


def _bdot(a, b):
    return jnp.dot(a, b, preferred_element_type=F32)


def _dot_nt(a, b):
    return lax.dot_general(a, b, (((1,), (1,)), ((), ())), preferred_element_type=F32)


def _dot_tn(a, b):
    return lax.dot_general(a, b, (((0,), (0,)), ((), ())), preferred_element_type=F32)


def _split(a, n):
    out = []
    rem = a
    for _ in range(n):
        p = _bf(rem)
        out.append(p)
        rem = rem - p.astype(F32)
    return out


def _dot_split_lhs(a, b_bf, n=3):
    acc = None
    for p in _split(a, n):
        t = _bdot(p, b_bf)
        acc = t if acc is None else acc + t
    return acc


def _dot_split_rhs(a_bf, b, n=3):
    acc = None
    for p in _split(b, n):
        t = _bdot(a_bf, p)
        acc = t if acc is None else acc + t
    return acc


def _dot3(a, b):
    ah, al = _split(a, 2)
    bh, bl = _split(b, 2)
    return _bdot(ah, bh) + (_bdot(ah, bl) + _bdot(al, bh))


def _sigmoid(x):
    return 1.0 / (1.0 + jnp.exp(-x))


def _softplus(x):
    return jnp.maximum(x, 0.0) + jnp.log1p(jnp.exp(-jnp.abs(x)))


def _rms(x):
    return x * lax.rsqrt(jnp.mean(x * x, axis=-1, keepdims=True) + NORM_EPS)


LANES = 128
ROW_TILE = D_MODEL // LANES


def _rows_to_tiles(x2d, ref, lead=()):
    n = x2d.shape[0]
    for j in range(ROW_TILE):
        ref[lead + (pl.ds(j, n, stride=ROW_TILE), slice(None))] = x2d[:, j * LANES:(j + 1) * LANES]


def _tiles_to_rows(ref, n, lead=()):
    return jnp.concatenate(
        [ref[lead + (pl.ds(j, n, stride=ROW_TILE), slice(None))] for j in range(ROW_TILE)], axis=1)


def _row_tile(ref, r, lead=()):
    return ref.at[lead + (pl.ds(pl.multiple_of(r * ROW_TILE, ROW_TILE), ROW_TILE),)]


def _mod_kernel(cs_ref, w_ref, b_ref, o_ref):
    cs = cs_ref[...]
    s = cs * _sigmoid(cs)
    o_ref[...] = _dot3(s, w_ref[...]) + b_ref[...]


def _modulation(cs, w_mod, b_mod):
    n_out = w_mod.shape[1]
    tn = 1536
    return pl.pallas_call(
        _mod_kernel,
        out_shape=jax.ShapeDtypeStruct((8, n_out), F32),
        grid=(n_out // tn,),
        in_specs=[
            pl.BlockSpec((8, D_MODEL), lambda j: (0, 0)),
            pl.BlockSpec((D_MODEL, tn), lambda j: (0, j)),
            pl.BlockSpec((1, tn), lambda j: (0, j)),
        ],
        out_specs=pl.BlockSpec((8, tn), lambda j: (0, j)),
        compiler_params=_params(("arbitrary",)),
        name="modulation",
    )(cs, w_mod, b_mod.reshape(1, n_out))


def _in_proj_kernel(x_ref, xp_ref, xn_ref, g_ref, sc_ref, sh_ref, w_ref, mu_ref, o_ref, h_s, hh_s, *, tm, n_shift):
    i = pl.program_id(0)
    j = pl.program_id(1)
    n_i = pl.num_programs(0)

    def norm_mod(xv):
        return (_rms(xv) * g_ref[...]) * (1.0 + sc_ref[...]) + sh_ref[...]

    @pl.when(j == 0)
    def _():
        h_s[...] = _bf(norm_mod(x_ref[...]))
        hp = jnp.where(i > 0, norm_mod(xp_ref[...]), 0.0)
        hn = jnp.where(i < n_i - 1, norm_mod(xn_ref[...]), 0.0)
        hh_s[0:8, :] = hp
        hh_s[8:16, :] = hn

    w = w_ref[...]
    p = _bdot(h_s[...], w)

    @pl.when(j < n_shift)
    def _():
        ph = _bdot(_bf(hh_s[...]), w)
        row = lax.broadcasted_iota(jnp.int32, p.shape, 0)
        prev = jnp.where(row == 0, ph[7:8, :], pltpu.roll(p, 1, 0))
        nxt = jnp.where(row == tm - 1, ph[8:9, :], pltpu.roll(p, tm - 1, 0))
        mu = mu_ref[...]
        o_ref[...] = p + mu[0:1, :] * (prev - p) + mu[1:2, :] * (nxt - p)

    @pl.when(j >= n_shift)
    def _():
        o_ref[...] = p


def _in_proj(x2d, g, sc, sh, w_bf, mu_pad):
    t_len = x2d.shape[0]
    tm = min(t_len, 1024)
    tn = PROJ_TN
    n_shift = SHIFT_PAD // tn
    tb8 = tm // 8
    nb8 = t_len // 8
    kern = functools.partial(_in_proj_kernel, tm=tm, n_shift=n_shift)
    vec = lambda: pl.BlockSpec((1, D_MODEL), lambda i, j: (0, 0))
    return pl.pallas_call(
        kern,
        out_shape=jax.ShapeDtypeStruct((t_len, PROJ_COLS), F32),
        grid=(t_len // tm, PROJ_COLS // tn),
        in_specs=[
            pl.BlockSpec((tm, D_MODEL), lambda i, j: (i, 0)),
            pl.BlockSpec((8, D_MODEL), lambda i, j: (jnp.maximum(i * tb8 - 1, 0), 0)),
            pl.BlockSpec((8, D_MODEL), lambda i, j: (jnp.minimum((i + 1) * tb8, nb8 - 1), 0)),
            vec(), vec(), vec(),
            pl.BlockSpec((D_MODEL, tn), lambda i, j: (0, j)),
            pl.BlockSpec((2, tn), lambda i, j: (0, jnp.minimum(j, n_shift - 1))),
        ],
        out_specs=pl.BlockSpec((tm, tn), lambda i, j: (i, j)),
        scratch_shapes=[pltpu.VMEM((tm, D_MODEL), BF16), pltpu.VMEM((16, D_MODEL), F32)],
        compiler_params=_params(("arbitrary", "arbitrary")),
        name="in_proj",
    )(x2d, x2d, x2d, g, sc, sh, w_bf, mu_pad)


def _rwkv_prep_kernel(r_ref, k_ref, v_ref, lora_ref, kk_w, ka_w, rk_w, w0_ref, wup_ref, a0_ref, aup_ref, gup_ref,
                      bd_ref, kk_o, lw0_o, lw1_o, kd0_o, kd1_o, b0_o, b1_o, bonus_o, g_o):
    r = r_ref[...]
    k = k_ref[...]
    v = v_ref[...]
    lora = lora_ref[...]
    xw = lora[:, 0:W_LORA]
    xa = lora[:, W_LORA:W_LORA + ICLR_LORA]
    xg = lora[:, W_LORA + ICLR_LORA:]
    bd = bd_ref[...]

    kk = k * kk_w[...]
    nrm = jnp.sqrt(_dot_split_lhs(kk * kk, bd))
    kk = kk / jnp.maximum(nrm, 1e-12)
    kk_o[...] = kk

    tw = jnp.tanh(xw)
    lw_outs = (lw0_o, lw1_o)
    kd_outs = (kd0_o, kd1_o)
    b_outs = (b0_o, b1_o)
    kd_sum = None
    for d in range(2):
        z = w0_ref[d:d + 1, :] + _dot3(tw, wup_ref[d])
        w_log = -_softplus(-z) - 0.5
        lw_outs[d][...] = -jnp.exp(w_log)
        a_d = _sigmoid(a0_ref[d:d + 1, :] + _dot3(xa, aup_ref[d]))
        kd = k * (1.0 + (a_d - 1.0) * ka_w[...])
        kd_outs[d][...] = kd
        b_outs[d][...] = kk * a_d
        kd_sum = kd if kd_sum is None else kd_sum + kd
    g_o[...] = _dot3(_sigmoid(xg), gup_ref[...])
    bonus_o[...] = _dot_split_lhs(r * kd_sum * rk_w[...], bd) * v


def _rwkv_prep(proj, lp, bd):
    t_len = proj.shape[0]
    tm = min(t_len, 512)
    aw = A_WIDTH
    col = lambda c: pl.BlockSpec((tm, aw), lambda i: (i, c))
    vec = lambda: pl.BlockSpec((1, aw), lambda i: (0, 0))
    full = lambda shp: pl.BlockSpec(shp, lambda i: (0,) * len(shp))
    outs = [jax.ShapeDtypeStruct((t_len, aw), F32)] * 9
    return pl.pallas_call(
        _rwkv_prep_kernel,
        out_shape=outs,
        grid=(t_len // tm,),
        in_specs=[
            col(0), col(1), col(2),
            pl.BlockSpec((tm, LORA_COLS), lambda i: (i, 3 * aw // LORA_COLS)),
            vec(), vec(), vec(),
            full((2, aw)), full((2, W_LORA, aw)), full((2, aw)), full((2, ICLR_LORA, aw)), full((G_LORA, aw)),
            full((aw, aw)),
        ],
        out_specs=[pl.BlockSpec((tm, aw), lambda i: (i, 0))] * 9,
        compiler_params=_params(("arbitrary",)),
        name="rwkv_prep",
    )(proj, proj, proj, proj, lp["k_k"], lp["k_a"], lp["r_k"], lp["w0"], lp["w_up"], lp["a0"], lp["a_up"],
      lp["g_up"], bd)


def _rwkv_scan_kernel(rf, vf, kkf, lwf, kdf, bf_, rb, vb, kkb, lwb, kdb, bb, s0_ref, of_ref, ob_ref, sfin_ref, s_s,
                      *, tb):
    step = pl.program_id(0)

    @pl.when(step == 0)
    def _():
        s_s[...] = s0_ref[...]

    c = RWKV_CHUNK
    n = A_HEAD_DIM
    assert c == n
    pw = 2 * n
    row = lax.broadcasted_iota(jnp.int32, (c, pw), 0)
    lane = lax.broadcasted_iota(jnp.int32, (c, pw), 1)
    col = lane % c
    first = lane < n
    eye = (row == col).astype(F32)
    r_c = lax.broadcasted_iota(jnp.int32, (c, c), 0)
    c_c = lax.broadcasted_iota(jnp.int32, (c, c), 1)
    n_ch = tb // c
    dirs = ((rf, vf, kkf, lwf, kdf, bf_, of_ref), (rb, vb, kkb, lwb, kdb, bb, ob_ref))
    pairs = range(A_HEADS // 2)
    psl = [slice(q * pw, (q + 1) * pw) for q in pairs]

    def bd(x):
        z = jnp.zeros_like(x)
        return jnp.concatenate([jnp.where(first, x, z), jnp.where(first, z, x)], axis=0)

    def chunk_terms(d, refs, ci):
        r_ref, v_ref, kk_ref, lw_ref, kd_ref, b_ref, _ = refs
        if d == 0:
            incl, strict, tri = row >= col, row > col, r_c >= c_c
        else:
            incl, strict, tri = row <= col, row < col, r_c <= c_c
        rows = slice(ci * c, (ci + 1) * c)
        lw = lw_ref[rows, :]
        r = r_ref[rows, :]
        kk = kk_ref[rows, :]
        kd = kd_ref[rows, :]
        b = b_ref[rows, :]
        cl = _dot_split_rhs(tri.astype(BF16), lw)
        cl_tot = cl[c - 1:c, :] if d == 0 else cl[0:1, :]
        e_neg = jnp.exp(-cl)
        e_end = jnp.exp(cl_tot - cl)
        g_tot = jnp.exp(cl_tot)
        rt = r * jnp.exp(cl)
        x_all = _bf(jnp.concatenate([-(kk * jnp.exp(cl - lw)), rt], axis=0))
        bt_all = _bf(b * e_neg)
        kt_all = _bf(kd * e_neg)
        return {
            "rows": rows, "strict": strict, "incl": incl,
            "x": [x_all[:, ps] for ps in psl],
            "z": [jnp.concatenate([bd(bt_all[:, ps]), bd(kt_all[:, ps])], axis=0) for ps in psl],
            "bd_bh": [bd(_bf(b * e_end)[:, ps]) for ps in psl],
            "bd_kh": [bd(_bf(kd * e_end)[:, ps]) for ps in psl],
            "bd_v": [bd(_bf(v_ref[rows, :])[:, ps]) for ps in psl],
            "rt": [rt[:, ps] for ps in psl],
            "g_tot": [g_tot[:, ps] for ps in psl],
        }

    chunks = [(d, cc) for cc in range(n_ch) for d in range(2)]
    prep = {(d, cc): chunk_terms(d, dirs[d], cc if d == 0 else n_ch - 1 - cc) for d, cc in chunks}
    units = [(k, q) for k in chunks for q in pairs]

    def per_unit(name, indexed=True):
        return [prep[k][name][q] if indexed else prep[k][name] for k, q in units]

    strict, incl = per_unit("strict", False), per_unit("incl", False)
    x_u, bd_bh, bd_kh, bd_v, rt_u = (per_unit(s) for s in ("x", "bd_bh", "bd_kh", "bd_v", "rt"))
    g = [_dot_nt(x, z) for x, z in zip(x_u, per_unit("z"))]
    a_ab = [jnp.where(s, gi[:c, :pw], 0.0) for s, gi in zip(strict, g)]
    a_rb = [_bf(jnp.where(i, gi[c:, :pw], 0.0)) for i, gi in zip(incl, g)]
    a_k = [_bf(jnp.concatenate([jnp.where(s, gi[:c, pw:], 0.0), jnp.where(i, gi[c:, pw:], 0.0)], axis=0))
           for s, i, gi in zip(strict, incl, g)]
    a2 = [_bdot(_bf(a), bd(_bf(a))) for a in a_ab]
    pa = [jnp.concatenate([eye + a, sq], axis=0) for a, sq in zip(a_ab, a2)]
    for _ in range(4):
        nxt = [_bdot(_bf(x), bd(_bf(x[c:]))) for x in pa]
        pa = [jnp.concatenate([x[:c] + y[:c], y[c:]], axis=0) for x, y in zip(pa, nxt)]
    p = [_bf(x[:c] + _bdot(_bf(x[:c]), bd(_bf(x[c:])))) for x in pa]
    vk = [_bdot(a, v) for a, v in zip(a_k, bd_v)]
    tw = [_bf(_bdot(pi, jnp.concatenate([bd(x[:c]), bd(_bf(vki[:c]))], axis=1)))
          for pi, vki, x in zip(p, vk, x_u)]
    bd_at = [bd(t[:, :pw]) for t in tw]
    bd_w = [bd(t[:, pw:]) for t in tw]
    mt = [_bf(_dot_tn(a, bh)) for a, bh in zip(bd_at, bd_bh)]
    ntf = [_dot_tn(jnp.concatenate([w, v], axis=0), jnp.concatenate([bh, kh], axis=0))
           for w, v, bh, kh in zip(bd_w, bd_v, bd_bh, bd_kh)]
    nt = [m[:n] + m[n:] for m in ntf]
    rw = [_bdot(a, jnp.concatenate([at, w], axis=1)) for a, at, w in zip(a_rb, bd_at, bd_w)]
    ry = [_bf(r + w[:, :pw]) for w, r in zip(rw, rt_u)]
    y0 = [w[:, pw:] + vki[c:] for w, vki in zip(rw, vk)]
    term = {u: vals for u, vals in zip(units, zip(mt, nt, ry, y0, per_unit("g_tot")))}

    state = [[s_s[d, q] for q in pairs] for d in range(2)]
    dq = [(d, q) for d in range(2) for q in pairs]
    for cc in range(n_ch):
        mt_c, nt_c, ry_c, y0_c, gt_c = zip(*[term[((d, cc), q)] for d, q in dq])
        s0 = [state[d][q] for d, q in dq]
        ys = [_dot_nt(ry_c[i], bd(_bf(s0[i]))) + y0_c[i] for i in range(len(dq))]
        upd = [_bdot(jnp.concatenate(_split(s0[i], 2), axis=0), mt_c[i]) for i in range(len(dq))]
        for i, (d, q) in enumerate(dq):
            state[d][q] = s0[i] * gt_c[i] + (upd[i][:n] + upd[i][n:]) + nt_c[i]
        for d in range(2):
            y_d = [ys[i] for i, (dd, _) in enumerate(dq) if dd == d]
            dirs[d][6][prep[(d, cc)]["rows"], :] = jnp.concatenate(y_d, axis=1)
    for d in range(2):
        for q in pairs:
            s_s[d, q] = state[d][q]

    @pl.when(step == pl.num_programs(0) - 1)
    def _():
        sfin_ref[...] = s_s[...]


def _rwkv_scan(proj, prep, s0):
    kk, lw0, lw1, kd0, kd1, b0, b1 = prep[:7]
    t_len = proj.shape[0]
    tb = 2 * RWKV_CHUNK
    nb = t_len // tb
    aw = A_WIDTH
    fwd = lambda c: pl.BlockSpec((tb, aw), lambda i: (i, c))
    bwd = lambda c: pl.BlockSpec((tb, aw), lambda i: (nb - 1 - i, c))
    st = pl.BlockSpec(RWKV_STATE_SHAPE, lambda i: (0, 0, 0, 0))
    kern = functools.partial(_rwkv_scan_kernel, tb=tb)
    return pl.pallas_call(
        kern,
        out_shape=[jax.ShapeDtypeStruct((t_len, aw), F32), jax.ShapeDtypeStruct((t_len, aw), F32),
                   jax.ShapeDtypeStruct(RWKV_STATE_SHAPE, F32)],
        grid=(nb,),
        in_specs=[fwd(0), fwd(2), fwd(0), fwd(0), fwd(0), fwd(0),
                  bwd(0), bwd(2), bwd(0), bwd(0), bwd(0), bwd(0), st],
        out_specs=[fwd(0), bwd(0), st],
        scratch_shapes=[pltpu.VMEM(RWKV_STATE_SHAPE, F32)],
        compiler_params=_params(("arbitrary",)),
        name="rwkv_scan",
    )(proj, proj, kk, lw0, kd0, b0, proj, proj, kk, lw1, kd1, b1, s0)


def _retention_kernel(qf, kf, vf, cosf, sinf, qb, kb, vb, cosb, sinb, lgt_ref, r0_ref, yf_ref, yb_ref, rfin_ref, r_s):
    step = pl.program_id(0)

    @pl.when(step == 0)
    def _():
        r_s[...] = r0_ref[...]

    c = RET_CHUNK
    dk = B_QK_DIM
    dv = B_V_DIM
    lg_all = -_softplus(-lgt_ref[...])
    rowf = lax.broadcasted_iota(jnp.int32, (c, c), 0).astype(F32)
    colf = lax.broadcasted_iota(jnp.int32, (c, c), 1).astype(F32)
    lane = lax.broadcasted_iota(jnp.int32, (c, dk), 1)
    first_half = (lane % 64) < 32
    dirs = ((qf, kf, vf, cosf, sinf, yf_ref), (qb, kb, vb, cosb, sinb, yb_ref))
    for d, (q_ref, k_ref, v_ref, cos_ref, sin_ref, y_ref) in enumerate(dirs):
        cos = cos_ref[...]
        sin = sin_ref[...]
        diff = (rowf - colf) if d == 0 else (colf - rowf)
        pos = rowf if d == 0 else (c - 1.0) - rowf
        for h in range(B_HEADS):
            lg = lg_all[d * B_HEADS + h:d * B_HEADS + h + 1, :]
            qh = q_ref[:, h * dk:(h + 1) * dk]
            kh = k_ref[:, h * dk:(h + 1) * dk] * (dk ** -0.5)

            def rope(xv):
                swapped = jnp.where(first_half, pltpu.roll(xv, dk - 32, 1), pltpu.roll(xv, 32, 1))
                return xv * cos + swapped * sin

            qh = rope(qh)
            kh = rope(kh)
            vh = _bf(v_ref[:, h * dv:(h + 1) * dv])
            dmask = jnp.where(diff >= 0.0, jnp.exp(lg * jnp.maximum(diff, 0.0)), 0.0)
            scores = _dot_nt(_bf(qh), _bf(kh)) * dmask
            inner = _bdot(_bf(scores), vh)
            xi = jnp.exp(lg * (pos + 1.0))
            zeta = jnp.exp(lg * ((c - 1.0) - pos))
            r_prev = r_s[d, h]
            cross = _bdot(_bf(qh * xi), _bf(r_prev))
            y_ref[:, h * dv:(h + 1) * dv] = inner + cross
            kv = _dot_tn(_bf(kh * zeta), vh)
            g_chunk = jnp.exp(lg * float(c))
            r_s[d, h] = jnp.concatenate([g_chunk, g_chunk], axis=1) * r_prev + kv

    @pl.when(step == pl.num_programs(0) - 1)
    def _():
        rfin_ref[...] = r_s[...]


def _retention(proj, cos, sin, lgt, r0):
    t_len = proj.shape[0]
    c = RET_CHUNK
    nc = t_len // c
    qw = B_QK_WIDTH
    vw = B_V_WIDTH
    fq = lambda col: pl.BlockSpec((c, qw), lambda i: (i, col))
    bq = lambda col: pl.BlockSpec((c, qw), lambda i: (nc - 1 - i, col))
    st = pl.BlockSpec((2, B_HEADS, B_QK_DIM, B_V_DIM), lambda i: (0, 0, 0, 0))
    q_col = SHIFT_PAD // qw
    v_col = (SHIFT_PAD + 2 * qw) // vw
    return pl.pallas_call(
        _retention_kernel,
        out_shape=[jax.ShapeDtypeStruct((t_len, vw), F32), jax.ShapeDtypeStruct((t_len, vw), F32),
                   jax.ShapeDtypeStruct((2, B_HEADS, B_QK_DIM, B_V_DIM), F32)],
        grid=(nc,),
        in_specs=[
            fq(q_col), fq(q_col + 1), pl.BlockSpec((c, vw), lambda i: (i, v_col)),
            pl.BlockSpec((c, B_QK_DIM), lambda i: (i, 0)), pl.BlockSpec((c, B_QK_DIM), lambda i: (i, 0)),
            bq(q_col), bq(q_col + 1), pl.BlockSpec((c, vw), lambda i: (nc - 1 - i, v_col)),
            pl.BlockSpec((c, B_QK_DIM), lambda i: (nc - 1 - i, 0)),
            pl.BlockSpec((c, B_QK_DIM), lambda i: (nc - 1 - i, 0)),
            pl.BlockSpec((2 * B_HEADS, 128), lambda i: (0, 0)), st,
        ],
        out_specs=[pl.BlockSpec((c, vw), lambda i: (i, 0)), pl.BlockSpec((c, vw), lambda i: (nc - 1 - i, 0)), st],
        scratch_shapes=[pltpu.VMEM((2, B_HEADS, B_QK_DIM, B_V_DIM), F32)],
        compiler_params=_params(("arbitrary",)),
        name="retention",
    )(proj, proj, proj, cos, sin, proj, proj, proj, cos, sin, lgt, r0)


def _mix_out_kernel(of_ref, ob_ref, bonus_ref, g_ref, yf_ref, yb_ref, gb_ref, ga_ref, gbb_ref, x_ref,
                    alnw, alnb, rlnw, rlnb, npm, npf, g1_ref, sc2_ref, sh2_ref, wa_ref, wb_ref, wo_ref, bd_ref,
                    x1_ref, h2_ref, h3_ref):
    bd = bd_ref[...]
    o = of_ref[...] + ob_ref[...]
    mu = _dot_split_lhs(o, bd) * (1.0 / A_HEAD_DIM)
    oc = o - mu
    var = _dot_split_lhs(oc * oc, bd) * (1.0 / A_HEAD_DIM)
    ya = oc * lax.rsqrt(var + RWKV_GN_EPS) * alnw[...] + alnb[...] + bonus_ref[...]
    ya = _bdot(_bf(ya * g_ref[...]), wa_ref[...])

    y = yf_ref[...] + yb_ref[...]
    parts = []
    for h in range(B_HEADS):
        seg = y[:, h * B_V_DIM:(h + 1) * B_V_DIM]
        m = jnp.mean(seg, axis=-1, keepdims=True)
        sc = seg - m
        vr = jnp.mean(sc * sc, axis=-1, keepdims=True)
        parts.append(sc * lax.rsqrt(vr + RET_GN_EPS))
    yn = jnp.concatenate(parts, axis=1) * rlnw[...] + rlnb[...]
    gb = gb_ref[...]
    yb = _bdot(_bf(yn * (gb * _sigmoid(gb))), wb_ref[...])

    merged = _sigmoid(ga_ref[...]) * ya + _sigmoid(gbb_ref[...]) * yb
    mix = _bdot(_bf(merged), wo_ref[...])
    x1 = x_ref[...] + g1_ref[...] * (_rms(mix) * npm[...])
    x1_ref[...] = x1
    h2 = (_rms(x1) * npf[...]) * (1.0 + sc2_ref[...]) + sh2_ref[...]
    h2_ref[...] = h2
    _rows_to_tiles(h2, h3_ref)


def _mix_out(x2d, proj, o_f, o_b, bonus, g, y_f, y_b, lp, vecs, bd):
    t_len = x2d.shape[0]
    tm = min(t_len, 256)
    aw = A_WIDTH
    d = D_MODEL
    ta = lambda: pl.BlockSpec((tm, aw), lambda i: (i, 0))
    td = lambda: pl.BlockSpec((tm, d), lambda i: (i, 0))
    pc = lambda c: pl.BlockSpec((tm, d), lambda i: (i, c))
    va = lambda: pl.BlockSpec((1, aw), lambda i: (0, 0))
    vd = lambda: pl.BlockSpec((1, d), lambda i: (0, 0))
    full = lambda shp: pl.BlockSpec(shp, lambda i: (0, 0))
    gcol = (SHIFT_PAD + 2 * B_QK_WIDTH + B_V_WIDTH) // d
    return pl.pallas_call(
        _mix_out_kernel,
        out_shape=[jax.ShapeDtypeStruct((t_len, d), F32), jax.ShapeDtypeStruct((t_len, d), F32),
                   jax.ShapeDtypeStruct((t_len * ROW_TILE, LANES), F32)],
        grid=(t_len // tm,),
        in_specs=[ta(), ta(), ta(), ta(), td(), td(), pc(gcol), pc(gcol + 1), pc(gcol + 2), td(),
                  va(), va(), vd(), vd(), vd(), vd(), vd(), vd(), vd(),
                  full((aw, d)), full((d, d)), full((d, d)), full((aw, aw))],
        out_specs=[td(), td(), pl.BlockSpec((tm * ROW_TILE, LANES), lambda i: (i, 0))],
        compiler_params=_params(("arbitrary",)),
        name="mix_out",
    )(o_f, o_b, bonus, g, y_f, y_b, proj, proj, proj, x2d,
      lp["ln_w"], lp["ln_b"], lp["ret_ln_w"], lp["ret_ln_b"], vecs["npm"], vecs["npf"], vecs["g1"], vecs["sc2"],
      vecs["sh2"], lp["w_a"], lp["w_b"], lp["w_o"], bd)


def _router_kernel(h_ref, rw_ref, bias_ref, sel_ref, wts_ref, rank_ref, cnt_ref, cnt_s, *, tm):
    step = pl.program_id(0)

    @pl.when(step == 0)
    def _():
        cnt_s[...] = jnp.zeros_like(cnt_s)

    ne = N_EXPERTS
    scores = _sigmoid(_dot3(h_ref[...], rw_ref[...]))
    work = scores + bias_ref[...]
    lane = lax.broadcasted_iota(jnp.int32, (tm, ne), 1).astype(F32)
    idxs = []
    vals = []
    for _ in range(TOP_K):
        m = jnp.max(work, axis=-1, keepdims=True)
        idx = jnp.min(jnp.where(work == m, lane, float(ne)), axis=-1, keepdims=True)
        oh = lane == idx
        vals.append(jnp.sum(jnp.where(oh, scores, 0.0), axis=-1, keepdims=True))
        idxs.append(idx)
        work = jnp.where(oh, -jnp.inf, work)
    sel_f = jnp.concatenate(idxs, axis=1)
    s_sel = jnp.concatenate(vals, axis=1)
    wts_ref[...] = s_sel / jnp.sum(s_sel, axis=1, keepdims=True) * ROUTED_SCALE
    sel_ref[...] = sel_f.astype(jnp.int32)

    hit = work == -jnp.inf
    r_i = lax.broadcasted_iota(jnp.int32, (tm, tm), 0)
    c_i = lax.broadcasted_iota(jnp.int32, (tm, tm), 1)
    before = _bdot((r_i > c_i).astype(BF16), hit.astype(BF16)) + cnt_s[...]
    ranks = [jnp.sum(jnp.where(lane == idxs[k], before, 0.0), axis=-1, keepdims=True) for k in range(TOP_K)]
    rank_ref[...] = jnp.concatenate(ranks, axis=1).astype(jnp.int32)
    cnt = cnt_s[...] + jnp.sum(hit.astype(F32), axis=0, keepdims=True)
    cnt_s[...] = cnt
    cnt_ref[...] = cnt.astype(jnp.int32)


def _router(h2, router_w, router_bias):
    t_len = h2.shape[0]
    tm = min(t_len, 256)
    kern = functools.partial(_router_kernel, tm=tm)
    tk = lambda: pl.BlockSpec((tm, TOP_K), lambda i: (i, 0))
    return pl.pallas_call(
        kern,
        out_shape=[jax.ShapeDtypeStruct((t_len, TOP_K), jnp.int32), jax.ShapeDtypeStruct((t_len, TOP_K), F32),
                   jax.ShapeDtypeStruct((t_len, TOP_K), jnp.int32), jax.ShapeDtypeStruct((1, N_EXPERTS), jnp.int32)],
        grid=(t_len // tm,),
        in_specs=[pl.BlockSpec((tm, D_MODEL), lambda i: (i, 0)),
                  pl.BlockSpec((D_MODEL, N_EXPERTS), lambda i: (0, 0)),
                  pl.BlockSpec((1, N_EXPERTS), lambda i: (0, 0))],
        out_specs=[tk(), tk(), tk(), pl.BlockSpec((1, N_EXPERTS), lambda i: (0, 0))],
        scratch_shapes=[pltpu.VMEM((1, N_EXPERTS), F32)],
        compiler_params=_params(("arbitrary",)),
        name="router",
    )(h2, router_w, router_bias.reshape(1, N_EXPERTS))


def _slot(sel_ref, rank_ref, start_ref, a):
    return start_ref[sel_ref[a]] + rank_ref[a]


def _dispatch_kernel(sel_ref, rank_ref, start_ref, cnt_ref, h_ref, xs_out, zrow, sem, zsem, *, tm):
    def row_copy(t, k):
        slot = _slot(sel_ref, rank_ref, start_ref, t * TOP_K + k)
        return pltpu.make_async_copy(_row_tile(h_ref, t), _row_tile(xs_out, slot), sem)

    def issue(t, carry):
        for k in range(TOP_K):
            row_copy(t, k).start(priority=k % 2)
        return carry

    lax.fori_loop(0, tm, issue, 0)

    def drain(t, carry):
        for k in range(TOP_K):
            row_copy(t, k).wait()
        return carry

    lax.fori_loop(0, tm, drain, 0)

    @pl.when(pl.program_id(0) == pl.num_programs(0) - 1)
    def _():
        zrow[...] = jnp.zeros_like(zrow)

        def for_pad_runs(e0, fn):
            def per_expert(e, carry):
                n = cnt_ref[e]
                n_pad = (n + MOE_BLOCK - 1) // MOE_BLOCK * MOE_BLOCK
                pad = n_pad - n
                off = start_ref[e] + n
                for bit in reversed(range(MOE_BLOCK.bit_length() - 1)):
                    size = 1 << bit
                    is_set = ((pad >> bit) & 1) == 1

                    @pl.when(is_set)
                    def _(off=off, size=size):
                        dst = xs_out.at[pl.ds(pl.multiple_of(off * ROW_TILE, ROW_TILE), size * ROW_TILE)]
                        fn(pltpu.make_async_copy(zrow.at[pl.ds(0, size * ROW_TILE)], dst, zsem))

                    off = off + jnp.where(is_set, size, 0)
                return carry

            lax.fori_loop(e0, e0 + ZERO_PAD_GROUP, per_expert, 0)

        def per_group(gi, carry):
            e0 = gi * ZERO_PAD_GROUP
            for_pad_runs(e0, lambda cp: cp.start())
            for_pad_runs(e0, lambda cp: cp.wait())
            return carry

        lax.fori_loop(0, N_EXPERTS // ZERO_PAD_GROUP, per_group, 0)


def _dispatch(h3, sel_flat, rank_flat, pad_start, counts, n_slots):
    t_len = h3.shape[0] // ROW_TILE
    tm = min(t_len, 256)
    kern = functools.partial(_dispatch_kernel, tm=tm)
    smem_blk = lambda: pl.BlockSpec((tm * TOP_K,), lambda i: (i,), memory_space=pltpu.SMEM)
    smem_all = lambda: pl.BlockSpec((N_EXPERTS,), lambda i: (0,), memory_space=pltpu.SMEM)
    return pl.pallas_call(
        kern,
        out_shape=jax.ShapeDtypeStruct((n_slots * ROW_TILE, LANES), F32),
        grid=(t_len // tm,),
        in_specs=[smem_blk(), smem_blk(), smem_all(), smem_all(),
                  pl.BlockSpec((tm * ROW_TILE, LANES), lambda i: (i, 0))],
        out_specs=pl.BlockSpec(memory_space=pl.ANY),
        scratch_shapes=[pltpu.VMEM((MOE_BLOCK // 2 * ROW_TILE, LANES), F32), pltpu.SemaphoreType.DMA(()),
                        pltpu.SemaphoreType.DMA(())],
        compiler_params=_params(("arbitrary",)),
        name="dispatch",
    )(sel_flat, rank_flat, pad_start, counts, h3)


def _expert_kernel(be_ref, nu_ref, xs_ref, wg_ref, wu_ref, wd_ref, ys_ref, wg_s, wu_s, wd_s):
    b = pl.program_id(0)
    used = b < nu_ref[0]
    new_expert = (b == 0) | (be_ref[b] != be_ref[jnp.maximum(b - 1, 0)])

    @pl.when(used & new_expert)
    def _():
        wg_s[...] = _bf(wg_ref[0])
        wu_s[...] = _bf(wu_ref[0])
        wd_s[...] = _bf(wd_ref[0])

    @pl.when(used)
    def _():
        xb = _bf(_tiles_to_rows(xs_ref, MOE_BLOCK))
        gate = _bdot(xb, wg_s[...])
        up = _bdot(xb, wu_s[...])
        act = gate * _sigmoid(gate) * up
        _rows_to_tiles(_bdot(_bf(act), wd_s[...]), ys_ref)

    @pl.when(jnp.logical_not(used))
    def _():
        ys_ref[...] = jnp.zeros_like(ys_ref)


def _experts(xs, block_e, n_used, w_gate, w_up, w_down):
    n_slots = xs.shape[0] // ROW_TILE
    n_blocks = n_slots // MOE_BLOCK
    rows = (MOE_BLOCK * ROW_TILE, LANES)
    grid_spec = pltpu.PrefetchScalarGridSpec(
        num_scalar_prefetch=2,
        grid=(n_blocks,),
        in_specs=[
            pl.BlockSpec(rows, lambda b, be, nu: (jnp.where(b < nu[0], b, 0), 0)),
            pl.BlockSpec((1, D_MODEL, EXPERT_FF), lambda b, be, nu: (be[b], 0, 0)),
            pl.BlockSpec((1, D_MODEL, EXPERT_FF), lambda b, be, nu: (be[b], 0, 0)),
            pl.BlockSpec((1, EXPERT_FF, D_MODEL), lambda b, be, nu: (be[b], 0, 0)),
        ],
        out_specs=pl.BlockSpec(rows, lambda b, be, nu: (b, 0)),
        scratch_shapes=[pltpu.VMEM((D_MODEL, EXPERT_FF), BF16), pltpu.VMEM((D_MODEL, EXPERT_FF), BF16),
                        pltpu.VMEM((EXPERT_FF, D_MODEL), BF16)],
    )
    return pl.pallas_call(
        _expert_kernel,
        out_shape=jax.ShapeDtypeStruct((n_slots * ROW_TILE, LANES), F32),
        grid_spec=grid_spec,
        compiler_params=_params(("arbitrary",)),
        name="experts",
    )(block_e, n_used, xs, w_gate, w_up, w_down)


def _combine_kernel(sel_ref, rank_ref, start_ref, ys_ref, wts_ref, h_ref, x1_ref, sg_ref, su_ref, sd_ref, npo, g2_ref,
                    o_ref, buf, sem, *, tm):
    def row_copy(t, k):
        slot = _slot(sel_ref, rank_ref, start_ref, t * TOP_K + k)
        return pltpu.make_async_copy(_row_tile(ys_ref, slot), _row_tile(buf, t, (k,)), sem)

    def issue(t, carry):
        for k in range(TOP_K):
            row_copy(t, k).start(priority=k % 2)
        return carry

    lax.fori_loop(0, tm, issue, 0)

    hb = _bf(h_ref[...])
    gate = _bdot(hb, sg_ref[...])
    up = _bdot(hb, su_ref[...])
    shared = _bdot(_bf(gate * _sigmoid(gate) * up), sd_ref[...])

    def drain(t, carry):
        for k in range(TOP_K):
            row_copy(t, k).wait()
        return carry

    lax.fori_loop(0, tm, drain, 0)

    wts = wts_ref[...]
    routed = _tiles_to_rows(buf, tm, (0,)) * wts[:, 0:1]
    for k in range(1, TOP_K):
        routed = routed + _tiles_to_rows(buf, tm, (k,)) * wts[:, k:k + 1]
    o_ref[...] = x1_ref[...] + g2_ref[...] * (_rms(routed + shared) * npo[...])


def _combine(ys, sel_flat, rank_flat, pad_start, wts, h2, x1, sg, su, sd, npo, g2):
    t_len = h2.shape[0]
    tm = min(t_len, 256)
    d = D_MODEL
    kern = functools.partial(_combine_kernel, tm=tm)
    smem_blk = lambda: pl.BlockSpec((tm * TOP_K,), lambda i: (i,), memory_space=pltpu.SMEM)
    td = lambda: pl.BlockSpec((tm, d), lambda i: (i, 0))
    vd = lambda: pl.BlockSpec((1, d), lambda i: (0, 0))
    return pl.pallas_call(
        kern,
        out_shape=jax.ShapeDtypeStruct((t_len, d), F32),
        grid=(t_len // tm,),
        in_specs=[smem_blk(), smem_blk(),
                  pl.BlockSpec((N_EXPERTS,), lambda i: (0,), memory_space=pltpu.SMEM),
                  pl.BlockSpec(memory_space=pl.ANY),
                  pl.BlockSpec((tm, TOP_K), lambda i: (i, 0)),
                  td(), td(),
                  pl.BlockSpec((d, SHARED_FF), lambda i: (0, 0)), pl.BlockSpec((d, SHARED_FF), lambda i: (0, 0)),
                  pl.BlockSpec((SHARED_FF, d), lambda i: (0, 0)), vd(), vd()],
        out_specs=td(),
        scratch_shapes=[pltpu.VMEM((TOP_K, tm * ROW_TILE, LANES), F32), pltpu.SemaphoreType.DMA(())],
        compiler_params=_params(("arbitrary",)),
        name="combine",
    )(sel_flat, rank_flat, pad_start, ys, wts, h2, x1, sg, su, sd, npo, g2)


def _block_diag_ones(width, group):
    idx = np.arange(width) // group
    return jnp.asarray(idx[:, None] == idx[None, :], dtype=BF16)


def _rope_tables(t_len):
    pos = np.arange(t_len)
    rows = (pos // GRID_W).astype(np.float32)
    cols = (pos % GRID_W).astype(np.float32)
    quarter = B_QK_DIM // 4
    inv_freq = jnp.asarray(ROPE_BASE, F32) ** (-jnp.arange(quarter, dtype=F32) / quarter)
    ang_r = jnp.asarray(rows)[:, None] * inv_freq
    ang_c = jnp.asarray(cols)[:, None] * inv_freq
    cr, sr, cc, sc = jnp.cos(ang_r), jnp.sin(ang_r), jnp.cos(ang_c), jnp.sin(ang_c)
    return jnp.concatenate([cr, cr, cc, cc], axis=1), jnp.concatenate([-sr, sr, -sc, sc], axis=1)


def _token_mixer(x2d, g_pre, sc, sh, lp, cos, sin, states, bd):
    proj = _in_proj(x2d, g_pre, sc, sh, lp["w_in"], lp["mu"])
    prep = _rwkv_prep(proj, lp, bd)
    o_f, o_b, s_fin = _rwkv_scan(proj, prep, states[0])
    y_f, y_b, r_fin = _retention(proj, cos, sin, lp["lgt"], states[1])
    return proj, prep, (o_f, o_b), (y_f, y_b), (s_fin, r_fin)


def kernel(x, c, ctx, c_ctx, w_mod, b_mod, norm_pre_mix, norm_post_mix, norm_pre_ffn, norm_post_ffn, w_in, shift_mu,
           rwkv_w0, rwkv_w_up, rwkv_a0, rwkv_a_up, rwkv_g_up, rwkv_k_k, rwkv_k_a, rwkv_r_k, rwkv_ln_w, rwkv_ln_b,
           w_branch_a, ret_decay_logit, ret_ln_w, ret_ln_b, w_branch_b, w_out, router_w, router_bias, exp_w_gate,
           exp_w_up, exp_w_down, sh_w_gate, sh_w_up, sh_w_down):
    d = D_MODEL
    assert x.shape[0] == 1 and w_in.shape[0] == 1, "single batch element, single layer"
    t_len = x.shape[1]
    x2d = x.reshape(t_len, d)
    ctx2d = ctx.reshape(ctx.shape[1], d)
    row = lambda a: a.reshape(1, -1)

    cs = jnp.zeros((8, d), F32).at[0].set(c[0]).at[1].set(c_ctx)
    mod = _modulation(cs, w_mod[0], b_mod[0])
    sh1, sc1, g1, sh2, sc2, g2 = [mod[0:1, i * d:(i + 1) * d] for i in range(6)]
    csh1, csc1 = mod[1:2, 0:d], mod[1:2, d:2 * d]

    w_in_p = jnp.concatenate(
        [w_in[0][:, :SHIFT_COLS], jnp.zeros((d, SHIFT_PAD - SHIFT_COLS), F32), w_in[0][:, SHIFT_COLS:]], axis=1)
    lp = {
        "w_in": w_in_p.astype(BF16),
        "mu": jnp.pad(shift_mu[0], ((0, 0), (0, SHIFT_PAD - SHIFT_COLS))),
        "k_k": row(rwkv_k_k[0]), "k_a": row(rwkv_k_a[0]), "r_k": row(rwkv_r_k[0]),
        "w0": rwkv_w0[0], "w_up": rwkv_w_up[0], "a0": rwkv_a0[0], "a_up": rwkv_a_up[0], "g_up": rwkv_g_up[0],
        "ln_w": row(rwkv_ln_w[0]), "ln_b": row(rwkv_ln_b[0]),
        "ret_ln_w": row(ret_ln_w[0]), "ret_ln_b": row(ret_ln_b[0]),
        "lgt": jnp.broadcast_to(ret_decay_logit[0].reshape(2 * B_HEADS, 1), (2 * B_HEADS, 128)),
        "w_a": w_branch_a[0].astype(BF16), "w_b": w_branch_b[0].astype(BF16), "w_o": w_out[0].astype(BF16),
    }
    bd = _block_diag_ones(A_WIDTH, A_HEAD_DIM)
    g_pre = row(norm_pre_mix[0])

    t_ctx = ctx2d.shape[0]
    zero_states = (jnp.zeros(RWKV_STATE_SHAPE, F32),
                   jnp.zeros((2, B_HEADS, B_QK_DIM, B_V_DIM), F32))
    ones = jnp.ones((t_ctx, B_QK_DIM), F32)
    *_, ctx_states = _token_mixer(ctx2d, g_pre, csc1, csh1, lp, ones, jnp.zeros_like(ones), zero_states, bd)

    cos, sin = _rope_tables(t_len)
    proj, prep, (o_f, o_b), (y_f, y_b), _ = _token_mixer(x2d, g_pre, sc1, sh1, lp, cos, sin, ctx_states, bd)
    vecs = {"npm": row(norm_post_mix[0]), "npf": row(norm_pre_ffn[0]), "g1": g1, "sc2": sc2, "sh2": sh2}
    x1, h2, h3 = _mix_out(x2d, proj, o_f, o_b, prep[7], prep[8], y_f, y_b, lp, vecs, bd)

    sel, wts, rank, counts = _router(h2, router_w[0], router_bias[0])
    counts = counts.reshape(N_EXPERTS)
    padded = (counts + MOE_BLOCK - 1) // MOE_BLOCK * MOE_BLOCK
    pad_end = jnp.cumsum(padded)
    pad_start = (pad_end - padded).astype(jnp.int32)
    n_assign = t_len * TOP_K
    n_blocks = (n_assign + N_EXPERTS * (MOE_BLOCK - 1) + MOE_BLOCK - 1) // MOE_BLOCK
    block_start = jnp.arange(n_blocks, dtype=jnp.int32) * MOE_BLOCK
    block_e = jnp.minimum(jnp.sum(pad_end[None, :] <= block_start[:, None], axis=1), N_EXPERTS - 1).astype(jnp.int32)
    n_used = (pad_end[-1:] // MOE_BLOCK).astype(jnp.int32)
    sel_flat = sel.reshape(n_assign)
    rank_flat = rank.reshape(n_assign)

    xs = _dispatch(h3, sel_flat, rank_flat, pad_start, counts, n_blocks * MOE_BLOCK)
    ys = _experts(xs, block_e, n_used, exp_w_gate[0], exp_w_up[0], exp_w_down[0])
    out = _combine(ys, sel_flat, rank_flat, pad_start, wts, h2, x1, sh_w_gate[0].astype(BF16),
                   sh_w_up[0].astype(BF16), sh_w_down[0].astype(BF16), row(norm_post_ffn[0]), g2)
    return out.reshape(x.shape)
```

```python
import functools

import jax
import jax.numpy as jnp
import numpy as np
from jax import lax
from jax.experimental import pallas as pl
from jax.experimental.pallas import tpu as pltpu

F32 = jnp.float32
BF16 = jnp.bfloat16

D_MODEL = 1024
GRID_W = 64
NORM_EPS = 1e-6

A_HEAD_DIM = 64
A_WIDTH = D_MODEL // 2
A_HEADS = A_WIDTH // A_HEAD_DIM
W_LORA = 64
ICLR_LORA = 64
G_LORA = 128
RWKV_GN_EPS = 64e-5

B_HEADS = 4
B_QK_WIDTH = D_MODEL // 2
B_V_WIDTH = D_MODEL
B_QK_DIM = B_QK_WIDTH // B_HEADS
B_V_DIM = B_V_WIDTH // B_HEADS
RET_CHUNK = 128
RET_GN_EPS = 1e-5
ROPE_BASE = 10000.0

SHIFT_COLS = 3 * A_WIDTH + W_LORA + ICLR_LORA + G_LORA
LORA_COLS = W_LORA + ICLR_LORA + G_LORA
SHIFT_PAD = 2048
PROJ_COLS = SHIFT_PAD + 2 * B_QK_WIDTH + 2 * B_V_WIDTH + 2 * D_MODEL

N_EXPERTS = 256
TOP_K = 8
EXPERT_FF = D_MODEL // 4
SHARED_FF = D_MODEL // 4
ROUTED_SCALE = 2.5
MOE_BLOCK = 512
ZERO_PAD_GROUP = 16

RWKV_CHUNK = 64
RWKV_STATE_SHAPE = (2, A_HEADS // 2, A_HEAD_DIM, 2 * A_HEAD_DIM)
PROJ_TN = 1024
VMEM_LIMIT = 48 * 1024 * 1024


def _params(sem):
    return pltpu.CompilerParams(dimension_semantics=sem, vmem_limit_bytes=VMEM_LIMIT)


def _bf(a):
    return a.astype(BF16)


def _bdot(a, b):
    return jnp.dot(a, b, preferred_element_type=F32)


def _dot_nt(a, b):
    return lax.dot_general(a, b, (((1,), (1,)), ((), ())), preferred_element_type=F32)


def _dot_tn(a, b):
    return lax.dot_general(a, b, (((0,), (0,)), ((), ())), preferred_element_type=F32)


def _split(a, n):
    out = []
    rem = a
    for _ in range(n):
        p = _bf(rem)
        out.append(p)
        rem = rem - p.astype(F32)
    return out


def _dot_split_lhs(a, b_bf, n=3):
    acc = None
    for p in _split(a, n):
        t = _bdot(p, b_bf)
        acc = t if acc is None else acc + t
    return acc


def _dot_split_rhs(a_bf, b, n=3):
    acc = None
    for p in _split(b, n):
        t = _bdot(a_bf, p)
        acc = t if acc is None else acc + t
    return acc


def _dot3(a, b):
    ah, al = _split(a, 2)
    bh, bl = _split(b, 2)
    return _bdot(ah, bh) + (_bdot(ah, bl) + _bdot(al, bh))


def _sigmoid(x):
    return 1.0 / (1.0 + jnp.exp(-x))


def _softplus(x):
    return jnp.maximum(x, 0.0) + jnp.log1p(jnp.exp(-jnp.abs(x)))


def _rms(x):
    return x * lax.rsqrt(jnp.mean(x * x, axis=-1, keepdims=True) + NORM_EPS)


LANES = 128
ROW_TILE = D_MODEL // LANES


def _rows_to_tiles(x2d, ref, lead=()):
    n = x2d.shape[0]
    for j in range(ROW_TILE):
        ref[lead + (pl.ds(j, n, stride=ROW_TILE), slice(None))] = x2d[:, j * LANES:(j + 1) * LANES]


def _tiles_to_rows(ref, n, lead=()):
    return jnp.concatenate(
        [ref[lead + (pl.ds(j, n, stride=ROW_TILE), slice(None))] for j in range(ROW_TILE)], axis=1)


def _row_tile(ref, r, lead=()):
    return ref.at[lead + (pl.ds(pl.multiple_of(r * ROW_TILE, ROW_TILE), ROW_TILE),)]


def _mod_kernel(cs_ref, w_ref, b_ref, o_ref):
    cs = cs_ref[...]
    s = cs * _sigmoid(cs)
    o_ref[...] = _dot3(s, w_ref[...]) + b_ref[...]


def _modulation(cs, w_mod, b_mod):
    n_out = w_mod.shape[1]
    tn = 1536
    return pl.pallas_call(
        _mod_kernel,
        out_shape=jax.ShapeDtypeStruct((8, n_out), F32),
        grid=(n_out // tn,),
        in_specs=[
            pl.BlockSpec((8, D_MODEL), lambda j: (0, 0)),
            pl.BlockSpec((D_MODEL, tn), lambda j: (0, j)),
            pl.BlockSpec((1, tn), lambda j: (0, j)),
        ],
        out_specs=pl.BlockSpec((8, tn), lambda j: (0, j)),
        compiler_params=_params(("arbitrary",)),
        name="modulation",
    )(cs, w_mod, b_mod.reshape(1, n_out))


def _in_proj_kernel(x_ref, xp_ref, xn_ref, g_ref, sc_ref, sh_ref, w_ref, mu_ref, o_ref, h_s, hh_s, *, tm, n_shift):
    i = pl.program_id(0)
    j = pl.program_id(1)
    n_i = pl.num_programs(0)

    def norm_mod(xv):
        return (_rms(xv) * g_ref[...]) * (1.0 + sc_ref[...]) + sh_ref[...]

    @pl.when(j == 0)
    def _():
        h_s[...] = _bf(norm_mod(x_ref[...]))
        hp = jnp.where(i > 0, norm_mod(xp_ref[...]), 0.0)
        hn = jnp.where(i < n_i - 1, norm_mod(xn_ref[...]), 0.0)
        hh_s[0:8, :] = hp
        hh_s[8:16, :] = hn

    w = w_ref[...]
    p = _bdot(h_s[...], w)

    @pl.when(j < n_shift)
    def _():
        ph = _bdot(_bf(hh_s[...]), w)
        row = lax.broadcasted_iota(jnp.int32, p.shape, 0)
        prev = jnp.where(row == 0, ph[7:8, :], pltpu.roll(p, 1, 0))
        nxt = jnp.where(row == tm - 1, ph[8:9, :], pltpu.roll(p, tm - 1, 0))
        mu = mu_ref[...]
        o_ref[...] = p + mu[0:1, :] * (prev - p) + mu[1:2, :] * (nxt - p)

    @pl.when(j >= n_shift)
    def _():
        o_ref[...] = p


def _in_proj(x2d, g, sc, sh, w_bf, mu_pad):
    t_len = x2d.shape[0]
    tm = min(t_len, 1024)
    tn = PROJ_TN
    n_shift = SHIFT_PAD // tn
    tb8 = tm // 8
    nb8 = t_len // 8
    kern = functools.partial(_in_proj_kernel, tm=tm, n_shift=n_shift)
    vec = lambda: pl.BlockSpec((1, D_MODEL), lambda i, j: (0, 0))
    return pl.pallas_call(
        kern,
        out_shape=jax.ShapeDtypeStruct((t_len, PROJ_COLS), F32),
        grid=(t_len // tm, PROJ_COLS // tn),
        in_specs=[
            pl.BlockSpec((tm, D_MODEL), lambda i, j: (i, 0)),
            pl.BlockSpec((8, D_MODEL), lambda i, j: (jnp.maximum(i * tb8 - 1, 0), 0)),
            pl.BlockSpec((8, D_MODEL), lambda i, j: (jnp.minimum((i + 1) * tb8, nb8 - 1), 0)),
            vec(), vec(), vec(),
            pl.BlockSpec((D_MODEL, tn), lambda i, j: (0, j)),
            pl.BlockSpec((2, tn), lambda i, j: (0, jnp.minimum(j, n_shift - 1))),
        ],
        out_specs=pl.BlockSpec((tm, tn), lambda i, j: (i, j)),
        scratch_shapes=[pltpu.VMEM((tm, D_MODEL), BF16), pltpu.VMEM((16, D_MODEL), F32)],
        compiler_params=_params(("arbitrary", "arbitrary")),
        name="in_proj",
    )(x2d, x2d, x2d, g, sc, sh, w_bf, mu_pad)


def _rwkv_prep_kernel(r_ref, k_ref, v_ref, lora_ref, kk_w, ka_w, rk_w, w0_ref, wup_ref, a0_ref, aup_ref, gup_ref,
                      bd_ref, kk_o, lw0_o, lw1_o, kd0_o, kd1_o, b0_o, b1_o, bonus_o, g_o):
    r = r_ref[...]
    k = k_ref[...]
    v = v_ref[...]
    lora = lora_ref[...]
    xw = lora[:, 0:W_LORA]
    xa = lora[:, W_LORA:W_LORA + ICLR_LORA]
    xg = lora[:, W_LORA + ICLR_LORA:]
    bd = bd_ref[...]

    kk = k * kk_w[...]
    nrm = jnp.sqrt(_dot_split_lhs(kk * kk, bd))
    kk = kk / jnp.maximum(nrm, 1e-12)
    kk_o[...] = kk

    tw = jnp.tanh(xw)
    lw_outs = (lw0_o, lw1_o)
    kd_outs = (kd0_o, kd1_o)
    b_outs = (b0_o, b1_o)
    kd_sum = None
    for d in range(2):
        z = w0_ref[d:d + 1, :] + _dot3(tw, wup_ref[d])
        w_log = -_softplus(-z) - 0.5
        lw_outs[d][...] = -jnp.exp(w_log)
        a_d = _sigmoid(a0_ref[d:d + 1, :] + _dot3(xa, aup_ref[d]))
        kd = k * (1.0 + (a_d - 1.0) * ka_w[...])
        kd_outs[d][...] = kd
        b_outs[d][...] = kk * a_d
        kd_sum = kd if kd_sum is None else kd_sum + kd
    g_o[...] = _dot3(_sigmoid(xg), gup_ref[...])
    bonus_o[...] = _dot_split_lhs(r * kd_sum * rk_w[...], bd) * v


def _rwkv_prep(proj, lp, bd):
    t_len = proj.shape[0]
    tm = min(t_len, 512)
    aw = A_WIDTH
    col = lambda c: pl.BlockSpec((tm, aw), lambda i: (i, c))
    vec = lambda: pl.BlockSpec((1, aw), lambda i: (0, 0))
    full = lambda shp: pl.BlockSpec(shp, lambda i: (0,) * len(shp))
    outs = [jax.ShapeDtypeStruct((t_len, aw), F32)] * 9
    return pl.pallas_call(
        _rwkv_prep_kernel,
        out_shape=outs,
        grid=(t_len // tm,),
        in_specs=[
            col(0), col(1), col(2),
            pl.BlockSpec((tm, LORA_COLS), lambda i: (i, 3 * aw // LORA_COLS)),
            vec(), vec(), vec(),
            full((2, aw)), full((2, W_LORA, aw)), full((2, aw)), full((2, ICLR_LORA, aw)), full((G_LORA, aw)),
            full((aw, aw)),
        ],
        out_specs=[pl.BlockSpec((tm, aw), lambda i: (i, 0))] * 9,
        compiler_params=_params(("arbitrary",)),
        name="rwkv_prep",
    )(proj, proj, proj, proj, lp["k_k"], lp["k_a"], lp["r_k"], lp["w0"], lp["w_up"], lp["a0"], lp["a_up"],
      lp["g_up"], bd)


def _rwkv_scan_kernel(rf, vf, kkf, lwf, kdf, bf_, rb, vb, kkb, lwb, kdb, bb, s0_ref, of_ref, ob_ref, sfin_ref, s_s,
                      *, tb):
    step = pl.program_id(0)

    @pl.when(step == 0)
    def _():
        s_s[...] = s0_ref[...]

    c = RWKV_CHUNK
    n = A_HEAD_DIM
    assert c == n
    pw = 2 * n
    row = lax.broadcasted_iota(jnp.int32, (c, pw), 0)
    lane = lax.broadcasted_iota(jnp.int32, (c, pw), 1)
    col = lane % c
    first = lane < n
    eye = (row == col).astype(F32)
    r_c = lax.broadcasted_iota(jnp.int32, (c, c), 0)
    c_c = lax.broadcasted_iota(jnp.int32, (c, c), 1)
    n_ch = tb // c
    dirs = ((rf, vf, kkf, lwf, kdf, bf_, of_ref), (rb, vb, kkb, lwb, kdb, bb, ob_ref))
    pairs = range(A_HEADS // 2)
    psl = [slice(q * pw, (q + 1) * pw) for q in pairs]

    def bd(x):
        z = jnp.zeros_like(x)
        return jnp.concatenate([jnp.where(first, x, z), jnp.where(first, z, x)], axis=0)

    def chunk_terms(d, refs, ci):
        r_ref, v_ref, kk_ref, lw_ref, kd_ref, b_ref, _ = refs
        if d == 0:
            incl, strict, tri = row >= col, row > col, r_c >= c_c
        else:
            incl, strict, tri = row <= col, row < col, r_c <= c_c
        rows = slice(ci * c, (ci + 1) * c)
        lw = lw_ref[rows, :]
        r = r_ref[rows, :]
        kk = kk_ref[rows, :]
        kd = kd_ref[rows, :]
        b = b_ref[rows, :]
        cl = _dot_split_rhs(tri.astype(BF16), lw)
        cl_tot = cl[c - 1:c, :] if d == 0 else cl[0:1, :]
        e_neg = jnp.exp(-cl)
        e_end = jnp.exp(cl_tot - cl)
        g_tot = jnp.exp(cl_tot)
        rt = r * jnp.exp(cl)
        x_all = _bf(jnp.concatenate([-(kk * jnp.exp(cl - lw)), rt], axis=0))
        bt_all = _bf(b * e_neg)
        kt_all = _bf(kd * e_neg)
        return {
            "rows": rows, "strict": strict, "incl": incl,
            "x": [x_all[:, ps] for ps in psl],
            "z": [jnp.concatenate([bd(bt_all[:, ps]), bd(kt_all[:, ps])], axis=0) for ps in psl],
            "bd_bh": [bd(_bf(b * e_end)[:, ps]) for ps in psl],
            "bd_kh": [bd(_bf(kd * e_end)[:, ps]) for ps in psl],
            "bd_v": [bd(_bf(v_ref[rows, :])[:, ps]) for ps in psl],
            "rt": [rt[:, ps] for ps in psl],
            "g_tot": [g_tot[:, ps] for ps in psl],
        }

    chunks = [(d, cc) for cc in range(n_ch) for d in range(2)]
    prep = {(d, cc): chunk_terms(d, dirs[d], cc if d == 0 else n_ch - 1 - cc) for d, cc in chunks}
    units = [(k, q) for k in chunks for q in pairs]

    def per_unit(name, indexed=True):
        return [prep[k][name][q] if indexed else prep[k][name] for k, q in units]

    strict, incl = per_unit("strict", False), per_unit("incl", False)
    x_u, bd_bh, bd_kh, bd_v, rt_u = (per_unit(s) for s in ("x", "bd_bh", "bd_kh", "bd_v", "rt"))
    g = [_dot_nt(x, z) for x, z in zip(x_u, per_unit("z"))]
    a_ab = [jnp.where(s, gi[:c, :pw], 0.0) for s, gi in zip(strict, g)]
    a_rb = [_bf(jnp.where(i, gi[c:, :pw], 0.0)) for i, gi in zip(incl, g)]
    a_k = [_bf(jnp.concatenate([jnp.where(s, gi[:c, pw:], 0.0), jnp.where(i, gi[c:, pw:], 0.0)], axis=0))
           for s, i, gi in zip(strict, incl, g)]
    a2 = [_bdot(_bf(a), bd(_bf(a))) for a in a_ab]
    pa = [jnp.concatenate([eye + a, sq], axis=0) for a, sq in zip(a_ab, a2)]
    for _ in range(4):
        nxt = [_bdot(_bf(x), bd(_bf(x[c:]))) for x in pa]
        pa = [jnp.concatenate([x[:c] + y[:c], y[c:]], axis=0) for x, y in zip(pa, nxt)]
    p = [_bf(x[:c] + _bdot(_bf(x[:c]), bd(_bf(x[c:])))) for x in pa]
    vk = [_bdot(a, v) for a, v in zip(a_k, bd_v)]
    tw = [_bf(_bdot(pi, jnp.concatenate([bd(x[:c]), bd(_bf(vki[:c]))], axis=1)))
          for pi, vki, x in zip(p, vk, x_u)]
    bd_at = [bd(t[:, :pw]) for t in tw]
    bd_w = [bd(t[:, pw:]) for t in tw]
    mt = [_bf(_dot_tn(a, bh)) for a, bh in zip(bd_at, bd_bh)]
    ntf = [_dot_tn(jnp.concatenate([w, v], axis=0), jnp.concatenate([bh, kh], axis=0))
           for w, v, bh, kh in zip(bd_w, bd_v, bd_bh, bd_kh)]
    nt = [m[:n] + m[n:] for m in ntf]
    rw = [_bdot(a, jnp.concatenate([at, w], axis=1)) for a, at, w in zip(a_rb, bd_at, bd_w)]
    ry = [_bf(r + w[:, :pw]) for w, r in zip(rw, rt_u)]
    y0 = [w[:, pw:] + vki[c:] for w, vki in zip(rw, vk)]
    term = {u: vals for u, vals in zip(units, zip(mt, nt, ry, y0, per_unit("g_tot")))}

    state = [[s_s[d, q] for q in pairs] for d in range(2)]
    dq = [(d, q) for d in range(2) for q in pairs]
    for cc in range(n_ch):
        mt_c, nt_c, ry_c, y0_c, gt_c = zip(*[term[((d, cc), q)] for d, q in dq])
        s0 = [state[d][q] for d, q in dq]
        ys = [_dot_nt(ry_c[i], bd(_bf(s0[i]))) + y0_c[i] for i in range(len(dq))]
        upd = [_bdot(jnp.concatenate(_split(s0[i], 2), axis=0), mt_c[i]) for i in range(len(dq))]
        for i, (d, q) in enumerate(dq):
            state[d][q] = s0[i] * gt_c[i] + (upd[i][:n] + upd[i][n:]) + nt_c[i]
        for d in range(2):
            y_d = [ys[i] for i, (dd, _) in enumerate(dq) if dd == d]
            dirs[d][6][prep[(d, cc)]["rows"], :] = jnp.concatenate(y_d, axis=1)
    for d in range(2):
        for q in pairs:
            s_s[d, q] = state[d][q]

    @pl.when(step == pl.num_programs(0) - 1)
    def _():
        sfin_ref[...] = s_s[...]


def _rwkv_scan(proj, prep, s0):
    kk, lw0, lw1, kd0, kd1, b0, b1 = prep[:7]
    t_len = proj.shape[0]
    tb = 2 * RWKV_CHUNK
    nb = t_len // tb
    aw = A_WIDTH
    fwd = lambda c: pl.BlockSpec((tb, aw), lambda i: (i, c))
    bwd = lambda c: pl.BlockSpec((tb, aw), lambda i: (nb - 1 - i, c))
    st = pl.BlockSpec(RWKV_STATE_SHAPE, lambda i: (0, 0, 0, 0))
    kern = functools.partial(_rwkv_scan_kernel, tb=tb)
    return pl.pallas_call(
        kern,
        out_shape=[jax.ShapeDtypeStruct((t_len, aw), F32), jax.ShapeDtypeStruct((t_len, aw), F32),
                   jax.ShapeDtypeStruct(RWKV_STATE_SHAPE, F32)],
        grid=(nb,),
        in_specs=[fwd(0), fwd(2), fwd(0), fwd(0), fwd(0), fwd(0),
                  bwd(0), bwd(2), bwd(0), bwd(0), bwd(0), bwd(0), st],
        out_specs=[fwd(0), bwd(0), st],
        scratch_shapes=[pltpu.VMEM(RWKV_STATE_SHAPE, F32)],
        compiler_params=_params(("arbitrary",)),
        name="rwkv_scan",
    )(proj, proj, kk, lw0, kd0, b0, proj, proj, kk, lw1, kd1, b1, s0)


def _retention_kernel(qf, kf, vf, cosf, sinf, qb, kb, vb, cosb, sinb, lgt_ref, r0_ref, yf_ref, yb_ref, rfin_ref, r_s):
    step = pl.program_id(0)

    @pl.when(step == 0)
    def _():
        r_s[...] = r0_ref[...]

    c = RET_CHUNK
    dk = B_QK_DIM
    dv = B_V_DIM
    lg_all = -_softplus(-lgt_ref[...])
    rowf = lax.broadcasted_iota(jnp.int32, (c, c), 0).astype(F32)
    colf = lax.broadcasted_iota(jnp.int32, (c, c), 1).astype(F32)
    lane = lax.broadcasted_iota(jnp.int32, (c, dk), 1)
    first_half = (lane % 64) < 32
    dirs = ((qf, kf, vf, cosf, sinf, yf_ref), (qb, kb, vb, cosb, sinb, yb_ref))
    for d, (q_ref, k_ref, v_ref, cos_ref, sin_ref, y_ref) in enumerate(dirs):
        cos = cos_ref[...]
        sin = sin_ref[...]
        diff = (rowf - colf) if d == 0 else (colf - rowf)
        pos = rowf if d == 0 else (c - 1.0) - rowf
        for h in range(B_HEADS):
            lg = lg_all[d * B_HEADS + h:d * B_HEADS + h + 1, :]
            qh = q_ref[:, h * dk:(h + 1) * dk]
            kh = k_ref[:, h * dk:(h + 1) * dk] * (dk ** -0.5)

            def rope(xv):
                swapped = jnp.where(first_half, pltpu.roll(xv, dk - 32, 1), pltpu.roll(xv, 32, 1))
                return xv * cos + swapped * sin

            qh = rope(qh)
            kh = rope(kh)
            vh = _bf(v_ref[:, h * dv:(h + 1) * dv])
            dmask = jnp.where(diff >= 0.0, jnp.exp(lg * jnp.maximum(diff, 0.0)), 0.0)
            scores = _dot_nt(_bf(qh), _bf(kh)) * dmask
            inner = _bdot(_bf(scores), vh)
            xi = jnp.exp(lg * (pos + 1.0))
            zeta = jnp.exp(lg * ((c - 1.0) - pos))
            r_prev = r_s[d, h]
            cross = _bdot(_bf(qh * xi), _bf(r_prev))
            y_ref[:, h * dv:(h + 1) * dv] = inner + cross
            kv = _dot_tn(_bf(kh * zeta), vh)
            g_chunk = jnp.exp(lg * float(c))
            r_s[d, h] = jnp.concatenate([g_chunk, g_chunk], axis=1) * r_prev + kv

    @pl.when(step == pl.num_programs(0) - 1)
    def _():
        rfin_ref[...] = r_s[...]


def _retention(proj, cos, sin, lgt, r0):
    t_len = proj.shape[0]
    c = RET_CHUNK
    nc = t_len // c
    qw = B_QK_WIDTH
    vw = B_V_WIDTH
    fq = lambda col: pl.BlockSpec((c, qw), lambda i: (i, col))
    bq = lambda col: pl.BlockSpec((c, qw), lambda i: (nc - 1 - i, col))
    st = pl.BlockSpec((2, B_HEADS, B_QK_DIM, B_V_DIM), lambda i: (0, 0, 0, 0))
    q_col = SHIFT_PAD // qw
    v_col = (SHIFT_PAD + 2 * qw) // vw
    return pl.pallas_call(
        _retention_kernel,
        out_shape=[jax.ShapeDtypeStruct((t_len, vw), F32), jax.ShapeDtypeStruct((t_len, vw), F32),
                   jax.ShapeDtypeStruct((2, B_HEADS, B_QK_DIM, B_V_DIM), F32)],
        grid=(nc,),
        in_specs=[
            fq(q_col), fq(q_col + 1), pl.BlockSpec((c, vw), lambda i: (i, v_col)),
            pl.BlockSpec((c, B_QK_DIM), lambda i: (i, 0)), pl.BlockSpec((c, B_QK_DIM), lambda i: (i, 0)),
            bq(q_col), bq(q_col + 1), pl.BlockSpec((c, vw), lambda i: (nc - 1 - i, v_col)),
            pl.BlockSpec((c, B_QK_DIM), lambda i: (nc - 1 - i, 0)),
            pl.BlockSpec((c, B_QK_DIM), lambda i: (nc - 1 - i, 0)),
            pl.BlockSpec((2 * B_HEADS, 128), lambda i: (0, 0)), st,
        ],
        out_specs=[pl.BlockSpec((c, vw), lambda i: (i, 0)), pl.BlockSpec((c, vw), lambda i: (nc - 1 - i, 0)), st],
        scratch_shapes=[pltpu.VMEM((2, B_HEADS, B_QK_DIM, B_V_DIM), F32)],
        compiler_params=_params(("arbitrary",)),
        name="retention",
    )(proj, proj, proj, cos, sin, proj, proj, proj, cos, sin, lgt, r0)


def _mix_out_kernel(of_ref, ob_ref, bonus_ref, g_ref, yf_ref, yb_ref, gb_ref, ga_ref, gbb_ref, x_ref,
                    alnw, alnb, rlnw, rlnb, npm, npf, g1_ref, sc2_ref, sh2_ref, wa_ref, wb_ref, wo_ref, bd_ref,
                    x1_ref, h2_ref, h3_ref):
    bd = bd_ref[...]
    o = of_ref[...] + ob_ref[...]
    mu = _dot_split_lhs(o, bd) * (1.0 / A_HEAD_DIM)
    oc = o - mu
    var = _dot_split_lhs(oc * oc, bd) * (1.0 / A_HEAD_DIM)
    ya = oc * lax.rsqrt(var + RWKV_GN_EPS) * alnw[...] + alnb[...] + bonus_ref[...]
    ya = _bdot(_bf(ya * g_ref[...]), wa_ref[...])

    y = yf_ref[...] + yb_ref[...]
    parts = []
    for h in range(B_HEADS):
        seg = y[:, h * B_V_DIM:(h + 1) * B_V_DIM]
        m = jnp.mean(seg, axis=-1, keepdims=True)
        sc = seg - m
        vr = jnp.mean(sc * sc, axis=-1, keepdims=True)
        parts.append(sc * lax.rsqrt(vr + RET_GN_EPS))
    yn = jnp.concatenate(parts, axis=1) * rlnw[...] + rlnb[...]
    gb = gb_ref[...]
    yb = _bdot(_bf(yn * (gb * _sigmoid(gb))), wb_ref[...])

    merged = _sigmoid(ga_ref[...]) * ya + _sigmoid(gbb_ref[...]) * yb
    mix = _bdot(_bf(merged), wo_ref[...])
    x1 = x_ref[...] + g1_ref[...] * (_rms(mix) * npm[...])
    x1_ref[...] = x1
    h2 = (_rms(x1) * npf[...]) * (1.0 + sc2_ref[...]) + sh2_ref[...]
    h2_ref[...] = h2
    _rows_to_tiles(h2, h3_ref)


def _mix_out(x2d, proj, o_f, o_b, bonus, g, y_f, y_b, lp, vecs, bd):
    t_len = x2d.shape[0]
    tm = min(t_len, 256)
    aw = A_WIDTH
    d = D_MODEL
    ta = lambda: pl.BlockSpec((tm, aw), lambda i: (i, 0))
    td = lambda: pl.BlockSpec((tm, d), lambda i: (i, 0))
    pc = lambda c: pl.BlockSpec((tm, d), lambda i: (i, c))
    va = lambda: pl.BlockSpec((1, aw), lambda i: (0, 0))
    vd = lambda: pl.BlockSpec((1, d), lambda i: (0, 0))
    full = lambda shp: pl.BlockSpec(shp, lambda i: (0, 0))
    gcol = (SHIFT_PAD + 2 * B_QK_WIDTH + B_V_WIDTH) // d
    return pl.pallas_call(
        _mix_out_kernel,
        out_shape=[jax.ShapeDtypeStruct((t_len, d), F32), jax.ShapeDtypeStruct((t_len, d), F32),
                   jax.ShapeDtypeStruct((t_len * ROW_TILE, LANES), F32)],
        grid=(t_len // tm,),
        in_specs=[ta(), ta(), ta(), ta(), td(), td(), pc(gcol), pc(gcol + 1), pc(gcol + 2), td(),
                  va(), va(), vd(), vd(), vd(), vd(), vd(), vd(), vd(),
                  full((aw, d)), full((d, d)), full((d, d)), full((aw, aw))],
        out_specs=[td(), td(), pl.BlockSpec((tm * ROW_TILE, LANES), lambda i: (i, 0))],
        compiler_params=_params(("arbitrary",)),
        name="mix_out",
    )(o_f, o_b, bonus, g, y_f, y_b, proj, proj, proj, x2d,
      lp["ln_w"], lp["ln_b"], lp["ret_ln_w"], lp["ret_ln_b"], vecs["npm"], vecs["npf"], vecs["g1"], vecs["sc2"],
      vecs["sh2"], lp["w_a"], lp["w_b"], lp["w_o"], bd)


def _router_kernel(h_ref, rw_ref, bias_ref, sel_ref, wts_ref, rank_ref, cnt_ref, cnt_s, *, tm):
    step = pl.program_id(0)

    @pl.when(step == 0)
    def _():
        cnt_s[...] = jnp.zeros_like(cnt_s)

    ne = N_EXPERTS
    scores = _sigmoid(_dot3(h_ref[...], rw_ref[...]))
    work = scores + bias_ref[...]
    lane = lax.broadcasted_iota(jnp.int32, (tm, ne), 1).astype(F32)
    idxs = []
    vals = []
    for _ in range(TOP_K):
        m = jnp.max(work, axis=-1, keepdims=True)
        idx = jnp.min(jnp.where(work == m, lane, float(ne)), axis=-1, keepdims=True)
        oh = lane == idx
        vals.append(jnp.sum(jnp.where(oh, scores, 0.0), axis=-1, keepdims=True))
        idxs.append(idx)
        work = jnp.where(oh, -jnp.inf, work)
    sel_f = jnp.concatenate(idxs, axis=1)
    s_sel = jnp.concatenate(vals, axis=1)
    wts_ref[...] = s_sel / jnp.sum(s_sel, axis=1, keepdims=True) * ROUTED_SCALE
    sel_ref[...] = sel_f.astype(jnp.int32)

    hit = work == -jnp.inf
    r_i = lax.broadcasted_iota(jnp.int32, (tm, tm), 0)
    c_i = lax.broadcasted_iota(jnp.int32, (tm, tm), 1)
    before = _bdot((r_i > c_i).astype(BF16), hit.astype(BF16)) + cnt_s[...]
    ranks = [jnp.sum(jnp.where(lane == idxs[k], before, 0.0), axis=-1, keepdims=True) for k in range(TOP_K)]
    rank_ref[...] = jnp.concatenate(ranks, axis=1).astype(jnp.int32)
    cnt = cnt_s[...] + jnp.sum(hit.astype(F32), axis=0, keepdims=True)
    cnt_s[...] = cnt
    cnt_ref[...] = cnt.astype(jnp.int32)


def _router(h2, router_w, router_bias):
    t_len = h2.shape[0]
    tm = min(t_len, 256)
    kern = functools.partial(_router_kernel, tm=tm)
    tk = lambda: pl.BlockSpec((tm, TOP_K), lambda i: (i, 0))
    return pl.pallas_call(
        kern,
        out_shape=[jax.ShapeDtypeStruct((t_len, TOP_K), jnp.int32), jax.ShapeDtypeStruct((t_len, TOP_K), F32),
                   jax.ShapeDtypeStruct((t_len, TOP_K), jnp.int32), jax.ShapeDtypeStruct((1, N_EXPERTS), jnp.int32)],
        grid=(t_len // tm,),
        in_specs=[pl.BlockSpec((tm, D_MODEL), lambda i: (i, 0)),
                  pl.BlockSpec((D_MODEL, N_EXPERTS), lambda i: (0, 0)),
                  pl.BlockSpec((1, N_EXPERTS), lambda i: (0, 0))],
        out_specs=[tk(), tk(), tk(), pl.BlockSpec((1, N_EXPERTS), lambda i: (0, 0))],
        scratch_shapes=[pltpu.VMEM((1, N_EXPERTS), F32)],
        compiler_params=_params(("arbitrary",)),
        name="router",
    )(h2, router_w, router_bias.reshape(1, N_EXPERTS))


def _slot(sel_ref, rank_ref, start_ref, a):
    return start_ref[sel_ref[a]] + rank_ref[a]


def _dispatch_kernel(sel_ref, rank_ref, start_ref, cnt_ref, h_ref, xs_out, zrow, sem, zsem, *, tm):
    def row_copy(t, k):
        slot = _slot(sel_ref, rank_ref, start_ref, t * TOP_K + k)
        return pltpu.make_async_copy(_row_tile(h_ref, t), _row_tile(xs_out, slot), sem)

    def issue(t, carry):
        for k in range(TOP_K):
            row_copy(t, k).start(priority=k % 2)
        return carry

    lax.fori_loop(0, tm, issue, 0)

    def drain(t, carry):
        for k in range(TOP_K):
            row_copy(t, k).wait()
        return carry

    lax.fori_loop(0, tm, drain, 0)

    @pl.when(pl.program_id(0) == pl.num_programs(0) - 1)
    def _():
        zrow[...] = jnp.zeros_like(zrow)

        def for_pad_runs(e0, fn):
            def per_expert(e, carry):
                n = cnt_ref[e]
                n_pad = (n + MOE_BLOCK - 1) // MOE_BLOCK * MOE_BLOCK
                pad = n_pad - n
                off = start_ref[e] + n
                for bit in reversed(range(MOE_BLOCK.bit_length() - 1)):
                    size = 1 << bit
                    is_set = ((pad >> bit) & 1) == 1

                    @pl.when(is_set)
                    def _(off=off, size=size):
                        dst = xs_out.at[pl.ds(pl.multiple_of(off * ROW_TILE, ROW_TILE), size * ROW_TILE)]
                        fn(pltpu.make_async_copy(zrow.at[pl.ds(0, size * ROW_TILE)], dst, zsem))

                    off = off + jnp.where(is_set, size, 0)
                return carry

            lax.fori_loop(e0, e0 + ZERO_PAD_GROUP, per_expert, 0)

        def per_group(gi, carry):
            e0 = gi * ZERO_PAD_GROUP
            for_pad_runs(e0, lambda cp: cp.start())
            for_pad_runs(e0, lambda cp: cp.wait())
            return carry

        lax.fori_loop(0, N_EXPERTS // ZERO_PAD_GROUP, per_group, 0)


def _dispatch(h3, sel_flat, rank_flat, pad_start, counts, n_slots):
    t_len = h3.shape[0] // ROW_TILE
    tm = min(t_len, 256)
    kern = functools.partial(_dispatch_kernel, tm=tm)
    smem_blk = lambda: pl.BlockSpec((tm * TOP_K,), lambda i: (i,), memory_space=pltpu.SMEM)
    smem_all = lambda: pl.BlockSpec((N_EXPERTS,), lambda i: (0,), memory_space=pltpu.SMEM)
    return pl.pallas_call(
        kern,
        out_shape=jax.ShapeDtypeStruct((n_slots * ROW_TILE, LANES), F32),
        grid=(t_len // tm,),
        in_specs=[smem_blk(), smem_blk(), smem_all(), smem_all(),
                  pl.BlockSpec((tm * ROW_TILE, LANES), lambda i: (i, 0))],
        out_specs=pl.BlockSpec(memory_space=pl.ANY),
        scratch_shapes=[pltpu.VMEM((MOE_BLOCK // 2 * ROW_TILE, LANES), F32), pltpu.SemaphoreType.DMA(()),
                        pltpu.SemaphoreType.DMA(())],
        compiler_params=_params(("arbitrary",)),
        name="dispatch",
    )(sel_flat, rank_flat, pad_start, counts, h3)


def _expert_kernel(be_ref, nu_ref, xs_ref, wg_ref, wu_ref, wd_ref, ys_ref, wg_s, wu_s, wd_s):
    b = pl.program_id(0)
    used = b < nu_ref[0]
    new_expert = (b == 0) | (be_ref[b] != be_ref[jnp.maximum(b - 1, 0)])

    @pl.when(used & new_expert)
    def _():
        wg_s[...] = _bf(wg_ref[0])
        wu_s[...] = _bf(wu_ref[0])
        wd_s[...] = _bf(wd_ref[0])

    @pl.when(used)
    def _():
        xb = _bf(_tiles_to_rows(xs_ref, MOE_BLOCK))
        gate = _bdot(xb, wg_s[...])
        up = _bdot(xb, wu_s[...])
        act = gate * _sigmoid(gate) * up
        _rows_to_tiles(_bdot(_bf(act), wd_s[...]), ys_ref)

    @pl.when(jnp.logical_not(used))
    def _():
        ys_ref[...] = jnp.zeros_like(ys_ref)


def _experts(xs, block_e, n_used, w_gate, w_up, w_down):
    n_slots = xs.shape[0] // ROW_TILE
    n_blocks = n_slots // MOE_BLOCK
    rows = (MOE_BLOCK * ROW_TILE, LANES)
    grid_spec = pltpu.PrefetchScalarGridSpec(
        num_scalar_prefetch=2,
        grid=(n_blocks,),
        in_specs=[
            pl.BlockSpec(rows, lambda b, be, nu: (jnp.where(b < nu[0], b, 0), 0)),
            pl.BlockSpec((1, D_MODEL, EXPERT_FF), lambda b, be, nu: (be[b], 0, 0)),
            pl.BlockSpec((1, D_MODEL, EXPERT_FF), lambda b, be, nu: (be[b], 0, 0)),
            pl.BlockSpec((1, EXPERT_FF, D_MODEL), lambda b, be, nu: (be[b], 0, 0)),
        ],
        out_specs=pl.BlockSpec(rows, lambda b, be, nu: (b, 0)),
        scratch_shapes=[pltpu.VMEM((D_MODEL, EXPERT_FF), BF16), pltpu.VMEM((D_MODEL, EXPERT_FF), BF16),
                        pltpu.VMEM((EXPERT_FF, D_MODEL), BF16)],
    )
    return pl.pallas_call(
        _expert_kernel,
        out_shape=jax.ShapeDtypeStruct((n_slots * ROW_TILE, LANES), F32),
        grid_spec=grid_spec,
        compiler_params=_params(("arbitrary",)),
        name="experts",
    )(block_e, n_used, xs, w_gate, w_up, w_down)


def _combine_kernel(sel_ref, rank_ref, sel_nx, rank_nx, start_ref, ys_ref, wts_ref, h_ref, x1_ref, sg_ref, su_ref,
                    sd_ref, npo, g2_ref, o_ref, buf, sem, *, tm):
    i = pl.program_id(0)
    cur = i % 2

    def row_copy(s_ref, r_ref, half, t, k):
        slot = _slot(s_ref, r_ref, start_ref, t * TOP_K + k)
        return pltpu.make_async_copy(_row_tile(ys_ref, slot), _row_tile(buf, t, (half, k)), sem.at[half])

    def issue_tile(s_ref, r_ref, half):
        def issue(t, carry):
            for k in range(TOP_K):
                row_copy(s_ref, r_ref, half, t, k).start(priority=k % 2)
            return carry

        lax.fori_loop(0, tm, issue, 0)

    @pl.when(i == 0)
    def _():
        issue_tile(sel_ref, rank_ref, 0)

    @pl.when(i + 1 < pl.num_programs(0))
    def _():
        issue_tile(sel_nx, rank_nx, 1 - cur)

    hb = _bf(h_ref[...])
    gate = _bdot(hb, sg_ref[...])
    up = _bdot(hb, su_ref[...])
    shared = _bdot(_bf(gate * _sigmoid(gate) * up), sd_ref[...])

    def drain(t, carry):
        for k in range(TOP_K):
            row_copy(sel_ref, rank_ref, cur, t, k).wait()
        return carry

    lax.fori_loop(0, tm, drain, 0)

    wts = wts_ref[...]
    routed = _tiles_to_rows(buf, tm, (cur, 0)) * wts[:, 0:1]
    for k in range(1, TOP_K):
        routed = routed + _tiles_to_rows(buf, tm, (cur, k)) * wts[:, k:k + 1]
    o_ref[...] = x1_ref[...] + g2_ref[...] * (_rms(routed + shared) * npo[...])


def _combine(ys, sel_flat, rank_flat, pad_start, wts, h2, x1, sg, su, sd, npo, g2):
    t_len = h2.shape[0]
    tm = min(t_len, 256)
    d = D_MODEL
    n_tiles = t_len // tm
    kern = functools.partial(_combine_kernel, tm=tm)
    smem_blk = lambda: pl.BlockSpec((tm * TOP_K,), lambda i: (i,), memory_space=pltpu.SMEM)
    smem_nxt = lambda: pl.BlockSpec((tm * TOP_K,), lambda i: (jnp.minimum(i + 1, n_tiles - 1),),
                                    memory_space=pltpu.SMEM)
    td = lambda: pl.BlockSpec((tm, d), lambda i: (i, 0))
    vd = lambda: pl.BlockSpec((1, d), lambda i: (0, 0))
    return pl.pallas_call(
        kern,
        out_shape=jax.ShapeDtypeStruct((t_len, d), F32),
        grid=(n_tiles,),
        in_specs=[smem_blk(), smem_blk(), smem_nxt(), smem_nxt(),
                  pl.BlockSpec((N_EXPERTS,), lambda i: (0,), memory_space=pltpu.SMEM),
                  pl.BlockSpec(memory_space=pl.ANY),
                  pl.BlockSpec((tm, TOP_K), lambda i: (i, 0)),
                  td(), td(),
                  pl.BlockSpec((d, SHARED_FF), lambda i: (0, 0)), pl.BlockSpec((d, SHARED_FF), lambda i: (0, 0)),
                  pl.BlockSpec((SHARED_FF, d), lambda i: (0, 0)), vd(), vd()],
        out_specs=td(),
        scratch_shapes=[pltpu.VMEM((2, TOP_K, tm * ROW_TILE, LANES), F32), pltpu.SemaphoreType.DMA((2,))],
        compiler_params=_params(("arbitrary",)),
        name="combine",
    )(sel_flat, rank_flat, sel_flat, rank_flat, pad_start, ys, wts, h2, x1, sg, su, sd, npo, g2)


def _block_diag_ones(width, group):
    idx = np.arange(width) // group
    return jnp.asarray(idx[:, None] == idx[None, :], dtype=BF16)


def _rope_tables(t_len):
    pos = np.arange(t_len)
    rows = (pos // GRID_W).astype(np.float32)
    cols = (pos % GRID_W).astype(np.float32)
    quarter = B_QK_DIM // 4
    inv_freq = jnp.asarray(ROPE_BASE, F32) ** (-jnp.arange(quarter, dtype=F32) / quarter)
    ang_r = jnp.asarray(rows)[:, None] * inv_freq
    ang_c = jnp.asarray(cols)[:, None] * inv_freq
    cr, sr, cc, sc = jnp.cos(ang_r), jnp.sin(ang_r), jnp.cos(ang_c), jnp.sin(ang_c)
    return jnp.concatenate([cr, cr, cc, cc], axis=1), jnp.concatenate([-sr, sr, -sc, sc], axis=1)


def _token_mixer(x2d, g_pre, sc, sh, lp, cos, sin, states, bd):
    proj = _in_proj(x2d, g_pre, sc, sh, lp["w_in"], lp["mu"])
    prep = _rwkv_prep(proj, lp, bd)
    o_f, o_b, s_fin = _rwkv_scan(proj, prep, states[0])
    y_f, y_b, r_fin = _retention(proj, cos, sin, lp["lgt"], states[1])
    return proj, prep, (o_f, o_b), (y_f, y_b), (s_fin, r_fin)


def kernel(x, c, ctx, c_ctx, w_mod, b_mod, norm_pre_mix, norm_post_mix, norm_pre_ffn, norm_post_ffn, w_in, shift_mu,
           rwkv_w0, rwkv_w_up, rwkv_a0, rwkv_a_up, rwkv_g_up, rwkv_k_k, rwkv_k_a, rwkv_r_k, rwkv_ln_w, rwkv_ln_b,
           w_branch_a, ret_decay_logit, ret_ln_w, ret_ln_b, w_branch_b, w_out, router_w, router_bias, exp_w_gate,
           exp_w_up, exp_w_down, sh_w_gate, sh_w_up, sh_w_down):
    d = D_MODEL
    assert x.shape[0] == 1 and w_in.shape[0] == 1, "single batch element, single layer"
    t_len = x.shape[1]
    x2d = x.reshape(t_len, d)
    ctx2d = ctx.reshape(ctx.shape[1], d)
    row = lambda a: a.reshape(1, -1)

    cs = jnp.zeros((8, d), F32).at[0].set(c[0]).at[1].set(c_ctx)
    mod = _modulation(cs, w_mod[0], b_mod[0])
    sh1, sc1, g1, sh2, sc2, g2 = [mod[0:1, i * d:(i + 1) * d] for i in range(6)]
    csh1, csc1 = mod[1:2, 0:d], mod[1:2, d:2 * d]

    w_in_p = jnp.concatenate(
        [w_in[0][:, :SHIFT_COLS], jnp.zeros((d, SHIFT_PAD - SHIFT_COLS), F32), w_in[0][:, SHIFT_COLS:]], axis=1)
    lp = {
        "w_in": w_in_p.astype(BF16),
        "mu": jnp.pad(shift_mu[0], ((0, 0), (0, SHIFT_PAD - SHIFT_COLS))),
        "k_k": row(rwkv_k_k[0]), "k_a": row(rwkv_k_a[0]), "r_k": row(rwkv_r_k[0]),
        "w0": rwkv_w0[0], "w_up": rwkv_w_up[0], "a0": rwkv_a0[0], "a_up": rwkv_a_up[0], "g_up": rwkv_g_up[0],
        "ln_w": row(rwkv_ln_w[0]), "ln_b": row(rwkv_ln_b[0]),
        "ret_ln_w": row(ret_ln_w[0]), "ret_ln_b": row(ret_ln_b[0]),
        "lgt": jnp.broadcast_to(ret_decay_logit[0].reshape(2 * B_HEADS, 1), (2 * B_HEADS, 128)),
        "w_a": w_branch_a[0].astype(BF16), "w_b": w_branch_b[0].astype(BF16), "w_o": w_out[0].astype(BF16),
    }
    bd = _block_diag_ones(A_WIDTH, A_HEAD_DIM)
    g_pre = row(norm_pre_mix[0])

    t_ctx = ctx2d.shape[0]
    zero_states = (jnp.zeros(RWKV_STATE_SHAPE, F32),
                   jnp.zeros((2, B_HEADS, B_QK_DIM, B_V_DIM), F32))
    ones = jnp.ones((t_ctx, B_QK_DIM), F32)
    *_, ctx_states = _token_mixer(ctx2d, g_pre, csc1, csh1, lp, ones, jnp.zeros_like(ones), zero_states, bd)

    cos, sin = _rope_tables(t_len)
    proj, prep, (o_f, o_b), (y_f, y_b), _ = _token_mixer(x2d, g_pre, sc1, sh1, lp, cos, sin, ctx_states, bd)
    vecs = {"npm": row(norm_post_mix[0]), "npf": row(norm_pre_ffn[0]), "g1": g1, "sc2": sc2, "sh2": sh2}
    x1, h2, h3 = _mix_out(x2d, proj, o_f, o_b, prep[7], prep[8], y_f, y_b, lp, vecs, bd)

    sel, wts, rank, counts = _router(h2, router_w[0], router_bias[0])
    counts = counts.reshape(N_EXPERTS)
    padded = (counts + MOE_BLOCK - 1) // MOE_BLOCK * MOE_BLOCK
    pad_end = jnp.cumsum(padded)
    pad_start = (pad_end - padded).astype(jnp.int32)
    n_assign = t_len * TOP_K
    n_blocks = (n_assign + N_EXPERTS * (MOE_BLOCK - 1) + MOE_BLOCK - 1) // MOE_BLOCK
    block_start = jnp.arange(n_blocks, dtype=jnp.int32) * MOE_BLOCK
    block_e = jnp.minimum(jnp.sum(pad_end[None, :] <= block_start[:, None], axis=1), N_EXPERTS - 1).astype(jnp.int32)
    n_used = (pad_end[-1:] // MOE_BLOCK).astype(jnp.int32)
    sel_flat = sel.reshape(n_assign)
    rank_flat = rank.reshape(n_assign)

    xs = _dispatch(h3, sel_flat, rank_flat, pad_start, counts, n_blocks * MOE_BLOCK)
    ys = _experts(xs, block_e, n_used, exp_w_gate[0], exp_w_up[0], exp_w_down[0])
    out = _combine(ys, sel_flat, rank_flat, pad_start, wts, h2, x1, sh_w_gate[0].astype(BF16),
                   sh_w_up[0].astype(BF16), sh_w_down[0].astype(BF16), row(norm_post_ffn[0]), g2)
    return out.reshape(x.shape)
```

```python
import functools

import jax
import jax.numpy as jnp
import numpy as np
from jax import lax
from jax.experimental import pallas as pl
from jax.experimental.pallas import tpu as pltpu

F32 = jnp.float32
BF16 = jnp.bfloat16

D_MODEL = 1024
GRID_W = 64
NORM_EPS = 1e-6

A_HEAD_DIM = 64
A_WIDTH = D_MODEL // 2
A_HEADS = A_WIDTH // A_HEAD_DIM
W_LORA = 64
ICLR_LORA = 64
G_LORA = 128
RWKV_GN_EPS = 64e-5

B_HEADS = 4
B_QK_WIDTH = D_MODEL // 2
B_V_WIDTH = D_MODEL
B_QK_DIM = B_QK_WIDTH // B_HEADS
B_V_DIM = B_V_WIDTH // B_HEADS
RET_CHUNK = 128
RET_GN_EPS = 1e-5
ROPE_BASE = 10000.0

SHIFT_COLS = 3 * A_WIDTH + W_LORA + ICLR_LORA + G_LORA
LORA_COLS = W_LORA + ICLR_LORA + G_LORA
SHIFT_PAD = 2048
PROJ_COLS = SHIFT_PAD + 2 * B_QK_WIDTH + 2 * B_V_WIDTH + 2 * D_MODEL

N_EXPERTS = 256
TOP_K = 8
EXPERT_FF = D_MODEL // 4
SHARED_FF = D_MODEL // 4
ROUTED_SCALE = 2.5
MOE_BLOCK = 512
ZERO_PAD_GROUP = 16

RWKV_CHUNK = 64
RWKV_STATE_SHAPE = (2, A_HEADS // 2, A_HEAD_DIM, 2 * A_HEAD_DIM)
PROJ_TN = 1024
VMEM_LIMIT = 48 * 1024 * 1024


def _params(sem):
    return pltpu.CompilerParams(dimension_semantics=sem, vmem_limit_bytes=VMEM_LIMIT)


def _bf(a):
    return a.astype(BF16)


def _bdot(a, b):
    return jnp.dot(a, b, preferred_element_type=F32)


def _dot_nt(a, b):
    return lax.dot_general(a, b, (((1,), (1,)), ((), ())), preferred_element_type=F32)


def _dot_tn(a, b):
    return lax.dot_general(a, b, (((0,), (0,)), ((), ())), preferred_element_type=F32)


def _split(a, n):
    out = []
    rem = a
    for _ in range(n):
        p = _bf(rem)
        out.append(p)
        rem = rem - p.astype(F32)
    return out


def _dot_split_lhs(a, b_bf, n=3):
    acc = None
    for p in _split(a, n):
        t = _bdot(p, b_bf)
        acc = t if acc is None else acc + t
    return acc


def _dot_split_rhs(a_bf, b, n=3):
    acc = None
    for p in _split(b, n):
        t = _bdot(a_bf, p)
        acc = t if acc is None else acc + t
    return acc


def _dot3(a, b):
    ah, al = _split(a, 2)
    bh, bl = _split(b, 2)
    return _bdot(ah, bh) + (_bdot(ah, bl) + _bdot(al, bh))


def _sigmoid(x):
    return 1.0 / (1.0 + jnp.exp(-x))


def _softplus(x):
    return jnp.maximum(x, 0.0) + jnp.log1p(jnp.exp(-jnp.abs(x)))


def _rms(x):
    return x * lax.rsqrt(jnp.mean(x * x, axis=-1, keepdims=True) + NORM_EPS)


LANES = 128
ROW_TILE = D_MODEL // LANES


def _rows_to_tiles(x2d, ref, lead=()):
    n = x2d.shape[0]
    for j in range(ROW_TILE):
        ref[lead + (pl.ds(j, n, stride=ROW_TILE), slice(None))] = x2d[:, j * LANES:(j + 1) * LANES]


def _tiles_to_rows(ref, n, lead=()):
    return jnp.concatenate(
        [ref[lead + (pl.ds(j, n, stride=ROW_TILE), slice(None))] for j in range(ROW_TILE)], axis=1)


def _row_tile(ref, r, lead=(), tile=ROW_TILE):
    return ref.at[lead + (pl.ds(pl.multiple_of(r * tile, tile), tile),)]


PACK_TILE = ROW_TILE // 2
HIGH_HALF = 0xFFFF0000


def _rows_to_packed_tiles(x2d, ref):
    n = x2d.shape[0]
    half = D_MODEL // 2
    for j in range(PACK_TILE):
        lo = _bf(x2d[:, j * LANES:(j + 1) * LANES]).astype(F32)
        hi = _bf(x2d[:, half + j * LANES:half + (j + 1) * LANES]).astype(F32)
        word = (pltpu.bitcast(hi, jnp.uint32) & jnp.uint32(HIGH_HALF)) | (pltpu.bitcast(lo, jnp.uint32) >> 16)
        ref[pl.ds(j, n, stride=PACK_TILE), :] = word


def _packed_tiles_to_rows(ref, n):
    lo, hi = [], []
    for j in range(PACK_TILE):
        word = ref[pl.ds(j, n, stride=PACK_TILE), :]
        lo.append(_bf(pltpu.bitcast(word << 16, F32)))
        hi.append(_bf(pltpu.bitcast(word & jnp.uint32(HIGH_HALF), F32)))
    return jnp.concatenate(lo + hi, axis=1)


def _mod_kernel(cs_ref, w_ref, b_ref, o_ref):
    cs = cs_ref[...]
    s = cs * _sigmoid(cs)
    o_ref[...] = _dot3(s, w_ref[...]) + b_ref[...]


def _modulation(cs, w_mod, b_mod):
    n_out = w_mod.shape[1]
    tn = 1536
    return pl.pallas_call(
        _mod_kernel,
        out_shape=jax.ShapeDtypeStruct((8, n_out), F32),
        grid=(n_out // tn,),
        in_specs=[
            pl.BlockSpec((8, D_MODEL), lambda j: (0, 0)),
            pl.BlockSpec((D_MODEL, tn), lambda j: (0, j)),
            pl.BlockSpec((1, tn), lambda j: (0, j)),
        ],
        out_specs=pl.BlockSpec((8, tn), lambda j: (0, j)),
        compiler_params=_params(("arbitrary",)),
        name="modulation",
    )(cs, w_mod, b_mod.reshape(1, n_out))


def _in_proj_kernel(x_ref, xp_ref, xn_ref, g_ref, sc_ref, sh_ref, w_ref, mu_ref, o_ref, ob_ref, h_s, hh_s,
                    *, tm, n_shift):
    i = pl.program_id(0)
    j = pl.program_id(1)
    n_i = pl.num_programs(0)

    def norm_mod(xv):
        return (_rms(xv) * g_ref[...]) * (1.0 + sc_ref[...]) + sh_ref[...]

    @pl.when(j == 0)
    def _():
        h_s[...] = _bf(norm_mod(x_ref[...]))
        hp = jnp.where(i > 0, norm_mod(xp_ref[...]), 0.0)
        hn = jnp.where(i < n_i - 1, norm_mod(xn_ref[...]), 0.0)
        hh_s[0:8, :] = hp
        hh_s[8:16, :] = hn

    w = w_ref[...]
    p = _bdot(h_s[...], w)

    @pl.when(j < n_shift)
    def _():
        ph = _bdot(_bf(hh_s[...]), w)
        row = lax.broadcasted_iota(jnp.int32, p.shape, 0)
        prev = jnp.where(row == 0, ph[7:8, :], pltpu.roll(p, 1, 0))
        nxt = jnp.where(row == tm - 1, ph[8:9, :], pltpu.roll(p, tm - 1, 0))
        mu = mu_ref[...]
        o_ref[...] = p + mu[0:1, :] * (prev - p) + mu[1:2, :] * (nxt - p)

    @pl.when(j >= n_shift)
    def _():
        ob_ref[...] = _bf(p)


def _in_proj(x2d, g, sc, sh, w_bf, mu_pad):
    t_len = x2d.shape[0]
    tm = min(t_len, 1024)
    tn = PROJ_TN
    n_shift = SHIFT_PAD // tn
    tb8 = tm // 8
    nb8 = t_len // 8
    kern = functools.partial(_in_proj_kernel, tm=tm, n_shift=n_shift)
    vec = lambda: pl.BlockSpec((1, D_MODEL), lambda i, j: (0, 0))
    return pl.pallas_call(
        kern,
        out_shape=[jax.ShapeDtypeStruct((t_len, SHIFT_PAD), F32),
                   jax.ShapeDtypeStruct((t_len, PROJ_COLS - SHIFT_PAD), BF16)],
        grid=(t_len // tm, PROJ_COLS // tn),
        in_specs=[
            pl.BlockSpec((tm, D_MODEL), lambda i, j: (i, 0)),
            pl.BlockSpec((8, D_MODEL), lambda i, j: (jnp.maximum(i * tb8 - 1, 0), 0)),
            pl.BlockSpec((8, D_MODEL), lambda i, j: (jnp.minimum((i + 1) * tb8, nb8 - 1), 0)),
            vec(), vec(), vec(),
            pl.BlockSpec((D_MODEL, tn), lambda i, j: (0, j)),
            pl.BlockSpec((2, tn), lambda i, j: (0, jnp.minimum(j, n_shift - 1))),
        ],
        out_specs=[pl.BlockSpec((tm, tn), lambda i, j: (i, jnp.minimum(j, n_shift - 1))),
                   pl.BlockSpec((tm, tn), lambda i, j: (i, jnp.maximum(j - n_shift, 0)))],
        scratch_shapes=[pltpu.VMEM((tm, D_MODEL), BF16), pltpu.VMEM((16, D_MODEL), F32)],
        compiler_params=_params(("arbitrary", "arbitrary")),
        name="in_proj",
    )(x2d, x2d, x2d, g, sc, sh, w_bf, mu_pad)


def _rwkv_prep_kernel(r_ref, k_ref, v_ref, lora_ref, kk_w, ka_w, rk_w, w0_ref, wup_ref, a0_ref, aup_ref, gup_ref,
                      bd_ref, kk_o, lw0_o, lw1_o, kd0_o, kd1_o, b0_o, b1_o, bonus_o, g_o):
    r = r_ref[...]
    k = k_ref[...]
    v = v_ref[...]
    lora = lora_ref[...]
    xw = lora[:, 0:W_LORA]
    xa = lora[:, W_LORA:W_LORA + ICLR_LORA]
    xg = lora[:, W_LORA + ICLR_LORA:]
    bd = bd_ref[...]

    kk = k * kk_w[...]
    nrm = jnp.sqrt(_dot_split_lhs(kk * kk, bd))
    kk = kk / jnp.maximum(nrm, 1e-12)
    kk_o[...] = kk

    tw = jnp.tanh(xw)
    lw_outs = (lw0_o, lw1_o)
    kd_outs = (kd0_o, kd1_o)
    b_outs = (b0_o, b1_o)
    kd_sum = None
    for d in range(2):
        z = w0_ref[d:d + 1, :] + _dot3(tw, wup_ref[d])
        w_log = -_softplus(-z) - 0.5
        lw_outs[d][...] = -jnp.exp(w_log)
        a_d = _sigmoid(a0_ref[d:d + 1, :] + _dot3(xa, aup_ref[d]))
        kd = k * (1.0 + (a_d - 1.0) * ka_w[...])
        kd_outs[d][...] = kd
        b_outs[d][...] = kk * a_d
        kd_sum = kd if kd_sum is None else kd_sum + kd
    g_o[...] = _dot3(_sigmoid(xg), gup_ref[...])
    bonus_o[...] = _dot_split_lhs(r * kd_sum * rk_w[...], bd) * v


def _rwkv_prep(proj, lp, bd):
    t_len = proj.shape[0]
    tm = min(t_len, 512)
    aw = A_WIDTH
    col = lambda c: pl.BlockSpec((tm, aw), lambda i: (i, c))
    vec = lambda: pl.BlockSpec((1, aw), lambda i: (0, 0))
    full = lambda shp: pl.BlockSpec(shp, lambda i: (0,) * len(shp))
    outs = [jax.ShapeDtypeStruct((t_len, aw), F32)] * 9
    return pl.pallas_call(
        _rwkv_prep_kernel,
        out_shape=outs,
        grid=(t_len // tm,),
        in_specs=[
            col(0), col(1), col(2),
            pl.BlockSpec((tm, LORA_COLS), lambda i: (i, 3 * aw // LORA_COLS)),
            vec(), vec(), vec(),
            full((2, aw)), full((2, W_LORA, aw)), full((2, aw)), full((2, ICLR_LORA, aw)), full((G_LORA, aw)),
            full((aw, aw)),
        ],
        out_specs=[pl.BlockSpec((tm, aw), lambda i: (i, 0))] * 9,
        compiler_params=_params(("arbitrary",)),
        name="rwkv_prep",
    )(proj, proj, proj, proj, lp["k_k"], lp["k_a"], lp["r_k"], lp["w0"], lp["w_up"], lp["a0"], lp["a_up"],
      lp["g_up"], bd)


def _rwkv_scan_kernel(rf, vf, kkf, lwf, kdf, bf_, rb, vb, kkb, lwb, kdb, bb, s0_ref, of_ref, ob_ref, sfin_ref, s_s,
                      *, tb):
    step = pl.program_id(0)

    @pl.when(step == 0)
    def _():
        s_s[...] = s0_ref[...]

    c = RWKV_CHUNK
    n = A_HEAD_DIM
    assert c == n
    pw = 2 * n
    row = lax.broadcasted_iota(jnp.int32, (c, pw), 0)
    lane = lax.broadcasted_iota(jnp.int32, (c, pw), 1)
    col = lane % c
    first = lane < n
    eye = (row == col).astype(F32)
    r_c = lax.broadcasted_iota(jnp.int32, (c, c), 0)
    c_c = lax.broadcasted_iota(jnp.int32, (c, c), 1)
    n_ch = tb // c
    dirs = ((rf, vf, kkf, lwf, kdf, bf_, of_ref), (rb, vb, kkb, lwb, kdb, bb, ob_ref))
    pairs = range(A_HEADS // 2)
    psl = [slice(q * pw, (q + 1) * pw) for q in pairs]

    def bd(x):
        z = jnp.zeros_like(x)
        return jnp.concatenate([jnp.where(first, x, z), jnp.where(first, z, x)], axis=0)

    def chunk_terms(d, refs, ci):
        r_ref, v_ref, kk_ref, lw_ref, kd_ref, b_ref, _ = refs
        if d == 0:
            incl, strict, tri = row >= col, row > col, r_c >= c_c
        else:
            incl, strict, tri = row <= col, row < col, r_c <= c_c
        rows = slice(ci * c, (ci + 1) * c)
        lw = lw_ref[rows, :]
        r = r_ref[rows, :]
        kk = kk_ref[rows, :]
        kd = kd_ref[rows, :]
        b = b_ref[rows, :]
        cl = _dot_split_rhs(tri.astype(BF16), lw)
        cl_tot = cl[c - 1:c, :] if d == 0 else cl[0:1, :]
        e_neg = jnp.exp(-cl)
        e_end = jnp.exp(cl_tot - cl)
        g_tot = jnp.exp(cl_tot)
        rt = r * jnp.exp(cl)
        x_all = _bf(jnp.concatenate([-(kk * jnp.exp(cl - lw)), rt], axis=0))
        bt_all = _bf(b * e_neg)
        kt_all = _bf(kd * e_neg)
        return {
            "rows": rows, "strict": strict, "incl": incl,
            "x": [x_all[:, ps] for ps in psl],
            "z": [jnp.concatenate([bd(bt_all[:, ps]), bd(kt_all[:, ps])], axis=0) for ps in psl],
            "bd_bh": [bd(_bf(b * e_end)[:, ps]) for ps in psl],
            "bd_kh": [bd(_bf(kd * e_end)[:, ps]) for ps in psl],
            "bd_v": [bd(_bf(v_ref[rows, :])[:, ps]) for ps in psl],
            "rt": [rt[:, ps] for ps in psl],
            "g_tot": [g_tot[:, ps] for ps in psl],
        }

    chunks = [(d, cc) for cc in range(n_ch) for d in range(2)]
    prep = {(d, cc): chunk_terms(d, dirs[d], cc if d == 0 else n_ch - 1 - cc) for d, cc in chunks}
    units = [(k, q) for k in chunks for q in pairs]

    def per_unit(name, indexed=True):
        return [prep[k][name][q] if indexed else prep[k][name] for k, q in units]

    strict, incl = per_unit("strict", False), per_unit("incl", False)
    x_u, bd_bh, bd_kh, bd_v, rt_u = (per_unit(s) for s in ("x", "bd_bh", "bd_kh", "bd_v", "rt"))
    g = [_dot_nt(x, z) for x, z in zip(x_u, per_unit("z"))]
    a_ab = [jnp.where(s, gi[:c, :pw], 0.0) for s, gi in zip(strict, g)]
    a_rb = [_bf(jnp.where(i, gi[c:, :pw], 0.0)) for i, gi in zip(incl, g)]
    a_k = [_bf(jnp.concatenate([jnp.where(s, gi[:c, pw:], 0.0), jnp.where(i, gi[c:, pw:], 0.0)], axis=0))
           for s, i, gi in zip(strict, incl, g)]
    a2 = [_bdot(_bf(a), bd(_bf(a))) for a in a_ab]
    pa = [jnp.concatenate([eye + a, sq], axis=0) for a, sq in zip(a_ab, a2)]
    for _ in range(4):
        nxt = [_bdot(_bf(x), bd(_bf(x[c:]))) for x in pa]
        pa = [jnp.concatenate([x[:c] + y[:c], y[c:]], axis=0) for x, y in zip(pa, nxt)]
    p = [_bf(x[:c] + _bdot(_bf(x[:c]), bd(_bf(x[c:])))) for x in pa]
    vk = [_bdot(a, v) for a, v in zip(a_k, bd_v)]
    tw = [_bf(_bdot(pi, jnp.concatenate([bd(x[:c]), bd(_bf(vki[:c]))], axis=1)))
          for pi, vki, x in zip(p, vk, x_u)]
    bd_at = [bd(t[:, :pw]) for t in tw]
    bd_w = [bd(t[:, pw:]) for t in tw]
    mt = [_bf(_dot_tn(a, bh)) for a, bh in zip(bd_at, bd_bh)]
    ntf = [_dot_tn(jnp.concatenate([w, v], axis=0), jnp.concatenate([bh, kh], axis=0))
           for w, v, bh, kh in zip(bd_w, bd_v, bd_bh, bd_kh)]
    nt = [m[:n] + m[n:] for m in ntf]
    rw = [_bdot(a, jnp.concatenate([at, w], axis=1)) for a, at, w in zip(a_rb, bd_at, bd_w)]
    ry = [_bf(r + w[:, :pw]) for w, r in zip(rw, rt_u)]
    y0 = [w[:, pw:] + vki[c:] for w, vki in zip(rw, vk)]
    term = {u: vals for u, vals in zip(units, zip(mt, nt, ry, y0, per_unit("g_tot")))}

    state = [[s_s[d, q] for q in pairs] for d in range(2)]
    dq = [(d, q) for d in range(2) for q in pairs]
    for cc in range(n_ch):
        mt_c, nt_c, ry_c, y0_c, gt_c = zip(*[term[((d, cc), q)] for d, q in dq])
        s0 = [state[d][q] for d, q in dq]
        ys = [_dot_nt(ry_c[i], bd(_bf(s0[i]))) + y0_c[i] for i in range(len(dq))]
        upd = [_bdot(jnp.concatenate(_split(s0[i], 2), axis=0), mt_c[i]) for i in range(len(dq))]
        for i, (d, q) in enumerate(dq):
            state[d][q] = s0[i] * gt_c[i] + (upd[i][:n] + upd[i][n:]) + nt_c[i]
        for d in range(2):
            y_d = [ys[i] for i, (dd, _) in enumerate(dq) if dd == d]
            dirs[d][6][prep[(d, cc)]["rows"], :] = jnp.concatenate(y_d, axis=1)
    for d in range(2):
        for q in pairs:
            s_s[d, q] = state[d][q]

    @pl.when(step == pl.num_programs(0) - 1)
    def _():
        sfin_ref[...] = s_s[...]


def _rwkv_scan(proj, prep, s0):
    kk, lw0, lw1, kd0, kd1, b0, b1 = prep[:7]
    t_len = proj.shape[0]
    tb = 2 * RWKV_CHUNK
    nb = t_len // tb
    aw = A_WIDTH
    fwd = lambda c: pl.BlockSpec((tb, aw), lambda i: (i, c))
    bwd = lambda c: pl.BlockSpec((tb, aw), lambda i: (nb - 1 - i, c))
    st = pl.BlockSpec(RWKV_STATE_SHAPE, lambda i: (0, 0, 0, 0))
    kern = functools.partial(_rwkv_scan_kernel, tb=tb)
    return pl.pallas_call(
        kern,
        out_shape=[jax.ShapeDtypeStruct((t_len, aw), F32), jax.ShapeDtypeStruct((t_len, aw), F32),
                   jax.ShapeDtypeStruct(RWKV_STATE_SHAPE, F32)],
        grid=(nb,),
        in_specs=[fwd(0), fwd(2), fwd(0), fwd(0), fwd(0), fwd(0),
                  bwd(0), bwd(2), bwd(0), bwd(0), bwd(0), bwd(0), st],
        out_specs=[fwd(0), bwd(0), st],
        scratch_shapes=[pltpu.VMEM(RWKV_STATE_SHAPE, F32)],
        compiler_params=_params(("arbitrary",)),
        name="rwkv_scan",
    )(proj, proj, kk, lw0, kd0, b0, proj, proj, kk, lw1, kd1, b1, s0)


def _retention_kernel(qf, kf, vf, cosf, sinf, qb, kb, vb, cosb, sinb, lgt_ref, r0_ref, yf_ref, yb_ref, rfin_ref, r_s):
    step = pl.program_id(0)

    @pl.when(step == 0)
    def _():
        r_s[...] = r0_ref[...]

    c = RET_CHUNK
    dk = B_QK_DIM
    dv = B_V_DIM
    lg_all = -_softplus(-lgt_ref[...])
    rowf = lax.broadcasted_iota(jnp.int32, (c, c), 0).astype(F32)
    colf = lax.broadcasted_iota(jnp.int32, (c, c), 1).astype(F32)
    lane = lax.broadcasted_iota(jnp.int32, (c, dk), 1)
    first_half = (lane % 64) < 32
    dirs = ((qf, kf, vf, cosf, sinf, yf_ref), (qb, kb, vb, cosb, sinb, yb_ref))
    for d, (q_ref, k_ref, v_ref, cos_ref, sin_ref, y_ref) in enumerate(dirs):
        cos = cos_ref[...]
        sin = sin_ref[...]
        diff = (rowf - colf) if d == 0 else (colf - rowf)
        pos = rowf if d == 0 else (c - 1.0) - rowf
        for h in range(B_HEADS):
            lg = lg_all[d * B_HEADS + h:d * B_HEADS + h + 1, :]
            qh = q_ref[:, h * dk:(h + 1) * dk].astype(F32)
            kh = k_ref[:, h * dk:(h + 1) * dk].astype(F32) * (dk ** -0.5)

            def rope(xv):
                swapped = jnp.where(first_half, pltpu.roll(xv, dk - 32, 1), pltpu.roll(xv, 32, 1))
                return xv * cos + swapped * sin

            qh = rope(qh)
            kh = rope(kh)
            vh = v_ref[:, h * dv:(h + 1) * dv]
            dmask = jnp.where(diff >= 0.0, jnp.exp(lg * jnp.maximum(diff, 0.0)), 0.0)
            scores = _dot_nt(_bf(qh), _bf(kh)) * dmask
            inner = _bdot(_bf(scores), vh)
            xi = jnp.exp(lg * (pos + 1.0))
            zeta = jnp.exp(lg * ((c - 1.0) - pos))
            r_prev = r_s[d, h]
            cross = _bdot(_bf(qh * xi), _bf(r_prev))
            y_ref[:, h * dv:(h + 1) * dv] = inner + cross
            kv = _dot_tn(_bf(kh * zeta), vh)
            g_chunk = jnp.exp(lg * float(c))
            r_s[d, h] = jnp.concatenate([g_chunk, g_chunk], axis=1) * r_prev + kv

    @pl.when(step == pl.num_programs(0) - 1)
    def _():
        rfin_ref[...] = r_s[...]


def _retention(proj, cos, sin, lgt, r0):
    t_len = proj.shape[0]
    c = RET_CHUNK
    nc = t_len // c
    qw = B_QK_WIDTH
    vw = B_V_WIDTH
    fq = lambda col: pl.BlockSpec((c, qw), lambda i: (i, col))
    bq = lambda col: pl.BlockSpec((c, qw), lambda i: (nc - 1 - i, col))
    st = pl.BlockSpec((2, B_HEADS, B_QK_DIM, B_V_DIM), lambda i: (0, 0, 0, 0))
    q_col = 0
    v_col = 2 * qw // vw
    return pl.pallas_call(
        _retention_kernel,
        out_shape=[jax.ShapeDtypeStruct((t_len, vw), F32), jax.ShapeDtypeStruct((t_len, vw), F32),
                   jax.ShapeDtypeStruct((2, B_HEADS, B_QK_DIM, B_V_DIM), F32)],
        grid=(nc,),
        in_specs=[
            fq(q_col), fq(q_col + 1), pl.BlockSpec((c, vw), lambda i: (i, v_col)),
            pl.BlockSpec((c, B_QK_DIM), lambda i: (i, 0)), pl.BlockSpec((c, B_QK_DIM), lambda i: (i, 0)),
            bq(q_col), bq(q_col + 1), pl.BlockSpec((c, vw), lambda i: (nc - 1 - i, v_col)),
            pl.BlockSpec((c, B_QK_DIM), lambda i: (nc - 1 - i, 0)),
            pl.BlockSpec((c, B_QK_DIM), lambda i: (nc - 1 - i, 0)),
            pl.BlockSpec((2 * B_HEADS, 128), lambda i: (0, 0)), st,
        ],
        out_specs=[pl.BlockSpec((c, vw), lambda i: (i, 0)), pl.BlockSpec((c, vw), lambda i: (nc - 1 - i, 0)), st],
        scratch_shapes=[pltpu.VMEM((2, B_HEADS, B_QK_DIM, B_V_DIM), F32)],
        compiler_params=_params(("arbitrary",)),
        name="retention",
    )(proj, proj, proj, cos, sin, proj, proj, proj, cos, sin, lgt, r0)


def _mix_out_kernel(of_ref, ob_ref, bonus_ref, g_ref, yf_ref, yb_ref, gb_ref, ga_ref, gbb_ref, x_ref,
                    alnw, alnb, rlnw, rlnb, npm, npf, g1_ref, sc2_ref, sh2_ref, wa_ref, wb_ref, wo_ref, bd_ref,
                    x1_ref, h2_ref, h3_ref):
    bd = bd_ref[...]
    o = of_ref[...] + ob_ref[...]
    mu = _dot_split_lhs(o, bd) * (1.0 / A_HEAD_DIM)
    oc = o - mu
    var = _dot_split_lhs(oc * oc, bd) * (1.0 / A_HEAD_DIM)
    ya = oc * lax.rsqrt(var + RWKV_GN_EPS) * alnw[...] + alnb[...] + bonus_ref[...]
    ya = _bdot(_bf(ya * g_ref[...]), wa_ref[...])

    y = yf_ref[...] + yb_ref[...]
    parts = []
    for h in range(B_HEADS):
        seg = y[:, h * B_V_DIM:(h + 1) * B_V_DIM]
        m = jnp.mean(seg, axis=-1, keepdims=True)
        sc = seg - m
        vr = jnp.mean(sc * sc, axis=-1, keepdims=True)
        parts.append(sc * lax.rsqrt(vr + RET_GN_EPS))
    yn = jnp.concatenate(parts, axis=1) * rlnw[...] + rlnb[...]
    gb = gb_ref[...].astype(F32)
    yb = _bdot(_bf(yn * (gb * _sigmoid(gb))), wb_ref[...])

    merged = _sigmoid(ga_ref[...].astype(F32)) * ya + _sigmoid(gbb_ref[...].astype(F32)) * yb
    mix = _bdot(_bf(merged), wo_ref[...])
    x1 = x_ref[...] + g1_ref[...] * (_rms(mix) * npm[...])
    x1_ref[...] = x1
    h2 = (_rms(x1) * npf[...]) * (1.0 + sc2_ref[...]) + sh2_ref[...]
    h2_ref[...] = h2
    _rows_to_packed_tiles(h2, h3_ref)


def _mix_out(x2d, proj, o_f, o_b, bonus, g, y_f, y_b, lp, vecs, bd):
    t_len = x2d.shape[0]
    tm = min(t_len, 256)
    aw = A_WIDTH
    d = D_MODEL
    ta = lambda: pl.BlockSpec((tm, aw), lambda i: (i, 0))
    td = lambda: pl.BlockSpec((tm, d), lambda i: (i, 0))
    pc = lambda c: pl.BlockSpec((tm, d), lambda i: (i, c))
    va = lambda: pl.BlockSpec((1, aw), lambda i: (0, 0))
    vd = lambda: pl.BlockSpec((1, d), lambda i: (0, 0))
    full = lambda shp: pl.BlockSpec(shp, lambda i: (0, 0))
    gcol = (2 * B_QK_WIDTH + B_V_WIDTH) // d
    return pl.pallas_call(
        _mix_out_kernel,
        out_shape=[jax.ShapeDtypeStruct((t_len, d), F32), jax.ShapeDtypeStruct((t_len, d), F32),
                   jax.ShapeDtypeStruct((t_len * PACK_TILE, LANES), jnp.uint32)],
        grid=(t_len // tm,),
        in_specs=[ta(), ta(), ta(), ta(), td(), td(), pc(gcol), pc(gcol + 1), pc(gcol + 2), td(),
                  va(), va(), vd(), vd(), vd(), vd(), vd(), vd(), vd(),
                  full((aw, d)), full((d, d)), full((d, d)), full((aw, aw))],
        out_specs=[td(), td(), pl.BlockSpec((tm * PACK_TILE, LANES), lambda i: (i, 0))],
        compiler_params=_params(("arbitrary",)),
        name="mix_out",
    )(o_f, o_b, bonus, g, y_f, y_b, proj, proj, proj, x2d,
      lp["ln_w"], lp["ln_b"], lp["ret_ln_w"], lp["ret_ln_b"], vecs["npm"], vecs["npf"], vecs["g1"], vecs["sc2"],
      vecs["sh2"], lp["w_a"], lp["w_b"], lp["w_o"], bd)


def _router_kernel(h_ref, rw_ref, bias_ref, sel_ref, wts_ref, rank_ref, cnt_ref, cnt_s, *, tm):
    step = pl.program_id(0)

    @pl.when(step == 0)
    def _():
        cnt_s[...] = jnp.zeros_like(cnt_s)

    ne = N_EXPERTS
    scores = _sigmoid(_dot3(h_ref[...], rw_ref[...]))
    work = scores + bias_ref[...]
    lane = lax.broadcasted_iota(jnp.int32, (tm, ne), 1).astype(F32)
    idxs = []
    vals = []
    for _ in range(TOP_K):
        m = jnp.max(work, axis=-1, keepdims=True)
        idx = jnp.min(jnp.where(work == m, lane, float(ne)), axis=-1, keepdims=True)
        oh = lane == idx
        vals.append(jnp.sum(jnp.where(oh, scores, 0.0), axis=-1, keepdims=True))
        idxs.append(idx)
        work = jnp.where(oh, -jnp.inf, work)
    sel_f = jnp.concatenate(idxs, axis=1)
    s_sel = jnp.concatenate(vals, axis=1)
    wts_ref[...] = s_sel / jnp.sum(s_sel, axis=1, keepdims=True) * ROUTED_SCALE
    sel_ref[...] = sel_f.astype(jnp.int32)

    hit = work == -jnp.inf
    r_i = lax.broadcasted_iota(jnp.int32, (tm, tm), 0)
    c_i = lax.broadcasted_iota(jnp.int32, (tm, tm), 1)
    before = _bdot((r_i > c_i).astype(BF16), hit.astype(BF16)) + cnt_s[...]
    ranks = [jnp.sum(jnp.where(lane == idxs[k], before, 0.0), axis=-1, keepdims=True) for k in range(TOP_K)]
    rank_ref[...] = jnp.concatenate(ranks, axis=1).astype(jnp.int32)
    cnt = cnt_s[...] + jnp.sum(hit.astype(F32), axis=0, keepdims=True)
    cnt_s[...] = cnt
    cnt_ref[...] = cnt.astype(jnp.int32)


def _router(h2, router_w, router_bias):
    t_len = h2.shape[0]
    tm = min(t_len, 256)
    kern = functools.partial(_router_kernel, tm=tm)
    tk = lambda: pl.BlockSpec((tm, TOP_K), lambda i: (i, 0))
    return pl.pallas_call(
        kern,
        out_shape=[jax.ShapeDtypeStruct((t_len, TOP_K), jnp.int32), jax.ShapeDtypeStruct((t_len, TOP_K), F32),
                   jax.ShapeDtypeStruct((t_len, TOP_K), jnp.int32), jax.ShapeDtypeStruct((1, N_EXPERTS), jnp.int32)],
        grid=(t_len // tm,),
        in_specs=[pl.BlockSpec((tm, D_MODEL), lambda i: (i, 0)),
                  pl.BlockSpec((D_MODEL, N_EXPERTS), lambda i: (0, 0)),
                  pl.BlockSpec((1, N_EXPERTS), lambda i: (0, 0))],
        out_specs=[tk(), tk(), tk(), pl.BlockSpec((1, N_EXPERTS), lambda i: (0, 0))],
        scratch_shapes=[pltpu.VMEM((1, N_EXPERTS), F32)],
        compiler_params=_params(("arbitrary",)),
        name="router",
    )(h2, router_w, router_bias.reshape(1, N_EXPERTS))


def _slot(sel_ref, rank_ref, start_ref, a):
    return start_ref[sel_ref[a]] + rank_ref[a]


def _dispatch_kernel(sel_ref, rank_ref, start_ref, cnt_ref, h_ref, xs_out, zrow, sem, zsem, *, tm):
    def row_copy(t, k):
        slot = _slot(sel_ref, rank_ref, start_ref, t * TOP_K + k)
        return pltpu.make_async_copy(_row_tile(h_ref, t, tile=PACK_TILE), _row_tile(xs_out, slot, tile=PACK_TILE), sem)

    def issue(t, carry):
        for k in range(TOP_K):
            row_copy(t, k).start(priority=k % 2)
        return carry

    lax.fori_loop(0, tm, issue, 0)

    def drain(t, carry):
        for k in range(TOP_K):
            row_copy(t, k).wait()
        return carry

    lax.fori_loop(0, tm, drain, 0)

    @pl.when(pl.program_id(0) == pl.num_programs(0) - 1)
    def _():
        zrow[...] = jnp.zeros_like(zrow)

        def for_pad_runs(e0, fn):
            def per_expert(e, carry):
                n = cnt_ref[e]
                n_pad = (n + MOE_BLOCK - 1) // MOE_BLOCK * MOE_BLOCK
                pad = n_pad - n
                off = start_ref[e] + n
                for bit in reversed(range(MOE_BLOCK.bit_length() - 1)):
                    size = 1 << bit
                    is_set = ((pad >> bit) & 1) == 1

                    @pl.when(is_set)
                    def _(off=off, size=size):
                        dst = xs_out.at[pl.ds(pl.multiple_of(off * PACK_TILE, PACK_TILE), size * PACK_TILE)]
                        fn(pltpu.make_async_copy(zrow.at[pl.ds(0, size * PACK_TILE)], dst, zsem))

                    off = off + jnp.where(is_set, size, 0)
                return carry

            lax.fori_loop(e0, e0 + ZERO_PAD_GROUP, per_expert, 0)

        def per_group(gi, carry):
            e0 = gi * ZERO_PAD_GROUP
            for_pad_runs(e0, lambda cp: cp.start())
            for_pad_runs(e0, lambda cp: cp.wait())
            return carry

        lax.fori_loop(0, N_EXPERTS // ZERO_PAD_GROUP, per_group, 0)


def _dispatch(h3, sel_flat, rank_flat, pad_start, counts, n_slots):
    t_len = h3.shape[0] // PACK_TILE
    tm = min(t_len, 256)
    kern = functools.partial(_dispatch_kernel, tm=tm)
    smem_blk = lambda: pl.BlockSpec((tm * TOP_K,), lambda i: (i,), memory_space=pltpu.SMEM)
    smem_all = lambda: pl.BlockSpec((N_EXPERTS,), lambda i: (0,), memory_space=pltpu.SMEM)
    return pl.pallas_call(
        kern,
        out_shape=jax.ShapeDtypeStruct((n_slots * PACK_TILE, LANES), jnp.uint32),
        grid=(t_len // tm,),
        in_specs=[smem_blk(), smem_blk(), smem_all(), smem_all(),
                  pl.BlockSpec((tm * PACK_TILE, LANES), lambda i: (i, 0))],
        out_specs=pl.BlockSpec(memory_space=pl.ANY),
        scratch_shapes=[pltpu.VMEM((MOE_BLOCK // 2 * PACK_TILE, LANES), jnp.uint32), pltpu.SemaphoreType.DMA(()),
                        pltpu.SemaphoreType.DMA(())],
        compiler_params=_params(("arbitrary",)),
        name="dispatch",
    )(sel_flat, rank_flat, pad_start, counts, h3)


def _expert_kernel(be_ref, nu_ref, xs_ref, wg_ref, wu_ref, wd_ref, ys_ref, wg_s, wu_s, wd_s):
    b = pl.program_id(0)
    used = b < nu_ref[0]
    new_expert = (b == 0) | (be_ref[b] != be_ref[jnp.maximum(b - 1, 0)])

    @pl.when(used & new_expert)
    def _():
        wg_s[...] = _bf(wg_ref[0])
        wu_s[...] = _bf(wu_ref[0])
        wd_s[...] = _bf(wd_ref[0])

    @pl.when(used)
    def _():
        xb = _packed_tiles_to_rows(xs_ref, MOE_BLOCK)
        gate = _bdot(xb, wg_s[...])
        up = _bdot(xb, wu_s[...])
        act = gate * _sigmoid(gate) * up
        _rows_to_tiles(_bdot(_bf(act), wd_s[...]), ys_ref)

    @pl.when(jnp.logical_not(used))
    def _():
        ys_ref[...] = jnp.zeros_like(ys_ref)


def _experts(xs, block_e, n_used, w_gate, w_up, w_down):
    n_slots = xs.shape[0] // PACK_TILE
    n_blocks = n_slots // MOE_BLOCK
    rows = (MOE_BLOCK * ROW_TILE, LANES)
    grid_spec = pltpu.PrefetchScalarGridSpec(
        num_scalar_prefetch=2,
        grid=(n_blocks,),
        in_specs=[
            pl.BlockSpec((MOE_BLOCK * PACK_TILE, LANES), lambda b, be, nu: (jnp.where(b < nu[0], b, 0), 0)),
            pl.BlockSpec((1, D_MODEL, EXPERT_FF), lambda b, be, nu: (be[b], 0, 0)),
            pl.BlockSpec((1, D_MODEL, EXPERT_FF), lambda b, be, nu: (be[b], 0, 0)),
            pl.BlockSpec((1, EXPERT_FF, D_MODEL), lambda b, be, nu: (be[b], 0, 0)),
        ],
        out_specs=pl.BlockSpec(rows, lambda b, be, nu: (b, 0)),
        scratch_shapes=[pltpu.VMEM((D_MODEL, EXPERT_FF), BF16), pltpu.VMEM((D_MODEL, EXPERT_FF), BF16),
                        pltpu.VMEM((EXPERT_FF, D_MODEL), BF16)],
    )
    return pl.pallas_call(
        _expert_kernel,
        out_shape=jax.ShapeDtypeStruct((n_slots * ROW_TILE, LANES), F32),
        grid_spec=grid_spec,
        compiler_params=_params(("arbitrary",)),
        name="experts",
    )(block_e, n_used, xs, w_gate, w_up, w_down)


def _combine_kernel(sel_ref, rank_ref, start_ref, ys_ref, wts_ref, h_ref, x1_ref, sg_ref, su_ref, sd_ref, npo, g2_ref,
                    o_ref, buf, sem, *, tm):
    def row_copy(t, k):
        slot = _slot(sel_ref, rank_ref, start_ref, t * TOP_K + k)
        return pltpu.make_async_copy(_row_tile(ys_ref, slot), _row_tile(buf, t, (k,)), sem)

    def issue(t, carry):
        for k in range(TOP_K):
            row_copy(t, k).start(priority=k % 2)
        return carry

    lax.fori_loop(0, tm, issue, 0)

    hb = _bf(h_ref[...])
    gate = _bdot(hb, sg_ref[...])
    up = _bdot(hb, su_ref[...])
    shared = _bdot(_bf(gate * _sigmoid(gate) * up), sd_ref[...])

    def drain(t, carry):
        for k in range(TOP_K):
            row_copy(t, k).wait()
        return carry

    lax.fori_loop(0, tm, drain, 0)

    wts = wts_ref[...]
    routed = _tiles_to_rows(buf, tm, (0,)) * wts[:, 0:1]
    for k in range(1, TOP_K):
        routed = routed + _tiles_to_rows(buf, tm, (k,)) * wts[:, k:k + 1]
    o_ref[...] = x1_ref[...] + g2_ref[...] * (_rms(routed + shared) * npo[...])


def _combine(ys, sel_flat, rank_flat, pad_start, wts, h2, x1, sg, su, sd, npo, g2):
    t_len = h2.shape[0]
    tm = min(t_len, 256)
    d = D_MODEL
    kern = functools.partial(_combine_kernel, tm=tm)
    smem_blk = lambda: pl.BlockSpec((tm * TOP_K,), lambda i: (i,), memory_space=pltpu.SMEM)
    td = lambda: pl.BlockSpec((tm, d), lambda i: (i, 0))
    vd = lambda: pl.BlockSpec((1, d), lambda i: (0, 0))
    return pl.pallas_call(
        kern,
        out_shape=jax.ShapeDtypeStruct((t_len, d), F32),
        grid=(t_len // tm,),
        in_specs=[smem_blk(), smem_blk(),
                  pl.BlockSpec((N_EXPERTS,), lambda i: (0,), memory_space=pltpu.SMEM),
                  pl.BlockSpec(memory_space=pl.ANY),
                  pl.BlockSpec((tm, TOP_K), lambda i: (i, 0)),
                  td(), td(),
                  pl.BlockSpec((d, SHARED_FF), lambda i: (0, 0)), pl.BlockSpec((d, SHARED_FF), lambda i: (0, 0)),
                  pl.BlockSpec((SHARED_FF, d), lambda i: (0, 0)), vd(), vd()],
        out_specs=td(),
        scratch_shapes=[pltpu.VMEM((TOP_K, tm * ROW_TILE, LANES), F32), pltpu.SemaphoreType.DMA(())],
        compiler_params=_params(("arbitrary",)),
        name="combine",
    )(sel_flat, rank_flat, pad_start, ys, wts, h2, x1, sg, su, sd, npo, g2)


def _block_diag_ones(width, group):
    idx = np.arange(width) // group
    return jnp.asarray(idx[:, None] == idx[None, :], dtype=BF16)


def _rope_tables(t_len):
    pos = np.arange(t_len)
    rows = (pos // GRID_W).astype(np.float32)
    cols = (pos % GRID_W).astype(np.float32)
    quarter = B_QK_DIM // 4
    inv_freq = jnp.asarray(ROPE_BASE, F32) ** (-jnp.arange(quarter, dtype=F32) / quarter)
    ang_r = jnp.asarray(rows)[:, None] * inv_freq
    ang_c = jnp.asarray(cols)[:, None] * inv_freq
    cr, sr, cc, sc = jnp.cos(ang_r), jnp.sin(ang_r), jnp.cos(ang_c), jnp.sin(ang_c)
    return jnp.concatenate([cr, cr, cc, cc], axis=1), jnp.concatenate([-sr, sr, -sc, sc], axis=1)


def _token_mixer(x2d, g_pre, sc, sh, lp, cos, sin, states, bd):
    proj_a, proj_b = _in_proj(x2d, g_pre, sc, sh, lp["w_in"], lp["mu"])
    prep = _rwkv_prep(proj_a, lp, bd)
    o_f, o_b, s_fin = _rwkv_scan(proj_a, prep, states[0])
    y_f, y_b, r_fin = _retention(proj_b, cos, sin, lp["lgt"], states[1])
    return proj_b, prep, (o_f, o_b), (y_f, y_b), (s_fin, r_fin)


def kernel(x, c, ctx, c_ctx, w_mod, b_mod, norm_pre_mix, norm_post_mix, norm_pre_ffn, norm_post_ffn, w_in, shift_mu,
           rwkv_w0, rwkv_w_up, rwkv_a0, rwkv_a_up, rwkv_g_up, rwkv_k_k, rwkv_k_a, rwkv_r_k, rwkv_ln_w, rwkv_ln_b,
           w_branch_a, ret_decay_logit, ret_ln_w, ret_ln_b, w_branch_b, w_out, router_w, router_bias, exp_w_gate,
           exp_w_up, exp_w_down, sh_w_gate, sh_w_up, sh_w_down):
    d = D_MODEL
    assert x.shape[0] == 1 and w_in.shape[0] == 1, "single batch element, single layer"
    t_len = x.shape[1]
    x2d = x.reshape(t_len, d)
    ctx2d = ctx.reshape(ctx.shape[1], d)
    row = lambda a: a.reshape(1, -1)

    cs = jnp.zeros((8, d), F32).at[0].set(c[0]).at[1].set(c_ctx)
    mod = _modulation(cs, w_mod[0], b_mod[0])
    sh1, sc1, g1, sh2, sc2, g2 = [mod[0:1, i * d:(i + 1) * d] for i in range(6)]
    csh1, csc1 = mod[1:2, 0:d], mod[1:2, d:2 * d]

    w_in_p = jnp.concatenate(
        [w_in[0][:, :SHIFT_COLS], jnp.zeros((d, SHIFT_PAD - SHIFT_COLS), F32), w_in[0][:, SHIFT_COLS:]], axis=1)
    lp = {
        "w_in": w_in_p.astype(BF16),
        "mu": jnp.pad(shift_mu[0], ((0, 0), (0, SHIFT_PAD - SHIFT_COLS))),
        "k_k": row(rwkv_k_k[0]), "k_a": row(rwkv_k_a[0]), "r_k": row(rwkv_r_k[0]),
        "w0": rwkv_w0[0], "w_up": rwkv_w_up[0], "a0": rwkv_a0[0], "a_up": rwkv_a_up[0], "g_up": rwkv_g_up[0],
        "ln_w": row(rwkv_ln_w[0]), "ln_b": row(rwkv_ln_b[0]),
        "ret_ln_w": row(ret_ln_w[0]), "ret_ln_b": row(ret_ln_b[0]),
        "lgt": jnp.broadcast_to(ret_decay_logit[0].reshape(2 * B_HEADS, 1), (2 * B_HEADS, 128)),
        "w_a": w_branch_a[0].astype(BF16), "w_b": w_branch_b[0].astype(BF16), "w_o": w_out[0].astype(BF16),
    }
    bd = _block_diag_ones(A_WIDTH, A_HEAD_DIM)
    g_pre = row(norm_pre_mix[0])

    t_ctx = ctx2d.shape[0]
    zero_states = (jnp.zeros(RWKV_STATE_SHAPE, F32),
                   jnp.zeros((2, B_HEADS, B_QK_DIM, B_V_DIM), F32))
    ones = jnp.ones((t_ctx, B_QK_DIM), F32)
    *_, ctx_states = _token_mixer(ctx2d, g_pre, csc1, csh1, lp, ones, jnp.zeros_like(ones), zero_states, bd)

    cos, sin = _rope_tables(t_len)
    proj, prep, (o_f, o_b), (y_f, y_b), _ = _token_mixer(x2d, g_pre, sc1, sh1, lp, cos, sin, ctx_states, bd)
    vecs = {"npm": row(norm_post_mix[0]), "npf": row(norm_pre_ffn[0]), "g1": g1, "sc2": sc2, "sh2": sh2}
    x1, h2, h3 = _mix_out(x2d, proj, o_f, o_b, prep[7], prep[8], y_f, y_b, lp, vecs, bd)

    sel, wts, rank, counts = _router(h2, router_w[0], router_bias[0])
    counts = counts.reshape(N_EXPERTS)
    padded = (counts + MOE_BLOCK - 1) // MOE_BLOCK * MOE_BLOCK
    pad_end = jnp.cumsum(padded)
    pad_start = (pad_end - padded).astype(jnp.int32)
    n_assign = t_len * TOP_K
    n_blocks = (n_assign + N_EXPERTS * (MOE_BLOCK - 1) + MOE_BLOCK - 1) // MOE_BLOCK
    block_start = jnp.arange(n_blocks, dtype=jnp.int32) * MOE_BLOCK
    block_e = jnp.minimum(jnp.sum(pad_end[None, :] <= block_start[:, None], axis=1), N_EXPERTS - 1).astype(jnp.int32)
    n_used = (pad_end[-1:] // MOE_BLOCK).astype(jnp.int32)
    sel_flat = sel.reshape(n_assign)
    rank_flat = rank.reshape(n_assign)

    xs = _dispatch(h3, sel_flat, rank_flat, pad_start, counts, n_blocks * MOE_BLOCK)
    ys = _experts(xs, block_e, n_used, exp_w_gate[0], exp_w_up[0], exp_w_down[0])
    out = _combine(ys, sel_flat, rank_flat, pad_start, wts, h2, x1, sh_w_gate[0].astype(BF16),
                   sh_w_up[0].astype(BF16), sh_w_down[0].astype(BF16), row(norm_post_ffn[0]), g2)
    return out.reshape(x.shape)
```

```python
import functools

import jax
import jax.numpy as jnp
import numpy as np
from jax import lax
from jax.experimental import pallas as pl
from jax.experimental.pallas import tpu as pltpu

F32 = jnp.float32
BF16 = jnp.bfloat16

D_MODEL = 1024
GRID_W = 64
NORM_EPS = 1e-6

A_HEAD_DIM = 64
A_WIDTH = D_MODEL // 2
A_HEADS = A_WIDTH // A_HEAD_DIM
W_LORA = 64
ICLR_LORA = 64
G_LORA = 128
RWKV_GN_EPS = 64e-5

B_HEADS = 4
B_QK_WIDTH = D_MODEL // 2
B_V_WIDTH = D_MODEL
B_QK_DIM = B_QK_WIDTH // B_HEADS
B_V_DIM = B_V_WIDTH // B_HEADS
RET_CHUNK = 128
RET_GN_EPS = 1e-5
ROPE_BASE = 10000.0

SHIFT_COLS = 3 * A_WIDTH + W_LORA + ICLR_LORA + G_LORA
LORA_COLS = W_LORA + ICLR_LORA + G_LORA
SHIFT_PAD = 2048
PROJ_COLS = SHIFT_PAD + 2 * B_QK_WIDTH + 2 * B_V_WIDTH + 2 * D_MODEL

N_EXPERTS = 256
TOP_K = 8
EXPERT_FF = D_MODEL // 4
SHARED_FF = D_MODEL // 4
ROUTED_SCALE = 2.5
MOE_BLOCK = 512
ZERO_PAD_GROUP = 16

RWKV_CHUNK = 64
RWKV_STATE_SHAPE = (2, A_HEADS // 2, A_HEAD_DIM, 2 * A_HEAD_DIM)
PROJ_TN = 1024
VMEM_LIMIT = 48 * 1024 * 1024


def _params(sem):
    return pltpu.CompilerParams(dimension_semantics=sem, vmem_limit_bytes=VMEM_LIMIT)


def _bf(a):
    return a.astype(BF16)


def _bdot(a, b):
    return jnp.dot(a, b, preferred_element_type=F32)


def _dot_nt(a, b):
    return lax.dot_general(a, b, (((1,), (1,)), ((), ())), preferred_element_type=F32)


def _dot_tn(a, b):
    return lax.dot_general(a, b, (((0,), (0,)), ((), ())), preferred_element_type=F32)


def _split(a, n):
    out = []
    rem = a
    for _ in range(n):
        p = _bf(rem)
        out.append(p)
        rem = rem - p.astype(F32)
    return out


def _dot_split_lhs(a, b_bf, n=3):
    acc = None
    for p in _split(a, n):
        t = _bdot(p, b_bf)
        acc = t if acc is None else acc + t
    return acc


def _dot_split_rhs(a_bf, b, n=3):
    acc = None
    for p in _split(b, n):
        t = _bdot(a_bf, p)
        acc = t if acc is None else acc + t
    return acc


def _dot3(a, b):
    ah, al = _split(a, 2)
    bh, bl = _split(b, 2)
    return _bdot(ah, bh) + (_bdot(ah, bl) + _bdot(al, bh))


def _sigmoid(x):
    return 1.0 / (1.0 + jnp.exp(-x))


def _softplus(x):
    return jnp.maximum(x, 0.0) + jnp.log1p(jnp.exp(-jnp.abs(x)))


def _rms(x):
    return x * lax.rsqrt(jnp.mean(x * x, axis=-1, keepdims=True) + NORM_EPS)


LANES = 128


PACK_TILE = D_MODEL // LANES // 2
HIGH_HALF = 0xFFFF0000


def _row_tile(ref, r, lead=()):
    return ref.at[lead + (pl.ds(pl.multiple_of(r * PACK_TILE, PACK_TILE), PACK_TILE),)]


def _rows_to_packed_tiles(x2d, ref):
    n = x2d.shape[0]
    half = D_MODEL // 2
    for j in range(PACK_TILE):
        lo = _bf(x2d[:, j * LANES:(j + 1) * LANES]).astype(F32)
        hi = _bf(x2d[:, half + j * LANES:half + (j + 1) * LANES]).astype(F32)
        word = (pltpu.bitcast(hi, jnp.uint32) & jnp.uint32(HIGH_HALF)) | (pltpu.bitcast(lo, jnp.uint32) >> 16)
        ref[pl.ds(j, n, stride=PACK_TILE), :] = word


def _packed_tiles_to_rows(ref, n, lead=(), dtype=BF16):
    lo, hi = [], []
    for j in range(PACK_TILE):
        word = ref[lead + (pl.ds(j, n, stride=PACK_TILE), slice(None))]
        lo.append(pltpu.bitcast(word << 16, F32).astype(dtype))
        hi.append(pltpu.bitcast(word & jnp.uint32(HIGH_HALF), F32).astype(dtype))
    return jnp.concatenate(lo + hi, axis=1)


def _mod_kernel(cs_ref, w_ref, b_ref, o_ref):
    cs = cs_ref[...]
    s = cs * _sigmoid(cs)
    o_ref[...] = _dot3(s, w_ref[...]) + b_ref[...]


def _modulation(cs, w_mod, b_mod):
    n_out = w_mod.shape[1]
    tn = 1536
    return pl.pallas_call(
        _mod_kernel,
        out_shape=jax.ShapeDtypeStruct((8, n_out), F32),
        grid=(n_out // tn,),
        in_specs=[
            pl.BlockSpec((8, D_MODEL), lambda j: (0, 0)),
            pl.BlockSpec((D_MODEL, tn), lambda j: (0, j)),
            pl.BlockSpec((1, tn), lambda j: (0, j)),
        ],
        out_specs=pl.BlockSpec((8, tn), lambda j: (0, j)),
        compiler_params=_params(("arbitrary",)),
        name="modulation",
    )(cs, w_mod, b_mod.reshape(1, n_out))


def _in_proj_kernel(x_ref, xp_ref, xn_ref, g_ref, sc_ref, sh_ref, w_ref, mu_ref, o_ref, ob_ref, h_s, hh_s,
                    *, tm, n_shift):
    i = pl.program_id(0)
    j = pl.program_id(1)
    n_i = pl.num_programs(0)

    def norm_mod(xv):
        return (_rms(xv) * g_ref[...]) * (1.0 + sc_ref[...]) + sh_ref[...]

    @pl.when(j == 0)
    def _():
        h_s[...] = _bf(norm_mod(x_ref[...]))
        hp = jnp.where(i > 0, norm_mod(xp_ref[...]), 0.0)
        hn = jnp.where(i < n_i - 1, norm_mod(xn_ref[...]), 0.0)
        hh_s[0:8, :] = hp
        hh_s[8:16, :] = hn

    @pl.when(j < n_shift)
    def _():
        w = w_ref[...]
        p = _bdot(h_s[...], w)
        ph = _bdot(_bf(hh_s[...]), w)
        row = lax.broadcasted_iota(jnp.int32, p.shape, 0)
        prev = jnp.where(row == 0, ph[7:8, :], pltpu.roll(p, 1, 0))
        nxt = jnp.where(row == tm - 1, ph[8:9, :], pltpu.roll(p, tm - 1, 0))
        mu = mu_ref[...]
        o_ref[...] = p + mu[0:1, :] * (prev - p) + mu[1:2, :] * (nxt - p)

    @pl.when(j >= n_shift)
    def _():
        ob_ref[...] = _bf(_bdot(h_s[...], w_ref[...]))


def _in_proj(x2d, g, sc, sh, w_bf, mu_pad):
    t_len = x2d.shape[0]
    tm = min(t_len, 1024)
    tn = PROJ_TN
    n_shift = SHIFT_PAD // tn
    tb8 = tm // 8
    nb8 = t_len // 8
    kern = functools.partial(_in_proj_kernel, tm=tm, n_shift=n_shift)
    vec = lambda: pl.BlockSpec((1, D_MODEL), lambda i, j: (0, 0))
    return pl.pallas_call(
        kern,
        out_shape=[jax.ShapeDtypeStruct((t_len, SHIFT_PAD), F32),
                   jax.ShapeDtypeStruct((t_len, PROJ_COLS - SHIFT_PAD), BF16)],
        grid=(t_len // tm, PROJ_COLS // tn),
        in_specs=[
            pl.BlockSpec((tm, D_MODEL), lambda i, j: (i, 0)),
            pl.BlockSpec((8, D_MODEL), lambda i, j: (jnp.maximum(i * tb8 - 1, 0), 0)),
            pl.BlockSpec((8, D_MODEL), lambda i, j: (jnp.minimum((i + 1) * tb8, nb8 - 1), 0)),
            vec(), vec(), vec(),
            pl.BlockSpec((D_MODEL, tn), lambda i, j: (0, j)),
            pl.BlockSpec((2, tn), lambda i, j: (0, jnp.minimum(j, n_shift - 1))),
        ],
        out_specs=[pl.BlockSpec((tm, tn), lambda i, j: (i, jnp.minimum(j, n_shift - 1))),
                   pl.BlockSpec((tm, tn), lambda i, j: (i, jnp.maximum(j - n_shift, 0)))],
        scratch_shapes=[pltpu.VMEM((tm, D_MODEL), BF16), pltpu.VMEM((16, D_MODEL), F32)],
        compiler_params=_params(("arbitrary", "arbitrary")),
        name="in_proj",
    )(x2d, x2d, x2d, g, sc, sh, w_bf, mu_pad)


def _rwkv_prep_kernel(r_ref, k_ref, v_ref, lora_ref, kk_w, ka_w, rk_w, w0_ref, wup_ref, a0_ref, aup_ref, gup_ref,
                      bd_ref, kk_o, lw0_o, lw1_o, kd0_o, kd1_o, b0_o, b1_o, bonus_o, g_o):
    r = r_ref[...]
    k = k_ref[...]
    v = v_ref[...]
    lora = lora_ref[...]
    xw = lora[:, 0:W_LORA]
    xa = lora[:, W_LORA:W_LORA + ICLR_LORA]
    xg = lora[:, W_LORA + ICLR_LORA:]
    bd = bd_ref[...]

    kk = k * kk_w[...]
    nrm = jnp.sqrt(_dot_split_lhs(kk * kk, bd))
    kk = kk / jnp.maximum(nrm, 1e-12)
    kk_o[...] = kk

    tw = jnp.tanh(xw)
    lw_outs = (lw0_o, lw1_o)
    kd_outs = (kd0_o, kd1_o)
    b_outs = (b0_o, b1_o)
    kd_sum = None
    for d in range(2):
        z = w0_ref[d:d + 1, :] + _dot3(tw, wup_ref[d])
        w_log = -_softplus(-z) - 0.5
        lw_outs[d][...] = -jnp.exp(w_log)
        a_d = _sigmoid(a0_ref[d:d + 1, :] + _dot3(xa, aup_ref[d]))
        kd = k * (1.0 + (a_d - 1.0) * ka_w[...])
        kd_outs[d][...] = kd
        b_outs[d][...] = kk * a_d
        kd_sum = kd if kd_sum is None else kd_sum + kd
    g_o[...] = _dot3(_sigmoid(xg), gup_ref[...])
    bonus_o[...] = _dot_split_lhs(r * kd_sum * rk_w[...], bd) * v


def _rwkv_prep(proj, lp, bd):
    t_len = proj.shape[0]
    tm = min(t_len, 512)
    aw = A_WIDTH
    col = lambda c: pl.BlockSpec((tm, aw), lambda i: (i, c))
    vec = lambda: pl.BlockSpec((1, aw), lambda i: (0, 0))
    full = lambda shp: pl.BlockSpec(shp, lambda i: (0,) * len(shp))
    outs = [jax.ShapeDtypeStruct((t_len, aw), F32)] * 9
    return pl.pallas_call(
        _rwkv_prep_kernel,
        out_shape=outs,
        grid=(t_len // tm,),
        in_specs=[
            col(0), col(1), col(2),
            pl.BlockSpec((tm, LORA_COLS), lambda i: (i, 3 * aw // LORA_COLS)),
            vec(), vec(), vec(),
            full((2, aw)), full((2, W_LORA, aw)), full((2, aw)), full((2, ICLR_LORA, aw)), full((G_LORA, aw)),
            full((aw, aw)),
        ],
        out_specs=[pl.BlockSpec((tm, aw), lambda i: (i, 0))] * 9,
        compiler_params=_params(("arbitrary",)),
        name="rwkv_prep",
    )(proj, proj, proj, proj, lp["k_k"], lp["k_a"], lp["r_k"], lp["w0"], lp["w_up"], lp["a0"], lp["a_up"],
      lp["g_up"], bd)


def _rwkv_scan_kernel(rf, vf, kkf, lwf, kdf, bf_, rb, vb, kkb, lwb, kdb, bb, s0_ref, of_ref, ob_ref, sfin_ref, s_s,
                      *, tb):
    step = pl.program_id(0)

    @pl.when(step == 0)
    def _():
        s_s[...] = s0_ref[...]

    c = RWKV_CHUNK
    n = A_HEAD_DIM
    assert c == n
    pw = 2 * n
    row = lax.broadcasted_iota(jnp.int32, (c, pw), 0)
    lane = lax.broadcasted_iota(jnp.int32, (c, pw), 1)
    col = lane % c
    first = lane < n
    eye = (row == col).astype(F32)
    r_c = lax.broadcasted_iota(jnp.int32, (c, c), 0)
    c_c = lax.broadcasted_iota(jnp.int32, (c, c), 1)
    n_ch = tb // c
    dirs = ((rf, vf, kkf, lwf, kdf, bf_, of_ref), (rb, vb, kkb, lwb, kdb, bb, ob_ref))
    pairs = range(A_HEADS // 2)
    psl = [slice(q * pw, (q + 1) * pw) for q in pairs]

    def bd(x):
        z = jnp.zeros_like(x)
        return jnp.concatenate([jnp.where(first, x, z), jnp.where(first, z, x)], axis=0)

    def chunk_terms(d, refs, ci):
        r_ref, v_ref, kk_ref, lw_ref, kd_ref, b_ref, _ = refs
        if d == 0:
            incl, strict, tri = row >= col, row > col, r_c >= c_c
        else:
            incl, strict, tri = row <= col, row < col, r_c <= c_c
        rows = slice(ci * c, (ci + 1) * c)
        lw = lw_ref[rows, :]
        r = r_ref[rows, :]
        kk = kk_ref[rows, :]
        kd = kd_ref[rows, :]
        b = b_ref[rows, :]
        cl = _dot_split_rhs(tri.astype(BF16), lw)
        cl_tot = cl[c - 1:c, :] if d == 0 else cl[0:1, :]
        e_neg = jnp.exp(-cl)
        e_end = jnp.exp(cl_tot - cl)
        g_tot = jnp.exp(cl_tot)
        rt = r * jnp.exp(cl)
        x_all = _bf(jnp.concatenate([-(kk * jnp.exp(cl - lw)), rt], axis=0))
        bt_all = _bf(b * e_neg)
        kt_all = _bf(kd * e_neg)
        return {
            "rows": rows, "strict": strict, "incl": incl,
            "x": [x_all[:, ps] for ps in psl],
            "z": [jnp.concatenate([bd(bt_all[:, ps]), bd(kt_all[:, ps])], axis=0) for ps in psl],
            "bd_bh": [bd(_bf(b * e_end)[:, ps]) for ps in psl],
            "bd_kh": [bd(_bf(kd * e_end)[:, ps]) for ps in psl],
            "bd_v": [bd(_bf(v_ref[rows, :])[:, ps]) for ps in psl],
            "rt": [rt[:, ps] for ps in psl],
            "g_tot": [g_tot[:, ps] for ps in psl],
        }

    chunks = [(d, cc) for cc in range(n_ch) for d in range(2)]
    prep = {(d, cc): chunk_terms(d, dirs[d], cc if d == 0 else n_ch - 1 - cc) for d, cc in chunks}
    units = [(k, q) for k in chunks for q in pairs]

    def per_unit(name, indexed=True):
        return [prep[k][name][q] if indexed else prep[k][name] for k, q in units]

    strict, incl = per_unit("strict", False), per_unit("incl", False)
    x_u, bd_bh, bd_kh, bd_v, rt_u = (per_unit(s) for s in ("x", "bd_bh", "bd_kh", "bd_v", "rt"))
    g = [_dot_nt(x, z) for x, z in zip(x_u, per_unit("z"))]
    a_ab = [jnp.where(s, gi[:c, :pw], 0.0) for s, gi in zip(strict, g)]
    a_rb = [_bf(jnp.where(i, gi[c:, :pw], 0.0)) for i, gi in zip(incl, g)]
    a_k = [_bf(jnp.concatenate([jnp.where(s, gi[:c, pw:], 0.0), jnp.where(i, gi[c:, pw:], 0.0)], axis=0))
           for s, i, gi in zip(strict, incl, g)]
    a2 = [_bdot(_bf(a), bd(_bf(a))) for a in a_ab]
    pa = [jnp.concatenate([eye + a, sq], axis=0) for a, sq in zip(a_ab, a2)]
    for _ in range(4):
        nxt = [_bdot(_bf(x), bd(_bf(x[c:]))) for x in pa]
        pa = [jnp.concatenate([x[:c] + y[:c], y[c:]], axis=0) for x, y in zip(pa, nxt)]
    p = [_bf(x[:c] + _bdot(_bf(x[:c]), bd(_bf(x[c:])))) for x in pa]
    vk = [_bdot(a, v) for a, v in zip(a_k, bd_v)]
    tw = [_bf(_bdot(pi, jnp.concatenate([bd(x[:c]), bd(_bf(vki[:c]))], axis=1)))
          for pi, vki, x in zip(p, vk, x_u)]
    bd_at = [bd(t[:, :pw]) for t in tw]
    bd_w = [bd(t[:, pw:]) for t in tw]
    mt = [_bf(_dot_tn(a, bh)) for a, bh in zip(bd_at, bd_bh)]
    ntf = [_dot_tn(jnp.concatenate([w, v], axis=0), jnp.concatenate([bh, kh], axis=0))
           for w, v, bh, kh in zip(bd_w, bd_v, bd_bh, bd_kh)]
    nt = [m[:n] + m[n:] for m in ntf]
    rw = [_bdot(a, jnp.concatenate([at, w], axis=1)) for a, at, w in zip(a_rb, bd_at, bd_w)]
    ry = [_bf(r + w[:, :pw]) for w, r in zip(rw, rt_u)]
    y0 = [w[:, pw:] + vki[c:] for w, vki in zip(rw, vk)]
    term = {u: vals for u, vals in zip(units, zip(mt, nt, ry, y0, per_unit("g_tot")))}

    state = [[s_s[d, q] for q in pairs] for d in range(2)]
    dq = [(d, q) for d in range(2) for q in pairs]
    for cc in range(n_ch):
        mt_c, nt_c, ry_c, y0_c, gt_c = zip(*[term[((d, cc), q)] for d, q in dq])
        s0 = [state[d][q] for d, q in dq]
        ys = [_dot_nt(ry_c[i], bd(_bf(s0[i]))) + y0_c[i] for i in range(len(dq))]
        upd = [_bdot(jnp.concatenate(_split(s0[i], 2), axis=0), mt_c[i]) for i in range(len(dq))]
        for i, (d, q) in enumerate(dq):
            state[d][q] = s0[i] * gt_c[i] + (upd[i][:n] + upd[i][n:]) + nt_c[i]
        for d in range(2):
            y_d = [ys[i] for i, (dd, _) in enumerate(dq) if dd == d]
            dirs[d][6][prep[(d, cc)]["rows"], :] = jnp.concatenate(y_d, axis=1)
    for d in range(2):
        for q in pairs:
            s_s[d, q] = state[d][q]

    @pl.when(step == pl.num_programs(0) - 1)
    def _():
        sfin_ref[...] = s_s[...]


def _rwkv_scan(proj, prep, s0):
    kk, lw0, lw1, kd0, kd1, b0, b1 = prep[:7]
    t_len = proj.shape[0]
    tb = 2 * RWKV_CHUNK
    nb = t_len // tb
    aw = A_WIDTH
    fwd = lambda c: pl.BlockSpec((tb, aw), lambda i: (i, c))
    bwd = lambda c: pl.BlockSpec((tb, aw), lambda i: (nb - 1 - i, c))
    st = pl.BlockSpec(RWKV_STATE_SHAPE, lambda i: (0, 0, 0, 0))
    kern = functools.partial(_rwkv_scan_kernel, tb=tb)
    return pl.pallas_call(
        kern,
        out_shape=[jax.ShapeDtypeStruct((t_len, aw), F32), jax.ShapeDtypeStruct((t_len, aw), F32),
                   jax.ShapeDtypeStruct(RWKV_STATE_SHAPE, F32)],
        grid=(nb,),
        in_specs=[fwd(0), fwd(2), fwd(0), fwd(0), fwd(0), fwd(0),
                  bwd(0), bwd(2), bwd(0), bwd(0), bwd(0), bwd(0), st],
        out_specs=[fwd(0), bwd(0), st],
        scratch_shapes=[pltpu.VMEM(RWKV_STATE_SHAPE, F32)],
        compiler_params=_params(("arbitrary",)),
        name="rwkv_scan",
    )(proj, proj, kk, lw0, kd0, b0, proj, proj, kk, lw1, kd1, b1, s0)


def _retention_kernel(qf, kf, vf, cosf, sinf, qb, kb, vb, cosb, sinb, lgt_ref, r0_ref, yf_ref, yb_ref, rfin_ref, r_s):
    step = pl.program_id(0)

    @pl.when(step == 0)
    def _():
        r_s[...] = r0_ref[...]

    c = RET_CHUNK
    dk = B_QK_DIM
    dv = B_V_DIM
    lg_all = -_softplus(-lgt_ref[...])
    rowf = lax.broadcasted_iota(jnp.int32, (c, c), 0).astype(F32)
    colf = lax.broadcasted_iota(jnp.int32, (c, c), 1).astype(F32)
    lane = lax.broadcasted_iota(jnp.int32, (c, dk), 1)
    first_half = (lane % 64) < 32
    dirs = ((qf, kf, vf, cosf, sinf, yf_ref), (qb, kb, vb, cosb, sinb, yb_ref))
    for d, (q_ref, k_ref, v_ref, cos_ref, sin_ref, y_ref) in enumerate(dirs):
        cos = cos_ref[...]
        sin = sin_ref[...]
        diff = (rowf - colf) if d == 0 else (colf - rowf)
        pos = rowf if d == 0 else (c - 1.0) - rowf
        for h in range(B_HEADS):
            lg = lg_all[d * B_HEADS + h:d * B_HEADS + h + 1, :]
            qh = q_ref[:, h * dk:(h + 1) * dk].astype(F32)
            kh = k_ref[:, h * dk:(h + 1) * dk].astype(F32) * (dk ** -0.5)

            def rope(xv):
                swapped = jnp.where(first_half, pltpu.roll(xv, dk - 32, 1), pltpu.roll(xv, 32, 1))
                return xv * cos + swapped * sin

            qh = rope(qh)
            kh = rope(kh)
            vh = v_ref[:, h * dv:(h + 1) * dv]
            dmask = jnp.where(diff >= 0.0, jnp.exp(lg * jnp.maximum(diff, 0.0)), 0.0)
            scores = _dot_nt(_bf(qh), _bf(kh)) * dmask
            inner = _bdot(_bf(scores), vh)
            xi = jnp.exp(lg * (pos + 1.0))
            zeta = jnp.exp(lg * ((c - 1.0) - pos))
            r_prev = r_s[d, h]
            cross = _bdot(_bf(qh * xi), _bf(r_prev))
            y_ref[:, h * dv:(h + 1) * dv] = inner + cross
            kv = _dot_tn(_bf(kh * zeta), vh)
            g_chunk = jnp.exp(lg * float(c))
            r_s[d, h] = jnp.concatenate([g_chunk, g_chunk], axis=1) * r_prev + kv

    @pl.when(step == pl.num_programs(0) - 1)
    def _():
        rfin_ref[...] = r_s[...]


def _retention(proj, cos, sin, lgt, r0):
    t_len = proj.shape[0]
    c = RET_CHUNK
    nc = t_len // c
    qw = B_QK_WIDTH
    vw = B_V_WIDTH
    fq = lambda col: pl.BlockSpec((c, qw), lambda i: (i, col))
    bq = lambda col: pl.BlockSpec((c, qw), lambda i: (nc - 1 - i, col))
    st = pl.BlockSpec((2, B_HEADS, B_QK_DIM, B_V_DIM), lambda i: (0, 0, 0, 0))
    q_col = 0
    v_col = 2 * qw // vw
    return pl.pallas_call(
        _retention_kernel,
        out_shape=[jax.ShapeDtypeStruct((t_len, vw), F32), jax.ShapeDtypeStruct((t_len, vw), F32),
                   jax.ShapeDtypeStruct((2, B_HEADS, B_QK_DIM, B_V_DIM), F32)],
        grid=(nc,),
        in_specs=[
            fq(q_col), fq(q_col + 1), pl.BlockSpec((c, vw), lambda i: (i, v_col)),
            pl.BlockSpec((c, B_QK_DIM), lambda i: (i, 0)), pl.BlockSpec((c, B_QK_DIM), lambda i: (i, 0)),
            bq(q_col), bq(q_col + 1), pl.BlockSpec((c, vw), lambda i: (nc - 1 - i, v_col)),
            pl.BlockSpec((c, B_QK_DIM), lambda i: (nc - 1 - i, 0)),
            pl.BlockSpec((c, B_QK_DIM), lambda i: (nc - 1 - i, 0)),
            pl.BlockSpec((2 * B_HEADS, 128), lambda i: (0, 0)), st,
        ],
        out_specs=[pl.BlockSpec((c, vw), lambda i: (i, 0)), pl.BlockSpec((c, vw), lambda i: (nc - 1 - i, 0)), st],
        scratch_shapes=[pltpu.VMEM((2, B_HEADS, B_QK_DIM, B_V_DIM), F32)],
        compiler_params=_params(("arbitrary",)),
        name="retention",
    )(proj, proj, proj, cos, sin, proj, proj, proj, cos, sin, lgt, r0)


def _mix_out_kernel(of_ref, ob_ref, bonus_ref, g_ref, yf_ref, yb_ref, gb_ref, ga_ref, gbb_ref, x_ref,
                    alnw, alnb, rlnw, rlnb, npm, npf, g1_ref, sc2_ref, sh2_ref, wa_ref, wb_ref, wo_ref, bd_ref,
                    x1_ref, h2_ref, h3_ref):
    bd = bd_ref[...]
    o = of_ref[...] + ob_ref[...]
    mu = _dot_split_lhs(o, bd) * (1.0 / A_HEAD_DIM)
    oc = o - mu
    var = _dot_split_lhs(oc * oc, bd) * (1.0 / A_HEAD_DIM)
    ya = oc * lax.rsqrt(var + RWKV_GN_EPS) * alnw[...] + alnb[...] + bonus_ref[...]
    ya = _bdot(_bf(ya * g_ref[...]), wa_ref[...])

    y = yf_ref[...] + yb_ref[...]
    parts = []
    for h in range(B_HEADS):
        seg = y[:, h * B_V_DIM:(h + 1) * B_V_DIM]
        m = jnp.mean(seg, axis=-1, keepdims=True)
        sc = seg - m
        vr = jnp.mean(sc * sc, axis=-1, keepdims=True)
        parts.append(sc * lax.rsqrt(vr + RET_GN_EPS))
    yn = jnp.concatenate(parts, axis=1) * rlnw[...] + rlnb[...]
    gb = gb_ref[...].astype(F32)
    yb = _bdot(_bf(yn * (gb * _sigmoid(gb))), wb_ref[...])

    merged = _sigmoid(ga_ref[...].astype(F32)) * ya + _sigmoid(gbb_ref[...].astype(F32)) * yb
    mix = _bdot(_bf(merged), wo_ref[...])
    x1 = x_ref[...] + g1_ref[...] * (_rms(mix) * npm[...])
    x1_ref[...] = x1
    h2 = (_rms(x1) * npf[...]) * (1.0 + sc2_ref[...]) + sh2_ref[...]
    h2_ref[...] = h2
    _rows_to_packed_tiles(h2, h3_ref)


def _mix_out(x2d, proj, o_f, o_b, bonus, g, y_f, y_b, lp, vecs, bd):
    t_len = x2d.shape[0]
    tm = min(t_len, 256)
    aw = A_WIDTH
    d = D_MODEL
    ta = lambda: pl.BlockSpec((tm, aw), lambda i: (i, 0))
    td = lambda: pl.BlockSpec((tm, d), lambda i: (i, 0))
    pc = lambda c: pl.BlockSpec((tm, d), lambda i: (i, c))
    va = lambda: pl.BlockSpec((1, aw), lambda i: (0, 0))
    vd = lambda: pl.BlockSpec((1, d), lambda i: (0, 0))
    full = lambda shp: pl.BlockSpec(shp, lambda i: (0, 0))
    gcol = (2 * B_QK_WIDTH + B_V_WIDTH) // d
    return pl.pallas_call(
        _mix_out_kernel,
        out_shape=[jax.ShapeDtypeStruct((t_len, d), F32), jax.ShapeDtypeStruct((t_len, d), F32),
                   jax.ShapeDtypeStruct((t_len * PACK_TILE, LANES), jnp.uint32)],
        grid=(t_len // tm,),
        in_specs=[ta(), ta(), ta(), ta(), td(), td(), pc(gcol), pc(gcol + 1), pc(gcol + 2), td(),
                  va(), va(), vd(), vd(), vd(), vd(), vd(), vd(), vd(),
                  full((aw, d)), full((d, d)), full((d, d)), full((aw, aw))],
        out_specs=[td(), td(), pl.BlockSpec((tm * PACK_TILE, LANES), lambda i: (i, 0))],
        compiler_params=_params(("arbitrary",)),
        name="mix_out",
    )(o_f, o_b, bonus, g, y_f, y_b, proj, proj, proj, x2d,
      lp["ln_w"], lp["ln_b"], lp["ret_ln_w"], lp["ret_ln_b"], vecs["npm"], vecs["npf"], vecs["g1"], vecs["sc2"],
      vecs["sh2"], lp["w_a"], lp["w_b"], lp["w_o"], bd)


def _router_kernel(h_ref, rw_ref, bias_ref, sel_ref, wts_ref, rank_ref, cnt_ref, cnt_s, *, tm):
    step = pl.program_id(0)

    @pl.when(step == 0)
    def _():
        cnt_s[...] = jnp.zeros_like(cnt_s)

    ne = N_EXPERTS
    scores = _sigmoid(_dot3(h_ref[...], rw_ref[...]))
    work = scores + bias_ref[...]
    lane = lax.broadcasted_iota(jnp.int32, (tm, ne), 1).astype(F32)
    idxs = []
    vals = []
    for _ in range(TOP_K):
        m = jnp.max(work, axis=-1, keepdims=True)
        idx = jnp.min(jnp.where(work == m, lane, float(ne)), axis=-1, keepdims=True)
        oh = lane == idx
        vals.append(jnp.sum(jnp.where(oh, scores, 0.0), axis=-1, keepdims=True))
        idxs.append(idx)
        work = jnp.where(oh, -jnp.inf, work)
    sel_f = jnp.concatenate(idxs, axis=1)
    s_sel = jnp.concatenate(vals, axis=1)
    wts_ref[...] = s_sel / jnp.sum(s_sel, axis=1, keepdims=True) * ROUTED_SCALE
    sel_ref[...] = sel_f.astype(jnp.int32)

    hit = work == -jnp.inf
    r_i = lax.broadcasted_iota(jnp.int32, (tm, tm), 0)
    c_i = lax.broadcasted_iota(jnp.int32, (tm, tm), 1)
    before = _bdot((r_i > c_i).astype(BF16), hit.astype(BF16)) + cnt_s[...]
    ranks = [jnp.sum(jnp.where(lane == idxs[k], before, 0.0), axis=-1, keepdims=True) for k in range(TOP_K)]
    rank_ref[...] = jnp.concatenate(ranks, axis=1).astype(jnp.int32)
    cnt = cnt_s[...] + jnp.sum(hit.astype(F32), axis=0, keepdims=True)
    cnt_s[...] = cnt
    cnt_ref[...] = cnt.astype(jnp.int32)


def _router(h2, router_w, router_bias):
    t_len = h2.shape[0]
    tm = min(t_len, 256)
    kern = functools.partial(_router_kernel, tm=tm)
    tk = lambda: pl.BlockSpec((tm, TOP_K), lambda i: (i, 0))
    return pl.pallas_call(
        kern,
        out_shape=[jax.ShapeDtypeStruct((t_len, TOP_K), jnp.int32), jax.ShapeDtypeStruct((t_len, TOP_K), F32),
                   jax.ShapeDtypeStruct((t_len, TOP_K), jnp.int32), jax.ShapeDtypeStruct((1, N_EXPERTS), jnp.int32)],
        grid=(t_len // tm,),
        in_specs=[pl.BlockSpec((tm, D_MODEL), lambda i: (i, 0)),
                  pl.BlockSpec((D_MODEL, N_EXPERTS), lambda i: (0, 0)),
                  pl.BlockSpec((1, N_EXPERTS), lambda i: (0, 0))],
        out_specs=[tk(), tk(), tk(), pl.BlockSpec((1, N_EXPERTS), lambda i: (0, 0))],
        scratch_shapes=[pltpu.VMEM((1, N_EXPERTS), F32)],
        compiler_params=_params(("arbitrary",)),
        name="router",
    )(h2, router_w, router_bias.reshape(1, N_EXPERTS))


def _slot(sel_ref, rank_ref, start_ref, a):
    return start_ref[sel_ref[a]] + rank_ref[a]


def _dispatch_kernel(sel_ref, rank_ref, start_ref, cnt_ref, h_ref, xs_out, zrow, sem, zsem, *, tm):
    def row_copy(t, k):
        slot = _slot(sel_ref, rank_ref, start_ref, t * TOP_K + k)
        return pltpu.make_async_copy(_row_tile(h_ref, t), _row_tile(xs_out, slot), sem)

    def issue(t, carry):
        for k in range(TOP_K):
            row_copy(t, k).start(priority=k % 2)
        return carry

    lax.fori_loop(0, tm, issue, 0)

    def drain(t, carry):
        for k in range(TOP_K):
            row_copy(t, k).wait()
        return carry

    lax.fori_loop(0, tm, drain, 0)

    @pl.when(pl.program_id(0) == pl.num_programs(0) - 1)
    def _():
        zrow[...] = jnp.zeros_like(zrow)

        def for_pad_runs(e0, fn):
            def per_expert(e, carry):
                n = cnt_ref[e]
                n_pad = (n + MOE_BLOCK - 1) // MOE_BLOCK * MOE_BLOCK
                pad = n_pad - n
                off = start_ref[e] + n
                for bit in reversed(range(MOE_BLOCK.bit_length() - 1)):
                    size = 1 << bit
                    is_set = ((pad >> bit) & 1) == 1

                    @pl.when(is_set)
                    def _(off=off, size=size):
                        dst = xs_out.at[pl.ds(pl.multiple_of(off * PACK_TILE, PACK_TILE), size * PACK_TILE)]
                        fn(pltpu.make_async_copy(zrow.at[pl.ds(0, size * PACK_TILE)], dst, zsem))

                    off = off + jnp.where(is_set, size, 0)
                return carry

            lax.fori_loop(e0, e0 + ZERO_PAD_GROUP, per_expert, 0)

        def per_group(gi, carry):
            e0 = gi * ZERO_PAD_GROUP
            for_pad_runs(e0, lambda cp: cp.start())
            for_pad_runs(e0, lambda cp: cp.wait())
            return carry

        lax.fori_loop(0, N_EXPERTS // ZERO_PAD_GROUP, per_group, 0)


def _dispatch(h3, sel_flat, rank_flat, pad_start, counts, n_slots):
    t_len = h3.shape[0] // PACK_TILE
    tm = min(t_len, 256)
    kern = functools.partial(_dispatch_kernel, tm=tm)
    smem_blk = lambda: pl.BlockSpec((tm * TOP_K,), lambda i: (i,), memory_space=pltpu.SMEM)
    smem_all = lambda: pl.BlockSpec((N_EXPERTS,), lambda i: (0,), memory_space=pltpu.SMEM)
    return pl.pallas_call(
        kern,
        out_shape=jax.ShapeDtypeStruct((n_slots * PACK_TILE, LANES), jnp.uint32),
        grid=(t_len // tm,),
        in_specs=[smem_blk(), smem_blk(), smem_all(), smem_all(),
                  pl.BlockSpec((tm * PACK_TILE, LANES), lambda i: (i, 0))],
        out_specs=pl.BlockSpec(memory_space=pl.ANY),
        scratch_shapes=[pltpu.VMEM((MOE_BLOCK // 2 * PACK_TILE, LANES), jnp.uint32), pltpu.SemaphoreType.DMA(()),
                        pltpu.SemaphoreType.DMA(())],
        compiler_params=_params(("arbitrary",)),
        name="dispatch",
    )(sel_flat, rank_flat, pad_start, counts, h3)


def _expert_kernel(be_ref, nu_ref, xs_ref, wg_ref, wu_ref, wd_ref, ys_ref, wg_s, wu_s, wd_s):
    b = pl.program_id(0)
    used = b < nu_ref[0]
    new_expert = (b == 0) | (be_ref[b] != be_ref[jnp.maximum(b - 1, 0)])

    @pl.when(used & new_expert)
    def _():
        wg_s[...] = _bf(wg_ref[0])
        wu_s[...] = _bf(wu_ref[0])
        wd_s[...] = _bf(wd_ref[0])

    @pl.when(used)
    def _():
        xb = _packed_tiles_to_rows(xs_ref, MOE_BLOCK)
        gate = _bdot(xb, wg_s[...])
        up = _bdot(xb, wu_s[...])
        act = gate * _sigmoid(gate) * up
        _rows_to_packed_tiles(_bdot(_bf(act), wd_s[...]), ys_ref)

    @pl.when(jnp.logical_not(used))
    def _():
        ys_ref[...] = jnp.zeros_like(ys_ref)


def _experts(xs, block_e, n_used, w_gate, w_up, w_down):
    n_slots = xs.shape[0] // PACK_TILE
    n_blocks = n_slots // MOE_BLOCK
    rows = (MOE_BLOCK * PACK_TILE, LANES)
    grid_spec = pltpu.PrefetchScalarGridSpec(
        num_scalar_prefetch=2,
        grid=(n_blocks,),
        in_specs=[
            pl.BlockSpec((MOE_BLOCK * PACK_TILE, LANES), lambda b, be, nu: (jnp.where(b < nu[0], b, 0), 0)),
            pl.BlockSpec((1, D_MODEL, EXPERT_FF), lambda b, be, nu: (be[b], 0, 0)),
            pl.BlockSpec((1, D_MODEL, EXPERT_FF), lambda b, be, nu: (be[b], 0, 0)),
            pl.BlockSpec((1, EXPERT_FF, D_MODEL), lambda b, be, nu: (be[b], 0, 0)),
        ],
        out_specs=pl.BlockSpec(rows, lambda b, be, nu: (b, 0)),
        scratch_shapes=[pltpu.VMEM((D_MODEL, EXPERT_FF), BF16), pltpu.VMEM((D_MODEL, EXPERT_FF), BF16),
                        pltpu.VMEM((EXPERT_FF, D_MODEL), BF16)],
    )
    return pl.pallas_call(
        _expert_kernel,
        out_shape=jax.ShapeDtypeStruct((n_slots * PACK_TILE, LANES), jnp.uint32),
        grid_spec=grid_spec,
        compiler_params=_params(("arbitrary",)),
        name="experts",
    )(block_e, n_used, xs, w_gate, w_up, w_down)


def _combine_kernel(sel_ref, rank_ref, start_ref, ys_ref, wts_ref, h_ref, x1_ref, sg_ref, su_ref, sd_ref, npo, g2_ref,
                    o_ref, buf, sem, *, tm):
    def row_copy(t, k):
        slot = _slot(sel_ref, rank_ref, start_ref, t * TOP_K + k)
        return pltpu.make_async_copy(_row_tile(ys_ref, slot), _row_tile(buf, t, (k,)), sem)

    def issue(t, carry):
        for k in range(TOP_K):
            row_copy(t, k).start(priority=k % 2)
        return carry

    lax.fori_loop(0, tm, issue, 0)

    hb = _bf(h_ref[...])
    gate = _bdot(hb, sg_ref[...])
    up = _bdot(hb, su_ref[...])
    shared = _bdot(_bf(gate * _sigmoid(gate) * up), sd_ref[...])

    def drain(t, carry):
        for k in range(TOP_K):
            row_copy(t, k).wait()
        return carry

    lax.fori_loop(0, tm, drain, 0)

    wts = wts_ref[...]
    routed = _packed_tiles_to_rows(buf, tm, (0,), F32) * wts[:, 0:1]
    for k in range(1, TOP_K):
        routed = routed + _packed_tiles_to_rows(buf, tm, (k,), F32) * wts[:, k:k + 1]
    o_ref[...] = x1_ref[...] + g2_ref[...] * (_rms(routed + shared) * npo[...])


def _combine(ys, sel_flat, rank_flat, pad_start, wts, h2, x1, sg, su, sd, npo, g2):
    t_len = h2.shape[0]
    tm = min(t_len, 256)
    d = D_MODEL
    kern = functools.partial(_combine_kernel, tm=tm)
    smem_blk = lambda: pl.BlockSpec((tm * TOP_K,), lambda i: (i,), memory_space=pltpu.SMEM)
    td = lambda: pl.BlockSpec((tm, d), lambda i: (i, 0))
    vd = lambda: pl.BlockSpec((1, d), lambda i: (0, 0))
    return pl.pallas_call(
        kern,
        out_shape=jax.ShapeDtypeStruct((t_len, d), F32),
        grid=(t_len // tm,),
        in_specs=[smem_blk(), smem_blk(),
                  pl.BlockSpec((N_EXPERTS,), lambda i: (0,), memory_space=pltpu.SMEM),
                  pl.BlockSpec(memory_space=pl.ANY),
                  pl.BlockSpec((tm, TOP_K), lambda i: (i, 0)),
                  td(), td(),
                  pl.BlockSpec((d, SHARED_FF), lambda i: (0, 0)), pl.BlockSpec((d, SHARED_FF), lambda i: (0, 0)),
                  pl.BlockSpec((SHARED_FF, d), lambda i: (0, 0)), vd(), vd()],
        out_specs=td(),
        scratch_shapes=[pltpu.VMEM((TOP_K, tm * PACK_TILE, LANES), jnp.uint32), pltpu.SemaphoreType.DMA(())],
        compiler_params=_params(("arbitrary",)),
        name="combine",
    )(sel_flat, rank_flat, pad_start, ys, wts, h2, x1, sg, su, sd, npo, g2)


def _block_diag_ones(width, group):
    idx = np.arange(width) // group
    return jnp.asarray(idx[:, None] == idx[None, :], dtype=BF16)


def _rope_tables(t_len):
    pos = np.arange(t_len)
    rows = (pos // GRID_W).astype(np.float32)
    cols = (pos % GRID_W).astype(np.float32)
    quarter = B_QK_DIM // 4
    inv_freq = jnp.asarray(ROPE_BASE, F32) ** (-jnp.arange(quarter, dtype=F32) / quarter)
    ang_r = jnp.asarray(rows)[:, None] * inv_freq
    ang_c = jnp.asarray(cols)[:, None] * inv_freq
    cr, sr, cc, sc = jnp.cos(ang_r), jnp.sin(ang_r), jnp.cos(ang_c), jnp.sin(ang_c)
    return jnp.concatenate([cr, cr, cc, cc], axis=1), jnp.concatenate([-sr, sr, -sc, sc], axis=1)


def _token_mixer(x2d, g_pre, sc, sh, lp, cos, sin, states, bd):
    proj_a, proj_b = _in_proj(x2d, g_pre, sc, sh, lp["w_in"], lp["mu"])
    prep = _rwkv_prep(proj_a, lp, bd)
    o_f, o_b, s_fin = _rwkv_scan(proj_a, prep, states[0])
    y_f, y_b, r_fin = _retention(proj_b, cos, sin, lp["lgt"], states[1])
    return proj_b, prep, (o_f, o_b), (y_f, y_b), (s_fin, r_fin)


def kernel(x, c, ctx, c_ctx, w_mod, b_mod, norm_pre_mix, norm_post_mix, norm_pre_ffn, norm_post_ffn, w_in, shift_mu,
           rwkv_w0, rwkv_w_up, rwkv_a0, rwkv_a_up, rwkv_g_up, rwkv_k_k, rwkv_k_a, rwkv_r_k, rwkv_ln_w, rwkv_ln_b,
           w_branch_a, ret_decay_logit, ret_ln_w, ret_ln_b, w_branch_b, w_out, router_w, router_bias, exp_w_gate,
           exp_w_up, exp_w_down, sh_w_gate, sh_w_up, sh_w_down):
    d = D_MODEL
    assert x.shape[0] == 1 and w_in.shape[0] == 1, "single batch element, single layer"
    t_len = x.shape[1]
    x2d = x.reshape(t_len, d)
    ctx2d = ctx.reshape(ctx.shape[1], d)
    row = lambda a: a.reshape(1, -1)

    cs = jnp.zeros((8, d), F32).at[0].set(c[0]).at[1].set(c_ctx)
    mod = _modulation(cs, w_mod[0], b_mod[0])
    sh1, sc1, g1, sh2, sc2, g2 = [mod[0:1, i * d:(i + 1) * d] for i in range(6)]
    csh1, csc1 = mod[1:2, 0:d], mod[1:2, d:2 * d]

    w_in_p = jnp.concatenate(
        [w_in[0][:, :SHIFT_COLS], jnp.zeros((d, SHIFT_PAD - SHIFT_COLS), F32), w_in[0][:, SHIFT_COLS:]], axis=1)
    lp = {
        "w_in": w_in_p.astype(BF16),
        "mu": jnp.pad(shift_mu[0], ((0, 0), (0, SHIFT_PAD - SHIFT_COLS))),
        "k_k": row(rwkv_k_k[0]), "k_a": row(rwkv_k_a[0]), "r_k": row(rwkv_r_k[0]),
        "w0": rwkv_w0[0], "w_up": rwkv_w_up[0], "a0": rwkv_a0[0], "a_up": rwkv_a_up[0], "g_up": rwkv_g_up[0],
        "ln_w": row(rwkv_ln_w[0]), "ln_b": row(rwkv_ln_b[0]),
        "ret_ln_w": row(ret_ln_w[0]), "ret_ln_b": row(ret_ln_b[0]),
        "lgt": jnp.broadcast_to(ret_decay_logit[0].reshape(2 * B_HEADS, 1), (2 * B_HEADS, 128)),
        "w_a": w_branch_a[0].astype(BF16), "w_b": w_branch_b[0].astype(BF16), "w_o": w_out[0].astype(BF16),
    }
    bd = _block_diag_ones(A_WIDTH, A_HEAD_DIM)
    g_pre = row(norm_pre_mix[0])

    t_ctx = ctx2d.shape[0]
    zero_states = (jnp.zeros(RWKV_STATE_SHAPE, F32),
                   jnp.zeros((2, B_HEADS, B_QK_DIM, B_V_DIM), F32))
    ones = jnp.ones((t_ctx, B_QK_DIM), F32)
    *_, ctx_states = _token_mixer(ctx2d, g_pre, csc1, csh1, lp, ones, jnp.zeros_like(ones), zero_states, bd)

    cos, sin = _rope_tables(t_len)
    proj, prep, (o_f, o_b), (y_f, y_b), _ = _token_mixer(x2d, g_pre, sc1, sh1, lp, cos, sin, ctx_states, bd)
    vecs = {"npm": row(norm_post_mix[0]), "npf": row(norm_pre_ffn[0]), "g1": g1, "sc2": sc2, "sh2": sh2}
    x1, h2, h3 = _mix_out(x2d, proj, o_f, o_b, prep[7], prep[8], y_f, y_b, lp, vecs, bd)

    sel, wts, rank, counts = _router(h2, router_w[0], router_bias[0])
    counts = counts.reshape(N_EXPERTS)
    padded = (counts + MOE_BLOCK - 1) // MOE_BLOCK * MOE_BLOCK
    pad_end = jnp.cumsum(padded)
    pad_start = (pad_end - padded).astype(jnp.int32)
    n_assign = t_len * TOP_K
    n_blocks = (n_assign + N_EXPERTS * (MOE_BLOCK - 1) + MOE_BLOCK - 1) // MOE_BLOCK
    block_start = jnp.arange(n_blocks, dtype=jnp.int32) * MOE_BLOCK
    block_e = jnp.minimum(jnp.sum(pad_end[None, :] <= block_start[:, None], axis=1), N_EXPERTS - 1).astype(jnp.int32)
    n_used = (pad_end[-1:] // MOE_BLOCK).astype(jnp.int32)
    sel_flat = sel.reshape(n_assign)
    rank_flat = rank.reshape(n_assign)

    xs = _dispatch(h3, sel_flat, rank_flat, pad_start, counts, n_blocks * MOE_BLOCK)
    ys = _experts(xs, block_e, n_used, exp_w_gate[0], exp_w_up[0], exp_w_down[0])
    out = _combine(ys, sel_flat, rank_flat, pad_start, wts, h2, x1, sh_w_gate[0].astype(BF16),
                   sh_w_up[0].astype(BF16), sh_w_down[0].astype(BF16), row(norm_post_ffn[0]), g2)
    return out.reshape(x.shape)
```

```python
import functools

import jax
import jax.numpy as jnp
import numpy as np
from jax import lax
from jax.experimental import pallas as pl
from jax.experimental.pallas import tpu as pltpu

F32 = jnp.float32
BF16 = jnp.bfloat16

D_MODEL = 1024
GRID_W = 64
NORM_EPS = 1e-6

A_HEAD_DIM = 64
A_WIDTH = D_MODEL // 2
A_HEADS = A_WIDTH // A_HEAD_DIM
W_LORA = 64
ICLR_LORA = 64
G_LORA = 128
RWKV_GN_EPS = 64e-5

B_HEADS = 4
B_QK_WIDTH = D_MODEL // 2
B_V_WIDTH = D_MODEL
B_QK_DIM = B_QK_WIDTH // B_HEADS
B_V_DIM = B_V_WIDTH // B_HEADS
RET_CHUNK = 128
RET_GN_EPS = 1e-5
ROPE_BASE = 10000.0

SHIFT_COLS = 3 * A_WIDTH + W_LORA + ICLR_LORA + G_LORA
LORA_COLS = W_LORA + ICLR_LORA + G_LORA
SHIFT_PAD = 2048
PROJ_COLS = SHIFT_PAD + 2 * B_QK_WIDTH + 2 * B_V_WIDTH + 2 * D_MODEL

N_EXPERTS = 256
TOP_K = 8
EXPERT_FF = D_MODEL // 4
SHARED_FF = D_MODEL // 4
ROUTED_SCALE = 2.5
MOE_BLOCK = 512
ZERO_PAD_GROUP = 16

RWKV_CHUNK = 64
RWKV_STATE_SHAPE = (2, A_HEADS // 2, A_HEAD_DIM, 2 * A_HEAD_DIM)
PROJ_TN = 1024
PROJ_TM = 1024
VMEM_LIMIT = 48 * 1024 * 1024


def _params(sem):
    return pltpu.CompilerParams(dimension_semantics=sem, vmem_limit_bytes=VMEM_LIMIT)


def _bf(a):
    return a.astype(BF16)


def _bdot(a, b):
    return jnp.dot(a, b, preferred_element_type=F32)


def _dot_nt(a, b):
    return lax.dot_general(a, b, (((1,), (1,)), ((), ())), preferred_element_type=F32)


def _dot_tn(a, b):
    return lax.dot_general(a, b, (((0,), (0,)), ((), ())), preferred_element_type=F32)


def _split(a, n):
    out = []
    rem = a
    for _ in range(n):
        p = _bf(rem)
        out.append(p)
        rem = rem - p.astype(F32)
    return out


def _dot_split_lhs(a, b_bf, n=3):
    acc = None
    for p in _split(a, n):
        t = _bdot(p, b_bf)
        acc = t if acc is None else acc + t
    return acc


def _dot_split_rhs(a_bf, b, n=3):
    acc = None
    for p in _split(b, n):
        t = _bdot(a_bf, p)
        acc = t if acc is None else acc + t
    return acc


def _dot3(a, b):
    ah, al = _split(a, 2)
    bh, bl = _split(b, 2)
    return _bdot(ah, bh) + (_bdot(ah, bl) + _bdot(al, bh))


def _sigmoid(x):
    return 1.0 / (1.0 + jnp.exp(-x))


def _softplus(x):
    return jnp.maximum(x, 0.0) + jnp.log1p(jnp.exp(-jnp.abs(x)))


def _rms(x):
    return x * lax.rsqrt(jnp.mean(x * x, axis=-1, keepdims=True) + NORM_EPS)


LANES = 128


PACK_TILE = D_MODEL // LANES // 2
HIGH_HALF = 0xFFFF0000


def _row_tile(ref, r, lead=()):
    return ref.at[lead + (pl.ds(pl.multiple_of(r * PACK_TILE, PACK_TILE), PACK_TILE),)]


def _rows_to_packed_tiles(x2d, ref):
    n = x2d.shape[0]
    half = D_MODEL // 2
    for j in range(PACK_TILE):
        lo = _bf(x2d[:, j * LANES:(j + 1) * LANES]).astype(F32)
        hi = _bf(x2d[:, half + j * LANES:half + (j + 1) * LANES]).astype(F32)
        word = (pltpu.bitcast(hi, jnp.uint32) & jnp.uint32(HIGH_HALF)) | (pltpu.bitcast(lo, jnp.uint32) >> 16)
        ref[pl.ds(j, n, stride=PACK_TILE), :] = word


def _packed_tiles_to_rows(ref, n, lead=(), dtype=BF16):
    lo, hi = [], []
    for j in range(PACK_TILE):
        word = ref[lead + (pl.ds(j, n, stride=PACK_TILE), slice(None))]
        lo.append(pltpu.bitcast(word << 16, F32).astype(dtype))
        hi.append(pltpu.bitcast(word & jnp.uint32(HIGH_HALF), F32).astype(dtype))
    return jnp.concatenate(lo + hi, axis=1)


def _mod_kernel(cs_ref, w_ref, b_ref, o_ref):
    cs = cs_ref[...]
    s = cs * _sigmoid(cs)
    o_ref[...] = _dot3(s, w_ref[...]) + b_ref[...]


def _modulation(cs, w_mod, b_mod):
    n_out = w_mod.shape[1]
    tn = 1536
    return pl.pallas_call(
        _mod_kernel,
        out_shape=jax.ShapeDtypeStruct((8, n_out), F32),
        grid=(n_out // tn,),
        in_specs=[
            pl.BlockSpec((8, D_MODEL), lambda j: (0, 0)),
            pl.BlockSpec((D_MODEL, tn), lambda j: (0, j)),
            pl.BlockSpec((1, tn), lambda j: (0, j)),
        ],
        out_specs=pl.BlockSpec((8, tn), lambda j: (0, j)),
        compiler_params=_params(("arbitrary",)),
        name="modulation",
    )(cs, w_mod, b_mod.reshape(1, n_out))


W_PAD_TN = SHIFT_PAD - SHIFT_COLS


def _w_in_layout_kernel(w_ref, o_ref, *, gap_tile):
    w = w_ref[...]
    o_ref[...] = _bf(jnp.where(pl.program_id(0) == gap_tile, jnp.zeros_like(w), w))


def _w_in_layout(w_in):
    tn = W_PAD_TN
    gap_tile = SHIFT_COLS // tn
    kern = functools.partial(_w_in_layout_kernel, gap_tile=gap_tile)
    src = lambda j: (0, jnp.where(j < gap_tile, j, jnp.maximum(j - 1, 0)))
    return pl.pallas_call(
        kern,
        out_shape=jax.ShapeDtypeStruct((D_MODEL, PROJ_COLS), BF16),
        grid=(PROJ_COLS // tn,),
        in_specs=[pl.BlockSpec((D_MODEL, tn), src)],
        out_specs=pl.BlockSpec((D_MODEL, tn), lambda j: (0, j)),
        compiler_params=_params(("arbitrary",)),
        name="w_in_layout",
    )(w_in)


def _in_proj_kernel(x_ref, xp_ref, xn_ref, g_ref, sc_ref, sh_ref, w_ref, mu_ref, o_ref, ob_ref, h_s, hh_s,
                    *, tm, n_shift):
    i = pl.program_id(0)
    j = pl.program_id(1)
    n_i = pl.num_programs(0)

    def norm_mod(xv):
        return (_rms(xv) * g_ref[...]) * (1.0 + sc_ref[...]) + sh_ref[...]

    @pl.when(j == 0)
    def _():
        h_s[...] = _bf(norm_mod(x_ref[...]))
        hp = jnp.where(i > 0, norm_mod(xp_ref[...]), 0.0)
        hn = jnp.where(i < n_i - 1, norm_mod(xn_ref[...]), 0.0)
        hh_s[0:8, :] = hp
        hh_s[8:16, :] = hn

    @pl.when(j < n_shift)
    def _():
        w = w_ref[...]
        p = _bdot(h_s[...], w)
        ph = _bdot(_bf(hh_s[...]), w)
        row = lax.broadcasted_iota(jnp.int32, p.shape, 0)
        prev = jnp.where(row == 0, ph[7:8, :], pltpu.roll(p, 1, 0))
        nxt = jnp.where(row == tm - 1, ph[8:9, :], pltpu.roll(p, tm - 1, 0))
        mu = mu_ref[...]
        o_ref[...] = p + mu[0:1, :] * (prev - p) + mu[1:2, :] * (nxt - p)

    @pl.when(j >= n_shift)
    def _():
        ob_ref[...] = _bf(_bdot(h_s[...], w_ref[...]))


def _in_proj(x2d, g, sc, sh, w_bf, mu_pad):
    t_len = x2d.shape[0]
    tm = min(t_len, PROJ_TM)
    tn = PROJ_TN
    n_shift = SHIFT_PAD // tn
    tb8 = tm // 8
    nb8 = t_len // 8
    kern = functools.partial(_in_proj_kernel, tm=tm, n_shift=n_shift)
    vec = lambda: pl.BlockSpec((1, D_MODEL), lambda i, j: (0, 0))
    return pl.pallas_call(
        kern,
        out_shape=[jax.ShapeDtypeStruct((t_len, SHIFT_PAD), F32),
                   jax.ShapeDtypeStruct((t_len, PROJ_COLS - SHIFT_PAD), BF16)],
        grid=(t_len // tm, PROJ_COLS // tn),
        in_specs=[
            pl.BlockSpec((tm, D_MODEL), lambda i, j: (i, 0)),
            pl.BlockSpec((8, D_MODEL), lambda i, j: (jnp.maximum(i * tb8 - 1, 0), 0)),
            pl.BlockSpec((8, D_MODEL), lambda i, j: (jnp.minimum((i + 1) * tb8, nb8 - 1), 0)),
            vec(), vec(), vec(),
            pl.BlockSpec((D_MODEL, tn), lambda i, j: (0, j)),
            pl.BlockSpec((2, tn), lambda i, j: (0, jnp.minimum(j, n_shift - 1))),
        ],
        out_specs=[pl.BlockSpec((tm, tn), lambda i, j: (i, jnp.minimum(j, n_shift - 1))),
                   pl.BlockSpec((tm, tn), lambda i, j: (i, jnp.maximum(j - n_shift, 0)))],
        scratch_shapes=[pltpu.VMEM((tm, D_MODEL), BF16), pltpu.VMEM((16, D_MODEL), F32)],
        compiler_params=_params(("arbitrary", "arbitrary")),
        name="in_proj",
    )(x2d, x2d, x2d, g, sc, sh, w_bf, mu_pad)


def _rwkv_prep_kernel(r_ref, k_ref, v_ref, lora_ref, kk_w, ka_w, rk_w, w0_ref, wup_ref, a0_ref, aup_ref, gup_ref,
                      bd_ref, kk_o, lw0_o, lw1_o, kd0_o, kd1_o, b0_o, b1_o, bonus_o, g_o):
    r = r_ref[...]
    k = k_ref[...]
    v = v_ref[...]
    lora = lora_ref[...]
    xw = lora[:, 0:W_LORA]
    xa = lora[:, W_LORA:W_LORA + ICLR_LORA]
    xg = lora[:, W_LORA + ICLR_LORA:]
    bd = bd_ref[...]

    kk = k * kk_w[...]
    nrm = jnp.sqrt(_dot_split_lhs(kk * kk, bd))
    kk = kk / jnp.maximum(nrm, 1e-12)
    kk_o[...] = kk

    tw = jnp.tanh(xw)
    lw_outs = (lw0_o, lw1_o)
    kd_outs = (kd0_o, kd1_o)
    b_outs = (b0_o, b1_o)
    kd_sum = None
    for d in range(2):
        z = w0_ref[d:d + 1, :] + _dot3(tw, wup_ref[d])
        w_log = -_softplus(-z) - 0.5
        lw_outs[d][...] = -jnp.exp(w_log)
        a_d = _sigmoid(a0_ref[d:d + 1, :] + _dot3(xa, aup_ref[d]))
        kd = k * (1.0 + (a_d - 1.0) * ka_w[...])
        kd_outs[d][...] = kd
        b_outs[d][...] = kk * a_d
        kd_sum = kd if kd_sum is None else kd_sum + kd
    g_o[...] = _dot3(_sigmoid(xg), gup_ref[...])
    bonus_o[...] = _dot_split_lhs(r * kd_sum * rk_w[...], bd) * v


def _rwkv_prep(proj, lp, bd):
    t_len = proj.shape[0]
    tm = min(t_len, 512)
    aw = A_WIDTH
    col = lambda c: pl.BlockSpec((tm, aw), lambda i: (i, c))
    vec = lambda: pl.BlockSpec((1, aw), lambda i: (0, 0))
    full = lambda shp: pl.BlockSpec(shp, lambda i: (0,) * len(shp))
    outs = [jax.ShapeDtypeStruct((t_len, aw), F32)] * 9
    return pl.pallas_call(
        _rwkv_prep_kernel,
        out_shape=outs,
        grid=(t_len // tm,),
        in_specs=[
            col(0), col(1), col(2),
            pl.BlockSpec((tm, LORA_COLS), lambda i: (i, 3 * aw // LORA_COLS)),
            vec(), vec(), vec(),
            full((2, aw)), full((2, W_LORA, aw)), full((2, aw)), full((2, ICLR_LORA, aw)), full((G_LORA, aw)),
            full((aw, aw)),
        ],
        out_specs=[pl.BlockSpec((tm, aw), lambda i: (i, 0))] * 9,
        compiler_params=_params(("arbitrary",)),
        name="rwkv_prep",
    )(proj, proj, proj, proj, lp["k_k"], lp["k_a"], lp["r_k"], lp["w0"], lp["w_up"], lp["a0"], lp["a_up"],
      lp["g_up"], bd)


def _rwkv_scan_kernel(rf, vf, kkf, lwf, kdf, bf_, rb, vb, kkb, lwb, kdb, bb, s0_ref, of_ref, ob_ref, sfin_ref, s_s,
                      *, tb):
    step = pl.program_id(0)

    @pl.when(step == 0)
    def _():
        s_s[...] = s0_ref[...]

    c = RWKV_CHUNK
    n = A_HEAD_DIM
    assert c == n
    pw = 2 * n
    row = lax.broadcasted_iota(jnp.int32, (c, pw), 0)
    lane = lax.broadcasted_iota(jnp.int32, (c, pw), 1)
    col = lane % c
    first = lane < n
    eye = (row == col).astype(F32)
    r_c = lax.broadcasted_iota(jnp.int32, (c, c), 0)
    c_c = lax.broadcasted_iota(jnp.int32, (c, c), 1)
    n_ch = tb // c
    dirs = ((rf, vf, kkf, lwf, kdf, bf_, of_ref), (rb, vb, kkb, lwb, kdb, bb, ob_ref))
    pairs = range(A_HEADS // 2)
    psl = [slice(q * pw, (q + 1) * pw) for q in pairs]

    def bd(x):
        z = jnp.zeros_like(x)
        return jnp.concatenate([jnp.where(first, x, z), jnp.where(first, z, x)], axis=0)

    def chunk_terms(d, refs, ci):
        r_ref, v_ref, kk_ref, lw_ref, kd_ref, b_ref, _ = refs
        if d == 0:
            incl, strict, tri = row >= col, row > col, r_c >= c_c
        else:
            incl, strict, tri = row <= col, row < col, r_c <= c_c
        rows = slice(ci * c, (ci + 1) * c)
        lw = lw_ref[rows, :]
        r = r_ref[rows, :]
        kk = kk_ref[rows, :]
        kd = kd_ref[rows, :]
        b = b_ref[rows, :]
        cl = _dot_split_rhs(tri.astype(BF16), lw)
        cl_tot = cl[c - 1:c, :] if d == 0 else cl[0:1, :]
        e_neg = jnp.exp(-cl)
        e_end = jnp.exp(cl_tot - cl)
        g_tot = jnp.exp(cl_tot)
        rt = r * jnp.exp(cl)
        x_all = _bf(jnp.concatenate([-(kk * jnp.exp(cl - lw)), rt], axis=0))
        bt_all = _bf(b * e_neg)
        kt_all = _bf(kd * e_neg)
        return {
            "rows": rows, "strict": strict, "incl": incl,
            "x": [x_all[:, ps] for ps in psl],
            "z": [jnp.concatenate([bd(bt_all[:, ps]), bd(kt_all[:, ps])], axis=0) for ps in psl],
            "bd_bh": [bd(_bf(b * e_end)[:, ps]) for ps in psl],
            "bd_kh": [bd(_bf(kd * e_end)[:, ps]) for ps in psl],
            "bd_v": [bd(_bf(v_ref[rows, :])[:, ps]) for ps in psl],
            "rt": [rt[:, ps] for ps in psl],
            "g_tot": [g_tot[:, ps] for ps in psl],
        }

    chunks = [(d, cc) for cc in range(n_ch) for d in range(2)]
    prep = {(d, cc): chunk_terms(d, dirs[d], cc if d == 0 else n_ch - 1 - cc) for d, cc in chunks}
    units = [(k, q) for k in chunks for q in pairs]

    def per_unit(name, indexed=True):
        return [prep[k][name][q] if indexed else prep[k][name] for k, q in units]

    strict, incl = per_unit("strict", False), per_unit("incl", False)
    x_u, bd_bh, bd_kh, bd_v, rt_u = (per_unit(s) for s in ("x", "bd_bh", "bd_kh", "bd_v", "rt"))
    g = [_dot_nt(x, z) for x, z in zip(x_u, per_unit("z"))]
    a_ab = [jnp.where(s, gi[:c, :pw], 0.0) for s, gi in zip(strict, g)]
    a_rb = [_bf(jnp.where(i, gi[c:, :pw], 0.0)) for i, gi in zip(incl, g)]
    a_k = [_bf(jnp.concatenate([jnp.where(s, gi[:c, pw:], 0.0), jnp.where(i, gi[c:, pw:], 0.0)], axis=0))
           for s, i, gi in zip(strict, incl, g)]
    a2 = [_bdot(_bf(a), bd(_bf(a))) for a in a_ab]
    pa = [jnp.concatenate([eye + a, sq], axis=0) for a, sq in zip(a_ab, a2)]
    for _ in range(4):
        nxt = [_bdot(_bf(x), bd(_bf(x[c:]))) for x in pa]
        pa = [jnp.concatenate([x[:c] + y[:c], y[c:]], axis=0) for x, y in zip(pa, nxt)]
    p = [_bf(x[:c] + _bdot(_bf(x[:c]), bd(_bf(x[c:])))) for x in pa]
    vk = [_bdot(a, v) for a, v in zip(a_k, bd_v)]
    tw = [_bf(_bdot(pi, jnp.concatenate([bd(x[:c]), bd(_bf(vki[:c]))], axis=1)))
          for pi, vki, x in zip(p, vk, x_u)]
    bd_at = [bd(t[:, :pw]) for t in tw]
    bd_w = [bd(t[:, pw:]) for t in tw]
    mt = [_bf(_dot_tn(a, bh)) for a, bh in zip(bd_at, bd_bh)]
    ntf = [_dot_tn(jnp.concatenate([w, v], axis=0), jnp.concatenate([bh, kh], axis=0))
           for w, v, bh, kh in zip(bd_w, bd_v, bd_bh, bd_kh)]
    nt = [m[:n] + m[n:] for m in ntf]
    rw = [_bdot(a, jnp.concatenate([at, w], axis=1)) for a, at, w in zip(a_rb, bd_at, bd_w)]
    ry = [_bf(r + w[:, :pw]) for w, r in zip(rw, rt_u)]
    y0 = [w[:, pw:] + vki[c:] for w, vki in zip(rw, vk)]
    term = {u: vals for u, vals in zip(units, zip(mt, nt, ry, y0, per_unit("g_tot")))}

    state = [[s_s[d, q] for q in pairs] for d in range(2)]
    dq = [(d, q) for d in range(2) for q in pairs]
    for cc in range(n_ch):
        mt_c, nt_c, ry_c, y0_c, gt_c = zip(*[term[((d, cc), q)] for d, q in dq])
        s0 = [state[d][q] for d, q in dq]
        ys = [_dot_nt(ry_c[i], bd(_bf(s0[i]))) + y0_c[i] for i in range(len(dq))]
        upd = [_bdot(jnp.concatenate(_split(s0[i], 2), axis=0), mt_c[i]) for i in range(len(dq))]
        for i, (d, q) in enumerate(dq):
            state[d][q] = s0[i] * gt_c[i] + (upd[i][:n] + upd[i][n:]) + nt_c[i]
        for d in range(2):
            y_d = [ys[i] for i, (dd, _) in enumerate(dq) if dd == d]
            dirs[d][6][prep[(d, cc)]["rows"], :] = jnp.concatenate(y_d, axis=1)
    for d in range(2):
        for q in pairs:
            s_s[d, q] = state[d][q]

    @pl.when(step == pl.num_programs(0) - 1)
    def _():
        sfin_ref[...] = s_s[...]


def _rwkv_scan(proj, prep, s0):
    kk, lw0, lw1, kd0, kd1, b0, b1 = prep[:7]
    t_len = proj.shape[0]
    tb = 4 * RWKV_CHUNK
    nb = t_len // tb
    aw = A_WIDTH
    fwd = lambda c: pl.BlockSpec((tb, aw), lambda i: (i, c))
    bwd = lambda c: pl.BlockSpec((tb, aw), lambda i: (nb - 1 - i, c))
    st = pl.BlockSpec(RWKV_STATE_SHAPE, lambda i: (0, 0, 0, 0))
    kern = functools.partial(_rwkv_scan_kernel, tb=tb)
    return pl.pallas_call(
        kern,
        out_shape=[jax.ShapeDtypeStruct((t_len, aw), F32), jax.ShapeDtypeStruct((t_len, aw), F32),
                   jax.ShapeDtypeStruct(RWKV_STATE_SHAPE, F32)],
        grid=(nb,),
        in_specs=[fwd(0), fwd(2), fwd(0), fwd(0), fwd(0), fwd(0),
                  bwd(0), bwd(2), bwd(0), bwd(0), bwd(0), bwd(0), st],
        out_specs=[fwd(0), bwd(0), st],
        scratch_shapes=[pltpu.VMEM(RWKV_STATE_SHAPE, F32)],
        compiler_params=_params(("arbitrary",)),
        name="rwkv_scan",
    )(proj, proj, kk, lw0, kd0, b0, proj, proj, kk, lw1, kd1, b1, s0)


def _retention_kernel(qf, kf, vf, cosf, sinf, qb, kb, vb, cosb, sinb, lgt_ref, r0_ref, yf_ref, yb_ref, rfin_ref, r_s):
    step = pl.program_id(0)

    @pl.when(step == 0)
    def _():
        r_s[...] = r0_ref[...]

    c = RET_CHUNK
    dk = B_QK_DIM
    dv = B_V_DIM
    lg_all = -_softplus(-lgt_ref[...])
    rowf = lax.broadcasted_iota(jnp.int32, (c, c), 0).astype(F32)
    colf = lax.broadcasted_iota(jnp.int32, (c, c), 1).astype(F32)
    lane = lax.broadcasted_iota(jnp.int32, (c, dk), 1)
    first_half = (lane % 64) < 32
    dirs = ((qf, kf, vf, cosf, sinf, yf_ref), (qb, kb, vb, cosb, sinb, yb_ref))
    for d, (q_ref, k_ref, v_ref, cos_ref, sin_ref, y_ref) in enumerate(dirs):
        cos = cos_ref[...]
        sin = sin_ref[...]
        diff = (rowf - colf) if d == 0 else (colf - rowf)
        pos = rowf if d == 0 else (c - 1.0) - rowf
        for h in range(B_HEADS):
            lg = lg_all[d * B_HEADS + h:d * B_HEADS + h + 1, :]
            qh = q_ref[:, h * dk:(h + 1) * dk].astype(F32)
            kh = k_ref[:, h * dk:(h + 1) * dk].astype(F32) * (dk ** -0.5)

            def rope(xv):
                swapped = jnp.where(first_half, pltpu.roll(xv, dk - 32, 1), pltpu.roll(xv, 32, 1))
                return xv * cos + swapped * sin

            qh = rope(qh)
            kh = rope(kh)
            vh = v_ref[:, h * dv:(h + 1) * dv]
            dmask = jnp.where(diff >= 0.0, jnp.exp(lg * jnp.maximum(diff, 0.0)), 0.0)
            scores = _dot_nt(_bf(qh), _bf(kh)) * dmask
            inner = _bdot(_bf(scores), vh)
            xi = jnp.exp(lg * (pos + 1.0))
            zeta = jnp.exp(lg * ((c - 1.0) - pos))
            r_prev = r_s[d, h]
            cross = _bdot(_bf(qh * xi), _bf(r_prev))
            y_ref[:, h * dv:(h + 1) * dv] = inner + cross
            kv = _dot_tn(_bf(kh * zeta), vh)
            g_chunk = jnp.exp(lg * float(c))
            r_s[d, h] = jnp.concatenate([g_chunk, g_chunk], axis=1) * r_prev + kv

    @pl.when(step == pl.num_programs(0) - 1)
    def _():
        rfin_ref[...] = r_s[...]


def _retention(proj, cos, sin, lgt, r0):
    t_len = proj.shape[0]
    c = RET_CHUNK
    nc = t_len // c
    qw = B_QK_WIDTH
    vw = B_V_WIDTH
    fq = lambda col: pl.BlockSpec((c, qw), lambda i: (i, col))
    bq = lambda col: pl.BlockSpec((c, qw), lambda i: (nc - 1 - i, col))
    st = pl.BlockSpec((2, B_HEADS, B_QK_DIM, B_V_DIM), lambda i: (0, 0, 0, 0))
    q_col = 0
    v_col = 2 * qw // vw
    return pl.pallas_call(
        _retention_kernel,
        out_shape=[jax.ShapeDtypeStruct((t_len, vw), F32), jax.ShapeDtypeStruct((t_len, vw), F32),
                   jax.ShapeDtypeStruct((2, B_HEADS, B_QK_DIM, B_V_DIM), F32)],
        grid=(nc,),
        in_specs=[
            fq(q_col), fq(q_col + 1), pl.BlockSpec((c, vw), lambda i: (i, v_col)),
            pl.BlockSpec((c, B_QK_DIM), lambda i: (i, 0)), pl.BlockSpec((c, B_QK_DIM), lambda i: (i, 0)),
            bq(q_col), bq(q_col + 1), pl.BlockSpec((c, vw), lambda i: (nc - 1 - i, v_col)),
            pl.BlockSpec((c, B_QK_DIM), lambda i: (nc - 1 - i, 0)),
            pl.BlockSpec((c, B_QK_DIM), lambda i: (nc - 1 - i, 0)),
            pl.BlockSpec((2 * B_HEADS, 128), lambda i: (0, 0)), st,
        ],
        out_specs=[pl.BlockSpec((c, vw), lambda i: (i, 0)), pl.BlockSpec((c, vw), lambda i: (nc - 1 - i, 0)), st],
        scratch_shapes=[pltpu.VMEM((2, B_HEADS, B_QK_DIM, B_V_DIM), F32)],
        compiler_params=_params(("arbitrary",)),
        name="retention",
    )(proj, proj, proj, cos, sin, proj, proj, proj, cos, sin, lgt, r0)


def _mix_out_kernel(of_ref, ob_ref, bonus_ref, g_ref, yf_ref, yb_ref, gb_ref, ga_ref, gbb_ref, x_ref,
                    alnw, alnb, rlnw, rlnb, npm, npf, g1_ref, sc2_ref, sh2_ref, wa_ref, wb_ref, wo_ref, bd_ref,
                    x1_ref, h2_ref, h3_ref):
    bd = bd_ref[...]
    o = of_ref[...] + ob_ref[...]
    mu = _dot_split_lhs(o, bd) * (1.0 / A_HEAD_DIM)
    oc = o - mu
    var = _dot_split_lhs(oc * oc, bd) * (1.0 / A_HEAD_DIM)
    ya = oc * lax.rsqrt(var + RWKV_GN_EPS) * alnw[...] + alnb[...] + bonus_ref[...]
    ya = _bdot(_bf(ya * g_ref[...]), wa_ref[...])

    y = yf_ref[...] + yb_ref[...]
    parts = []
    for h in range(B_HEADS):
        seg = y[:, h * B_V_DIM:(h + 1) * B_V_DIM]
        m = jnp.mean(seg, axis=-1, keepdims=True)
        sc = seg - m
        vr = jnp.mean(sc * sc, axis=-1, keepdims=True)
        parts.append(sc * lax.rsqrt(vr + RET_GN_EPS))
    yn = jnp.concatenate(parts, axis=1) * rlnw[...] + rlnb[...]
    gb = gb_ref[...].astype(F32)
    yb = _bdot(_bf(yn * (gb * _sigmoid(gb))), wb_ref[...])

    merged = _sigmoid(ga_ref[...].astype(F32)) * ya + _sigmoid(gbb_ref[...].astype(F32)) * yb
    mix = _bdot(_bf(merged), wo_ref[...])
    x1 = x_ref[...] + g1_ref[...] * (_rms(mix) * npm[...])
    x1_ref[...] = x1
    h2 = (_rms(x1) * npf[...]) * (1.0 + sc2_ref[...]) + sh2_ref[...]
    h2_ref[...] = h2
    _rows_to_packed_tiles(h2, h3_ref)


def _mix_out(x2d, proj, o_f, o_b, bonus, g, y_f, y_b, lp, vecs, bd):
    t_len = x2d.shape[0]
    tm = min(t_len, 256)
    aw = A_WIDTH
    d = D_MODEL
    ta = lambda: pl.BlockSpec((tm, aw), lambda i: (i, 0))
    td = lambda: pl.BlockSpec((tm, d), lambda i: (i, 0))
    pc = lambda c: pl.BlockSpec((tm, d), lambda i: (i, c))
    va = lambda: pl.BlockSpec((1, aw), lambda i: (0, 0))
    vd = lambda: pl.BlockSpec((1, d), lambda i: (0, 0))
    full = lambda shp: pl.BlockSpec(shp, lambda i: (0, 0))
    gcol = (2 * B_QK_WIDTH + B_V_WIDTH) // d
    return pl.pallas_call(
        _mix_out_kernel,
        out_shape=[jax.ShapeDtypeStruct((t_len, d), F32), jax.ShapeDtypeStruct((t_len, d), F32),
                   jax.ShapeDtypeStruct((t_len * PACK_TILE, LANES), jnp.uint32)],
        grid=(t_len // tm,),
        in_specs=[ta(), ta(), ta(), ta(), td(), td(), pc(gcol), pc(gcol + 1), pc(gcol + 2), td(),
                  va(), va(), vd(), vd(), vd(), vd(), vd(), vd(), vd(),
                  full((aw, d)), full((d, d)), full((d, d)), full((aw, aw))],
        out_specs=[td(), td(), pl.BlockSpec((tm * PACK_TILE, LANES), lambda i: (i, 0))],
        compiler_params=_params(("arbitrary",)),
        name="mix_out",
    )(o_f, o_b, bonus, g, y_f, y_b, proj, proj, proj, x2d,
      lp["ln_w"], lp["ln_b"], lp["ret_ln_w"], lp["ret_ln_b"], vecs["npm"], vecs["npf"], vecs["g1"], vecs["sc2"],
      vecs["sh2"], lp["w_a"], lp["w_b"], lp["w_o"], bd)


def _router_kernel(h_ref, rw_ref, bias_ref, sel_ref, wts_ref, rank_ref, cnt_ref, cnt_s, *, tm):
    step = pl.program_id(0)

    @pl.when(step == 0)
    def _():
        cnt_s[...] = jnp.zeros_like(cnt_s)

    ne = N_EXPERTS
    scores = _sigmoid(_dot3(h_ref[...], rw_ref[...]))
    work = scores + bias_ref[...]
    lane = lax.broadcasted_iota(jnp.int32, (tm, ne), 1).astype(F32)
    idxs = []
    vals = []
    for _ in range(TOP_K):
        m = jnp.max(work, axis=-1, keepdims=True)
        idx = jnp.min(jnp.where(work == m, lane, float(ne)), axis=-1, keepdims=True)
        oh = lane == idx
        vals.append(jnp.sum(jnp.where(oh, scores, 0.0), axis=-1, keepdims=True))
        idxs.append(idx)
        work = jnp.where(oh, -jnp.inf, work)
    sel_f = jnp.concatenate(idxs, axis=1)
    s_sel = jnp.concatenate(vals, axis=1)
    wts_ref[...] = s_sel / jnp.sum(s_sel, axis=1, keepdims=True) * ROUTED_SCALE
    sel_ref[...] = sel_f.astype(jnp.int32)

    hit = work == -jnp.inf
    r_i = lax.broadcasted_iota(jnp.int32, (tm, tm), 0)
    c_i = lax.broadcasted_iota(jnp.int32, (tm, tm), 1)
    before = _bdot((r_i > c_i).astype(BF16), hit.astype(BF16)) + cnt_s[...]
    ranks = [jnp.sum(jnp.where(lane == idxs[k], before, 0.0), axis=-1, keepdims=True) for k in range(TOP_K)]
    rank_ref[...] = jnp.concatenate(ranks, axis=1).astype(jnp.int32)
    cnt = cnt_s[...] + jnp.sum(hit.astype(F32), axis=0, keepdims=True)
    cnt_s[...] = cnt
    cnt_ref[...] = cnt.astype(jnp.int32)


def _router(h2, router_w, router_bias):
    t_len = h2.shape[0]
    tm = min(t_len, 256)
    kern = functools.partial(_router_kernel, tm=tm)
    tk = lambda: pl.BlockSpec((tm, TOP_K), lambda i: (i, 0))
    return pl.pallas_call(
        kern,
        out_shape=[jax.ShapeDtypeStruct((t_len, TOP_K), jnp.int32), jax.ShapeDtypeStruct((t_len, TOP_K), F32),
                   jax.ShapeDtypeStruct((t_len, TOP_K), jnp.int32), jax.ShapeDtypeStruct((1, N_EXPERTS), jnp.int32)],
        grid=(t_len // tm,),
        in_specs=[pl.BlockSpec((tm, D_MODEL), lambda i: (i, 0)),
                  pl.BlockSpec((D_MODEL, N_EXPERTS), lambda i: (0, 0)),
                  pl.BlockSpec((1, N_EXPERTS), lambda i: (0, 0))],
        out_specs=[tk(), tk(), tk(), pl.BlockSpec((1, N_EXPERTS), lambda i: (0, 0))],
        scratch_shapes=[pltpu.VMEM((1, N_EXPERTS), F32)],
        compiler_params=_params(("arbitrary",)),
        name="router",
    )(h2, router_w, router_bias.reshape(1, N_EXPERTS))


def _slot(sel_ref, rank_ref, start_ref, a):
    return start_ref[sel_ref[a]] + rank_ref[a]


def _dispatch_kernel(sel_ref, rank_ref, start_ref, cnt_ref, h_ref, xs_out, zrow, sem, zsem, *, tm):
    def row_copy(t, k):
        slot = _slot(sel_ref, rank_ref, start_ref, t * TOP_K + k)
        return pltpu.make_async_copy(_row_tile(h_ref, t), _row_tile(xs_out, slot), sem)

    def issue(t, carry):
        for k in range(TOP_K):
            row_copy(t, k).start(priority=k % 2)
        return carry

    lax.fori_loop(0, tm, issue, 0)

    def drain(t, carry):
        for k in range(TOP_K):
            row_copy(t, k).wait()
        return carry

    lax.fori_loop(0, tm, drain, 0)

    @pl.when(pl.program_id(0) == pl.num_programs(0) - 1)
    def _():
        zrow[...] = jnp.zeros_like(zrow)

        def for_pad_runs(e0, fn):
            def per_expert(e, carry):
                n = cnt_ref[e]
                n_pad = (n + MOE_BLOCK - 1) // MOE_BLOCK * MOE_BLOCK
                pad = n_pad - n
                off = start_ref[e] + n
                for bit in reversed(range(MOE_BLOCK.bit_length() - 1)):
                    size = 1 << bit
                    is_set = ((pad >> bit) & 1) == 1

                    @pl.when(is_set)
                    def _(off=off, size=size):
                        dst = xs_out.at[pl.ds(pl.multiple_of(off * PACK_TILE, PACK_TILE), size * PACK_TILE)]
                        fn(pltpu.make_async_copy(zrow.at[pl.ds(0, size * PACK_TILE)], dst, zsem))

                    off = off + jnp.where(is_set, size, 0)
                return carry

            lax.fori_loop(e0, e0 + ZERO_PAD_GROUP, per_expert, 0)

        def per_group(gi, carry):
            e0 = gi * ZERO_PAD_GROUP
            for_pad_runs(e0, lambda cp: cp.start())
            for_pad_runs(e0, lambda cp: cp.wait())
            return carry

        lax.fori_loop(0, N_EXPERTS // ZERO_PAD_GROUP, per_group, 0)


def _dispatch(h3, sel_flat, rank_flat, pad_start, counts, n_slots):
    t_len = h3.shape[0] // PACK_TILE
    tm = min(t_len, 256)
    kern = functools.partial(_dispatch_kernel, tm=tm)
    smem_blk = lambda: pl.BlockSpec((tm * TOP_K,), lambda i: (i,), memory_space=pltpu.SMEM)
    smem_all = lambda: pl.BlockSpec((N_EXPERTS,), lambda i: (0,), memory_space=pltpu.SMEM)
    return pl.pallas_call(
        kern,
        out_shape=jax.ShapeDtypeStruct((n_slots * PACK_TILE, LANES), jnp.uint32),
        grid=(t_len // tm,),
        in_specs=[smem_blk(), smem_blk(), smem_all(), smem_all(),
                  pl.BlockSpec((tm * PACK_TILE, LANES), lambda i: (i, 0))],
        out_specs=pl.BlockSpec(memory_space=pl.ANY),
        scratch_shapes=[pltpu.VMEM((MOE_BLOCK // 2 * PACK_TILE, LANES), jnp.uint32), pltpu.SemaphoreType.DMA(()),
                        pltpu.SemaphoreType.DMA(())],
        compiler_params=_params(("arbitrary",)),
        name="dispatch",
    )(sel_flat, rank_flat, pad_start, counts, h3)


def _expert_kernel(be_ref, nu_ref, xs_ref, wg_ref, wu_ref, wd_ref, ys_ref, wg_s, wu_s, wd_s):
    b = pl.program_id(0)
    used = b < nu_ref[0]
    new_expert = (b == 0) | (be_ref[b] != be_ref[jnp.maximum(b - 1, 0)])

    @pl.when(used & new_expert)
    def _():
        wg_s[...] = _bf(wg_ref[0])
        wu_s[...] = _bf(wu_ref[0])
        wd_s[...] = _bf(wd_ref[0])

    @pl.when(used)
    def _():
        xb = _packed_tiles_to_rows(xs_ref, MOE_BLOCK)
        gate = _bdot(xb, wg_s[...])
        up = _bdot(xb, wu_s[...])
        act = gate * _sigmoid(gate) * up
        _rows_to_packed_tiles(_bdot(_bf(act), wd_s[...]), ys_ref)

    @pl.when(jnp.logical_not(used))
    def _():
        ys_ref[...] = jnp.zeros_like(ys_ref)


def _experts(xs, block_e, n_used, w_gate, w_up, w_down):
    n_slots = xs.shape[0] // PACK_TILE
    n_blocks = n_slots // MOE_BLOCK
    rows = (MOE_BLOCK * PACK_TILE, LANES)
    grid_spec = pltpu.PrefetchScalarGridSpec(
        num_scalar_prefetch=2,
        grid=(n_blocks,),
        in_specs=[
            pl.BlockSpec((MOE_BLOCK * PACK_TILE, LANES), lambda b, be, nu: (jnp.where(b < nu[0], b, 0), 0)),
            pl.BlockSpec((1, D_MODEL, EXPERT_FF), lambda b, be, nu: (be[b], 0, 0)),
            pl.BlockSpec((1, D_MODEL, EXPERT_FF), lambda b, be, nu: (be[b], 0, 0)),
            pl.BlockSpec((1, EXPERT_FF, D_MODEL), lambda b, be, nu: (be[b], 0, 0)),
        ],
        out_specs=pl.BlockSpec(rows, lambda b, be, nu: (b, 0)),
        scratch_shapes=[pltpu.VMEM((D_MODEL, EXPERT_FF), BF16), pltpu.VMEM((D_MODEL, EXPERT_FF), BF16),
                        pltpu.VMEM((EXPERT_FF, D_MODEL), BF16)],
    )
    return pl.pallas_call(
        _expert_kernel,
        out_shape=jax.ShapeDtypeStruct((n_slots * PACK_TILE, LANES), jnp.uint32),
        grid_spec=grid_spec,
        compiler_params=_params(("arbitrary",)),
        name="experts",
    )(block_e, n_used, xs, w_gate, w_up, w_down)


def _combine_kernel(sel_ref, rank_ref, start_ref, ys_ref, wts_ref, h_ref, x1_ref, sg_ref, su_ref, sd_ref, npo, g2_ref,
                    o_ref, buf, sem, *, tm):
    def row_copy(t, k):
        slot = _slot(sel_ref, rank_ref, start_ref, t * TOP_K + k)
        return pltpu.make_async_copy(_row_tile(ys_ref, slot), _row_tile(buf, t, (k,)), sem)

    def issue(t, carry):
        for k in range(TOP_K):
            row_copy(t, k).start(priority=k % 2)
        return carry

    lax.fori_loop(0, tm, issue, 0)

    hb = _bf(h_ref[...])
    gate = _bdot(hb, sg_ref[...])
    up = _bdot(hb, su_ref[...])
    shared = _bdot(_bf(gate * _sigmoid(gate) * up), sd_ref[...])

    def drain(t, carry):
        for k in range(TOP_K):
            row_copy(t, k).wait()
        return carry

    lax.fori_loop(0, tm, drain, 0)

    wts = wts_ref[...]
    routed = _packed_tiles_to_rows(buf, tm, (0,), F32) * wts[:, 0:1]
    for k in range(1, TOP_K):
        routed = routed + _packed_tiles_to_rows(buf, tm, (k,), F32) * wts[:, k:k + 1]
    o_ref[...] = x1_ref[...] + g2_ref[...] * (_rms(routed + shared) * npo[...])


def _combine(ys, sel_flat, rank_flat, pad_start, wts, h2, x1, sg, su, sd, npo, g2):
    t_len = h2.shape[0]
    tm = min(t_len, 256)
    d = D_MODEL
    kern = functools.partial(_combine_kernel, tm=tm)
    smem_blk = lambda: pl.BlockSpec((tm * TOP_K,), lambda i: (i,), memory_space=pltpu.SMEM)
    td = lambda: pl.BlockSpec((tm, d), lambda i: (i, 0))
    vd = lambda: pl.BlockSpec((1, d), lambda i: (0, 0))
    return pl.pallas_call(
        kern,
        out_shape=jax.ShapeDtypeStruct((t_len, d), F32),
        grid=(t_len // tm,),
        in_specs=[smem_blk(), smem_blk(),
                  pl.BlockSpec((N_EXPERTS,), lambda i: (0,), memory_space=pltpu.SMEM),
                  pl.BlockSpec(memory_space=pl.ANY),
                  pl.BlockSpec((tm, TOP_K), lambda i: (i, 0)),
                  td(), td(),
                  pl.BlockSpec((d, SHARED_FF), lambda i: (0, 0)), pl.BlockSpec((d, SHARED_FF), lambda i: (0, 0)),
                  pl.BlockSpec((SHARED_FF, d), lambda i: (0, 0)), vd(), vd()],
        out_specs=td(),
        scratch_shapes=[pltpu.VMEM((TOP_K, tm * PACK_TILE, LANES), jnp.uint32), pltpu.SemaphoreType.DMA(())],
        compiler_params=_params(("arbitrary",)),
        name="combine",
    )(sel_flat, rank_flat, pad_start, ys, wts, h2, x1, sg, su, sd, npo, g2)


def _block_diag_ones(width, group):
    idx = np.arange(width) // group
    return jnp.asarray(idx[:, None] == idx[None, :], dtype=BF16)


def _rope_tables(t_len):
    pos = np.arange(t_len)
    rows = (pos // GRID_W).astype(np.float32)
    cols = (pos % GRID_W).astype(np.float32)
    quarter = B_QK_DIM // 4
    inv_freq = jnp.asarray(ROPE_BASE, F32) ** (-jnp.arange(quarter, dtype=F32) / quarter)
    ang_r = jnp.asarray(rows)[:, None] * inv_freq
    ang_c = jnp.asarray(cols)[:, None] * inv_freq
    cr, sr, cc, sc = jnp.cos(ang_r), jnp.sin(ang_r), jnp.cos(ang_c), jnp.sin(ang_c)
    return jnp.concatenate([cr, cr, cc, cc], axis=1), jnp.concatenate([-sr, sr, -sc, sc], axis=1)


def _token_mixer(x2d, g_pre, sc, sh, lp, cos, sin, states, bd):
    proj_a, proj_b = _in_proj(x2d, g_pre, sc, sh, lp["w_in"], lp["mu"])
    prep = _rwkv_prep(proj_a, lp, bd)
    o_f, o_b, s_fin = _rwkv_scan(proj_a, prep, states[0])
    y_f, y_b, r_fin = _retention(proj_b, cos, sin, lp["lgt"], states[1])
    return proj_b, prep, (o_f, o_b), (y_f, y_b), (s_fin, r_fin)


def kernel(x, c, ctx, c_ctx, w_mod, b_mod, norm_pre_mix, norm_post_mix, norm_pre_ffn, norm_post_ffn, w_in, shift_mu,
           rwkv_w0, rwkv_w_up, rwkv_a0, rwkv_a_up, rwkv_g_up, rwkv_k_k, rwkv_k_a, rwkv_r_k, rwkv_ln_w, rwkv_ln_b,
           w_branch_a, ret_decay_logit, ret_ln_w, ret_ln_b, w_branch_b, w_out, router_w, router_bias, exp_w_gate,
           exp_w_up, exp_w_down, sh_w_gate, sh_w_up, sh_w_down):
    d = D_MODEL
    assert x.shape[0] == 1 and w_in.shape[0] == 1, "single batch element, single layer"
    t_len = x.shape[1]
    x2d = x.reshape(t_len, d)
    ctx2d = ctx.reshape(ctx.shape[1], d)
    row = lambda a: a.reshape(1, -1)

    cs = jnp.zeros((8, d), F32).at[0].set(c[0]).at[1].set(c_ctx)
    mod = _modulation(cs, w_mod[0], b_mod[0])
    sh1, sc1, g1, sh2, sc2, g2 = [mod[0:1, i * d:(i + 1) * d] for i in range(6)]
    csh1, csc1 = mod[1:2, 0:d], mod[1:2, d:2 * d]

    lp = {
        "w_in": _w_in_layout(w_in[0]),
        "mu": jnp.pad(shift_mu[0], ((0, 0), (0, SHIFT_PAD - SHIFT_COLS))),
        "k_k": row(rwkv_k_k[0]), "k_a": row(rwkv_k_a[0]), "r_k": row(rwkv_r_k[0]),
        "w0": rwkv_w0[0], "w_up": rwkv_w_up[0], "a0": rwkv_a0[0], "a_up": rwkv_a_up[0], "g_up": rwkv_g_up[0],
        "ln_w": row(rwkv_ln_w[0]), "ln_b": row(rwkv_ln_b[0]),
        "ret_ln_w": row(ret_ln_w[0]), "ret_ln_b": row(ret_ln_b[0]),
        "lgt": jnp.broadcast_to(ret_decay_logit[0].reshape(2 * B_HEADS, 1), (2 * B_HEADS, 128)),
        "w_a": w_branch_a[0].astype(BF16), "w_b": w_branch_b[0].astype(BF16), "w_o": w_out[0].astype(BF16),
    }
    bd = _block_diag_ones(A_WIDTH, A_HEAD_DIM)
    g_pre = row(norm_pre_mix[0])

    t_ctx = ctx2d.shape[0]
    zero_states = (jnp.zeros(RWKV_STATE_SHAPE, F32),
                   jnp.zeros((2, B_HEADS, B_QK_DIM, B_V_DIM), F32))
    ones = jnp.ones((t_ctx, B_QK_DIM), F32)
    *_, ctx_states = _token_mixer(ctx2d, g_pre, csc1, csh1, lp, ones, jnp.zeros_like(ones), zero_states, bd)

    cos, sin = _rope_tables(t_len)
    proj, prep, (o_f, o_b), (y_f, y_b), _ = _token_mixer(x2d, g_pre, sc1, sh1, lp, cos, sin, ctx_states, bd)
    vecs = {"npm": row(norm_post_mix[0]), "npf": row(norm_pre_ffn[0]), "g1": g1, "sc2": sc2, "sh2": sh2}
    x1, h2, h3 = _mix_out(x2d, proj, o_f, o_b, prep[7], prep[8], y_f, y_b, lp, vecs, bd)

    sel, wts, rank, counts = _router(h2, router_w[0], router_bias[0])
    counts = counts.reshape(N_EXPERTS)
    padded = (counts + MOE_BLOCK - 1) // MOE_BLOCK * MOE_BLOCK
    pad_end = jnp.cumsum(padded)
    pad_start = (pad_end - padded).astype(jnp.int32)
    n_assign = t_len * TOP_K
    n_blocks = (n_assign + N_EXPERTS * (MOE_BLOCK - 1) + MOE_BLOCK - 1) // MOE_BLOCK
    block_start = jnp.arange(n_blocks, dtype=jnp.int32) * MOE_BLOCK
    block_e = jnp.minimum(jnp.sum(pad_end[None, :] <= block_start[:, None], axis=1), N_EXPERTS - 1).astype(jnp.int32)
    n_used = (pad_end[-1:] // MOE_BLOCK).astype(jnp.int32)
    sel_flat = sel.reshape(n_assign)
    rank_flat = rank.reshape(n_assign)

    xs = _dispatch(h3, sel_flat, rank_flat, pad_start, counts, n_blocks * MOE_BLOCK)
    ys = _experts(xs, block_e, n_used, exp_w_gate[0], exp_w_up[0], exp_w_down[0])
    out = _combine(ys, sel_flat, rank_flat, pad_start, wts, h2, x1, sh_w_gate[0].astype(BF16),
                   sh_w_up[0].astype(BF16), sh_w_down[0].astype(BF16), row(norm_post_ffn[0]), g2)
    return out.reshape(x.shape)
```

```python
import functools

import jax
import jax.numpy as jnp
import numpy as np
from jax import lax
from jax.experimental import pallas as pl
from jax.experimental.pallas import tpu as pltpu

F32 = jnp.float32
BF16 = jnp.bfloat16

D_MODEL = 1024
GRID_W = 64
NORM_EPS = 1e-6

A_HEAD_DIM = 64
A_WIDTH = D_MODEL // 2
A_HEADS = A_WIDTH // A_HEAD_DIM
W_LORA = 64
ICLR_LORA = 64
G_LORA = 128
RWKV_GN_EPS = 64e-5

B_HEADS = 4
B_QK_WIDTH = D_MODEL // 2
B_V_WIDTH = D_MODEL
B_QK_DIM = B_QK_WIDTH // B_HEADS
B_V_DIM = B_V_WIDTH // B_HEADS
RET_CHUNK = 128
RET_GN_EPS = 1e-5
ROPE_BASE = 10000.0

SHIFT_COLS = 3 * A_WIDTH + W_LORA + ICLR_LORA + G_LORA
LORA_COLS = W_LORA + ICLR_LORA + G_LORA
SHIFT_PAD = 2048
PROJ_COLS = SHIFT_PAD + 2 * B_QK_WIDTH + 2 * B_V_WIDTH + 2 * D_MODEL

N_EXPERTS = 256
TOP_K = 8
EXPERT_FF = D_MODEL // 4
SHARED_FF = D_MODEL // 4
ROUTED_SCALE = 2.5
MOE_BLOCK = 512
ZERO_PAD_GROUP = 16

RWKV_CHUNK = 64
RWKV_STATE_SHAPE = (2, A_HEADS // 2, A_HEAD_DIM, 2 * A_HEAD_DIM)
PROJ_TN = 1024
PROJ_TM = 1024
VMEM_LIMIT = 48 * 1024 * 1024


def _params(sem):
    return pltpu.CompilerParams(dimension_semantics=sem, vmem_limit_bytes=VMEM_LIMIT)


def _bf(a):
    return a.astype(BF16)


def _bdot(a, b):
    return jnp.dot(a, b, preferred_element_type=F32)


def _dot_nt(a, b):
    return lax.dot_general(a, b, (((1,), (1,)), ((), ())), preferred_element_type=F32)


def _dot_tn(a, b):
    return lax.dot_general(a, b, (((0,), (0,)), ((), ())), preferred_element_type=F32)


def _split(a, n):
    out = []
    rem = a
    for _ in range(n):
        p = _bf(rem)
        out.append(p)
        rem = rem - p.astype(F32)
    return out


def _dot_split_lhs(a, b_bf, n=3):
    acc = None
    for p in _split(a, n):
        t = _bdot(p, b_bf)
        acc = t if acc is None else acc + t
    return acc


def _dot_split_rhs(a_bf, b, n=3):
    acc = None
    for p in _split(b, n):
        t = _bdot(a_bf, p)
        acc = t if acc is None else acc + t
    return acc


def _dot3(a, b):
    ah, al = _split(a, 2)
    bh, bl = _split(b, 2)
    return _bdot(ah, bh) + (_bdot(ah, bl) + _bdot(al, bh))


def _sigmoid(x):
    return 1.0 / (1.0 + jnp.exp(-x))


def _softplus(x):
    return jnp.maximum(x, 0.0) + jnp.log1p(jnp.exp(-jnp.abs(x)))


def _rms(x):
    return x * lax.rsqrt(jnp.mean(x * x, axis=-1, keepdims=True) + NORM_EPS)


LANES = 128


PACK_TILE = D_MODEL // LANES // 2
HIGH_HALF = 0xFFFF0000


def _row_tile(ref, r, lead=()):
    return ref.at[lead + (pl.ds(pl.multiple_of(r * PACK_TILE, PACK_TILE), PACK_TILE),)]


def _rows_to_packed_tiles(x2d, ref):
    n = x2d.shape[0]
    half = D_MODEL // 2
    for j in range(PACK_TILE):
        lo = _bf(x2d[:, j * LANES:(j + 1) * LANES]).astype(F32)
        hi = _bf(x2d[:, half + j * LANES:half + (j + 1) * LANES]).astype(F32)
        word = (pltpu.bitcast(hi, jnp.uint32) & jnp.uint32(HIGH_HALF)) | (pltpu.bitcast(lo, jnp.uint32) >> 16)
        ref[pl.ds(j, n, stride=PACK_TILE), :] = word


def _packed_tiles_to_rows(ref, n, lead=(), dtype=BF16):
    lo, hi = [], []
    for j in range(PACK_TILE):
        word = ref[lead + (pl.ds(j, n, stride=PACK_TILE), slice(None))]
        lo.append(pltpu.bitcast(word << 16, F32).astype(dtype))
        hi.append(pltpu.bitcast(word & jnp.uint32(HIGH_HALF), F32).astype(dtype))
    return jnp.concatenate(lo + hi, axis=1)


def _mod_kernel(cs_ref, w_ref, b_ref, o_ref):
    cs = cs_ref[...]
    s = cs * _sigmoid(cs)
    o_ref[...] = _dot3(s, w_ref[...]) + b_ref[...]


def _modulation(cs, w_mod, b_mod):
    n_out = w_mod.shape[1]
    tn = 1536
    return pl.pallas_call(
        _mod_kernel,
        out_shape=jax.ShapeDtypeStruct((8, n_out), F32),
        grid=(n_out // tn,),
        in_specs=[
            pl.BlockSpec((8, D_MODEL), lambda j: (0, 0)),
            pl.BlockSpec((D_MODEL, tn), lambda j: (0, j)),
            pl.BlockSpec((1, tn), lambda j: (0, j)),
        ],
        out_specs=pl.BlockSpec((8, tn), lambda j: (0, j)),
        compiler_params=_params(("arbitrary",)),
        name="modulation",
    )(cs, w_mod, b_mod.reshape(1, n_out))


W_PAD_TN = SHIFT_PAD - SHIFT_COLS


def _w_in_layout_kernel(w_ref, o_ref, *, gap_tile):
    w = w_ref[...]
    o_ref[...] = _bf(jnp.where(pl.program_id(0) == gap_tile, jnp.zeros_like(w), w))


def _w_in_layout(w_in):
    tn = W_PAD_TN
    gap_tile = SHIFT_COLS // tn
    kern = functools.partial(_w_in_layout_kernel, gap_tile=gap_tile)
    src = lambda j: (0, jnp.where(j < gap_tile, j, jnp.maximum(j - 1, 0)))
    return pl.pallas_call(
        kern,
        out_shape=jax.ShapeDtypeStruct((D_MODEL, PROJ_COLS), BF16),
        grid=(PROJ_COLS // tn,),
        in_specs=[pl.BlockSpec((D_MODEL, tn), src)],
        out_specs=pl.BlockSpec((D_MODEL, tn), lambda j: (0, j)),
        compiler_params=_params(("arbitrary",)),
        name="w_in_layout",
    )(w_in)


def _in_proj_kernel(x_ref, xp_ref, xn_ref, g_ref, sc_ref, sh_ref, w_ref, mu_ref, o_ref, ob_ref, h_s, hh_s,
                    *, tm, n_shift):
    i = pl.program_id(0)
    j = pl.program_id(1)
    n_i = pl.num_programs(0)

    def norm_mod(xv):
        return (_rms(xv) * g_ref[...]) * (1.0 + sc_ref[...]) + sh_ref[...]

    @pl.when(j == 0)
    def _():
        h_s[...] = _bf(norm_mod(x_ref[...]))
        hp = jnp.where(i > 0, norm_mod(xp_ref[...]), 0.0)
        hn = jnp.where(i < n_i - 1, norm_mod(xn_ref[...]), 0.0)
        hh_s[0:8, :] = hp
        hh_s[8:16, :] = hn

    @pl.when(j < n_shift)
    def _():
        w = w_ref[...]
        p = _bdot(h_s[...], w)
        ph = _bdot(_bf(hh_s[...]), w)
        row = lax.broadcasted_iota(jnp.int32, p.shape, 0)
        prev = jnp.where(row == 0, ph[7:8, :], pltpu.roll(p, 1, 0))
        nxt = jnp.where(row == tm - 1, ph[8:9, :], pltpu.roll(p, tm - 1, 0))
        mu = mu_ref[...]
        o_ref[...] = p + mu[0:1, :] * (prev - p) + mu[1:2, :] * (nxt - p)

    @pl.when(j >= n_shift)
    def _():
        ob_ref[...] = _bf(_bdot(h_s[...], w_ref[...]))


def _in_proj(x2d, g, sc, sh, w_bf, mu_pad):
    t_len = x2d.shape[0]
    tm = min(t_len, PROJ_TM)
    tn = PROJ_TN
    n_shift = SHIFT_PAD // tn
    tb8 = tm // 8
    nb8 = t_len // 8
    kern = functools.partial(_in_proj_kernel, tm=tm, n_shift=n_shift)
    vec = lambda: pl.BlockSpec((1, D_MODEL), lambda i, j: (0, 0))
    return pl.pallas_call(
        kern,
        out_shape=[jax.ShapeDtypeStruct((t_len, SHIFT_PAD), F32),
                   jax.ShapeDtypeStruct((t_len, PROJ_COLS - SHIFT_PAD), BF16)],
        grid=(t_len // tm, PROJ_COLS // tn),
        in_specs=[
            pl.BlockSpec((tm, D_MODEL), lambda i, j: (i, 0)),
            pl.BlockSpec((8, D_MODEL), lambda i, j: (jnp.maximum(i * tb8 - 1, 0), 0)),
            pl.BlockSpec((8, D_MODEL), lambda i, j: (jnp.minimum((i + 1) * tb8, nb8 - 1), 0)),
            vec(), vec(), vec(),
            pl.BlockSpec((D_MODEL, tn), lambda i, j: (0, j)),
            pl.BlockSpec((2, tn), lambda i, j: (0, jnp.minimum(j, n_shift - 1))),
        ],
        out_specs=[pl.BlockSpec((tm, tn), lambda i, j: (i, jnp.minimum(j, n_shift - 1))),
                   pl.BlockSpec((tm, tn), lambda i, j: (i, jnp.maximum(j - n_shift, 0)))],
        scratch_shapes=[pltpu.VMEM((tm, D_MODEL), BF16), pltpu.VMEM((16, D_MODEL), F32)],
        compiler_params=_params(("arbitrary", "arbitrary")),
        name="in_proj",
    )(x2d, x2d, x2d, g, sc, sh, w_bf, mu_pad)


def _rwkv_prep_kernel(r_ref, k_ref, v_ref, lora_ref, kk_w, ka_w, rk_w, w0_ref, wup_ref, a0_ref, aup_ref, gup_ref,
                      bd_ref, kk_o, lw0_o, lw1_o, kd0_o, kd1_o, b0_o, b1_o, bonus_o, g_o):
    r = r_ref[...]
    k = k_ref[...]
    v = v_ref[...]
    lora = lora_ref[...]
    xw = lora[:, 0:W_LORA]
    xa = lora[:, W_LORA:W_LORA + ICLR_LORA]
    xg = lora[:, W_LORA + ICLR_LORA:]
    bd = bd_ref[...]

    kk = k * kk_w[...]
    nrm = jnp.sqrt(_dot_split_lhs(kk * kk, bd))
    kk = kk / jnp.maximum(nrm, 1e-12)
    kk_o[...] = kk

    tw = jnp.tanh(xw)
    lw_outs = (lw0_o, lw1_o)
    kd_outs = (kd0_o, kd1_o)
    b_outs = (b0_o, b1_o)
    kd_sum = None
    for d in range(2):
        z = w0_ref[d:d + 1, :] + _dot3(tw, wup_ref[d])
        w_log = -_softplus(-z) - 0.5
        lw_outs[d][...] = -jnp.exp(w_log)
        a_d = _sigmoid(a0_ref[d:d + 1, :] + _dot3(xa, aup_ref[d]))
        kd = k * (1.0 + (a_d - 1.0) * ka_w[...])
        kd_outs[d][...] = kd
        b_outs[d][...] = kk * a_d
        kd_sum = kd if kd_sum is None else kd_sum + kd
    g_o[...] = _dot3(_sigmoid(xg), gup_ref[...])
    bonus_o[...] = _dot_split_lhs(r * kd_sum * rk_w[...], bd) * v


def _rwkv_prep(proj, lp, bd):
    t_len = proj.shape[0]
    tm = min(t_len, 512)
    aw = A_WIDTH
    col = lambda c: pl.BlockSpec((tm, aw), lambda i: (i, c))
    vec = lambda: pl.BlockSpec((1, aw), lambda i: (0, 0))
    full = lambda shp: pl.BlockSpec(shp, lambda i: (0,) * len(shp))
    outs = [jax.ShapeDtypeStruct((t_len, aw), F32)] * 9
    return pl.pallas_call(
        _rwkv_prep_kernel,
        out_shape=outs,
        grid=(t_len // tm,),
        in_specs=[
            col(0), col(1), col(2),
            pl.BlockSpec((tm, LORA_COLS), lambda i: (i, 3 * aw // LORA_COLS)),
            vec(), vec(), vec(),
            full((2, aw)), full((2, W_LORA, aw)), full((2, aw)), full((2, ICLR_LORA, aw)), full((G_LORA, aw)),
            full((aw, aw)),
        ],
        out_specs=[pl.BlockSpec((tm, aw), lambda i: (i, 0))] * 9,
        compiler_params=_params(("arbitrary",)),
        name="rwkv_prep",
    )(proj, proj, proj, proj, lp["k_k"], lp["k_a"], lp["r_k"], lp["w0"], lp["w_up"], lp["a0"], lp["a_up"],
      lp["g_up"], bd)


def _rwkv_scan_kernel(rf, vf, kkf, lwf, kdf, bf_, rb, vb, kkb, lwb, kdb, bb, s0_ref, of_ref, ob_ref, sfin_ref, s_s,
                      *, tb):
    step = pl.program_id(0)

    @pl.when(step == 0)
    def _():
        s_s[...] = s0_ref[...]

    c = RWKV_CHUNK
    n = A_HEAD_DIM
    assert c == n
    pw = 2 * n
    row = lax.broadcasted_iota(jnp.int32, (c, pw), 0)
    lane = lax.broadcasted_iota(jnp.int32, (c, pw), 1)
    col = lane % c
    first = lane < n
    eye = (row == col).astype(F32)
    r_c = lax.broadcasted_iota(jnp.int32, (c, c), 0)
    c_c = lax.broadcasted_iota(jnp.int32, (c, c), 1)
    n_ch = tb // c
    dirs = ((rf, vf, kkf, lwf, kdf, bf_, of_ref), (rb, vb, kkb, lwb, kdb, bb, ob_ref))
    pairs = range(A_HEADS // 2)
    psl = [slice(q * pw, (q + 1) * pw) for q in pairs]

    def bd(x):
        z = jnp.zeros_like(x)
        return jnp.concatenate([jnp.where(first, x, z), jnp.where(first, z, x)], axis=0)

    def chunk_terms(d, refs, ci):
        r_ref, v_ref, kk_ref, lw_ref, kd_ref, b_ref, _ = refs
        if d == 0:
            incl, strict, tri = row >= col, row > col, r_c >= c_c
        else:
            incl, strict, tri = row <= col, row < col, r_c <= c_c
        rows = slice(ci * c, (ci + 1) * c)
        lw = lw_ref[rows, :]
        r = r_ref[rows, :]
        kk = kk_ref[rows, :]
        kd = kd_ref[rows, :]
        b = b_ref[rows, :]
        cl = _dot_split_rhs(tri.astype(BF16), lw)
        cl_tot = cl[c - 1:c, :] if d == 0 else cl[0:1, :]
        e_neg = jnp.exp(-cl)
        e_end = jnp.exp(cl_tot - cl)
        g_tot = jnp.exp(cl_tot)
        rt = r * jnp.exp(cl)
        x_all = _bf(jnp.concatenate([-(kk * jnp.exp(cl - lw)), rt], axis=0))
        bt_all = _bf(b * e_neg)
        kt_all = _bf(kd * e_neg)
        return {
            "rows": rows, "strict": strict, "incl": incl,
            "x": [x_all[:, ps] for ps in psl],
            "z": [jnp.concatenate([bd(bt_all[:, ps]), bd(kt_all[:, ps])], axis=0) for ps in psl],
            "bd_bh": [bd(_bf(b * e_end)[:, ps]) for ps in psl],
            "bd_kh": [bd(_bf(kd * e_end)[:, ps]) for ps in psl],
            "bd_v": [bd(_bf(v_ref[rows, :])[:, ps]) for ps in psl],
            "rt": [rt[:, ps] for ps in psl],
            "g_tot": [g_tot[:, ps] for ps in psl],
        }

    chunks = [(d, cc) for cc in range(n_ch) for d in range(2)]
    prep = {(d, cc): chunk_terms(d, dirs[d], cc if d == 0 else n_ch - 1 - cc) for d, cc in chunks}
    units = [(k, q) for k in chunks for q in pairs]

    def per_unit(name, indexed=True):
        return [prep[k][name][q] if indexed else prep[k][name] for k, q in units]

    strict, incl = per_unit("strict", False), per_unit("incl", False)
    x_u, bd_bh, bd_kh, bd_v, rt_u = (per_unit(s) for s in ("x", "bd_bh", "bd_kh", "bd_v", "rt"))
    g = [_dot_nt(x, z) for x, z in zip(x_u, per_unit("z"))]
    a_ab = [jnp.where(s, gi[:c, :pw], 0.0) for s, gi in zip(strict, g)]
    a_rb = [_bf(jnp.where(i, gi[c:, :pw], 0.0)) for i, gi in zip(incl, g)]
    a_k = [_bf(jnp.concatenate([jnp.where(s, gi[:c, pw:], 0.0), jnp.where(i, gi[c:, pw:], 0.0)], axis=0))
           for s, i, gi in zip(strict, incl, g)]
    a2 = [_bdot(_bf(a), bd(_bf(a))) for a in a_ab]
    pa = [jnp.concatenate([eye + a, sq], axis=0) for a, sq in zip(a_ab, a2)]
    for _ in range(4):
        nxt = [_bdot(_bf(x), bd(_bf(x[c:]))) for x in pa]
        pa = [jnp.concatenate([x[:c] + y[:c], y[c:]], axis=0) for x, y in zip(pa, nxt)]
    p = [_bf(x[:c] + _bdot(_bf(x[:c]), bd(_bf(x[c:])))) for x in pa]
    vk = [_bdot(a, v) for a, v in zip(a_k, bd_v)]
    tw = [_bf(_bdot(pi, jnp.concatenate([bd(x[:c]), bd(_bf(vki[:c]))], axis=1)))
          for pi, vki, x in zip(p, vk, x_u)]
    bd_at = [bd(t[:, :pw]) for t in tw]
    bd_w = [bd(t[:, pw:]) for t in tw]
    mt = [_bf(_dot_tn(a, bh)) for a, bh in zip(bd_at, bd_bh)]
    ntf = [_dot_tn(jnp.concatenate([w, v], axis=0), jnp.concatenate([bh, kh], axis=0))
           for w, v, bh, kh in zip(bd_w, bd_v, bd_bh, bd_kh)]
    nt = [m[:n] + m[n:] for m in ntf]
    rw = [_bdot(a, jnp.concatenate([at, w], axis=1)) for a, at, w in zip(a_rb, bd_at, bd_w)]
    ry = [_bf(r + w[:, :pw]) for w, r in zip(rw, rt_u)]
    y0 = [w[:, pw:] + vki[c:] for w, vki in zip(rw, vk)]
    term = {u: vals for u, vals in zip(units, zip(mt, nt, ry, y0, per_unit("g_tot")))}

    state = [[s_s[d, q] for q in pairs] for d in range(2)]
    dq = [(d, q) for d in range(2) for q in pairs]
    for cc in range(n_ch):
        mt_c, nt_c, ry_c, y0_c, gt_c = zip(*[term[((d, cc), q)] for d, q in dq])
        s0 = [state[d][q] for d, q in dq]
        ys = [_dot_nt(ry_c[i], bd(_bf(s0[i]))) + y0_c[i] for i in range(len(dq))]
        upd = [_bdot(jnp.concatenate(_split(s0[i], 2), axis=0), mt_c[i]) for i in range(len(dq))]
        for i, (d, q) in enumerate(dq):
            state[d][q] = s0[i] * gt_c[i] + (upd[i][:n] + upd[i][n:]) + nt_c[i]
        for d in range(2):
            y_d = [ys[i] for i, (dd, _) in enumerate(dq) if dd == d]
            dirs[d][6][prep[(d, cc)]["rows"], :] = jnp.concatenate(y_d, axis=1)
    for d in range(2):
        for q in pairs:
            s_s[d, q] = state[d][q]

    @pl.when(step == pl.num_programs(0) - 1)
    def _():
        sfin_ref[...] = s_s[...]


def _rwkv_scan(proj, prep, s0):
    kk, lw0, lw1, kd0, kd1, b0, b1 = prep[:7]
    t_len = proj.shape[0]
    tb = 4 * RWKV_CHUNK
    nb = t_len // tb
    aw = A_WIDTH
    fwd = lambda c: pl.BlockSpec((tb, aw), lambda i: (i, c))
    bwd = lambda c: pl.BlockSpec((tb, aw), lambda i: (nb - 1 - i, c))
    st = pl.BlockSpec(RWKV_STATE_SHAPE, lambda i: (0, 0, 0, 0))
    kern = functools.partial(_rwkv_scan_kernel, tb=tb)
    return pl.pallas_call(
        kern,
        out_shape=[jax.ShapeDtypeStruct((t_len, aw), F32), jax.ShapeDtypeStruct((t_len, aw), F32),
                   jax.ShapeDtypeStruct(RWKV_STATE_SHAPE, F32)],
        grid=(nb,),
        in_specs=[fwd(0), fwd(2), fwd(0), fwd(0), fwd(0), fwd(0),
                  bwd(0), bwd(2), bwd(0), bwd(0), bwd(0), bwd(0), st],
        out_specs=[fwd(0), bwd(0), st],
        scratch_shapes=[pltpu.VMEM(RWKV_STATE_SHAPE, F32)],
        compiler_params=_params(("arbitrary",)),
        name="rwkv_scan",
    )(proj, proj, kk, lw0, kd0, b0, proj, proj, kk, lw1, kd1, b1, s0)


def _retention_kernel(qf, kf, vf, cosf, sinf, qb, kb, vb, cosb, sinb, lgt_ref, r0_ref, yf_ref, yb_ref, rfin_ref, r_s):
    step = pl.program_id(0)

    @pl.when(step == 0)
    def _():
        r_s[...] = r0_ref[...]

    c = RET_CHUNK
    dk = B_QK_DIM
    dv = B_V_DIM
    lg_all = -_softplus(-lgt_ref[...])
    rowf = lax.broadcasted_iota(jnp.int32, (c, c), 0).astype(F32)
    colf = lax.broadcasted_iota(jnp.int32, (c, c), 1).astype(F32)
    lane = lax.broadcasted_iota(jnp.int32, (c, dk), 1)
    first_half = (lane % 64) < 32
    dirs = ((qf, kf, vf, cosf, sinf, yf_ref), (qb, kb, vb, cosb, sinb, yb_ref))
    for d, (q_ref, k_ref, v_ref, cos_ref, sin_ref, y_ref) in enumerate(dirs):
        cos = cos_ref[...]
        sin = sin_ref[...]
        diff = (rowf - colf) if d == 0 else (colf - rowf)
        pos = rowf if d == 0 else (c - 1.0) - rowf
        for h in range(B_HEADS):
            lg = lg_all[d * B_HEADS + h:d * B_HEADS + h + 1, :]
            qh = q_ref[:, h * dk:(h + 1) * dk].astype(F32)
            kh = k_ref[:, h * dk:(h + 1) * dk].astype(F32) * (dk ** -0.5)

            def rope(xv):
                swapped = jnp.where(first_half, pltpu.roll(xv, dk - 32, 1), pltpu.roll(xv, 32, 1))
                return xv * cos + swapped * sin

            qh = rope(qh)
            kh = rope(kh)
            vh = v_ref[:, h * dv:(h + 1) * dv]
            dmask = jnp.where(diff >= 0.0, jnp.exp(lg * jnp.maximum(diff, 0.0)), 0.0)
            scores = _dot_nt(_bf(qh), _bf(kh)) * dmask
            inner = _bdot(_bf(scores), vh)
            xi = jnp.exp(lg * (pos + 1.0))
            zeta = jnp.exp(lg * ((c - 1.0) - pos))
            r_prev = r_s[d, h]
            cross = _bdot(_bf(qh * xi), _bf(r_prev))
            y_ref[:, h * dv:(h + 1) * dv] = inner + cross
            kv = _dot_tn(_bf(kh * zeta), vh)
            g_chunk = jnp.exp(lg * float(c))
            r_s[d, h] = jnp.concatenate([g_chunk, g_chunk], axis=1) * r_prev + kv

    @pl.when(step == pl.num_programs(0) - 1)
    def _():
        rfin_ref[...] = r_s[...]


def _retention(proj, cos, sin, lgt, r0):
    t_len = proj.shape[0]
    c = RET_CHUNK
    nc = t_len // c
    qw = B_QK_WIDTH
    vw = B_V_WIDTH
    fq = lambda col: pl.BlockSpec((c, qw), lambda i: (i, col))
    bq = lambda col: pl.BlockSpec((c, qw), lambda i: (nc - 1 - i, col))
    st = pl.BlockSpec((2, B_HEADS, B_QK_DIM, B_V_DIM), lambda i: (0, 0, 0, 0))
    q_col = 0
    v_col = 2 * qw // vw
    return pl.pallas_call(
        _retention_kernel,
        out_shape=[jax.ShapeDtypeStruct((t_len, vw), F32), jax.ShapeDtypeStruct((t_len, vw), F32),
                   jax.ShapeDtypeStruct((2, B_HEADS, B_QK_DIM, B_V_DIM), F32)],
        grid=(nc,),
        in_specs=[
            fq(q_col), fq(q_col + 1), pl.BlockSpec((c, vw), lambda i: (i, v_col)),
            pl.BlockSpec((c, B_QK_DIM), lambda i: (i, 0)), pl.BlockSpec((c, B_QK_DIM), lambda i: (i, 0)),
            bq(q_col), bq(q_col + 1), pl.BlockSpec((c, vw), lambda i: (nc - 1 - i, v_col)),
            pl.BlockSpec((c, B_QK_DIM), lambda i: (nc - 1 - i, 0)),
            pl.BlockSpec((c, B_QK_DIM), lambda i: (nc - 1 - i, 0)),
            pl.BlockSpec((2 * B_HEADS, 128), lambda i: (0, 0)), st,
        ],
        out_specs=[pl.BlockSpec((c, vw), lambda i: (i, 0)), pl.BlockSpec((c, vw), lambda i: (nc - 1 - i, 0)), st],
        scratch_shapes=[pltpu.VMEM((2, B_HEADS, B_QK_DIM, B_V_DIM), F32)],
        compiler_params=_params(("arbitrary",)),
        name="retention",
    )(proj, proj, proj, cos, sin, proj, proj, proj, cos, sin, lgt, r0)


def _mix_out_kernel(of_ref, ob_ref, bonus_ref, g_ref, yf_ref, yb_ref, gb_ref, ga_ref, gbb_ref, x_ref,
                    alnw, alnb, rlnw, rlnb, npm, npf, g1_ref, sc2_ref, sh2_ref, wa_ref, wb_ref, wo_ref, bd_ref,
                    x1_ref, h2_ref, h3_ref):
    bd = bd_ref[...]
    o = of_ref[...] + ob_ref[...]
    mu = _dot_split_lhs(o, bd) * (1.0 / A_HEAD_DIM)
    oc = o - mu
    var = _dot_split_lhs(oc * oc, bd) * (1.0 / A_HEAD_DIM)
    ya = oc * lax.rsqrt(var + RWKV_GN_EPS) * alnw[...] + alnb[...] + bonus_ref[...]
    ya = _bdot(_bf(ya * g_ref[...]), wa_ref[...])

    y = yf_ref[...] + yb_ref[...]
    parts = []
    for h in range(B_HEADS):
        seg = y[:, h * B_V_DIM:(h + 1) * B_V_DIM]
        m = jnp.mean(seg, axis=-1, keepdims=True)
        sc = seg - m
        vr = jnp.mean(sc * sc, axis=-1, keepdims=True)
        parts.append(sc * lax.rsqrt(vr + RET_GN_EPS))
    yn = jnp.concatenate(parts, axis=1) * rlnw[...] + rlnb[...]
    gb = gb_ref[...].astype(F32)
    yb = _bdot(_bf(yn * (gb * _sigmoid(gb))), wb_ref[...])

    merged = _sigmoid(ga_ref[...].astype(F32)) * ya + _sigmoid(gbb_ref[...].astype(F32)) * yb
    mix = _bdot(_bf(merged), wo_ref[...])
    x1 = x_ref[...] + g1_ref[...] * (_rms(mix) * npm[...])
    x1_ref[...] = x1
    h2 = (_rms(x1) * npf[...]) * (1.0 + sc2_ref[...]) + sh2_ref[...]
    h2_ref[...] = h2
    _rows_to_packed_tiles(h2, h3_ref)


def _mix_out(x2d, proj, o_f, o_b, bonus, g, y_f, y_b, lp, vecs, bd):
    t_len = x2d.shape[0]
    tm = min(t_len, 256)
    aw = A_WIDTH
    d = D_MODEL
    ta = lambda: pl.BlockSpec((tm, aw), lambda i: (i, 0))
    td = lambda: pl.BlockSpec((tm, d), lambda i: (i, 0))
    pc = lambda c: pl.BlockSpec((tm, d), lambda i: (i, c))
    va = lambda: pl.BlockSpec((1, aw), lambda i: (0, 0))
    vd = lambda: pl.BlockSpec((1, d), lambda i: (0, 0))
    full = lambda shp: pl.BlockSpec(shp, lambda i: (0, 0))
    gcol = (2 * B_QK_WIDTH + B_V_WIDTH) // d
    return pl.pallas_call(
        _mix_out_kernel,
        out_shape=[jax.ShapeDtypeStruct((t_len, d), F32), jax.ShapeDtypeStruct((t_len, d), F32),
                   jax.ShapeDtypeStruct((t_len * PACK_TILE, LANES), jnp.uint32)],
        grid=(t_len // tm,),
        in_specs=[ta(), ta(), ta(), ta(), td(), td(), pc(gcol), pc(gcol + 1), pc(gcol + 2), td(),
                  va(), va(), vd(), vd(), vd(), vd(), vd(), vd(), vd(),
                  full((aw, d)), full((d, d)), full((d, d)), full((aw, aw))],
        out_specs=[td(), td(), pl.BlockSpec((tm * PACK_TILE, LANES), lambda i: (i, 0))],
        compiler_params=_params(("arbitrary",)),
        name="mix_out",
    )(o_f, o_b, bonus, g, y_f, y_b, proj, proj, proj, x2d,
      lp["ln_w"], lp["ln_b"], lp["ret_ln_w"], lp["ret_ln_b"], vecs["npm"], vecs["npf"], vecs["g1"], vecs["sc2"],
      vecs["sh2"], lp["w_a"], lp["w_b"], lp["w_o"], bd)


def _router_kernel(h_ref, rw_ref, bias_ref, sel_ref, wts_ref, rank_ref, cnt_ref, cnt_s, *, tm):
    step = pl.program_id(0)

    @pl.when(step == 0)
    def _():
        cnt_s[...] = jnp.zeros_like(cnt_s)

    ne = N_EXPERTS
    scores = _sigmoid(_dot3(h_ref[...], rw_ref[...]))
    work = scores + bias_ref[...]
    lane = lax.broadcasted_iota(jnp.int32, (tm, ne), 1).astype(F32)
    idxs = []
    vals = []
    for _ in range(TOP_K):
        m = jnp.max(work, axis=-1, keepdims=True)
        idx = jnp.min(jnp.where(work == m, lane, float(ne)), axis=-1, keepdims=True)
        oh = lane == idx
        vals.append(jnp.sum(jnp.where(oh, scores, 0.0), axis=-1, keepdims=True))
        idxs.append(idx)
        work = jnp.where(oh, -jnp.inf, work)
    sel_f = jnp.concatenate(idxs, axis=1)
    s_sel = jnp.concatenate(vals, axis=1)
    wts_ref[...] = s_sel / jnp.sum(s_sel, axis=1, keepdims=True) * ROUTED_SCALE
    sel_ref[...] = sel_f.astype(jnp.int32)

    hit = work == -jnp.inf
    r_i = lax.broadcasted_iota(jnp.int32, (tm, tm), 0)
    c_i = lax.broadcasted_iota(jnp.int32, (tm, tm), 1)
    before = _bdot((r_i > c_i).astype(BF16), hit.astype(BF16)) + cnt_s[...]
    ranks = [jnp.sum(jnp.where(lane == idxs[k], before, 0.0), axis=-1, keepdims=True) for k in range(TOP_K)]
    rank_ref[...] = jnp.concatenate(ranks, axis=1).astype(jnp.int32)
    cnt = cnt_s[...] + jnp.sum(hit.astype(F32), axis=0, keepdims=True)
    cnt_s[...] = cnt
    cnt_ref[...] = cnt.astype(jnp.int32)


def _router(h2, router_w, router_bias):
    t_len = h2.shape[0]
    tm = min(t_len, 256)
    kern = functools.partial(_router_kernel, tm=tm)
    tk = lambda: pl.BlockSpec((tm, TOP_K), lambda i: (i, 0))
    return pl.pallas_call(
        kern,
        out_shape=[jax.ShapeDtypeStruct((t_len, TOP_K), jnp.int32), jax.ShapeDtypeStruct((t_len, TOP_K), F32),
                   jax.ShapeDtypeStruct((t_len, TOP_K), jnp.int32), jax.ShapeDtypeStruct((1, N_EXPERTS), jnp.int32)],
        grid=(t_len // tm,),
        in_specs=[pl.BlockSpec((tm, D_MODEL), lambda i: (i, 0)),
                  pl.BlockSpec((D_MODEL, N_EXPERTS), lambda i: (0, 0)),
                  pl.BlockSpec((1, N_EXPERTS), lambda i: (0, 0))],
        out_specs=[tk(), tk(), tk(), pl.BlockSpec((1, N_EXPERTS), lambda i: (0, 0))],
        scratch_shapes=[pltpu.VMEM((1, N_EXPERTS), F32)],
        compiler_params=_params(("arbitrary",)),
        name="router",
    )(h2, router_w, router_bias.reshape(1, N_EXPERTS))


def _slot(sel_ref, rank_ref, start_ref, a):
    return start_ref[sel_ref[a]] + rank_ref[a]


def _dispatch_kernel(sel_ref, rank_ref, start_ref, cnt_ref, h_ref, xs_out, zrow, sem, zsem, *, tm):
    def row_copy(t, k):
        slot = _slot(sel_ref, rank_ref, start_ref, t * TOP_K + k)
        return pltpu.make_async_copy(_row_tile(h_ref, t), _row_tile(xs_out, slot), sem)

    def issue(t, carry):
        for k in range(TOP_K):
            row_copy(t, k).start(priority=k % 2)
        return carry

    lax.fori_loop(0, tm, issue, 0)

    def drain(t, carry):
        for k in range(TOP_K):
            row_copy(t, k).wait()
        return carry

    lax.fori_loop(0, tm, drain, 0)

    @pl.when(pl.program_id(0) == pl.num_programs(0) - 1)
    def _():
        zrow[...] = jnp.zeros_like(zrow)

        def for_pad_runs(e0, fn):
            def per_expert(e, carry):
                n = cnt_ref[e]
                n_pad = (n + MOE_BLOCK - 1) // MOE_BLOCK * MOE_BLOCK
                pad = n_pad - n
                off = start_ref[e] + n
                for bit in reversed(range(MOE_BLOCK.bit_length() - 1)):
                    size = 1 << bit
                    is_set = ((pad >> bit) & 1) == 1

                    @pl.when(is_set)
                    def _(off=off, size=size):
                        dst = xs_out.at[pl.ds(pl.multiple_of(off * PACK_TILE, PACK_TILE), size * PACK_TILE)]
                        fn(pltpu.make_async_copy(zrow.at[pl.ds(0, size * PACK_TILE)], dst, zsem))

                    off = off + jnp.where(is_set, size, 0)
                return carry

            lax.fori_loop(e0, e0 + ZERO_PAD_GROUP, per_expert, 0)

        def per_group(gi, carry):
            e0 = gi * ZERO_PAD_GROUP
            for_pad_runs(e0, lambda cp: cp.start())
            for_pad_runs(e0, lambda cp: cp.wait())
            return carry

        lax.fori_loop(0, N_EXPERTS // ZERO_PAD_GROUP, per_group, 0)


def _dispatch(h3, sel_flat, rank_flat, pad_start, counts, n_slots):
    t_len = h3.shape[0] // PACK_TILE
    tm = min(t_len, 256)
    kern = functools.partial(_dispatch_kernel, tm=tm)
    smem_blk = lambda: pl.BlockSpec((tm * TOP_K,), lambda i: (i,), memory_space=pltpu.SMEM)
    smem_all = lambda: pl.BlockSpec((N_EXPERTS,), lambda i: (0,), memory_space=pltpu.SMEM)
    return pl.pallas_call(
        kern,
        out_shape=jax.ShapeDtypeStruct((n_slots * PACK_TILE, LANES), jnp.uint32),
        grid=(t_len // tm,),
        in_specs=[smem_blk(), smem_blk(), smem_all(), smem_all(),
                  pl.BlockSpec((tm * PACK_TILE, LANES), lambda i: (i, 0))],
        out_specs=pl.BlockSpec(memory_space=pl.ANY),
        scratch_shapes=[pltpu.VMEM((MOE_BLOCK // 2 * PACK_TILE, LANES), jnp.uint32), pltpu.SemaphoreType.DMA(()),
                        pltpu.SemaphoreType.DMA(())],
        compiler_params=_params(("arbitrary",)),
        name="dispatch",
    )(sel_flat, rank_flat, pad_start, counts, h3)


def _expert_kernel(be_ref, nu_ref, xs_ref, wg_ref, wu_ref, wd_ref, ys_ref, wg_s, wu_s, wd_s):
    b = pl.program_id(0)
    used = b < nu_ref[0]
    new_expert = (b == 0) | (be_ref[b] != be_ref[jnp.maximum(b - 1, 0)])

    @pl.when(used & new_expert)
    def _():
        wg_s[...] = _bf(wg_ref[0])
        wu_s[...] = _bf(wu_ref[0])
        wd_s[...] = _bf(wd_ref[0])

    @pl.when(used)
    def _():
        xb = _packed_tiles_to_rows(xs_ref, MOE_BLOCK)
        gate = _bdot(xb, wg_s[...])
        up = _bdot(xb, wu_s[...])
        act = gate * _sigmoid(gate) * up
        _rows_to_packed_tiles(_bdot(_bf(act), wd_s[...]), ys_ref)

    @pl.when(jnp.logical_not(used))
    def _():
        ys_ref[...] = jnp.zeros_like(ys_ref)


def _experts(xs, block_e, n_used, w_gate, w_up, w_down):
    n_slots = xs.shape[0] // PACK_TILE
    n_blocks = n_slots // MOE_BLOCK
    rows = (MOE_BLOCK * PACK_TILE, LANES)
    grid_spec = pltpu.PrefetchScalarGridSpec(
        num_scalar_prefetch=2,
        grid=(n_blocks,),
        in_specs=[
            pl.BlockSpec((MOE_BLOCK * PACK_TILE, LANES), lambda b, be, nu: (jnp.where(b < nu[0], b, 0), 0)),
            pl.BlockSpec((1, D_MODEL, EXPERT_FF), lambda b, be, nu: (be[b], 0, 0)),
            pl.BlockSpec((1, D_MODEL, EXPERT_FF), lambda b, be, nu: (be[b], 0, 0)),
            pl.BlockSpec((1, EXPERT_FF, D_MODEL), lambda b, be, nu: (be[b], 0, 0)),
        ],
        out_specs=pl.BlockSpec(rows, lambda b, be, nu: (b, 0)),
        scratch_shapes=[pltpu.VMEM((D_MODEL, EXPERT_FF), BF16), pltpu.VMEM((D_MODEL, EXPERT_FF), BF16),
                        pltpu.VMEM((EXPERT_FF, D_MODEL), BF16)],
    )
    return pl.pallas_call(
        _expert_kernel,
        out_shape=jax.ShapeDtypeStruct((n_slots * PACK_TILE, LANES), jnp.uint32),
        grid_spec=grid_spec,
        compiler_params=_params(("arbitrary",)),
        name="experts",
    )(block_e, n_used, xs, w_gate, w_up, w_down)


def _combine_kernel(sel_ref, rank_ref, start_ref, ys_ref, wts_ref, h_ref, x1_ref, sg_ref, su_ref, sd_ref, npo, g2_ref,
                    o_ref, buf, sem, *, tm):
    def row_copy(t, k):
        slot = _slot(sel_ref, rank_ref, start_ref, t * TOP_K + k)
        return pltpu.make_async_copy(_row_tile(ys_ref, slot), _row_tile(buf, t, (k,)), sem)

    def issue(t, carry):
        for k in range(TOP_K):
            row_copy(t, k).start(priority=k % 2)
        return carry

    lax.fori_loop(0, tm, issue, 0)

    hb = _bf(h_ref[...])
    gate = _bdot(hb, sg_ref[...])
    up = _bdot(hb, su_ref[...])
    shared = _bdot(_bf(gate * _sigmoid(gate) * up), sd_ref[...])

    def drain(t, carry):
        for k in range(TOP_K):
            row_copy(t, k).wait()
        return carry

    lax.fori_loop(0, tm, drain, 0)

    wts = wts_ref[...]
    routed = _packed_tiles_to_rows(buf, tm, (0,), F32) * wts[:, 0:1]
    for k in range(1, TOP_K):
        routed = routed + _packed_tiles_to_rows(buf, tm, (k,), F32) * wts[:, k:k + 1]
    o_ref[...] = x1_ref[...] + g2_ref[...] * (_rms(routed + shared) * npo[...])


def _combine(ys, sel_flat, rank_flat, pad_start, wts, h2, x1, sg, su, sd, npo, g2):
    t_len = h2.shape[0]
    tm = min(t_len, 256)
    d = D_MODEL
    kern = functools.partial(_combine_kernel, tm=tm)
    smem_blk = lambda: pl.BlockSpec((tm * TOP_K,), lambda i: (i,), memory_space=pltpu.SMEM)
    td = lambda: pl.BlockSpec((tm, d), lambda i: (i, 0))
    vd = lambda: pl.BlockSpec((1, d), lambda i: (0, 0))
    return pl.pallas_call(
        kern,
        out_shape=jax.ShapeDtypeStruct((t_len, d), F32),
        grid=(t_len // tm,),
        in_specs=[smem_blk(), smem_blk(),
                  pl.BlockSpec((N_EXPERTS,), lambda i: (0,), memory_space=pltpu.SMEM),
                  pl.BlockSpec(memory_space=pl.ANY),
                  pl.BlockSpec((tm, TOP_K), lambda i: (i, 0)),
                  td(), td(),
                  pl.BlockSpec((d, SHARED_FF), lambda i: (0, 0)), pl.BlockSpec((d, SHARED_FF), lambda i: (0, 0)),
                  pl.BlockSpec((SHARED_FF, d), lambda i: (0, 0)), vd(), vd()],
        out_specs=td(),
        scratch_shapes=[pltpu.VMEM((TOP_K, tm * PACK_TILE, LANES), jnp.uint32), pltpu.SemaphoreType.DMA(())],
        compiler_params=_params(("arbitrary",)),
        name="combine",
    )(sel_flat, rank_flat, pad_start, ys, wts, h2, x1, sg, su, sd, npo, g2)


def _block_diag_ones(width, group):
    idx = np.arange(width) // group
    return jnp.asarray(idx[:, None] == idx[None, :], dtype=BF16)


def _rope_tables(t_len):
    n_rows = t_len // GRID_W
    quarter = B_QK_DIM // 4
    inv_freq = jnp.asarray(ROPE_BASE, F32) ** (-jnp.arange(quarter, dtype=F32) / quarter)
    ang_r = jnp.arange(n_rows, dtype=F32)[:, None] * inv_freq
    ang_c = jnp.arange(GRID_W, dtype=F32)[:, None] * inv_freq
    grid = (n_rows, GRID_W, quarter)
    by_row = lambda a: jnp.broadcast_to(a[:, None, :], grid)
    by_col = lambda a: jnp.broadcast_to(a[None, :, :], grid)
    cr, sr, cc, sc = by_row(jnp.cos(ang_r)), by_row(jnp.sin(ang_r)), by_col(jnp.cos(ang_c)), by_col(jnp.sin(ang_c))
    cos = jnp.concatenate([cr, cr, cc, cc], axis=-1).reshape(t_len, B_QK_DIM)
    sin = jnp.concatenate([-sr, sr, -sc, sc], axis=-1).reshape(t_len, B_QK_DIM)
    return cos, sin


def _token_mixer(x2d, g_pre, sc, sh, lp, cos, sin, states, bd):
    proj_a, proj_b = _in_proj(x2d, g_pre, sc, sh, lp["w_in"], lp["mu"])
    prep = _rwkv_prep(proj_a, lp, bd)
    o_f, o_b, s_fin = _rwkv_scan(proj_a, prep, states[0])
    y_f, y_b, r_fin = _retention(proj_b, cos, sin, lp["lgt"], states[1])
    return proj_b, prep, (o_f, o_b), (y_f, y_b), (s_fin, r_fin)


def kernel(x, c, ctx, c_ctx, w_mod, b_mod, norm_pre_mix, norm_post_mix, norm_pre_ffn, norm_post_ffn, w_in, shift_mu,
           rwkv_w0, rwkv_w_up, rwkv_a0, rwkv_a_up, rwkv_g_up, rwkv_k_k, rwkv_k_a, rwkv_r_k, rwkv_ln_w, rwkv_ln_b,
           w_branch_a, ret_decay_logit, ret_ln_w, ret_ln_b, w_branch_b, w_out, router_w, router_bias, exp_w_gate,
           exp_w_up, exp_w_down, sh_w_gate, sh_w_up, sh_w_down):
    d = D_MODEL
    assert x.shape[0] == 1 and w_in.shape[0] == 1, "single batch element, single layer"
    t_len = x.shape[1]
    x2d = x.reshape(t_len, d)
    ctx2d = ctx.reshape(ctx.shape[1], d)
    row = lambda a: a.reshape(1, -1)

    cs = jnp.zeros((8, d), F32).at[0].set(c[0]).at[1].set(c_ctx)
    mod = _modulation(cs, w_mod[0], b_mod[0])
    sh1, sc1, g1, sh2, sc2, g2 = [mod[0:1, i * d:(i + 1) * d] for i in range(6)]
    csh1, csc1 = mod[1:2, 0:d], mod[1:2, d:2 * d]

    lp = {
        "w_in": _w_in_layout(w_in[0]),
        "mu": jnp.pad(shift_mu[0], ((0, 0), (0, SHIFT_PAD - SHIFT_COLS))),
        "k_k": row(rwkv_k_k[0]), "k_a": row(rwkv_k_a[0]), "r_k": row(rwkv_r_k[0]),
        "w0": rwkv_w0[0], "w_up": rwkv_w_up[0], "a0": rwkv_a0[0], "a_up": rwkv_a_up[0], "g_up": rwkv_g_up[0],
        "ln_w": row(rwkv_ln_w[0]), "ln_b": row(rwkv_ln_b[0]),
        "ret_ln_w": row(ret_ln_w[0]), "ret_ln_b": row(ret_ln_b[0]),
        "lgt": jnp.broadcast_to(ret_decay_logit[0].reshape(2 * B_HEADS, 1), (2 * B_HEADS, 128)),
        "w_a": w_branch_a[0].astype(BF16), "w_b": w_branch_b[0].astype(BF16), "w_o": w_out[0].astype(BF16),
    }
    bd = _block_diag_ones(A_WIDTH, A_HEAD_DIM)
    g_pre = row(norm_pre_mix[0])

    t_ctx = ctx2d.shape[0]
    zero_states = (jnp.zeros(RWKV_STATE_SHAPE, F32),
                   jnp.zeros((2, B_HEADS, B_QK_DIM, B_V_DIM), F32))
    ones = jnp.ones((t_ctx, B_QK_DIM), F32)
    *_, ctx_states = _token_mixer(ctx2d, g_pre, csc1, csh1, lp, ones, jnp.zeros_like(ones), zero_states, bd)

    cos, sin = _rope_tables(t_len)
    proj, prep, (o_f, o_b), (y_f, y_b), _ = _token_mixer(x2d, g_pre, sc1, sh1, lp, cos, sin, ctx_states, bd)
    vecs = {"npm": row(norm_post_mix[0]), "npf": row(norm_pre_ffn[0]), "g1": g1, "sc2": sc2, "sh2": sh2}
    x1, h2, h3 = _mix_out(x2d, proj, o_f, o_b, prep[7], prep[8], y_f, y_b, lp, vecs, bd)

    sel, wts, rank, counts = _router(h2, router_w[0], router_bias[0])
    counts = counts.reshape(N_EXPERTS)
    padded = (counts + MOE_BLOCK - 1) // MOE_BLOCK * MOE_BLOCK
    pad_end = jnp.cumsum(padded)
    pad_start = (pad_end - padded).astype(jnp.int32)
    n_assign = t_len * TOP_K
    n_blocks = (n_assign + N_EXPERTS * (MOE_BLOCK - 1) + MOE_BLOCK - 1) // MOE_BLOCK
    block_start = jnp.arange(n_blocks, dtype=jnp.int32) * MOE_BLOCK
    block_e = jnp.minimum(jnp.sum(pad_end[None, :] <= block_start[:, None], axis=1), N_EXPERTS - 1).astype(jnp.int32)
    n_used = (pad_end[-1:] // MOE_BLOCK).astype(jnp.int32)
    sel_flat = sel.reshape(n_assign)
    rank_flat = rank.reshape(n_assign)

    xs = _dispatch(h3, sel_flat, rank_flat, pad_start, counts, n_blocks * MOE_BLOCK)
    ys = _experts(xs, block_e, n_used, exp_w_gate[0], exp_w_up[0], exp_w_down[0])
    out = _combine(ys, sel_flat, rank_flat, pad_start, wts, h2, x1, sh_w_gate[0].astype(BF16),
                   sh_w_up[0].astype(BF16), sh_w_down[0].astype(BF16), row(norm_post_ffn[0]), g2)
    return out.reshape(x.shape)
```

```python
import functools

import jax
import jax.numpy as jnp
import numpy as np
from jax import lax
from jax.experimental import pallas as pl
from jax.experimental.pallas import tpu as pltpu

F32 = jnp.float32
BF16 = jnp.bfloat16

D_MODEL = 1024
GRID_W = 64
NORM_EPS = 1e-6

A_HEAD_DIM = 64
A_WIDTH = D_MODEL // 2
A_HEADS = A_WIDTH // A_HEAD_DIM
W_LORA = 64
ICLR_LORA = 64
G_LORA = 128
RWKV_GN_EPS = 64e-5

B_HEADS = 4
B_QK_WIDTH = D_MODEL // 2
B_V_WIDTH = D_MODEL
B_QK_DIM = B_QK_WIDTH // B_HEADS
B_V_DIM = B_V_WIDTH // B_HEADS
RET_CHUNK = 128
RET_GN_EPS = 1e-5
ROPE_BASE = 10000.0

SHIFT_COLS = 3 * A_WIDTH + W_LORA + ICLR_LORA + G_LORA
LORA_COLS = W_LORA + ICLR_LORA + G_LORA
SHIFT_PAD = 2048
PROJ_COLS = SHIFT_PAD + 2 * B_QK_WIDTH + 2 * B_V_WIDTH + 2 * D_MODEL

N_EXPERTS = 256
TOP_K = 8
EXPERT_FF = D_MODEL // 4
SHARED_FF = D_MODEL // 4
ROUTED_SCALE = 2.5
MOE_BLOCK = 512
ZERO_PAD_GROUP = 16
MOE_TOKEN_TILE = 512

RWKV_CHUNK = 64
RWKV_STATE_SHAPE = (2, A_HEADS // 2, A_HEAD_DIM, 2 * A_HEAD_DIM)
PROJ_TN = 1024
PROJ_TM = 1024
VMEM_LIMIT = 48 * 1024 * 1024


def _params(sem):
    return pltpu.CompilerParams(dimension_semantics=sem, vmem_limit_bytes=VMEM_LIMIT)


def _bf(a):
    return a.astype(BF16)


def _bdot(a, b):
    return jnp.dot(a, b, preferred_element_type=F32)


def _dot_nt(a, b):
    return lax.dot_general(a, b, (((1,), (1,)), ((), ())), preferred_element_type=F32)


def _dot_tn(a, b):
    return lax.dot_general(a, b, (((0,), (0,)), ((), ())), preferred_element_type=F32)


def _split(a, n):
    out = []
    rem = a
    for _ in range(n):
        p = _bf(rem)
        out.append(p)
        rem = rem - p.astype(F32)
    return out


def _dot_split_lhs(a, b_bf, n=3):
    acc = None
    for p in _split(a, n):
        t = _bdot(p, b_bf)
        acc = t if acc is None else acc + t
    return acc


def _dot_split_rhs(a_bf, b, n=3):
    acc = None
    for p in _split(b, n):
        t = _bdot(a_bf, p)
        acc = t if acc is None else acc + t
    return acc


def _dot3(a, b):
    ah, al = _split(a, 2)
    bh, bl = _split(b, 2)
    return _bdot(ah, bh) + (_bdot(ah, bl) + _bdot(al, bh))


def _sigmoid(x):
    return 1.0 / (1.0 + jnp.exp(-x))


def _softplus(x):
    return jnp.maximum(x, 0.0) + jnp.log1p(jnp.exp(-jnp.abs(x)))


def _rms(x):
    return x * lax.rsqrt(jnp.mean(x * x, axis=-1, keepdims=True) + NORM_EPS)


LANES = 128


PACK_TILE = D_MODEL // LANES // 2
HIGH_HALF = 0xFFFF0000


def _row_tile(ref, r, lead=()):
    return ref.at[lead + (pl.ds(pl.multiple_of(r * PACK_TILE, PACK_TILE), PACK_TILE),)]


def _rows_to_packed_tiles(x2d, ref):
    n = x2d.shape[0]
    half = D_MODEL // 2
    for j in range(PACK_TILE):
        lo = _bf(x2d[:, j * LANES:(j + 1) * LANES]).astype(F32)
        hi = _bf(x2d[:, half + j * LANES:half + (j + 1) * LANES]).astype(F32)
        word = (pltpu.bitcast(hi, jnp.uint32) & jnp.uint32(HIGH_HALF)) | (pltpu.bitcast(lo, jnp.uint32) >> 16)
        ref[pl.ds(j, n, stride=PACK_TILE), :] = word


def _packed_tiles_to_rows(ref, n, lead=(), dtype=BF16):
    lo, hi = [], []
    for j in range(PACK_TILE):
        word = ref[lead + (pl.ds(j, n, stride=PACK_TILE), slice(None))]
        lo.append(pltpu.bitcast(word << 16, F32).astype(dtype))
        hi.append(pltpu.bitcast(word & jnp.uint32(HIGH_HALF), F32).astype(dtype))
    return jnp.concatenate(lo + hi, axis=1)


def _mod_kernel(cs_ref, w_ref, b_ref, o_ref):
    cs = cs_ref[...]
    s = cs * _sigmoid(cs)
    o_ref[...] = _dot3(s, w_ref[...]) + b_ref[...]


def _modulation(cs, w_mod, b_mod):
    n_out = w_mod.shape[1]
    tn = 1536
    return pl.pallas_call(
        _mod_kernel,
        out_shape=jax.ShapeDtypeStruct((8, n_out), F32),
        grid=(n_out // tn,),
        in_specs=[
            pl.BlockSpec((8, D_MODEL), lambda j: (0, 0)),
            pl.BlockSpec((D_MODEL, tn), lambda j: (0, j)),
            pl.BlockSpec((1, tn), lambda j: (0, j)),
        ],
        out_specs=pl.BlockSpec((8, tn), lambda j: (0, j)),
        compiler_params=_params(("arbitrary",)),
        name="modulation",
    )(cs, w_mod, b_mod.reshape(1, n_out))


W_PAD_TN = SHIFT_PAD - SHIFT_COLS


def _w_in_layout_kernel(w_ref, o_ref, *, gap_tile):
    w = w_ref[...]
    o_ref[...] = _bf(jnp.where(pl.program_id(0) == gap_tile, jnp.zeros_like(w), w))


def _w_in_layout(w_in):
    tn = W_PAD_TN
    gap_tile = SHIFT_COLS // tn
    kern = functools.partial(_w_in_layout_kernel, gap_tile=gap_tile)
    src = lambda j: (0, jnp.where(j < gap_tile, j, jnp.maximum(j - 1, 0)))
    return pl.pallas_call(
        kern,
        out_shape=jax.ShapeDtypeStruct((D_MODEL, PROJ_COLS), BF16),
        grid=(PROJ_COLS // tn,),
        in_specs=[pl.BlockSpec((D_MODEL, tn), src)],
        out_specs=pl.BlockSpec((D_MODEL, tn), lambda j: (0, j)),
        compiler_params=_params(("arbitrary",)),
        name="w_in_layout",
    )(w_in)


def _in_proj_kernel(x_ref, xp_ref, xn_ref, g_ref, sc_ref, sh_ref, w_ref, mu_ref, o_ref, ob_ref, h_s, hh_s,
                    *, tm, n_shift):
    i = pl.program_id(0)
    j = pl.program_id(1)
    n_i = pl.num_programs(0)

    def norm_mod(xv):
        return (_rms(xv) * g_ref[...]) * (1.0 + sc_ref[...]) + sh_ref[...]

    @pl.when(j == 0)
    def _():
        h_s[...] = _bf(norm_mod(x_ref[...]))
        hp = jnp.where(i > 0, norm_mod(xp_ref[...]), 0.0)
        hn = jnp.where(i < n_i - 1, norm_mod(xn_ref[...]), 0.0)
        hh_s[0:8, :] = hp
        hh_s[8:16, :] = hn

    @pl.when(j < n_shift)
    def _():
        w = w_ref[...]
        p = _bdot(h_s[...], w)
        ph = _bdot(_bf(hh_s[...]), w)
        row = lax.broadcasted_iota(jnp.int32, p.shape, 0)
        prev = jnp.where(row == 0, ph[7:8, :], pltpu.roll(p, 1, 0))
        nxt = jnp.where(row == tm - 1, ph[8:9, :], pltpu.roll(p, tm - 1, 0))
        mu = mu_ref[...]
        o_ref[...] = p + mu[0:1, :] * (prev - p) + mu[1:2, :] * (nxt - p)

    @pl.when(j >= n_shift)
    def _():
        ob_ref[...] = _bf(_bdot(h_s[...], w_ref[...]))


def _in_proj(x2d, g, sc, sh, w_bf, mu_pad):
    t_len = x2d.shape[0]
    tm = min(t_len, PROJ_TM)
    tn = PROJ_TN
    n_shift = SHIFT_PAD // tn
    tb8 = tm // 8
    nb8 = t_len // 8
    kern = functools.partial(_in_proj_kernel, tm=tm, n_shift=n_shift)
    vec = lambda: pl.BlockSpec((1, D_MODEL), lambda i, j: (0, 0))
    return pl.pallas_call(
        kern,
        out_shape=[jax.ShapeDtypeStruct((t_len, SHIFT_PAD), F32),
                   jax.ShapeDtypeStruct((t_len, PROJ_COLS - SHIFT_PAD), BF16)],
        grid=(t_len // tm, PROJ_COLS // tn),
        in_specs=[
            pl.BlockSpec((tm, D_MODEL), lambda i, j: (i, 0)),
            pl.BlockSpec((8, D_MODEL), lambda i, j: (jnp.maximum(i * tb8 - 1, 0), 0)),
            pl.BlockSpec((8, D_MODEL), lambda i, j: (jnp.minimum((i + 1) * tb8, nb8 - 1), 0)),
            vec(), vec(), vec(),
            pl.BlockSpec((D_MODEL, tn), lambda i, j: (0, j)),
            pl.BlockSpec((2, tn), lambda i, j: (0, jnp.minimum(j, n_shift - 1))),
        ],
        out_specs=[pl.BlockSpec((tm, tn), lambda i, j: (i, jnp.minimum(j, n_shift - 1))),
                   pl.BlockSpec((tm, tn), lambda i, j: (i, jnp.maximum(j - n_shift, 0)))],
        scratch_shapes=[pltpu.VMEM((tm, D_MODEL), BF16), pltpu.VMEM((16, D_MODEL), F32)],
        compiler_params=_params(("arbitrary", "arbitrary")),
        name="in_proj",
    )(x2d, x2d, x2d, g, sc, sh, w_bf, mu_pad)


def _rwkv_prep_kernel(r_ref, k_ref, v_ref, lora_ref, kk_w, ka_w, rk_w, w0_ref, wup_ref, a0_ref, aup_ref, gup_ref,
                      bd_ref, kk_o, lw0_o, lw1_o, kd0_o, kd1_o, b0_o, b1_o, bonus_o, g_o):
    r = r_ref[...]
    k = k_ref[...]
    v = v_ref[...]
    lora = lora_ref[...]
    xw = lora[:, 0:W_LORA]
    xa = lora[:, W_LORA:W_LORA + ICLR_LORA]
    xg = lora[:, W_LORA + ICLR_LORA:]
    bd = bd_ref[...]

    kk = k * kk_w[...]
    nrm = jnp.sqrt(_dot_split_lhs(kk * kk, bd))
    kk = kk / jnp.maximum(nrm, 1e-12)
    kk_o[...] = kk

    tw = jnp.tanh(xw)
    lw_outs = (lw0_o, lw1_o)
    kd_outs = (kd0_o, kd1_o)
    b_outs = (b0_o, b1_o)
    kd_sum = None
    for d in range(2):
        z = w0_ref[d:d + 1, :] + _dot3(tw, wup_ref[d])
        w_log = -_softplus(-z) - 0.5
        lw_outs[d][...] = -jnp.exp(w_log)
        a_d = _sigmoid(a0_ref[d:d + 1, :] + _dot3(xa, aup_ref[d]))
        kd = k * (1.0 + (a_d - 1.0) * ka_w[...])
        kd_outs[d][...] = kd
        b_outs[d][...] = kk * a_d
        kd_sum = kd if kd_sum is None else kd_sum + kd
    g_o[...] = _dot3(_sigmoid(xg), gup_ref[...])
    bonus_o[...] = _dot_split_lhs(r * kd_sum * rk_w[...], bd) * v


def _rwkv_prep(proj, lp, bd):
    t_len = proj.shape[0]
    tm = min(t_len, 512)
    aw = A_WIDTH
    col = lambda c: pl.BlockSpec((tm, aw), lambda i: (i, c))
    vec = lambda: pl.BlockSpec((1, aw), lambda i: (0, 0))
    full = lambda shp: pl.BlockSpec(shp, lambda i: (0,) * len(shp))
    outs = [jax.ShapeDtypeStruct((t_len, aw), F32)] * 9
    return pl.pallas_call(
        _rwkv_prep_kernel,
        out_shape=outs,
        grid=(t_len // tm,),
        in_specs=[
            col(0), col(1), col(2),
            pl.BlockSpec((tm, LORA_COLS), lambda i: (i, 3 * aw // LORA_COLS)),
            vec(), vec(), vec(),
            full((2, aw)), full((2, W_LORA, aw)), full((2, aw)), full((2, ICLR_LORA, aw)), full((G_LORA, aw)),
            full((aw, aw)),
        ],
        out_specs=[pl.BlockSpec((tm, aw), lambda i: (i, 0))] * 9,
        compiler_params=_params(("arbitrary",)),
        name="rwkv_prep",
    )(proj, proj, proj, proj, lp["k_k"], lp["k_a"], lp["r_k"], lp["w0"], lp["w_up"], lp["a0"], lp["a_up"],
      lp["g_up"], bd)


def _rwkv_scan_kernel(rf, vf, kkf, lwf, kdf, bf_, rb, vb, kkb, lwb, kdb, bb, s0_ref, of_ref, ob_ref, sfin_ref, s_s,
                      *, tb):
    step = pl.program_id(0)

    @pl.when(step == 0)
    def _():
        s_s[...] = s0_ref[...]

    c = RWKV_CHUNK
    n = A_HEAD_DIM
    assert c == n
    pw = 2 * n
    row = lax.broadcasted_iota(jnp.int32, (c, pw), 0)
    lane = lax.broadcasted_iota(jnp.int32, (c, pw), 1)
    col = lane % c
    first = lane < n
    eye = (row == col).astype(F32)
    r_c = lax.broadcasted_iota(jnp.int32, (c, c), 0)
    c_c = lax.broadcasted_iota(jnp.int32, (c, c), 1)
    n_ch = tb // c
    dirs = ((rf, vf, kkf, lwf, kdf, bf_, of_ref), (rb, vb, kkb, lwb, kdb, bb, ob_ref))
    pairs = range(A_HEADS // 2)
    psl = [slice(q * pw, (q + 1) * pw) for q in pairs]

    def bd(x):
        z = jnp.zeros_like(x)
        return jnp.concatenate([jnp.where(first, x, z), jnp.where(first, z, x)], axis=0)

    def chunk_terms(d, refs, ci):
        r_ref, v_ref, kk_ref, lw_ref, kd_ref, b_ref, _ = refs
        if d == 0:
            incl, strict, tri = row >= col, row > col, r_c >= c_c
        else:
            incl, strict, tri = row <= col, row < col, r_c <= c_c
        rows = slice(ci * c, (ci + 1) * c)
        lw = lw_ref[rows, :]
        r = r_ref[rows, :]
        kk = kk_ref[rows, :]
        kd = kd_ref[rows, :]
        b = b_ref[rows, :]
        cl = _dot_split_rhs(tri.astype(BF16), lw)
        cl_tot = cl[c - 1:c, :] if d == 0 else cl[0:1, :]
        e_neg = jnp.exp(-cl)
        e_end = jnp.exp(cl_tot - cl)
        g_tot = jnp.exp(cl_tot)
        rt = r * jnp.exp(cl)
        x_all = _bf(jnp.concatenate([-(kk * jnp.exp(cl - lw)), rt], axis=0))
        bt_all = _bf(b * e_neg)
        kt_all = _bf(kd * e_neg)
        return {
            "rows": rows, "strict": strict, "incl": incl,
            "x": [x_all[:, ps] for ps in psl],
            "z": [jnp.concatenate([bd(bt_all[:, ps]), bd(kt_all[:, ps])], axis=0) for ps in psl],
            "bd_bh": [bd(_bf(b * e_end)[:, ps]) for ps in psl],
            "bd_kh": [bd(_bf(kd * e_end)[:, ps]) for ps in psl],
            "bd_v": [bd(_bf(v_ref[rows, :])[:, ps]) for ps in psl],
            "rt": [rt[:, ps] for ps in psl],
            "g_tot": [g_tot[:, ps] for ps in psl],
        }

    chunks = [(d, cc) for cc in range(n_ch) for d in range(2)]
    prep = {(d, cc): chunk_terms(d, dirs[d], cc if d == 0 else n_ch - 1 - cc) for d, cc in chunks}
    units = [(k, q) for k in chunks for q in pairs]

    def per_unit(name, indexed=True):
        return [prep[k][name][q] if indexed else prep[k][name] for k, q in units]

    strict, incl = per_unit("strict", False), per_unit("incl", False)
    x_u, bd_bh, bd_kh, bd_v, rt_u = (per_unit(s) for s in ("x", "bd_bh", "bd_kh", "bd_v", "rt"))
    g = [_dot_nt(x, z) for x, z in zip(x_u, per_unit("z"))]
    a_ab = [jnp.where(s, gi[:c, :pw], 0.0) for s, gi in zip(strict, g)]
    a_rb = [_bf(jnp.where(i, gi[c:, :pw], 0.0)) for i, gi in zip(incl, g)]
    a_k = [_bf(jnp.concatenate([jnp.where(s, gi[:c, pw:], 0.0), jnp.where(i, gi[c:, pw:], 0.0)], axis=0))
           for s, i, gi in zip(strict, incl, g)]
    a2 = [_bdot(_bf(a), bd(_bf(a))) for a in a_ab]
    pa = [jnp.concatenate([eye + a, sq], axis=0) for a, sq in zip(a_ab, a2)]
    for _ in range(4):
        nxt = [_bdot(_bf(x), bd(_bf(x[c:]))) for x in pa]
        pa = [jnp.concatenate([x[:c] + y[:c], y[c:]], axis=0) for x, y in zip(pa, nxt)]
    p = [_bf(x[:c] + _bdot(_bf(x[:c]), bd(_bf(x[c:])))) for x in pa]
    vk = [_bdot(a, v) for a, v in zip(a_k, bd_v)]
    tw = [_bf(_bdot(pi, jnp.concatenate([bd(x[:c]), bd(_bf(vki[:c]))], axis=1)))
          for pi, vki, x in zip(p, vk, x_u)]
    bd_at = [bd(t[:, :pw]) for t in tw]
    bd_w = [bd(t[:, pw:]) for t in tw]
    mt = [_bf(_dot_tn(a, bh)) for a, bh in zip(bd_at, bd_bh)]
    ntf = [_dot_tn(jnp.concatenate([w, v], axis=0), jnp.concatenate([bh, kh], axis=0))
           for w, v, bh, kh in zip(bd_w, bd_v, bd_bh, bd_kh)]
    nt = [m[:n] + m[n:] for m in ntf]
    rw = [_bdot(a, jnp.concatenate([at, w], axis=1)) for a, at, w in zip(a_rb, bd_at, bd_w)]
    ry = [_bf(r + w[:, :pw]) for w, r in zip(rw, rt_u)]
    y0 = [w[:, pw:] + vki[c:] for w, vki in zip(rw, vk)]
    term = {u: vals for u, vals in zip(units, zip(mt, nt, ry, y0, per_unit("g_tot")))}

    state = [[s_s[d, q] for q in pairs] for d in range(2)]
    dq = [(d, q) for d in range(2) for q in pairs]
    for cc in range(n_ch):
        mt_c, nt_c, ry_c, y0_c, gt_c = zip(*[term[((d, cc), q)] for d, q in dq])
        s0 = [state[d][q] for d, q in dq]
        ys = [_dot_nt(ry_c[i], bd(_bf(s0[i]))) + y0_c[i] for i in range(len(dq))]
        upd = [_bdot(jnp.concatenate(_split(s0[i], 2), axis=0), mt_c[i]) for i in range(len(dq))]
        for i, (d, q) in enumerate(dq):
            state[d][q] = s0[i] * gt_c[i] + (upd[i][:n] + upd[i][n:]) + nt_c[i]
        for d in range(2):
            y_d = [ys[i] for i, (dd, _) in enumerate(dq) if dd == d]
            dirs[d][6][prep[(d, cc)]["rows"], :] = jnp.concatenate(y_d, axis=1)
    for d in range(2):
        for q in pairs:
            s_s[d, q] = state[d][q]

    @pl.when(step == pl.num_programs(0) - 1)
    def _():
        sfin_ref[...] = s_s[...]


def _rwkv_scan(proj, prep, s0):
    kk, lw0, lw1, kd0, kd1, b0, b1 = prep[:7]
    t_len = proj.shape[0]
    tb = 4 * RWKV_CHUNK
    nb = t_len // tb
    aw = A_WIDTH
    fwd = lambda c: pl.BlockSpec((tb, aw), lambda i: (i, c))
    bwd = lambda c: pl.BlockSpec((tb, aw), lambda i: (nb - 1 - i, c))
    st = pl.BlockSpec(RWKV_STATE_SHAPE, lambda i: (0, 0, 0, 0))
    kern = functools.partial(_rwkv_scan_kernel, tb=tb)
    return pl.pallas_call(
        kern,
        out_shape=[jax.ShapeDtypeStruct((t_len, aw), F32), jax.ShapeDtypeStruct((t_len, aw), F32),
                   jax.ShapeDtypeStruct(RWKV_STATE_SHAPE, F32)],
        grid=(nb,),
        in_specs=[fwd(0), fwd(2), fwd(0), fwd(0), fwd(0), fwd(0),
                  bwd(0), bwd(2), bwd(0), bwd(0), bwd(0), bwd(0), st],
        out_specs=[fwd(0), bwd(0), st],
        scratch_shapes=[pltpu.VMEM(RWKV_STATE_SHAPE, F32)],
        compiler_params=_params(("arbitrary",)),
        name="rwkv_scan",
    )(proj, proj, kk, lw0, kd0, b0, proj, proj, kk, lw1, kd1, b1, s0)


def _retention_kernel(qf, kf, vf, cosf, sinf, qb, kb, vb, cosb, sinb, lgt_ref, r0_ref, yf_ref, yb_ref, rfin_ref, r_s):
    step = pl.program_id(0)

    @pl.when(step == 0)
    def _():
        r_s[...] = r0_ref[...]

    c = RET_CHUNK
    dk = B_QK_DIM
    dv = B_V_DIM
    lg_all = -_softplus(-lgt_ref[...])
    rowf = lax.broadcasted_iota(jnp.int32, (c, c), 0).astype(F32)
    colf = lax.broadcasted_iota(jnp.int32, (c, c), 1).astype(F32)
    lane = lax.broadcasted_iota(jnp.int32, (c, dk), 1)
    first_half = (lane % 64) < 32
    dirs = ((qf, kf, vf, cosf, sinf, yf_ref), (qb, kb, vb, cosb, sinb, yb_ref))
    for d, (q_ref, k_ref, v_ref, cos_ref, sin_ref, y_ref) in enumerate(dirs):
        cos = cos_ref[...]
        sin = sin_ref[...]
        diff = (rowf - colf) if d == 0 else (colf - rowf)
        pos = rowf if d == 0 else (c - 1.0) - rowf
        for h in range(B_HEADS):
            lg = lg_all[d * B_HEADS + h:d * B_HEADS + h + 1, :]
            qh = q_ref[:, h * dk:(h + 1) * dk].astype(F32)
            kh = k_ref[:, h * dk:(h + 1) * dk].astype(F32) * (dk ** -0.5)

            def rope(xv):
                swapped = jnp.where(first_half, pltpu.roll(xv, dk - 32, 1), pltpu.roll(xv, 32, 1))
                return xv * cos + swapped * sin

            qh = rope(qh)
            kh = rope(kh)
            vh = v_ref[:, h * dv:(h + 1) * dv]
            dmask = jnp.where(diff >= 0.0, jnp.exp(lg * jnp.maximum(diff, 0.0)), 0.0)
            scores = _dot_nt(_bf(qh), _bf(kh)) * dmask
            inner = _bdot(_bf(scores), vh)
            xi = jnp.exp(lg * (pos + 1.0))
            zeta = jnp.exp(lg * ((c - 1.0) - pos))
            r_prev = r_s[d, h]
            cross = _bdot(_bf(qh * xi), _bf(r_prev))
            y_ref[:, h * dv:(h + 1) * dv] = inner + cross
            kv = _dot_tn(_bf(kh * zeta), vh)
            g_chunk = jnp.exp(lg * float(c))
            r_s[d, h] = jnp.concatenate([g_chunk, g_chunk], axis=1) * r_prev + kv

    @pl.when(step == pl.num_programs(0) - 1)
    def _():
        rfin_ref[...] = r_s[...]


def _retention(proj, cos, sin, lgt, r0):
    t_len = proj.shape[0]
    c = RET_CHUNK
    nc = t_len // c
    qw = B_QK_WIDTH
    vw = B_V_WIDTH
    fq = lambda col: pl.BlockSpec((c, qw), lambda i: (i, col))
    bq = lambda col: pl.BlockSpec((c, qw), lambda i: (nc - 1 - i, col))
    st = pl.BlockSpec((2, B_HEADS, B_QK_DIM, B_V_DIM), lambda i: (0, 0, 0, 0))
    q_col = 0
    v_col = 2 * qw // vw
    return pl.pallas_call(
        _retention_kernel,
        out_shape=[jax.ShapeDtypeStruct((t_len, vw), F32), jax.ShapeDtypeStruct((t_len, vw), F32),
                   jax.ShapeDtypeStruct((2, B_HEADS, B_QK_DIM, B_V_DIM), F32)],
        grid=(nc,),
        in_specs=[
            fq(q_col), fq(q_col + 1), pl.BlockSpec((c, vw), lambda i: (i, v_col)),
            pl.BlockSpec((c, B_QK_DIM), lambda i: (i, 0)), pl.BlockSpec((c, B_QK_DIM), lambda i: (i, 0)),
            bq(q_col), bq(q_col + 1), pl.BlockSpec((c, vw), lambda i: (nc - 1 - i, v_col)),
            pl.BlockSpec((c, B_QK_DIM), lambda i: (nc - 1 - i, 0)),
            pl.BlockSpec((c, B_QK_DIM), lambda i: (nc - 1 - i, 0)),
            pl.BlockSpec((2 * B_HEADS, 128), lambda i: (0, 0)), st,
        ],
        out_specs=[pl.BlockSpec((c, vw), lambda i: (i, 0)), pl.BlockSpec((c, vw), lambda i: (nc - 1 - i, 0)), st],
        scratch_shapes=[pltpu.VMEM((2, B_HEADS, B_QK_DIM, B_V_DIM), F32)],
        compiler_params=_params(("arbitrary",)),
        name="retention",
    )(proj, proj, proj, cos, sin, proj, proj, proj, cos, sin, lgt, r0)


def _mix_out_kernel(of_ref, ob_ref, bonus_ref, g_ref, yf_ref, yb_ref, gb_ref, ga_ref, gbb_ref, x_ref,
                    alnw, alnb, rlnw, rlnb, npm, npf, g1_ref, sc2_ref, sh2_ref, wa_ref, wb_ref, wo_ref, bd_ref,
                    x1_ref, h2_ref, h3_ref):
    bd = bd_ref[...]
    o = of_ref[...] + ob_ref[...]
    mu = _dot_split_lhs(o, bd) * (1.0 / A_HEAD_DIM)
    oc = o - mu
    var = _dot_split_lhs(oc * oc, bd) * (1.0 / A_HEAD_DIM)
    ya = oc * lax.rsqrt(var + RWKV_GN_EPS) * alnw[...] + alnb[...] + bonus_ref[...]
    ya = _bdot(_bf(ya * g_ref[...]), wa_ref[...])

    y = yf_ref[...] + yb_ref[...]
    parts = []
    for h in range(B_HEADS):
        seg = y[:, h * B_V_DIM:(h + 1) * B_V_DIM]
        m = jnp.mean(seg, axis=-1, keepdims=True)
        sc = seg - m
        vr = jnp.mean(sc * sc, axis=-1, keepdims=True)
        parts.append(sc * lax.rsqrt(vr + RET_GN_EPS))
    yn = jnp.concatenate(parts, axis=1) * rlnw[...] + rlnb[...]
    gb = gb_ref[...].astype(F32)
    yb = _bdot(_bf(yn * (gb * _sigmoid(gb))), wb_ref[...])

    merged = _sigmoid(ga_ref[...].astype(F32)) * ya + _sigmoid(gbb_ref[...].astype(F32)) * yb
    mix = _bdot(_bf(merged), wo_ref[...])
    x1 = x_ref[...] + g1_ref[...] * (_rms(mix) * npm[...])
    x1_ref[...] = x1
    h2 = (_rms(x1) * npf[...]) * (1.0 + sc2_ref[...]) + sh2_ref[...]
    h2_ref[...] = h2
    _rows_to_packed_tiles(h2, h3_ref)


def _mix_out(x2d, proj, o_f, o_b, bonus, g, y_f, y_b, lp, vecs, bd):
    t_len = x2d.shape[0]
    tm = min(t_len, 256)
    aw = A_WIDTH
    d = D_MODEL
    ta = lambda: pl.BlockSpec((tm, aw), lambda i: (i, 0))
    td = lambda: pl.BlockSpec((tm, d), lambda i: (i, 0))
    pc = lambda c: pl.BlockSpec((tm, d), lambda i: (i, c))
    va = lambda: pl.BlockSpec((1, aw), lambda i: (0, 0))
    vd = lambda: pl.BlockSpec((1, d), lambda i: (0, 0))
    full = lambda shp: pl.BlockSpec(shp, lambda i: (0, 0))
    gcol = (2 * B_QK_WIDTH + B_V_WIDTH) // d
    return pl.pallas_call(
        _mix_out_kernel,
        out_shape=[jax.ShapeDtypeStruct((t_len, d), F32), jax.ShapeDtypeStruct((t_len, d), F32),
                   jax.ShapeDtypeStruct((t_len * PACK_TILE, LANES), jnp.uint32)],
        grid=(t_len // tm,),
        in_specs=[ta(), ta(), ta(), ta(), td(), td(), pc(gcol), pc(gcol + 1), pc(gcol + 2), td(),
                  va(), va(), vd(), vd(), vd(), vd(), vd(), vd(), vd(),
                  full((aw, d)), full((d, d)), full((d, d)), full((aw, aw))],
        out_specs=[td(), td(), pl.BlockSpec((tm * PACK_TILE, LANES), lambda i: (i, 0))],
        compiler_params=_params(("arbitrary",)),
        name="mix_out",
    )(o_f, o_b, bonus, g, y_f, y_b, proj, proj, proj, x2d,
      lp["ln_w"], lp["ln_b"], lp["ret_ln_w"], lp["ret_ln_b"], vecs["npm"], vecs["npf"], vecs["g1"], vecs["sc2"],
      vecs["sh2"], lp["w_a"], lp["w_b"], lp["w_o"], bd)


def _router_kernel(h_ref, rw_ref, bias_ref, sel_ref, wts_ref, rank_ref, cnt_ref, cnt_s, *, tm):
    step = pl.program_id(0)

    @pl.when(step == 0)
    def _():
        cnt_s[...] = jnp.zeros_like(cnt_s)

    ne = N_EXPERTS
    scores = _sigmoid(_dot3(h_ref[...], rw_ref[...]))
    work = scores + bias_ref[...]
    lane = lax.broadcasted_iota(jnp.int32, (tm, ne), 1).astype(F32)
    idxs = []
    vals = []
    for _ in range(TOP_K):
        m = jnp.max(work, axis=-1, keepdims=True)
        idx = jnp.min(jnp.where(work == m, lane, float(ne)), axis=-1, keepdims=True)
        oh = lane == idx
        vals.append(jnp.sum(jnp.where(oh, scores, 0.0), axis=-1, keepdims=True))
        idxs.append(idx)
        work = jnp.where(oh, -jnp.inf, work)
    sel_f = jnp.concatenate(idxs, axis=1)
    s_sel = jnp.concatenate(vals, axis=1)
    wts_ref[...] = s_sel / jnp.sum(s_sel, axis=1, keepdims=True) * ROUTED_SCALE
    sel_ref[...] = sel_f.astype(jnp.int32)

    hit = work == -jnp.inf
    r_i = lax.broadcasted_iota(jnp.int32, (tm, tm), 0)
    c_i = lax.broadcasted_iota(jnp.int32, (tm, tm), 1)
    before = _bdot((r_i > c_i).astype(BF16), hit.astype(BF16)) + cnt_s[...]
    ranks = [jnp.sum(jnp.where(lane == idxs[k], before, 0.0), axis=-1, keepdims=True) for k in range(TOP_K)]
    rank_ref[...] = jnp.concatenate(ranks, axis=1).astype(jnp.int32)
    cnt = cnt_s[...] + jnp.sum(hit.astype(F32), axis=0, keepdims=True)
    cnt_s[...] = cnt
    cnt_ref[...] = cnt.astype(jnp.int32)


def _router(h2, router_w, router_bias):
    t_len = h2.shape[0]
    tm = min(t_len, 256)
    kern = functools.partial(_router_kernel, tm=tm)
    tk = lambda: pl.BlockSpec((tm, TOP_K), lambda i: (i, 0))
    return pl.pallas_call(
        kern,
        out_shape=[jax.ShapeDtypeStruct((t_len, TOP_K), jnp.int32), jax.ShapeDtypeStruct((t_len, TOP_K), F32),
                   jax.ShapeDtypeStruct((t_len, TOP_K), jnp.int32), jax.ShapeDtypeStruct((1, N_EXPERTS), jnp.int32)],
        grid=(t_len // tm,),
        in_specs=[pl.BlockSpec((tm, D_MODEL), lambda i: (i, 0)),
                  pl.BlockSpec((D_MODEL, N_EXPERTS), lambda i: (0, 0)),
                  pl.BlockSpec((1, N_EXPERTS), lambda i: (0, 0))],
        out_specs=[tk(), tk(), tk(), pl.BlockSpec((1, N_EXPERTS), lambda i: (0, 0))],
        scratch_shapes=[pltpu.VMEM((1, N_EXPERTS), F32)],
        compiler_params=_params(("arbitrary",)),
        name="router",
    )(h2, router_w, router_bias.reshape(1, N_EXPERTS))


def _slot(sel_ref, rank_ref, start_ref, a):
    return start_ref[sel_ref[a]] + rank_ref[a]


def _dispatch_kernel(sel_ref, rank_ref, start_ref, cnt_ref, h_ref, xs_out, zrow, sem, zsem, *, tm):
    def row_copy(t, k):
        slot = _slot(sel_ref, rank_ref, start_ref, t * TOP_K + k)
        return pltpu.make_async_copy(_row_tile(h_ref, t), _row_tile(xs_out, slot), sem)

    def issue(t, carry):
        for k in range(TOP_K):
            row_copy(t, k).start(priority=k % 2)
        return carry

    lax.fori_loop(0, tm, issue, 0)

    def drain(t, carry):
        for k in range(TOP_K):
            row_copy(t, k).wait()
        return carry

    lax.fori_loop(0, tm, drain, 0)

    @pl.when(pl.program_id(0) == pl.num_programs(0) - 1)
    def _():
        zrow[...] = jnp.zeros_like(zrow)

        def for_pad_runs(e0, fn):
            def per_expert(e, carry):
                n = cnt_ref[e]
                n_pad = (n + MOE_BLOCK - 1) // MOE_BLOCK * MOE_BLOCK
                pad = n_pad - n
                off = start_ref[e] + n
                for bit in reversed(range(MOE_BLOCK.bit_length() - 1)):
                    size = 1 << bit
                    is_set = ((pad >> bit) & 1) == 1

                    @pl.when(is_set)
                    def _(off=off, size=size):
                        dst = xs_out.at[pl.ds(pl.multiple_of(off * PACK_TILE, PACK_TILE), size * PACK_TILE)]
                        fn(pltpu.make_async_copy(zrow.at[pl.ds(0, size * PACK_TILE)], dst, zsem))

                    off = off + jnp.where(is_set, size, 0)
                return carry

            lax.fori_loop(e0, e0 + ZERO_PAD_GROUP, per_expert, 0)

        def per_group(gi, carry):
            e0 = gi * ZERO_PAD_GROUP
            for_pad_runs(e0, lambda cp: cp.start())
            for_pad_runs(e0, lambda cp: cp.wait())
            return carry

        lax.fori_loop(0, N_EXPERTS // ZERO_PAD_GROUP, per_group, 0)


def _dispatch(h3, sel_flat, rank_flat, pad_start, counts, n_slots):
    t_len = h3.shape[0] // PACK_TILE
    tm = min(t_len, MOE_TOKEN_TILE)
    kern = functools.partial(_dispatch_kernel, tm=tm)
    smem_blk = lambda: pl.BlockSpec((tm * TOP_K,), lambda i: (i,), memory_space=pltpu.SMEM)
    smem_all = lambda: pl.BlockSpec((N_EXPERTS,), lambda i: (0,), memory_space=pltpu.SMEM)
    return pl.pallas_call(
        kern,
        out_shape=jax.ShapeDtypeStruct((n_slots * PACK_TILE, LANES), jnp.uint32),
        grid=(t_len // tm,),
        in_specs=[smem_blk(), smem_blk(), smem_all(), smem_all(),
                  pl.BlockSpec((tm * PACK_TILE, LANES), lambda i: (i, 0))],
        out_specs=pl.BlockSpec(memory_space=pl.ANY),
        scratch_shapes=[pltpu.VMEM((MOE_BLOCK // 2 * PACK_TILE, LANES), jnp.uint32), pltpu.SemaphoreType.DMA(()),
                        pltpu.SemaphoreType.DMA(())],
        compiler_params=_params(("arbitrary",)),
        name="dispatch",
    )(sel_flat, rank_flat, pad_start, counts, h3)


def _expert_kernel(be_ref, nu_ref, xs_ref, wg_ref, wu_ref, wd_ref, ys_ref, wg_s, wu_s, wd_s):
    b = pl.program_id(0)
    used = b < nu_ref[0]
    new_expert = (b == 0) | (be_ref[b] != be_ref[jnp.maximum(b - 1, 0)])

    @pl.when(used & new_expert)
    def _():
        wg_s[...] = _bf(wg_ref[0])
        wu_s[...] = _bf(wu_ref[0])
        wd_s[...] = _bf(wd_ref[0])

    @pl.when(used)
    def _():
        xb = _packed_tiles_to_rows(xs_ref, MOE_BLOCK)
        gate = _bdot(xb, wg_s[...])
        up = _bdot(xb, wu_s[...])
        act = gate * _sigmoid(gate) * up
        _rows_to_packed_tiles(_bdot(_bf(act), wd_s[...]), ys_ref)

    @pl.when(jnp.logical_not(used))
    def _():
        ys_ref[...] = jnp.zeros_like(ys_ref)


def _experts(xs, block_e, n_used, w_gate, w_up, w_down):
    n_slots = xs.shape[0] // PACK_TILE
    n_blocks = n_slots // MOE_BLOCK
    rows = (MOE_BLOCK * PACK_TILE, LANES)
    grid_spec = pltpu.PrefetchScalarGridSpec(
        num_scalar_prefetch=2,
        grid=(n_blocks,),
        in_specs=[
            pl.BlockSpec((MOE_BLOCK * PACK_TILE, LANES), lambda b, be, nu: (jnp.where(b < nu[0], b, 0), 0)),
            pl.BlockSpec((1, D_MODEL, EXPERT_FF), lambda b, be, nu: (be[b], 0, 0)),
            pl.BlockSpec((1, D_MODEL, EXPERT_FF), lambda b, be, nu: (be[b], 0, 0)),
            pl.BlockSpec((1, EXPERT_FF, D_MODEL), lambda b, be, nu: (be[b], 0, 0)),
        ],
        out_specs=pl.BlockSpec(rows, lambda b, be, nu: (b, 0)),
        scratch_shapes=[pltpu.VMEM((D_MODEL, EXPERT_FF), BF16), pltpu.VMEM((D_MODEL, EXPERT_FF), BF16),
                        pltpu.VMEM((EXPERT_FF, D_MODEL), BF16)],
    )
    return pl.pallas_call(
        _expert_kernel,
        out_shape=jax.ShapeDtypeStruct((n_slots * PACK_TILE, LANES), jnp.uint32),
        grid_spec=grid_spec,
        compiler_params=_params(("arbitrary",)),
        name="experts",
    )(block_e, n_used, xs, w_gate, w_up, w_down)


def _combine_kernel(sel_ref, rank_ref, start_ref, ys_ref, wts_ref, h_ref, x1_ref, sg_ref, su_ref, sd_ref, npo, g2_ref,
                    o_ref, buf, sem, *, tm):
    def row_copy(t, k):
        slot = _slot(sel_ref, rank_ref, start_ref, t * TOP_K + k)
        return pltpu.make_async_copy(_row_tile(ys_ref, slot), _row_tile(buf, t, (k,)), sem)

    def issue(t, carry):
        for k in range(TOP_K):
            row_copy(t, k).start(priority=k % 2)
        return carry

    lax.fori_loop(0, tm, issue, 0)

    hb = _bf(h_ref[...])
    gate = _bdot(hb, sg_ref[...])
    up = _bdot(hb, su_ref[...])
    shared = _bdot(_bf(gate * _sigmoid(gate) * up), sd_ref[...])

    def drain(t, carry):
        for k in range(TOP_K):
            row_copy(t, k).wait()
        return carry

    lax.fori_loop(0, tm, drain, 0)

    wts = wts_ref[...]
    routed = _packed_tiles_to_rows(buf, tm, (0,), F32) * wts[:, 0:1]
    for k in range(1, TOP_K):
        routed = routed + _packed_tiles_to_rows(buf, tm, (k,), F32) * wts[:, k:k + 1]
    o_ref[...] = x1_ref[...] + g2_ref[...] * (_rms(routed + shared) * npo[...])


def _combine(ys, sel_flat, rank_flat, pad_start, wts, h2, x1, sg, su, sd, npo, g2):
    t_len = h2.shape[0]
    tm = min(t_len, MOE_TOKEN_TILE)
    d = D_MODEL
    kern = functools.partial(_combine_kernel, tm=tm)
    smem_blk = lambda: pl.BlockSpec((tm * TOP_K,), lambda i: (i,), memory_space=pltpu.SMEM)
    td = lambda: pl.BlockSpec((tm, d), lambda i: (i, 0))
    vd = lambda: pl.BlockSpec((1, d), lambda i: (0, 0))
    return pl.pallas_call(
        kern,
        out_shape=jax.ShapeDtypeStruct((t_len, d), F32),
        grid=(t_len // tm,),
        in_specs=[smem_blk(), smem_blk(),
                  pl.BlockSpec((N_EXPERTS,), lambda i: (0,), memory_space=pltpu.SMEM),
                  pl.BlockSpec(memory_space=pl.ANY),
                  pl.BlockSpec((tm, TOP_K), lambda i: (i, 0)),
                  td(), td(),
                  pl.BlockSpec((d, SHARED_FF), lambda i: (0, 0)), pl.BlockSpec((d, SHARED_FF), lambda i: (0, 0)),
                  pl.BlockSpec((SHARED_FF, d), lambda i: (0, 0)), vd(), vd()],
        out_specs=td(),
        scratch_shapes=[pltpu.VMEM((TOP_K, tm * PACK_TILE, LANES), jnp.uint32), pltpu.SemaphoreType.DMA(())],
        compiler_params=_params(("arbitrary",)),
        name="combine",
    )(sel_flat, rank_flat, pad_start, ys, wts, h2, x1, sg, su, sd, npo, g2)


def _block_diag_ones(width, group):
    idx = np.arange(width) // group
    return jnp.asarray(idx[:, None] == idx[None, :], dtype=BF16)


def _rope_tables(t_len):
    n_rows = t_len // GRID_W
    quarter = B_QK_DIM // 4
    inv_freq = jnp.asarray(ROPE_BASE, F32) ** (-jnp.arange(quarter, dtype=F32) / quarter)
    ang_r = jnp.arange(n_rows, dtype=F32)[:, None] * inv_freq
    ang_c = jnp.arange(GRID_W, dtype=F32)[:, None] * inv_freq
    grid = (n_rows, GRID_W, quarter)
    by_row = lambda a: jnp.broadcast_to(a[:, None, :], grid)
    by_col = lambda a: jnp.broadcast_to(a[None, :, :], grid)
    cr, sr, cc, sc = by_row(jnp.cos(ang_r)), by_row(jnp.sin(ang_r)), by_col(jnp.cos(ang_c)), by_col(jnp.sin(ang_c))
    cos = jnp.concatenate([cr, cr, cc, cc], axis=-1).reshape(t_len, B_QK_DIM)
    sin = jnp.concatenate([-sr, sr, -sc, sc], axis=-1).reshape(t_len, B_QK_DIM)
    return cos, sin


def _token_mixer(x2d, g_pre, sc, sh, lp, cos, sin, states, bd):
    proj_a, proj_b = _in_proj(x2d, g_pre, sc, sh, lp["w_in"], lp["mu"])
    prep = _rwkv_prep(proj_a, lp, bd)
    o_f, o_b, s_fin = _rwkv_scan(proj_a, prep, states[0])
    y_f, y_b, r_fin = _retention(proj_b, cos, sin, lp["lgt"], states[1])
    return proj_b, prep, (o_f, o_b), (y_f, y_b), (s_fin, r_fin)


def kernel(x, c, ctx, c_ctx, w_mod, b_mod, norm_pre_mix, norm_post_mix, norm_pre_ffn, norm_post_ffn, w_in, shift_mu,
           rwkv_w0, rwkv_w_up, rwkv_a0, rwkv_a_up, rwkv_g_up, rwkv_k_k, rwkv_k_a, rwkv_r_k, rwkv_ln_w, rwkv_ln_b,
           w_branch_a, ret_decay_logit, ret_ln_w, ret_ln_b, w_branch_b, w_out, router_w, router_bias, exp_w_gate,
           exp_w_up, exp_w_down, sh_w_gate, sh_w_up, sh_w_down):
    d = D_MODEL
    assert x.shape[0] == 1 and w_in.shape[0] == 1, "single batch element, single layer"
    t_len = x.shape[1]
    x2d = x.reshape(t_len, d)
    ctx2d = ctx.reshape(ctx.shape[1], d)
    row = lambda a: a.reshape(1, -1)

    cs = jnp.zeros((8, d), F32).at[0].set(c[0]).at[1].set(c_ctx)
    mod = _modulation(cs, w_mod[0], b_mod[0])
    sh1, sc1, g1, sh2, sc2, g2 = [mod[0:1, i * d:(i + 1) * d] for i in range(6)]
    csh1, csc1 = mod[1:2, 0:d], mod[1:2, d:2 * d]

    lp = {
        "w_in": _w_in_layout(w_in[0]),
        "mu": jnp.pad(shift_mu[0], ((0, 0), (0, SHIFT_PAD - SHIFT_COLS))),
        "k_k": row(rwkv_k_k[0]), "k_a": row(rwkv_k_a[0]), "r_k": row(rwkv_r_k[0]),
        "w0": rwkv_w0[0], "w_up": rwkv_w_up[0], "a0": rwkv_a0[0], "a_up": rwkv_a_up[0], "g_up": rwkv_g_up[0],
        "ln_w": row(rwkv_ln_w[0]), "ln_b": row(rwkv_ln_b[0]),
        "ret_ln_w": row(ret_ln_w[0]), "ret_ln_b": row(ret_ln_b[0]),
        "lgt": jnp.broadcast_to(ret_decay_logit[0].reshape(2 * B_HEADS, 1), (2 * B_HEADS, 128)),
        "w_a": w_branch_a[0].astype(BF16), "w_b": w_branch_b[0].astype(BF16), "w_o": w_out[0].astype(BF16),
    }
    bd = _block_diag_ones(A_WIDTH, A_HEAD_DIM)
    g_pre = row(norm_pre_mix[0])

    t_ctx = ctx2d.shape[0]
    zero_states = (jnp.zeros(RWKV_STATE_SHAPE, F32),
                   jnp.zeros((2, B_HEADS, B_QK_DIM, B_V_DIM), F32))
    ones = jnp.ones((t_ctx, B_QK_DIM), F32)
    *_, ctx_states = _token_mixer(ctx2d, g_pre, csc1, csh1, lp, ones, jnp.zeros_like(ones), zero_states, bd)

    cos, sin = _rope_tables(t_len)
    proj, prep, (o_f, o_b), (y_f, y_b), _ = _token_mixer(x2d, g_pre, sc1, sh1, lp, cos, sin, ctx_states, bd)
    vecs = {"npm": row(norm_post_mix[0]), "npf": row(norm_pre_ffn[0]), "g1": g1, "sc2": sc2, "sh2": sh2}
    x1, h2, h3 = _mix_out(x2d, proj, o_f, o_b, prep[7], prep[8], y_f, y_b, lp, vecs, bd)

    sel, wts, rank, counts = _router(h2, router_w[0], router_bias[0])
    counts = counts.reshape(N_EXPERTS)
    padded = (counts + MOE_BLOCK - 1) // MOE_BLOCK * MOE_BLOCK
    pad_end = jnp.cumsum(padded)
    pad_start = (pad_end - padded).astype(jnp.int32)
    n_assign = t_len * TOP_K
    n_blocks = (n_assign + N_EXPERTS * (MOE_BLOCK - 1) + MOE_BLOCK - 1) // MOE_BLOCK
    block_start = jnp.arange(n_blocks, dtype=jnp.int32) * MOE_BLOCK
    block_e = jnp.minimum(jnp.sum(pad_end[None, :] <= block_start[:, None], axis=1), N_EXPERTS - 1).astype(jnp.int32)
    n_used = (pad_end[-1:] // MOE_BLOCK).astype(jnp.int32)
    sel_flat = sel.reshape(n_assign)
    rank_flat = rank.reshape(n_assign)

    xs = _dispatch(h3, sel_flat, rank_flat, pad_start, counts, n_blocks * MOE_BLOCK)
    ys = _experts(xs, block_e, n_used, exp_w_gate[0], exp_w_up[0], exp_w_down[0])
    out = _combine(ys, sel_flat, rank_flat, pad_start, wts, h2, x1, sh_w_gate[0].astype(BF16),
                   sh_w_up[0].astype(BF16), sh_w_down[0].astype(BF16), row(norm_post_ffn[0]), g2)
    return out.reshape(x.shape)
```

```python
import functools

import jax
import jax.numpy as jnp
import numpy as np
from jax import lax
from jax.experimental import pallas as pl
from jax.experimental.pallas import tpu as pltpu

F32 = jnp.float32
BF16 = jnp.bfloat16

D_MODEL = 1024
GRID_W = 64
NORM_EPS = 1e-6

A_HEAD_DIM = 64
A_WIDTH = D_MODEL // 2
A_HEADS = A_WIDTH // A_HEAD_DIM
W_LORA = 64
ICLR_LORA = 64
G_LORA = 128
RWKV_GN_EPS = 64e-5

B_HEADS = 4
B_QK_WIDTH = D_MODEL // 2
B_V_WIDTH = D_MODEL
B_QK_DIM = B_QK_WIDTH // B_HEADS
B_V_DIM = B_V_WIDTH // B_HEADS
RET_CHUNK = 128
RET_GN_EPS = 1e-5
ROPE_BASE = 10000.0

SHIFT_COLS = 3 * A_WIDTH + W_LORA + ICLR_LORA + G_LORA
LORA_COLS = W_LORA + ICLR_LORA + G_LORA
SHIFT_PAD = 2048
PROJ_COLS = SHIFT_PAD + 2 * B_QK_WIDTH + 2 * B_V_WIDTH + 2 * D_MODEL

N_EXPERTS = 256
TOP_K = 8
EXPERT_FF = D_MODEL // 4
SHARED_FF = D_MODEL // 4
ROUTED_SCALE = 2.5
MOE_BLOCK = 512
ZERO_PAD_GROUP = 16
MOE_TOKEN_TILE = 512

RWKV_CHUNK = 64
RWKV_STATE_SHAPE = (2, A_HEADS // 2, A_HEAD_DIM, 2 * A_HEAD_DIM)
PROJ_TN = 1024
PROJ_TM = 1024
VMEM_LIMIT = 48 * 1024 * 1024


def _params(sem):
    return pltpu.CompilerParams(dimension_semantics=sem, vmem_limit_bytes=VMEM_LIMIT)


def _bf(a):
    return a.astype(BF16)


def _bdot(a, b):
    return jnp.dot(a, b, preferred_element_type=F32)


def _dot_nt(a, b):
    return lax.dot_general(a, b, (((1,), (1,)), ((), ())), preferred_element_type=F32)


def _dot_tn(a, b):
    return lax.dot_general(a, b, (((0,), (0,)), ((), ())), preferred_element_type=F32)


def _split(a, n):
    out = []
    rem = a
    for _ in range(n):
        p = _bf(rem)
        out.append(p)
        rem = rem - p.astype(F32)
    return out


def _dot_split_lhs(a, b_bf, n=3):
    acc = None
    for p in _split(a, n):
        t = _bdot(p, b_bf)
        acc = t if acc is None else acc + t
    return acc


def _dot_split_rhs(a_bf, b, n=3):
    acc = None
    for p in _split(b, n):
        t = _bdot(a_bf, p)
        acc = t if acc is None else acc + t
    return acc


def _dot3(a, b):
    ah, al = _split(a, 2)
    bh, bl = _split(b, 2)
    return _bdot(ah, bh) + (_bdot(ah, bl) + _bdot(al, bh))


def _sigmoid(x):
    return 1.0 / (1.0 + jnp.exp(-x))


def _softplus(x):
    return jnp.maximum(x, 0.0) + jnp.log1p(jnp.exp(-jnp.abs(x)))


def _rms(x):
    return x * lax.rsqrt(jnp.mean(x * x, axis=-1, keepdims=True) + NORM_EPS)


LANES = 128


PACK_TILE = D_MODEL // LANES // 2
HIGH_HALF = 0xFFFF0000


def _row_tile(ref, r, lead=()):
    return ref.at[lead + (pl.ds(pl.multiple_of(r * PACK_TILE, PACK_TILE), PACK_TILE),)]


def _rows_to_packed_tiles(x2d, ref):
    n = x2d.shape[0]
    half = D_MODEL // 2
    for j in range(PACK_TILE):
        lo = _bf(x2d[:, j * LANES:(j + 1) * LANES]).astype(F32)
        hi = _bf(x2d[:, half + j * LANES:half + (j + 1) * LANES]).astype(F32)
        word = (pltpu.bitcast(hi, jnp.uint32) & jnp.uint32(HIGH_HALF)) | (pltpu.bitcast(lo, jnp.uint32) >> 16)
        ref[pl.ds(j, n, stride=PACK_TILE), :] = word


def _packed_tiles_to_rows(ref, n, lead=(), dtype=BF16):
    lo, hi = [], []
    for j in range(PACK_TILE):
        word = ref[lead + (pl.ds(j, n, stride=PACK_TILE), slice(None))]
        lo.append(pltpu.bitcast(word << 16, F32).astype(dtype))
        hi.append(pltpu.bitcast(word & jnp.uint32(HIGH_HALF), F32).astype(dtype))
    return jnp.concatenate(lo + hi, axis=1)


def _mod_kernel(cs_ref, w_ref, b_ref, o_ref):
    cs = cs_ref[...]
    s = cs * _sigmoid(cs)
    o_ref[...] = _dot3(s, w_ref[...]) + b_ref[...]


def _modulation(cs, w_mod, b_mod):
    n_out = w_mod.shape[1]
    tn = 1536
    return pl.pallas_call(
        _mod_kernel,
        out_shape=jax.ShapeDtypeStruct((8, n_out), F32),
        grid=(n_out // tn,),
        in_specs=[
            pl.BlockSpec((8, D_MODEL), lambda j: (0, 0)),
            pl.BlockSpec((D_MODEL, tn), lambda j: (0, j)),
            pl.BlockSpec((1, tn), lambda j: (0, j)),
        ],
        out_specs=pl.BlockSpec((8, tn), lambda j: (0, j)),
        compiler_params=_params(("arbitrary",)),
        name="modulation",
    )(cs, w_mod, b_mod.reshape(1, n_out))


W_PAD_TN = SHIFT_PAD - SHIFT_COLS


def _w_in_layout_kernel(w_ref, o_ref, *, gap_tile):
    w = w_ref[...]
    o_ref[...] = _bf(jnp.where(pl.program_id(0) == gap_tile, jnp.zeros_like(w), w))


def _w_in_layout(w_in):
    tn = W_PAD_TN
    gap_tile = SHIFT_COLS // tn
    kern = functools.partial(_w_in_layout_kernel, gap_tile=gap_tile)
    src = lambda j: (0, jnp.where(j < gap_tile, j, jnp.maximum(j - 1, 0)))
    return pl.pallas_call(
        kern,
        out_shape=jax.ShapeDtypeStruct((D_MODEL, PROJ_COLS), BF16),
        grid=(PROJ_COLS // tn,),
        in_specs=[pl.BlockSpec((D_MODEL, tn), src)],
        out_specs=pl.BlockSpec((D_MODEL, tn), lambda j: (0, j)),
        compiler_params=_params(("arbitrary",)),
        name="w_in_layout",
    )(w_in)


def _in_proj_kernel(x_ref, xp_ref, xn_ref, g_ref, sc_ref, sh_ref, w_ref, mu_ref, o_ref, ob_ref, h_s, hh_s,
                    *, tm, n_shift):
    i = pl.program_id(0)
    j = pl.program_id(1)
    n_i = pl.num_programs(0)

    def norm_mod(xv):
        return (_rms(xv) * g_ref[...]) * (1.0 + sc_ref[...]) + sh_ref[...]

    @pl.when(j == 0)
    def _():
        h_s[...] = _bf(norm_mod(x_ref[...]))
        hp = jnp.where(i > 0, norm_mod(xp_ref[...]), 0.0)
        hn = jnp.where(i < n_i - 1, norm_mod(xn_ref[...]), 0.0)
        hh_s[0:8, :] = hp
        hh_s[8:16, :] = hn

    @pl.when(j < n_shift)
    def _():
        w = w_ref[...]
        p = _bdot(h_s[...], w)
        ph = _bdot(_bf(hh_s[...]), w)
        row = lax.broadcasted_iota(jnp.int32, p.shape, 0)
        prev = jnp.where(row == 0, ph[7:8, :], pltpu.roll(p, 1, 0))
        nxt = jnp.where(row == tm - 1, ph[8:9, :], pltpu.roll(p, tm - 1, 0))
        mu = mu_ref[...]
        o_ref[...] = p + mu[0:1, :] * (prev - p) + mu[1:2, :] * (nxt - p)

    @pl.when(j >= n_shift)
    def _():
        ob_ref[...] = _bf(_bdot(h_s[...], w_ref[...]))


def _in_proj(x2d, g, sc, sh, w_bf, mu_pad):
    t_len = x2d.shape[0]
    tm = min(t_len, PROJ_TM)
    tn = PROJ_TN
    n_shift = SHIFT_PAD // tn
    tb8 = tm // 8
    nb8 = t_len // 8
    kern = functools.partial(_in_proj_kernel, tm=tm, n_shift=n_shift)
    vec = lambda: pl.BlockSpec((1, D_MODEL), lambda i, j: (0, 0))
    return pl.pallas_call(
        kern,
        out_shape=[jax.ShapeDtypeStruct((t_len, SHIFT_PAD), F32),
                   jax.ShapeDtypeStruct((t_len, PROJ_COLS - SHIFT_PAD), BF16)],
        grid=(t_len // tm, PROJ_COLS // tn),
        in_specs=[
            pl.BlockSpec((tm, D_MODEL), lambda i, j: (i, 0)),
            pl.BlockSpec((8, D_MODEL), lambda i, j: (jnp.maximum(i * tb8 - 1, 0), 0)),
            pl.BlockSpec((8, D_MODEL), lambda i, j: (jnp.minimum((i + 1) * tb8, nb8 - 1), 0)),
            vec(), vec(), vec(),
            pl.BlockSpec((D_MODEL, tn), lambda i, j: (0, j)),
            pl.BlockSpec((2, tn), lambda i, j: (0, jnp.minimum(j, n_shift - 1))),
        ],
        out_specs=[pl.BlockSpec((tm, tn), lambda i, j: (i, jnp.minimum(j, n_shift - 1))),
                   pl.BlockSpec((tm, tn), lambda i, j: (i, jnp.maximum(j - n_shift, 0)))],
        scratch_shapes=[pltpu.VMEM((tm, D_MODEL), BF16), pltpu.VMEM((16, D_MODEL), F32)],
        compiler_params=_params(("arbitrary", "arbitrary")),
        name="in_proj",
    )(x2d, x2d, x2d, g, sc, sh, w_bf, mu_pad)


def _rwkv_prep_kernel(r_ref, k_ref, v_ref, lora_ref, kk_w, ka_w, rk_w, w0_ref, wup_ref, a0_ref, aup_ref, gup_ref,
                      bd_ref, kk_o, lw0_o, lw1_o, kd0_o, kd1_o, b0_o, b1_o, bonus_o, g_o):
    r = r_ref[...]
    k = k_ref[...]
    v = v_ref[...]
    lora = lora_ref[...]
    xw = lora[:, 0:W_LORA]
    xa = lora[:, W_LORA:W_LORA + ICLR_LORA]
    xg = lora[:, W_LORA + ICLR_LORA:]
    bd = bd_ref[...]

    kk = k * kk_w[...]
    nrm = jnp.sqrt(_dot_split_lhs(kk * kk, bd))
    kk = kk / jnp.maximum(nrm, 1e-12)
    kk_o[...] = kk

    tw = jnp.tanh(xw)
    lw_outs = (lw0_o, lw1_o)
    kd_outs = (kd0_o, kd1_o)
    b_outs = (b0_o, b1_o)
    kd_sum = None
    for d in range(2):
        z = w0_ref[d:d + 1, :] + _dot3(tw, wup_ref[d])
        w_log = -_softplus(-z) - 0.5
        lw_outs[d][...] = -jnp.exp(w_log)
        a_d = _sigmoid(a0_ref[d:d + 1, :] + _dot3(xa, aup_ref[d]))
        kd = k * (1.0 + (a_d - 1.0) * ka_w[...])
        kd_outs[d][...] = kd
        b_outs[d][...] = kk * a_d
        kd_sum = kd if kd_sum is None else kd_sum + kd
    g_o[...] = _dot3(_sigmoid(xg), gup_ref[...])
    bonus_o[...] = _dot_split_lhs(r * kd_sum * rk_w[...], bd) * v


def _rwkv_prep(proj, lp, bd):
    t_len = proj.shape[0]
    tm = min(t_len, 512)
    aw = A_WIDTH
    col = lambda c: pl.BlockSpec((tm, aw), lambda i: (i, c))
    vec = lambda: pl.BlockSpec((1, aw), lambda i: (0, 0))
    full = lambda shp: pl.BlockSpec(shp, lambda i: (0,) * len(shp))
    outs = [jax.ShapeDtypeStruct((t_len, aw), F32)] * 9
    return pl.pallas_call(
        _rwkv_prep_kernel,
        out_shape=outs,
        grid=(t_len // tm,),
        in_specs=[
            col(0), col(1), col(2),
            pl.BlockSpec((tm, LORA_COLS), lambda i: (i, 3 * aw // LORA_COLS)),
            vec(), vec(), vec(),
            full((2, aw)), full((2, W_LORA, aw)), full((2, aw)), full((2, ICLR_LORA, aw)), full((G_LORA, aw)),
            full((aw, aw)),
        ],
        out_specs=[pl.BlockSpec((tm, aw), lambda i: (i, 0))] * 9,
        compiler_params=_params(("arbitrary",)),
        name="rwkv_prep",
    )(proj, proj, proj, proj, lp["k_k"], lp["k_a"], lp["r_k"], lp["w0"], lp["w_up"], lp["a0"], lp["a_up"],
      lp["g_up"], bd)


def _rwkv_scan_kernel(rf, vf, kkf, lwf, kdf, bf_, rb, vb, kkb, lwb, kdb, bb, s0_ref, of_ref, ob_ref, sfin_ref, s_s,
                      *, tb):
    step = pl.program_id(0)

    @pl.when(step == 0)
    def _():
        s_s[...] = s0_ref[...]

    c = RWKV_CHUNK
    n = A_HEAD_DIM
    assert c == n
    pw = 2 * n
    row = lax.broadcasted_iota(jnp.int32, (c, pw), 0)
    lane = lax.broadcasted_iota(jnp.int32, (c, pw), 1)
    col = lane % c
    first = lane < n
    eye = (row == col).astype(F32)
    r_c = lax.broadcasted_iota(jnp.int32, (c, c), 0)
    c_c = lax.broadcasted_iota(jnp.int32, (c, c), 1)
    n_ch = tb // c
    dirs = ((rf, vf, kkf, lwf, kdf, bf_, of_ref), (rb, vb, kkb, lwb, kdb, bb, ob_ref))
    pairs = range(A_HEADS // 2)
    psl = [slice(q * pw, (q + 1) * pw) for q in pairs]

    def bd(x):
        z = jnp.zeros_like(x)
        return jnp.concatenate([jnp.where(first, x, z), jnp.where(first, z, x)], axis=0)

    def chunk_terms(d, refs, ci):
        r_ref, v_ref, kk_ref, lw_ref, kd_ref, b_ref, _ = refs
        if d == 0:
            incl, strict, tri = row >= col, row > col, r_c >= c_c
        else:
            incl, strict, tri = row <= col, row < col, r_c <= c_c
        rows = slice(ci * c, (ci + 1) * c)
        lw = lw_ref[rows, :]
        r = r_ref[rows, :]
        kk = kk_ref[rows, :]
        kd = kd_ref[rows, :]
        b = b_ref[rows, :]
        cl = _dot_split_rhs(tri.astype(BF16), lw)
        cl_tot = cl[c - 1:c, :] if d == 0 else cl[0:1, :]
        e_neg = jnp.exp(-cl)
        e_end = jnp.exp(cl_tot - cl)
        g_tot = jnp.exp(cl_tot)
        rt = r * jnp.exp(cl)
        x_all = _bf(jnp.concatenate([-(kk * jnp.exp(cl - lw)), rt], axis=0))
        bt_all = _bf(b * e_neg)
        kt_all = _bf(kd * e_neg)
        return {
            "rows": rows, "strict": strict, "incl": incl,
            "x": [x_all[:, ps] for ps in psl],
            "z": [jnp.concatenate([bd(bt_all[:, ps]), bd(kt_all[:, ps])], axis=0) for ps in psl],
            "bd_bh": [bd(_bf(b * e_end)[:, ps]) for ps in psl],
            "bd_kh": [bd(_bf(kd * e_end)[:, ps]) for ps in psl],
            "bd_v": [bd(_bf(v_ref[rows, :])[:, ps]) for ps in psl],
            "rt": [rt[:, ps] for ps in psl],
            "g_tot": [g_tot[:, ps] for ps in psl],
        }

    chunks = [(d, cc) for cc in range(n_ch) for d in range(2)]
    prep = {(d, cc): chunk_terms(d, dirs[d], cc if d == 0 else n_ch - 1 - cc) for d, cc in chunks}
    units = [(k, q) for k in chunks for q in pairs]

    def per_unit(name, indexed=True):
        return [prep[k][name][q] if indexed else prep[k][name] for k, q in units]

    strict, incl = per_unit("strict", False), per_unit("incl", False)
    x_u, bd_bh, bd_kh, bd_v, rt_u = (per_unit(s) for s in ("x", "bd_bh", "bd_kh", "bd_v", "rt"))
    g = [_dot_nt(x, z) for x, z in zip(x_u, per_unit("z"))]
    a_ab = [jnp.where(s, gi[:c, :pw], 0.0) for s, gi in zip(strict, g)]
    a_rb = [_bf(jnp.where(i, gi[c:, :pw], 0.0)) for i, gi in zip(incl, g)]
    a_k = [_bf(jnp.concatenate([jnp.where(s, gi[:c, pw:], 0.0), jnp.where(i, gi[c:, pw:], 0.0)], axis=0))
           for s, i, gi in zip(strict, incl, g)]
    a2 = [_bdot(_bf(a), bd(_bf(a))) for a in a_ab]
    pa = [jnp.concatenate([eye + a, sq], axis=0) for a, sq in zip(a_ab, a2)]
    for _ in range(4):
        nxt = [_bdot(_bf(x), bd(_bf(x[c:]))) for x in pa]
        pa = [jnp.concatenate([x[:c] + y[:c], y[c:]], axis=0) for x, y in zip(pa, nxt)]
    p = [_bf(x[:c] + _bdot(_bf(x[:c]), bd(_bf(x[c:])))) for x in pa]
    vk = [_bdot(a, v) for a, v in zip(a_k, bd_v)]
    tw = [_bf(_bdot(pi, jnp.concatenate([bd(x[:c]), bd(_bf(vki[:c]))], axis=1)))
          for pi, vki, x in zip(p, vk, x_u)]
    bd_at = [bd(t[:, :pw]) for t in tw]
    bd_w = [bd(t[:, pw:]) for t in tw]
    mt = [_bf(_dot_tn(a, bh)) for a, bh in zip(bd_at, bd_bh)]
    ntf = [_dot_tn(jnp.concatenate([w, v], axis=0), jnp.concatenate([bh, kh], axis=0))
           for w, v, bh, kh in zip(bd_w, bd_v, bd_bh, bd_kh)]
    nt = [m[:n] + m[n:] for m in ntf]
    rw = [_bdot(a, jnp.concatenate([at, w], axis=1)) for a, at, w in zip(a_rb, bd_at, bd_w)]
    ry = [_bf(r + w[:, :pw]) for w, r in zip(rw, rt_u)]
    y0 = [w[:, pw:] + vki[c:] for w, vki in zip(rw, vk)]
    term = {u: vals for u, vals in zip(units, zip(mt, nt, ry, y0, per_unit("g_tot")))}

    state = [[s_s[d, q] for q in pairs] for d in range(2)]
    dq = [(d, q) for d in range(2) for q in pairs]
    for cc in range(n_ch):
        mt_c, nt_c, ry_c, y0_c, gt_c = zip(*[term[((d, cc), q)] for d, q in dq])
        s0 = [state[d][q] for d, q in dq]
        ys = [_dot_nt(ry_c[i], bd(_bf(s0[i]))) + y0_c[i] for i in range(len(dq))]
        upd = [_bdot(jnp.concatenate(_split(s0[i], 2), axis=0), mt_c[i]) for i in range(len(dq))]
        for i, (d, q) in enumerate(dq):
            state[d][q] = s0[i] * gt_c[i] + (upd[i][:n] + upd[i][n:]) + nt_c[i]
        for d in range(2):
            y_d = [ys[i] for i, (dd, _) in enumerate(dq) if dd == d]
            dirs[d][6][prep[(d, cc)]["rows"], :] = jnp.concatenate(y_d, axis=1)
    for d in range(2):
        for q in pairs:
            s_s[d, q] = state[d][q]

    @pl.when(step == pl.num_programs(0) - 1)
    def _():
        sfin_ref[...] = s_s[...]


def _rwkv_scan(proj, prep, s0):
    kk, lw0, lw1, kd0, kd1, b0, b1 = prep[:7]
    t_len = proj.shape[0]
    tb = 4 * RWKV_CHUNK
    nb = t_len // tb
    aw = A_WIDTH
    fwd = lambda c: pl.BlockSpec((tb, aw), lambda i: (i, c))
    bwd = lambda c: pl.BlockSpec((tb, aw), lambda i: (nb - 1 - i, c))
    st = pl.BlockSpec(RWKV_STATE_SHAPE, lambda i: (0, 0, 0, 0))
    kern = functools.partial(_rwkv_scan_kernel, tb=tb)
    return pl.pallas_call(
        kern,
        out_shape=[jax.ShapeDtypeStruct((t_len, aw), F32), jax.ShapeDtypeStruct((t_len, aw), F32),
                   jax.ShapeDtypeStruct(RWKV_STATE_SHAPE, F32)],
        grid=(nb,),
        in_specs=[fwd(0), fwd(2), fwd(0), fwd(0), fwd(0), fwd(0),
                  bwd(0), bwd(2), bwd(0), bwd(0), bwd(0), bwd(0), st],
        out_specs=[fwd(0), bwd(0), st],
        scratch_shapes=[pltpu.VMEM(RWKV_STATE_SHAPE, F32)],
        compiler_params=_params(("arbitrary",)),
        name="rwkv_scan",
    )(proj, proj, kk, lw0, kd0, b0, proj, proj, kk, lw1, kd1, b1, s0)


def _retention_kernel(qf, kf, vf, cosf, sinf, qb, kb, vb, cosb, sinb, lgt_ref, r0_ref, yf_ref, yb_ref, rfin_ref, r_s,
                      *, n_ch):
    step = pl.program_id(0)

    @pl.when(step == 0)
    def _():
        r_s[...] = r0_ref[...]

    c = RET_CHUNK
    dk = B_QK_DIM
    dv = B_V_DIM
    lg_all = -_softplus(-lgt_ref[...])
    rowf = lax.broadcasted_iota(jnp.int32, (c, c), 0).astype(F32)
    colf = lax.broadcasted_iota(jnp.int32, (c, c), 1).astype(F32)
    lane = lax.broadcasted_iota(jnp.int32, (c, dk), 1)
    first_half = (lane % 64) < 32
    dirs = ((qf, kf, vf, cosf, sinf, yf_ref), (qb, kb, vb, cosb, sinb, yb_ref))
    units = [(d, h) for d in range(2) for h in range(B_HEADS)]

    lg = [lg_all[d * B_HEADS + h:d * B_HEADS + h + 1, :] for d, h in units]
    diff = [(rowf - colf) if d == 0 else (colf - rowf) for d, _ in units]
    pos = [rowf if d == 0 else (c - 1.0) - rowf for d, _ in units]
    dmask = [jnp.where(df >= 0.0, jnp.exp(l * jnp.maximum(df, 0.0)), 0.0) for l, df in zip(lg, diff)]
    xi = [jnp.exp(l * (p + 1.0)) for l, p in zip(lg, pos)]
    zeta = [jnp.exp(l * ((c - 1.0) - p)) for l, p in zip(lg, pos)]
    g_chunk = [jnp.exp(l * float(c)) for l in lg]
    g_chunk = [jnp.concatenate([g, g], axis=1) for g in g_chunk]

    state = [r_s[d, h] for d, h in units]
    for cc in range(n_ch):
        rows = [slice((cc if d == 0 else n_ch - 1 - cc) * c, (cc if d == 0 else n_ch - 1 - cc) * c + c) for d in range(2)]
        cos = [dirs[d][3][rows[d], :] for d in range(2)]
        sin = [dirs[d][4][rows[d], :] for d in range(2)]

        def rope(xv, d):
            swapped = jnp.where(first_half, pltpu.roll(xv, dk - 32, 1), pltpu.roll(xv, 32, 1))
            return xv * cos[d] + swapped * sin[d]

        qh = [rope(dirs[d][0][rows[d], h * dk:(h + 1) * dk].astype(F32), d) for d, h in units]
        kh = [rope(dirs[d][1][rows[d], h * dk:(h + 1) * dk].astype(F32) * (dk ** -0.5), d) for d, h in units]
        vh = [dirs[d][2][rows[d], h * dv:(h + 1) * dv] for d, h in units]
        scores = [_dot_nt(_bf(q), _bf(k)) * m for q, k, m in zip(qh, kh, dmask)]
        inner = [_bdot(_bf(s), v) for s, v in zip(scores, vh)]
        cross = [_bdot(_bf(q * x), _bf(r)) for q, x, r in zip(qh, xi, state)]
        kv = [_dot_tn(_bf(k * z), v) for k, z, v in zip(kh, zeta, vh)]
        for i, (d, h) in enumerate(units):
            dirs[d][5][rows[d], h * dv:(h + 1) * dv] = inner[i] + cross[i]
        state = [g * r + x for g, r, x in zip(g_chunk, state, kv)]
    for i, (d, h) in enumerate(units):
        r_s[d, h] = state[i]

    @pl.when(step == pl.num_programs(0) - 1)
    def _():
        rfin_ref[...] = r_s[...]


def _retention(proj, cos, sin, lgt, r0):
    t_len = proj.shape[0]
    n_ch = 2
    c = n_ch * RET_CHUNK
    nc = t_len // c
    qw = B_QK_WIDTH
    vw = B_V_WIDTH
    fq = lambda col: pl.BlockSpec((c, qw), lambda i: (i, col))
    bq = lambda col: pl.BlockSpec((c, qw), lambda i: (nc - 1 - i, col))
    st = pl.BlockSpec((2, B_HEADS, B_QK_DIM, B_V_DIM), lambda i: (0, 0, 0, 0))
    q_col = 0
    v_col = 2 * qw // vw
    return pl.pallas_call(
        functools.partial(_retention_kernel, n_ch=n_ch),
        out_shape=[jax.ShapeDtypeStruct((t_len, vw), F32), jax.ShapeDtypeStruct((t_len, vw), F32),
                   jax.ShapeDtypeStruct((2, B_HEADS, B_QK_DIM, B_V_DIM), F32)],
        grid=(nc,),
        in_specs=[
            fq(q_col), fq(q_col + 1), pl.BlockSpec((c, vw), lambda i: (i, v_col)),
            pl.BlockSpec((c, B_QK_DIM), lambda i: (i, 0)), pl.BlockSpec((c, B_QK_DIM), lambda i: (i, 0)),
            bq(q_col), bq(q_col + 1), pl.BlockSpec((c, vw), lambda i: (nc - 1 - i, v_col)),
            pl.BlockSpec((c, B_QK_DIM), lambda i: (nc - 1 - i, 0)),
            pl.BlockSpec((c, B_QK_DIM), lambda i: (nc - 1 - i, 0)),
            pl.BlockSpec((2 * B_HEADS, 128), lambda i: (0, 0)), st,
        ],
        out_specs=[pl.BlockSpec((c, vw), lambda i: (i, 0)), pl.BlockSpec((c, vw), lambda i: (nc - 1 - i, 0)), st],
        scratch_shapes=[pltpu.VMEM((2, B_HEADS, B_QK_DIM, B_V_DIM), F32)],
        compiler_params=_params(("arbitrary",)),
        name="retention",
    )(proj, proj, proj, cos, sin, proj, proj, proj, cos, sin, lgt, r0)


def _mix_out_kernel(of_ref, ob_ref, bonus_ref, g_ref, yf_ref, yb_ref, gb_ref, ga_ref, gbb_ref, x_ref,
                    alnw, alnb, rlnw, rlnb, npm, npf, g1_ref, sc2_ref, sh2_ref, wa_ref, wb_ref, wo_ref, bd_ref,
                    x1_ref, h2_ref, h3_ref):
    bd = bd_ref[...]
    o = of_ref[...] + ob_ref[...]
    mu = _dot_split_lhs(o, bd) * (1.0 / A_HEAD_DIM)
    oc = o - mu
    var = _dot_split_lhs(oc * oc, bd) * (1.0 / A_HEAD_DIM)
    ya = oc * lax.rsqrt(var + RWKV_GN_EPS) * alnw[...] + alnb[...] + bonus_ref[...]
    ya = _bdot(_bf(ya * g_ref[...]), wa_ref[...])

    y = yf_ref[...] + yb_ref[...]
    parts = []
    for h in range(B_HEADS):
        seg = y[:, h * B_V_DIM:(h + 1) * B_V_DIM]
        m = jnp.mean(seg, axis=-1, keepdims=True)
        sc = seg - m
        vr = jnp.mean(sc * sc, axis=-1, keepdims=True)
        parts.append(sc * lax.rsqrt(vr + RET_GN_EPS))
    yn = jnp.concatenate(parts, axis=1) * rlnw[...] + rlnb[...]
    gb = gb_ref[...].astype(F32)
    yb = _bdot(_bf(yn * (gb * _sigmoid(gb))), wb_ref[...])

    merged = _sigmoid(ga_ref[...].astype(F32)) * ya + _sigmoid(gbb_ref[...].astype(F32)) * yb
    mix = _bdot(_bf(merged), wo_ref[...])
    x1 = x_ref[...] + g1_ref[...] * (_rms(mix) * npm[...])
    x1_ref[...] = x1
    h2 = (_rms(x1) * npf[...]) * (1.0 + sc2_ref[...]) + sh2_ref[...]
    h2_ref[...] = h2
    _rows_to_packed_tiles(h2, h3_ref)


def _mix_out(x2d, proj, o_f, o_b, bonus, g, y_f, y_b, lp, vecs, bd):
    t_len = x2d.shape[0]
    tm = min(t_len, 256)
    aw = A_WIDTH
    d = D_MODEL
    ta = lambda: pl.BlockSpec((tm, aw), lambda i: (i, 0))
    td = lambda: pl.BlockSpec((tm, d), lambda i: (i, 0))
    pc = lambda c: pl.BlockSpec((tm, d), lambda i: (i, c))
    va = lambda: pl.BlockSpec((1, aw), lambda i: (0, 0))
    vd = lambda: pl.BlockSpec((1, d), lambda i: (0, 0))
    full = lambda shp: pl.BlockSpec(shp, lambda i: (0, 0))
    gcol = (2 * B_QK_WIDTH + B_V_WIDTH) // d
    return pl.pallas_call(
        _mix_out_kernel,
        out_shape=[jax.ShapeDtypeStruct((t_len, d), F32), jax.ShapeDtypeStruct((t_len, d), F32),
                   jax.ShapeDtypeStruct((t_len * PACK_TILE, LANES), jnp.uint32)],
        grid=(t_len // tm,),
        in_specs=[ta(), ta(), ta(), ta(), td(), td(), pc(gcol), pc(gcol + 1), pc(gcol + 2), td(),
                  va(), va(), vd(), vd(), vd(), vd(), vd(), vd(), vd(),
                  full((aw, d)), full((d, d)), full((d, d)), full((aw, aw))],
        out_specs=[td(), td(), pl.BlockSpec((tm * PACK_TILE, LANES), lambda i: (i, 0))],
        compiler_params=_params(("arbitrary",)),
        name="mix_out",
    )(o_f, o_b, bonus, g, y_f, y_b, proj, proj, proj, x2d,
      lp["ln_w"], lp["ln_b"], lp["ret_ln_w"], lp["ret_ln_b"], vecs["npm"], vecs["npf"], vecs["g1"], vecs["sc2"],
      vecs["sh2"], lp["w_a"], lp["w_b"], lp["w_o"], bd)


def _router_kernel(h_ref, rw_ref, bias_ref, sel_ref, wts_ref, rank_ref, cnt_ref, cnt_s, *, tm):
    step = pl.program_id(0)

    @pl.when(step == 0)
    def _():
        cnt_s[...] = jnp.zeros_like(cnt_s)

    ne = N_EXPERTS
    scores = _sigmoid(_dot3(h_ref[...], rw_ref[...]))
    work = scores + bias_ref[...]
    lane = lax.broadcasted_iota(jnp.int32, (tm, ne), 1).astype(F32)
    idxs = []
    vals = []
    for _ in range(TOP_K):
        m = jnp.max(work, axis=-1, keepdims=True)
        idx = jnp.min(jnp.where(work == m, lane, float(ne)), axis=-1, keepdims=True)
        oh = lane == idx
        vals.append(jnp.sum(jnp.where(oh, scores, 0.0), axis=-1, keepdims=True))
        idxs.append(idx)
        work = jnp.where(oh, -jnp.inf, work)
    sel_f = jnp.concatenate(idxs, axis=1)
    s_sel = jnp.concatenate(vals, axis=1)
    wts_ref[...] = s_sel / jnp.sum(s_sel, axis=1, keepdims=True) * ROUTED_SCALE
    sel_ref[...] = sel_f.astype(jnp.int32)

    hit = work == -jnp.inf
    r_i = lax.broadcasted_iota(jnp.int32, (tm, tm), 0)
    c_i = lax.broadcasted_iota(jnp.int32, (tm, tm), 1)
    before = _bdot((r_i > c_i).astype(BF16), hit.astype(BF16)) + cnt_s[...]
    ranks = [jnp.sum(jnp.where(lane == idxs[k], before, 0.0), axis=-1, keepdims=True) for k in range(TOP_K)]
    rank_ref[...] = jnp.concatenate(ranks, axis=1).astype(jnp.int32)
    cnt = cnt_s[...] + jnp.sum(hit.astype(F32), axis=0, keepdims=True)
    cnt_s[...] = cnt
    cnt_ref[...] = cnt.astype(jnp.int32)


def _router(h2, router_w, router_bias):
    t_len = h2.shape[0]
    tm = min(t_len, 256)
    kern = functools.partial(_router_kernel, tm=tm)
    tk = lambda: pl.BlockSpec((tm, TOP_K), lambda i: (i, 0))
    return pl.pallas_call(
        kern,
        out_shape=[jax.ShapeDtypeStruct((t_len, TOP_K), jnp.int32), jax.ShapeDtypeStruct((t_len, TOP_K), F32),
                   jax.ShapeDtypeStruct((t_len, TOP_K), jnp.int32), jax.ShapeDtypeStruct((1, N_EXPERTS), jnp.int32)],
        grid=(t_len // tm,),
        in_specs=[pl.BlockSpec((tm, D_MODEL), lambda i: (i, 0)),
                  pl.BlockSpec((D_MODEL, N_EXPERTS), lambda i: (0, 0)),
                  pl.BlockSpec((1, N_EXPERTS), lambda i: (0, 0))],
        out_specs=[tk(), tk(), tk(), pl.BlockSpec((1, N_EXPERTS), lambda i: (0, 0))],
        scratch_shapes=[pltpu.VMEM((1, N_EXPERTS), F32)],
        compiler_params=_params(("arbitrary",)),
        name="router",
    )(h2, router_w, router_bias.reshape(1, N_EXPERTS))


def _slot(sel_ref, rank_ref, start_ref, a):
    return start_ref[sel_ref[a]] + rank_ref[a]


def _dispatch_kernel(sel_ref, rank_ref, start_ref, cnt_ref, h_ref, xs_out, zrow, sem, zsem, *, tm):
    def row_copy(t, k):
        slot = _slot(sel_ref, rank_ref, start_ref, t * TOP_K + k)
        return pltpu.make_async_copy(_row_tile(h_ref, t), _row_tile(xs_out, slot), sem)

    def issue(t, carry):
        for k in range(TOP_K):
            row_copy(t, k).start(priority=k % 2)
        return carry

    lax.fori_loop(0, tm, issue, 0)

    def drain(t, carry):
        for k in range(TOP_K):
            row_copy(t, k).wait()
        return carry

    lax.fori_loop(0, tm, drain, 0)

    @pl.when(pl.program_id(0) == pl.num_programs(0) - 1)
    def _():
        zrow[...] = jnp.zeros_like(zrow)

        def for_pad_runs(e0, fn):
            def per_expert(e, carry):
                n = cnt_ref[e]
                n_pad = (n + MOE_BLOCK - 1) // MOE_BLOCK * MOE_BLOCK
                pad = n_pad - n
                off = start_ref[e] + n
                for bit in reversed(range(MOE_BLOCK.bit_length() - 1)):
                    size = 1 << bit
                    is_set = ((pad >> bit) & 1) == 1

                    @pl.when(is_set)
                    def _(off=off, size=size):
                        dst = xs_out.at[pl.ds(pl.multiple_of(off * PACK_TILE, PACK_TILE), size * PACK_TILE)]
                        fn(pltpu.make_async_copy(zrow.at[pl.ds(0, size * PACK_TILE)], dst, zsem))

                    off = off + jnp.where(is_set, size, 0)
                return carry

            lax.fori_loop(e0, e0 + ZERO_PAD_GROUP, per_expert, 0)

        def per_group(gi, carry):
            e0 = gi * ZERO_PAD_GROUP
            for_pad_runs(e0, lambda cp: cp.start())
            for_pad_runs(e0, lambda cp: cp.wait())
            return carry

        lax.fori_loop(0, N_EXPERTS // ZERO_PAD_GROUP, per_group, 0)


def _dispatch(h3, sel_flat, rank_flat, pad_start, counts, n_slots):
    t_len = h3.shape[0] // PACK_TILE
    tm = min(t_len, MOE_TOKEN_TILE)
    kern = functools.partial(_dispatch_kernel, tm=tm)
    smem_blk = lambda: pl.BlockSpec((tm * TOP_K,), lambda i: (i,), memory_space=pltpu.SMEM)
    smem_all = lambda: pl.BlockSpec((N_EXPERTS,), lambda i: (0,), memory_space=pltpu.SMEM)
    return pl.pallas_call(
        kern,
        out_shape=jax.ShapeDtypeStruct((n_slots * PACK_TILE, LANES), jnp.uint32),
        grid=(t_len // tm,),
        in_specs=[smem_blk(), smem_blk(), smem_all(), smem_all(),
                  pl.BlockSpec((tm * PACK_TILE, LANES), lambda i: (i, 0))],
        out_specs=pl.BlockSpec(memory_space=pl.ANY),
        scratch_shapes=[pltpu.VMEM((MOE_BLOCK // 2 * PACK_TILE, LANES), jnp.uint32), pltpu.SemaphoreType.DMA(()),
                        pltpu.SemaphoreType.DMA(())],
        compiler_params=_params(("arbitrary",)),
        name="dispatch",
    )(sel_flat, rank_flat, pad_start, counts, h3)


def _expert_kernel(be_ref, nu_ref, xs_ref, wg_ref, wu_ref, wd_ref, ys_ref, wg_s, wu_s, wd_s):
    b = pl.program_id(0)
    used = b < nu_ref[0]
    new_expert = (b == 0) | (be_ref[b] != be_ref[jnp.maximum(b - 1, 0)])

    @pl.when(used & new_expert)
    def _():
        wg_s[...] = _bf(wg_ref[0])
        wu_s[...] = _bf(wu_ref[0])
        wd_s[...] = _bf(wd_ref[0])

    @pl.when(used)
    def _():
        xb = _packed_tiles_to_rows(xs_ref, MOE_BLOCK)
        gate = _bdot(xb, wg_s[...])
        up = _bdot(xb, wu_s[...])
        act = gate * _sigmoid(gate) * up
        _rows_to_packed_tiles(_bdot(_bf(act), wd_s[...]), ys_ref)

    @pl.when(jnp.logical_not(used))
    def _():
        ys_ref[...] = jnp.zeros_like(ys_ref)


def _experts(xs, block_e, n_used, w_gate, w_up, w_down):
    n_slots = xs.shape[0] // PACK_TILE
    n_blocks = n_slots // MOE_BLOCK
    rows = (MOE_BLOCK * PACK_TILE, LANES)
    grid_spec = pltpu.PrefetchScalarGridSpec(
        num_scalar_prefetch=2,
        grid=(n_blocks,),
        in_specs=[
            pl.BlockSpec((MOE_BLOCK * PACK_TILE, LANES), lambda b, be, nu: (jnp.where(b < nu[0], b, 0), 0)),
            pl.BlockSpec((1, D_MODEL, EXPERT_FF), lambda b, be, nu: (be[b], 0, 0)),
            pl.BlockSpec((1, D_MODEL, EXPERT_FF), lambda b, be, nu: (be[b], 0, 0)),
            pl.BlockSpec((1, EXPERT_FF, D_MODEL), lambda b, be, nu: (be[b], 0, 0)),
        ],
        out_specs=pl.BlockSpec(rows, lambda b, be, nu: (b, 0)),
        scratch_shapes=[pltpu.VMEM((D_MODEL, EXPERT_FF), BF16), pltpu.VMEM((D_MODEL, EXPERT_FF), BF16),
                        pltpu.VMEM((EXPERT_FF, D_MODEL), BF16)],
    )
    return pl.pallas_call(
        _expert_kernel,
        out_shape=jax.ShapeDtypeStruct((n_slots * PACK_TILE, LANES), jnp.uint32),
        grid_spec=grid_spec,
        compiler_params=_params(("arbitrary",)),
        name="experts",
    )(block_e, n_used, xs, w_gate, w_up, w_down)


def _combine_kernel(sel_ref, rank_ref, start_ref, ys_ref, wts_ref, h_ref, x1_ref, sg_ref, su_ref, sd_ref, npo, g2_ref,
                    o_ref, buf, sem, *, tm):
    def row_copy(t, k):
        slot = _slot(sel_ref, rank_ref, start_ref, t * TOP_K + k)
        return pltpu.make_async_copy(_row_tile(ys_ref, slot), _row_tile(buf, t, (k,)), sem)

    def issue(t, carry):
        for k in range(TOP_K):
            row_copy(t, k).start(priority=k % 2)
        return carry

    lax.fori_loop(0, tm, issue, 0)

    hb = _bf(h_ref[...])
    gate = _bdot(hb, sg_ref[...])
    up = _bdot(hb, su_ref[...])
    shared = _bdot(_bf(gate * _sigmoid(gate) * up), sd_ref[...])

    def drain(t, carry):
        for k in range(TOP_K):
            row_copy(t, k).wait()
        return carry

    lax.fori_loop(0, tm, drain, 0)

    wts = wts_ref[...]
    routed = _packed_tiles_to_rows(buf, tm, (0,), F32) * wts[:, 0:1]
    for k in range(1, TOP_K):
        routed = routed + _packed_tiles_to_rows(buf, tm, (k,), F32) * wts[:, k:k + 1]
    o_ref[...] = x1_ref[...] + g2_ref[...] * (_rms(routed + shared) * npo[...])


def _combine(ys, sel_flat, rank_flat, pad_start, wts, h2, x1, sg, su, sd, npo, g2):
    t_len = h2.shape[0]
    tm = min(t_len, MOE_TOKEN_TILE)
    d = D_MODEL
    kern = functools.partial(_combine_kernel, tm=tm)
    smem_blk = lambda: pl.BlockSpec((tm * TOP_K,), lambda i: (i,), memory_space=pltpu.SMEM)
    td = lambda: pl.BlockSpec((tm, d), lambda i: (i, 0))
    vd = lambda: pl.BlockSpec((1, d), lambda i: (0, 0))
    return pl.pallas_call(
        kern,
        out_shape=jax.ShapeDtypeStruct((t_len, d), F32),
        grid=(t_len // tm,),
        in_specs=[smem_blk(), smem_blk(),
                  pl.BlockSpec((N_EXPERTS,), lambda i: (0,), memory_space=pltpu.SMEM),
                  pl.BlockSpec(memory_space=pl.ANY),
                  pl.BlockSpec((tm, TOP_K), lambda i: (i, 0)),
                  td(), td(),
                  pl.BlockSpec((d, SHARED_FF), lambda i: (0, 0)), pl.BlockSpec((d, SHARED_FF), lambda i: (0, 0)),
                  pl.BlockSpec((SHARED_FF, d), lambda i: (0, 0)), vd(), vd()],
        out_specs=td(),
        scratch_shapes=[pltpu.VMEM((TOP_K, tm * PACK_TILE, LANES), jnp.uint32), pltpu.SemaphoreType.DMA(())],
        compiler_params=_params(("arbitrary",)),
        name="combine",
    )(sel_flat, rank_flat, pad_start, ys, wts, h2, x1, sg, su, sd, npo, g2)


def _block_diag_ones(width, group):
    idx = np.arange(width) // group
    return jnp.asarray(idx[:, None] == idx[None, :], dtype=BF16)


def _rope_tables(t_len):
    n_rows = t_len // GRID_W
    quarter = B_QK_DIM // 4
    inv_freq = jnp.asarray(ROPE_BASE, F32) ** (-jnp.arange(quarter, dtype=F32) / quarter)
    ang_r = jnp.arange(n_rows, dtype=F32)[:, None] * inv_freq
    ang_c = jnp.arange(GRID_W, dtype=F32)[:, None] * inv_freq
    grid = (n_rows, GRID_W, quarter)
    by_row = lambda a: jnp.broadcast_to(a[:, None, :], grid)
    by_col = lambda a: jnp.broadcast_to(a[None, :, :], grid)
    cr, sr, cc, sc = by_row(jnp.cos(ang_r)), by_row(jnp.sin(ang_r)), by_col(jnp.cos(ang_c)), by_col(jnp.sin(ang_c))
    cos = jnp.concatenate([cr, cr, cc, cc], axis=-1).reshape(t_len, B_QK_DIM)
    sin = jnp.concatenate([-sr, sr, -sc, sc], axis=-1).reshape(t_len, B_QK_DIM)
    return cos, sin


def _token_mixer(x2d, g_pre, sc, sh, lp, cos, sin, states, bd):
    proj_a, proj_b = _in_proj(x2d, g_pre, sc, sh, lp["w_in"], lp["mu"])
    prep = _rwkv_prep(proj_a, lp, bd)
    o_f, o_b, s_fin = _rwkv_scan(proj_a, prep, states[0])
    y_f, y_b, r_fin = _retention(proj_b, cos, sin, lp["lgt"], states[1])
    return proj_b, prep, (o_f, o_b), (y_f, y_b), (s_fin, r_fin)


def kernel(x, c, ctx, c_ctx, w_mod, b_mod, norm_pre_mix, norm_post_mix, norm_pre_ffn, norm_post_ffn, w_in, shift_mu,
           rwkv_w0, rwkv_w_up, rwkv_a0, rwkv_a_up, rwkv_g_up, rwkv_k_k, rwkv_k_a, rwkv_r_k, rwkv_ln_w, rwkv_ln_b,
           w_branch_a, ret_decay_logit, ret_ln_w, ret_ln_b, w_branch_b, w_out, router_w, router_bias, exp_w_gate,
           exp_w_up, exp_w_down, sh_w_gate, sh_w_up, sh_w_down):
    d = D_MODEL
    assert x.shape[0] == 1 and w_in.shape[0] == 1, "single batch element, single layer"
    t_len = x.shape[1]
    x2d = x.reshape(t_len, d)
    ctx2d = ctx.reshape(ctx.shape[1], d)
    row = lambda a: a.reshape(1, -1)

    cs = jnp.zeros((8, d), F32).at[0].set(c[0]).at[1].set(c_ctx)
    mod = _modulation(cs, w_mod[0], b_mod[0])
    sh1, sc1, g1, sh2, sc2, g2 = [mod[0:1, i * d:(i + 1) * d] for i in range(6)]
    csh1, csc1 = mod[1:2, 0:d], mod[1:2, d:2 * d]

    lp = {
        "w_in": _w_in_layout(w_in[0]),
        "mu": jnp.pad(shift_mu[0], ((0, 0), (0, SHIFT_PAD - SHIFT_COLS))),
        "k_k": row(rwkv_k_k[0]), "k_a": row(rwkv_k_a[0]), "r_k": row(rwkv_r_k[0]),
        "w0": rwkv_w0[0], "w_up": rwkv_w_up[0], "a0": rwkv_a0[0], "a_up": rwkv_a_up[0], "g_up": rwkv_g_up[0],
        "ln_w": row(rwkv_ln_w[0]), "ln_b": row(rwkv_ln_b[0]),
        "ret_ln_w": row(ret_ln_w[0]), "ret_ln_b": row(ret_ln_b[0]),
        "lgt": jnp.broadcast_to(ret_decay_logit[0].reshape(2 * B_HEADS, 1), (2 * B_HEADS, 128)),
        "w_a": w_branch_a[0].astype(BF16), "w_b": w_branch_b[0].astype(BF16), "w_o": w_out[0].astype(BF16),
    }
    bd = _block_diag_ones(A_WIDTH, A_HEAD_DIM)
    g_pre = row(norm_pre_mix[0])

    t_ctx = ctx2d.shape[0]
    zero_states = (jnp.zeros(RWKV_STATE_SHAPE, F32),
                   jnp.zeros((2, B_HEADS, B_QK_DIM, B_V_DIM), F32))
    ones = jnp.ones((t_ctx, B_QK_DIM), F32)
    *_, ctx_states = _token_mixer(ctx2d, g_pre, csc1, csh1, lp, ones, jnp.zeros_like(ones), zero_states, bd)

    cos, sin = _rope_tables(t_len)
    proj, prep, (o_f, o_b), (y_f, y_b), _ = _token_mixer(x2d, g_pre, sc1, sh1, lp, cos, sin, ctx_states, bd)
    vecs = {"npm": row(norm_post_mix[0]), "npf": row(norm_pre_ffn[0]), "g1": g1, "sc2": sc2, "sh2": sh2}
    x1, h2, h3 = _mix_out(x2d, proj, o_f, o_b, prep[7], prep[8], y_f, y_b, lp, vecs, bd)

    sel, wts, rank, counts = _router(h2, router_w[0], router_bias[0])
    counts = counts.reshape(N_EXPERTS)
    padded = (counts + MOE_BLOCK - 1) // MOE_BLOCK * MOE_BLOCK
    pad_end = jnp.cumsum(padded)
    pad_start = (pad_end - padded).astype(jnp.int32)
    n_assign = t_len * TOP_K
    n_blocks = (n_assign + N_EXPERTS * (MOE_BLOCK - 1) + MOE_BLOCK - 1) // MOE_BLOCK
    block_start = jnp.arange(n_blocks, dtype=jnp.int32) * MOE_BLOCK
    block_e = jnp.minimum(jnp.sum(pad_end[None, :] <= block_start[:, None], axis=1), N_EXPERTS - 1).astype(jnp.int32)
    n_used = (pad_end[-1:] // MOE_BLOCK).astype(jnp.int32)
    sel_flat = sel.reshape(n_assign)
    rank_flat = rank.reshape(n_assign)

    xs = _dispatch(h3, sel_flat, rank_flat, pad_start, counts, n_blocks * MOE_BLOCK)
    ys = _experts(xs, block_e, n_used, exp_w_gate[0], exp_w_up[0], exp_w_down[0])
    out = _combine(ys, sel_flat, rank_flat, pad_start, wts, h2, x1, sh_w_gate[0].astype(BF16),
                   sh_w_up[0].astype(BF16), sh_w_down[0].astype(BF16), row(norm_post_ffn[0]), g2)
    return out.reshape(x.shape)
```

```python
import functools

import jax
import jax.numpy as jnp
import numpy as np
from jax import lax
from jax.experimental import pallas as pl
from jax.experimental.pallas import tpu as pltpu

F32 = jnp.float32
BF16 = jnp.bfloat16

D_MODEL = 1024
GRID_W = 64
NORM_EPS = 1e-6

A_HEAD_DIM = 64
A_WIDTH = D_MODEL // 2
A_HEADS = A_WIDTH // A_HEAD_DIM
W_LORA = 64
ICLR_LORA = 64
G_LORA = 128
RWKV_GN_EPS = 64e-5

B_HEADS = 4
B_QK_WIDTH = D_MODEL // 2
B_V_WIDTH = D_MODEL
B_QK_DIM = B_QK_WIDTH // B_HEADS
B_V_DIM = B_V_WIDTH // B_HEADS
RET_CHUNK = 128
RET_GN_EPS = 1e-5
ROPE_BASE = 10000.0

SHIFT_COLS = 3 * A_WIDTH + W_LORA + ICLR_LORA + G_LORA
LORA_COLS = W_LORA + ICLR_LORA + G_LORA
SHIFT_PAD = 2048
PROJ_COLS = SHIFT_PAD + 2 * B_QK_WIDTH + 2 * B_V_WIDTH + 2 * D_MODEL

N_EXPERTS = 256
TOP_K = 8
EXPERT_FF = D_MODEL // 4
SHARED_FF = D_MODEL // 4
ROUTED_SCALE = 2.5
MOE_BLOCK = 512
ZERO_PAD_GROUP = 16
MOE_TOKEN_TILE = 512

RWKV_CHUNK = 64
RWKV_STATE_SHAPE = (2, A_HEADS // 2, A_HEAD_DIM, 2 * A_HEAD_DIM)
PROJ_TN = 1024
PROJ_TM = 1024
VMEM_LIMIT = 48 * 1024 * 1024


def _params(sem):
    return pltpu.CompilerParams(dimension_semantics=sem, vmem_limit_bytes=VMEM_LIMIT)


def _bf(a):
    return a.astype(BF16)


def _bdot(a, b):
    return jnp.dot(a, b, preferred_element_type=F32)


def _dot_nt(a, b):
    return lax.dot_general(a, b, (((1,), (1,)), ((), ())), preferred_element_type=F32)


def _dot_tn(a, b):
    return lax.dot_general(a, b, (((0,), (0,)), ((), ())), preferred_element_type=F32)


def _split(a, n):
    out = []
    rem = a
    for _ in range(n):
        p = _bf(rem)
        out.append(p)
        rem = rem - p.astype(F32)
    return out


def _dot_split_lhs(a, b_bf, n=3):
    acc = None
    for p in _split(a, n):
        t = _bdot(p, b_bf)
        acc = t if acc is None else acc + t
    return acc


def _dot_split_rhs(a_bf, b, n=3):
    acc = None
    for p in _split(b, n):
        t = _bdot(a_bf, p)
        acc = t if acc is None else acc + t
    return acc


def _dot3(a, b):
    ah, al = _split(a, 2)
    bh, bl = _split(b, 2)
    return _bdot(ah, bh) + (_bdot(ah, bl) + _bdot(al, bh))


def _sigmoid(x):
    return 1.0 / (1.0 + jnp.exp(-x))


def _softplus(x):
    return jnp.maximum(x, 0.0) + jnp.log1p(jnp.exp(-jnp.abs(x)))


def _rms(x):
    return x * lax.rsqrt(jnp.mean(x * x, axis=-1, keepdims=True) + NORM_EPS)


LANES = 128


PACK_TILE = D_MODEL // LANES // 2
HIGH_HALF = 0xFFFF0000


def _row_tile(ref, r, lead=()):
    return ref.at[lead + (pl.ds(pl.multiple_of(r * PACK_TILE, PACK_TILE), PACK_TILE),)]


def _rows_to_packed_tiles(x2d, ref):
    n = x2d.shape[0]
    half = D_MODEL // 2
    for j in range(PACK_TILE):
        lo = _bf(x2d[:, j * LANES:(j + 1) * LANES]).astype(F32)
        hi = _bf(x2d[:, half + j * LANES:half + (j + 1) * LANES]).astype(F32)
        word = (pltpu.bitcast(hi, jnp.uint32) & jnp.uint32(HIGH_HALF)) | (pltpu.bitcast(lo, jnp.uint32) >> 16)
        ref[pl.ds(j, n, stride=PACK_TILE), :] = word


def _packed_tiles_to_rows(ref, n, lead=(), dtype=BF16):
    lo, hi = [], []
    for j in range(PACK_TILE):
        word = ref[lead + (pl.ds(j, n, stride=PACK_TILE), slice(None))]
        lo.append(pltpu.bitcast(word << 16, F32).astype(dtype))
        hi.append(pltpu.bitcast(word & jnp.uint32(HIGH_HALF), F32).astype(dtype))
    return jnp.concatenate(lo + hi, axis=1)


def _mod_kernel(cs_ref, w_ref, b_ref, o_ref):
    cs = cs_ref[...]
    s = cs * _sigmoid(cs)
    o_ref[...] = _dot3(s, w_ref[...]) + b_ref[...]


def _modulation(cs, w_mod, b_mod):
    n_out = w_mod.shape[1]
    tn = 1536
    return pl.pallas_call(
        _mod_kernel,
        out_shape=jax.ShapeDtypeStruct((8, n_out), F32),
        grid=(n_out // tn,),
        in_specs=[
            pl.BlockSpec((8, D_MODEL), lambda j: (0, 0)),
            pl.BlockSpec((D_MODEL, tn), lambda j: (0, j)),
            pl.BlockSpec((1, tn), lambda j: (0, j)),
        ],
        out_specs=pl.BlockSpec((8, tn), lambda j: (0, j)),
        compiler_params=_params(("arbitrary",)),
        name="modulation",
    )(cs, w_mod, b_mod.reshape(1, n_out))


W_PAD_TN = SHIFT_PAD - SHIFT_COLS


def _w_in_layout_kernel(w_ref, o_ref, *, gap_tile):
    w = w_ref[...]
    o_ref[...] = _bf(jnp.where(pl.program_id(0) == gap_tile, jnp.zeros_like(w), w))


def _w_in_layout(w_in):
    tn = W_PAD_TN
    gap_tile = SHIFT_COLS // tn
    kern = functools.partial(_w_in_layout_kernel, gap_tile=gap_tile)
    src = lambda j: (0, jnp.where(j < gap_tile, j, jnp.maximum(j - 1, 0)))
    return pl.pallas_call(
        kern,
        out_shape=jax.ShapeDtypeStruct((D_MODEL, PROJ_COLS), BF16),
        grid=(PROJ_COLS // tn,),
        in_specs=[pl.BlockSpec((D_MODEL, tn), src)],
        out_specs=pl.BlockSpec((D_MODEL, tn), lambda j: (0, j)),
        compiler_params=_params(("arbitrary",)),
        name="w_in_layout",
    )(w_in)


def _in_proj_kernel(x_ref, xp_ref, xn_ref, g_ref, sc_ref, sh_ref, w_ref, mu_ref, o_ref, ob_ref, h_s, hh_s,
                    *, tm, n_shift):
    i = pl.program_id(0)
    j = pl.program_id(1)
    n_i = pl.num_programs(0)

    def norm_mod(xv):
        return (_rms(xv) * g_ref[...]) * (1.0 + sc_ref[...]) + sh_ref[...]

    @pl.when(j == 0)
    def _():
        h_s[...] = _bf(norm_mod(x_ref[...]))
        hp = jnp.where(i > 0, norm_mod(xp_ref[...]), 0.0)
        hn = jnp.where(i < n_i - 1, norm_mod(xn_ref[...]), 0.0)
        hh_s[0:8, :] = hp
        hh_s[8:16, :] = hn

    @pl.when(j < n_shift)
    def _():
        w = w_ref[...]
        p = _bdot(h_s[...], w)
        ph = _bdot(_bf(hh_s[...]), w)
        row = lax.broadcasted_iota(jnp.int32, p.shape, 0)
        prev = jnp.where(row == 0, ph[7:8, :], pltpu.roll(p, 1, 0))
        nxt = jnp.where(row == tm - 1, ph[8:9, :], pltpu.roll(p, tm - 1, 0))
        mu = mu_ref[...]
        o_ref[...] = p + mu[0:1, :] * (prev - p) + mu[1:2, :] * (nxt - p)

    @pl.when(j >= n_shift)
    def _():
        ob_ref[...] = _bf(_bdot(h_s[...], w_ref[...]))


def _in_proj(x2d, g, sc, sh, w_bf, mu_pad):
    t_len = x2d.shape[0]
    tm = min(t_len, PROJ_TM)
    tn = PROJ_TN
    n_shift = SHIFT_PAD // tn
    tb8 = tm // 8
    nb8 = t_len // 8
    kern = functools.partial(_in_proj_kernel, tm=tm, n_shift=n_shift)
    vec = lambda: pl.BlockSpec((1, D_MODEL), lambda i, j: (0, 0))
    return pl.pallas_call(
        kern,
        out_shape=[jax.ShapeDtypeStruct((t_len, SHIFT_PAD), F32),
                   jax.ShapeDtypeStruct((t_len, PROJ_COLS - SHIFT_PAD), BF16)],
        grid=(t_len // tm, PROJ_COLS // tn),
        in_specs=[
            pl.BlockSpec((tm, D_MODEL), lambda i, j: (i, 0)),
            pl.BlockSpec((8, D_MODEL), lambda i, j: (jnp.maximum(i * tb8 - 1, 0), 0)),
            pl.BlockSpec((8, D_MODEL), lambda i, j: (jnp.minimum((i + 1) * tb8, nb8 - 1), 0)),
            vec(), vec(), vec(),
            pl.BlockSpec((D_MODEL, tn), lambda i, j: (0, j)),
            pl.BlockSpec((2, tn), lambda i, j: (0, jnp.minimum(j, n_shift - 1))),
        ],
        out_specs=[pl.BlockSpec((tm, tn), lambda i, j: (i, jnp.minimum(j, n_shift - 1))),
                   pl.BlockSpec((tm, tn), lambda i, j: (i, jnp.maximum(j - n_shift, 0)))],
        scratch_shapes=[pltpu.VMEM((tm, D_MODEL), BF16), pltpu.VMEM((16, D_MODEL), F32)],
        compiler_params=_params(("arbitrary", "arbitrary")),
        name="in_proj",
    )(x2d, x2d, x2d, g, sc, sh, w_bf, mu_pad)


def _rwkv_prep_kernel(r_ref, k_ref, v_ref, lora_ref, kk_w, ka_w, rk_w, w0_ref, wup_ref, a0_ref, aup_ref, gup_ref,
                      bd_ref, kk_o, lw0_o, lw1_o, kd0_o, kd1_o, b0_o, b1_o, bonus_o, g_o):
    r = r_ref[...]
    k = k_ref[...]
    v = v_ref[...]
    lora = lora_ref[...]
    xw = lora[:, 0:W_LORA]
    xa = lora[:, W_LORA:W_LORA + ICLR_LORA]
    xg = lora[:, W_LORA + ICLR_LORA:]
    bd = bd_ref[...]

    kk = k * kk_w[...]
    nrm = jnp.sqrt(_dot_split_lhs(kk * kk, bd))
    kk = kk / jnp.maximum(nrm, 1e-12)
    kk_o[...] = kk

    tw = jnp.tanh(xw)
    lw_outs = (lw0_o, lw1_o)
    kd_outs = (kd0_o, kd1_o)
    b_outs = (b0_o, b1_o)
    kd_sum = None
    for d in range(2):
        z = w0_ref[d:d + 1, :] + _dot3(tw, wup_ref[d])
        w_log = -_softplus(-z) - 0.5
        lw_outs[d][...] = -jnp.exp(w_log)
        a_d = _sigmoid(a0_ref[d:d + 1, :] + _dot3(xa, aup_ref[d]))
        kd = k * (1.0 + (a_d - 1.0) * ka_w[...])
        kd_outs[d][...] = kd
        b_outs[d][...] = kk * a_d
        kd_sum = kd if kd_sum is None else kd_sum + kd
    g_o[...] = _dot3(_sigmoid(xg), gup_ref[...])
    bonus_o[...] = _dot_split_lhs(r * kd_sum * rk_w[...], bd) * v


def _rwkv_prep(proj, lp, bd):
    t_len = proj.shape[0]
    tm = min(t_len, 512)
    aw = A_WIDTH
    col = lambda c: pl.BlockSpec((tm, aw), lambda i: (i, c))
    vec = lambda: pl.BlockSpec((1, aw), lambda i: (0, 0))
    full = lambda shp: pl.BlockSpec(shp, lambda i: (0,) * len(shp))
    outs = [jax.ShapeDtypeStruct((t_len, aw), F32)] * 9
    return pl.pallas_call(
        _rwkv_prep_kernel,
        out_shape=outs,
        grid=(t_len // tm,),
        in_specs=[
            col(0), col(1), col(2),
            pl.BlockSpec((tm, LORA_COLS), lambda i: (i, 3 * aw // LORA_COLS)),
            vec(), vec(), vec(),
            full((2, aw)), full((2, W_LORA, aw)), full((2, aw)), full((2, ICLR_LORA, aw)), full((G_LORA, aw)),
            full((aw, aw)),
        ],
        out_specs=[pl.BlockSpec((tm, aw), lambda i: (i, 0))] * 9,
        compiler_params=_params(("arbitrary",)),
        name="rwkv_prep",
    )(proj, proj, proj, proj, lp["k_k"], lp["k_a"], lp["r_k"], lp["w0"], lp["w_up"], lp["a0"], lp["a_up"],
      lp["g_up"], bd)


def _rwkv_scan_kernel(rf, vf, kkf, lwf, kdf, bf_, rb, vb, kkb, lwb, kdb, bb, s0_ref, of_ref, ob_ref, sfin_ref, s_s,
                      *, tb):
    step = pl.program_id(0)

    @pl.when(step == 0)
    def _():
        s_s[...] = s0_ref[...]

    c = RWKV_CHUNK
    n = A_HEAD_DIM
    assert c == n
    pw = 2 * n
    row = lax.broadcasted_iota(jnp.int32, (c, pw), 0)
    lane = lax.broadcasted_iota(jnp.int32, (c, pw), 1)
    col = lane % c
    first = lane < n
    eye = (row == col).astype(F32)
    r_c = lax.broadcasted_iota(jnp.int32, (c, c), 0)
    c_c = lax.broadcasted_iota(jnp.int32, (c, c), 1)
    n_ch = tb // c
    dirs = ((rf, vf, kkf, lwf, kdf, bf_, of_ref), (rb, vb, kkb, lwb, kdb, bb, ob_ref))
    pairs = range(A_HEADS // 2)
    psl = [slice(q * pw, (q + 1) * pw) for q in pairs]

    def bd(x):
        z = jnp.zeros_like(x)
        return jnp.concatenate([jnp.where(first, x, z), jnp.where(first, z, x)], axis=0)

    def chunk_terms(d, refs, ci):
        r_ref, v_ref, kk_ref, lw_ref, kd_ref, b_ref, _ = refs
        if d == 0:
            incl, strict, tri = row >= col, row > col, r_c >= c_c
        else:
            incl, strict, tri = row <= col, row < col, r_c <= c_c
        rows = slice(ci * c, (ci + 1) * c)
        lw = lw_ref[rows, :]
        r = r_ref[rows, :]
        kk = kk_ref[rows, :]
        kd = kd_ref[rows, :]
        b = b_ref[rows, :]
        cl = _dot_split_rhs(tri.astype(BF16), lw)
        cl_tot = cl[c - 1:c, :] if d == 0 else cl[0:1, :]
        e_neg = jnp.exp(-cl)
        e_end = jnp.exp(cl_tot - cl)
        g_tot = jnp.exp(cl_tot)
        rt = r * jnp.exp(cl)
        x_all = _bf(jnp.concatenate([-(kk * jnp.exp(cl - lw)), rt], axis=0))
        bt_all = _bf(b * e_neg)
        kt_all = _bf(kd * e_neg)
        return {
            "rows": rows, "strict": strict, "incl": incl,
            "x": [x_all[:, ps] for ps in psl],
            "z": [jnp.concatenate([bd(bt_all[:, ps]), bd(kt_all[:, ps])], axis=0) for ps in psl],
            "bd_bh": [bd(_bf(b * e_end)[:, ps]) for ps in psl],
            "bd_kh": [bd(_bf(kd * e_end)[:, ps]) for ps in psl],
            "bd_v": [bd(_bf(v_ref[rows, :])[:, ps]) for ps in psl],
            "rt": [rt[:, ps] for ps in psl],
            "g_tot": [g_tot[:, ps] for ps in psl],
        }

    chunks = [(d, cc) for cc in range(n_ch) for d in range(2)]
    prep = {(d, cc): chunk_terms(d, dirs[d], cc if d == 0 else n_ch - 1 - cc) for d, cc in chunks}
    units = [(k, q) for k in chunks for q in pairs]

    def per_unit(name, indexed=True):
        return [prep[k][name][q] if indexed else prep[k][name] for k, q in units]

    strict, incl = per_unit("strict", False), per_unit("incl", False)
    x_u, bd_bh, bd_kh, bd_v, rt_u = (per_unit(s) for s in ("x", "bd_bh", "bd_kh", "bd_v", "rt"))
    g = [_dot_nt(x, z) for x, z in zip(x_u, per_unit("z"))]
    a_ab = [jnp.where(s, gi[:c, :pw], 0.0) for s, gi in zip(strict, g)]
    a_rb = [_bf(jnp.where(i, gi[c:, :pw], 0.0)) for i, gi in zip(incl, g)]
    a_k = [_bf(jnp.concatenate([jnp.where(s, gi[:c, pw:], 0.0), jnp.where(i, gi[c:, pw:], 0.0)], axis=0))
           for s, i, gi in zip(strict, incl, g)]
    a2 = [_bdot(_bf(a), bd(_bf(a))) for a in a_ab]
    pa = [jnp.concatenate([eye + a, sq], axis=0) for a, sq in zip(a_ab, a2)]
    for _ in range(4):
        nxt = [_bdot(_bf(x), bd(_bf(x[c:]))) for x in pa]
        pa = [jnp.concatenate([x[:c] + y[:c], y[c:]], axis=0) for x, y in zip(pa, nxt)]
    p = [_bf(x[:c] + _bdot(_bf(x[:c]), bd(_bf(x[c:])))) for x in pa]
    vk = [_bdot(a, v) for a, v in zip(a_k, bd_v)]
    tw = [_bf(_bdot(pi, jnp.concatenate([bd(x[:c]), bd(_bf(vki[:c]))], axis=1)))
          for pi, vki, x in zip(p, vk, x_u)]
    bd_at = [bd(t[:, :pw]) for t in tw]
    bd_w = [bd(t[:, pw:]) for t in tw]
    mt = [_bf(_dot_tn(a, bh)) for a, bh in zip(bd_at, bd_bh)]
    ntf = [_dot_tn(jnp.concatenate([w, v], axis=0), jnp.concatenate([bh, kh], axis=0))
           for w, v, bh, kh in zip(bd_w, bd_v, bd_bh, bd_kh)]
    nt = [m[:n] + m[n:] for m in ntf]
    rw = [_bdot(a, jnp.concatenate([at, w], axis=1)) for a, at, w in zip(a_rb, bd_at, bd_w)]
    ry = [_bf(r + w[:, :pw]) for w, r in zip(rw, rt_u)]
    y0 = [w[:, pw:] + vki[c:] for w, vki in zip(rw, vk)]
    term = {u: vals for u, vals in zip(units, zip(mt, nt, ry, y0, per_unit("g_tot")))}

    state = [[s_s[d, q] for q in pairs] for d in range(2)]
    dq = [(d, q) for d in range(2) for q in pairs]
    for cc in range(n_ch):
        mt_c, nt_c, ry_c, y0_c, gt_c = zip(*[term[((d, cc), q)] for d, q in dq])
        s0 = [state[d][q] for d, q in dq]
        ys = [_dot_nt(ry_c[i], bd(_bf(s0[i]))) + y0_c[i] for i in range(len(dq))]
        upd = [_bdot(jnp.concatenate(_split(s0[i], 2), axis=0), mt_c[i]) for i in range(len(dq))]
        for i, (d, q) in enumerate(dq):
            state[d][q] = s0[i] * gt_c[i] + (upd[i][:n] + upd[i][n:]) + nt_c[i]
        for d in range(2):
            y_d = [ys[i] for i, (dd, _) in enumerate(dq) if dd == d]
            dirs[d][6][prep[(d, cc)]["rows"], :] = jnp.concatenate(y_d, axis=1)
    for d in range(2):
        for q in pairs:
            s_s[d, q] = state[d][q]

    @pl.when(step == pl.num_programs(0) - 1)
    def _():
        sfin_ref[...] = s_s[...]


def _rwkv_scan(proj, prep, s0):
    kk, lw0, lw1, kd0, kd1, b0, b1 = prep[:7]
    t_len = proj.shape[0]
    tb = 4 * RWKV_CHUNK
    nb = t_len // tb
    aw = A_WIDTH
    fwd = lambda c: pl.BlockSpec((tb, aw), lambda i: (i, c))
    bwd = lambda c: pl.BlockSpec((tb, aw), lambda i: (nb - 1 - i, c))
    st = pl.BlockSpec(RWKV_STATE_SHAPE, lambda i: (0, 0, 0, 0))
    kern = functools.partial(_rwkv_scan_kernel, tb=tb)
    return pl.pallas_call(
        kern,
        out_shape=[jax.ShapeDtypeStruct((t_len, aw), F32), jax.ShapeDtypeStruct((t_len, aw), F32),
                   jax.ShapeDtypeStruct(RWKV_STATE_SHAPE, F32)],
        grid=(nb,),
        in_specs=[fwd(0), fwd(2), fwd(0), fwd(0), fwd(0), fwd(0),
                  bwd(0), bwd(2), bwd(0), bwd(0), bwd(0), bwd(0), st],
        out_specs=[fwd(0), bwd(0), st],
        scratch_shapes=[pltpu.VMEM(RWKV_STATE_SHAPE, F32)],
        compiler_params=_params(("arbitrary",)),
        name="rwkv_scan",
    )(proj, proj, kk, lw0, kd0, b0, proj, proj, kk, lw1, kd1, b1, s0)


def _retention_kernel(qf, kf, vf, cosf, sinf, qb, kb, vb, cosb, sinb, lgt_ref, r0_ref, yf_ref, yb_ref, rfin_ref, r_s,
                      *, n_ch):
    step = pl.program_id(0)

    @pl.when(step == 0)
    def _():
        r_s[...] = r0_ref[...]

    c = RET_CHUNK
    dk = B_QK_DIM
    dv = B_V_DIM
    lg_all = -_softplus(-lgt_ref[...])
    rowf = lax.broadcasted_iota(jnp.int32, (c, c), 0).astype(F32)
    colf = lax.broadcasted_iota(jnp.int32, (c, c), 1).astype(F32)
    lane = lax.broadcasted_iota(jnp.int32, (c, dk), 1)
    first_half = (lane % 64) < 32
    dirs = ((qf, kf, vf, cosf, sinf, yf_ref), (qb, kb, vb, cosb, sinb, yb_ref))
    units = [(d, h) for d in range(2) for h in range(B_HEADS)]

    lg = [lg_all[d * B_HEADS + h:d * B_HEADS + h + 1, :] for d, h in units]
    diff = [(rowf - colf) if d == 0 else (colf - rowf) for d, _ in units]
    pos = [rowf if d == 0 else (c - 1.0) - rowf for d, _ in units]
    dmask = [jnp.where(df >= 0.0, jnp.exp(l * jnp.maximum(df, 0.0)), 0.0) for l, df in zip(lg, diff)]
    xi = [jnp.exp(l * (p + 1.0)) for l, p in zip(lg, pos)]
    zeta = [jnp.exp(l * ((c - 1.0) - p)) for l, p in zip(lg, pos)]
    g_chunk = [jnp.exp(l * float(c)) for l in lg]
    g_chunk = [jnp.concatenate([g, g], axis=1) for g in g_chunk]

    state = [r_s[d, h] for d, h in units]
    for cc in range(n_ch):
        rows = [slice((cc if d == 0 else n_ch - 1 - cc) * c, (cc if d == 0 else n_ch - 1 - cc) * c + c) for d in range(2)]
        cos = [dirs[d][3][rows[d], :] for d in range(2)]
        sin = [dirs[d][4][rows[d], :] for d in range(2)]

        def rope(xv, d):
            swapped = jnp.where(first_half, pltpu.roll(xv, dk - 32, 1), pltpu.roll(xv, 32, 1))
            return xv * cos[d] + swapped * sin[d]

        qh = [rope(dirs[d][0][rows[d], h * dk:(h + 1) * dk].astype(F32), d) for d, h in units]
        kh = [rope(dirs[d][1][rows[d], h * dk:(h + 1) * dk].astype(F32) * (dk ** -0.5), d) for d, h in units]
        vh = [dirs[d][2][rows[d], h * dv:(h + 1) * dv] for d, h in units]
        scores = [_dot_nt(_bf(q), _bf(k)) * m for q, k, m in zip(qh, kh, dmask)]
        inner = [_bdot(_bf(s), v) for s, v in zip(scores, vh)]
        cross = [_bdot(_bf(q * x), _bf(r)) for q, x, r in zip(qh, xi, state)]
        kv = [_dot_tn(_bf(k * z), v) for k, z, v in zip(kh, zeta, vh)]
        for i, (d, h) in enumerate(units):
            dirs[d][5][rows[d], h * dv:(h + 1) * dv] = inner[i] + cross[i]
        state = [g * r + x for g, r, x in zip(g_chunk, state, kv)]
    for i, (d, h) in enumerate(units):
        r_s[d, h] = state[i]

    @pl.when(step == pl.num_programs(0) - 1)
    def _():
        rfin_ref[...] = r_s[...]


def _retention(proj, cos, sin, lgt, r0):
    t_len = proj.shape[0]
    n_ch = min(4, t_len // RET_CHUNK)
    c = n_ch * RET_CHUNK
    nc = t_len // c
    qw = B_QK_WIDTH
    vw = B_V_WIDTH
    fq = lambda col: pl.BlockSpec((c, qw), lambda i: (i, col))
    bq = lambda col: pl.BlockSpec((c, qw), lambda i: (nc - 1 - i, col))
    st = pl.BlockSpec((2, B_HEADS, B_QK_DIM, B_V_DIM), lambda i: (0, 0, 0, 0))
    q_col = 0
    v_col = 2 * qw // vw
    return pl.pallas_call(
        functools.partial(_retention_kernel, n_ch=n_ch),
        out_shape=[jax.ShapeDtypeStruct((t_len, vw), F32), jax.ShapeDtypeStruct((t_len, vw), F32),
                   jax.ShapeDtypeStruct((2, B_HEADS, B_QK_DIM, B_V_DIM), F32)],
        grid=(nc,),
        in_specs=[
            fq(q_col), fq(q_col + 1), pl.BlockSpec((c, vw), lambda i: (i, v_col)),
            pl.BlockSpec((c, B_QK_DIM), lambda i: (i, 0)), pl.BlockSpec((c, B_QK_DIM), lambda i: (i, 0)),
            bq(q_col), bq(q_col + 1), pl.BlockSpec((c, vw), lambda i: (nc - 1 - i, v_col)),
            pl.BlockSpec((c, B_QK_DIM), lambda i: (nc - 1 - i, 0)),
            pl.BlockSpec((c, B_QK_DIM), lambda i: (nc - 1 - i, 0)),
            pl.BlockSpec((2 * B_HEADS, 128), lambda i: (0, 0)), st,
        ],
        out_specs=[pl.BlockSpec((c, vw), lambda i: (i, 0)), pl.BlockSpec((c, vw), lambda i: (nc - 1 - i, 0)), st],
        scratch_shapes=[pltpu.VMEM((2, B_HEADS, B_QK_DIM, B_V_DIM), F32)],
        compiler_params=_params(("arbitrary",)),
        name="retention",
    )(proj, proj, proj, cos, sin, proj, proj, proj, cos, sin, lgt, r0)


def _mix_out_kernel(of_ref, ob_ref, bonus_ref, g_ref, yf_ref, yb_ref, gb_ref, ga_ref, gbb_ref, x_ref,
                    alnw, alnb, rlnw, rlnb, npm, npf, g1_ref, sc2_ref, sh2_ref, wa_ref, wb_ref, wo_ref, bd_ref,
                    x1_ref, h2_ref, h3_ref):
    bd = bd_ref[...]
    o = of_ref[...] + ob_ref[...]
    mu = _dot_split_lhs(o, bd) * (1.0 / A_HEAD_DIM)
    oc = o - mu
    var = _dot_split_lhs(oc * oc, bd) * (1.0 / A_HEAD_DIM)
    ya = oc * lax.rsqrt(var + RWKV_GN_EPS) * alnw[...] + alnb[...] + bonus_ref[...]
    ya = _bdot(_bf(ya * g_ref[...]), wa_ref[...])

    y = yf_ref[...] + yb_ref[...]
    parts = []
    for h in range(B_HEADS):
        seg = y[:, h * B_V_DIM:(h + 1) * B_V_DIM]
        m = jnp.mean(seg, axis=-1, keepdims=True)
        sc = seg - m
        vr = jnp.mean(sc * sc, axis=-1, keepdims=True)
        parts.append(sc * lax.rsqrt(vr + RET_GN_EPS))
    yn = jnp.concatenate(parts, axis=1) * rlnw[...] + rlnb[...]
    gb = gb_ref[...].astype(F32)
    yb = _bdot(_bf(yn * (gb * _sigmoid(gb))), wb_ref[...])

    merged = _sigmoid(ga_ref[...].astype(F32)) * ya + _sigmoid(gbb_ref[...].astype(F32)) * yb
    mix = _bdot(_bf(merged), wo_ref[...])
    x1 = x_ref[...] + g1_ref[...] * (_rms(mix) * npm[...])
    x1_ref[...] = x1
    h2 = (_rms(x1) * npf[...]) * (1.0 + sc2_ref[...]) + sh2_ref[...]
    h2_ref[...] = h2
    _rows_to_packed_tiles(h2, h3_ref)


def _mix_out(x2d, proj, o_f, o_b, bonus, g, y_f, y_b, lp, vecs, bd):
    t_len = x2d.shape[0]
    tm = min(t_len, 256)
    aw = A_WIDTH
    d = D_MODEL
    ta = lambda: pl.BlockSpec((tm, aw), lambda i: (i, 0))
    td = lambda: pl.BlockSpec((tm, d), lambda i: (i, 0))
    pc = lambda c: pl.BlockSpec((tm, d), lambda i: (i, c))
    va = lambda: pl.BlockSpec((1, aw), lambda i: (0, 0))
    vd = lambda: pl.BlockSpec((1, d), lambda i: (0, 0))
    full = lambda shp: pl.BlockSpec(shp, lambda i: (0, 0))
    gcol = (2 * B_QK_WIDTH + B_V_WIDTH) // d
    return pl.pallas_call(
        _mix_out_kernel,
        out_shape=[jax.ShapeDtypeStruct((t_len, d), F32), jax.ShapeDtypeStruct((t_len, d), F32),
                   jax.ShapeDtypeStruct((t_len * PACK_TILE, LANES), jnp.uint32)],
        grid=(t_len // tm,),
        in_specs=[ta(), ta(), ta(), ta(), td(), td(), pc(gcol), pc(gcol + 1), pc(gcol + 2), td(),
                  va(), va(), vd(), vd(), vd(), vd(), vd(), vd(), vd(),
                  full((aw, d)), full((d, d)), full((d, d)), full((aw, aw))],
        out_specs=[td(), td(), pl.BlockSpec((tm * PACK_TILE, LANES), lambda i: (i, 0))],
        compiler_params=_params(("arbitrary",)),
        name="mix_out",
    )(o_f, o_b, bonus, g, y_f, y_b, proj, proj, proj, x2d,
      lp["ln_w"], lp["ln_b"], lp["ret_ln_w"], lp["ret_ln_b"], vecs["npm"], vecs["npf"], vecs["g1"], vecs["sc2"],
      vecs["sh2"], lp["w_a"], lp["w_b"], lp["w_o"], bd)


def _router_kernel(h_ref, rw_ref, bias_ref, sel_ref, wts_ref, rank_ref, cnt_ref, cnt_s, *, tm):
    step = pl.program_id(0)

    @pl.when(step == 0)
    def _():
        cnt_s[...] = jnp.zeros_like(cnt_s)

    ne = N_EXPERTS
    scores = _sigmoid(_dot3(h_ref[...], rw_ref[...]))
    work = scores + bias_ref[...]
    lane = lax.broadcasted_iota(jnp.int32, (tm, ne), 1).astype(F32)
    idxs = []
    vals = []
    for _ in range(TOP_K):
        m = jnp.max(work, axis=-1, keepdims=True)
        idx = jnp.min(jnp.where(work == m, lane, float(ne)), axis=-1, keepdims=True)
        oh = lane == idx
        vals.append(jnp.sum(jnp.where(oh, scores, 0.0), axis=-1, keepdims=True))
        idxs.append(idx)
        work = jnp.where(oh, -jnp.inf, work)
    sel_f = jnp.concatenate(idxs, axis=1)
    s_sel = jnp.concatenate(vals, axis=1)
    wts_ref[...] = s_sel / jnp.sum(s_sel, axis=1, keepdims=True) * ROUTED_SCALE
    sel_ref[...] = sel_f.astype(jnp.int32)

    hit = work == -jnp.inf
    r_i = lax.broadcasted_iota(jnp.int32, (tm, tm), 0)
    c_i = lax.broadcasted_iota(jnp.int32, (tm, tm), 1)
    before = _bdot((r_i > c_i).astype(BF16), hit.astype(BF16)) + cnt_s[...]
    ranks = [jnp.sum(jnp.where(lane == idxs[k], before, 0.0), axis=-1, keepdims=True) for k in range(TOP_K)]
    rank_ref[...] = jnp.concatenate(ranks, axis=1).astype(jnp.int32)
    cnt = cnt_s[...] + jnp.sum(hit.astype(F32), axis=0, keepdims=True)
    cnt_s[...] = cnt
    cnt_ref[...] = cnt.astype(jnp.int32)


def _router(h2, router_w, router_bias):
    t_len = h2.shape[0]
    tm = min(t_len, 256)
    kern = functools.partial(_router_kernel, tm=tm)
    tk = lambda: pl.BlockSpec((tm, TOP_K), lambda i: (i, 0))
    return pl.pallas_call(
        kern,
        out_shape=[jax.ShapeDtypeStruct((t_len, TOP_K), jnp.int32), jax.ShapeDtypeStruct((t_len, TOP_K), F32),
                   jax.ShapeDtypeStruct((t_len, TOP_K), jnp.int32), jax.ShapeDtypeStruct((1, N_EXPERTS), jnp.int32)],
        grid=(t_len // tm,),
        in_specs=[pl.BlockSpec((tm, D_MODEL), lambda i: (i, 0)),
                  pl.BlockSpec((D_MODEL, N_EXPERTS), lambda i: (0, 0)),
                  pl.BlockSpec((1, N_EXPERTS), lambda i: (0, 0))],
        out_specs=[tk(), tk(), tk(), pl.BlockSpec((1, N_EXPERTS), lambda i: (0, 0))],
        scratch_shapes=[pltpu.VMEM((1, N_EXPERTS), F32)],
        compiler_params=_params(("arbitrary",)),
        name="router",
    )(h2, router_w, router_bias.reshape(1, N_EXPERTS))


def _slot(sel_ref, rank_ref, start_ref, a):
    return start_ref[sel_ref[a]] + rank_ref[a]


def _dispatch_kernel(sel_ref, rank_ref, start_ref, cnt_ref, h_ref, xs_out, zrow, sem, zsem, *, tm):
    def row_copy(t, k):
        slot = _slot(sel_ref, rank_ref, start_ref, t * TOP_K + k)
        return pltpu.make_async_copy(_row_tile(h_ref, t), _row_tile(xs_out, slot), sem)

    def issue(t, carry):
        for k in range(TOP_K):
            row_copy(t, k).start(priority=k % 2)
        return carry

    lax.fori_loop(0, tm, issue, 0)

    def drain(t, carry):
        for k in range(TOP_K):
            row_copy(t, k).wait()
        return carry

    lax.fori_loop(0, tm, drain, 0)

    @pl.when(pl.program_id(0) == pl.num_programs(0) - 1)
    def _():
        zrow[...] = jnp.zeros_like(zrow)

        def for_pad_runs(e0, fn):
            def per_expert(e, carry):
                n = cnt_ref[e]
                n_pad = (n + MOE_BLOCK - 1) // MOE_BLOCK * MOE_BLOCK
                pad = n_pad - n
                off = start_ref[e] + n
                for bit in reversed(range(MOE_BLOCK.bit_length() - 1)):
                    size = 1 << bit
                    is_set = ((pad >> bit) & 1) == 1

                    @pl.when(is_set)
                    def _(off=off, size=size):
                        dst = xs_out.at[pl.ds(pl.multiple_of(off * PACK_TILE, PACK_TILE), size * PACK_TILE)]
                        fn(pltpu.make_async_copy(zrow.at[pl.ds(0, size * PACK_TILE)], dst, zsem))

                    off = off + jnp.where(is_set, size, 0)
                return carry

            lax.fori_loop(e0, e0 + ZERO_PAD_GROUP, per_expert, 0)

        def per_group(gi, carry):
            e0 = gi * ZERO_PAD_GROUP
            for_pad_runs(e0, lambda cp: cp.start())
            for_pad_runs(e0, lambda cp: cp.wait())
            return carry

        lax.fori_loop(0, N_EXPERTS // ZERO_PAD_GROUP, per_group, 0)


def _dispatch(h3, sel_flat, rank_flat, pad_start, counts, n_slots):
    t_len = h3.shape[0] // PACK_TILE
    tm = min(t_len, MOE_TOKEN_TILE)
    kern = functools.partial(_dispatch_kernel, tm=tm)
    smem_blk = lambda: pl.BlockSpec((tm * TOP_K,), lambda i: (i,), memory_space=pltpu.SMEM)
    smem_all = lambda: pl.BlockSpec((N_EXPERTS,), lambda i: (0,), memory_space=pltpu.SMEM)
    return pl.pallas_call(
        kern,
        out_shape=jax.ShapeDtypeStruct((n_slots * PACK_TILE, LANES), jnp.uint32),
        grid=(t_len // tm,),
        in_specs=[smem_blk(), smem_blk(), smem_all(), smem_all(),
                  pl.BlockSpec((tm * PACK_TILE, LANES), lambda i: (i, 0))],
        out_specs=pl.BlockSpec(memory_space=pl.ANY),
        scratch_shapes=[pltpu.VMEM((MOE_BLOCK // 2 * PACK_TILE, LANES), jnp.uint32), pltpu.SemaphoreType.DMA(()),
                        pltpu.SemaphoreType.DMA(())],
        compiler_params=_params(("arbitrary",)),
        name="dispatch",
    )(sel_flat, rank_flat, pad_start, counts, h3)


def _expert_kernel(be_ref, nu_ref, xs_ref, wg_ref, wu_ref, wd_ref, ys_ref, wg_s, wu_s, wd_s):
    b = pl.program_id(0)
    used = b < nu_ref[0]
    new_expert = (b == 0) | (be_ref[b] != be_ref[jnp.maximum(b - 1, 0)])

    @pl.when(used & new_expert)
    def _():
        wg_s[...] = _bf(wg_ref[0])
        wu_s[...] = _bf(wu_ref[0])
        wd_s[...] = _bf(wd_ref[0])

    @pl.when(used)
    def _():
        xb = _packed_tiles_to_rows(xs_ref, MOE_BLOCK)
        gate = _bdot(xb, wg_s[...])
        up = _bdot(xb, wu_s[...])
        act = gate * _sigmoid(gate) * up
        _rows_to_packed_tiles(_bdot(_bf(act), wd_s[...]), ys_ref)

    @pl.when(jnp.logical_not(used))
    def _():
        ys_ref[...] = jnp.zeros_like(ys_ref)


def _experts(xs, block_e, n_used, w_gate, w_up, w_down):
    n_slots = xs.shape[0] // PACK_TILE
    n_blocks = n_slots // MOE_BLOCK
    rows = (MOE_BLOCK * PACK_TILE, LANES)
    grid_spec = pltpu.PrefetchScalarGridSpec(
        num_scalar_prefetch=2,
        grid=(n_blocks,),
        in_specs=[
            pl.BlockSpec((MOE_BLOCK * PACK_TILE, LANES), lambda b, be, nu: (jnp.where(b < nu[0], b, 0), 0)),
            pl.BlockSpec((1, D_MODEL, EXPERT_FF), lambda b, be, nu: (be[b], 0, 0)),
            pl.BlockSpec((1, D_MODEL, EXPERT_FF), lambda b, be, nu: (be[b], 0, 0)),
            pl.BlockSpec((1, EXPERT_FF, D_MODEL), lambda b, be, nu: (be[b], 0, 0)),
        ],
        out_specs=pl.BlockSpec(rows, lambda b, be, nu: (b, 0)),
        scratch_shapes=[pltpu.VMEM((D_MODEL, EXPERT_FF), BF16), pltpu.VMEM((D_MODEL, EXPERT_FF), BF16),
                        pltpu.VMEM((EXPERT_FF, D_MODEL), BF16)],
    )
    return pl.pallas_call(
        _expert_kernel,
        out_shape=jax.ShapeDtypeStruct((n_slots * PACK_TILE, LANES), jnp.uint32),
        grid_spec=grid_spec,
        compiler_params=_params(("arbitrary",)),
        name="experts",
    )(block_e, n_used, xs, w_gate, w_up, w_down)


def _combine_kernel(sel_ref, rank_ref, start_ref, ys_ref, wts_ref, h_ref, x1_ref, sg_ref, su_ref, sd_ref, npo, g2_ref,
                    o_ref, buf, sem, *, tm):
    def row_copy(t, k):
        slot = _slot(sel_ref, rank_ref, start_ref, t * TOP_K + k)
        return pltpu.make_async_copy(_row_tile(ys_ref, slot), _row_tile(buf, t, (k,)), sem)

    def issue(t, carry):
        for k in range(TOP_K):
            row_copy(t, k).start(priority=k % 2)
        return carry

    lax.fori_loop(0, tm, issue, 0)

    hb = _bf(h_ref[...])
    gate = _bdot(hb, sg_ref[...])
    up = _bdot(hb, su_ref[...])
    shared = _bdot(_bf(gate * _sigmoid(gate) * up), sd_ref[...])

    def drain(t, carry):
        for k in range(TOP_K):
            row_copy(t, k).wait()
        return carry

    lax.fori_loop(0, tm, drain, 0)

    wts = wts_ref[...]
    routed = _packed_tiles_to_rows(buf, tm, (0,), F32) * wts[:, 0:1]
    for k in range(1, TOP_K):
        routed = routed + _packed_tiles_to_rows(buf, tm, (k,), F32) * wts[:, k:k + 1]
    o_ref[...] = x1_ref[...] + g2_ref[...] * (_rms(routed + shared) * npo[...])


def _combine(ys, sel_flat, rank_flat, pad_start, wts, h2, x1, sg, su, sd, npo, g2):
    t_len = h2.shape[0]
    tm = min(t_len, MOE_TOKEN_TILE)
    d = D_MODEL
    kern = functools.partial(_combine_kernel, tm=tm)
    smem_blk = lambda: pl.BlockSpec((tm * TOP_K,), lambda i: (i,), memory_space=pltpu.SMEM)
    td = lambda: pl.BlockSpec((tm, d), lambda i: (i, 0))
    vd = lambda: pl.BlockSpec((1, d), lambda i: (0, 0))
    return pl.pallas_call(
        kern,
        out_shape=jax.ShapeDtypeStruct((t_len, d), F32),
        grid=(t_len // tm,),
        in_specs=[smem_blk(), smem_blk(),
                  pl.BlockSpec((N_EXPERTS,), lambda i: (0,), memory_space=pltpu.SMEM),
                  pl.BlockSpec(memory_space=pl.ANY),
                  pl.BlockSpec((tm, TOP_K), lambda i: (i, 0)),
                  td(), td(),
                  pl.BlockSpec((d, SHARED_FF), lambda i: (0, 0)), pl.BlockSpec((d, SHARED_FF), lambda i: (0, 0)),
                  pl.BlockSpec((SHARED_FF, d), lambda i: (0, 0)), vd(), vd()],
        out_specs=td(),
        scratch_shapes=[pltpu.VMEM((TOP_K, tm * PACK_TILE, LANES), jnp.uint32), pltpu.SemaphoreType.DMA(())],
        compiler_params=_params(("arbitrary",)),
        name="combine",
    )(sel_flat, rank_flat, pad_start, ys, wts, h2, x1, sg, su, sd, npo, g2)


def _block_diag_ones(width, group):
    idx = np.arange(width) // group
    return jnp.asarray(idx[:, None] == idx[None, :], dtype=BF16)


def _rope_tables(t_len):
    n_rows = t_len // GRID_W
    quarter = B_QK_DIM // 4
    inv_freq = jnp.asarray(ROPE_BASE, F32) ** (-jnp.arange(quarter, dtype=F32) / quarter)
    ang_r = jnp.arange(n_rows, dtype=F32)[:, None] * inv_freq
    ang_c = jnp.arange(GRID_W, dtype=F32)[:, None] * inv_freq
    grid = (n_rows, GRID_W, quarter)
    by_row = lambda a: jnp.broadcast_to(a[:, None, :], grid)
    by_col = lambda a: jnp.broadcast_to(a[None, :, :], grid)
    cr, sr, cc, sc = by_row(jnp.cos(ang_r)), by_row(jnp.sin(ang_r)), by_col(jnp.cos(ang_c)), by_col(jnp.sin(ang_c))
    cos = jnp.concatenate([cr, cr, cc, cc], axis=-1).reshape(t_len, B_QK_DIM)
    sin = jnp.concatenate([-sr, sr, -sc, sc], axis=-1).reshape(t_len, B_QK_DIM)
    return cos, sin


def _token_mixer(x2d, g_pre, sc, sh, lp, cos, sin, states, bd):
    proj_a, proj_b = _in_proj(x2d, g_pre, sc, sh, lp["w_in"], lp["mu"])
    prep = _rwkv_prep(proj_a, lp, bd)
    o_f, o_b, s_fin = _rwkv_scan(proj_a, prep, states[0])
    y_f, y_b, r_fin = _retention(proj_b, cos, sin, lp["lgt"], states[1])
    return proj_b, prep, (o_f, o_b), (y_f, y_b), (s_fin, r_fin)


def kernel(x, c, ctx, c_ctx, w_mod, b_mod, norm_pre_mix, norm_post_mix, norm_pre_ffn, norm_post_ffn, w_in, shift_mu,
           rwkv_w0, rwkv_w_up, rwkv_a0, rwkv_a_up, rwkv_g_up, rwkv_k_k, rwkv_k_a, rwkv_r_k, rwkv_ln_w, rwkv_ln_b,
           w_branch_a, ret_decay_logit, ret_ln_w, ret_ln_b, w_branch_b, w_out, router_w, router_bias, exp_w_gate,
           exp_w_up, exp_w_down, sh_w_gate, sh_w_up, sh_w_down):
    d = D_MODEL
    assert x.shape[0] == 1 and w_in.shape[0] == 1, "single batch element, single layer"
    t_len = x.shape[1]
    x2d = x.reshape(t_len, d)
    ctx2d = ctx.reshape(ctx.shape[1], d)
    row = lambda a: a.reshape(1, -1)

    cs = jnp.zeros((8, d), F32).at[0].set(c[0]).at[1].set(c_ctx)
    mod = _modulation(cs, w_mod[0], b_mod[0])
    sh1, sc1, g1, sh2, sc2, g2 = [mod[0:1, i * d:(i + 1) * d] for i in range(6)]
    csh1, csc1 = mod[1:2, 0:d], mod[1:2, d:2 * d]

    lp = {
        "w_in": _w_in_layout(w_in[0]),
        "mu": jnp.pad(shift_mu[0], ((0, 0), (0, SHIFT_PAD - SHIFT_COLS))),
        "k_k": row(rwkv_k_k[0]), "k_a": row(rwkv_k_a[0]), "r_k": row(rwkv_r_k[0]),
        "w0": rwkv_w0[0], "w_up": rwkv_w_up[0], "a0": rwkv_a0[0], "a_up": rwkv_a_up[0], "g_up": rwkv_g_up[0],
        "ln_w": row(rwkv_ln_w[0]), "ln_b": row(rwkv_ln_b[0]),
        "ret_ln_w": row(ret_ln_w[0]), "ret_ln_b": row(ret_ln_b[0]),
        "lgt": jnp.broadcast_to(ret_decay_logit[0].reshape(2 * B_HEADS, 1), (2 * B_HEADS, 128)),
        "w_a": w_branch_a[0].astype(BF16), "w_b": w_branch_b[0].astype(BF16), "w_o": w_out[0].astype(BF16),
    }
    bd = _block_diag_ones(A_WIDTH, A_HEAD_DIM)
    g_pre = row(norm_pre_mix[0])

    t_ctx = ctx2d.shape[0]
    zero_states = (jnp.zeros(RWKV_STATE_SHAPE, F32),
                   jnp.zeros((2, B_HEADS, B_QK_DIM, B_V_DIM), F32))
    ones = jnp.ones((t_ctx, B_QK_DIM), F32)
    *_, ctx_states = _token_mixer(ctx2d, g_pre, csc1, csh1, lp, ones, jnp.zeros_like(ones), zero_states, bd)

    cos, sin = _rope_tables(t_len)
    proj, prep, (o_f, o_b), (y_f, y_b), _ = _token_mixer(x2d, g_pre, sc1, sh1, lp, cos, sin, ctx_states, bd)
    vecs = {"npm": row(norm_post_mix[0]), "npf": row(norm_pre_ffn[0]), "g1": g1, "sc2": sc2, "sh2": sh2}
    x1, h2, h3 = _mix_out(x2d, proj, o_f, o_b, prep[7], prep[8], y_f, y_b, lp, vecs, bd)

    sel, wts, rank, counts = _router(h2, router_w[0], router_bias[0])
    counts = counts.reshape(N_EXPERTS)
    padded = (counts + MOE_BLOCK - 1) // MOE_BLOCK * MOE_BLOCK
    pad_end = jnp.cumsum(padded)
    pad_start = (pad_end - padded).astype(jnp.int32)
    n_assign = t_len * TOP_K
    n_blocks = (n_assign + N_EXPERTS * (MOE_BLOCK - 1) + MOE_BLOCK - 1) // MOE_BLOCK
    block_start = jnp.arange(n_blocks, dtype=jnp.int32) * MOE_BLOCK
    block_e = jnp.minimum(jnp.sum(pad_end[None, :] <= block_start[:, None], axis=1), N_EXPERTS - 1).astype(jnp.int32)
    n_used = (pad_end[-1:] // MOE_BLOCK).astype(jnp.int32)
    sel_flat = sel.reshape(n_assign)
    rank_flat = rank.reshape(n_assign)

    xs = _dispatch(h3, sel_flat, rank_flat, pad_start, counts, n_blocks * MOE_BLOCK)
    ys = _experts(xs, block_e, n_used, exp_w_gate[0], exp_w_up[0], exp_w_down[0])
    out = _combine(ys, sel_flat, rank_flat, pad_start, wts, h2, x1, sh_w_gate[0].astype(BF16),
                   sh_w_up[0].astype(BF16), sh_w_down[0].astype(BF16), row(norm_post_ffn[0]), g2)
    return out.reshape(x.shape)
```

```python
import functools

import jax
import jax.numpy as jnp
import numpy as np
from jax import lax
from jax.experimental import pallas as pl
from jax.experimental.pallas import tpu as pltpu

F32 = jnp.float32
BF16 = jnp.bfloat16

D_MODEL = 1024
GRID_W = 64
NORM_EPS = 1e-6

A_HEAD_DIM = 64
A_WIDTH = D_MODEL // 2
A_HEADS = A_WIDTH // A_HEAD_DIM
W_LORA = 64
ICLR_LORA = 64
G_LORA = 128
RWKV_GN_EPS = 64e-5

B_HEADS = 4
B_QK_WIDTH = D_MODEL // 2
B_V_WIDTH = D_MODEL
B_QK_DIM = B_QK_WIDTH // B_HEADS
B_V_DIM = B_V_WIDTH // B_HEADS
RET_CHUNK = 128
RET_GN_EPS = 1e-5
ROPE_BASE = 10000.0

SHIFT_COLS = 3 * A_WIDTH + W_LORA + ICLR_LORA + G_LORA
LORA_COLS = W_LORA + ICLR_LORA + G_LORA
SHIFT_PAD = 2048
PROJ_COLS = SHIFT_PAD + 2 * B_QK_WIDTH + 2 * B_V_WIDTH + 2 * D_MODEL

N_EXPERTS = 256
TOP_K = 8
EXPERT_FF = D_MODEL // 4
SHARED_FF = D_MODEL // 4
ROUTED_SCALE = 2.5
MOE_BLOCK = 512
ZERO_PAD_GROUP = 16
MOE_TOKEN_TILE = 512

RWKV_CHUNK = 64
RWKV_STATE_SHAPE = (2, A_HEADS // 2, A_HEAD_DIM, 2 * A_HEAD_DIM)
PROJ_TN = 1024
PROJ_TM = 1024
VMEM_LIMIT = 48 * 1024 * 1024


def _params(sem):
    return pltpu.CompilerParams(dimension_semantics=sem, vmem_limit_bytes=VMEM_LIMIT)


def _bf(a):
    return a.astype(BF16)


def _bdot(a, b):
    return jnp.dot(a, b, preferred_element_type=F32)


def _dot_nt(a, b):
    return lax.dot_general(a, b, (((1,), (1,)), ((), ())), preferred_element_type=F32)


def _dot_tn(a, b):
    return lax.dot_general(a, b, (((0,), (0,)), ((), ())), preferred_element_type=F32)


def _split(a, n):
    out = []
    rem = a
    for _ in range(n):
        p = _bf(rem)
        out.append(p)
        rem = rem - p.astype(F32)
    return out


def _dot_split_lhs(a, b_bf, n=3):
    acc = None
    for p in _split(a, n):
        t = _bdot(p, b_bf)
        acc = t if acc is None else acc + t
    return acc


def _dot_split_rhs(a_bf, b, n=3):
    acc = None
    for p in _split(b, n):
        t = _bdot(a_bf, p)
        acc = t if acc is None else acc + t
    return acc


def _dot3(a, b):
    ah, al = _split(a, 2)
    bh, bl = _split(b, 2)
    return _bdot(ah, bh) + (_bdot(ah, bl) + _bdot(al, bh))


def _sigmoid(x):
    return 1.0 / (1.0 + jnp.exp(-x))


def _softplus(x):
    return jnp.maximum(x, 0.0) + jnp.log1p(jnp.exp(-jnp.abs(x)))


def _rms(x):
    return x * lax.rsqrt(jnp.mean(x * x, axis=-1, keepdims=True) + NORM_EPS)


LANES = 128


PACK_TILE = D_MODEL // LANES // 2
HIGH_HALF = 0xFFFF0000


def _row_tile(ref, r, lead=()):
    return ref.at[lead + (pl.ds(pl.multiple_of(r * PACK_TILE, PACK_TILE), PACK_TILE),)]


def _rows_to_packed_tiles(x2d, ref):
    n = x2d.shape[0]
    half = D_MODEL // 2
    for j in range(PACK_TILE):
        lo = _bf(x2d[:, j * LANES:(j + 1) * LANES]).astype(F32)
        hi = _bf(x2d[:, half + j * LANES:half + (j + 1) * LANES]).astype(F32)
        word = (pltpu.bitcast(hi, jnp.uint32) & jnp.uint32(HIGH_HALF)) | (pltpu.bitcast(lo, jnp.uint32) >> 16)
        ref[pl.ds(j, n, stride=PACK_TILE), :] = word


def _packed_tiles_to_rows(ref, n, lead=(), dtype=BF16):
    lo, hi = [], []
    for j in range(PACK_TILE):
        word = ref[lead + (pl.ds(j, n, stride=PACK_TILE), slice(None))]
        lo.append(pltpu.bitcast(word << 16, F32).astype(dtype))
        hi.append(pltpu.bitcast(word & jnp.uint32(HIGH_HALF), F32).astype(dtype))
    return jnp.concatenate(lo + hi, axis=1)


def _mod_kernel(cs_ref, w_ref, b_ref, o_ref):
    cs = cs_ref[...]
    s = cs * _sigmoid(cs)
    o_ref[...] = _dot3(s, w_ref[...]) + b_ref[...]


def _modulation(cs, w_mod, b_mod):
    n_out = w_mod.shape[1]
    tn = 1536
    return pl.pallas_call(
        _mod_kernel,
        out_shape=jax.ShapeDtypeStruct((8, n_out), F32),
        grid=(n_out // tn,),
        in_specs=[
            pl.BlockSpec((8, D_MODEL), lambda j: (0, 0)),
            pl.BlockSpec((D_MODEL, tn), lambda j: (0, j)),
            pl.BlockSpec((1, tn), lambda j: (0, j)),
        ],
        out_specs=pl.BlockSpec((8, tn), lambda j: (0, j)),
        compiler_params=_params(("arbitrary",)),
        name="modulation",
    )(cs, w_mod, b_mod.reshape(1, n_out))


W_PAD_TN = SHIFT_PAD - SHIFT_COLS


def _w_in_layout_kernel(w_ref, o_ref, *, gap_tile):
    w = w_ref[...]
    o_ref[...] = _bf(jnp.where(pl.program_id(0) == gap_tile, jnp.zeros_like(w), w))


def _w_in_layout(w_in):
    tn = W_PAD_TN
    gap_tile = SHIFT_COLS // tn
    kern = functools.partial(_w_in_layout_kernel, gap_tile=gap_tile)
    src = lambda j: (0, jnp.where(j < gap_tile, j, jnp.maximum(j - 1, 0)))
    return pl.pallas_call(
        kern,
        out_shape=jax.ShapeDtypeStruct((D_MODEL, PROJ_COLS), BF16),
        grid=(PROJ_COLS // tn,),
        in_specs=[pl.BlockSpec((D_MODEL, tn), src)],
        out_specs=pl.BlockSpec((D_MODEL, tn), lambda j: (0, j)),
        compiler_params=_params(("arbitrary",)),
        name="w_in_layout",
    )(w_in)


def _in_proj_kernel(x_ref, xp_ref, xn_ref, g_ref, sc_ref, sh_ref, w_ref, mu_ref, o_ref, ob_ref, h_s, hh_s,
                    *, tm, n_shift):
    i = pl.program_id(0)
    j = pl.program_id(1)
    n_i = pl.num_programs(0)

    def norm_mod(xv):
        return (_rms(xv) * g_ref[...]) * (1.0 + sc_ref[...]) + sh_ref[...]

    @pl.when(j == 0)
    def _():
        h_s[...] = _bf(norm_mod(x_ref[...]))
        hp = jnp.where(i > 0, norm_mod(xp_ref[...]), 0.0)
        hn = jnp.where(i < n_i - 1, norm_mod(xn_ref[...]), 0.0)
        hh_s[0:8, :] = hp
        hh_s[8:16, :] = hn

    @pl.when(j < n_shift)
    def _():
        w = w_ref[...]
        p = _bdot(h_s[...], w)
        ph = _bdot(_bf(hh_s[...]), w)
        row = lax.broadcasted_iota(jnp.int32, p.shape, 0)
        prev = jnp.where(row == 0, ph[7:8, :], pltpu.roll(p, 1, 0))
        nxt = jnp.where(row == tm - 1, ph[8:9, :], pltpu.roll(p, tm - 1, 0))
        mu = mu_ref[...]
        o_ref[...] = p + mu[0:1, :] * (prev - p) + mu[1:2, :] * (nxt - p)

    @pl.when(j >= n_shift)
    def _():
        ob_ref[...] = _bf(_bdot(h_s[...], w_ref[...]))


def _in_proj(x2d, g, sc, sh, w_bf, mu_pad):
    t_len = x2d.shape[0]
    tm = min(t_len, PROJ_TM)
    tn = PROJ_TN
    n_shift = SHIFT_PAD // tn
    tb8 = tm // 8
    nb8 = t_len // 8
    kern = functools.partial(_in_proj_kernel, tm=tm, n_shift=n_shift)
    vec = lambda: pl.BlockSpec((1, D_MODEL), lambda i, j: (0, 0))
    return pl.pallas_call(
        kern,
        out_shape=[jax.ShapeDtypeStruct((t_len, SHIFT_PAD), F32),
                   jax.ShapeDtypeStruct((t_len, PROJ_COLS - SHIFT_PAD), BF16)],
        grid=(t_len // tm, PROJ_COLS // tn),
        in_specs=[
            pl.BlockSpec((tm, D_MODEL), lambda i, j: (i, 0)),
            pl.BlockSpec((8, D_MODEL), lambda i, j: (jnp.maximum(i * tb8 - 1, 0), 0)),
            pl.BlockSpec((8, D_MODEL), lambda i, j: (jnp.minimum((i + 1) * tb8, nb8 - 1), 0)),
            vec(), vec(), vec(),
            pl.BlockSpec((D_MODEL, tn), lambda i, j: (0, j)),
            pl.BlockSpec((2, tn), lambda i, j: (0, jnp.minimum(j, n_shift - 1))),
        ],
        out_specs=[pl.BlockSpec((tm, tn), lambda i, j: (i, jnp.minimum(j, n_shift - 1))),
                   pl.BlockSpec((tm, tn), lambda i, j: (i, jnp.maximum(j - n_shift, 0)))],
        scratch_shapes=[pltpu.VMEM((tm, D_MODEL), BF16), pltpu.VMEM((16, D_MODEL), F32)],
        compiler_params=_params(("arbitrary", "arbitrary")),
        name="in_proj",
    )(x2d, x2d, x2d, g, sc, sh, w_bf, mu_pad)


def _rwkv_prep_kernel(r_ref, k_ref, v_ref, lora_ref, kk_w, ka_w, rk_w, w0_ref, wup_ref, a0_ref, aup_ref, gup_ref,
                      bd_ref, kk_o, lw0_o, lw1_o, kd0_o, kd1_o, b0_o, b1_o, bonus_o, g_o):
    r = r_ref[...]
    k = k_ref[...]
    v = v_ref[...]
    lora = lora_ref[...]
    xw = lora[:, 0:W_LORA]
    xa = lora[:, W_LORA:W_LORA + ICLR_LORA]
    xg = lora[:, W_LORA + ICLR_LORA:]
    bd = bd_ref[...]

    kk = k * kk_w[...]
    nrm = jnp.sqrt(_dot_split_lhs(kk * kk, bd))
    kk = kk / jnp.maximum(nrm, 1e-12)
    kk_o[...] = kk

    tw = jnp.tanh(xw)
    lw_outs = (lw0_o, lw1_o)
    kd_outs = (kd0_o, kd1_o)
    b_outs = (b0_o, b1_o)
    kd_sum = None
    for d in range(2):
        z = w0_ref[d:d + 1, :] + _dot3(tw, wup_ref[d])
        w_log = -_softplus(-z) - 0.5
        lw_outs[d][...] = -jnp.exp(w_log)
        a_d = _sigmoid(a0_ref[d:d + 1, :] + _dot3(xa, aup_ref[d]))
        kd = k * (1.0 + (a_d - 1.0) * ka_w[...])
        kd_outs[d][...] = kd
        b_outs[d][...] = kk * a_d
        kd_sum = kd if kd_sum is None else kd_sum + kd
    g_o[...] = _bf(_dot3(_sigmoid(xg), gup_ref[...]))
    bonus_o[...] = _bf(_dot_split_lhs(r * kd_sum * rk_w[...], bd) * v)


def _rwkv_prep(proj, lp, bd):
    t_len = proj.shape[0]
    tm = min(t_len, 512)
    aw = A_WIDTH
    col = lambda c: pl.BlockSpec((tm, aw), lambda i: (i, c))
    vec = lambda: pl.BlockSpec((1, aw), lambda i: (0, 0))
    full = lambda shp: pl.BlockSpec(shp, lambda i: (0,) * len(shp))
    outs = [jax.ShapeDtypeStruct((t_len, aw), F32)] * 7 + [jax.ShapeDtypeStruct((t_len, aw), BF16)] * 2
    return pl.pallas_call(
        _rwkv_prep_kernel,
        out_shape=outs,
        grid=(t_len // tm,),
        in_specs=[
            col(0), col(1), col(2),
            pl.BlockSpec((tm, LORA_COLS), lambda i: (i, 3 * aw // LORA_COLS)),
            vec(), vec(), vec(),
            full((2, aw)), full((2, W_LORA, aw)), full((2, aw)), full((2, ICLR_LORA, aw)), full((G_LORA, aw)),
            full((aw, aw)),
        ],
        out_specs=[pl.BlockSpec((tm, aw), lambda i: (i, 0))] * 9,
        compiler_params=_params(("arbitrary",)),
        name="rwkv_prep",
    )(proj, proj, proj, proj, lp["k_k"], lp["k_a"], lp["r_k"], lp["w0"], lp["w_up"], lp["a0"], lp["a_up"],
      lp["g_up"], bd)


def _rwkv_scan_kernel(rf, vf, kkf, lwf, kdf, bf_, rb, vb, kkb, lwb, kdb, bb, s0_ref, of_ref, ob_ref, sfin_ref, s_s,
                      *, tb):
    step = pl.program_id(0)

    @pl.when(step == 0)
    def _():
        s_s[...] = s0_ref[...]

    c = RWKV_CHUNK
    n = A_HEAD_DIM
    assert c == n
    pw = 2 * n
    row = lax.broadcasted_iota(jnp.int32, (c, pw), 0)
    lane = lax.broadcasted_iota(jnp.int32, (c, pw), 1)
    col = lane % c
    first = lane < n
    eye = (row == col).astype(F32)
    r_c = lax.broadcasted_iota(jnp.int32, (c, c), 0)
    c_c = lax.broadcasted_iota(jnp.int32, (c, c), 1)
    n_ch = tb // c
    dirs = ((rf, vf, kkf, lwf, kdf, bf_, of_ref), (rb, vb, kkb, lwb, kdb, bb, ob_ref))
    pairs = range(A_HEADS // 2)
    psl = [slice(q * pw, (q + 1) * pw) for q in pairs]

    def bd(x):
        z = jnp.zeros_like(x)
        return jnp.concatenate([jnp.where(first, x, z), jnp.where(first, z, x)], axis=0)

    def chunk_terms(d, refs, ci):
        r_ref, v_ref, kk_ref, lw_ref, kd_ref, b_ref, _ = refs
        if d == 0:
            incl, strict, tri = row >= col, row > col, r_c >= c_c
        else:
            incl, strict, tri = row <= col, row < col, r_c <= c_c
        rows = slice(ci * c, (ci + 1) * c)
        lw = lw_ref[rows, :]
        r = r_ref[rows, :]
        kk = kk_ref[rows, :]
        kd = kd_ref[rows, :]
        b = b_ref[rows, :]
        cl = _dot_split_rhs(tri.astype(BF16), lw)
        cl_tot = cl[c - 1:c, :] if d == 0 else cl[0:1, :]
        e_neg = jnp.exp(-cl)
        e_end = jnp.exp(cl_tot - cl)
        g_tot = jnp.exp(cl_tot)
        rt = r * jnp.exp(cl)
        x_all = _bf(jnp.concatenate([-(kk * jnp.exp(cl - lw)), rt], axis=0))
        bt_all = _bf(b * e_neg)
        kt_all = _bf(kd * e_neg)
        return {
            "rows": rows, "strict": strict, "incl": incl,
            "x": [x_all[:, ps] for ps in psl],
            "z": [jnp.concatenate([bd(bt_all[:, ps]), bd(kt_all[:, ps])], axis=0) for ps in psl],
            "bd_bh": [bd(_bf(b * e_end)[:, ps]) for ps in psl],
            "bd_kh": [bd(_bf(kd * e_end)[:, ps]) for ps in psl],
            "bd_v": [bd(_bf(v_ref[rows, :])[:, ps]) for ps in psl],
            "rt": [rt[:, ps] for ps in psl],
            "g_tot": [g_tot[:, ps] for ps in psl],
        }

    chunks = [(d, cc) for cc in range(n_ch) for d in range(2)]
    prep = {(d, cc): chunk_terms(d, dirs[d], cc if d == 0 else n_ch - 1 - cc) for d, cc in chunks}
    units = [(k, q) for k in chunks for q in pairs]

    def per_unit(name, indexed=True):
        return [prep[k][name][q] if indexed else prep[k][name] for k, q in units]

    strict, incl = per_unit("strict", False), per_unit("incl", False)
    x_u, bd_bh, bd_kh, bd_v, rt_u = (per_unit(s) for s in ("x", "bd_bh", "bd_kh", "bd_v", "rt"))
    g = [_dot_nt(x, z) for x, z in zip(x_u, per_unit("z"))]
    a_ab = [jnp.where(s, gi[:c, :pw], 0.0) for s, gi in zip(strict, g)]
    a_rb = [_bf(jnp.where(i, gi[c:, :pw], 0.0)) for i, gi in zip(incl, g)]
    a_k = [_bf(jnp.concatenate([jnp.where(s, gi[:c, pw:], 0.0), jnp.where(i, gi[c:, pw:], 0.0)], axis=0))
           for s, i, gi in zip(strict, incl, g)]
    a2 = [_bdot(_bf(a), bd(_bf(a))) for a in a_ab]
    pa = [jnp.concatenate([eye + a, sq], axis=0) for a, sq in zip(a_ab, a2)]
    for _ in range(4):
        nxt = [_bdot(_bf(x), bd(_bf(x[c:]))) for x in pa]
        pa = [jnp.concatenate([x[:c] + y[:c], y[c:]], axis=0) for x, y in zip(pa, nxt)]
    p = [_bf(x[:c] + _bdot(_bf(x[:c]), bd(_bf(x[c:])))) for x in pa]
    vk = [_bdot(a, v) for a, v in zip(a_k, bd_v)]
    tw = [_bf(_bdot(pi, jnp.concatenate([bd(x[:c]), bd(_bf(vki[:c]))], axis=1)))
          for pi, vki, x in zip(p, vk, x_u)]
    bd_at = [bd(t[:, :pw]) for t in tw]
    bd_w = [bd(t[:, pw:]) for t in tw]
    mt = [_bf(_dot_tn(a, bh)) for a, bh in zip(bd_at, bd_bh)]
    ntf = [_dot_tn(jnp.concatenate([w, v], axis=0), jnp.concatenate([bh, kh], axis=0))
           for w, v, bh, kh in zip(bd_w, bd_v, bd_bh, bd_kh)]
    nt = [m[:n] + m[n:] for m in ntf]
    rw = [_bdot(a, jnp.concatenate([at, w], axis=1)) for a, at, w in zip(a_rb, bd_at, bd_w)]
    ry = [_bf(r + w[:, :pw]) for w, r in zip(rw, rt_u)]
    y0 = [w[:, pw:] + vki[c:] for w, vki in zip(rw, vk)]
    term = {u: vals for u, vals in zip(units, zip(mt, nt, ry, y0, per_unit("g_tot")))}

    state = [[s_s[d, q] for q in pairs] for d in range(2)]
    dq = [(d, q) for d in range(2) for q in pairs]
    for cc in range(n_ch):
        mt_c, nt_c, ry_c, y0_c, gt_c = zip(*[term[((d, cc), q)] for d, q in dq])
        s0 = [state[d][q] for d, q in dq]
        ys = [_dot_nt(ry_c[i], bd(_bf(s0[i]))) + y0_c[i] for i in range(len(dq))]
        upd = [_bdot(jnp.concatenate(_split(s0[i], 2), axis=0), mt_c[i]) for i in range(len(dq))]
        for i, (d, q) in enumerate(dq):
            state[d][q] = s0[i] * gt_c[i] + (upd[i][:n] + upd[i][n:]) + nt_c[i]
        for d in range(2):
            y_d = [ys[i] for i, (dd, _) in enumerate(dq) if dd == d]
            dirs[d][6][prep[(d, cc)]["rows"], :] = jnp.concatenate(y_d, axis=1)
    for d in range(2):
        for q in pairs:
            s_s[d, q] = state[d][q]

    @pl.when(step == pl.num_programs(0) - 1)
    def _():
        sfin_ref[...] = s_s[...]


def _rwkv_scan(proj, prep, s0):
    kk, lw0, lw1, kd0, kd1, b0, b1 = prep[:7]
    t_len = proj.shape[0]
    tb = 4 * RWKV_CHUNK
    nb = t_len // tb
    aw = A_WIDTH
    fwd = lambda c: pl.BlockSpec((tb, aw), lambda i: (i, c))
    bwd = lambda c: pl.BlockSpec((tb, aw), lambda i: (nb - 1 - i, c))
    st = pl.BlockSpec(RWKV_STATE_SHAPE, lambda i: (0, 0, 0, 0))
    kern = functools.partial(_rwkv_scan_kernel, tb=tb)
    return pl.pallas_call(
        kern,
        out_shape=[jax.ShapeDtypeStruct((t_len, aw), F32), jax.ShapeDtypeStruct((t_len, aw), F32),
                   jax.ShapeDtypeStruct(RWKV_STATE_SHAPE, F32)],
        grid=(nb,),
        in_specs=[fwd(0), fwd(2), fwd(0), fwd(0), fwd(0), fwd(0),
                  bwd(0), bwd(2), bwd(0), bwd(0), bwd(0), bwd(0), st],
        out_specs=[fwd(0), bwd(0), st],
        scratch_shapes=[pltpu.VMEM(RWKV_STATE_SHAPE, F32)],
        compiler_params=_params(("arbitrary",)),
        name="rwkv_scan",
    )(proj, proj, kk, lw0, kd0, b0, proj, proj, kk, lw1, kd1, b1, s0)


def _retention_kernel(qf, kf, vf, cosf, sinf, qb, kb, vb, cosb, sinb, lgt_ref, r0_ref, yf_ref, yb_ref, rfin_ref, r_s,
                      *, n_ch):
    step = pl.program_id(0)

    @pl.when(step == 0)
    def _():
        r_s[...] = r0_ref[...]

    c = RET_CHUNK
    dk = B_QK_DIM
    dv = B_V_DIM
    lg_all = -_softplus(-lgt_ref[...])
    rowf = lax.broadcasted_iota(jnp.int32, (c, c), 0).astype(F32)
    colf = lax.broadcasted_iota(jnp.int32, (c, c), 1).astype(F32)
    lane = lax.broadcasted_iota(jnp.int32, (c, dk), 1)
    first_half = (lane % 64) < 32
    dirs = ((qf, kf, vf, cosf, sinf, yf_ref), (qb, kb, vb, cosb, sinb, yb_ref))
    units = [(d, h) for d in range(2) for h in range(B_HEADS)]

    lg = [lg_all[d * B_HEADS + h:d * B_HEADS + h + 1, :] for d, h in units]
    diff = [(rowf - colf) if d == 0 else (colf - rowf) for d, _ in units]
    pos = [rowf if d == 0 else (c - 1.0) - rowf for d, _ in units]
    dmask = [jnp.where(df >= 0.0, jnp.exp(l * jnp.maximum(df, 0.0)), 0.0) for l, df in zip(lg, diff)]
    xi = [jnp.exp(l * (p + 1.0)) for l, p in zip(lg, pos)]
    zeta = [jnp.exp(l * ((c - 1.0) - p)) for l, p in zip(lg, pos)]
    g_chunk = [jnp.exp(l * float(c)) for l in lg]
    g_chunk = [jnp.concatenate([g, g], axis=1) for g in g_chunk]

    state = [r_s[d, h] for d, h in units]
    for cc in range(n_ch):
        rows = [slice((cc if d == 0 else n_ch - 1 - cc) * c, (cc if d == 0 else n_ch - 1 - cc) * c + c) for d in range(2)]
        cos = [dirs[d][3][rows[d], :] for d in range(2)]
        sin = [dirs[d][4][rows[d], :] for d in range(2)]

        def rope(xv, d):
            swapped = jnp.where(first_half, pltpu.roll(xv, dk - 32, 1), pltpu.roll(xv, 32, 1))
            return xv * cos[d] + swapped * sin[d]

        qh = [rope(dirs[d][0][rows[d], h * dk:(h + 1) * dk].astype(F32), d) for d, h in units]
        kh = [rope(dirs[d][1][rows[d], h * dk:(h + 1) * dk].astype(F32) * (dk ** -0.5), d) for d, h in units]
        vh = [dirs[d][2][rows[d], h * dv:(h + 1) * dv] for d, h in units]
        scores = [_dot_nt(_bf(q), _bf(k)) * m for q, k, m in zip(qh, kh, dmask)]
        inner = [_bdot(_bf(s), v) for s, v in zip(scores, vh)]
        cross = [_bdot(_bf(q * x), _bf(r)) for q, x, r in zip(qh, xi, state)]
        kv = [_dot_tn(_bf(k * z), v) for k, z, v in zip(kh, zeta, vh)]
        for i, (d, h) in enumerate(units):
            dirs[d][5][rows[d], h * dv:(h + 1) * dv] = inner[i] + cross[i]
        state = [g * r + x for g, r, x in zip(g_chunk, state, kv)]
    for i, (d, h) in enumerate(units):
        r_s[d, h] = state[i]

    @pl.when(step == pl.num_programs(0) - 1)
    def _():
        rfin_ref[...] = r_s[...]


def _retention(proj, cos, sin, lgt, r0):
    t_len = proj.shape[0]
    n_ch = min(4, t_len // RET_CHUNK)
    c = n_ch * RET_CHUNK
    nc = t_len // c
    qw = B_QK_WIDTH
    vw = B_V_WIDTH
    fq = lambda col: pl.BlockSpec((c, qw), lambda i: (i, col))
    bq = lambda col: pl.BlockSpec((c, qw), lambda i: (nc - 1 - i, col))
    st = pl.BlockSpec((2, B_HEADS, B_QK_DIM, B_V_DIM), lambda i: (0, 0, 0, 0))
    q_col = 0
    v_col = 2 * qw // vw
    return pl.pallas_call(
        functools.partial(_retention_kernel, n_ch=n_ch),
        out_shape=[jax.ShapeDtypeStruct((t_len, vw), F32), jax.ShapeDtypeStruct((t_len, vw), F32),
                   jax.ShapeDtypeStruct((2, B_HEADS, B_QK_DIM, B_V_DIM), F32)],
        grid=(nc,),
        in_specs=[
            fq(q_col), fq(q_col + 1), pl.BlockSpec((c, vw), lambda i: (i, v_col)),
            pl.BlockSpec((c, B_QK_DIM), lambda i: (i, 0)), pl.BlockSpec((c, B_QK_DIM), lambda i: (i, 0)),
            bq(q_col), bq(q_col + 1), pl.BlockSpec((c, vw), lambda i: (nc - 1 - i, v_col)),
            pl.BlockSpec((c, B_QK_DIM), lambda i: (nc - 1 - i, 0)),
            pl.BlockSpec((c, B_QK_DIM), lambda i: (nc - 1 - i, 0)),
            pl.BlockSpec((2 * B_HEADS, 128), lambda i: (0, 0)), st,
        ],
        out_specs=[pl.BlockSpec((c, vw), lambda i: (i, 0)), pl.BlockSpec((c, vw), lambda i: (nc - 1 - i, 0)), st],
        scratch_shapes=[pltpu.VMEM((2, B_HEADS, B_QK_DIM, B_V_DIM), F32)],
        compiler_params=_params(("arbitrary",)),
        name="retention",
    )(proj, proj, proj, cos, sin, proj, proj, proj, cos, sin, lgt, r0)


def _mix_out_kernel(of_ref, ob_ref, bonus_ref, g_ref, yf_ref, yb_ref, gb_ref, ga_ref, gbb_ref, x_ref,
                    alnw, alnb, rlnw, rlnb, npm, npf, g1_ref, sc2_ref, sh2_ref, wa_ref, wb_ref, wo_ref, bd_ref,
                    x1_ref, h2_ref, h3_ref):
    bd = bd_ref[...]
    o = of_ref[...] + ob_ref[...]
    mu = _dot_split_lhs(o, bd) * (1.0 / A_HEAD_DIM)
    oc = o - mu
    var = _dot_split_lhs(oc * oc, bd) * (1.0 / A_HEAD_DIM)
    ya = oc * lax.rsqrt(var + RWKV_GN_EPS) * alnw[...] + alnb[...] + bonus_ref[...].astype(F32)
    ya = _bdot(_bf(ya * g_ref[...].astype(F32)), wa_ref[...])

    y = yf_ref[...] + yb_ref[...]
    parts = []
    for h in range(B_HEADS):
        seg = y[:, h * B_V_DIM:(h + 1) * B_V_DIM]
        m = jnp.mean(seg, axis=-1, keepdims=True)
        sc = seg - m
        vr = jnp.mean(sc * sc, axis=-1, keepdims=True)
        parts.append(sc * lax.rsqrt(vr + RET_GN_EPS))
    yn = jnp.concatenate(parts, axis=1) * rlnw[...] + rlnb[...]
    gb = gb_ref[...].astype(F32)
    yb = _bdot(_bf(yn * (gb * _sigmoid(gb))), wb_ref[...])

    merged = _sigmoid(ga_ref[...].astype(F32)) * ya + _sigmoid(gbb_ref[...].astype(F32)) * yb
    mix = _bdot(_bf(merged), wo_ref[...])
    x1 = x_ref[...] + g1_ref[...] * (_rms(mix) * npm[...])
    x1_ref[...] = x1
    h2 = (_rms(x1) * npf[...]) * (1.0 + sc2_ref[...]) + sh2_ref[...]
    h2_ref[...] = h2
    _rows_to_packed_tiles(h2, h3_ref)


def _mix_out(x2d, proj, o_f, o_b, bonus, g, y_f, y_b, lp, vecs, bd):
    t_len = x2d.shape[0]
    tm = min(t_len, 256)
    aw = A_WIDTH
    d = D_MODEL
    ta = lambda: pl.BlockSpec((tm, aw), lambda i: (i, 0))
    td = lambda: pl.BlockSpec((tm, d), lambda i: (i, 0))
    pc = lambda c: pl.BlockSpec((tm, d), lambda i: (i, c))
    va = lambda: pl.BlockSpec((1, aw), lambda i: (0, 0))
    vd = lambda: pl.BlockSpec((1, d), lambda i: (0, 0))
    full = lambda shp: pl.BlockSpec(shp, lambda i: (0, 0))
    gcol = (2 * B_QK_WIDTH + B_V_WIDTH) // d
    return pl.pallas_call(
        _mix_out_kernel,
        out_shape=[jax.ShapeDtypeStruct((t_len, d), F32), jax.ShapeDtypeStruct((t_len, d), F32),
                   jax.ShapeDtypeStruct((t_len * PACK_TILE, LANES), jnp.uint32)],
        grid=(t_len // tm,),
        in_specs=[ta(), ta(), ta(), ta(), td(), td(), pc(gcol), pc(gcol + 1), pc(gcol + 2), td(),
                  va(), va(), vd(), vd(), vd(), vd(), vd(), vd(), vd(),
                  full((aw, d)), full((d, d)), full((d, d)), full((aw, aw))],
        out_specs=[td(), td(), pl.BlockSpec((tm * PACK_TILE, LANES), lambda i: (i, 0))],
        compiler_params=_params(("arbitrary",)),
        name="mix_out",
    )(o_f, o_b, bonus, g, y_f, y_b, proj, proj, proj, x2d,
      lp["ln_w"], lp["ln_b"], lp["ret_ln_w"], lp["ret_ln_b"], vecs["npm"], vecs["npf"], vecs["g1"], vecs["sc2"],
      vecs["sh2"], lp["w_a"], lp["w_b"], lp["w_o"], bd)


def _router_kernel(h_ref, rw_ref, bias_ref, sel_ref, wts_ref, rank_ref, cnt_ref, cnt_s, *, tm):
    step = pl.program_id(0)

    @pl.when(step == 0)
    def _():
        cnt_s[...] = jnp.zeros_like(cnt_s)

    ne = N_EXPERTS
    scores = _sigmoid(_dot3(h_ref[...], rw_ref[...]))
    work = scores + bias_ref[...]
    lane = lax.broadcasted_iota(jnp.int32, (tm, ne), 1).astype(F32)
    idxs = []
    vals = []
    for _ in range(TOP_K):
        m = jnp.max(work, axis=-1, keepdims=True)
        idx = jnp.min(jnp.where(work == m, lane, float(ne)), axis=-1, keepdims=True)
        oh = lane == idx
        vals.append(jnp.sum(jnp.where(oh, scores, 0.0), axis=-1, keepdims=True))
        idxs.append(idx)
        work = jnp.where(oh, -jnp.inf, work)
    sel_f = jnp.concatenate(idxs, axis=1)
    s_sel = jnp.concatenate(vals, axis=1)
    wts_ref[...] = s_sel / jnp.sum(s_sel, axis=1, keepdims=True) * ROUTED_SCALE
    sel_ref[...] = sel_f.astype(jnp.int32)

    hit = work == -jnp.inf
    r_i = lax.broadcasted_iota(jnp.int32, (tm, tm), 0)
    c_i = lax.broadcasted_iota(jnp.int32, (tm, tm), 1)
    before = _bdot((r_i > c_i).astype(BF16), hit.astype(BF16)) + cnt_s[...]
    ranks = [jnp.sum(jnp.where(lane == idxs[k], before, 0.0), axis=-1, keepdims=True) for k in range(TOP_K)]
    rank_ref[...] = jnp.concatenate(ranks, axis=1).astype(jnp.int32)
    cnt = cnt_s[...] + jnp.sum(hit.astype(F32), axis=0, keepdims=True)
    cnt_s[...] = cnt
    cnt_ref[...] = cnt.astype(jnp.int32)


def _router(h2, router_w, router_bias):
    t_len = h2.shape[0]
    tm = min(t_len, 512)
    kern = functools.partial(_router_kernel, tm=tm)
    tk = lambda: pl.BlockSpec((tm, TOP_K), lambda i: (i, 0))
    return pl.pallas_call(
        kern,
        out_shape=[jax.ShapeDtypeStruct((t_len, TOP_K), jnp.int32), jax.ShapeDtypeStruct((t_len, TOP_K), F32),
                   jax.ShapeDtypeStruct((t_len, TOP_K), jnp.int32), jax.ShapeDtypeStruct((1, N_EXPERTS), jnp.int32)],
        grid=(t_len // tm,),
        in_specs=[pl.BlockSpec((tm, D_MODEL), lambda i: (i, 0)),
                  pl.BlockSpec((D_MODEL, N_EXPERTS), lambda i: (0, 0)),
                  pl.BlockSpec((1, N_EXPERTS), lambda i: (0, 0))],
        out_specs=[tk(), tk(), tk(), pl.BlockSpec((1, N_EXPERTS), lambda i: (0, 0))],
        scratch_shapes=[pltpu.VMEM((1, N_EXPERTS), F32)],
        compiler_params=_params(("arbitrary",)),
        name="router",
    )(h2, router_w, router_bias.reshape(1, N_EXPERTS))


def _slot(sel_ref, rank_ref, start_ref, a):
    return start_ref[sel_ref[a]] + rank_ref[a]


def _dispatch_kernel(sel_ref, rank_ref, start_ref, cnt_ref, h_ref, xs_out, zrow, sem, zsem, *, tm):
    def row_copy(t, k):
        slot = _slot(sel_ref, rank_ref, start_ref, t * TOP_K + k)
        return pltpu.make_async_copy(_row_tile(h_ref, t), _row_tile(xs_out, slot), sem)

    def issue(t, carry):
        for k in range(TOP_K):
            row_copy(t, k).start(priority=k % 2)
        return carry

    lax.fori_loop(0, tm, issue, 0)

    def drain(t, carry):
        for k in range(TOP_K):
            row_copy(t, k).wait()
        return carry

    lax.fori_loop(0, tm, drain, 0)

    @pl.when(pl.program_id(0) == pl.num_programs(0) - 1)
    def _():
        zrow[...] = jnp.zeros_like(zrow)

        def for_pad_runs(e0, fn):
            def per_expert(e, carry):
                n = cnt_ref[e]
                n_pad = (n + MOE_BLOCK - 1) // MOE_BLOCK * MOE_BLOCK
                pad = n_pad - n
                off = start_ref[e] + n
                for bit in reversed(range(MOE_BLOCK.bit_length() - 1)):
                    size = 1 << bit
                    is_set = ((pad >> bit) & 1) == 1

                    @pl.when(is_set)
                    def _(off=off, size=size):
                        dst = xs_out.at[pl.ds(pl.multiple_of(off * PACK_TILE, PACK_TILE), size * PACK_TILE)]
                        fn(pltpu.make_async_copy(zrow.at[pl.ds(0, size * PACK_TILE)], dst, zsem))

                    off = off + jnp.where(is_set, size, 0)
                return carry

            lax.fori_loop(e0, e0 + ZERO_PAD_GROUP, per_expert, 0)

        def per_group(gi, carry):
            e0 = gi * ZERO_PAD_GROUP
            for_pad_runs(e0, lambda cp: cp.start())
            for_pad_runs(e0, lambda cp: cp.wait())
            return carry

        lax.fori_loop(0, N_EXPERTS // ZERO_PAD_GROUP, per_group, 0)


def _dispatch(h3, sel_flat, rank_flat, pad_start, counts, n_slots):
    t_len = h3.shape[0] // PACK_TILE
    tm = min(t_len, MOE_TOKEN_TILE)
    kern = functools.partial(_dispatch_kernel, tm=tm)
    smem_blk = lambda: pl.BlockSpec((tm * TOP_K,), lambda i: (i,), memory_space=pltpu.SMEM)
    smem_all = lambda: pl.BlockSpec((N_EXPERTS,), lambda i: (0,), memory_space=pltpu.SMEM)
    return pl.pallas_call(
        kern,
        out_shape=jax.ShapeDtypeStruct((n_slots * PACK_TILE, LANES), jnp.uint32),
        grid=(t_len // tm,),
        in_specs=[smem_blk(), smem_blk(), smem_all(), smem_all(),
                  pl.BlockSpec((tm * PACK_TILE, LANES), lambda i: (i, 0))],
        out_specs=pl.BlockSpec(memory_space=pl.ANY),
        scratch_shapes=[pltpu.VMEM((MOE_BLOCK // 2 * PACK_TILE, LANES), jnp.uint32), pltpu.SemaphoreType.DMA(()),
                        pltpu.SemaphoreType.DMA(())],
        compiler_params=_params(("arbitrary",)),
        name="dispatch",
    )(sel_flat, rank_flat, pad_start, counts, h3)


def _expert_kernel(be_ref, nu_ref, xs_ref, wg_ref, wu_ref, wd_ref, ys_ref, wg_s, wu_s, wd_s):
    b = pl.program_id(0)
    used = b < nu_ref[0]
    new_expert = (b == 0) | (be_ref[b] != be_ref[jnp.maximum(b - 1, 0)])

    @pl.when(used & new_expert)
    def _():
        wg_s[...] = _bf(wg_ref[0])
        wu_s[...] = _bf(wu_ref[0])
        wd_s[...] = _bf(wd_ref[0])

    @pl.when(used)
    def _():
        xb = _packed_tiles_to_rows(xs_ref, MOE_BLOCK)
        gate = _bdot(xb, wg_s[...])
        up = _bdot(xb, wu_s[...])
        act = gate * _sigmoid(gate) * up
        _rows_to_packed_tiles(_bdot(_bf(act), wd_s[...]), ys_ref)

    @pl.when(jnp.logical_not(used))
    def _():
        ys_ref[...] = jnp.zeros_like(ys_ref)


def _experts(xs, block_e, n_used, w_gate, w_up, w_down):
    n_slots = xs.shape[0] // PACK_TILE
    n_blocks = n_slots // MOE_BLOCK
    rows = (MOE_BLOCK * PACK_TILE, LANES)
    grid_spec = pltpu.PrefetchScalarGridSpec(
        num_scalar_prefetch=2,
        grid=(n_blocks,),
        in_specs=[
            pl.BlockSpec((MOE_BLOCK * PACK_TILE, LANES), lambda b, be, nu: (jnp.where(b < nu[0], b, 0), 0)),
            pl.BlockSpec((1, D_MODEL, EXPERT_FF), lambda b, be, nu: (be[b], 0, 0)),
            pl.BlockSpec((1, D_MODEL, EXPERT_FF), lambda b, be, nu: (be[b], 0, 0)),
            pl.BlockSpec((1, EXPERT_FF, D_MODEL), lambda b, be, nu: (be[b], 0, 0)),
        ],
        out_specs=pl.BlockSpec(rows, lambda b, be, nu: (b, 0)),
        scratch_shapes=[pltpu.VMEM((D_MODEL, EXPERT_FF), BF16), pltpu.VMEM((D_MODEL, EXPERT_FF), BF16),
                        pltpu.VMEM((EXPERT_FF, D_MODEL), BF16)],
    )
    return pl.pallas_call(
        _expert_kernel,
        out_shape=jax.ShapeDtypeStruct((n_slots * PACK_TILE, LANES), jnp.uint32),
        grid_spec=grid_spec,
        compiler_params=_params(("arbitrary",)),
        name="experts",
    )(block_e, n_used, xs, w_gate, w_up, w_down)


def _combine_kernel(sel_ref, rank_ref, start_ref, ys_ref, wts_ref, h_ref, x1_ref, sg_ref, su_ref, sd_ref, npo, g2_ref,
                    o_ref, buf, sem, *, tm):
    def row_copy(t, k):
        slot = _slot(sel_ref, rank_ref, start_ref, t * TOP_K + k)
        return pltpu.make_async_copy(_row_tile(ys_ref, slot), _row_tile(buf, t, (k,)), sem)

    def issue(t, carry):
        for k in range(TOP_K):
            row_copy(t, k).start(priority=k % 2)
        return carry

    lax.fori_loop(0, tm, issue, 0)

    hb = _bf(h_ref[...])
    gate = _bdot(hb, sg_ref[...])
    up = _bdot(hb, su_ref[...])
    shared = _bdot(_bf(gate * _sigmoid(gate) * up), sd_ref[...])

    def drain(t, carry):
        for k in range(TOP_K):
            row_copy(t, k).wait()
        return carry

    lax.fori_loop(0, tm, drain, 0)

    wts = wts_ref[...]
    routed = _packed_tiles_to_rows(buf, tm, (0,), F32) * wts[:, 0:1]
    for k in range(1, TOP_K):
        routed = routed + _packed_tiles_to_rows(buf, tm, (k,), F32) * wts[:, k:k + 1]
    o_ref[...] = x1_ref[...] + g2_ref[...] * (_rms(routed + shared) * npo[...])


def _combine(ys, sel_flat, rank_flat, pad_start, wts, h2, x1, sg, su, sd, npo, g2):
    t_len = h2.shape[0]
    tm = min(t_len, MOE_TOKEN_TILE)
    d = D_MODEL
    kern = functools.partial(_combine_kernel, tm=tm)
    smem_blk = lambda: pl.BlockSpec((tm * TOP_K,), lambda i: (i,), memory_space=pltpu.SMEM)
    td = lambda: pl.BlockSpec((tm, d), lambda i: (i, 0))
    vd = lambda: pl.BlockSpec((1, d), lambda i: (0, 0))
    return pl.pallas_call(
        kern,
        out_shape=jax.ShapeDtypeStruct((t_len, d), F32),
        grid=(t_len // tm,),
        in_specs=[smem_blk(), smem_blk(),
                  pl.BlockSpec((N_EXPERTS,), lambda i: (0,), memory_space=pltpu.SMEM),
                  pl.BlockSpec(memory_space=pl.ANY),
                  pl.BlockSpec((tm, TOP_K), lambda i: (i, 0)),
                  td(), td(),
                  pl.BlockSpec((d, SHARED_FF), lambda i: (0, 0)), pl.BlockSpec((d, SHARED_FF), lambda i: (0, 0)),
                  pl.BlockSpec((SHARED_FF, d), lambda i: (0, 0)), vd(), vd()],
        out_specs=td(),
        scratch_shapes=[pltpu.VMEM((TOP_K, tm * PACK_TILE, LANES), jnp.uint32), pltpu.SemaphoreType.DMA(())],
        compiler_params=_params(("arbitrary",)),
        name="combine",
    )(sel_flat, rank_flat, pad_start, ys, wts, h2, x1, sg, su, sd, npo, g2)


def _block_diag_ones(width, group):
    idx = np.arange(width) // group
    return jnp.asarray(idx[:, None] == idx[None, :], dtype=BF16)


def _rope_tables(t_len):
    n_rows = t_len // GRID_W
    quarter = B_QK_DIM // 4
    inv_freq = jnp.asarray(ROPE_BASE, F32) ** (-jnp.arange(quarter, dtype=F32) / quarter)
    ang_r = jnp.arange(n_rows, dtype=F32)[:, None] * inv_freq
    ang_c = jnp.arange(GRID_W, dtype=F32)[:, None] * inv_freq
    grid = (n_rows, GRID_W, quarter)
    by_row = lambda a: jnp.broadcast_to(a[:, None, :], grid)
    by_col = lambda a: jnp.broadcast_to(a[None, :, :], grid)
    cr, sr, cc, sc = by_row(jnp.cos(ang_r)), by_row(jnp.sin(ang_r)), by_col(jnp.cos(ang_c)), by_col(jnp.sin(ang_c))
    cos = jnp.concatenate([cr, cr, cc, cc], axis=-1).reshape(t_len, B_QK_DIM)
    sin = jnp.concatenate([-sr, sr, -sc, sc], axis=-1).reshape(t_len, B_QK_DIM)
    return cos, sin


def _token_mixer(x2d, g_pre, sc, sh, lp, cos, sin, states, bd):
    proj_a, proj_b = _in_proj(x2d, g_pre, sc, sh, lp["w_in"], lp["mu"])
    prep = _rwkv_prep(proj_a, lp, bd)
    o_f, o_b, s_fin = _rwkv_scan(proj_a, prep, states[0])
    y_f, y_b, r_fin = _retention(proj_b, cos, sin, lp["lgt"], states[1])
    return proj_b, prep, (o_f, o_b), (y_f, y_b), (s_fin, r_fin)


def kernel(x, c, ctx, c_ctx, w_mod, b_mod, norm_pre_mix, norm_post_mix, norm_pre_ffn, norm_post_ffn, w_in, shift_mu,
           rwkv_w0, rwkv_w_up, rwkv_a0, rwkv_a_up, rwkv_g_up, rwkv_k_k, rwkv_k_a, rwkv_r_k, rwkv_ln_w, rwkv_ln_b,
           w_branch_a, ret_decay_logit, ret_ln_w, ret_ln_b, w_branch_b, w_out, router_w, router_bias, exp_w_gate,
           exp_w_up, exp_w_down, sh_w_gate, sh_w_up, sh_w_down):
    d = D_MODEL
    assert x.shape[0] == 1 and w_in.shape[0] == 1, "single batch element, single layer"
    t_len = x.shape[1]
    x2d = x.reshape(t_len, d)
    ctx2d = ctx.reshape(ctx.shape[1], d)
    row = lambda a: a.reshape(1, -1)

    cs = jnp.zeros((8, d), F32).at[0].set(c[0]).at[1].set(c_ctx)
    mod = _modulation(cs, w_mod[0], b_mod[0])
    sh1, sc1, g1, sh2, sc2, g2 = [mod[0:1, i * d:(i + 1) * d] for i in range(6)]
    csh1, csc1 = mod[1:2, 0:d], mod[1:2, d:2 * d]

    lp = {
        "w_in": _w_in_layout(w_in[0]),
        "mu": jnp.pad(shift_mu[0], ((0, 0), (0, SHIFT_PAD - SHIFT_COLS))),
        "k_k": row(rwkv_k_k[0]), "k_a": row(rwkv_k_a[0]), "r_k": row(rwkv_r_k[0]),
        "w0": rwkv_w0[0], "w_up": rwkv_w_up[0], "a0": rwkv_a0[0], "a_up": rwkv_a_up[0], "g_up": rwkv_g_up[0],
        "ln_w": row(rwkv_ln_w[0]), "ln_b": row(rwkv_ln_b[0]),
        "ret_ln_w": row(ret_ln_w[0]), "ret_ln_b": row(ret_ln_b[0]),
        "lgt": jnp.broadcast_to(ret_decay_logit[0].reshape(2 * B_HEADS, 1), (2 * B_HEADS, 128)),
        "w_a": w_branch_a[0].astype(BF16), "w_b": w_branch_b[0].astype(BF16), "w_o": w_out[0].astype(BF16),
    }
    bd = _block_diag_ones(A_WIDTH, A_HEAD_DIM)
    g_pre = row(norm_pre_mix[0])

    t_ctx = ctx2d.shape[0]
    zero_states = (jnp.zeros(RWKV_STATE_SHAPE, F32),
                   jnp.zeros((2, B_HEADS, B_QK_DIM, B_V_DIM), F32))
    ones = jnp.ones((t_ctx, B_QK_DIM), F32)
    *_, ctx_states = _token_mixer(ctx2d, g_pre, csc1, csh1, lp, ones, jnp.zeros_like(ones), zero_states, bd)

    cos, sin = _rope_tables(t_len)
    proj, prep, (o_f, o_b), (y_f, y_b), _ = _token_mixer(x2d, g_pre, sc1, sh1, lp, cos, sin, ctx_states, bd)
    vecs = {"npm": row(norm_post_mix[0]), "npf": row(norm_pre_ffn[0]), "g1": g1, "sc2": sc2, "sh2": sh2}
    x1, h2, h3 = _mix_out(x2d, proj, o_f, o_b, prep[7], prep[8], y_f, y_b, lp, vecs, bd)

    sel, wts, rank, counts = _router(h2, router_w[0], router_bias[0])
    counts = counts.reshape(N_EXPERTS)
    padded = (counts + MOE_BLOCK - 1) // MOE_BLOCK * MOE_BLOCK
    pad_end = jnp.cumsum(padded)
    pad_start = (pad_end - padded).astype(jnp.int32)
    n_assign = t_len * TOP_K
    n_blocks = (n_assign + N_EXPERTS * (MOE_BLOCK - 1) + MOE_BLOCK - 1) // MOE_BLOCK
    block_start = jnp.arange(n_blocks, dtype=jnp.int32) * MOE_BLOCK
    block_e = jnp.minimum(jnp.sum(pad_end[None, :] <= block_start[:, None], axis=1), N_EXPERTS - 1).astype(jnp.int32)
    n_used = (pad_end[-1:] // MOE_BLOCK).astype(jnp.int32)
    sel_flat = sel.reshape(n_assign)
    rank_flat = rank.reshape(n_assign)

    xs = _dispatch(h3, sel_flat, rank_flat, pad_start, counts, n_blocks * MOE_BLOCK)
    ys = _experts(xs, block_e, n_used, exp_w_gate[0], exp_w_up[0], exp_w_down[0])
    out = _combine(ys, sel_flat, rank_flat, pad_start, wts, h2, x1, sh_w_gate[0].astype(BF16),
                   sh_w_up[0].astype(BF16), sh_w_down[0].astype(BF16), row(norm_post_ffn[0]), g2)
    return out.reshape(x.shape)
```

```python
import functools

import jax
import jax.numpy as jnp
import numpy as np
from jax import lax
from jax.experimental import pallas as pl
from jax.experimental.pallas import tpu as pltpu

F32 = jnp.float32
BF16 = jnp.bfloat16

D_MODEL = 1024
GRID_W = 64
NORM_EPS = 1e-6

A_HEAD_DIM = 64
A_WIDTH = D_MODEL // 2
A_HEADS = A_WIDTH // A_HEAD_DIM
W_LORA = 64
ICLR_LORA = 64
G_LORA = 128
RWKV_GN_EPS = 64e-5

B_HEADS = 4
B_QK_WIDTH = D_MODEL // 2
B_V_WIDTH = D_MODEL
B_QK_DIM = B_QK_WIDTH // B_HEADS
B_V_DIM = B_V_WIDTH // B_HEADS
RET_CHUNK = 128
RET_GN_EPS = 1e-5
ROPE_BASE = 10000.0

SHIFT_COLS = 3 * A_WIDTH + W_LORA + ICLR_LORA + G_LORA
LORA_COLS = W_LORA + ICLR_LORA + G_LORA
SHIFT_PAD = 2048
PROJ_COLS = SHIFT_PAD + 2 * B_QK_WIDTH + 2 * B_V_WIDTH + 2 * D_MODEL

N_EXPERTS = 256
TOP_K = 8
EXPERT_FF = D_MODEL // 4
SHARED_FF = D_MODEL // 4
ROUTED_SCALE = 2.5
MOE_BLOCK = 512
ZERO_PAD_GROUP = 16
MOE_TOKEN_TILE = 512

RWKV_CHUNK = 64
RWKV_STATE_SHAPE = (2, A_HEADS // 2, A_HEAD_DIM, 2 * A_HEAD_DIM)
PROJ_TN = 1024
PROJ_TM = 1024
VMEM_LIMIT = 48 * 1024 * 1024


def _params(sem):
    return pltpu.CompilerParams(dimension_semantics=sem, vmem_limit_bytes=VMEM_LIMIT)


def _bf(a):
    return a.astype(BF16)


def _bdot(a, b):
    return jnp.dot(a, b, preferred_element_type=F32)


def _dot_nt(a, b):
    return lax.dot_general(a, b, (((1,), (1,)), ((), ())), preferred_element_type=F32)


def _dot_tn(a, b):
    return lax.dot_general(a, b, (((0,), (0,)), ((), ())), preferred_element_type=F32)


def _split(a, n):
    out = []
    rem = a
    for _ in range(n):
        p = _bf(rem)
        out.append(p)
        rem = rem - p.astype(F32)
    return out


def _dot_split_lhs(a, b_bf, n=3):
    acc = None
    for p in _split(a, n):
        t = _bdot(p, b_bf)
        acc = t if acc is None else acc + t
    return acc


def _dot_split_rhs(a_bf, b, n=3):
    acc = None
    for p in _split(b, n):
        t = _bdot(a_bf, p)
        acc = t if acc is None else acc + t
    return acc


def _dot3(a, b):
    ah, al = _split(a, 2)
    bh, bl = _split(b, 2)
    return _bdot(ah, bh) + (_bdot(ah, bl) + _bdot(al, bh))


def _sigmoid(x):
    return 1.0 / (1.0 + jnp.exp(-x))


def _softplus(x):
    return jnp.maximum(x, 0.0) + jnp.log1p(jnp.exp(-jnp.abs(x)))


def _rms(x):
    return x * lax.rsqrt(jnp.mean(x * x, axis=-1, keepdims=True) + NORM_EPS)


LANES = 128


PACK_TILE = D_MODEL // LANES // 2
HIGH_HALF = 0xFFFF0000


def _row_tile(ref, r, lead=()):
    return ref.at[lead + (pl.ds(pl.multiple_of(r * PACK_TILE, PACK_TILE), PACK_TILE),)]


def _rows_to_packed_tiles(x2d, ref):
    n = x2d.shape[0]
    half = D_MODEL // 2
    for j in range(PACK_TILE):
        lo = _bf(x2d[:, j * LANES:(j + 1) * LANES]).astype(F32)
        hi = _bf(x2d[:, half + j * LANES:half + (j + 1) * LANES]).astype(F32)
        word = (pltpu.bitcast(hi, jnp.uint32) & jnp.uint32(HIGH_HALF)) | (pltpu.bitcast(lo, jnp.uint32) >> 16)
        ref[pl.ds(j, n, stride=PACK_TILE), :] = word


def _packed_tiles_to_rows(ref, n, lead=(), dtype=BF16):
    lo, hi = [], []
    for j in range(PACK_TILE):
        word = ref[lead + (pl.ds(j, n, stride=PACK_TILE), slice(None))]
        lo.append(pltpu.bitcast(word << 16, F32).astype(dtype))
        hi.append(pltpu.bitcast(word & jnp.uint32(HIGH_HALF), F32).astype(dtype))
    return jnp.concatenate(lo + hi, axis=1)


def _mod_kernel(cs_ref, w_ref, b_ref, o_ref):
    cs = cs_ref[...]
    s = cs * _sigmoid(cs)
    o_ref[...] = _dot3(s, w_ref[...]) + b_ref[...]


def _modulation(cs, w_mod, b_mod):
    n_out = w_mod.shape[1]
    tn = 1536
    return pl.pallas_call(
        _mod_kernel,
        out_shape=jax.ShapeDtypeStruct((8, n_out), F32),
        grid=(n_out // tn,),
        in_specs=[
            pl.BlockSpec((8, D_MODEL), lambda j: (0, 0)),
            pl.BlockSpec((D_MODEL, tn), lambda j: (0, j)),
            pl.BlockSpec((1, tn), lambda j: (0, j)),
        ],
        out_specs=pl.BlockSpec((8, tn), lambda j: (0, j)),
        compiler_params=_params(("arbitrary",)),
        name="modulation",
    )(cs, w_mod, b_mod.reshape(1, n_out))


W_PAD_TN = SHIFT_PAD - SHIFT_COLS


def _w_in_layout_kernel(w_ref, o_ref, *, gap_tile):
    w = w_ref[...]
    o_ref[...] = _bf(jnp.where(pl.program_id(0) == gap_tile, jnp.zeros_like(w), w))


def _w_in_layout(w_in):
    tn = W_PAD_TN
    gap_tile = SHIFT_COLS // tn
    kern = functools.partial(_w_in_layout_kernel, gap_tile=gap_tile)
    src = lambda j: (0, jnp.where(j < gap_tile, j, jnp.maximum(j - 1, 0)))
    return pl.pallas_call(
        kern,
        out_shape=jax.ShapeDtypeStruct((D_MODEL, PROJ_COLS), BF16),
        grid=(PROJ_COLS // tn,),
        in_specs=[pl.BlockSpec((D_MODEL, tn), src)],
        out_specs=pl.BlockSpec((D_MODEL, tn), lambda j: (0, j)),
        compiler_params=_params(("arbitrary",)),
        name="w_in_layout",
    )(w_in)


def _in_proj_kernel(x_ref, xp_ref, xn_ref, g_ref, sc_ref, sh_ref, w_ref, mu_ref, o_ref, ob_ref, h_s, hh_s,
                    *, tm, n_shift):
    i = pl.program_id(0)
    j = pl.program_id(1)
    n_i = pl.num_programs(0)

    def norm_mod(xv):
        return (_rms(xv) * g_ref[...]) * (1.0 + sc_ref[...]) + sh_ref[...]

    @pl.when(j == 0)
    def _():
        h_s[...] = _bf(norm_mod(x_ref[...]))
        hp = jnp.where(i > 0, norm_mod(xp_ref[...]), 0.0)
        hn = jnp.where(i < n_i - 1, norm_mod(xn_ref[...]), 0.0)
        hh_s[0:8, :] = hp
        hh_s[8:16, :] = hn

    @pl.when(j < n_shift)
    def _():
        w = w_ref[...]
        p = _bdot(h_s[...], w)
        ph = _bdot(_bf(hh_s[...]), w)
        row = lax.broadcasted_iota(jnp.int32, p.shape, 0)
        prev = jnp.where(row == 0, ph[7:8, :], pltpu.roll(p, 1, 0))
        nxt = jnp.where(row == tm - 1, ph[8:9, :], pltpu.roll(p, tm - 1, 0))
        mu = mu_ref[...]
        o_ref[...] = p + mu[0:1, :] * (prev - p) + mu[1:2, :] * (nxt - p)

    @pl.when(j >= n_shift)
    def _():
        ob_ref[...] = _bf(_bdot(h_s[...], w_ref[...]))


def _in_proj(x2d, g, sc, sh, w_bf, mu_pad):
    t_len = x2d.shape[0]
    tm = min(t_len, PROJ_TM)
    tn = PROJ_TN
    n_shift = SHIFT_PAD // tn
    tb8 = tm // 8
    nb8 = t_len // 8
    kern = functools.partial(_in_proj_kernel, tm=tm, n_shift=n_shift)
    vec = lambda: pl.BlockSpec((1, D_MODEL), lambda i, j: (0, 0))
    return pl.pallas_call(
        kern,
        out_shape=[jax.ShapeDtypeStruct((t_len, SHIFT_PAD), F32),
                   jax.ShapeDtypeStruct((t_len, PROJ_COLS - SHIFT_PAD), BF16)],
        grid=(t_len // tm, PROJ_COLS // tn),
        in_specs=[
            pl.BlockSpec((tm, D_MODEL), lambda i, j: (i, 0)),
            pl.BlockSpec((8, D_MODEL), lambda i, j: (jnp.maximum(i * tb8 - 1, 0), 0)),
            pl.BlockSpec((8, D_MODEL), lambda i, j: (jnp.minimum((i + 1) * tb8, nb8 - 1), 0)),
            vec(), vec(), vec(),
            pl.BlockSpec((D_MODEL, tn), lambda i, j: (0, j)),
            pl.BlockSpec((2, tn), lambda i, j: (0, jnp.minimum(j, n_shift - 1))),
        ],
        out_specs=[pl.BlockSpec((tm, tn), lambda i, j: (i, jnp.minimum(j, n_shift - 1))),
                   pl.BlockSpec((tm, tn), lambda i, j: (i, jnp.maximum(j - n_shift, 0)))],
        scratch_shapes=[pltpu.VMEM((tm, D_MODEL), BF16), pltpu.VMEM((16, D_MODEL), F32)],
        compiler_params=_params(("arbitrary", "arbitrary")),
        name="in_proj",
    )(x2d, x2d, x2d, g, sc, sh, w_bf, mu_pad)


def _rwkv_prep_kernel(r_ref, k_ref, v_ref, lora_ref, kk_w, ka_w, rk_w, w0_ref, wup_ref, a0_ref, aup_ref, gup_ref,
                      bd_ref, kk_o, lw0_o, lw1_o, kd0_o, kd1_o, b0_o, b1_o, bonus_o, g_o):
    r = r_ref[...]
    k = k_ref[...]
    v = v_ref[...]
    lora = lora_ref[...]
    xw = lora[:, 0:W_LORA]
    xa = lora[:, W_LORA:W_LORA + ICLR_LORA]
    xg = lora[:, W_LORA + ICLR_LORA:]
    bd = bd_ref[...]

    kk = k * kk_w[...]
    nrm = jnp.sqrt(_dot_split_lhs(kk * kk, bd))
    kk = kk / jnp.maximum(nrm, 1e-12)
    kk_o[...] = kk

    tw = jnp.tanh(xw)
    lw_outs = (lw0_o, lw1_o)
    kd_outs = (kd0_o, kd1_o)
    b_outs = (b0_o, b1_o)
    kd_sum = None
    for d in range(2):
        z = w0_ref[d:d + 1, :] + _dot3(tw, wup_ref[d])
        w_log = -_softplus(-z) - 0.5
        lw_outs[d][...] = -jnp.exp(w_log)
        a_d = _sigmoid(a0_ref[d:d + 1, :] + _dot3(xa, aup_ref[d]))
        kd = k * (1.0 + (a_d - 1.0) * ka_w[...])
        kd_outs[d][...] = kd
        b_outs[d][...] = kk * a_d
        kd_sum = kd if kd_sum is None else kd_sum + kd
    g_o[...] = _bf(_dot3(_sigmoid(xg), gup_ref[...]))
    bonus_o[...] = _bf(_dot_split_lhs(r * kd_sum * rk_w[...], bd) * v)


def _rwkv_prep(proj, lp, bd):
    t_len = proj.shape[0]
    tm = min(t_len, 512)
    aw = A_WIDTH
    col = lambda c: pl.BlockSpec((tm, aw), lambda i: (i, c))
    vec = lambda: pl.BlockSpec((1, aw), lambda i: (0, 0))
    full = lambda shp: pl.BlockSpec(shp, lambda i: (0,) * len(shp))
    outs = [jax.ShapeDtypeStruct((t_len, aw), F32)] * 7 + [jax.ShapeDtypeStruct((t_len, aw), BF16)] * 2
    return pl.pallas_call(
        _rwkv_prep_kernel,
        out_shape=outs,
        grid=(t_len // tm,),
        in_specs=[
            col(0), col(1), col(2),
            pl.BlockSpec((tm, LORA_COLS), lambda i: (i, 3 * aw // LORA_COLS)),
            vec(), vec(), vec(),
            full((2, aw)), full((2, W_LORA, aw)), full((2, aw)), full((2, ICLR_LORA, aw)), full((G_LORA, aw)),
            full((aw, aw)),
        ],
        out_specs=[pl.BlockSpec((tm, aw), lambda i: (i, 0))] * 9,
        compiler_params=_params(("arbitrary",)),
        name="rwkv_prep",
    )(proj, proj, proj, proj, lp["k_k"], lp["k_a"], lp["r_k"], lp["w0"], lp["w_up"], lp["a0"], lp["a_up"],
      lp["g_up"], bd)


def _rwkv_scan_kernel(rf, vf, kkf, lwf, kdf, bf_, rb, vb, kkb, lwb, kdb, bb, s0_ref, of_ref, ob_ref, sfin_ref, s_s,
                      *, tb):
    step = pl.program_id(0)

    @pl.when(step == 0)
    def _():
        s_s[...] = s0_ref[...]

    c = RWKV_CHUNK
    n = A_HEAD_DIM
    assert c == n
    pw = 2 * n
    row = lax.broadcasted_iota(jnp.int32, (c, pw), 0)
    lane = lax.broadcasted_iota(jnp.int32, (c, pw), 1)
    col = lane % c
    first = lane < n
    eye = (row == col).astype(F32)
    r_c = lax.broadcasted_iota(jnp.int32, (c, c), 0)
    c_c = lax.broadcasted_iota(jnp.int32, (c, c), 1)
    n_ch = tb // c
    dirs = ((rf, vf, kkf, lwf, kdf, bf_, of_ref), (rb, vb, kkb, lwb, kdb, bb, ob_ref))
    pairs = range(A_HEADS // 2)
    psl = [slice(q * pw, (q + 1) * pw) for q in pairs]

    def bd(x):
        z = jnp.zeros_like(x)
        return jnp.concatenate([jnp.where(first, x, z), jnp.where(first, z, x)], axis=0)

    def chunk_terms(d, refs, ci):
        r_ref, v_ref, kk_ref, lw_ref, kd_ref, b_ref, _ = refs
        if d == 0:
            incl, strict, tri = row >= col, row > col, r_c >= c_c
        else:
            incl, strict, tri = row <= col, row < col, r_c <= c_c
        rows = slice(ci * c, (ci + 1) * c)
        lw = lw_ref[rows, :]
        r = r_ref[rows, :]
        kk = kk_ref[rows, :]
        kd = kd_ref[rows, :]
        b = b_ref[rows, :]
        cl = _dot_split_rhs(tri.astype(BF16), lw)
        cl_tot = cl[c - 1:c, :] if d == 0 else cl[0:1, :]
        e_neg = jnp.exp(-cl)
        e_end = jnp.exp(cl_tot - cl)
        g_tot = jnp.exp(cl_tot)
        rt = r * jnp.exp(cl)
        x_all = _bf(jnp.concatenate([-(kk * jnp.exp(cl - lw)), rt], axis=0))
        bt_all = _bf(b * e_neg)
        kt_all = _bf(kd * e_neg)
        return {
            "rows": rows, "strict": strict, "incl": incl,
            "x": [x_all[:, ps] for ps in psl],
            "z": [jnp.concatenate([bd(bt_all[:, ps]), bd(kt_all[:, ps])], axis=0) for ps in psl],
            "bd_bh": [bd(_bf(b * e_end)[:, ps]) for ps in psl],
            "bd_kh": [bd(_bf(kd * e_end)[:, ps]) for ps in psl],
            "bd_v": [bd(_bf(v_ref[rows, :])[:, ps]) for ps in psl],
            "rt": [rt[:, ps] for ps in psl],
            "g_tot": [g_tot[:, ps] for ps in psl],
        }

    chunks = [(d, cc) for cc in range(n_ch) for d in range(2)]
    prep = {(d, cc): chunk_terms(d, dirs[d], cc if d == 0 else n_ch - 1 - cc) for d, cc in chunks}
    units = [(k, q) for k in chunks for q in pairs]

    def per_unit(name, indexed=True):
        return [prep[k][name][q] if indexed else prep[k][name] for k, q in units]

    strict, incl = per_unit("strict", False), per_unit("incl", False)
    x_u, bd_bh, bd_kh, bd_v, rt_u = (per_unit(s) for s in ("x", "bd_bh", "bd_kh", "bd_v", "rt"))
    g = [_dot_nt(x, z) for x, z in zip(x_u, per_unit("z"))]
    a_ab = [jnp.where(s, gi[:c, :pw], 0.0) for s, gi in zip(strict, g)]
    a_rb = [_bf(jnp.where(i, gi[c:, :pw], 0.0)) for i, gi in zip(incl, g)]
    a_k = [_bf(jnp.concatenate([jnp.where(s, gi[:c, pw:], 0.0), jnp.where(i, gi[c:, pw:], 0.0)], axis=0))
           for s, i, gi in zip(strict, incl, g)]
    a2 = [_bdot(_bf(a), bd(_bf(a))) for a in a_ab]
    pa = [jnp.concatenate([eye + a, sq], axis=0) for a, sq in zip(a_ab, a2)]
    for _ in range(4):
        nxt = [_bdot(_bf(x), bd(_bf(x[c:]))) for x in pa]
        pa = [jnp.concatenate([x[:c] + y[:c], y[c:]], axis=0) for x, y in zip(pa, nxt)]
    p = [_bf(x[:c] + _bdot(_bf(x[:c]), bd(_bf(x[c:])))) for x in pa]
    vk = [_bdot(a, v) for a, v in zip(a_k, bd_v)]
    tw = [_bf(_bdot(pi, jnp.concatenate([bd(x[:c]), bd(_bf(vki[:c]))], axis=1)))
          for pi, vki, x in zip(p, vk, x_u)]
    bd_at = [bd(t[:, :pw]) for t in tw]
    bd_w = [bd(t[:, pw:]) for t in tw]
    mt = [_bf(_dot_tn(a, bh)) for a, bh in zip(bd_at, bd_bh)]
    ntf = [_dot_tn(jnp.concatenate([w, v], axis=0), jnp.concatenate([bh, kh], axis=0))
           for w, v, bh, kh in zip(bd_w, bd_v, bd_bh, bd_kh)]
    nt = [m[:n] + m[n:] for m in ntf]
    rw = [_bdot(a, jnp.concatenate([at, w], axis=1)) for a, at, w in zip(a_rb, bd_at, bd_w)]
    ry = [_bf(r + w[:, :pw]) for w, r in zip(rw, rt_u)]
    y0 = [w[:, pw:] + vki[c:] for w, vki in zip(rw, vk)]
    term = {u: vals for u, vals in zip(units, zip(mt, nt, ry, y0, per_unit("g_tot")))}

    state = [[s_s[d, q] for q in pairs] for d in range(2)]
    dq = [(d, q) for d in range(2) for q in pairs]
    for cc in range(n_ch):
        mt_c, nt_c, ry_c, y0_c, gt_c = zip(*[term[((d, cc), q)] for d, q in dq])
        s0 = [state[d][q] for d, q in dq]
        ys = [_dot_nt(ry_c[i], bd(_bf(s0[i]))) + y0_c[i] for i in range(len(dq))]
        upd = [_bdot(jnp.concatenate(_split(s0[i], 2), axis=0), mt_c[i]) for i in range(len(dq))]
        for i, (d, q) in enumerate(dq):
            state[d][q] = s0[i] * gt_c[i] + (upd[i][:n] + upd[i][n:]) + nt_c[i]
        for d in range(2):
            y_d = [ys[i] for i, (dd, _) in enumerate(dq) if dd == d]
            dirs[d][6][prep[(d, cc)]["rows"], :] = jnp.concatenate(y_d, axis=1)
    for d in range(2):
        for q in pairs:
            s_s[d, q] = state[d][q]

    @pl.when(step == pl.num_programs(0) - 1)
    def _():
        sfin_ref[...] = s_s[...]


def _rwkv_scan(proj, prep, s0):
    kk, lw0, lw1, kd0, kd1, b0, b1 = prep[:7]
    t_len = proj.shape[0]
    tb = 4 * RWKV_CHUNK
    nb = t_len // tb
    aw = A_WIDTH
    fwd = lambda c: pl.BlockSpec((tb, aw), lambda i: (i, c))
    bwd = lambda c: pl.BlockSpec((tb, aw), lambda i: (nb - 1 - i, c))
    st = pl.BlockSpec(RWKV_STATE_SHAPE, lambda i: (0, 0, 0, 0))
    kern = functools.partial(_rwkv_scan_kernel, tb=tb)
    return pl.pallas_call(
        kern,
        out_shape=[jax.ShapeDtypeStruct((t_len, aw), F32), jax.ShapeDtypeStruct((t_len, aw), F32),
                   jax.ShapeDtypeStruct(RWKV_STATE_SHAPE, F32)],
        grid=(nb,),
        in_specs=[fwd(0), fwd(2), fwd(0), fwd(0), fwd(0), fwd(0),
                  bwd(0), bwd(2), bwd(0), bwd(0), bwd(0), bwd(0), st],
        out_specs=[fwd(0), bwd(0), st],
        scratch_shapes=[pltpu.VMEM(RWKV_STATE_SHAPE, F32)],
        compiler_params=_params(("arbitrary",)),
        name="rwkv_scan",
    )(proj, proj, kk, lw0, kd0, b0, proj, proj, kk, lw1, kd1, b1, s0)


def _retention_kernel(qf, kf, vf, cosf, sinf, qb, kb, vb, cosb, sinb, lgt_ref, r0_ref, yf_ref, yb_ref, rfin_ref, r_s,
                      *, n_ch):
    step = pl.program_id(0)

    @pl.when(step == 0)
    def _():
        r_s[...] = r0_ref[...]

    c = RET_CHUNK
    dk = B_QK_DIM
    dv = B_V_DIM
    lg_all = -_softplus(-lgt_ref[...])
    rowf = lax.broadcasted_iota(jnp.int32, (c, c), 0).astype(F32)
    colf = lax.broadcasted_iota(jnp.int32, (c, c), 1).astype(F32)
    lane = lax.broadcasted_iota(jnp.int32, (c, dk), 1)
    first_half = (lane % 64) < 32
    dirs = ((qf, kf, vf, cosf, sinf, yf_ref), (qb, kb, vb, cosb, sinb, yb_ref))
    units = [(d, h) for d in range(2) for h in range(B_HEADS)]

    lg = [lg_all[d * B_HEADS + h:d * B_HEADS + h + 1, :] for d, h in units]
    diff = [(rowf - colf) if d == 0 else (colf - rowf) for d, _ in units]
    pos = [rowf if d == 0 else (c - 1.0) - rowf for d, _ in units]
    dmask = [jnp.where(df >= 0.0, jnp.exp(l * jnp.maximum(df, 0.0)), 0.0) for l, df in zip(lg, diff)]
    xi = [jnp.exp(l * (p + 1.0)) for l, p in zip(lg, pos)]
    zeta = [jnp.exp(l * ((c - 1.0) - p)) for l, p in zip(lg, pos)]
    g_chunk = [jnp.exp(l * float(c)) for l in lg]
    g_chunk = [jnp.concatenate([g, g], axis=1) for g in g_chunk]

    state = [r_s[d, h] for d, h in units]
    for cc in range(n_ch):
        rows = [slice((cc if d == 0 else n_ch - 1 - cc) * c, (cc if d == 0 else n_ch - 1 - cc) * c + c) for d in range(2)]
        cos = [dirs[d][3][rows[d], :] for d in range(2)]
        sin = [dirs[d][4][rows[d], :] for d in range(2)]

        def rope(xv, d):
            swapped = jnp.where(first_half, pltpu.roll(xv, dk - 32, 1), pltpu.roll(xv, 32, 1))
            return xv * cos[d] + swapped * sin[d]

        qh = [rope(dirs[d][0][rows[d], h * dk:(h + 1) * dk].astype(F32), d) for d, h in units]
        kh = [rope(dirs[d][1][rows[d], h * dk:(h + 1) * dk].astype(F32) * (dk ** -0.5), d) for d, h in units]
        vh = [dirs[d][2][rows[d], h * dv:(h + 1) * dv] for d, h in units]
        scores = [_dot_nt(_bf(q), _bf(k)) * m for q, k, m in zip(qh, kh, dmask)]
        inner = [_bdot(_bf(s), v) for s, v in zip(scores, vh)]
        cross = [_bdot(_bf(q * x), _bf(r)) for q, x, r in zip(qh, xi, state)]
        kv = [_dot_tn(_bf(k * z), v) for k, z, v in zip(kh, zeta, vh)]
        for i, (d, h) in enumerate(units):
            dirs[d][5][rows[d], h * dv:(h + 1) * dv] = inner[i] + cross[i]
        state = [g * r + x for g, r, x in zip(g_chunk, state, kv)]
    for i, (d, h) in enumerate(units):
        r_s[d, h] = state[i]

    @pl.when(step == pl.num_programs(0) - 1)
    def _():
        rfin_ref[...] = r_s[...]


def _retention(proj, cos, sin, lgt, r0):
    t_len = proj.shape[0]
    n_ch = min(4, t_len // RET_CHUNK)
    c = n_ch * RET_CHUNK
    nc = t_len // c
    qw = B_QK_WIDTH
    vw = B_V_WIDTH
    fq = lambda col: pl.BlockSpec((c, qw), lambda i: (i, col))
    bq = lambda col: pl.BlockSpec((c, qw), lambda i: (nc - 1 - i, col))
    st = pl.BlockSpec((2, B_HEADS, B_QK_DIM, B_V_DIM), lambda i: (0, 0, 0, 0))
    q_col = 0
    v_col = 2 * qw // vw
    return pl.pallas_call(
        functools.partial(_retention_kernel, n_ch=n_ch),
        out_shape=[jax.ShapeDtypeStruct((t_len, vw), F32), jax.ShapeDtypeStruct((t_len, vw), F32),
                   jax.ShapeDtypeStruct((2, B_HEADS, B_QK_DIM, B_V_DIM), F32)],
        grid=(nc,),
        in_specs=[
            fq(q_col), fq(q_col + 1), pl.BlockSpec((c, vw), lambda i: (i, v_col)),
            pl.BlockSpec((c, B_QK_DIM), lambda i: (i, 0)), pl.BlockSpec((c, B_QK_DIM), lambda i: (i, 0)),
            bq(q_col), bq(q_col + 1), pl.BlockSpec((c, vw), lambda i: (nc - 1 - i, v_col)),
            pl.BlockSpec((c, B_QK_DIM), lambda i: (nc - 1 - i, 0)),
            pl.BlockSpec((c, B_QK_DIM), lambda i: (nc - 1 - i, 0)),
            pl.BlockSpec((2 * B_HEADS, 128), lambda i: (0, 0)), st,
        ],
        out_specs=[pl.BlockSpec((c, vw), lambda i: (i, 0)), pl.BlockSpec((c, vw), lambda i: (nc - 1 - i, 0)), st],
        scratch_shapes=[pltpu.VMEM((2, B_HEADS, B_QK_DIM, B_V_DIM), F32)],
        compiler_params=_params(("arbitrary",)),
        name="retention",
    )(proj, proj, proj, cos, sin, proj, proj, proj, cos, sin, lgt, r0)


def _mix_out_kernel(of_ref, ob_ref, bonus_ref, g_ref, yf_ref, yb_ref, gb_ref, ga_ref, gbb_ref, x_ref,
                    alnw, alnb, rlnw, rlnb, npm, npf, g1_ref, sc2_ref, sh2_ref, wa_ref, wb_ref, wo_ref, bd_ref,
                    x1_ref, h2_ref, h3_ref):
    bd = bd_ref[...]
    o = of_ref[...] + ob_ref[...]
    mu = _dot_split_lhs(o, bd) * (1.0 / A_HEAD_DIM)
    oc = o - mu
    var = _dot_split_lhs(oc * oc, bd) * (1.0 / A_HEAD_DIM)
    ya = oc * lax.rsqrt(var + RWKV_GN_EPS) * alnw[...] + alnb[...] + bonus_ref[...].astype(F32)
    ya = _bdot(_bf(ya * g_ref[...].astype(F32)), wa_ref[...])

    y = yf_ref[...] + yb_ref[...]
    parts = []
    for h in range(B_HEADS):
        seg = y[:, h * B_V_DIM:(h + 1) * B_V_DIM]
        m = jnp.mean(seg, axis=-1, keepdims=True)
        sc = seg - m
        vr = jnp.mean(sc * sc, axis=-1, keepdims=True)
        parts.append(sc * lax.rsqrt(vr + RET_GN_EPS))
    yn = jnp.concatenate(parts, axis=1) * rlnw[...] + rlnb[...]
    gb = gb_ref[...].astype(F32)
    yb = _bdot(_bf(yn * (gb * _sigmoid(gb))), wb_ref[...])

    merged = _sigmoid(ga_ref[...].astype(F32)) * ya + _sigmoid(gbb_ref[...].astype(F32)) * yb
    mix = _bdot(_bf(merged), wo_ref[...])
    x1 = x_ref[...] + g1_ref[...] * (_rms(mix) * npm[...])
    x1_ref[...] = x1
    h2 = (_rms(x1) * npf[...]) * (1.0 + sc2_ref[...]) + sh2_ref[...]
    h2_ref[...] = h2
    _rows_to_packed_tiles(h2, h3_ref)


def _mix_out(x2d, proj, o_f, o_b, bonus, g, y_f, y_b, lp, vecs, bd):
    t_len = x2d.shape[0]
    tm = min(t_len, 256)
    aw = A_WIDTH
    d = D_MODEL
    ta = lambda: pl.BlockSpec((tm, aw), lambda i: (i, 0))
    td = lambda: pl.BlockSpec((tm, d), lambda i: (i, 0))
    pc = lambda c: pl.BlockSpec((tm, d), lambda i: (i, c))
    va = lambda: pl.BlockSpec((1, aw), lambda i: (0, 0))
    vd = lambda: pl.BlockSpec((1, d), lambda i: (0, 0))
    full = lambda shp: pl.BlockSpec(shp, lambda i: (0, 0))
    gcol = (2 * B_QK_WIDTH + B_V_WIDTH) // d
    return pl.pallas_call(
        _mix_out_kernel,
        out_shape=[jax.ShapeDtypeStruct((t_len, d), F32), jax.ShapeDtypeStruct((t_len, d), F32),
                   jax.ShapeDtypeStruct((t_len * PACK_TILE, LANES), jnp.uint32)],
        grid=(t_len // tm,),
        in_specs=[ta(), ta(), ta(), ta(), td(), td(), pc(gcol), pc(gcol + 1), pc(gcol + 2), td(),
                  va(), va(), vd(), vd(), vd(), vd(), vd(), vd(), vd(),
                  full((aw, d)), full((d, d)), full((d, d)), full((aw, aw))],
        out_specs=[td(), td(), pl.BlockSpec((tm * PACK_TILE, LANES), lambda i: (i, 0))],
        compiler_params=_params(("arbitrary",)),
        name="mix_out",
    )(o_f, o_b, bonus, g, y_f, y_b, proj, proj, proj, x2d,
      lp["ln_w"], lp["ln_b"], lp["ret_ln_w"], lp["ret_ln_b"], vecs["npm"], vecs["npf"], vecs["g1"], vecs["sc2"],
      vecs["sh2"], lp["w_a"], lp["w_b"], lp["w_o"], bd)


def _router_kernel(h_ref, rw_ref, bias_ref, sel_ref, wts_ref, rank_ref, cnt_ref, cnt_s, *, tm):
    step = pl.program_id(0)

    @pl.when(step == 0)
    def _():
        cnt_s[...] = jnp.zeros_like(cnt_s)

    ne = N_EXPERTS
    scores = _sigmoid(_dot3(h_ref[...], rw_ref[...]))
    work = scores + bias_ref[...]
    lane = lax.broadcasted_iota(jnp.int32, (tm, ne), 1).astype(F32)
    idxs = []
    vals = []
    for _ in range(TOP_K):
        m = jnp.max(work, axis=-1, keepdims=True)
        idx = jnp.min(jnp.where(work == m, lane, float(ne)), axis=-1, keepdims=True)
        oh = lane == idx
        vals.append(jnp.sum(jnp.where(oh, scores, 0.0), axis=-1, keepdims=True))
        idxs.append(idx)
        work = jnp.where(oh, -jnp.inf, work)
    sel_f = jnp.concatenate(idxs, axis=1)
    s_sel = jnp.concatenate(vals, axis=1)
    wts_ref[...] = s_sel / jnp.sum(s_sel, axis=1, keepdims=True) * ROUTED_SCALE
    sel_ref[...] = sel_f.astype(jnp.int32)

    hit = work == -jnp.inf
    r_i = lax.broadcasted_iota(jnp.int32, (tm, tm), 0)
    c_i = lax.broadcasted_iota(jnp.int32, (tm, tm), 1)
    before = _bdot((r_i > c_i).astype(BF16), hit.astype(BF16)) + cnt_s[...]
    ranks = [jnp.sum(jnp.where(lane == idxs[k], before, 0.0), axis=-1, keepdims=True) for k in range(TOP_K)]
    rank_ref[...] = jnp.concatenate(ranks, axis=1).astype(jnp.int32)
    cnt = cnt_s[...] + jnp.sum(hit.astype(F32), axis=0, keepdims=True)
    cnt_s[...] = cnt
    cnt_ref[...] = cnt.astype(jnp.int32)


def _router(h2, router_w, router_bias):
    t_len = h2.shape[0]
    tm = min(t_len, 512)
    kern = functools.partial(_router_kernel, tm=tm)
    tk = lambda: pl.BlockSpec((tm, TOP_K), lambda i: (i, 0))
    return pl.pallas_call(
        kern,
        out_shape=[jax.ShapeDtypeStruct((t_len, TOP_K), jnp.int32), jax.ShapeDtypeStruct((t_len, TOP_K), F32),
                   jax.ShapeDtypeStruct((t_len, TOP_K), jnp.int32), jax.ShapeDtypeStruct((1, N_EXPERTS), jnp.int32)],
        grid=(t_len // tm,),
        in_specs=[pl.BlockSpec((tm, D_MODEL), lambda i: (i, 0)),
                  pl.BlockSpec((D_MODEL, N_EXPERTS), lambda i: (0, 0)),
                  pl.BlockSpec((1, N_EXPERTS), lambda i: (0, 0))],
        out_specs=[tk(), tk(), tk(), pl.BlockSpec((1, N_EXPERTS), lambda i: (0, 0))],
        scratch_shapes=[pltpu.VMEM((1, N_EXPERTS), F32)],
        compiler_params=_params(("arbitrary",)),
        name="router",
    )(h2, router_w, router_bias.reshape(1, N_EXPERTS))


def _slot(sel_ref, rank_ref, start_ref, a):
    return start_ref[sel_ref[a]] + rank_ref[a]


def _dispatch_kernel(sel_ref, rank_ref, start_ref, cnt_ref, h_ref, xs_out, zrow, sem, zsem, *, tm):
    row0 = pl.program_id(0) * tm

    def row_copy(t, k):
        slot = _slot(sel_ref, rank_ref, start_ref, t * TOP_K + k)
        return pltpu.make_async_copy(_row_tile(h_ref, row0 + t), _row_tile(xs_out, slot), sem)

    def issue(t, carry):
        for k in range(TOP_K):
            row_copy(t, k).start(priority=k % 2)
        return carry

    lax.fori_loop(0, tm, issue, 0)

    def drain(t, carry):
        for k in range(TOP_K):
            row_copy(t, k).wait()
        return carry

    lax.fori_loop(0, tm, drain, 0)

    @pl.when(pl.program_id(0) == pl.num_programs(0) - 1)
    def _():
        zrow[...] = jnp.zeros_like(zrow)

        def for_pad_runs(e0, fn):
            def per_expert(e, carry):
                n = cnt_ref[e]
                n_pad = (n + MOE_BLOCK - 1) // MOE_BLOCK * MOE_BLOCK
                pad = n_pad - n
                off = start_ref[e] + n
                for bit in reversed(range(MOE_BLOCK.bit_length() - 1)):
                    size = 1 << bit
                    is_set = ((pad >> bit) & 1) == 1

                    @pl.when(is_set)
                    def _(off=off, size=size):
                        dst = xs_out.at[pl.ds(pl.multiple_of(off * PACK_TILE, PACK_TILE), size * PACK_TILE)]
                        fn(pltpu.make_async_copy(zrow.at[pl.ds(0, size * PACK_TILE)], dst, zsem))

                    off = off + jnp.where(is_set, size, 0)
                return carry

            lax.fori_loop(e0, e0 + ZERO_PAD_GROUP, per_expert, 0)

        def per_group(gi, carry):
            e0 = gi * ZERO_PAD_GROUP
            for_pad_runs(e0, lambda cp: cp.start())
            for_pad_runs(e0, lambda cp: cp.wait())
            return carry

        lax.fori_loop(0, N_EXPERTS // ZERO_PAD_GROUP, per_group, 0)


def _dispatch(h3, sel_flat, rank_flat, pad_start, counts, n_slots):
    t_len = h3.shape[0] // PACK_TILE
    tm = min(t_len, MOE_TOKEN_TILE)
    kern = functools.partial(_dispatch_kernel, tm=tm)
    smem_blk = lambda: pl.BlockSpec((tm * TOP_K,), lambda i: (i,), memory_space=pltpu.SMEM)
    smem_all = lambda: pl.BlockSpec((N_EXPERTS,), lambda i: (0,), memory_space=pltpu.SMEM)
    return pl.pallas_call(
        kern,
        out_shape=jax.ShapeDtypeStruct((n_slots * PACK_TILE, LANES), jnp.uint32),
        grid=(t_len // tm,),
        in_specs=[smem_blk(), smem_blk(), smem_all(), smem_all(),
                  pl.BlockSpec(memory_space=pl.ANY)],
        out_specs=pl.BlockSpec(memory_space=pl.ANY),
        scratch_shapes=[pltpu.VMEM((MOE_BLOCK // 2 * PACK_TILE, LANES), jnp.uint32), pltpu.SemaphoreType.DMA(()),
                        pltpu.SemaphoreType.DMA(())],
        compiler_params=_params(("arbitrary",)),
        name="dispatch",
    )(sel_flat, rank_flat, pad_start, counts, h3)


def _expert_kernel(be_ref, nu_ref, xs_ref, wg_ref, wu_ref, wd_ref, ys_ref, wg_s, wu_s, wd_s):
    b = pl.program_id(0)
    used = b < nu_ref[0]
    new_expert = (b == 0) | (be_ref[b] != be_ref[jnp.maximum(b - 1, 0)])

    @pl.when(used & new_expert)
    def _():
        wg_s[...] = _bf(wg_ref[0])
        wu_s[...] = _bf(wu_ref[0])
        wd_s[...] = _bf(wd_ref[0])

    @pl.when(used)
    def _():
        xb = _packed_tiles_to_rows(xs_ref, MOE_BLOCK)
        gate = _bdot(xb, wg_s[...])
        up = _bdot(xb, wu_s[...])
        act = gate * _sigmoid(gate) * up
        _rows_to_packed_tiles(_bdot(_bf(act), wd_s[...]), ys_ref)

    @pl.when(jnp.logical_not(used))
    def _():
        ys_ref[...] = jnp.zeros_like(ys_ref)


def _experts(xs, block_e, n_used, w_gate, w_up, w_down):
    n_slots = xs.shape[0] // PACK_TILE
    n_blocks = n_slots // MOE_BLOCK
    rows = (MOE_BLOCK * PACK_TILE, LANES)
    grid_spec = pltpu.PrefetchScalarGridSpec(
        num_scalar_prefetch=2,
        grid=(n_blocks,),
        in_specs=[
            pl.BlockSpec((MOE_BLOCK * PACK_TILE, LANES), lambda b, be, nu: (jnp.where(b < nu[0], b, 0), 0)),
            pl.BlockSpec((1, D_MODEL, EXPERT_FF), lambda b, be, nu: (be[b], 0, 0)),
            pl.BlockSpec((1, D_MODEL, EXPERT_FF), lambda b, be, nu: (be[b], 0, 0)),
            pl.BlockSpec((1, EXPERT_FF, D_MODEL), lambda b, be, nu: (be[b], 0, 0)),
        ],
        out_specs=pl.BlockSpec(rows, lambda b, be, nu: (b, 0)),
        scratch_shapes=[pltpu.VMEM((D_MODEL, EXPERT_FF), BF16), pltpu.VMEM((D_MODEL, EXPERT_FF), BF16),
                        pltpu.VMEM((EXPERT_FF, D_MODEL), BF16)],
    )
    return pl.pallas_call(
        _expert_kernel,
        out_shape=jax.ShapeDtypeStruct((n_slots * PACK_TILE, LANES), jnp.uint32),
        grid_spec=grid_spec,
        compiler_params=_params(("arbitrary",)),
        name="experts",
    )(block_e, n_used, xs, w_gate, w_up, w_down)


def _combine_kernel(sel_ref, rank_ref, start_ref, ys_ref, wts_ref, h_ref, x1_ref, sg_ref, su_ref, sd_ref, npo, g2_ref,
                    o_ref, buf, sem, *, tm):
    def row_copy(t, k):
        slot = _slot(sel_ref, rank_ref, start_ref, t * TOP_K + k)
        return pltpu.make_async_copy(_row_tile(ys_ref, slot), _row_tile(buf, t, (k,)), sem)

    def issue(t, carry):
        for k in range(TOP_K):
            row_copy(t, k).start(priority=k % 2)
        return carry

    lax.fori_loop(0, tm, issue, 0)

    hb = _bf(h_ref[...])
    gate = _bdot(hb, sg_ref[...])
    up = _bdot(hb, su_ref[...])
    shared = _bdot(_bf(gate * _sigmoid(gate) * up), sd_ref[...])

    def drain(t, carry):
        for k in range(TOP_K):
            row_copy(t, k).wait()
        return carry

    lax.fori_loop(0, tm, drain, 0)

    wts = wts_ref[...]
    routed = _packed_tiles_to_rows(buf, tm, (0,), F32) * wts[:, 0:1]
    for k in range(1, TOP_K):
        routed = routed + _packed_tiles_to_rows(buf, tm, (k,), F32) * wts[:, k:k + 1]
    o_ref[...] = x1_ref[...] + g2_ref[...] * (_rms(routed + shared) * npo[...])


def _combine(ys, sel_flat, rank_flat, pad_start, wts, h2, x1, sg, su, sd, npo, g2):
    t_len = h2.shape[0]
    tm = min(t_len, MOE_TOKEN_TILE)
    d = D_MODEL
    kern = functools.partial(_combine_kernel, tm=tm)
    smem_blk = lambda: pl.BlockSpec((tm * TOP_K,), lambda i: (i,), memory_space=pltpu.SMEM)
    td = lambda: pl.BlockSpec((tm, d), lambda i: (i, 0))
    vd = lambda: pl.BlockSpec((1, d), lambda i: (0, 0))
    return pl.pallas_call(
        kern,
        out_shape=jax.ShapeDtypeStruct((t_len, d), F32),
        grid=(t_len // tm,),
        in_specs=[smem_blk(), smem_blk(),
                  pl.BlockSpec((N_EXPERTS,), lambda i: (0,), memory_space=pltpu.SMEM),
                  pl.BlockSpec(memory_space=pl.ANY),
                  pl.BlockSpec((tm, TOP_K), lambda i: (i, 0)),
                  td(), td(),
                  pl.BlockSpec((d, SHARED_FF), lambda i: (0, 0)), pl.BlockSpec((d, SHARED_FF), lambda i: (0, 0)),
                  pl.BlockSpec((SHARED_FF, d), lambda i: (0, 0)), vd(), vd()],
        out_specs=td(),
        scratch_shapes=[pltpu.VMEM((TOP_K, tm * PACK_TILE, LANES), jnp.uint32), pltpu.SemaphoreType.DMA(())],
        compiler_params=_params(("arbitrary",)),
        name="combine",
    )(sel_flat, rank_flat, pad_start, ys, wts, h2, x1, sg, su, sd, npo, g2)


def _block_diag_ones(width, group):
    idx = np.arange(width) // group
    return jnp.asarray(idx[:, None] == idx[None, :], dtype=BF16)


def _rope_tables(t_len):
    n_rows = t_len // GRID_W
    quarter = B_QK_DIM // 4
    inv_freq = jnp.asarray(ROPE_BASE, F32) ** (-jnp.arange(quarter, dtype=F32) / quarter)
    ang_r = jnp.arange(n_rows, dtype=F32)[:, None] * inv_freq
    ang_c = jnp.arange(GRID_W, dtype=F32)[:, None] * inv_freq
    grid = (n_rows, GRID_W, quarter)
    by_row = lambda a: jnp.broadcast_to(a[:, None, :], grid)
    by_col = lambda a: jnp.broadcast_to(a[None, :, :], grid)
    cr, sr, cc, sc = by_row(jnp.cos(ang_r)), by_row(jnp.sin(ang_r)), by_col(jnp.cos(ang_c)), by_col(jnp.sin(ang_c))
    cos = jnp.concatenate([cr, cr, cc, cc], axis=-1).reshape(t_len, B_QK_DIM)
    sin = jnp.concatenate([-sr, sr, -sc, sc], axis=-1).reshape(t_len, B_QK_DIM)
    return cos, sin


def _token_mixer(x2d, g_pre, sc, sh, lp, cos, sin, states, bd):
    proj_a, proj_b = _in_proj(x2d, g_pre, sc, sh, lp["w_in"], lp["mu"])
    prep = _rwkv_prep(proj_a, lp, bd)
    o_f, o_b, s_fin = _rwkv_scan(proj_a, prep, states[0])
    y_f, y_b, r_fin = _retention(proj_b, cos, sin, lp["lgt"], states[1])
    return proj_b, prep, (o_f, o_b), (y_f, y_b), (s_fin, r_fin)


def kernel(x, c, ctx, c_ctx, w_mod, b_mod, norm_pre_mix, norm_post_mix, norm_pre_ffn, norm_post_ffn, w_in, shift_mu,
           rwkv_w0, rwkv_w_up, rwkv_a0, rwkv_a_up, rwkv_g_up, rwkv_k_k, rwkv_k_a, rwkv_r_k, rwkv_ln_w, rwkv_ln_b,
           w_branch_a, ret_decay_logit, ret_ln_w, ret_ln_b, w_branch_b, w_out, router_w, router_bias, exp_w_gate,
           exp_w_up, exp_w_down, sh_w_gate, sh_w_up, sh_w_down):
    d = D_MODEL
    assert x.shape[0] == 1 and w_in.shape[0] == 1, "single batch element, single layer"
    t_len = x.shape[1]
    x2d = x.reshape(t_len, d)
    ctx2d = ctx.reshape(ctx.shape[1], d)
    row = lambda a: a.reshape(1, -1)

    cs = jnp.zeros((8, d), F32).at[0].set(c[0]).at[1].set(c_ctx)
    mod = _modulation(cs, w_mod[0], b_mod[0])
    sh1, sc1, g1, sh2, sc2, g2 = [mod[0:1, i * d:(i + 1) * d] for i in range(6)]
    csh1, csc1 = mod[1:2, 0:d], mod[1:2, d:2 * d]

    lp = {
        "w_in": _w_in_layout(w_in[0]),
        "mu": jnp.pad(shift_mu[0], ((0, 0), (0, SHIFT_PAD - SHIFT_COLS))),
        "k_k": row(rwkv_k_k[0]), "k_a": row(rwkv_k_a[0]), "r_k": row(rwkv_r_k[0]),
        "w0": rwkv_w0[0], "w_up": rwkv_w_up[0], "a0": rwkv_a0[0], "a_up": rwkv_a_up[0], "g_up": rwkv_g_up[0],
        "ln_w": row(rwkv_ln_w[0]), "ln_b": row(rwkv_ln_b[0]),
        "ret_ln_w": row(ret_ln_w[0]), "ret_ln_b": row(ret_ln_b[0]),
        "lgt": jnp.broadcast_to(ret_decay_logit[0].reshape(2 * B_HEADS, 1), (2 * B_HEADS, 128)),
        "w_a": w_branch_a[0].astype(BF16), "w_b": w_branch_b[0].astype(BF16), "w_o": w_out[0].astype(BF16),
    }
    bd = _block_diag_ones(A_WIDTH, A_HEAD_DIM)
    g_pre = row(norm_pre_mix[0])

    t_ctx = ctx2d.shape[0]
    zero_states = (jnp.zeros(RWKV_STATE_SHAPE, F32),
                   jnp.zeros((2, B_HEADS, B_QK_DIM, B_V_DIM), F32))
    ones = jnp.ones((t_ctx, B_QK_DIM), F32)
    *_, ctx_states = _token_mixer(ctx2d, g_pre, csc1, csh1, lp, ones, jnp.zeros_like(ones), zero_states, bd)

    cos, sin = _rope_tables(t_len)
    proj, prep, (o_f, o_b), (y_f, y_b), _ = _token_mixer(x2d, g_pre, sc1, sh1, lp, cos, sin, ctx_states, bd)
    vecs = {"npm": row(norm_post_mix[0]), "npf": row(norm_pre_ffn[0]), "g1": g1, "sc2": sc2, "sh2": sh2}
    x1, h2, h3 = _mix_out(x2d, proj, o_f, o_b, prep[7], prep[8], y_f, y_b, lp, vecs, bd)

    sel, wts, rank, counts = _router(h2, router_w[0], router_bias[0])
    counts = counts.reshape(N_EXPERTS)
    padded = (counts + MOE_BLOCK - 1) // MOE_BLOCK * MOE_BLOCK
    pad_end = jnp.cumsum(padded)
    pad_start = (pad_end - padded).astype(jnp.int32)
    n_assign = t_len * TOP_K
    n_blocks = (n_assign + N_EXPERTS * (MOE_BLOCK - 1) + MOE_BLOCK - 1) // MOE_BLOCK
    block_start = jnp.arange(n_blocks, dtype=jnp.int32) * MOE_BLOCK
    block_e = jnp.minimum(jnp.sum(pad_end[None, :] <= block_start[:, None], axis=1), N_EXPERTS - 1).astype(jnp.int32)
    n_used = (pad_end[-1:] // MOE_BLOCK).astype(jnp.int32)
    sel_flat = sel.reshape(n_assign)
    rank_flat = rank.reshape(n_assign)

    xs = _dispatch(h3, sel_flat, rank_flat, pad_start, counts, n_blocks * MOE_BLOCK)
    ys = _experts(xs, block_e, n_used, exp_w_gate[0], exp_w_up[0], exp_w_down[0])
    out = _combine(ys, sel_flat, rank_flat, pad_start, wts, h2, x1, sh_w_gate[0].astype(BF16),
                   sh_w_up[0].astype(BF16), sh_w_down[0].astype(BF16), row(norm_post_ffn[0]), g2)
    return out.reshape(x.shape)
```
